```python
import math
import jax, jax.numpy as jnp
from jax import lax
import numpy as np

D_MODEL = 1024
BATCH = 8
SEQ = 16384
DEPTH = 1

HEAD_DIM = 64
ATT_GROUPS = ((128, 1), (512, 4), (2048, 16))
HEADS_PER_GROUP = 4
N_ATT_HEADS = HEADS_PER_GROUP * len(ATT_GROUPS)
ATT_WIDTH = N_ATT_HEADS * HEAD_DIM
ATT_MERGED = HEADS_PER_GROUP * HEAD_DIM
BLK = 128
POOL_WINDOWS = (2, 4, 8, 16)
POOL_GROUP_WIDTH = 3 * D_MODEL // 16
POOL_WIDTH = POOL_GROUP_WIDTH * len(POOL_WINDOWS)
D_FF = 4 * D_MODEL
N_IN = 3 * ATT_WIDTH + POOL_WIDTH + 2 * D_MODEL
NORM_EPS = 1e-6
ALIBI_MAX_BIAS = 8.0

kernel_name = "hybrid_dilated_attn_pool_gated_block"


def _rmsnorm(x, g):
    xf = x.astype(jnp.float32)
    y = xf * lax.rsqrt(jnp.mean(xf * xf, axis=-1, keepdims=True) + NORM_EPS)
    return (y * g.astype(jnp.float32)).astype(x.dtype)


def _dilated_window_attention(q, k, v, dilation, n_steps, slopes):
    B, S, H, Dh = q.shape
    L = S // dilation
    nb = -(-L // BLK)
    Lp = nb * BLK
    Z = B * dilation

    def to_sub(t):
        t = t.reshape(B, L, dilation, H, Dh).transpose(0, 2, 1, 3, 4).reshape(Z, L, H, Dh)
        return jnp.pad(t, ((0, 0), (0, Lp - L), (0, 0), (0, 0)))

    def band(t):
        prev = jnp.pad(t, ((0, 0), (BLK, 0), (0, 0), (0, 0)))[:, :Lp]
        return jnp.concatenate([prev.reshape(Z, nb, BLK, H, Dh),
                                t.reshape(Z, nb, BLK, H, Dh)], axis=2)

    qb = to_sub(q).reshape(Z, nb, BLK, H, Dh).astype(jnp.float32)
    kb = band(to_sub(k)).astype(jnp.float32)
    vb = band(to_sub(v)).astype(jnp.float32)

    s = jnp.einsum('znqhd,znkhd->znhqk', qb, kb) * (Dh ** -0.5)
    steps = BLK + jnp.arange(BLK)[:, None] - jnp.arange(2 * BLK)[None, :]
    key_idx = (jnp.arange(nb)[:, None, None] * BLK
               + jnp.arange(2 * BLK)[None, None, :] - BLK)
    valid = (steps >= 0) & (steps <= n_steps) & (key_idx >= 0)
    bias = -(slopes[:, None, None] * (steps * dilation).astype(jnp.float32))
    s = jnp.where(valid[None, :, None], s + bias[None, None], -jnp.inf)
    lse = jax.nn.logsumexp(s, axis=-1)
    p = jnp.exp(s - lse[..., None])
    o = jnp.einsum('znhqk,znkhd->znqhd', p, vb).reshape(Z, Lp, H, Dh)[:, :L]
    lse = lse.transpose(0, 1, 3, 2).reshape(Z, Lp, H)[:, :L]
    o = o.reshape(B, dilation, L, H, Dh).transpose(0, 2, 1, 3, 4).reshape(B, S, H, Dh)
    lse = lse.reshape(B, dilation, L, H).transpose(0, 2, 1, 3).reshape(B, S, H)
    return o, lse


def _attention_branch(q, k, v):
    B, S, _ = q.shape
    q = q.reshape(B, S, N_ATT_HEADS, HEAD_DIM)
    k = k.reshape(B, S, N_ATT_HEADS, HEAD_DIM)
    v = v.reshape(B, S, N_ATT_HEADS, HEAD_DIM)
    slopes = 2.0 ** (-ALIBI_MAX_BIAS * (jnp.arange(N_ATT_HEADS, dtype=jnp.float32) + 1.0)
                     / N_ATT_HEADS)
    outs, lses = [], []
    for g, (window, dilation) in enumerate(ATT_GROUPS):
        hs = slice(g * HEADS_PER_GROUP, (g + 1) * HEADS_PER_GROUP)
        o, l = _dilated_window_attention(q[:, :, hs], k[:, :, hs], v[:, :, hs],
                                         dilation, window // dilation, slopes[hs])
        outs.append(o)
        lses.append(l)
    outs = jnp.stack(outs, axis=0)
    wts = jax.nn.softmax(jnp.stack(lses, axis=0), axis=0)
    merged = jnp.sum(wts[..., None] * outs, axis=0)
    return merged.reshape(B, S, ATT_MERGED)


def _pool_branch(pz, w_grp, scale):
    B, S, _ = pz.shape
    pf = pz.astype(jnp.float32).reshape(B, S, len(POOL_WINDOWS), POOL_GROUP_WIDTH)
    c0 = jnp.pad(jnp.cumsum(pf, axis=1), ((0, 0), (1, 0), (0, 0), (0, 0)))
    t = jnp.arange(S)
    pooled = []
    for g, w in enumerate(POOL_WINDOWS):
        lower = jnp.take(c0[:, :, g], jnp.maximum(t + 1 - w, 0), axis=1)
        count = jnp.minimum(t + 1, w).astype(jnp.float32)[None, :, None]
        pooled.append((c0[:, 1:, g] - lower) / count)
    pooled = jnp.stack(pooled, axis=2) - pf
    mixed = jnp.einsum('bsgc,gcd->bsgd', pooled, w_grp.astype(jnp.float32))
    return mixed.reshape(B, S, POOL_WIDTH) * scale.astype(jnp.float32)


def _fwd_setup_inputs(seed: int = 0) -> dict:
    key = jax.random.key(seed)
    ks = jax.random.split(key, 12)
    f32 = jnp.float32

    def nrm(k, shape, fan_in):
        return jax.random.normal(k, shape, f32) * (fan_in ** -0.5)

    def gain(k, shape):
        return 1.0 + 0.02 * jax.random.normal(k, shape, f32)

    return {
        "x": jax.random.normal(ks[0], (BATCH, SEQ, D_MODEL), f32),
        "norm_mix_g": gain(ks[1], (DEPTH, D_MODEL)),
        "w_in": nrm(ks[2], (DEPTH, D_MODEL, N_IN), D_MODEL),
        "w_att_out": nrm(ks[3], (DEPTH, ATT_MERGED, D_MODEL), ATT_MERGED),
        "w_pool_grp": nrm(ks[4], (DEPTH, len(POOL_WINDOWS), POOL_GROUP_WIDTH, POOL_GROUP_WIDTH),
                          POOL_GROUP_WIDTH),
        "pool_scale": 1.0 + 0.1 * jax.random.normal(ks[5], (DEPTH, POOL_WIDTH), f32),
        "w_pool_out": nrm(ks[6], (DEPTH, POOL_WIDTH, D_MODEL), POOL_WIDTH),
        "w_out": nrm(ks[7], (DEPTH, D_MODEL, D_MODEL), D_MODEL),
        "norm_mlp_g": gain(ks[8], (DEPTH, D_MODEL)),
        "w_mlp_in": nrm(ks[9], (DEPTH, D_MODEL, D_FF), D_MODEL),
        "w_mlp_out": nrm(ks[10], (DEPTH, D_FF, D_MODEL), D_FF),
        "norm_final_g": gain(ks[11], (D_MODEL,)),
    }


def _fwd_reference(x, norm_mix_g, w_in, w_att_out, w_pool_grp, pool_scale, w_pool_out, w_out,
              norm_mlp_g, w_mlp_in, w_mlp_out, norm_final_g):
    dt = x.dtype
    offs = np.cumsum([ATT_WIDTH, ATT_WIDTH, ATT_WIDTH, POOL_WIDTH, D_MODEL]).tolist()
    h = x
    for l in range(DEPTH):
        u = _rmsnorm(h, norm_mix_g[l])
        z = jnp.einsum('bsd,dn->bsn', u, w_in[l])
        q, k, v, pz, ga, gp = jnp.split(z, offs, axis=-1)
        a = _attention_branch(q, k, v).astype(dt)
        p = _pool_branch(pz, w_pool_grp[l], pool_scale[l]).astype(dt)
        merged = (jax.nn.sigmoid(ga) * jnp.einsum('bsc,cd->bsd', a, w_att_out[l])
                  + jax.nn.sigmoid(gp) * jnp.einsum('bsc,cd->bsd', p, w_pool_out[l]))
        h = h + jnp.einsum('bsd,de->bse', merged, w_out[l])
        m = _rmsnorm(h, norm_mlp_g[l])
        hid = jnp.square(jax.nn.relu(jnp.einsum('bsd,df->bsf', m, w_mlp_in[l])))
        h = h + jnp.einsum('bsf,fd->bsd', hid, w_mlp_out[l])
    return _rmsnorm(h, norm_final_g)


import jax as _jax
import jax.numpy as _jnp

TWIN_FORMAT = 'train_step'
FWD_PARAMS = ['x', 'norm_mix_g', 'w_in', 'w_att_out', 'w_pool_grp', 'pool_scale', 'w_pool_out', 'w_out', 'norm_mlp_g', 'w_mlp_in', 'w_mlp_out', 'norm_final_g']
TWIN_WEIGHTS = ['norm_mix_g', 'w_in', 'w_att_out', 'w_pool_grp', 'pool_scale', 'w_pool_out', 'w_out', 'norm_mlp_g', 'w_mlp_in', 'w_mlp_out', 'norm_final_g']
TWIN_DIFF_INPUT = 'x'
TWIN_INPUTS = ['x', 'norm_mix_g', 'w_in', 'w_att_out', 'w_pool_grp', 'pool_scale', 'w_pool_out', 'w_out', 'norm_mlp_g', 'w_mlp_in', 'w_mlp_out', 'norm_final_g', 'loss_target', 'm_norm_mix_g', 'm_w_in', 'm_w_att_out', 'm_w_pool_grp', 'm_pool_scale', 'm_w_pool_out', 'm_w_out', 'm_norm_mlp_g', 'm_w_mlp_in', 'm_w_mlp_out', 'm_norm_final_g', 'v_norm_mix_g', 'v_w_in', 'v_w_att_out', 'v_w_pool_grp', 'v_pool_scale', 'v_w_pool_out', 'v_w_out', 'v_norm_mlp_g', 'v_w_mlp_in', 'v_w_mlp_out', 'v_norm_final_g']
TWIN_OUTPUTS = ['loss', 'grad_x', 'grad_norm_mix_g', 'grad_w_in', 'grad_w_att_out', 'grad_w_pool_grp', 'grad_pool_scale', 'grad_w_pool_out', 'grad_w_out', 'grad_norm_mlp_g', 'grad_w_mlp_in', 'grad_w_mlp_out', 'grad_norm_final_g', 'delta_norm_mix_g', 'delta_w_in', 'delta_w_att_out', 'delta_w_pool_grp', 'delta_pool_scale', 'delta_w_pool_out', 'delta_w_out', 'delta_norm_mlp_g', 'delta_w_mlp_in', 'delta_w_mlp_out', 'delta_norm_final_g', 'new_m_norm_mix_g', 'new_m_w_in', 'new_m_w_att_out', 'new_m_w_pool_grp', 'new_m_pool_scale', 'new_m_w_pool_out', 'new_m_w_out', 'new_m_norm_mlp_g', 'new_m_w_mlp_in', 'new_m_w_mlp_out', 'new_m_norm_final_g', 'new_v_norm_mix_g', 'new_v_w_in', 'new_v_w_att_out', 'new_v_w_pool_grp', 'new_v_pool_scale', 'new_v_w_pool_out', 'new_v_w_out', 'new_v_norm_mlp_g', 'new_v_w_mlp_in', 'new_v_w_mlp_out', 'new_v_norm_final_g']
TWIN_LEAF_KINDS = {'loss': 'loss', 'grad_x': 'grad_x', 'grad_norm_mix_g': 'grad_w', 'grad_w_in': 'grad_w', 'grad_w_att_out': 'grad_w', 'grad_w_pool_grp': 'grad_w', 'grad_pool_scale': 'grad_w', 'grad_w_pool_out': 'grad_w', 'grad_w_out': 'grad_w', 'grad_norm_mlp_g': 'grad_w', 'grad_w_mlp_in': 'grad_w', 'grad_w_mlp_out': 'grad_w', 'grad_norm_final_g': 'grad_w', 'delta_norm_mix_g': 'delta_w', 'delta_w_in': 'delta_w', 'delta_w_att_out': 'delta_w', 'delta_w_pool_grp': 'delta_w', 'delta_pool_scale': 'delta_w', 'delta_w_pool_out': 'delta_w', 'delta_w_out': 'delta_w', 'delta_norm_mlp_g': 'delta_w', 'delta_w_mlp_in': 'delta_w', 'delta_w_mlp_out': 'delta_w', 'delta_norm_final_g': 'delta_w', 'new_m_norm_mix_g': 'new_m', 'new_m_w_in': 'new_m', 'new_m_w_att_out': 'new_m', 'new_m_w_pool_grp': 'new_m', 'new_m_pool_scale': 'new_m', 'new_m_w_pool_out': 'new_m', 'new_m_w_out': 'new_m', 'new_m_norm_mlp_g': 'new_m', 'new_m_w_mlp_in': 'new_m', 'new_m_w_mlp_out': 'new_m', 'new_m_norm_final_g': 'new_m', 'new_v_norm_mix_g': 'new_v', 'new_v_w_in': 'new_v', 'new_v_w_att_out': 'new_v', 'new_v_w_pool_grp': 'new_v', 'new_v_pool_scale': 'new_v', 'new_v_w_pool_out': 'new_v', 'new_v_w_out': 'new_v', 'new_v_norm_mlp_g': 'new_v', 'new_v_w_mlp_in': 'new_v', 'new_v_w_mlp_out': 'new_v', 'new_v_norm_final_g': 'new_v'}


def _forward(args):
    return _fwd_reference(*[args[k] for k in FWD_PARAMS])


def _output_shape():
    def fwd():
        inp = _fwd_setup_inputs(0)
        return _fwd_reference(*[inp[k] for k in FWD_PARAMS])
    out = _jax.eval_shape(fwd)
    return out.shape, out.dtype

N_MICROBATCH = 1
ADAM_LR = 0.001
ADAM_B1 = 0.9
ADAM_B2 = 0.999
ADAM_EPS = 1e-08
ADAM_WD = 0.01
ADAM_STEP = 10
PER_EXAMPLE_BATCH_AXIS = {'x': 0, 'loss_target': 0}
SHARED_INPUTS = []
_WEIGHT_DTYPES = {'norm_mix_g': _jnp.float32, 'w_in': _jnp.float32, 'w_att_out': _jnp.float32, 'w_pool_grp': _jnp.float32, 'pool_scale': _jnp.float32, 'w_pool_out': _jnp.float32, 'w_out': _jnp.float32, 'norm_mlp_g': _jnp.float32, 'w_mlp_in': _jnp.float32, 'w_mlp_out': _jnp.float32, 'norm_final_g': _jnp.float32}
MOMENT_SCALE = {'norm_mix_g': 2.034695e-01, 'w_in': 9.003699e-02, 'w_att_out': 6.155901e-02, 'w_pool_grp': 1.959830e-01, 'pool_scale': 1.825879e-01, 'w_pool_out': 1.693228e-01, 'w_out': 1.745585e-01, 'norm_mlp_g': 3.028960e-01, 'w_mlp_in': 1.521899e-01, 'w_mlp_out': 3.411039e-01, 'norm_final_g': 1.291053e+02}


def _to_microbatches(a, axis):
    t = _jnp.moveaxis(a, axis, 0)
    t = t.reshape((N_MICROBATCH, t.shape[0] // N_MICROBATCH) + t.shape[1:])
    return _jnp.moveaxis(t, 1, axis + 1)


def setup_inputs(seed: int = 0) -> dict:
    inp = _fwd_setup_inputs(seed)
    key = _jax.random.fold_in(_jax.random.key(seed), 7919)
    shape, _ = _output_shape()
    out = dict(inp)
    out["loss_target"] = _jax.random.normal(_jax.random.fold_in(key, 0), shape, _jnp.float32)
    for i, name in enumerate(TWIN_WEIGHTS):
        w = inp[name].astype(_jnp.float32)
        if MOMENT_SCALE is None:
            s = _jnp.sqrt(_jnp.mean(_jnp.square(w)) + 1e-30)
        else:
            s = MOMENT_SCALE[name]
        km, kv = _jax.random.split(_jax.random.fold_in(key, i + 1))
        out[name] = w
        out["m_" + name] = s * _jax.random.normal(km, w.shape, _jnp.float32)
        out["v_" + name] = (s * s) * _jax.random.uniform(kv, w.shape, _jnp.float32, 0.5, 1.5)
    if N_MICROBATCH > 1:
        for name, axis in PER_EXAMPLE_BATCH_AXIS.items():
            out[name] = _to_microbatches(out[name], axis)
    return {'x': out['x'], 'norm_mix_g': out['norm_mix_g'], 'w_in': out['w_in'], 'w_att_out': out['w_att_out'], 'w_pool_grp': out['w_pool_grp'], 'pool_scale': out['pool_scale'], 'w_pool_out': out['w_pool_out'], 'w_out': out['w_out'], 'norm_mlp_g': out['norm_mlp_g'], 'w_mlp_in': out['w_mlp_in'], 'w_mlp_out': out['w_mlp_out'], 'norm_final_g': out['norm_final_g'], 'loss_target': out['loss_target'], 'm_norm_mix_g': out['m_norm_mix_g'], 'm_w_in': out['m_w_in'], 'm_w_att_out': out['m_w_att_out'], 'm_w_pool_grp': out['m_w_pool_grp'], 'm_pool_scale': out['m_pool_scale'], 'm_w_pool_out': out['m_w_pool_out'], 'm_w_out': out['m_w_out'], 'm_norm_mlp_g': out['m_norm_mlp_g'], 'm_w_mlp_in': out['m_w_mlp_in'], 'm_w_mlp_out': out['m_w_mlp_out'], 'm_norm_final_g': out['m_norm_final_g'], 'v_norm_mix_g': out['v_norm_mix_g'], 'v_w_in': out['v_w_in'], 'v_w_att_out': out['v_w_att_out'], 'v_w_pool_grp': out['v_w_pool_grp'], 'v_pool_scale': out['v_pool_scale'], 'v_w_pool_out': out['v_w_pool_out'], 'v_w_out': out['v_w_out'], 'v_norm_mlp_g': out['v_norm_mlp_g'], 'v_w_mlp_in': out['v_w_mlp_in'], 'v_w_mlp_out': out['v_w_mlp_out'], 'v_norm_final_g': out['v_norm_final_g']}


def _loss(weights, diff, rest, loss_target):
    with _jax.named_scope("forward"):
        args = {**rest, TWIN_DIFF_INPUT: diff, **{k: w.astype(_WEIGHT_DTYPES[k]) for k, w in weights.items()}}
        y = _forward(args)
    with _jax.named_scope("loss_head"):
        err = _jnp.square(y.astype(_jnp.float32) - loss_target)
        return 0.5 * _jnp.sum(_jnp.mean(err, axis=-1)) if err.ndim else 0.5 * err


def _adamw(w, g, m, v):
    m = ADAM_B1 * m + (1.0 - ADAM_B1) * g
    v = ADAM_B2 * v + (1.0 - ADAM_B2) * _jnp.square(g)
    m_hat = m / (1.0 - ADAM_B1 ** ADAM_STEP)
    v_hat = v / (1.0 - ADAM_B2 ** ADAM_STEP)
    delta = -ADAM_LR * (m_hat / (_jnp.sqrt(v_hat) + ADAM_EPS) + ADAM_WD * w)
    return delta, m, v


def reference(x, norm_mix_g, w_in, w_att_out, w_pool_grp, pool_scale, w_pool_out, w_out, norm_mlp_g, w_mlp_in, w_mlp_out, norm_final_g, loss_target, m_norm_mix_g, m_w_in, m_w_att_out, m_w_pool_grp, m_pool_scale, m_w_pool_out, m_w_out, m_norm_mlp_g, m_w_mlp_in, m_w_mlp_out, m_norm_final_g, v_norm_mix_g, v_w_in, v_w_att_out, v_w_pool_grp, v_pool_scale, v_w_pool_out, v_w_out, v_norm_mlp_g, v_w_mlp_in, v_w_mlp_out, v_norm_final_g):
    given = dict(x=x, norm_mix_g=norm_mix_g, w_in=w_in, w_att_out=w_att_out, w_pool_grp=w_pool_grp, pool_scale=pool_scale, w_pool_out=w_pool_out, w_out=w_out, norm_mlp_g=norm_mlp_g, w_mlp_in=w_mlp_in, w_mlp_out=w_mlp_out, norm_final_g=norm_final_g, loss_target=loss_target, m_norm_mix_g=m_norm_mix_g, m_w_in=m_w_in, m_w_att_out=m_w_att_out, m_w_pool_grp=m_w_pool_grp, m_pool_scale=m_pool_scale, m_w_pool_out=m_w_pool_out, m_w_out=m_w_out, m_norm_mlp_g=m_norm_mlp_g, m_w_mlp_in=m_w_mlp_in, m_w_mlp_out=m_w_mlp_out, m_norm_final_g=m_norm_final_g, v_norm_mix_g=v_norm_mix_g, v_w_in=v_w_in, v_w_att_out=v_w_att_out, v_w_pool_grp=v_w_pool_grp, v_pool_scale=v_pool_scale, v_w_pool_out=v_w_pool_out, v_w_out=v_w_out, v_norm_mlp_g=v_norm_mlp_g, v_w_mlp_in=v_w_mlp_in, v_w_mlp_out=v_w_mlp_out, v_norm_final_g=v_norm_final_g)
    weights = {n: given[n] for n in TWIN_WEIGHTS}
    shared = {n: given[n] for n in SHARED_INPUTS}
    per_example = {n: given[n] for n in ['x']}
    grad_fn = _jax.value_and_grad(_loss, argnums=(0, 1))

    def one_microbatch(ex, loss_target):
        ex = dict(ex)
        diff = ex.pop(TWIN_DIFF_INPUT)
        return grad_fn(weights, diff, {**shared, **ex}, loss_target)

    if N_MICROBATCH == 1:
        loss, (grad_w, grad_x) = one_microbatch(per_example, given["loss_target"])
    else:
        def body(carry, xs):
            loss_sum, grad_sum = carry
            l_k, (gw_k, gx_k) = one_microbatch(xs[0], xs[1])
            with _jax.named_scope("update"):
                return (loss_sum + l_k, _jax.tree.map(_jnp.add, grad_sum, gw_k)), gx_k

        init = (_jnp.zeros((), _jnp.float32), _jax.tree.map(_jnp.zeros_like, weights))
        (loss, grad_w), grad_x = _jax.lax.scan(body, init, (per_example, given["loss_target"]))
    with _jax.named_scope("update"):
        delta_w, new_m, new_v = {}, {}, {}
        for n in TWIN_WEIGHTS:
            delta_w[n], new_m[n], new_v[n] = _adamw(weights[n], grad_w[n], given["m_" + n], given["v_" + n])
    return (loss, grad_x, *[grad_w[n] for n in TWIN_WEIGHTS], *[delta_w[n] for n in TWIN_WEIGHTS],
            *[new_m[n] for n in TWIN_WEIGHTS], *[new_v[n] for n in TWIN_WEIGHTS])
```

```python
import functools

import jax
import jax.numpy as jnp
from jax import lax
from jax.experimental import pallas as pl
from jax.experimental.pallas import tpu as pltpu

F32 = jnp.float32
BF16 = jnp.bfloat16
SDS = jax.ShapeDtypeStruct
MESH = pl.DeviceIdType.MESH

D_MODEL = 1024
D_FF = 4096
N_CHIPS = 4
N_DEV = 8
DILATIONS = (1, 4, 16)
BAND = 128
GROUP_W = 256
PAIR_W = 128
HEAD_W = 64
POOL_W = 768
POOL_GROUP_W = 192
POOL_WINDOWS = (2, 4, 8, 16)
POOL_HALO = 16
N_IN = 5120
CHUNK = 256
N_CHUNKS = N_IN // CHUNK
CHUNKS_PER_SHARD = 5
NORM_EPS = 1e-6
ALIBI_MAX_BIAS = 8.0
N_HEADS = 12
NEG = -1e30

ADAM_LR, ADAM_B1, ADAM_B2, ADAM_EPS, ADAM_WD, ADAM_STEP = 0.001, 0.9, 0.999, 1e-08, 0.01, 10

TM = 512
TMB = 256
ATT_TILE = ((1, 4), (4, 1), (4, 1))
BK = 1024
VMEM_LIMIT = 56 * 1024 * 1024
PACK_ROWS = 184

NT = (((1,), (1,)), ((), ()))
TN = (((0,), (0,)), ((), ()))


def _cp(*sem):
    return pltpu.CompilerParams(dimension_semantics=sem, vmem_limit_bytes=VMEM_LIMIT)


def _resident(shape):
    nd = len(shape)
    return pl.BlockSpec(shape, lambda *_: (0,) * nd, pipeline_mode=pl.Buffered(1))


def _row_block(rows, cap=256):
    return max(b for b in range(16, min(rows, cap) + 1, 16) if rows % b == 0)


def _dot(a, b):
    return jnp.dot(a, b, preferred_element_type=F32)


def _dot_nt(a, b):
    return lax.dot_general(a, b, NT, preferred_element_type=F32)


def _dot_tn(a, b):
    return lax.dot_general(a, b, TN, preferred_element_type=F32)


def _w_in_chunk(w_ref, n):
    return w_ref[n // CHUNKS_PER_SHARD, :, (n % CHUNKS_PER_SHARD) * CHUNK:(n % CHUNKS_PER_SHARD + 1) * CHUNK]


def _rms_fwd(x, g):
    r = lax.rsqrt(jnp.mean(x * x, axis=-1, keepdims=True) + NORM_EPS)
    xh = x * r
    return xh * g, xh, r


def _rms_bwd(dy, xh, r, g):
    dxh = dy * g
    return r * (dxh - xh * jnp.mean(dxh * xh, axis=-1, keepdims=True))


def _deinterleave_store(val, s_ref, out_ref, lead, d, rows, dtype):
    if d == 1:
        out_ref[lead + (0,)] = val.astype(dtype)
        return
    for h in range(2):
        s_ref[h] = val[:, h * PAIR_W:(h + 1) * PAIR_W]
    for r in range(d):
        for h in range(2):
            out_ref[lead + (r, slice(None), slice(h * PAIR_W, (h + 1) * PAIR_W))] = (
                s_ref[h, pl.ds(r, rows // d, stride=d), :].astype(dtype))


def _interleave_load(in_ref, lead, s_ref, d, rows):
    for r in range(d):
        for h in range(2):
            s_ref[h, pl.ds(r, rows // d, stride=d), :] = (
                in_ref[lead + (r, slice(None), slice(h * PAIR_W, (h + 1) * PAIR_W))].astype(F32))


def _norm_inproj(x, g, w_in):
    S = x.shape[0]

    def body(x_ref, g_ref, w_ref, u_ref, q0_ref, q1_ref, q2_ref, pz_ref, gate_ref, s_ref):
        u = _rms_fwd(x_ref[...], g_ref[...])[0].astype(BF16)
        u_ref[...] = u
        qkv_refs = (q0_ref, q1_ref, q2_ref)
        for n in range(N_CHUNKS):
            zc = _dot(u, _w_in_chunk(w_ref, n))
            if n < 9:
                which, grp = n // 3, n % 3
                if which == 0:
                    zc = zc * 0.125
                _deinterleave_store(zc, s_ref, qkv_refs[grp], (which,), DILATIONS[grp], TM, BF16)
            elif n < 12:
                pz_ref[:, (n - 9) * CHUNK:(n - 8) * CHUNK] = zc
            else:
                gate_ref[:, (n - 12) * CHUNK:(n - 11) * CHUNK] = zc

    row = lambda w: pl.BlockSpec((TM, w), lambda i: (i, 0))
    return pl.pallas_call(
        body, grid=(S // TM,), name="norm_inproj",
        in_specs=[row(D_MODEL), _resident((1, D_MODEL)), _resident(w_in.shape)],
        out_specs=[row(D_MODEL)]
        + [pl.BlockSpec((3, d, TM // d, GROUP_W), lambda i: (0, 0, i, 0)) for d in DILATIONS]
        + [row(POOL_W), row(2 * D_MODEL)],
        out_shape=[SDS((S, D_MODEL), BF16)]
        + [SDS((3, d, S // d, GROUP_W), BF16) for d in DILATIONS]
        + [SDS((S, POOL_W), F32), SDS((S, 2 * D_MODEL), F32)],
        scratch_shapes=[pltpu.VMEM((2, TM, PAIR_W), F32)],
        compiler_params=_cp("parallel"),
    )(x, g, w_in)


def _band_bias(grp, d):
    row = lax.broadcasted_iota(jnp.int32, (BAND, 2 * BAND), 0)
    col = lax.broadcasted_iota(jnp.int32, (BAND, 2 * BAND), 1)
    steps = BAND + row - col
    valid = (steps >= 0) & (steps <= BAND)
    stepsf = (steps * d).astype(F32)
    biases = []
    for hh in range(4):
        slope = 2.0 ** (-ALIBI_MAX_BIAS * (grp * 4 + hh + 1) / N_HEADS)
        biases.append(jnp.where(valid, -slope * stepsf, NEG))
    return biases, col


def _attn_tiles(grp, L):
    rr, rb = ATT_TILE[grp]
    rb = min(rb, L // BAND)
    return rr, rb, L // (rb * BAND)


def _kv_tile(cur_ref, prev_ref, rr, rb, cs):
    if rb == 0:
        return jnp.concatenate([prev_ref[rr, :, cs], cur_ref[rr, 0:BAND, cs]], axis=0)
    return cur_ref[rr, (rb - 1) * BAND:(rb + 1) * BAND, cs]


def _attn_fwd(qkv, grp):
    d = DILATIONS[grp]
    L = qkv.shape[2]
    RR, RB, nb = _attn_tiles(grp, L)

    def body(q_ref, kc_ref, kp_ref, vc_ref, vp_ref, o_ref, lse_ref):
        i = pl.program_id(0)
        biases, col = _band_bias(grp, d)
        first_keys_ok = (col >= BAND) | (i > 0)
        is_a = lax.broadcasted_iota(jnp.int32, (BAND, PAIR_W), 1) < HEAD_W
        for rr in range(RR):
            for rb in range(RB):
                rows = slice(rb * BAND, (rb + 1) * BAND)
                for cp in range(2):
                    cs = slice(cp * PAIR_W, (cp + 1) * PAIR_W)
                    q2 = q_ref[rr, rows, cs]
                    kcat = _kv_tile(kc_ref, kp_ref, rr, rb, cs)
                    vcat = _kv_tile(vc_ref, vp_ref, rr, rb, cs)
                    res = []
                    for h2 in range(2):
                        sel = is_a if h2 == 0 else jnp.logical_not(is_a)
                        b = biases[cp * 2 + h2]
                        if rb == 0:
                            b = jnp.where(first_keys_ok, b, NEG)
                        s = _dot_nt(jnp.where(sel, q2, jnp.zeros_like(q2)), kcat) + b
                        m = jnp.max(s, axis=-1, keepdims=True)
                        p = jnp.exp(s - m)
                        l = jnp.sum(p, axis=-1, keepdims=True)
                        o = _dot(p.astype(BF16), vcat) * (1.0 / l)
                        res.append((o, m + jnp.log(l)))
                    o_ref[rr, rows, cs] = jnp.where(is_a, res[0][0], res[1][0])
                    lse_ref[rr, rows, cs] = jnp.where(is_a, res[0][1], res[1][1])

    cur = lambda w: pl.BlockSpec((None, RR, RB * BAND, GROUP_W), lambda i, j: (w, j, i, 0))
    prev = lambda w: pl.BlockSpec((None, RR, BAND, GROUP_W), lambda i, j: (w, j, jnp.maximum(i * RB - 1, 0), 0))
    out = pl.BlockSpec((RR, RB * BAND, GROUP_W), lambda i, j: (j, i, 0))
    return pl.pallas_call(
        body, grid=(nb, d // RR), name=f"attn_fwd_g{grp}",
        in_specs=[cur(0), cur(1), prev(1), cur(2), prev(2)],
        out_specs=[out, out],
        out_shape=[SDS((d, L, GROUP_W), F32), SDS((d, L, GROUP_W), F32)],
        compiler_params=_cp("parallel", "parallel"),
    )(qkv, qkv, qkv, qkv, qkv)


def _pool_column_select(col, vals):
    return jnp.where(col < POOL_GROUP_W, vals[0],
                     jnp.where(col < 2 * POOL_GROUP_W, vals[1],
                               jnp.where(col < 3 * POOL_GROUP_W, vals[2], vals[3])))


def _pool_inv_count(i, rows):
    t = i * rows + lax.broadcasted_iota(jnp.int32, (rows, POOL_W), 0)
    col = lax.broadcasted_iota(jnp.int32, (rows, POOL_W), 1)
    win = _pool_column_select(col, POOL_WINDOWS)
    return 1.0 / jnp.minimum(t + 1, win).astype(F32), col


def _mixer_out(outs, lses, pz, gates, x, w_ao, w_po, wbd, scale, w_out, g_mlp):
    S = x.shape[0]
    n_tiles = S // TMB

    def body(o0_ref, l0_ref, o1_ref, l1_ref, o2_ref, l2_ref, pz_ref, halo_ref, gate_ref, x_ref,
             wao_ref, wpo_ref, wbd_ref, sc_ref, wout_ref, g_ref,
             a_ref, lt0_ref, lt1_ref, lt2_ref, pooled_ref, mixed_ref, p_ref, merged_ref, h1_ref, m_ref,
             so1, sl1, so2, sl2, slt, ext_ref):
        i = pl.program_id(0)
        _interleave_load(o1_ref, (), so1, DILATIONS[1], TMB)
        _interleave_load(l1_ref, (), sl1, DILATIONS[1], TMB)
        _interleave_load(o2_ref, (), so2, DILATIONS[2], TMB)
        _interleave_load(l2_ref, (), sl2, DILATIONS[2], TMB)
        for h in range(2):
            hs = slice(h * PAIR_W, (h + 1) * PAIR_W)
            l0, l1, l2 = l0_ref[0, :, hs], sl1[h], sl2[h]
            mx = jnp.maximum(jnp.maximum(l0, l1), l2)
            e0, e1, e2 = jnp.exp(l0 - mx), jnp.exp(l1 - mx), jnp.exp(l2 - mx)
            den = e0 + e1 + e2
            a_ref[:, hs] = ((e0 * o0_ref[0, :, hs] + e1 * so1[h] + e2 * so2[h]) * (1.0 / den)).astype(BF16)
            slt[h] = mx + jnp.log(den)
        lt = jnp.concatenate([slt[0], slt[1]], axis=1)
        lt0_ref[0] = lt
        for ref, d in ((lt1_ref, DILATIONS[1]), (lt2_ref, DILATIONS[2])):
            for r in range(d):
                for h in range(2):
                    ref[r, :, h * PAIR_W:(h + 1) * PAIR_W] = slt[h, pl.ds(r, TMB // d, stride=d), :]

        pz_t = pz_ref[...]
        ext_ref[0:POOL_HALO, :] = jnp.where(i > 0, halo_ref[...], 0.0)
        ext_ref[POOL_HALO:, :] = pz_t
        sums = []
        acc = ext_ref[...]
        for k in (1, 2, 4, 8):
            acc = acc + pltpu.roll(acc, k, 0)
            sums.append(acc[POOL_HALO:, :])
        inv_cnt, col = _pool_inv_count(i, TMB)
        pooled = (_pool_column_select(col, sums) * inv_cnt - pz_t).astype(BF16)
        pooled_ref[...] = pooled
        mixed = _dot(pooled, wbd_ref[...])
        mixed_ref[...] = mixed
        p = (mixed * sc_ref[...]).astype(BF16)
        p_ref[...] = p

        a = a_ref[...]
        for j in range(N_CHIPS):
            js = slice(j * CHUNK, (j + 1) * CHUNK)
            ga = gate_ref[:, js]
            gp = gate_ref[:, D_MODEL + j * CHUNK:D_MODEL + (j + 1) * CHUNK]
            mj = jax.nn.sigmoid(ga) * _dot(a, wao_ref[j]) + jax.nn.sigmoid(gp) * _dot(p, wpo_ref[j])
            merged_ref[:, js] = mj.astype(BF16)
        h1 = x_ref[...] + _dot(merged_ref[...], wout_ref[...])
        h1_ref[...] = h1
        m_ref[...] = _rms_fwd(h1, g_ref[...])[0].astype(BF16)

    row = lambda w: pl.BlockSpec((TMB, w), lambda i: (i, 0))
    grp_spec = lambda d: pl.BlockSpec((d, TMB // d, GROUP_W), lambda i: (0, i, 0))
    halo = pl.BlockSpec((POOL_HALO, POOL_W), lambda i: (jnp.maximum(i * (TMB // POOL_HALO) - 1, 0), 0))
    d0, d1, d2 = DILATIONS
    return pl.pallas_call(
        body, grid=(n_tiles,), name="mixer_out",
        in_specs=[grp_spec(d0), grp_spec(d0), grp_spec(d1), grp_spec(d1), grp_spec(d2), grp_spec(d2),
                  row(POOL_W), halo, row(2 * D_MODEL), row(D_MODEL),
                  _resident(w_ao.shape), _resident(w_po.shape), _resident(wbd.shape), _resident(scale.shape),
                  _resident(w_out.shape), _resident(g_mlp.shape)],
        out_specs=[row(GROUP_W), grp_spec(d0), grp_spec(d1), grp_spec(d2),
                   row(POOL_W), row(POOL_W), row(POOL_W), row(D_MODEL), row(D_MODEL), row(D_MODEL)],
        out_shape=[SDS((S, GROUP_W), BF16)] + [SDS((d, S // d, GROUP_W), F32) for d in DILATIONS]
        + [SDS((S, POOL_W), BF16), SDS((S, POOL_W), F32), SDS((S, POOL_W), BF16),
           SDS((S, D_MODEL), BF16), SDS((S, D_MODEL), F32), SDS((S, D_MODEL), BF16)],
        scratch_shapes=[pltpu.VMEM((2, TMB, PAIR_W), F32) for _ in range(5)]
        + [pltpu.VMEM((TMB + POOL_HALO, POOL_W), F32)],
        compiler_params=_cp("parallel"),
    )(outs[0], lses[0], outs[1], lses[1], outs[2], lses[2], pz, pz, gates, x,
      w_ao, w_po, wbd, scale, w_out, g_mlp)


def _mlp_fwd_loss(m, h1, target, w_mi, w_mo, g_f):
    S = m.shape[0]

    def body(m_ref, h1_ref, t_ref, wmi_ref, wmo_ref, g_ref, hid_ref, dh2_ref, dh2b_ref, loss_ref, dg_ref):
        @pl.when(pl.program_id(0) == 0)
        def _():
            loss_ref[...] = jnp.zeros_like(loss_ref)
            dg_ref[...] = jnp.zeros_like(dg_ref)

        mt = m_ref[...]
        acc = h1_ref[...]
        for c in range(N_CHIPS):
            hid = jnp.square(jnp.maximum(_dot(mt, wmi_ref[c]), 0.0)).astype(BF16)
            hid_ref[:, c * D_MODEL:(c + 1) * D_MODEL] = hid
            acc = acc + _dot(hid, wmo_ref[c])
        g = g_ref[...]
        y, hh, r = _rms_fwd(acc, g)
        e = y - t_ref[...]
        loss_ref[...] += jnp.sum(e * e, axis=0, keepdims=True)
        dy = e * (1.0 / D_MODEL)
        dg_ref[...] += jnp.sum(dy * hh, axis=0, keepdims=True)
        dh2 = _rms_bwd(dy, hh, r, g)
        dh2_ref[...] = dh2
        dh2b_ref[...] = dh2.astype(BF16)

    row = lambda w: pl.BlockSpec((TM, w), lambda i: (i, 0))
    vec = pl.BlockSpec((1, D_MODEL), lambda i: (0, 0))
    return pl.pallas_call(
        body, grid=(S // TM,), name="mlp_fwd_loss",
        in_specs=[row(D_MODEL), row(D_MODEL), row(D_MODEL), _resident(w_mi.shape), _resident(w_mo.shape),
                  _resident(g_f.shape)],
        out_specs=[row(D_FF), row(D_MODEL), row(D_MODEL), vec, vec],
        out_shape=[SDS((S, D_FF), BF16), SDS((S, D_MODEL), F32), SDS((S, D_MODEL), BF16),
                   SDS((1, D_MODEL), F32), SDS((1, D_MODEL), F32)],
        compiler_params=_cp("arbitrary"),
    )(m, h1, target, w_mi, w_mo, g_f)


def _mlp_bwd(dh2, dh2b, hid, h1, w_mi, w_mo, g_mlp):
    S = dh2.shape[0]

    def body(dh2_ref, dh2b_ref, hid_ref, h1_ref, wmi_ref, wmo_ref, g_ref, dpre_ref, dh1_ref, dh1b_ref, dg_ref):
        @pl.when(pl.program_id(0) == 0)
        def _():
            dg_ref[...] = jnp.zeros_like(dg_ref)

        d2 = dh2b_ref[...]
        dm = jnp.zeros((TM, D_MODEL), F32)
        for c in range(N_CHIPS):
            cs = slice(c * D_MODEL, (c + 1) * D_MODEL)
            dhid = _dot_nt(d2, wmo_ref[c])
            dpre = (dhid * (2.0 * jnp.sqrt(hid_ref[:, cs].astype(F32)))).astype(BF16)
            dpre_ref[:, cs] = dpre
            dm = dm + _dot_nt(dpre, wmi_ref[c])
        g = g_ref[...]
        _, hh, r = _rms_fwd(h1_ref[...], g)
        dg_ref[...] += jnp.sum(dm * hh, axis=0, keepdims=True)
        dh1 = dh2_ref[...] + _rms_bwd(dm, hh, r, g)
        dh1_ref[...] = dh1
        dh1b_ref[...] = dh1.astype(BF16)

    row = lambda w: pl.BlockSpec((TM, w), lambda i: (i, 0))
    return pl.pallas_call(
        body, grid=(S // TM,), name="mlp_bwd",
        in_specs=[row(D_MODEL), row(D_MODEL), row(D_FF), row(D_MODEL), _resident(w_mi.shape),
                  _resident(w_mo.shape), _resident(g_mlp.shape)],
        out_specs=[row(D_FF), row(D_MODEL), row(D_MODEL), pl.BlockSpec((1, D_MODEL), lambda i: (0, 0))],
        out_shape=[SDS((S, D_FF), BF16), SDS((S, D_MODEL), F32), SDS((S, D_MODEL), BF16), SDS((1, D_MODEL), F32)],
        compiler_params=_cp("arbitrary"),
    )(dh2, dh2b, hid, h1, w_mi, w_mo, g_mlp)


def _mixer_bwd(dh1b, a, p, mixed, gates, w_out, w_ao, w_po, wbd, scale, head_ones):
    S = a.shape[0]

    def body(dh1b_ref, a_ref, p_ref, mixed_ref, gate_ref, wout_ref, wao_ref, wpo_ref, wbd_ref, sc_ref, ones_ref,
             da1_ref, dp1_ref, dgate_ref, da0_ref, dag1_ref, dag2_ref, dd0_ref, dd1_ref, dd2_ref,
             dmixed_ref, dqp_ref, dscale_ref, s_da, s_dd):
        i = pl.program_id(0)

        @pl.when(i == 0)
        def _():
            dscale_ref[...] = jnp.zeros_like(dscale_ref)

        dmerged = _dot_nt(dh1b_ref[...], wout_ref[...])
        a = a_ref[...]
        p = p_ref[...]
        da = jnp.zeros((TMB, GROUP_W), F32)
        dp = jnp.zeros((TMB, POOL_W), F32)
        for j in range(N_CHIPS):
            js = slice(j * CHUNK, (j + 1) * CHUNK)
            sa = jax.nn.sigmoid(gate_ref[:, js])
            sp = jax.nn.sigmoid(gate_ref[:, D_MODEL + j * CHUNK:D_MODEL + (j + 1) * CHUNK])
            dmj = dmerged[:, js]
            da1 = (dmj * sa).astype(BF16)
            dp1 = (dmj * sp).astype(BF16)
            da1_ref[:, js] = da1
            dp1_ref[:, js] = dp1
            dgate_ref[j] = (dmj * _dot(a, wao_ref[j]) * sa * (1.0 - sa)).astype(BF16)
            dgate_ref[N_CHIPS + j] = (dmj * _dot(p, wpo_ref[j]) * sp * (1.0 - sp)).astype(BF16)
            da = da + _dot_nt(da1, wao_ref[j])
            dp = dp + _dot_nt(dp1, wpo_ref[j])

        prod = da * a.astype(F32)
        hi = prod.astype(BF16)
        lo = (prod - hi.astype(F32)).astype(BF16)
        dd = _dot(hi, ones_ref[...]) + _dot(lo, ones_ref[...])
        for ref, val, sref, dtype in ((da0_ref, da, s_da, BF16), (dd0_ref, dd, s_dd, F32)):
            ref[0] = val.astype(dtype)
            for h in range(2):
                sref[h] = val[:, h * PAIR_W:(h + 1) * PAIR_W]
        for refs, d in (((dag1_ref, dd1_ref), DILATIONS[1]), ((dag2_ref, dd2_ref), DILATIONS[2])):
            for r in range(d):
                for h in range(2):
                    hs = slice(h * PAIR_W, (h + 1) * PAIR_W)
                    refs[0][r, :, hs] = s_da[h, pl.ds(r, TMB // d, stride=d), :].astype(BF16)
                    refs[1][r, :, hs] = s_dd[h, pl.ds(r, TMB // d, stride=d), :]

        sc = sc_ref[...]
        dscale_ref[...] += jnp.sum(dp * mixed_ref[...], axis=0, keepdims=True)
        dmixed = (dp * sc).astype(BF16)
        dmixed_ref[...] = dmixed
        inv_cnt, _ = _pool_inv_count(i, TMB)
        dqp_ref[...] = _dot_nt(dmixed, wbd_ref[...]) * inv_cnt

    row = lambda w: pl.BlockSpec((TMB, w), lambda i: (i, 0))
    grp_spec = lambda d: pl.BlockSpec((d, TMB // d, GROUP_W), lambda i: (0, i, 0))
    d0, d1, d2 = DILATIONS
    return pl.pallas_call(
        body, grid=(S // TMB,), name="mixer_bwd",
        in_specs=[row(D_MODEL), row(GROUP_W), row(POOL_W), row(POOL_W), row(2 * D_MODEL),
                  _resident(w_out.shape), _resident(w_ao.shape), _resident(w_po.shape), _resident(wbd.shape),
                  _resident(scale.shape), _resident(head_ones.shape)],
        out_specs=[row(D_MODEL), row(D_MODEL), pl.BlockSpec((2 * N_CHIPS, TMB, CHUNK), lambda i: (0, i, 0)),
                   grp_spec(d0), grp_spec(d1), grp_spec(d2), grp_spec(d0), grp_spec(d1), grp_spec(d2),
                   row(POOL_W), row(POOL_W), pl.BlockSpec((1, POOL_W), lambda i: (0, 0))],
        out_shape=[SDS((S, D_MODEL), BF16), SDS((S, D_MODEL), BF16), SDS((2 * N_CHIPS, S, CHUNK), BF16)]
        + [SDS((d, S // d, GROUP_W), BF16) for d in DILATIONS]
        + [SDS((d, S // d, GROUP_W), F32) for d in DILATIONS]
        + [SDS((S, POOL_W), BF16), SDS((S, POOL_W), F32), SDS((1, POOL_W), F32)],
        scratch_shapes=[pltpu.VMEM((2, TMB, PAIR_W), F32), pltpu.VMEM((2, TMB, PAIR_W), F32)],
        compiler_params=_cp("arbitrary"),
    )(dh1b, a, p, mixed, gates, w_out, w_ao, w_po, wbd, scale, head_ones)


def _attn_bwd(qkv, da, lt, dd, grp):
    d = DILATIONS[grp]
    L = qkv.shape[2]
    RR, RB, nb = _attn_tiles(grp, L)

    def body(q_ref, kc_ref, kp_ref, vc_ref, vp_ref, da_ref, lt_ref, dd_ref, dq_ref, dk_ref, dv_ref, dk_acc, dv_acc):
        i = pl.program_id(1)
        cur = i % 2
        prv = 1 - cur

        @pl.when(i == 0)
        def _():
            dk_acc[...] = jnp.zeros_like(dk_acc)
            dv_acc[...] = jnp.zeros_like(dv_acc)

        @pl.when(i < nb)
        def _():
            dk_acc[cur] = jnp.zeros((RR, RB * BAND, GROUP_W), F32)
            dv_acc[cur] = jnp.zeros((RR, RB * BAND, GROUP_W), F32)
            biases, col = _band_bias(grp, d)
            first_keys_ok = (col >= BAND) | (i > 0)
            is_a = lax.broadcasted_iota(jnp.int32, (BAND, PAIR_W), 1) < HEAD_W
            is_a_kv = lax.broadcasted_iota(jnp.int32, (2 * BAND, PAIR_W), 1) < HEAD_W
            for rr in range(RR):
                for rb in range(RB):
                    rows = slice(rb * BAND, (rb + 1) * BAND)
                    for cp in range(2):
                        cs = slice(cp * PAIR_W, (cp + 1) * PAIR_W)
                        q2 = q_ref[rr, rows, cs]
                        da2 = da_ref[rr, rows, cs]
                        lt2 = lt_ref[rr, rows, cs]
                        dd2 = dd_ref[rr, rows, cs]
                        kcat = _kv_tile(kc_ref, kp_ref, rr, rb, cs)
                        vcat = _kv_tile(vc_ref, vp_ref, rr, rb, cs)
                        res = []
                        for h2 in range(2):
                            sel = is_a if h2 == 0 else jnp.logical_not(is_a)
                            lane0 = h2 * HEAD_W
                            b = biases[cp * 2 + h2]
                            if rb == 0:
                                b = jnp.where(first_keys_ok, b, NEG)
                            s = _dot_nt(jnp.where(sel, q2, jnp.zeros_like(q2)), kcat) + b
                            p = jnp.exp(s - lt2[:, lane0:lane0 + 1])
                            dpv = _dot_nt(jnp.where(sel, da2, jnp.zeros_like(da2)), vcat)
                            ds = (p * (dpv - dd2[:, lane0:lane0 + 1])).astype(BF16)
                            res.append((_dot(ds, kcat), _dot_tn(ds, q2), _dot_tn(p.astype(BF16), da2)))
                        dq_ref[rr, rows, cs] = (jnp.where(is_a, res[0][0], res[1][0]) * 0.125).astype(BF16)
                        dkc = jnp.where(is_a_kv, res[0][1], res[1][1])
                        dvc = jnp.where(is_a_kv, res[0][2], res[1][2])
                        if rb == 0:
                            last = slice((RB - 1) * BAND, RB * BAND)
                            dk_acc[prv, rr, last, cs] += dkc[0:BAND]
                            dv_acc[prv, rr, last, cs] += dvc[0:BAND]
                            dk_acc[cur, rr, 0:BAND, cs] += dkc[BAND:]
                            dv_acc[cur, rr, 0:BAND, cs] += dvc[BAND:]
                        else:
                            both = slice((rb - 1) * BAND, (rb + 1) * BAND)
                            dk_acc[cur, rr, both, cs] += dkc
                            dv_acc[cur, rr, both, cs] += dvc

        @pl.when(i > 0)
        def _():
            dk_ref[...] = dk_acc[prv].astype(BF16)
            dv_ref[...] = dv_acc[prv].astype(BF16)

    qi = lambda i: jnp.minimum(i, nb - 1)
    cur_w = lambda w: pl.BlockSpec((None, RR, RB * BAND, GROUP_W), lambda j, i: (w, j, qi(i), 0))
    prev_w = lambda w: pl.BlockSpec((None, RR, BAND, GROUP_W),
                                    lambda j, i: (w, j, jnp.maximum(qi(i) * RB - 1, 0), 0))
    blk = pl.BlockSpec((RR, RB * BAND, GROUP_W), lambda j, i: (j, qi(i), 0))
    late = pl.BlockSpec((RR, RB * BAND, GROUP_W), lambda j, i: (j, jnp.maximum(i - 1, 0), 0))
    return pl.pallas_call(
        body, grid=(d // RR, nb + 1), name=f"attn_bwd_g{grp}",
        in_specs=[cur_w(0), cur_w(1), prev_w(1), cur_w(2), prev_w(2), blk, blk, blk],
        out_specs=[blk, late, late],
        out_shape=[SDS((d, L, GROUP_W), BF16)] * 3,
        scratch_shapes=[pltpu.VMEM((2, RR, RB * BAND, GROUP_W), F32), pltpu.VMEM((2, RR, RB * BAND, GROUP_W), F32)],
        compiler_params=_cp("parallel", "arbitrary"),
    )(qkv, qkv, qkv, qkv, qkv, da, lt, dd)


def _inproj_bwd(dqkv, dqp, dgates, dh1, x, g, w_in):
    S = x.shape[0]
    n_tiles = S // TMB

    def body(*refs):
        dqkv_refs = refs[0:9]
        dqp_ref, halo_ref, dgate_ref, dh1_ref, x_ref, g_ref, w_ref = refs[9:16]
        dz_ref, dx_ref, dg_ref, s_ref, ext_ref = refs[16:]
        i = pl.program_id(0)

        @pl.when(i == 0)
        def _():
            dg_ref[...] = jnp.zeros_like(dg_ref)

        for grp in range(3):
            for which in range(3):
                n = which * 3 + grp
                ref = dqkv_refs[grp * 3 + which]
                if DILATIONS[grp] == 1:
                    dz_ref[n] = ref[0]
                else:
                    _interleave_load(ref, (), s_ref, DILATIONS[grp], TMB)
                    for h in range(2):
                        dz_ref[n, :, h * PAIR_W:(h + 1) * PAIR_W] = s_ref[h].astype(BF16)

        dqp = dqp_ref[...]
        ext_ref[0:TMB, :] = dqp
        ext_ref[TMB:, :] = jnp.where(i < n_tiles - 1, halo_ref[...], 0.0)
        sums = []
        acc = ext_ref[...]
        for k in (1, 2, 4, 8):
            acc = acc + pltpu.roll(acc, TMB + POOL_HALO - k, 0)
            sums.append(acc[0:TMB, :])
        inv_cnt, col = _pool_inv_count(i, TMB)
        dpz = _pool_column_select(col, sums) - dqp / inv_cnt
        for t in range(3):
            dz_ref[9 + t] = dpz[:, t * CHUNK:(t + 1) * CHUNK].astype(BF16)
        for t in range(2 * N_CHIPS):
            dz_ref[12 + t] = dgate_ref[t]

        du = jnp.zeros((TMB, D_MODEL), F32)
        for n in range(N_CHUNKS):
            du = du + _dot_nt(dz_ref[n], _w_in_chunk(w_ref, n))
        gv = g_ref[...]
        _, xh, r = _rms_fwd(x_ref[...], gv)
        dg_ref[...] += jnp.sum(du * xh, axis=0, keepdims=True)
        dx_ref[...] = dh1_ref[...] + _rms_bwd(du, xh, r, gv)

    row = lambda w: pl.BlockSpec((TMB, w), lambda i: (i, 0))
    grp_spec = lambda d: pl.BlockSpec((d, TMB // d, GROUP_W), lambda i: (0, i, 0))
    halo = pl.BlockSpec((POOL_HALO, POOL_W),
                        lambda i: (jnp.minimum((i + 1) * (TMB // POOL_HALO), S // POOL_HALO - 1), 0))
    flat = [t for grp in range(3) for t in dqkv[grp]]
    return pl.pallas_call(
        body, grid=(n_tiles,), name="inproj_bwd",
        in_specs=[grp_spec(DILATIONS[grp]) for grp in range(3) for _ in range(3)]
        + [row(POOL_W), halo, pl.BlockSpec((2 * N_CHIPS, TMB, CHUNK), lambda i: (0, i, 0)),
           row(D_MODEL), row(D_MODEL), _resident(g.shape), _resident(w_in.shape)],
        out_specs=[pl.BlockSpec((N_CHUNKS, TMB, CHUNK), lambda i: (0, i, 0)), row(D_MODEL),
                   pl.BlockSpec((1, D_MODEL), lambda i: (0, 0))],
        out_shape=[SDS((N_CHUNKS, S, CHUNK), BF16), SDS((S, D_MODEL), F32), SDS((1, D_MODEL), F32)],
        scratch_shapes=[pltpu.VMEM((2, TMB, PAIR_W), F32), pltpu.VMEM((TMB + POOL_HALO, POOL_W), F32)],
        compiler_params=_cp("arbitrary"),
    )(*flat, dqp, dqp, dgates, dh1, x, g, w_in)


def _wgrad(a, b, name, *, out_shape, a_spec, b_spec, out_spec, grid, n_out_cols=None):
    k_axis = len(grid) - 1

    def body(a_ref, b_ref, o_ref):
        @pl.when(pl.program_id(k_axis) == 0)
        def _():
            o_ref[...] = jnp.zeros_like(o_ref)

        at = a_ref[...]
        if n_out_cols is None:
            o_ref[...] += _dot_tn(at, b_ref[...])
        elif n_out_cols[0] == "lead_b":
            w = n_out_cols[1]
            for t in range(b_ref.shape[0]):
                o_ref[:, t * w:(t + 1) * w] += _dot_tn(at, b_ref[t])
        else:
            w = n_out_cols[1]
            for t in range(o_ref.shape[0]):
                o_ref[t] += _dot_tn(at, b_ref[:, t * w:(t + 1) * w])

    sem = ("parallel",) * k_axis + ("arbitrary",)
    return pl.pallas_call(body, grid=grid, name=name, in_specs=[a_spec, b_spec], out_specs=out_spec,
                          out_shape=out_shape, compiler_params=_cp(*sem))(a, b)


def _weight_grads(S, u, dz, a, da1, p, dp1, merged, dh1b, m, dpre, hid, dh2b, pooled, dmixed):
    nk = S // BK
    g_in = _wgrad(
        u, dz, "wgrad_in", grid=(N_CHIPS, nk), n_out_cols=("lead_b", CHUNK),
        a_spec=pl.BlockSpec((BK, D_MODEL), lambda j, k: (k, 0)),
        b_spec=pl.BlockSpec((CHUNKS_PER_SHARD, BK, CHUNK), lambda j, k: (j, k, 0)),
        out_spec=pl.BlockSpec((None, D_MODEL, CHUNKS_PER_SHARD * CHUNK), lambda j, k: (j, 0, 0)),
        out_shape=SDS((N_CHIPS, D_MODEL, CHUNKS_PER_SHARD * CHUNK), F32))
    g_ao = _wgrad(
        a, da1, "wgrad_att_out", grid=(nk,), n_out_cols=("cols_b", CHUNK),
        a_spec=pl.BlockSpec((BK, GROUP_W), lambda k: (k, 0)),
        b_spec=pl.BlockSpec((BK, D_MODEL), lambda k: (k, 0)),
        out_spec=pl.BlockSpec((N_CHIPS, GROUP_W, CHUNK), lambda k: (0, 0, 0)),
        out_shape=SDS((N_CHIPS, GROUP_W, CHUNK), F32))
    g_po = _wgrad(
        p, dp1, "wgrad_pool_out", grid=(nk,), n_out_cols=("cols_b", CHUNK),
        a_spec=pl.BlockSpec((BK, POOL_W), lambda k: (k, 0)),
        b_spec=pl.BlockSpec((BK, D_MODEL), lambda k: (k, 0)),
        out_spec=pl.BlockSpec((N_CHIPS, POOL_W, CHUNK), lambda k: (0, 0, 0)),
        out_shape=SDS((N_CHIPS, POOL_W, CHUNK), F32))
    g_out = _wgrad(
        merged, dh1b, "wgrad_out", grid=(nk,),
        a_spec=pl.BlockSpec((BK, D_MODEL), lambda k: (k, 0)),
        b_spec=pl.BlockSpec((BK, D_MODEL), lambda k: (k, 0)),
        out_spec=pl.BlockSpec((D_MODEL, D_MODEL), lambda k: (0, 0)),
        out_shape=SDS((D_MODEL, D_MODEL), F32))
    g_mi = _wgrad(
        m, dpre, "wgrad_mlp_in", grid=(N_CHIPS, nk),
        a_spec=pl.BlockSpec((BK, D_MODEL), lambda c, k: (k, 0)),
        b_spec=pl.BlockSpec((BK, D_MODEL), lambda c, k: (k, c)),
        out_spec=pl.BlockSpec((None, D_MODEL, D_MODEL), lambda c, k: (c, 0, 0)),
        out_shape=SDS((N_CHIPS, D_MODEL, D_MODEL), F32))
    g_mo = _wgrad(
        hid, dh2b, "wgrad_mlp_out", grid=(N_CHIPS, nk),
        a_spec=pl.BlockSpec((BK, D_MODEL), lambda c, k: (k, c)),
        b_spec=pl.BlockSpec((BK, D_MODEL), lambda c, k: (k, 0)),
        out_spec=pl.BlockSpec((None, D_MODEL, D_MODEL), lambda c, k: (c, 0, 0)),
        out_shape=SDS((N_CHIPS, D_MODEL, D_MODEL), F32))
    g_bd = _wgrad(
        pooled, dmixed, "wgrad_pool_grp", grid=(nk,),
        a_spec=pl.BlockSpec((BK, POOL_W), lambda k: (k, 0)),
        b_spec=pl.BlockSpec((BK, POOL_W), lambda k: (k, 0)),
        out_spec=pl.BlockSpec((POOL_W, POOL_W), lambda k: (0, 0)),
        out_shape=SDS((POOL_W, POOL_W), F32))
    return [g_in, g_ao, g_po, g_out.reshape(N_CHIPS, D_MODEL // N_CHIPS, D_MODEL), g_mi, g_mo], g_bd


def _mesh_place():
    x, y, c = lax.axis_index("x"), lax.axis_index("y"), lax.axis_index("c")
    other_chips = [(x, 1 - y), (1 - x, y), (1 - x, 1 - y)]
    return x, y, c, other_chips


ANY = pl.BlockSpec(memory_space=pl.ANY)


def _allgather_weights(shards):
    n = len(shards)

    def body(*refs):
        src, dst = refs[:n], refs[n:2 * n]
        local_sem, send_sem, recv_sem, fsend_sem, frecv_sem = refs[2 * n:]
        x, y, c, chips = _mesh_place()
        me = 2 * x + y

        def half(w, slot, cc):
            rh = shards[w].shape[0] // 2
            return dst[w].at[slot, pl.ds(cc * rh, rh), :]

        local = [pltpu.make_async_copy(src[w], dst[w].at[me], local_sem.at[w]) for w in range(n)]
        for cpy in local:
            cpy.start()
        sends, fwds = [], []
        for w in range(n):
            rh = shards[w].shape[0] // 2
            for r, (px, py) in enumerate(chips):
                k = w * 3 + r
                sends.append(pltpu.make_async_remote_copy(
                    src_ref=src[w].at[pl.ds(c * rh, rh), :], dst_ref=half(w, me, c),
                    send_sem=send_sem.at[k], recv_sem=recv_sem.at[k],
                    device_id=(px, py, c), device_id_type=MESH))
                sends[-1].start()
        for w in range(n):
            for r, (px, py) in enumerate(chips):
                k = w * 3 + r
                slot = 2 * px + py
                landed = half(w, slot, c)
                pltpu.make_async_remote_copy(
                    src_ref=landed, dst_ref=landed, send_sem=send_sem.at[k], recv_sem=recv_sem.at[k],
                    device_id=(px, py, c), device_id_type=MESH).wait_recv()
                fwds.append(pltpu.make_async_remote_copy(
                    src_ref=landed, dst_ref=landed, send_sem=fsend_sem.at[k], recv_sem=frecv_sem.at[k],
                    device_id=(x, y, 1 - c), device_id_type=MESH))
                fwds[-1].start()
        for w in range(n):
            for r, (px, py) in enumerate(chips):
                k = w * 3 + r
                theirs = half(w, 2 * px + py, 1 - c)
                pltpu.make_async_remote_copy(
                    src_ref=theirs, dst_ref=theirs, send_sem=fsend_sem.at[k], recv_sem=frecv_sem.at[k],
                    device_id=(x, y, 1 - c), device_id_type=MESH).wait_recv()
        for cpy in sends + fwds:
            cpy.wait_send()
        for cpy in local:
            cpy.wait()

    return pl.pallas_call(
        body, name="allgather_weights",
        in_specs=[ANY] * n, out_specs=[ANY] * n,
        out_shape=[SDS((N_CHIPS,) + s.shape, s.dtype) for s in shards],
        scratch_shapes=[pltpu.SemaphoreType.DMA((n,))] + [pltpu.SemaphoreType.DMA((3 * n,))] * 4,
    )(*shards)


def _pair_exchange(grads):
    n = len(grads)

    def body(*refs):
        src, dst = refs[:n], refs[n:2 * n]
        send_sem, recv_sem = refs[2 * n:]
        x, y, c, _ = _mesh_place()
        copies = []
        for w in range(n):
            rh = grads[w].shape[1] // 2
            copies.append(pltpu.make_async_remote_copy(
                src_ref=src[w].at[:, pl.ds((1 - c) * rh, rh), :], dst_ref=dst[w],
                send_sem=send_sem.at[w], recv_sem=recv_sem.at[w],
                device_id=(x, y, 1 - c), device_id_type=MESH))
            copies[-1].start()
        for cpy in copies:
            cpy.wait()

    return pl.pallas_call(
        body, name="grad_pair_exchange",
        in_specs=[ANY] * n, out_specs=[ANY] * n,
        out_shape=[SDS((N_CHIPS, g.shape[1] // 2, g.shape[2]), F32) for g in grads],
        scratch_shapes=[pltpu.SemaphoreType.DMA((n,)), pltpu.SemaphoreType.DMA((n,))],
    )(*grads)


def _pair_sum(place, grad, recv, name):
    _, R, C = grad.shape
    rh = R // 2
    br = _row_block(rh)
    nbh = rh // br

    def body(place_ref, g_ref, r_ref, own_ref, sums_ref):
        s = g_ref[...] + r_ref[...]

        @pl.when(pl.program_id(1) == 0)
        def _():
            own_ref[...] = s

        sums_ref[...] = s.astype(BF16)

    slot = lambda rel, pr: jnp.bitwise_xor(pr[0], rel)
    return pl.pallas_call(
        body, name=name,
        grid_spec=pltpu.PrefetchScalarGridSpec(
            num_scalar_prefetch=1, grid=(nbh, N_CHIPS),
            in_specs=[pl.BlockSpec((None, br, C), lambda i, rel, pr: (slot(rel, pr), pr[1] * nbh + i, 0)),
                      pl.BlockSpec((None, br, C), lambda i, rel, pr: (slot(rel, pr), i, 0))],
            out_specs=[pl.BlockSpec((br, C), lambda i, rel, pr: (i, 0)),
                       pl.BlockSpec((None, br, C), lambda i, rel, pr: (rel, i, 0))]),
        out_shape=[SDS((rh, C), F32), SDS((N_CHIPS, rh, C), BF16)],
        compiler_params=_cp("parallel", "arbitrary"),
    )(place, grad, recv)


def _chip_exchange(sums, small):
    n = len(sums)
    flips = [(fx, fy, fc) for fx in (0, 1) for fy in (0, 1) for fc in (0, 1)][1:]

    def body(*refs):
        src, small_ref = refs[:n], refs[n]
        dst, all_ref = refs[n + 1:2 * n + 1], refs[2 * n + 1]
        send_sem, recv_sem, ssend_sem, srecv_sem, local_sem = refs[2 * n + 2:]
        x, y, c, chips = _mesh_place()
        me = 4 * x + 2 * y + c
        local = pltpu.make_async_copy(small_ref, all_ref.at[me], local_sem)
        local.start()
        copies = []
        for w in range(n):
            for r, (px, py) in enumerate(chips):
                k = w * 3 + r
                copies.append(pltpu.make_async_remote_copy(
                    src_ref=src[w].at[r + 1], dst_ref=dst[w].at[r + 1],
                    send_sem=send_sem.at[k], recv_sem=recv_sem.at[k],
                    device_id=(px, py, c), device_id_type=MESH))
                copies[-1].start()
        flip = lambda v, f: 1 - v if f else v
        small_copies = []
        for k, (fx, fy, fc) in enumerate(flips):
            peer = (flip(x, fx), flip(y, fy), flip(c, fc))
            small_copies.append(pltpu.make_async_remote_copy(
                src_ref=small_ref, dst_ref=all_ref.at[me], send_sem=ssend_sem.at[k], recv_sem=srecv_sem.at[k],
                device_id=peer, device_id_type=MESH))
            small_copies[-1].start()
        for cpy in copies:
            cpy.wait()
        for k, (fx, fy, fc) in enumerate(flips):
            theirs = all_ref.at[4 * flip(x, fx) + 2 * flip(y, fy) + flip(c, fc)]
            pltpu.make_async_remote_copy(
                src_ref=small_ref, dst_ref=theirs, send_sem=ssend_sem.at[k], recv_sem=srecv_sem.at[k],
                device_id=(flip(x, fx), flip(y, fy), flip(c, fc)), device_id_type=MESH).wait_recv()
        for cpy in small_copies:
            cpy.wait_send()
        local.wait()

    return pl.pallas_call(
        body, name="grad_chip_exchange",
        in_specs=[ANY] * (n + 1), out_specs=[ANY] * (n + 1),
        out_shape=[SDS(s.shape, s.dtype) for s in sums] + [SDS((N_DEV,) + small.shape, F32)],
        scratch_shapes=[pltpu.SemaphoreType.DMA((3 * n,)), pltpu.SemaphoreType.DMA((3 * n,)),
                        pltpu.SemaphoreType.DMA((N_DEV - 1,)), pltpu.SemaphoreType.DMA((N_DEV - 1,)),
                        pltpu.SemaphoreType.DMA],
    )(*sums, small)


def _chip_sum(own, recv, name):
    rh, C = own.shape
    br = _row_block(rh)

    def body(own_ref, r_ref, o_ref):
        o_ref[...] = ((own_ref[...] + r_ref[1].astype(F32)) + r_ref[2].astype(F32)) + r_ref[3].astype(F32)

    return pl.pallas_call(
        body, grid=(rh // br,), name=name,
        in_specs=[pl.BlockSpec((br, C), lambda i: (i, 0)), pl.BlockSpec((N_CHIPS, br, C), lambda i: (0, i, 0))],
        out_specs=pl.BlockSpec((br, C), lambda i: (i, 0)),
        out_shape=SDS((rh, C), F32),
        compiler_params=_cp("parallel"),
    )(own, recv)


def _pair_allgather(halves):
    n = len(halves)

    def body(*refs):
        src, dst = refs[:n], refs[n:2 * n]
        local_sem, send_sem, recv_sem = refs[2 * n:]
        x, y, c, _ = _mesh_place()
        local, remote = [], []
        for w in range(n):
            rh = halves[w].shape[0]
            mine = dst[w].at[pl.ds(c * rh, rh), :]
            local.append(pltpu.make_async_copy(src[w], mine, local_sem.at[w]))
            local[-1].start()
            remote.append(pltpu.make_async_remote_copy(
                src_ref=src[w], dst_ref=mine, send_sem=send_sem.at[w], recv_sem=recv_sem.at[w],
                device_id=(x, y, 1 - c), device_id_type=MESH))
            remote[-1].start()
        for w in range(n):
            rh = halves[w].shape[0]
            theirs = dst[w].at[pl.ds((1 - c) * rh, rh), :]
            pltpu.make_async_remote_copy(
                src_ref=src[w], dst_ref=theirs, send_sem=send_sem.at[w], recv_sem=recv_sem.at[w],
                device_id=(x, y, 1 - c), device_id_type=MESH).wait_recv()
        for cpy in remote:
            cpy.wait_send()
        for cpy in local:
            cpy.wait()

    return pl.pallas_call(
        body, name="grad_pair_allgather",
        in_specs=[ANY] * n, out_specs=[ANY] * n,
        out_shape=[SDS((2 * h.shape[0], h.shape[1]), F32) for h in halves],
        scratch_shapes=[pltpu.SemaphoreType.DMA((n,))] * 3,
    )(*halves)


def _adamw_math(w, g, m, v):
    m = ADAM_B1 * m + (1.0 - ADAM_B1) * g
    v = ADAM_B2 * v + (1.0 - ADAM_B2) * jnp.square(g)
    m_hat = m / (1.0 - ADAM_B1 ** ADAM_STEP)
    v_hat = v / (1.0 - ADAM_B2 ** ADAM_STEP)
    delta = -ADAM_LR * (m_hat / (jnp.sqrt(v_hat) + ADAM_EPS) + ADAM_WD * w)
    return delta, m, v


def _adamw(w, g, m, v, name):
    R, C = w.shape
    br = _row_block(R)

    def body(w_ref, g_ref, m_ref, v_ref, d_ref, nm_ref, nv_ref):
        d_ref[...], nm_ref[...], nv_ref[...] = _adamw_math(w_ref[...], g_ref[...], m_ref[...], v_ref[...])

    spec = pl.BlockSpec((br, C), lambda i: (i, 0))
    return pl.pallas_call(
        body, grid=(R // br,), name=name, in_specs=[spec] * 4, out_specs=[spec] * 3,
        out_shape=[SDS((R, C), F32)] * 3, compiler_params=_cp("parallel"),
    )(w, g, m, v)


def _small_sum_adamw(all_small, w, m, v):
    loss_row = PACK_ROWS - 8

    def body(all_ref, w_ref, m_ref, v_ref, g_ref, d_ref, nm_ref, nv_ref, loss_ref):
        g = all_ref[0]
        for k in range(1, N_DEV):
            g = g + all_ref[k]
        g_ref[...] = g
        d_ref[...], nm_ref[...], nv_ref[...] = _adamw_math(w_ref[...], g, m_ref[...], v_ref[...])
        total = jnp.sum(g[loss_row:loss_row + 1, :]) * (0.5 / D_MODEL)
        loss_ref[...] = jnp.full(loss_ref.shape, total, F32)

    full = lambda s: pl.BlockSpec(s, lambda i: (0,) * len(s))
    pack = (PACK_ROWS, D_MODEL)
    return pl.pallas_call(
        body, grid=(1,), name="small_sum_adamw",
        in_specs=[full((N_DEV,) + pack), full(pack), full(pack), full(pack)],
        out_specs=[full(pack)] * 4 + [full((8, 128))],
        out_shape=[SDS(pack, F32)] * 4 + [SDS((8, 128), F32)],
        compiler_params=_cp("arbitrary"),
    )(all_small, w, m, v)


def _pack_small(grp, scale, g_mix, g_mlp, g_f, loss_lanes):
    def part(vec):
        vec = vec.reshape(1, -1)
        return jnp.pad(vec, ((0, 7), (0, D_MODEL - vec.shape[1])))
    return jnp.concatenate([grp.reshape(-1, D_MODEL), part(scale), part(g_mix), part(g_mlp), part(g_f),
                            part(loss_lanes)], axis=0)


def _unpack_small(pack):
    n_grp = len(POOL_WINDOWS) * POOL_GROUP_W * POOL_GROUP_W // D_MODEL
    grp = pack[:n_grp].reshape(1, len(POOL_WINDOWS), POOL_GROUP_W, POOL_GROUP_W)
    scale = pack[n_grp, :POOL_W].reshape(1, POOL_W)
    g_mix = pack[n_grp + 8].reshape(1, D_MODEL)
    g_mlp = pack[n_grp + 16].reshape(1, D_MODEL)
    g_f = pack[n_grp + 24].reshape(D_MODEL)
    return grp, scale, g_mix, g_mlp, g_f


def _block_diag(grp):
    out = jnp.zeros((POOL_W, POOL_W), grp.dtype)
    for k in range(len(POOL_WINDOWS)):
        out = lax.dynamic_update_slice(out, grp[k], (k * POOL_GROUP_W, k * POOL_GROUP_W))
    return out


def kernel(x, norm_mix_g, w_in, w_att_out, w_pool_grp, pool_scale, w_pool_out, w_out, norm_mlp_g, w_mlp_in, w_mlp_out, norm_final_g, loss_target, m_norm_mix_g, m_w_in, m_w_att_out, m_w_pool_grp, m_pool_scale, m_w_pool_out, m_w_out, m_norm_mlp_g, m_w_mlp_in, m_w_mlp_out, m_norm_final_g, v_norm_mix_g, v_w_in, v_w_att_out, v_w_pool_grp, v_pool_scale, v_w_pool_out, v_w_out, v_norm_mlp_g, v_w_mlp_in, v_w_mlp_out, v_norm_final_g):
    S = x.shape[1]
    xs, target = x[0], loss_target[0]
    big = [w_in[0], w_att_out[0], w_pool_out[0], w_out[0], w_mlp_in[0], w_mlp_out[0]]
    big_m = [m_w_in[0], m_w_att_out[0], m_w_pool_out[0], m_w_out[0], m_w_mlp_in[0], m_w_mlp_out[0]]
    big_v = [v_w_in[0], v_w_att_out[0], v_w_pool_out[0], v_w_out[0], v_w_mlp_in[0], v_w_mlp_out[0]]

    wg_in, wg_ao, wg_po, wg_out, wg_mi, wg_mo = _allgather_weights([w.astype(BF16) for w in big])
    wg_out = wg_out.reshape(D_MODEL, D_MODEL)
    wbd = _block_diag(w_pool_grp[0]).astype(BF16)
    g_final = norm_final_g.reshape(1, D_MODEL)
    lane = lax.broadcasted_iota(jnp.int32, (GROUP_W, GROUP_W), 0) // HEAD_W
    head_ones = (lane == lane.T).astype(BF16)

    u, qkv0, qkv1, qkv2, pz, gates = _norm_inproj(xs, norm_mix_g, wg_in)
    qkv = (qkv0, qkv1, qkv2)
    att = [_attn_fwd(qkv[grp], grp) for grp in range(3)]
    a, lt0, lt1, lt2, pooled, mixed, p, merged, h1, m = _mixer_out(
        [o for o, _ in att], [l for _, l in att], pz, gates, xs, wg_ao, wg_po, wbd, pool_scale, wg_out, norm_mlp_g)
    hid, dh2, dh2b, loss_lanes, dg_final = _mlp_fwd_loss(m, h1, target, wg_mi, wg_mo, g_final)

    dpre, dh1, dh1b, dg_mlp = _mlp_bwd(dh2, dh2b, hid, h1, wg_mi, wg_mo, norm_mlp_g)
    (da1, dp1, dgates, da0, dag1, dag2, dd0, dd1, dd2, dmixed, dqp, dscale) = _mixer_bwd(
        dh1b, a, p, mixed, gates, wg_out, wg_ao, wg_po, wbd, pool_scale, head_ones)
    dqkv = [_attn_bwd(qkv[grp], da_g, lt_g, dd_g, grp)
            for grp, (da_g, lt_g, dd_g) in enumerate(((da0, lt0, dd0), (dag1, lt1, dd1), (dag2, lt2, dd2)))]
    dz, dx, dg_mix = _inproj_bwd(dqkv, dqp, dgates, dh1, xs, norm_mix_g, wg_in)
    grads, g_bd = _weight_grads(S, u, dz, a, da1, p, dp1, merged, dh1b, m, dpre, hid, dh2b, pooled, dmixed)
    g_grp = jnp.stack([g_bd[k * POOL_GROUP_W:(k + 1) * POOL_GROUP_W, k * POOL_GROUP_W:(k + 1) * POOL_GROUP_W]
                       for k in range(len(POOL_WINDOWS))])
    small = _pack_small(g_grp, dscale, dg_mix, dg_mlp, dg_final, loss_lanes)

    place = jnp.stack([2 * lax.axis_index("x") + lax.axis_index("y"), lax.axis_index("c")]).astype(jnp.int32)
    names = ("w_in", "w_att_out", "w_pool_out", "w_out", "w_mlp_in", "w_mlp_out")
    recv = _pair_exchange(grads)
    pair = [_pair_sum(place, g, r, f"pair_sum_{nm}") for g, r, nm in zip(grads, recv, names)]
    exchanged = _chip_exchange([s for _, s in pair], small)
    halves = [_chip_sum(own, r, f"chip_sum_{nm}") for (own, _), r, nm in zip(pair, exchanged[:-1], names)]
    full = _pair_allgather(halves)

    zero = jnp.zeros((D_MODEL,), F32)
    small_w = _pack_small(w_pool_grp[0], pool_scale, norm_mix_g, norm_mlp_g, norm_final_g, zero)
    small_m = _pack_small(m_w_pool_grp[0], m_pool_scale, m_norm_mix_g, m_norm_mlp_g, m_norm_final_g, zero)
    small_v = _pack_small(v_w_pool_grp[0], v_pool_scale, v_norm_mix_g, v_norm_mlp_g, v_norm_final_g, zero)
    sg, sd, sm, sv, loss_tile = _small_sum_adamw(exchanged[-1], small_w, small_m, small_v)
    upd = [_adamw(w, g, mm, vv, f"adamw_{nm}") for w, g, mm, vv, nm in zip(big, full, big_m, big_v, names)]

    def ordered(small_pack, bigs):
        grp, scale, g_mix, g_mlp, g_f = _unpack_small(small_pack)
        b_in, b_ao, b_po, b_out, b_mi, b_mo = [b[None] for b in bigs]
        return (g_mix, b_in, b_ao, grp, scale, b_po, b_out, g_mlp, b_mi, b_mo, g_f)

    return (loss_tile[0, 0], dx[None],
            *ordered(sg, full),
            *ordered(sd, [t[0] for t in upd]),
            *ordered(sm, [t[1] for t in upd]),
            *ordered(sv, [t[2] for t in upd]))
```

```python
import functools

import jax
import jax.numpy as jnp
from jax import lax
from jax.experimental import pallas as pl
from jax.experimental.pallas import tpu as pltpu

F32 = jnp.float32
BF16 = jnp.bfloat16
SDS = jax.ShapeDtypeStruct
MESH = pl.DeviceIdType.MESH

D_MODEL = 1024
D_FF = 4096
N_CHIPS = 4
N_DEV = 8
DILATIONS = (1, 4, 16)
BAND = 128
GROUP_W = 256
PAIR_W = 128
HEAD_W = 64
POOL_W = 768
POOL_GROUP_W = 192
POOL_WINDOWS = (2, 4, 8, 16)
POOL_HALO = 16
N_IN = 5120
CHUNK = 256
N_CHUNKS = N_IN // CHUNK
CHUNKS_PER_SHARD = 5
NORM_EPS = 1e-6
ALIBI_MAX_BIAS = 8.0
N_HEADS = 12
NEG = -1e30

ADAM_LR, ADAM_B1, ADAM_B2, ADAM_EPS, ADAM_WD, ADAM_STEP = 0.001, 0.9, 0.999, 1e-08, 0.01, 10

TM = 512
TMB = 256
ATT_TILE = ((1, 4), (4, 1), (4, 1))
BK = 1024
VMEM_LIMIT = 56 * 1024 * 1024
PACK_ROWS = 184

NT = (((1,), (1,)), ((), ()))
TN = (((0,), (0,)), ((), ()))


def _cp(*sem):
    return pltpu.CompilerParams(dimension_semantics=sem, vmem_limit_bytes=VMEM_LIMIT)


def _resident(shape):
    nd = len(shape)
    return pl.BlockSpec(shape, lambda *_: (0,) * nd, pipeline_mode=pl.Buffered(1))


def _row_block(rows, cap=256):
    return max(b for b in range(16, min(rows, cap) + 1, 16) if rows % b == 0)


def _dot(a, b):
    return jnp.dot(a, b, preferred_element_type=F32)


def _dot_nt(a, b):
    return lax.dot_general(a, b, NT, preferred_element_type=F32)


def _dot_tn(a, b):
    return lax.dot_general(a, b, TN, preferred_element_type=F32)


def _w_in_chunk(w_ref, n):
    return w_ref[n // CHUNKS_PER_SHARD, :, (n % CHUNKS_PER_SHARD) * CHUNK:(n % CHUNKS_PER_SHARD + 1) * CHUNK]


def _rms_fwd(x, g):
    r = lax.rsqrt(jnp.mean(x * x, axis=-1, keepdims=True) + NORM_EPS)
    xh = x * r
    return xh * g, xh, r


def _rms_bwd(dy, xh, r, g):
    dxh = dy * g
    return r * (dxh - xh * jnp.mean(dxh * xh, axis=-1, keepdims=True))


def _deinterleave_store(val, s_ref, out_ref, lead, d, rows, dtype):
    if d == 1:
        out_ref[lead + (0,)] = val.astype(dtype)
        return
    for h in range(2):
        s_ref[h] = val[:, h * PAIR_W:(h + 1) * PAIR_W]
    for r in range(d):
        for h in range(2):
            out_ref[lead + (r, slice(None), slice(h * PAIR_W, (h + 1) * PAIR_W))] = (
                s_ref[h, pl.ds(r, rows // d, stride=d), :].astype(dtype))


def _interleave_load(in_ref, lead, s_ref, d, rows):
    for r in range(d):
        for h in range(2):
            s_ref[h, pl.ds(r, rows // d, stride=d), :] = (
                in_ref[lead + (r, slice(None), slice(h * PAIR_W, (h + 1) * PAIR_W))].astype(F32))


def _norm_inproj(x, g, w_in, shards, bufs):
    S = x.shape[0]
    n_tiles = S // TM
    n = len(shards)

    def body(*refs):
        x_ref, g_ref, w_ref = refs[0:3]
        shard_refs = refs[3:3 + n]
        u_ref, q0_ref, q1_ref, q2_ref, pz_ref, gate_ref = refs[3 + 2 * n:9 + 2 * n]
        buf_refs = refs[9 + 2 * n:9 + 3 * n]
        s_ref, send_sem, recv_sem = refs[9 + 3 * n:]
        i = pl.program_id(0)

        def copies():
            return _weight_half_copies(shard_refs, buf_refs, [sh.shape[0] for sh in shards], send_sem, recv_sem)

        @pl.when(i == 0)
        def _():
            for cpy in copies():
                cpy.start()

        u = _rms_fwd(x_ref[...], g_ref[...])[0].astype(BF16)
        u_ref[...] = u
        qkv_refs = (q0_ref, q1_ref, q2_ref)
        for k in range(N_CHUNKS):
            zc = _dot(u, _w_in_chunk(w_ref, k))
            if k < 9:
                which, grp = k // 3, k % 3
                if which == 0:
                    zc = zc * 0.125
                _deinterleave_store(zc, s_ref, qkv_refs[grp], (which,), DILATIONS[grp], TM, BF16)
            elif k < 12:
                pz_ref[:, (k - 9) * CHUNK:(k - 8) * CHUNK] = zc
            else:
                gate_ref[:, (k - 12) * CHUNK:(k - 11) * CHUNK] = zc

        @pl.when(i == n_tiles - 1)
        def _():
            for cpy in copies():
                cpy.wait()

    row = lambda w: pl.BlockSpec((TM, w), lambda i: (i, 0))
    res = pl.pallas_call(
        body, grid=(n_tiles,), name="norm_inproj",
        in_specs=[row(D_MODEL), _resident((1, D_MODEL)), _resident(w_in.shape)] + [ANY] * (2 * n),
        out_specs=[row(D_MODEL)]
        + [pl.BlockSpec((3, d, TM // d, GROUP_W), lambda i: (0, 0, i, 0)) for d in DILATIONS]
        + [row(POOL_W), row(2 * D_MODEL)] + [ANY] * n,
        out_shape=[SDS((S, D_MODEL), BF16)]
        + [SDS((3, d, S // d, GROUP_W), BF16) for d in DILATIONS]
        + [SDS((S, POOL_W), F32), SDS((S, 2 * D_MODEL), F32)] + [SDS(b.shape, b.dtype) for b in bufs],
        scratch_shapes=[pltpu.VMEM((2, TM, PAIR_W), F32), pltpu.SemaphoreType.DMA((3 * n,)),
                        pltpu.SemaphoreType.DMA((3 * n,))],
        input_output_aliases={3 + n + w: 6 + w for w in range(n)},
        compiler_params=_cp("arbitrary"),
    )(x, g, w_in, *shards, *bufs)
    return res[:6], res[6:]


def _band_bias(grp, d):
    row = lax.broadcasted_iota(jnp.int32, (BAND, 2 * BAND), 0)
    col = lax.broadcasted_iota(jnp.int32, (BAND, 2 * BAND), 1)
    steps = BAND + row - col
    valid = (steps >= 0) & (steps <= BAND)
    stepsf = (steps * d).astype(F32)
    biases = []
    for hh in range(4):
        slope = 2.0 ** (-ALIBI_MAX_BIAS * (grp * 4 + hh + 1) / N_HEADS)
        biases.append(jnp.where(valid, -slope * stepsf, NEG))
    return biases, col


def _attn_tiles(grp, L):
    rr, rb = ATT_TILE[grp]
    rb = min(rb, L // BAND)
    return rr, rb, L // (rb * BAND)


def _kv_tile(cur_ref, prev_ref, rr, rb, cs):
    if rb == 0:
        return jnp.concatenate([prev_ref[rr, :, cs], cur_ref[rr, 0:BAND, cs]], axis=0)
    return cur_ref[rr, (rb - 1) * BAND:(rb + 1) * BAND, cs]


def _attn_fwd(qkv, grp):
    d = DILATIONS[grp]
    L = qkv.shape[2]
    RR, RB, nb = _attn_tiles(grp, L)

    def body(q_ref, kc_ref, kp_ref, vc_ref, vp_ref, o_ref, lse_ref):
        i = pl.program_id(0)
        biases, col = _band_bias(grp, d)
        first_keys_ok = (col >= BAND) | (i > 0)
        is_a = lax.broadcasted_iota(jnp.int32, (BAND, PAIR_W), 1) < HEAD_W
        for rr in range(RR):
            for rb in range(RB):
                rows = slice(rb * BAND, (rb + 1) * BAND)
                for cp in range(2):
                    cs = slice(cp * PAIR_W, (cp + 1) * PAIR_W)
                    q2 = q_ref[rr, rows, cs]
                    kcat = _kv_tile(kc_ref, kp_ref, rr, rb, cs)
                    vcat = _kv_tile(vc_ref, vp_ref, rr, rb, cs)
                    res = []
                    for h2 in range(2):
                        sel = is_a if h2 == 0 else jnp.logical_not(is_a)
                        b = biases[cp * 2 + h2]
                        if rb == 0:
                            b = jnp.where(first_keys_ok, b, NEG)
                        s = _dot_nt(jnp.where(sel, q2, jnp.zeros_like(q2)), kcat) + b
                        m = jnp.max(s, axis=-1, keepdims=True)
                        p = jnp.exp(s - m)
                        l = jnp.sum(p, axis=-1, keepdims=True)
                        o = _dot(p.astype(BF16), vcat) * (1.0 / l)
                        res.append((o, m + jnp.log(l)))
                    o_ref[rr, rows, cs] = jnp.where(is_a, res[0][0], res[1][0])
                    lse_ref[rr, rows, cs] = jnp.where(is_a, res[0][1], res[1][1])

    cur = lambda w: pl.BlockSpec((None, RR, RB * BAND, GROUP_W), lambda i, j: (w, j, i, 0))
    prev = lambda w: pl.BlockSpec((None, RR, BAND, GROUP_W), lambda i, j: (w, j, jnp.maximum(i * RB - 1, 0), 0))
    out = pl.BlockSpec((RR, RB * BAND, GROUP_W), lambda i, j: (j, i, 0))
    return pl.pallas_call(
        body, grid=(nb, d // RR), name=f"attn_fwd_g{grp}",
        in_specs=[cur(0), cur(1), prev(1), cur(2), prev(2)],
        out_specs=[out, out],
        out_shape=[SDS((d, L, GROUP_W), F32), SDS((d, L, GROUP_W), F32)],
        compiler_params=_cp("parallel", "parallel"),
    )(qkv, qkv, qkv, qkv, qkv)


def _pool_column_select(col, vals):
    return jnp.where(col < POOL_GROUP_W, vals[0],
                     jnp.where(col < 2 * POOL_GROUP_W, vals[1],
                               jnp.where(col < 3 * POOL_GROUP_W, vals[2], vals[3])))


def _pool_inv_count(i, rows):
    t = i * rows + lax.broadcasted_iota(jnp.int32, (rows, POOL_W), 0)
    col = lax.broadcasted_iota(jnp.int32, (rows, POOL_W), 1)
    win = _pool_column_select(col, POOL_WINDOWS)
    return 1.0 / jnp.minimum(t + 1, win).astype(F32), col


def _mixer_out(outs, lses, pz, gates, x, w_ao, w_po, wbd, scale, w_out, g_mlp):
    S = x.shape[0]
    n_tiles = S // TMB

    def body(o0_ref, l0_ref, o1_ref, l1_ref, o2_ref, l2_ref, pz_ref, halo_ref, gate_ref, x_ref,
             wao_ref, wpo_ref, wbd_ref, sc_ref, wout_ref, g_ref,
             a_ref, lt0_ref, lt1_ref, lt2_ref, pooled_ref, mixed_ref, p_ref, merged_ref, h1_ref, m_ref,
             so1, sl1, so2, sl2, slt, ext_ref):
        i = pl.program_id(0)
        _interleave_load(o1_ref, (), so1, DILATIONS[1], TMB)
        _interleave_load(l1_ref, (), sl1, DILATIONS[1], TMB)
        _interleave_load(o2_ref, (), so2, DILATIONS[2], TMB)
        _interleave_load(l2_ref, (), sl2, DILATIONS[2], TMB)
        for h in range(2):
            hs = slice(h * PAIR_W, (h + 1) * PAIR_W)
            l0, l1, l2 = l0_ref[0, :, hs], sl1[h], sl2[h]
            mx = jnp.maximum(jnp.maximum(l0, l1), l2)
            e0, e1, e2 = jnp.exp(l0 - mx), jnp.exp(l1 - mx), jnp.exp(l2 - mx)
            den = e0 + e1 + e2
            a_ref[:, hs] = ((e0 * o0_ref[0, :, hs] + e1 * so1[h] + e2 * so2[h]) * (1.0 / den)).astype(BF16)
            slt[h] = mx + jnp.log(den)
        lt = jnp.concatenate([slt[0], slt[1]], axis=1)
        lt0_ref[0] = lt
        for ref, d in ((lt1_ref, DILATIONS[1]), (lt2_ref, DILATIONS[2])):
            for r in range(d):
                for h in range(2):
                    ref[r, :, h * PAIR_W:(h + 1) * PAIR_W] = slt[h, pl.ds(r, TMB // d, stride=d), :]

        pz_t = pz_ref[...]
        ext_ref[0:POOL_HALO, :] = jnp.where(i > 0, halo_ref[...], 0.0)
        ext_ref[POOL_HALO:, :] = pz_t
        sums = []
        acc = ext_ref[...]
        for k in (1, 2, 4, 8):
            acc = acc + pltpu.roll(acc, k, 0)
            sums.append(acc[POOL_HALO:, :])
        inv_cnt, col = _pool_inv_count(i, TMB)
        pooled = (_pool_column_select(col, sums) * inv_cnt - pz_t).astype(BF16)
        pooled_ref[...] = pooled
        mixed = _dot(pooled, wbd_ref[...])
        mixed_ref[...] = mixed
        p = (mixed * sc_ref[...]).astype(BF16)
        p_ref[...] = p

        a = a_ref[...]
        for j in range(N_CHIPS):
            js = slice(j * CHUNK, (j + 1) * CHUNK)
            ga = gate_ref[:, js]
            gp = gate_ref[:, D_MODEL + j * CHUNK:D_MODEL + (j + 1) * CHUNK]
            mj = jax.nn.sigmoid(ga) * _dot(a, wao_ref[j]) + jax.nn.sigmoid(gp) * _dot(p, wpo_ref[j])
            merged_ref[:, js] = mj.astype(BF16)
        h1 = x_ref[...] + _dot(merged_ref[...], wout_ref[...])
        h1_ref[...] = h1
        m_ref[...] = _rms_fwd(h1, g_ref[...])[0].astype(BF16)

    row = lambda w: pl.BlockSpec((TMB, w), lambda i: (i, 0))
    grp_spec = lambda d: pl.BlockSpec((d, TMB // d, GROUP_W), lambda i: (0, i, 0))
    halo = pl.BlockSpec((POOL_HALO, POOL_W), lambda i: (jnp.maximum(i * (TMB // POOL_HALO) - 1, 0), 0))
    d0, d1, d2 = DILATIONS
    return pl.pallas_call(
        body, grid=(n_tiles,), name="mixer_out",
        in_specs=[grp_spec(d0), grp_spec(d0), grp_spec(d1), grp_spec(d1), grp_spec(d2), grp_spec(d2),
                  row(POOL_W), halo, row(2 * D_MODEL), row(D_MODEL),
                  _resident(w_ao.shape), _resident(w_po.shape), _resident(wbd.shape), _resident(scale.shape),
                  _resident(w_out.shape), _resident(g_mlp.shape)],
        out_specs=[row(GROUP_W), grp_spec(d0), grp_spec(d1), grp_spec(d2),
                   row(POOL_W), row(POOL_W), row(POOL_W), row(D_MODEL), row(D_MODEL), row(D_MODEL)],
        out_shape=[SDS((S, GROUP_W), BF16)] + [SDS((d, S // d, GROUP_W), F32) for d in DILATIONS]
        + [SDS((S, POOL_W), BF16), SDS((S, POOL_W), F32), SDS((S, POOL_W), BF16),
           SDS((S, D_MODEL), BF16), SDS((S, D_MODEL), F32), SDS((S, D_MODEL), BF16)],
        scratch_shapes=[pltpu.VMEM((2, TMB, PAIR_W), F32) for _ in range(5)]
        + [pltpu.VMEM((TMB + POOL_HALO, POOL_W), F32)],
        compiler_params=_cp("parallel"),
    )(outs[0], lses[0], outs[1], lses[1], outs[2], lses[2], pz, pz, gates, x,
      w_ao, w_po, wbd, scale, w_out, g_mlp)


def _mlp_fwd_loss(m, h1, target, w_mi, w_mo, g_f):
    S = m.shape[0]

    def body(m_ref, h1_ref, t_ref, wmi_ref, wmo_ref, g_ref, hid_ref, dh2_ref, dh2b_ref, loss_ref, dg_ref):
        @pl.when(pl.program_id(0) == 0)
        def _():
            loss_ref[...] = jnp.zeros_like(loss_ref)
            dg_ref[...] = jnp.zeros_like(dg_ref)

        mt = m_ref[...]
        acc = h1_ref[...]
        for c in range(N_CHIPS):
            hid = jnp.square(jnp.maximum(_dot(mt, wmi_ref[c]), 0.0)).astype(BF16)
            hid_ref[:, c * D_MODEL:(c + 1) * D_MODEL] = hid
            acc = acc + _dot(hid, wmo_ref[c])
        g = g_ref[...]
        y, hh, r = _rms_fwd(acc, g)
        e = y - t_ref[...]
        loss_ref[...] += jnp.sum(e * e, axis=0, keepdims=True)
        dy = e * (1.0 / D_MODEL)
        dg_ref[...] += jnp.sum(dy * hh, axis=0, keepdims=True)
        dh2 = _rms_bwd(dy, hh, r, g)
        dh2_ref[...] = dh2
        dh2b_ref[...] = dh2.astype(BF16)

    row = lambda w: pl.BlockSpec((TM, w), lambda i: (i, 0))
    vec = pl.BlockSpec((1, D_MODEL), lambda i: (0, 0))
    return pl.pallas_call(
        body, grid=(S // TM,), name="mlp_fwd_loss",
        in_specs=[row(D_MODEL), row(D_MODEL), row(D_MODEL), _resident(w_mi.shape), _resident(w_mo.shape),
                  _resident(g_f.shape)],
        out_specs=[row(D_FF), row(D_MODEL), row(D_MODEL), vec, vec],
        out_shape=[SDS((S, D_FF), BF16), SDS((S, D_MODEL), F32), SDS((S, D_MODEL), BF16),
                   SDS((1, D_MODEL), F32), SDS((1, D_MODEL), F32)],
        compiler_params=_cp("arbitrary"),
    )(m, h1, target, w_mi, w_mo, g_f)


def _mlp_bwd(dh2, dh2b, hid, h1, w_mi, w_mo, g_mlp):
    S = dh2.shape[0]

    def body(dh2_ref, dh2b_ref, hid_ref, h1_ref, wmi_ref, wmo_ref, g_ref, dpre_ref, dh1_ref, dh1b_ref, dg_ref):
        @pl.when(pl.program_id(0) == 0)
        def _():
            dg_ref[...] = jnp.zeros_like(dg_ref)

        d2 = dh2b_ref[...]
        dm = jnp.zeros((TM, D_MODEL), F32)
        for c in range(N_CHIPS):
            cs = slice(c * D_MODEL, (c + 1) * D_MODEL)
            dhid = _dot_nt(d2, wmo_ref[c])
            dpre = (dhid * (2.0 * jnp.sqrt(hid_ref[:, cs].astype(F32)))).astype(BF16)
            dpre_ref[:, cs] = dpre
            dm = dm + _dot_nt(dpre, wmi_ref[c])
        g = g_ref[...]
        _, hh, r = _rms_fwd(h1_ref[...], g)
        dg_ref[...] += jnp.sum(dm * hh, axis=0, keepdims=True)
        dh1 = dh2_ref[...] + _rms_bwd(dm, hh, r, g)
        dh1_ref[...] = dh1
        dh1b_ref[...] = dh1.astype(BF16)

    row = lambda w: pl.BlockSpec((TM, w), lambda i: (i, 0))
    return pl.pallas_call(
        body, grid=(S // TM,), name="mlp_bwd",
        in_specs=[row(D_MODEL), row(D_MODEL), row(D_FF), row(D_MODEL), _resident(w_mi.shape),
                  _resident(w_mo.shape), _resident(g_mlp.shape)],
        out_specs=[row(D_FF), row(D_MODEL), row(D_MODEL), pl.BlockSpec((1, D_MODEL), lambda i: (0, 0))],
        out_shape=[SDS((S, D_FF), BF16), SDS((S, D_MODEL), F32), SDS((S, D_MODEL), BF16), SDS((1, D_MODEL), F32)],
        compiler_params=_cp("arbitrary"),
    )(dh2, dh2b, hid, h1, w_mi, w_mo, g_mlp)


def _mixer_bwd(dh1b, a, p, mixed, gates, w_out, w_ao, w_po, wbd, scale, head_ones, sums):
    S = a.shape[0]
    n_tiles = S // TMB
    n = len(sums)

    def body(*refs):
        (dh1b_ref, a_ref, p_ref, mixed_ref, gate_ref, wout_ref, wao_ref, wpo_ref, wbd_ref, sc_ref,
         ones_ref) = refs[0:11]
        sum_refs = refs[11:11 + n]
        (da1_ref, dp1_ref, dgate_ref, da0_ref, dag1_ref, dag2_ref, dd0_ref, dd1_ref, dd2_ref,
         dmixed_ref, dqp_ref, dscale_ref) = refs[11 + n:23 + n]
        land_refs = refs[23 + n:23 + 2 * n]
        s_da, s_dd, send_sem, recv_sem = refs[23 + 2 * n:]
        i = pl.program_id(0)

        @pl.when(i == 0)
        def _():
            dscale_ref[...] = jnp.zeros_like(dscale_ref)
            for cpy in _chip_sum_copies(sum_refs, land_refs, send_sem, recv_sem):
                cpy.start()

        dmerged = _dot_nt(dh1b_ref[...], wout_ref[...])
        a = a_ref[...]
        p = p_ref[...]
        da = jnp.zeros((TMB, GROUP_W), F32)
        dp = jnp.zeros((TMB, POOL_W), F32)
        for j in range(N_CHIPS):
            js = slice(j * CHUNK, (j + 1) * CHUNK)
            sa = jax.nn.sigmoid(gate_ref[:, js])
            sp = jax.nn.sigmoid(gate_ref[:, D_MODEL + j * CHUNK:D_MODEL + (j + 1) * CHUNK])
            dmj = dmerged[:, js]
            da1 = (dmj * sa).astype(BF16)
            dp1 = (dmj * sp).astype(BF16)
            da1_ref[:, js] = da1
            dp1_ref[:, js] = dp1
            dgate_ref[j] = (dmj * _dot(a, wao_ref[j]) * sa * (1.0 - sa)).astype(BF16)
            dgate_ref[N_CHIPS + j] = (dmj * _dot(p, wpo_ref[j]) * sp * (1.0 - sp)).astype(BF16)
            da = da + _dot_nt(da1, wao_ref[j])
            dp = dp + _dot_nt(dp1, wpo_ref[j])

        prod = da * a.astype(F32)
        hi = prod.astype(BF16)
        lo = (prod - hi.astype(F32)).astype(BF16)
        dd = _dot(hi, ones_ref[...]) + _dot(lo, ones_ref[...])
        for ref, val, sref, dtype in ((da0_ref, da, s_da, BF16), (dd0_ref, dd, s_dd, F32)):
            ref[0] = val.astype(dtype)
            for h in range(2):
                sref[h] = val[:, h * PAIR_W:(h + 1) * PAIR_W]
        for refs, d in (((dag1_ref, dd1_ref), DILATIONS[1]), ((dag2_ref, dd2_ref), DILATIONS[2])):
            for r in range(d):
                for h in range(2):
                    hs = slice(h * PAIR_W, (h + 1) * PAIR_W)
                    refs[0][r, :, hs] = s_da[h, pl.ds(r, TMB // d, stride=d), :].astype(BF16)
                    refs[1][r, :, hs] = s_dd[h, pl.ds(r, TMB // d, stride=d), :]

        sc = sc_ref[...]
        dscale_ref[...] += jnp.sum(dp * mixed_ref[...], axis=0, keepdims=True)
        dmixed = (dp * sc).astype(BF16)
        dmixed_ref[...] = dmixed
        inv_cnt, _ = _pool_inv_count(i, TMB)
        dqp_ref[...] = _dot_nt(dmixed, wbd_ref[...]) * inv_cnt

        @pl.when(i == n_tiles - 1)
        def _():
            for cpy in _chip_sum_copies(sum_refs, land_refs, send_sem, recv_sem):
                cpy.wait()

    row = lambda w: pl.BlockSpec((TMB, w), lambda i: (i, 0))
    grp_spec = lambda d: pl.BlockSpec((d, TMB // d, GROUP_W), lambda i: (0, i, 0))
    d0, d1, d2 = DILATIONS
    res = pl.pallas_call(
        body, grid=(n_tiles,), name="mixer_bwd",
        in_specs=[row(D_MODEL), row(GROUP_W), row(POOL_W), row(POOL_W), row(2 * D_MODEL),
                  _resident(w_out.shape), _resident(w_ao.shape), _resident(w_po.shape), _resident(wbd.shape),
                  _resident(scale.shape), _resident(head_ones.shape)] + [ANY] * n,
        out_specs=[row(D_MODEL), row(D_MODEL), pl.BlockSpec((2 * N_CHIPS, TMB, CHUNK), lambda i: (0, i, 0)),
                   grp_spec(d0), grp_spec(d1), grp_spec(d2), grp_spec(d0), grp_spec(d1), grp_spec(d2),
                   row(POOL_W), row(POOL_W), pl.BlockSpec((1, POOL_W), lambda i: (0, 0))] + [ANY] * n,
        out_shape=[SDS((S, D_MODEL), BF16), SDS((S, D_MODEL), BF16), SDS((2 * N_CHIPS, S, CHUNK), BF16)]
        + [SDS((d, S // d, GROUP_W), BF16) for d in DILATIONS]
        + [SDS((d, S // d, GROUP_W), F32) for d in DILATIONS]
        + [SDS((S, POOL_W), BF16), SDS((S, POOL_W), F32), SDS((1, POOL_W), F32)]
        + [SDS(t.shape, t.dtype) for t in sums],
        scratch_shapes=[pltpu.VMEM((2, TMB, PAIR_W), F32), pltpu.VMEM((2, TMB, PAIR_W), F32),
                        pltpu.SemaphoreType.DMA((3 * n,)), pltpu.SemaphoreType.DMA((3 * n,))],
        compiler_params=_cp("arbitrary"),
    )(dh1b, a, p, mixed, gates, w_out, w_ao, w_po, wbd, scale, head_ones, *sums)
    return res[:12], res[12:]


def _attn_bwd(qkv, da, lt, dd, grp):
    d = DILATIONS[grp]
    L = qkv.shape[2]
    RR, RB, nb = _attn_tiles(grp, L)

    def body(q_ref, kc_ref, kp_ref, vc_ref, vp_ref, da_ref, lt_ref, dd_ref, dq_ref, dk_ref, dv_ref, dk_acc, dv_acc):
        i = pl.program_id(1)
        cur = i % 2
        prv = 1 - cur

        @pl.when(i == 0)
        def _():
            dk_acc[...] = jnp.zeros_like(dk_acc)
            dv_acc[...] = jnp.zeros_like(dv_acc)

        @pl.when(i < nb)
        def _():
            dk_acc[cur] = jnp.zeros((RR, RB * BAND, GROUP_W), F32)
            dv_acc[cur] = jnp.zeros((RR, RB * BAND, GROUP_W), F32)
            biases, col = _band_bias(grp, d)
            first_keys_ok = (col >= BAND) | (i > 0)
            is_a = lax.broadcasted_iota(jnp.int32, (BAND, PAIR_W), 1) < HEAD_W
            is_a_kv = lax.broadcasted_iota(jnp.int32, (2 * BAND, PAIR_W), 1) < HEAD_W
            for rr in range(RR):
                for rb in range(RB):
                    rows = slice(rb * BAND, (rb + 1) * BAND)
                    for cp in range(2):
                        cs = slice(cp * PAIR_W, (cp + 1) * PAIR_W)
                        q2 = q_ref[rr, rows, cs]
                        da2 = da_ref[rr, rows, cs]
                        lt2 = lt_ref[rr, rows, cs]
                        dd2 = dd_ref[rr, rows, cs]
                        kcat = _kv_tile(kc_ref, kp_ref, rr, rb, cs)
                        vcat = _kv_tile(vc_ref, vp_ref, rr, rb, cs)
                        res = []
                        for h2 in range(2):
                            sel = is_a if h2 == 0 else jnp.logical_not(is_a)
                            lane0 = h2 * HEAD_W
                            b = biases[cp * 2 + h2]
                            if rb == 0:
                                b = jnp.where(first_keys_ok, b, NEG)
                            s = _dot_nt(jnp.where(sel, q2, jnp.zeros_like(q2)), kcat) + b
                            p = jnp.exp(s - lt2[:, lane0:lane0 + 1])
                            dpv = _dot_nt(jnp.where(sel, da2, jnp.zeros_like(da2)), vcat)
                            ds = (p * (dpv - dd2[:, lane0:lane0 + 1])).astype(BF16)
                            res.append((_dot(ds, kcat), _dot_tn(ds, q2), _dot_tn(p.astype(BF16), da2)))
                        dq_ref[rr, rows, cs] = (jnp.where(is_a, res[0][0], res[1][0]) * 0.125).astype(BF16)
                        dkc = jnp.where(is_a_kv, res[0][1], res[1][1])
                        dvc = jnp.where(is_a_kv, res[0][2], res[1][2])
                        if rb == 0:
                            last = slice((RB - 1) * BAND, RB * BAND)
                            dk_acc[prv, rr, last, cs] += dkc[0:BAND]
                            dv_acc[prv, rr, last, cs] += dvc[0:BAND]
                            dk_acc[cur, rr, 0:BAND, cs] += dkc[BAND:]
                            dv_acc[cur, rr, 0:BAND, cs] += dvc[BAND:]
                        else:
                            both = slice((rb - 1) * BAND, (rb + 1) * BAND)
                            dk_acc[cur, rr, both, cs] += dkc
                            dv_acc[cur, rr, both, cs] += dvc

        @pl.when(i > 0)
        def _():
            dk_ref[...] = dk_acc[prv].astype(BF16)
            dv_ref[...] = dv_acc[prv].astype(BF16)

    qi = lambda i: jnp.minimum(i, nb - 1)
    cur_w = lambda w: pl.BlockSpec((None, RR, RB * BAND, GROUP_W), lambda j, i: (w, j, qi(i), 0))
    prev_w = lambda w: pl.BlockSpec((None, RR, BAND, GROUP_W),
                                    lambda j, i: (w, j, jnp.maximum(qi(i) * RB - 1, 0), 0))
    blk = pl.BlockSpec((RR, RB * BAND, GROUP_W), lambda j, i: (j, qi(i), 0))
    late = pl.BlockSpec((RR, RB * BAND, GROUP_W), lambda j, i: (j, jnp.maximum(i - 1, 0), 0))
    return pl.pallas_call(
        body, grid=(d // RR, nb + 1), name=f"attn_bwd_g{grp}",
        in_specs=[cur_w(0), cur_w(1), prev_w(1), cur_w(2), prev_w(2), blk, blk, blk],
        out_specs=[blk, late, late],
        out_shape=[SDS((d, L, GROUP_W), BF16)] * 3,
        scratch_shapes=[pltpu.VMEM((2, RR, RB * BAND, GROUP_W), F32), pltpu.VMEM((2, RR, RB * BAND, GROUP_W), F32)],
        compiler_params=_cp("parallel", "arbitrary"),
    )(qkv, qkv, qkv, qkv, qkv, da, lt, dd)


def _dz_assemble(dqkv, dqp, dgates):
    S = dqp.shape[0]
    n_tiles = S // TMB

    def body(*refs):
        dqkv_refs = refs[0:9]
        dqp_ref, halo_ref, dgate_ref = refs[9:12]
        dz_ref, s_ref, ext_ref = refs[12:]
        i = pl.program_id(0)

        for grp in range(3):
            for which in range(3):
                n = which * 3 + grp
                ref = dqkv_refs[grp * 3 + which]
                if DILATIONS[grp] == 1:
                    dz_ref[n] = ref[0]
                else:
                    _interleave_load(ref, (), s_ref, DILATIONS[grp], TMB)
                    for h in range(2):
                        dz_ref[n, :, h * PAIR_W:(h + 1) * PAIR_W] = s_ref[h].astype(BF16)

        dqp = dqp_ref[...]
        ext_ref[0:TMB, :] = dqp
        ext_ref[TMB:, :] = jnp.where(i < n_tiles - 1, halo_ref[...], 0.0)
        sums = []
        acc = ext_ref[...]
        for k in (1, 2, 4, 8):
            acc = acc + pltpu.roll(acc, TMB + POOL_HALO - k, 0)
            sums.append(acc[0:TMB, :])
        inv_cnt, col = _pool_inv_count(i, TMB)
        dpz = _pool_column_select(col, sums) - dqp / inv_cnt
        for t in range(3):
            dz_ref[9 + t] = dpz[:, t * CHUNK:(t + 1) * CHUNK].astype(BF16)
        for t in range(2 * N_CHIPS):
            dz_ref[12 + t] = dgate_ref[t]

    row = lambda w: pl.BlockSpec((TMB, w), lambda i: (i, 0))
    grp_spec = lambda d: pl.BlockSpec((d, TMB // d, GROUP_W), lambda i: (0, i, 0))
    halo = pl.BlockSpec((POOL_HALO, POOL_W),
                        lambda i: (jnp.minimum((i + 1) * (TMB // POOL_HALO), S // POOL_HALO - 1), 0))
    flat = [t for grp in range(3) for t in dqkv[grp]]
    return pl.pallas_call(
        body, grid=(n_tiles,), name="dz_assemble",
        in_specs=[grp_spec(DILATIONS[grp]) for grp in range(3) for _ in range(3)]
        + [row(POOL_W), halo, pl.BlockSpec((2 * N_CHIPS, TMB, CHUNK), lambda i: (0, i, 0))],
        out_specs=pl.BlockSpec((N_CHUNKS, TMB, CHUNK), lambda i: (0, i, 0)),
        out_shape=SDS((N_CHUNKS, S, CHUNK), BF16),
        scratch_shapes=[pltpu.VMEM((2, TMB, PAIR_W), F32), pltpu.VMEM((TMB + POOL_HALO, POOL_W), F32)],
        compiler_params=_cp("parallel"),
    )(*flat, dqp, dqp, dgates)


def _inproj_dx(dz, dh1, x, g, w_in, sums):
    S = x.shape[0]
    n_tiles = S // TM
    n = len(sums)

    def body(*refs):
        dz_ref, dh1_ref, x_ref, g_ref, w_ref = refs[0:5]
        sum_refs = refs[5:5 + n]
        dx_ref, dg_ref = refs[5 + n:7 + n]
        land_refs = refs[7 + n:7 + 2 * n]
        send_sem, recv_sem = refs[7 + 2 * n:]
        i = pl.program_id(0)

        @pl.when(i == 0)
        def _():
            dg_ref[...] = jnp.zeros_like(dg_ref)
            for cpy in _chip_sum_copies(sum_refs, land_refs, send_sem, recv_sem):
                cpy.start()

        du = jnp.zeros((TM, D_MODEL), F32)
        for k in range(N_CHUNKS):
            du = du + _dot_nt(dz_ref[k], _w_in_chunk(w_ref, k))
        gv = g_ref[...]
        _, xh, r = _rms_fwd(x_ref[...], gv)
        dg_ref[...] += jnp.sum(du * xh, axis=0, keepdims=True)
        dx_ref[...] = dh1_ref[...] + _rms_bwd(du, xh, r, gv)

        @pl.when(i == n_tiles - 1)
        def _():
            for cpy in _chip_sum_copies(sum_refs, land_refs, send_sem, recv_sem):
                cpy.wait()

    row = lambda w: pl.BlockSpec((TM, w), lambda i: (i, 0))
    res = pl.pallas_call(
        body, grid=(n_tiles,), name="inproj_dx",
        in_specs=[pl.BlockSpec((N_CHUNKS, TM, CHUNK), lambda i: (0, i, 0)), row(D_MODEL), row(D_MODEL),
                  _resident(g.shape), _resident(w_in.shape)] + [ANY] * n,
        out_specs=[row(D_MODEL), pl.BlockSpec((1, D_MODEL), lambda i: (0, 0))] + [ANY] * n,
        out_shape=[SDS((S, D_MODEL), F32), SDS((1, D_MODEL), F32)] + [SDS(t.shape, t.dtype) for t in sums],
        scratch_shapes=[pltpu.SemaphoreType.DMA((3 * n,)), pltpu.SemaphoreType.DMA((3 * n,))],
        compiler_params=_cp("arbitrary"),
    )(dz, dh1, x, g, w_in, *sums)
    return res[0], res[1], res[2:]


def _wgrad(a, b, name, *, out_shape, a_spec, b_spec, out_spec, grid, n_out_cols=None):
    k_axis = len(grid) - 1

    def body(a_ref, b_ref, o_ref):
        @pl.when(pl.program_id(k_axis) == 0)
        def _():
            o_ref[...] = jnp.zeros_like(o_ref)

        at = a_ref[...]
        if n_out_cols is None:
            o_ref[...] += _dot_tn(at, b_ref[...])
        elif n_out_cols[0] == "lead_b":
            w = n_out_cols[1]
            for t in range(b_ref.shape[0]):
                o_ref[:, t * w:(t + 1) * w] += _dot_tn(at, b_ref[t])
        else:
            w = n_out_cols[1]
            for t in range(o_ref.shape[0]):
                o_ref[t] += _dot_tn(at, b_ref[:, t * w:(t + 1) * w])

    sem = ("parallel",) * k_axis + ("arbitrary",)
    return pl.pallas_call(body, grid=grid, name=name, in_specs=[a_spec, b_spec], out_specs=out_spec,
                          out_shape=out_shape, compiler_params=_cp(*sem))(a, b)


def _wgrad_in(u, dz):
    return _wgrad(
        u, dz, "wgrad_in", grid=(N_CHIPS, u.shape[0] // BK), n_out_cols=("lead_b", CHUNK),
        a_spec=pl.BlockSpec((BK, D_MODEL), lambda j, k: (k, 0)),
        b_spec=pl.BlockSpec((CHUNKS_PER_SHARD, BK, CHUNK), lambda j, k: (j, k, 0)),
        out_spec=pl.BlockSpec((None, D_MODEL, CHUNKS_PER_SHARD * CHUNK), lambda j, k: (j, 0, 0)),
        out_shape=SDS((N_CHIPS, D_MODEL, CHUNKS_PER_SHARD * CHUNK), F32))


def _wgrads_mixer(a, da1, p, dp1, merged, dh1b, pooled, dmixed):
    nk = a.shape[0] // BK
    g_ao = _wgrad(
        a, da1, "wgrad_att_out", grid=(nk,), n_out_cols=("cols_b", CHUNK),
        a_spec=pl.BlockSpec((BK, GROUP_W), lambda k: (k, 0)),
        b_spec=pl.BlockSpec((BK, D_MODEL), lambda k: (k, 0)),
        out_spec=pl.BlockSpec((N_CHIPS, GROUP_W, CHUNK), lambda k: (0, 0, 0)),
        out_shape=SDS((N_CHIPS, GROUP_W, CHUNK), F32))
    g_po = _wgrad(
        p, dp1, "wgrad_pool_out", grid=(nk,), n_out_cols=("cols_b", CHUNK),
        a_spec=pl.BlockSpec((BK, POOL_W), lambda k: (k, 0)),
        b_spec=pl.BlockSpec((BK, D_MODEL), lambda k: (k, 0)),
        out_spec=pl.BlockSpec((N_CHIPS, POOL_W, CHUNK), lambda k: (0, 0, 0)),
        out_shape=SDS((N_CHIPS, POOL_W, CHUNK), F32))
    g_out = _wgrad(
        merged, dh1b, "wgrad_out", grid=(nk,),
        a_spec=pl.BlockSpec((BK, D_MODEL), lambda k: (k, 0)),
        b_spec=pl.BlockSpec((BK, D_MODEL), lambda k: (k, 0)),
        out_spec=pl.BlockSpec((D_MODEL, D_MODEL), lambda k: (0, 0)),
        out_shape=SDS((D_MODEL, D_MODEL), F32))
    g_bd = _wgrad(
        pooled, dmixed, "wgrad_pool_grp", grid=(nk,),
        a_spec=pl.BlockSpec((BK, POOL_W), lambda k: (k, 0)),
        b_spec=pl.BlockSpec((BK, POOL_W), lambda k: (k, 0)),
        out_spec=pl.BlockSpec((POOL_W, POOL_W), lambda k: (0, 0)),
        out_shape=SDS((POOL_W, POOL_W), F32))
    return [g_ao, g_po, g_out.reshape(N_CHIPS, D_MODEL // N_CHIPS, D_MODEL)], g_bd


def _wgrads_mlp(m, dpre, hid, dh2b):
    nk = m.shape[0] // BK
    g_mi = _wgrad(
        m, dpre, "wgrad_mlp_in", grid=(N_CHIPS, nk),
        a_spec=pl.BlockSpec((BK, D_MODEL), lambda c, k: (k, 0)),
        b_spec=pl.BlockSpec((BK, D_MODEL), lambda c, k: (k, c)),
        out_spec=pl.BlockSpec((None, D_MODEL, D_MODEL), lambda c, k: (c, 0, 0)),
        out_shape=SDS((N_CHIPS, D_MODEL, D_MODEL), F32))
    g_mo = _wgrad(
        hid, dh2b, "wgrad_mlp_out", grid=(N_CHIPS, nk),
        a_spec=pl.BlockSpec((BK, D_MODEL), lambda c, k: (k, c)),
        b_spec=pl.BlockSpec((BK, D_MODEL), lambda c, k: (k, 0)),
        out_spec=pl.BlockSpec((None, D_MODEL, D_MODEL), lambda c, k: (c, 0, 0)),
        out_shape=SDS((N_CHIPS, D_MODEL, D_MODEL), F32))
    return [g_mi, g_mo]


def _mesh_place():
    x, y, c = lax.axis_index("x"), lax.axis_index("y"), lax.axis_index("c")
    other_chips = [(x, 1 - y), (1 - x, y), (1 - x, 1 - y)]
    return x, y, c, other_chips


ANY = pl.BlockSpec(memory_space=pl.ANY)


def _weight_half_copies(shard_refs, buf_refs, rows, send_sem, recv_sem):
    x, y, c, chips = _mesh_place()
    me = 2 * x + y
    copies = []
    for w, r_full in enumerate(rows):
        rh = r_full // 2
        for r, (px, py) in enumerate(chips):
            k = w * 3 + r
            copies.append(pltpu.make_async_remote_copy(
                src_ref=shard_refs[w].at[pl.ds(c * rh, rh), :], dst_ref=buf_refs[w].at[me, pl.ds(c * rh, rh), :],
                send_sem=send_sem.at[k], recv_sem=recv_sem.at[k], device_id=(px, py, c), device_id_type=MESH))
    return copies


def _pair_forward_copies(buf_refs, rows, send_sem, recv_sem):
    x, y, c, chips = _mesh_place()
    out = []
    for w, r_full in enumerate(rows):
        rh = r_full // 2
        for r, (px, py) in enumerate(chips):
            k = w * 3 + r
            landed = buf_refs[w].at[2 * px + py, pl.ds(c * rh, rh), :]
            theirs = buf_refs[w].at[2 * px + py, pl.ds((1 - c) * rh, rh), :]
            mk = lambda ref: pltpu.make_async_remote_copy(
                src_ref=ref, dst_ref=ref, send_sem=send_sem.at[k], recv_sem=recv_sem.at[k],
                device_id=(x, y, 1 - c), device_id_type=MESH)
            out.append((mk(landed), mk(theirs)))
    return out


def _place_own(block, n_slots, slot):
    buf = jnp.zeros((n_slots,) + block.shape, block.dtype)
    return lax.dynamic_update_slice(buf, block[None], (slot,) + (0,) * block.ndim)


def _allgather_weights(shards, bufs):
    n = len(shards)
    rows = [sh.shape[0] for sh in shards]

    def body(*refs):
        src, dst = refs[:n], refs[2 * n:3 * n]
        send_sem, recv_sem, fsend_sem, frecv_sem = refs[3 * n:]
        sends = _weight_half_copies(src, dst, rows, send_sem, recv_sem)
        for cpy in sends:
            cpy.start()
        fwds = _pair_forward_copies(dst, rows, fsend_sem, frecv_sem)
        for cpy, (fwd, _) in zip(sends, fwds):
            cpy.wait_recv()
            fwd.start()
        for _, landing in fwds:
            landing.wait_recv()
        for cpy in sends + [f for f, _ in fwds]:
            cpy.wait_send()

    return pl.pallas_call(
        body, name="allgather_w_in",
        in_specs=[ANY] * (2 * n), out_specs=[ANY] * n,
        out_shape=[SDS(b.shape, b.dtype) for b in bufs],
        scratch_shapes=[pltpu.SemaphoreType.DMA((3 * n,))] * 4,
        input_output_aliases={n + w: w for w in range(n)},
    )(*shards, *bufs)


def _pair_forward(bufs, rows):
    n = len(bufs)

    def body(*refs):
        dst = refs[n:2 * n]
        send_sem, recv_sem = refs[2 * n:]
        fwds = _pair_forward_copies(dst, rows, send_sem, recv_sem)
        for fwd, _ in fwds:
            fwd.start()
        for fwd, landing in fwds:
            landing.wait_recv()
            fwd.wait_send()

    return pl.pallas_call(
        body, name="weights_pair_forward",
        in_specs=[ANY] * n, out_specs=[ANY] * n,
        out_shape=[SDS(b.shape, b.dtype) for b in bufs],
        scratch_shapes=[pltpu.SemaphoreType.DMA((3 * n,))] * 2,
        input_output_aliases={w: w for w in range(n)},
    )(*bufs)


def _chip_sum_copies(src, dst, send_sem, recv_sem):
    x, y, c, chips = _mesh_place()
    copies = []
    for w in range(len(src)):
        for r, (px, py) in enumerate(chips):
            k = w * 3 + r
            copies.append(pltpu.make_async_remote_copy(
                src_ref=src[w].at[r + 1], dst_ref=dst[w].at[r + 1], send_sem=send_sem.at[k], recv_sem=recv_sem.at[k],
                device_id=(px, py, c), device_id_type=MESH))
    return copies


def _pair_exchange(grads):
    n = len(grads)

    def body(*refs):
        src, dst = refs[:n], refs[n:2 * n]
        send_sem, recv_sem = refs[2 * n:]
        x, y, c, _ = _mesh_place()
        copies = []
        for w in range(n):
            rh = grads[w].shape[1] // 2
            copies.append(pltpu.make_async_remote_copy(
                src_ref=src[w].at[:, pl.ds((1 - c) * rh, rh), :], dst_ref=dst[w],
                send_sem=send_sem.at[w], recv_sem=recv_sem.at[w],
                device_id=(x, y, 1 - c), device_id_type=MESH))
            copies[-1].start()
        for cpy in copies:
            cpy.wait()

    return pl.pallas_call(
        body, name="grad_pair_exchange",
        in_specs=[ANY] * n, out_specs=[ANY] * n,
        out_shape=[SDS((N_CHIPS, g.shape[1] // 2, g.shape[2]), F32) for g in grads],
        scratch_shapes=[pltpu.SemaphoreType.DMA((n,)), pltpu.SemaphoreType.DMA((n,))],
    )(*grads)


def _pair_sum(place, grad, recv, name):
    _, R, C = grad.shape
    rh = R // 2
    br = _row_block(rh)
    nbh = rh // br

    def body(place_ref, g_ref, r_ref, own_ref, sums_ref):
        s = g_ref[...] + r_ref[...]

        @pl.when(pl.program_id(1) == 0)
        def _():
            own_ref[...] = s

        sums_ref[...] = s.astype(BF16)

    slot = lambda rel, pr: jnp.bitwise_xor(pr[0], rel)
    return pl.pallas_call(
        body, name=name,
        grid_spec=pltpu.PrefetchScalarGridSpec(
            num_scalar_prefetch=1, grid=(nbh, N_CHIPS),
            in_specs=[pl.BlockSpec((None, br, C), lambda i, rel, pr: (slot(rel, pr), pr[1] * nbh + i, 0)),
                      pl.BlockSpec((None, br, C), lambda i, rel, pr: (slot(rel, pr), i, 0))],
            out_specs=[pl.BlockSpec((br, C), lambda i, rel, pr: (i, 0)),
                       pl.BlockSpec((None, br, C), lambda i, rel, pr: (rel, i, 0))]),
        out_shape=[SDS((rh, C), F32), SDS((N_CHIPS, rh, C), BF16)],
        compiler_params=_cp("parallel", "arbitrary"),
    )(place, grad, recv)


def _chip_sum(place, own, recv, name):
    rh, C = own.shape
    br = _row_block(rh)
    nbh = rh // br

    def body(place_ref, own_ref, r_ref, o_ref):
        o_ref[...] = ((own_ref[...] + r_ref[1].astype(F32)) + r_ref[2].astype(F32)) + r_ref[3].astype(F32)

    return pl.pallas_call(
        body, name=name,
        grid_spec=pltpu.PrefetchScalarGridSpec(
            num_scalar_prefetch=1, grid=(nbh,),
            in_specs=[pl.BlockSpec((br, C), lambda i, pr: (i, 0)),
                      pl.BlockSpec((N_CHIPS, br, C), lambda i, pr: (0, i, 0))],
            out_specs=pl.BlockSpec((br, C), lambda i, pr: (pr[1] * nbh + i, 0))),
        out_shape=SDS((2 * rh, C), F32),
        compiler_params=_cp("parallel"),
    )(place, own, recv)


def _finish_exchange(grads, small_all):
    n = len(grads)
    flips = [(fx, fy, fc) for fx in (0, 1) for fy in (0, 1) for fc in (0, 1)][1:]

    def body(*refs):
        dst, all_ref = refs[n + 1:2 * n + 1], refs[2 * n + 1]
        send_sem, recv_sem, ssend_sem, srecv_sem = refs[2 * n + 2:]
        x, y, c, _ = _mesh_place()
        me = 4 * x + 2 * y + c
        flip = lambda v, f: 1 - v if f else v
        pack_copies, pack_landings = [], []
        for k, (fx, fy, fc) in enumerate(flips):
            peer = (flip(x, fx), flip(y, fy), flip(c, fc))
            mk = lambda slot: pltpu.make_async_remote_copy(
                src_ref=all_ref.at[slot], dst_ref=all_ref.at[slot], send_sem=ssend_sem.at[k],
                recv_sem=srecv_sem.at[k], device_id=peer, device_id_type=MESH)
            pack_copies.append(mk(me))
            pack_landings.append(mk(4 * peer[0] + 2 * peer[1] + peer[2]))
            pack_copies[-1].start()
        sends, landings = [], []
        for w in range(n):
            rh = grads[w].shape[0] // 2
            mk = lambda cc: pltpu.make_async_remote_copy(
                src_ref=dst[w].at[pl.ds(cc * rh, rh), :], dst_ref=dst[w].at[pl.ds(cc * rh, rh), :],
                send_sem=send_sem.at[w], recv_sem=recv_sem.at[w], device_id=(x, y, 1 - c), device_id_type=MESH)
            sends.append(mk(c))
            landings.append(mk(1 - c))
            sends[-1].start()
        for cpy in landings + pack_landings:
            cpy.wait_recv()
        for cpy in sends + pack_copies:
            cpy.wait_send()

    res = pl.pallas_call(
        body, name="grad_finish_exchange",
        in_specs=[ANY] * (n + 1), out_specs=[ANY] * (n + 1),
        out_shape=[SDS(g.shape, g.dtype) for g in grads] + [SDS(small_all.shape, small_all.dtype)],
        scratch_shapes=[pltpu.SemaphoreType.DMA((n,)), pltpu.SemaphoreType.DMA((n,)),
                        pltpu.SemaphoreType.DMA((N_DEV - 1,)), pltpu.SemaphoreType.DMA((N_DEV - 1,))],
        input_output_aliases={w: w for w in range(n + 1)},
    )(*grads, small_all)
    return res[:n], res[n]


def _adamw_math(w, g, m, v):
    m = ADAM_B1 * m + (1.0 - ADAM_B1) * g
    v = ADAM_B2 * v + (1.0 - ADAM_B2) * jnp.square(g)
    m_hat = m / (1.0 - ADAM_B1 ** ADAM_STEP)
    v_hat = v / (1.0 - ADAM_B2 ** ADAM_STEP)
    delta = -ADAM_LR * (m_hat / (jnp.sqrt(v_hat) + ADAM_EPS) + ADAM_WD * w)
    return delta, m, v


def _adamw(w, g, m, v, name):
    R, C = w.shape
    br = _row_block(R)

    def body(w_ref, g_ref, m_ref, v_ref, d_ref, nm_ref, nv_ref):
        d_ref[...], nm_ref[...], nv_ref[...] = _adamw_math(w_ref[...], g_ref[...], m_ref[...], v_ref[...])

    spec = pl.BlockSpec((br, C), lambda i: (i, 0))
    return pl.pallas_call(
        body, grid=(R // br,), name=name, in_specs=[spec] * 4, out_specs=[spec] * 3,
        out_shape=[SDS((R, C), F32)] * 3, compiler_params=_cp("parallel"),
    )(w, g, m, v)


def _small_sum_adamw(all_small, w, m, v):
    loss_row = PACK_ROWS - 8

    def body(all_ref, w_ref, m_ref, v_ref, g_ref, d_ref, nm_ref, nv_ref, loss_ref):
        g = all_ref[0]
        for k in range(1, N_DEV):
            g = g + all_ref[k]
        g_ref[...] = g
        d_ref[...], nm_ref[...], nv_ref[...] = _adamw_math(w_ref[...], g, m_ref[...], v_ref[...])
        total = jnp.sum(g[loss_row:loss_row + 1, :]) * (0.5 / D_MODEL)
        loss_ref[...] = jnp.full(loss_ref.shape, total, F32)

    full = lambda s: pl.BlockSpec(s, lambda i: (0,) * len(s))
    pack = (PACK_ROWS, D_MODEL)
    return pl.pallas_call(
        body, grid=(1,), name="small_sum_adamw",
        in_specs=[full((N_DEV,) + pack), full(pack), full(pack), full(pack)],
        out_specs=[full(pack)] * 4 + [full((8, 128))],
        out_shape=[SDS(pack, F32)] * 4 + [SDS((8, 128), F32)],
        compiler_params=_cp("arbitrary"),
    )(all_small, w, m, v)


def _pack_small(grp, scale, g_mix, g_mlp, g_f, loss_lanes):
    def part(vec):
        vec = vec.reshape(1, -1)
        return jnp.pad(vec, ((0, 7), (0, D_MODEL - vec.shape[1])))
    return jnp.concatenate([grp.reshape(-1, D_MODEL), part(scale), part(g_mix), part(g_mlp), part(g_f),
                            part(loss_lanes)], axis=0)


def _unpack_small(pack):
    n_grp = len(POOL_WINDOWS) * POOL_GROUP_W * POOL_GROUP_W // D_MODEL
    grp = pack[:n_grp].reshape(1, len(POOL_WINDOWS), POOL_GROUP_W, POOL_GROUP_W)
    scale = pack[n_grp, :POOL_W].reshape(1, POOL_W)
    g_mix = pack[n_grp + 8].reshape(1, D_MODEL)
    g_mlp = pack[n_grp + 16].reshape(1, D_MODEL)
    g_f = pack[n_grp + 24].reshape(D_MODEL)
    return grp, scale, g_mix, g_mlp, g_f


def _block_diag(grp):
    out = jnp.zeros((POOL_W, POOL_W), grp.dtype)
    for k in range(len(POOL_WINDOWS)):
        out = lax.dynamic_update_slice(out, grp[k], (k * POOL_GROUP_W, k * POOL_GROUP_W))
    return out


def kernel(x, norm_mix_g, w_in, w_att_out, w_pool_grp, pool_scale, w_pool_out, w_out, norm_mlp_g, w_mlp_in, w_mlp_out, norm_final_g, loss_target, m_norm_mix_g, m_w_in, m_w_att_out, m_w_pool_grp, m_pool_scale, m_w_pool_out, m_w_out, m_norm_mlp_g, m_w_mlp_in, m_w_mlp_out, m_norm_final_g, v_norm_mix_g, v_w_in, v_w_att_out, v_w_pool_grp, v_pool_scale, v_w_pool_out, v_w_out, v_norm_mlp_g, v_w_mlp_in, v_w_mlp_out, v_norm_final_g):
    S = x.shape[1]
    xs, target = x[0], loss_target[0]
    big = [w_in[0], w_att_out[0], w_pool_out[0], w_out[0], w_mlp_in[0], w_mlp_out[0]]
    big_m = [m_w_in[0], m_w_att_out[0], m_w_pool_out[0], m_w_out[0], m_w_mlp_in[0], m_w_mlp_out[0]]
    big_v = [v_w_in[0], v_w_att_out[0], v_w_pool_out[0], v_w_out[0], v_w_mlp_in[0], v_w_mlp_out[0]]

    chip = 2 * lax.axis_index("x") + lax.axis_index("y")
    core = lax.axis_index("c")
    place = jnp.stack([chip, core]).astype(jnp.int32)
    names = ("w_in", "w_att_out", "w_pool_out", "w_out", "w_mlp_in", "w_mlp_out")

    shards = [w.astype(BF16) for w in big]
    bufs = [_place_own(sh, N_CHIPS, chip) for sh in shards]
    (wg_in,) = _allgather_weights(shards[:1], bufs[:1])
    wbd = _block_diag(w_pool_grp[0]).astype(BF16)
    g_final = norm_final_g.reshape(1, D_MODEL)
    lane = lax.broadcasted_iota(jnp.int32, (GROUP_W, GROUP_W), 0) // HEAD_W
    head_ones = (lane == lane.T).astype(BF16)

    (u, qkv0, qkv1, qkv2, pz, gates), landed = _norm_inproj(xs, norm_mix_g, wg_in, shards[1:], bufs[1:])
    wg_ao, wg_po, wg_out, wg_mi, wg_mo = _pair_forward(landed, [sh.shape[0] for sh in shards[1:]])
    wg_out = wg_out.reshape(D_MODEL, D_MODEL)
    qkv = (qkv0, qkv1, qkv2)
    att = [_attn_fwd(qkv[grp], grp) for grp in range(3)]
    a, lt0, lt1, lt2, pooled, mixed, p, merged, h1, m = _mixer_out(
        [o for o, _ in att], [l for _, l in att], pz, gates, xs, wg_ao, wg_po, wbd, pool_scale, wg_out, norm_mlp_g)
    hid, dh2, dh2b, loss_lanes, dg_final = _mlp_fwd_loss(m, h1, target, wg_mi, wg_mo, g_final)

    def pair_reduce(grads, grad_names):
        recv = _pair_exchange(grads)
        pair = [_pair_sum(place, g, r, f"pair_sum_{nm}") for g, r, nm in zip(grads, recv, grad_names)]
        return [own for own, _ in pair], [s for _, s in pair]

    def chip_reduce(owns, landed_sums, grad_names):
        return [_chip_sum(place, own, r, f"chip_sum_{nm}") for own, r, nm in zip(owns, landed_sums, grad_names)]

    dpre, dh1, dh1b, dg_mlp = _mlp_bwd(dh2, dh2b, hid, h1, wg_mi, wg_mo, norm_mlp_g)
    own_mlp, sums_mlp = pair_reduce(_wgrads_mlp(m, dpre, hid, dh2b), names[4:])
    (da1, dp1, dgates, da0, dag1, dag2, dd0, dd1, dd2, dmixed, dqp, dscale), landed_mlp = _mixer_bwd(
        dh1b, a, p, mixed, gates, wg_out, wg_ao, wg_po, wbd, pool_scale, head_ones, sums_mlp)
    g_mi, g_mo = chip_reduce(own_mlp, landed_mlp, names[4:])
    grads_mixer, g_bd = _wgrads_mixer(a, da1, p, dp1, merged, dh1b, pooled, dmixed)
    dqkv = [_attn_bwd(qkv[grp], da_g, lt_g, dd_g, grp)
            for grp, (da_g, lt_g, dd_g) in enumerate(((da0, lt0, dd0), (dag1, lt1, dd1), (dag2, lt2, dd2)))]
    dz = _dz_assemble(dqkv, dqp, dgates)
    own_in, sums_in = pair_reduce([_wgrad_in(u, dz)] + grads_mixer, names[:4])
    dx, dg_mix, landed_in = _inproj_dx(dz, dh1, xs, norm_mix_g, wg_in, sums_in)
    g_in, g_ao, g_po, g_out = chip_reduce(own_in, landed_in, names[:4])

    g_grp = jnp.stack([g_bd[k * POOL_GROUP_W:(k + 1) * POOL_GROUP_W, k * POOL_GROUP_W:(k + 1) * POOL_GROUP_W]
                       for k in range(len(POOL_WINDOWS))])
    small = _pack_small(g_grp, dscale, dg_mix, dg_mlp, dg_final, loss_lanes)
    full, small_all = _finish_exchange([g_in, g_ao, g_po, g_out, g_mi, g_mo],
                                       _place_own(small, N_DEV, 2 * chip + core))

    zero = jnp.zeros((D_MODEL,), F32)
    small_w = _pack_small(w_pool_grp[0], pool_scale, norm_mix_g, norm_mlp_g, norm_final_g, zero)
    small_m = _pack_small(m_w_pool_grp[0], m_pool_scale, m_norm_mix_g, m_norm_mlp_g, m_norm_final_g, zero)
    small_v = _pack_small(v_w_pool_grp[0], v_pool_scale, v_norm_mix_g, v_norm_mlp_g, v_norm_final_g, zero)
    sg, sd, sm, sv, loss_tile = _small_sum_adamw(small_all, small_w, small_m, small_v)
    upd = [_adamw(w, g, mm, vv, f"adamw_{nm}") for w, g, mm, vv, nm in zip(big, full, big_m, big_v, names)]

    def ordered(small_pack, bigs):
        grp, scale, g_mix, g_mlp, g_f = _unpack_small(small_pack)
        b_in, b_ao, b_po, b_out, b_mi, b_mo = [b[None] for b in bigs]
        return (g_mix, b_in, b_ao, grp, scale, b_po, b_out, g_mlp, b_mi, b_mo, g_f)

    return (loss_tile[0, 0], dx[None],
            *ordered(sg, full),
            *ordered(sd, [t[0] for t in upd]),
            *ordered(sm, [t[1] for t in upd]),
            *ordered(sv, [t[2] for t in upd]))
```

```python
import functools

import jax
import jax.numpy as jnp
from jax import lax
from jax.experimental import pallas as pl
from jax.experimental.pallas import tpu as pltpu

F32 = jnp.float32
BF16 = jnp.bfloat16
SDS = jax.ShapeDtypeStruct
MESH = pl.DeviceIdType.MESH

D_MODEL = 1024
D_FF = 4096
N_CHIPS = 4
N_DEV = 8
DILATIONS = (1, 4, 16)
BAND = 128
GROUP_W = 256
PAIR_W = 128
HEAD_W = 64
POOL_W = 768
POOL_GROUP_W = 192
POOL_WINDOWS = (2, 4, 8, 16)
POOL_HALO = 16
N_IN = 5120
CHUNK = 256
N_CHUNKS = N_IN // CHUNK
N_DZ_CHUNKS = 12
CHUNKS_PER_SHARD = 5
WGRAD_IN_GROUP = 4
NORM_EPS = 1e-6
ALIBI_MAX_BIAS = 8.0
N_HEADS = 12
NEG = -1e30

ADAM_LR, ADAM_B1, ADAM_B2, ADAM_EPS, ADAM_WD, ADAM_STEP = 0.001, 0.9, 0.999, 1e-08, 0.01, 10

TM = 512
TMB = 256
ATT_TILE = ((1, 4), (4, 1), (4, 1))
BK = 1024
VMEM_LIMIT = 56 * 1024 * 1024
PACK_ROWS = 184

NT = (((1,), (1,)), ((), ()))
TN = (((0,), (0,)), ((), ()))


def _cp(*sem):
    return pltpu.CompilerParams(dimension_semantics=sem, vmem_limit_bytes=VMEM_LIMIT)


def _resident(shape):
    nd = len(shape)
    return pl.BlockSpec(shape, lambda *_: (0,) * nd, pipeline_mode=pl.Buffered(1))


def _row_block(rows, cap=256):
    return max(b for b in range(16, min(rows, cap) + 1, 16) if rows % b == 0)


def _dot(a, b):
    return jnp.dot(a, b, preferred_element_type=F32)


def _dot_nt(a, b):
    return lax.dot_general(a, b, NT, preferred_element_type=F32)


def _dot_tn(a, b):
    return lax.dot_general(a, b, TN, preferred_element_type=F32)


def _w_in_chunk(w_ref, n):
    return w_ref[n // CHUNKS_PER_SHARD, :, (n % CHUNKS_PER_SHARD) * CHUNK:(n % CHUNKS_PER_SHARD + 1) * CHUNK]


def _sigmoid(x):
    return 0.5 * jnp.tanh(0.5 * x.astype(F32)) + 0.5


def _rms_fwd(x, g):
    r = lax.rsqrt(jnp.mean(x * x, axis=-1, keepdims=True) + NORM_EPS)
    xh = x * r
    return xh * g, xh, r


def _rms_bwd(dy, xh, r, g):
    dxh = dy * g
    return r * (dxh - xh * jnp.mean(dxh * xh, axis=-1, keepdims=True))


def _deinterleave_store(val, s_ref, out_ref, lead, d, rows, dtype):
    if d == 1:
        out_ref[lead + (0,)] = val.astype(dtype)
        return
    for h in range(2):
        s_ref[h] = val[:, h * PAIR_W:(h + 1) * PAIR_W]
    for r in range(d):
        for h in range(2):
            out_ref[lead + (r, slice(None), slice(h * PAIR_W, (h + 1) * PAIR_W))] = (
                s_ref[h, pl.ds(r, rows // d, stride=d), :].astype(dtype))


def _interleave_load(in_ref, lead, s_ref, d, rows):
    for r in range(d):
        for h in range(2):
            s_ref[h, pl.ds(r, rows // d, stride=d), :] = (
                in_ref[lead + (r, slice(None), slice(h * PAIR_W, (h + 1) * PAIR_W))].astype(F32))


def _norm_inproj(x, g, w_in, shards, bufs):
    S = x.shape[0]
    n_tiles = S // TM
    n = len(shards)

    def body(*refs):
        x_ref, g_ref, w_ref = refs[0:3]
        shard_refs = refs[3:3 + n]
        u_ref, q0_ref, q1_ref, q2_ref, pz_ref, gate_ref = refs[3 + 2 * n:9 + 2 * n]
        buf_refs = refs[9 + 2 * n:9 + 3 * n]
        s_ref, send_sem, recv_sem = refs[9 + 3 * n:]
        i = pl.program_id(0)

        def copies():
            return _weight_half_copies(shard_refs, buf_refs, [sh.shape[0] for sh in shards], send_sem, recv_sem)

        @pl.when(i == 0)
        def _():
            for cpy in copies():
                cpy.start()

        u = _rms_fwd(x_ref[...], g_ref[...])[0].astype(BF16)
        u_ref[...] = u
        qkv_refs = (q0_ref, q1_ref, q2_ref)
        for k in range(N_CHUNKS):
            zc = _dot(u, _w_in_chunk(w_ref, k))
            if k < 9:
                which, grp = k // 3, k % 3
                if which == 0:
                    zc = zc * 0.125
                _deinterleave_store(zc, s_ref, qkv_refs[grp], (which,), DILATIONS[grp], TM, BF16)
            elif k < 12:
                pz_ref[:, (k - 9) * CHUNK:(k - 8) * CHUNK] = zc
            else:
                gate_ref[:, (k - 12) * CHUNK:(k - 11) * CHUNK] = zc.astype(BF16)

        @pl.when(i == n_tiles - 1)
        def _():
            for cpy in copies():
                cpy.wait()

    row = lambda w: pl.BlockSpec((TM, w), lambda i: (i, 0))
    res = pl.pallas_call(
        body, grid=(n_tiles,), name="norm_inproj",
        in_specs=[row(D_MODEL), _resident((1, D_MODEL)), _resident(w_in.shape)] + [ANY] * (2 * n),
        out_specs=[row(D_MODEL)]
        + [pl.BlockSpec((3, d, TM // d, GROUP_W), lambda i: (0, 0, i, 0)) for d in DILATIONS]
        + [row(POOL_W), row(2 * D_MODEL)] + [ANY] * n,
        out_shape=[SDS((S, D_MODEL), BF16)]
        + [SDS((3, d, S // d, GROUP_W), BF16) for d in DILATIONS]
        + [SDS((S, POOL_W), F32), SDS((S, 2 * D_MODEL), BF16)] + [SDS(b.shape, b.dtype) for b in bufs],
        scratch_shapes=[pltpu.VMEM((2, TM, PAIR_W), F32), pltpu.SemaphoreType.DMA((3 * n,)),
                        pltpu.SemaphoreType.DMA((3 * n,))],
        input_output_aliases={3 + n + w: 6 + w for w in range(n)},
        compiler_params=_cp("arbitrary"),
    )(x, g, w_in, *shards, *bufs)
    return res[:6], res[6:]


def _band_bias(grp, d):
    row = lax.broadcasted_iota(jnp.int32, (BAND, 2 * BAND), 0)
    col = lax.broadcasted_iota(jnp.int32, (BAND, 2 * BAND), 1)
    steps = BAND + row - col
    valid = (steps >= 0) & (steps <= BAND)
    stepsf = (steps * d).astype(F32)
    biases = []
    for hh in range(4):
        slope = 2.0 ** (-ALIBI_MAX_BIAS * (grp * 4 + hh + 1) / N_HEADS)
        biases.append(jnp.where(valid, -slope * stepsf, NEG))
    return biases, col


def _attn_tiles(grp, L):
    rr, rb = ATT_TILE[grp]
    rb = min(rb, L // BAND)
    return rr, rb, L // (rb * BAND)


def _kv_tile(cur_ref, prev_ref, rr, rb, cs):
    if rb == 0:
        return jnp.concatenate([prev_ref[rr, :, cs], cur_ref[rr, 0:BAND, cs]], axis=0)
    return cur_ref[rr, (rb - 1) * BAND:(rb + 1) * BAND, cs]


def _attn_fwd(qkv, grp):
    d = DILATIONS[grp]
    L = qkv.shape[2]
    RR, RB, nb = _attn_tiles(grp, L)

    def body(q_ref, kc_ref, kp_ref, vc_ref, vp_ref, o_ref, lse_ref):
        i = pl.program_id(0)
        biases, col = _band_bias(grp, d)
        first_keys_ok = (col >= BAND) | (i > 0)
        is_a = lax.broadcasted_iota(jnp.int32, (BAND, PAIR_W), 1) < HEAD_W
        for rr in range(RR):
            for rb in range(RB):
                rows = slice(rb * BAND, (rb + 1) * BAND)
                for cp in range(2):
                    cs = slice(cp * PAIR_W, (cp + 1) * PAIR_W)
                    q2 = q_ref[rr, rows, cs]
                    kcat = _kv_tile(kc_ref, kp_ref, rr, rb, cs)
                    vcat = _kv_tile(vc_ref, vp_ref, rr, rb, cs)
                    res = []
                    for h2 in range(2):
                        sel = is_a if h2 == 0 else jnp.logical_not(is_a)
                        b = biases[cp * 2 + h2]
                        if rb == 0:
                            b = jnp.where(first_keys_ok, b, NEG)
                        s = _dot_nt(jnp.where(sel, q2, jnp.zeros_like(q2)), kcat) + b
                        m = jnp.max(s, axis=-1, keepdims=True)
                        p = jnp.exp(s - m)
                        l = jnp.sum(p, axis=-1, keepdims=True)
                        o = _dot(p.astype(BF16), vcat) * (1.0 / l)
                        res.append((o, m + jnp.log(l)))
                    o_ref[rr, rows, cs] = jnp.where(is_a, res[0][0], res[1][0]).astype(BF16)
                    lse_ref[rr, rows, cs] = jnp.where(is_a, res[0][1], res[1][1])

    cur = lambda w: pl.BlockSpec((None, RR, RB * BAND, GROUP_W), lambda i, j: (w, j, i, 0))
    prev = lambda w: pl.BlockSpec((None, RR, BAND, GROUP_W), lambda i, j: (w, j, jnp.maximum(i * RB - 1, 0), 0))
    out = pl.BlockSpec((RR, RB * BAND, GROUP_W), lambda i, j: (j, i, 0))
    return pl.pallas_call(
        body, grid=(nb, d // RR), name=f"attn_fwd_g{grp}",
        in_specs=[cur(0), cur(1), prev(1), cur(2), prev(2)],
        out_specs=[out, out],
        out_shape=[SDS((d, L, GROUP_W), BF16), SDS((d, L, GROUP_W), F32)],
        compiler_params=_cp("parallel", "parallel"),
    )(qkv, qkv, qkv, qkv, qkv)


def _pool_column_select(col, vals):
    return jnp.where(col < POOL_GROUP_W, vals[0],
                     jnp.where(col < 2 * POOL_GROUP_W, vals[1],
                               jnp.where(col < 3 * POOL_GROUP_W, vals[2], vals[3])))


def _pool_inv_count(i, rows):
    t = i * rows + lax.broadcasted_iota(jnp.int32, (rows, POOL_W), 0)
    col = lax.broadcasted_iota(jnp.int32, (rows, POOL_W), 1)
    win = _pool_column_select(col, POOL_WINDOWS)
    return 1.0 / jnp.minimum(t + 1, win).astype(F32), col


def _mixer_out(outs, lses, pz, gates, x, w_ao, w_po, wbd, scale, w_out, g_mlp):
    S = x.shape[0]
    n_tiles = S // TMB

    def body(o0_ref, l0_ref, o1_ref, l1_ref, o2_ref, l2_ref, pz_ref, halo_ref, gate_ref, x_ref,
             wao_ref, wpo_ref, wbd_ref, sc_ref, wout_ref, g_ref,
             a_ref, lt0_ref, lt1_ref, lt2_ref, pooled_ref, mixed_ref, p_ref, merged_ref, h1_ref, m_ref,
             so1, sl1, so2, sl2, slt, ext_ref):
        i = pl.program_id(0)
        _interleave_load(o1_ref, (), so1, DILATIONS[1], TMB)
        _interleave_load(l1_ref, (), sl1, DILATIONS[1], TMB)
        _interleave_load(o2_ref, (), so2, DILATIONS[2], TMB)
        _interleave_load(l2_ref, (), sl2, DILATIONS[2], TMB)
        for h in range(2):
            hs = slice(h * PAIR_W, (h + 1) * PAIR_W)
            l0, l1, l2 = l0_ref[0, :, hs], sl1[h], sl2[h]
            mx = jnp.maximum(jnp.maximum(l0, l1), l2)
            e0, e1, e2 = jnp.exp(l0 - mx), jnp.exp(l1 - mx), jnp.exp(l2 - mx)
            den = e0 + e1 + e2
            a_ref[:, hs] = ((e0 * o0_ref[0, :, hs].astype(F32) + e1 * so1[h] + e2 * so2[h])
                            * (1.0 / den)).astype(BF16)
            slt[h] = mx + jnp.log(den)
        lt = jnp.concatenate([slt[0], slt[1]], axis=1)
        lt0_ref[0] = lt
        for ref, d in ((lt1_ref, DILATIONS[1]), (lt2_ref, DILATIONS[2])):
            for r in range(d):
                for h in range(2):
                    ref[r, :, h * PAIR_W:(h + 1) * PAIR_W] = slt[h, pl.ds(r, TMB // d, stride=d), :]

        pz_t = pz_ref[...]
        ext_ref[0:POOL_HALO, :] = jnp.where(i > 0, halo_ref[...], 0.0)
        ext_ref[POOL_HALO:, :] = pz_t
        sums = []
        acc = ext_ref[...]
        for k in (1, 2, 4, 8):
            acc = acc + pltpu.roll(acc, k, 0)
            sums.append(acc[POOL_HALO:, :])
        inv_cnt, col = _pool_inv_count(i, TMB)
        pooled = (_pool_column_select(col, sums) * inv_cnt - pz_t).astype(BF16)
        pooled_ref[...] = pooled
        mixed = _dot(pooled, wbd_ref[...])
        mixed_ref[...] = mixed.astype(BF16)
        p = (mixed * sc_ref[...]).astype(BF16)
        p_ref[...] = p

        a = a_ref[...]
        for j in range(N_CHIPS):
            js = slice(j * CHUNK, (j + 1) * CHUNK)
            ga = gate_ref[:, js]
            gp = gate_ref[:, D_MODEL + j * CHUNK:D_MODEL + (j + 1) * CHUNK]
            mj = _sigmoid(ga) * _dot(a, wao_ref[j]) + _sigmoid(gp) * _dot(p, wpo_ref[j])
            merged_ref[:, js] = mj.astype(BF16)
        h1 = x_ref[...] + _dot(merged_ref[...], wout_ref[...])
        h1_ref[...] = h1
        m_ref[...] = _rms_fwd(h1, g_ref[...])[0].astype(BF16)

    row = lambda w: pl.BlockSpec((TMB, w), lambda i: (i, 0))
    grp_spec = lambda d: pl.BlockSpec((d, TMB // d, GROUP_W), lambda i: (0, i, 0))
    halo = pl.BlockSpec((POOL_HALO, POOL_W), lambda i: (jnp.maximum(i * (TMB // POOL_HALO) - 1, 0), 0))
    d0, d1, d2 = DILATIONS
    return pl.pallas_call(
        body, grid=(n_tiles,), name="mixer_out",
        in_specs=[grp_spec(d0), grp_spec(d0), grp_spec(d1), grp_spec(d1), grp_spec(d2), grp_spec(d2),
                  row(POOL_W), halo, row(2 * D_MODEL), row(D_MODEL),
                  _resident(w_ao.shape), _resident(w_po.shape), _resident(wbd.shape), _resident(scale.shape),
                  _resident(w_out.shape), _resident(g_mlp.shape)],
        out_specs=[row(GROUP_W), grp_spec(d0), grp_spec(d1), grp_spec(d2),
                   row(POOL_W), row(POOL_W), row(POOL_W), row(D_MODEL), row(D_MODEL), row(D_MODEL)],
        out_shape=[SDS((S, GROUP_W), BF16)] + [SDS((d, S // d, GROUP_W), F32) for d in DILATIONS]
        + [SDS((S, POOL_W), BF16), SDS((S, POOL_W), BF16), SDS((S, POOL_W), BF16),
           SDS((S, D_MODEL), BF16), SDS((S, D_MODEL), F32), SDS((S, D_MODEL), BF16)],
        scratch_shapes=[pltpu.VMEM((2, TMB, PAIR_W), F32) for _ in range(5)]
        + [pltpu.VMEM((TMB + POOL_HALO, POOL_W), F32)],
        compiler_params=_cp("parallel"),
    )(outs[0], lses[0], outs[1], lses[1], outs[2], lses[2], pz, pz, gates, x,
      w_ao, w_po, wbd, scale, w_out, g_mlp)


def _mlp_fwd_loss(m, h1, target, w_mi, w_mo, g_f):
    S = m.shape[0]

    def body(m_ref, h1_ref, t_ref, wmi_ref, wmo_ref, g_ref, hid_ref, dh2_ref, dh2b_ref, loss_ref, dg_ref):
        @pl.when(pl.program_id(0) == 0)
        def _():
            loss_ref[...] = jnp.zeros_like(loss_ref)
            dg_ref[...] = jnp.zeros_like(dg_ref)

        mt = m_ref[...]
        acc = h1_ref[...]
        for c in range(N_CHIPS):
            hid = jnp.square(jnp.maximum(_dot(mt, wmi_ref[c]), 0.0)).astype(BF16)
            hid_ref[:, c * D_MODEL:(c + 1) * D_MODEL] = hid
            acc = acc + _dot(hid, wmo_ref[c])
        g = g_ref[...]
        y, hh, r = _rms_fwd(acc, g)
        e = y - t_ref[...]
        loss_ref[...] += jnp.sum(e * e, axis=0, keepdims=True)
        dy = e * (1.0 / D_MODEL)
        dg_ref[...] += jnp.sum(dy * hh, axis=0, keepdims=True)
        dh2 = _rms_bwd(dy, hh, r, g)
        dh2_ref[...] = dh2
        dh2b_ref[...] = dh2.astype(BF16)

    row = lambda w: pl.BlockSpec((TM, w), lambda i: (i, 0))
    vec = pl.BlockSpec((1, D_MODEL), lambda i: (0, 0))
    return pl.pallas_call(
        body, grid=(S // TM,), name="mlp_fwd_loss",
        in_specs=[row(D_MODEL), row(D_MODEL), row(D_MODEL), _resident(w_mi.shape), _resident(w_mo.shape),
                  _resident(g_f.shape)],
        out_specs=[row(D_FF), row(D_MODEL), row(D_MODEL), vec, vec],
        out_shape=[SDS((S, D_FF), BF16), SDS((S, D_MODEL), F32), SDS((S, D_MODEL), BF16),
                   SDS((1, D_MODEL), F32), SDS((1, D_MODEL), F32)],
        compiler_params=_cp("arbitrary"),
    )(m, h1, target, w_mi, w_mo, g_f)


def _mlp_bwd(dh2, dh2b, hid, h1, w_mi, w_mo, g_mlp):
    S = dh2.shape[0]

    def body(dh2_ref, dh2b_ref, hid_ref, h1_ref, wmi_ref, wmo_ref, g_ref, dpre_ref, dh1_ref, dh1b_ref, dg_ref):
        @pl.when(pl.program_id(0) == 0)
        def _():
            dg_ref[...] = jnp.zeros_like(dg_ref)

        d2 = dh2b_ref[...]
        dm = jnp.zeros((TM, D_MODEL), F32)
        for c in range(N_CHIPS):
            cs = slice(c * D_MODEL, (c + 1) * D_MODEL)
            dhid = _dot_nt(d2, wmo_ref[c])
            dpre = (dhid * (2.0 * jnp.sqrt(hid_ref[:, cs].astype(F32)))).astype(BF16)
            dpre_ref[:, cs] = dpre
            dm = dm + _dot_nt(dpre, wmi_ref[c])
        g = g_ref[...]
        _, hh, r = _rms_fwd(h1_ref[...], g)
        dg_ref[...] += jnp.sum(dm * hh, axis=0, keepdims=True)
        dh1 = dh2_ref[...] + _rms_bwd(dm, hh, r, g)
        dh1_ref[...] = dh1
        dh1b_ref[...] = dh1.astype(BF16)

    row = lambda w: pl.BlockSpec((TM, w), lambda i: (i, 0))
    return pl.pallas_call(
        body, grid=(S // TM,), name="mlp_bwd",
        in_specs=[row(D_MODEL), row(D_MODEL), row(D_FF), row(D_MODEL), _resident(w_mi.shape),
                  _resident(w_mo.shape), _resident(g_mlp.shape)],
        out_specs=[row(D_FF), row(D_MODEL), row(D_MODEL), pl.BlockSpec((1, D_MODEL), lambda i: (0, 0))],
        out_shape=[SDS((S, D_FF), BF16), SDS((S, D_MODEL), F32), SDS((S, D_MODEL), BF16), SDS((1, D_MODEL), F32)],
        compiler_params=_cp("arbitrary"),
    )(dh2, dh2b, hid, h1, w_mi, w_mo, g_mlp)


def _mixer_bwd(dh1b, a, p, mixed, gates, w_out, w_ao, w_po, wbd, scale, head_ones, sums):
    S = a.shape[0]
    n_tiles = S // TMB
    n = len(sums)

    def body(*refs):
        (dh1b_ref, a_ref, p_ref, mixed_ref, gate_ref, wout_ref, wao_ref, wpo_ref, wbd_ref, sc_ref,
         ones_ref) = refs[0:11]
        sum_refs = refs[11:11 + n]
        (da1_ref, dp1_ref, dgate_ref, da0_ref, dag1_ref, dag2_ref, dd0_ref, dd1_ref, dd2_ref,
         dmixed_ref, dqp_ref, dscale_ref) = refs[11 + n:23 + n]
        land_refs = refs[23 + n:23 + 2 * n]
        s_da, s_dd, send_sem, recv_sem = refs[23 + 2 * n:]
        i = pl.program_id(0)

        @pl.when(i == 0)
        def _():
            dscale_ref[...] = jnp.zeros_like(dscale_ref)
            for cpy in _chip_sum_copies(sum_refs, land_refs, send_sem, recv_sem):
                cpy.start()

        dmerged = _dot_nt(dh1b_ref[...], wout_ref[...])
        a = a_ref[...]
        p = p_ref[...]
        da = jnp.zeros((TMB, GROUP_W), F32)
        dp = jnp.zeros((TMB, POOL_W), F32)
        for j in range(N_CHIPS):
            js = slice(j * CHUNK, (j + 1) * CHUNK)
            sa = _sigmoid(gate_ref[:, js])
            sp = _sigmoid(gate_ref[:, D_MODEL + j * CHUNK:D_MODEL + (j + 1) * CHUNK])
            dmj = dmerged[:, js]
            da1 = (dmj * sa).astype(BF16)
            dp1 = (dmj * sp).astype(BF16)
            da1_ref[:, js] = da1
            dp1_ref[:, js] = dp1
            dgate_ref[j] = (dmj * _dot(a, wao_ref[j]) * sa * (1.0 - sa)).astype(BF16)
            dgate_ref[N_CHIPS + j] = (dmj * _dot(p, wpo_ref[j]) * sp * (1.0 - sp)).astype(BF16)
            da = da + _dot_nt(da1, wao_ref[j])
            dp = dp + _dot_nt(dp1, wpo_ref[j])

        prod = da * a.astype(F32)
        hi = prod.astype(BF16)
        lo = (prod - hi.astype(F32)).astype(BF16)
        dd = _dot(hi, ones_ref[...]) + _dot(lo, ones_ref[...])
        for ref, val, sref, dtype in ((da0_ref, da, s_da, BF16), (dd0_ref, dd, s_dd, F32)):
            ref[0] = val.astype(dtype)
            for h in range(2):
                sref[h] = val[:, h * PAIR_W:(h + 1) * PAIR_W]
        for refs, d in (((dag1_ref, dd1_ref), DILATIONS[1]), ((dag2_ref, dd2_ref), DILATIONS[2])):
            for r in range(d):
                for h in range(2):
                    hs = slice(h * PAIR_W, (h + 1) * PAIR_W)
                    refs[0][r, :, hs] = s_da[h, pl.ds(r, TMB // d, stride=d), :].astype(BF16)
                    refs[1][r, :, hs] = s_dd[h, pl.ds(r, TMB // d, stride=d), :]

        sc = sc_ref[...]
        dscale_ref[...] += jnp.sum(dp * mixed_ref[...].astype(F32), axis=0, keepdims=True)
        dmixed = (dp * sc).astype(BF16)
        dmixed_ref[...] = dmixed
        inv_cnt, _ = _pool_inv_count(i, TMB)
        dqp_ref[...] = (_dot_nt(dmixed, wbd_ref[...]) * inv_cnt).astype(BF16)

        @pl.when(i == n_tiles - 1)
        def _():
            for cpy in _chip_sum_copies(sum_refs, land_refs, send_sem, recv_sem):
                cpy.wait()

    row = lambda w: pl.BlockSpec((TMB, w), lambda i: (i, 0))
    grp_spec = lambda d: pl.BlockSpec((d, TMB // d, GROUP_W), lambda i: (0, i, 0))
    d0, d1, d2 = DILATIONS
    res = pl.pallas_call(
        body, grid=(n_tiles,), name="mixer_bwd",
        in_specs=[row(D_MODEL), row(GROUP_W), row(POOL_W), row(POOL_W), row(2 * D_MODEL),
                  _resident(w_out.shape), _resident(w_ao.shape), _resident(w_po.shape), _resident(wbd.shape),
                  _resident(scale.shape), _resident(head_ones.shape)] + [ANY] * n,
        out_specs=[row(D_MODEL), row(D_MODEL), pl.BlockSpec((2 * N_CHIPS, TMB, CHUNK), lambda i: (0, i, 0)),
                   grp_spec(d0), grp_spec(d1), grp_spec(d2), grp_spec(d0), grp_spec(d1), grp_spec(d2),
                   row(POOL_W), row(POOL_W), pl.BlockSpec((1, POOL_W), lambda i: (0, 0))] + [ANY] * n,
        out_shape=[SDS((S, D_MODEL), BF16), SDS((S, D_MODEL), BF16), SDS((2 * N_CHIPS, S, CHUNK), BF16)]
        + [SDS((d, S // d, GROUP_W), BF16) for d in DILATIONS]
        + [SDS((d, S // d, GROUP_W), F32) for d in DILATIONS]
        + [SDS((S, POOL_W), BF16), SDS((S, POOL_W), BF16), SDS((1, POOL_W), F32)]
        + [SDS(t.shape, t.dtype) for t in sums],
        scratch_shapes=[pltpu.VMEM((2, TMB, PAIR_W), F32), pltpu.VMEM((2, TMB, PAIR_W), F32),
                        pltpu.SemaphoreType.DMA((3 * n,)), pltpu.SemaphoreType.DMA((3 * n,))],
        compiler_params=_cp("arbitrary"),
    )(dh1b, a, p, mixed, gates, w_out, w_ao, w_po, wbd, scale, head_ones, *sums)
    return res[:12], res[12:]


def _attn_bwd(qkv, da, lt, dd, grp):
    d = DILATIONS[grp]
    L = qkv.shape[2]
    RR, RB, nb = _attn_tiles(grp, L)

    def body(q_ref, kc_ref, kp_ref, vc_ref, vp_ref, da_ref, lt_ref, dd_ref, dq_ref, dk_ref, dv_ref, dk_acc, dv_acc):
        i = pl.program_id(1)
        cur = i % 2
        prv = 1 - cur

        @pl.when(i == 0)
        def _():
            dk_acc[...] = jnp.zeros_like(dk_acc)
            dv_acc[...] = jnp.zeros_like(dv_acc)

        @pl.when(i < nb)
        def _():
            dk_acc[cur] = jnp.zeros((RR, RB * BAND, GROUP_W), F32)
            dv_acc[cur] = jnp.zeros((RR, RB * BAND, GROUP_W), F32)
            biases, col = _band_bias(grp, d)
            first_keys_ok = (col >= BAND) | (i > 0)
            is_a = lax.broadcasted_iota(jnp.int32, (BAND, PAIR_W), 1) < HEAD_W
            is_a_kv = lax.broadcasted_iota(jnp.int32, (2 * BAND, PAIR_W), 1) < HEAD_W
            for rr in range(RR):
                for rb in range(RB):
                    rows = slice(rb * BAND, (rb + 1) * BAND)
                    for cp in range(2):
                        cs = slice(cp * PAIR_W, (cp + 1) * PAIR_W)
                        q2 = q_ref[rr, rows, cs]
                        da2 = da_ref[rr, rows, cs]
                        lt2 = lt_ref[rr, rows, cs]
                        dd2 = dd_ref[rr, rows, cs]
                        kcat = _kv_tile(kc_ref, kp_ref, rr, rb, cs)
                        vcat = _kv_tile(vc_ref, vp_ref, rr, rb, cs)
                        res = []
                        for h2 in range(2):
                            sel = is_a if h2 == 0 else jnp.logical_not(is_a)
                            lane0 = h2 * HEAD_W
                            b = biases[cp * 2 + h2]
                            if rb == 0:
                                b = jnp.where(first_keys_ok, b, NEG)
                            s = _dot_nt(jnp.where(sel, q2, jnp.zeros_like(q2)), kcat) + b
                            p = jnp.exp(s - lt2[:, lane0:lane0 + 1])
                            dpv = _dot_nt(jnp.where(sel, da2, jnp.zeros_like(da2)), vcat)
                            ds = (p * (dpv - dd2[:, lane0:lane0 + 1])).astype(BF16)
                            res.append((_dot(ds, kcat), _dot_tn(ds, q2), _dot_tn(p.astype(BF16), da2)))
                        dq_ref[rr, rows, cs] = (jnp.where(is_a, res[0][0], res[1][0]) * 0.125).astype(BF16)
                        dkc = jnp.where(is_a_kv, res[0][1], res[1][1])
                        dvc = jnp.where(is_a_kv, res[0][2], res[1][2])
                        if rb == 0:
                            last = slice((RB - 1) * BAND, RB * BAND)
                            dk_acc[prv, rr, last, cs] += dkc[0:BAND]
                            dv_acc[prv, rr, last, cs] += dvc[0:BAND]
                            dk_acc[cur, rr, 0:BAND, cs] += dkc[BAND:]
                            dv_acc[cur, rr, 0:BAND, cs] += dvc[BAND:]
                        else:
                            both = slice((rb - 1) * BAND, (rb + 1) * BAND)
                            dk_acc[cur, rr, both, cs] += dkc
                            dv_acc[cur, rr, both, cs] += dvc

        @pl.when(i > 0)
        def _():
            dk_ref[...] = dk_acc[prv].astype(BF16)
            dv_ref[...] = dv_acc[prv].astype(BF16)

    qi = lambda i: jnp.minimum(i, nb - 1)
    cur_w = lambda w: pl.BlockSpec((None, RR, RB * BAND, GROUP_W), lambda j, i: (w, j, qi(i), 0))
    prev_w = lambda w: pl.BlockSpec((None, RR, BAND, GROUP_W),
                                    lambda j, i: (w, j, jnp.maximum(qi(i) * RB - 1, 0), 0))
    blk = pl.BlockSpec((RR, RB * BAND, GROUP_W), lambda j, i: (j, qi(i), 0))
    late = pl.BlockSpec((RR, RB * BAND, GROUP_W), lambda j, i: (j, jnp.maximum(i - 1, 0), 0))
    return pl.pallas_call(
        body, grid=(d // RR, nb + 1), name=f"attn_bwd_g{grp}",
        in_specs=[cur_w(0), cur_w(1), prev_w(1), cur_w(2), prev_w(2), blk, blk, blk],
        out_specs=[blk, late, late],
        out_shape=[SDS((d, L, GROUP_W), BF16)] * 3,
        scratch_shapes=[pltpu.VMEM((2, RR, RB * BAND, GROUP_W), F32), pltpu.VMEM((2, RR, RB * BAND, GROUP_W), F32)],
        compiler_params=_cp("parallel", "arbitrary"),
    )(qkv, qkv, qkv, qkv, qkv, da, lt, dd)


def _dz_assemble(dqkv, dqp):
    S = dqp.shape[0]
    n_tiles = S // TMB

    def body(*refs):
        dqkv_refs = refs[0:9]
        dqp_ref, halo_ref = refs[9:11]
        dz_ref, s_ref, ext_ref = refs[11:]
        i = pl.program_id(0)

        for grp in range(3):
            for which in range(3):
                n = which * 3 + grp
                ref = dqkv_refs[grp * 3 + which]
                if DILATIONS[grp] == 1:
                    dz_ref[n] = ref[0]
                else:
                    _interleave_load(ref, (), s_ref, DILATIONS[grp], TMB)
                    for h in range(2):
                        dz_ref[n, :, h * PAIR_W:(h + 1) * PAIR_W] = s_ref[h].astype(BF16)

        dqp = dqp_ref[...].astype(F32)
        ext_ref[0:TMB, :] = dqp
        ext_ref[TMB:, :] = jnp.where(i < n_tiles - 1, halo_ref[...].astype(F32), 0.0)
        sums = []
        acc = ext_ref[...]
        for k in (1, 2, 4, 8):
            acc = acc + pltpu.roll(acc, TMB + POOL_HALO - k, 0)
            sums.append(acc[0:TMB, :])
        inv_cnt, col = _pool_inv_count(i, TMB)
        dpz = _pool_column_select(col, sums) - dqp / inv_cnt
        for t in range(3):
            dz_ref[9 + t] = dpz[:, t * CHUNK:(t + 1) * CHUNK].astype(BF16)

    row = lambda w: pl.BlockSpec((TMB, w), lambda i: (i, 0))
    grp_spec = lambda d: pl.BlockSpec((d, TMB // d, GROUP_W), lambda i: (0, i, 0))
    halo = pl.BlockSpec((POOL_HALO, POOL_W),
                        lambda i: (jnp.minimum((i + 1) * (TMB // POOL_HALO), S // POOL_HALO - 1), 0))
    flat = [t for grp in range(3) for t in dqkv[grp]]
    return pl.pallas_call(
        body, grid=(n_tiles,), name="dz_assemble",
        in_specs=[grp_spec(DILATIONS[grp]) for grp in range(3) for _ in range(3)] + [row(POOL_W), halo],
        out_specs=pl.BlockSpec((N_DZ_CHUNKS, TMB, CHUNK), lambda i: (0, i, 0)),
        out_shape=SDS((N_DZ_CHUNKS, S, CHUNK), BF16),
        scratch_shapes=[pltpu.VMEM((2, TMB, PAIR_W), F32), pltpu.VMEM((TMB + POOL_HALO, POOL_W), F32)],
        compiler_params=_cp("parallel"),
    )(*flat, dqp, dqp)


def _inproj_dx(dz, dgates, dh1, x, g, w_in, sums):
    S = x.shape[0]
    n_tiles = S // TM
    n = len(sums)

    def body(*refs):
        dz_ref, dgate_ref, dh1_ref, x_ref, g_ref, w_ref = refs[0:6]
        sum_refs = refs[6:6 + n]
        dx_ref, dg_ref = refs[6 + n:8 + n]
        land_refs = refs[8 + n:8 + 2 * n]
        send_sem, recv_sem = refs[8 + 2 * n:]
        i = pl.program_id(0)

        @pl.when(i == 0)
        def _():
            dg_ref[...] = jnp.zeros_like(dg_ref)
            for cpy in _chip_sum_copies(sum_refs, land_refs, send_sem, recv_sem):
                cpy.start()

        du = jnp.zeros((TM, D_MODEL), F32)
        for k in range(N_CHUNKS):
            dzk = dz_ref[k] if k < N_DZ_CHUNKS else dgate_ref[k - N_DZ_CHUNKS]
            du = du + _dot_nt(dzk, _w_in_chunk(w_ref, k))
        gv = g_ref[...]
        _, xh, r = _rms_fwd(x_ref[...], gv)
        dg_ref[...] += jnp.sum(du * xh, axis=0, keepdims=True)
        dx_ref[...] = dh1_ref[...] + _rms_bwd(du, xh, r, gv)

        @pl.when(i == n_tiles - 1)
        def _():
            for cpy in _chip_sum_copies(sum_refs, land_refs, send_sem, recv_sem):
                cpy.wait()

    row = lambda w: pl.BlockSpec((TM, w), lambda i: (i, 0))
    res = pl.pallas_call(
        body, grid=(n_tiles,), name="inproj_dx",
        in_specs=[pl.BlockSpec((N_DZ_CHUNKS, TM, CHUNK), lambda i: (0, i, 0)),
                  pl.BlockSpec((N_CHUNKS - N_DZ_CHUNKS, TM, CHUNK), lambda i: (0, i, 0)), row(D_MODEL), row(D_MODEL),
                  _resident(g.shape), _resident(w_in.shape)] + [ANY] * n,
        out_specs=[row(D_MODEL), pl.BlockSpec((1, D_MODEL), lambda i: (0, 0))] + [ANY] * n,
        out_shape=[SDS((S, D_MODEL), F32), SDS((1, D_MODEL), F32)] + [SDS(t.shape, t.dtype) for t in sums],
        scratch_shapes=[pltpu.SemaphoreType.DMA((3 * n,)), pltpu.SemaphoreType.DMA((3 * n,))],
        compiler_params=_cp("arbitrary"),
    )(dz, dgates, dh1, x, g, w_in, *sums)
    return res[0], res[1], res[2:]


def _wgrad(a, b, name, *, out_shape, a_spec, b_spec, out_spec, grid, n_out_cols=None, fill=None):
    k_axis = len(grid) - 1

    def body(a_ref, b_ref, *rest):
        o_ref = rest[-1]

        @pl.when(pl.program_id(k_axis) == 0)
        def _():
            o_ref[...] = jnp.zeros_like(o_ref)

        at = a_ref[...]
        if n_out_cols is None:
            o_ref[...] += _dot_tn(at, b_ref[...])
        elif n_out_cols[0] == "lead_both":
            for t in range(b_ref.shape[0]):
                o_ref[t] += _dot_tn(at, b_ref[t])
        else:
            w = n_out_cols[1]
            for t in range(o_ref.shape[0]):
                o_ref[t] += _dot_tn(at, b_ref[:, t * w:(t + 1) * w])

    sem = ("parallel",) * k_axis + ("arbitrary",)
    extra = [] if fill is None else [fill]
    return pl.pallas_call(body, grid=grid, name=name, in_specs=[a_spec, b_spec] + [ANY] * len(extra),
                          out_specs=out_spec, out_shape=out_shape,
                          input_output_aliases={} if fill is None else {2: 0},
                          compiler_params=_cp(*sem))(a, b, *extra)


def _wgrad_in(u, dz, dgates):
    nk = u.shape[0] // BK
    g = WGRAD_IN_GROUP
    kw = dict(n_out_cols=("lead_both", CHUNK), a_spec=pl.BlockSpec((BK, D_MODEL), lambda j, k: (k, 0)),
              b_spec=pl.BlockSpec((g, BK, CHUNK), lambda j, k: (j, k, 0)),
              out_shape=SDS((N_CHUNKS, D_MODEL, CHUNK), F32))
    first = _wgrad(u, dz, "wgrad_in_qkvp", grid=(N_DZ_CHUNKS // g, nk),
                   out_spec=pl.BlockSpec((g, D_MODEL, CHUNK), lambda j, k: (j, 0, 0)), **kw)
    return _wgrad(u, dgates, "wgrad_in_gates", grid=((N_CHUNKS - N_DZ_CHUNKS) // g, nk), fill=first,
                  out_spec=pl.BlockSpec((g, D_MODEL, CHUNK), lambda j, k: (N_DZ_CHUNKS // g + j, 0, 0)), **kw)


def _wgrads_mixer(a, da1, p, dp1, merged, dh1b, pooled, dmixed):
    nk = a.shape[0] // BK
    g_ao = _wgrad(
        a, da1, "wgrad_att_out", grid=(nk,), n_out_cols=("cols_b", CHUNK),
        a_spec=pl.BlockSpec((BK, GROUP_W), lambda k: (k, 0)),
        b_spec=pl.BlockSpec((BK, D_MODEL), lambda k: (k, 0)),
        out_spec=pl.BlockSpec((N_CHIPS, GROUP_W, CHUNK), lambda k: (0, 0, 0)),
        out_shape=SDS((N_CHIPS, GROUP_W, CHUNK), F32))
    g_po = _wgrad(
        p, dp1, "wgrad_pool_out", grid=(nk,), n_out_cols=("cols_b", CHUNK),
        a_spec=pl.BlockSpec((BK, POOL_W), lambda k: (k, 0)),
        b_spec=pl.BlockSpec((BK, D_MODEL), lambda k: (k, 0)),
        out_spec=pl.BlockSpec((N_CHIPS, POOL_W, CHUNK), lambda k: (0, 0, 0)),
        out_shape=SDS((N_CHIPS, POOL_W, CHUNK), F32))
    g_out = _wgrad(
        merged, dh1b, "wgrad_out", grid=(nk,),
        a_spec=pl.BlockSpec((BK, D_MODEL), lambda k: (k, 0)),
        b_spec=pl.BlockSpec((BK, D_MODEL), lambda k: (k, 0)),
        out_spec=pl.BlockSpec((D_MODEL, D_MODEL), lambda k: (0, 0)),
        out_shape=SDS((D_MODEL, D_MODEL), F32))
    g_bd = _wgrad(
        pooled, dmixed, "wgrad_pool_grp", grid=(nk,),
        a_spec=pl.BlockSpec((BK, POOL_W), lambda k: (k, 0)),
        b_spec=pl.BlockSpec((BK, POOL_W), lambda k: (k, 0)),
        out_spec=pl.BlockSpec((POOL_W, POOL_W), lambda k: (0, 0)),
        out_shape=SDS((POOL_W, POOL_W), F32))
    return [g_ao, g_po, g_out.reshape(N_CHIPS, D_MODEL // N_CHIPS, D_MODEL)], g_bd


def _wgrads_mlp(m, dpre, hid, dh2b):
    nk = m.shape[0] // BK
    g_mi = _wgrad(
        m, dpre, "wgrad_mlp_in", grid=(N_CHIPS, nk),
        a_spec=pl.BlockSpec((BK, D_MODEL), lambda c, k: (k, 0)),
        b_spec=pl.BlockSpec((BK, D_MODEL), lambda c, k: (k, c)),
        out_spec=pl.BlockSpec((None, D_MODEL, D_MODEL), lambda c, k: (c, 0, 0)),
        out_shape=SDS((N_CHIPS, D_MODEL, D_MODEL), F32))
    g_mo = _wgrad(
        hid, dh2b, "wgrad_mlp_out", grid=(N_CHIPS, nk),
        a_spec=pl.BlockSpec((BK, D_MODEL), lambda c, k: (k, c)),
        b_spec=pl.BlockSpec((BK, D_MODEL), lambda c, k: (k, 0)),
        out_spec=pl.BlockSpec((None, D_MODEL, D_MODEL), lambda c, k: (c, 0, 0)),
        out_shape=SDS((N_CHIPS, D_MODEL, D_MODEL), F32))
    return [g_mi, g_mo]


def _mesh_place():
    x, y, c = lax.axis_index("x"), lax.axis_index("y"), lax.axis_index("c")
    other_chips = [(x, 1 - y), (1 - x, y), (1 - x, 1 - y)]
    return x, y, c, other_chips


ANY = pl.BlockSpec(memory_space=pl.ANY)


def _weight_half_copies(shard_refs, buf_refs, rows, send_sem, recv_sem):
    x, y, c, chips = _mesh_place()
    me = 2 * x + y
    copies = []
    for w, r_full in enumerate(rows):
        rh = r_full // 2
        for r, (px, py) in enumerate(chips):
            k = w * 3 + r
            copies.append(pltpu.make_async_remote_copy(
                src_ref=shard_refs[w].at[pl.ds(c * rh, rh), :], dst_ref=buf_refs[w].at[me, pl.ds(c * rh, rh), :],
                send_sem=send_sem.at[k], recv_sem=recv_sem.at[k], device_id=(px, py, c), device_id_type=MESH))
    return copies


def _pair_forward_copies(buf_refs, rows, send_sem, recv_sem):
    x, y, c, chips = _mesh_place()
    out = []
    for w, r_full in enumerate(rows):
        rh = r_full // 2
        for r, (px, py) in enumerate(chips):
            k = w * 3 + r
            landed = buf_refs[w].at[2 * px + py, pl.ds(c * rh, rh), :]
            theirs = buf_refs[w].at[2 * px + py, pl.ds((1 - c) * rh, rh), :]
            mk = lambda ref: pltpu.make_async_remote_copy(
                src_ref=ref, dst_ref=ref, send_sem=send_sem.at[k], recv_sem=recv_sem.at[k],
                device_id=(x, y, 1 - c), device_id_type=MESH)
            out.append((mk(landed), mk(theirs)))
    return out


def _place_own(block, n_slots, slot):
    buf = jnp.zeros((n_slots,) + block.shape, block.dtype)
    return lax.dynamic_update_slice(buf, block[None], (slot,) + (0,) * block.ndim)


def _allgather_weights(shards, bufs):
    n = len(shards)
    rows = [sh.shape[0] for sh in shards]

    def body(*refs):
        src, dst = refs[:n], refs[2 * n:3 * n]
        send_sem, recv_sem, fsend_sem, frecv_sem = refs[3 * n:]
        sends = _weight_half_copies(src, dst, rows, send_sem, recv_sem)
        for cpy in sends:
            cpy.start()
        fwds = _pair_forward_copies(dst, rows, fsend_sem, frecv_sem)
        for cpy, (fwd, _) in zip(sends, fwds):
            cpy.wait_recv()
            fwd.start()
        for _, landing in fwds:
            landing.wait_recv()
        for cpy in sends + [f for f, _ in fwds]:
            cpy.wait_send()

    return pl.pallas_call(
        body, name="allgather_w_in",
        in_specs=[ANY] * (2 * n), out_specs=[ANY] * n,
        out_shape=[SDS(b.shape, b.dtype) for b in bufs],
        scratch_shapes=[pltpu.SemaphoreType.DMA((3 * n,))] * 4,
        input_output_aliases={n + w: w for w in range(n)},
    )(*shards, *bufs)


def _pair_forward(bufs, rows):
    n = len(bufs)

    def body(*refs):
        dst = refs[n:2 * n]
        send_sem, recv_sem = refs[2 * n:]
        fwds = _pair_forward_copies(dst, rows, send_sem, recv_sem)
        for fwd, _ in fwds:
            fwd.start()
        for fwd, landing in fwds:
            landing.wait_recv()
            fwd.wait_send()

    return pl.pallas_call(
        body, name="weights_pair_forward",
        in_specs=[ANY] * n, out_specs=[ANY] * n,
        out_shape=[SDS(b.shape, b.dtype) for b in bufs],
        scratch_shapes=[pltpu.SemaphoreType.DMA((3 * n,))] * 2,
        input_output_aliases={w: w for w in range(n)},
    )(*bufs)


def _chip_sum_copies(src, dst, send_sem, recv_sem):
    x, y, c, chips = _mesh_place()
    copies = []
    for w in range(len(src)):
        for r, (px, py) in enumerate(chips):
            k = w * 3 + r
            copies.append(pltpu.make_async_remote_copy(
                src_ref=src[w].at[r + 1], dst_ref=dst[w].at[r + 1], send_sem=send_sem.at[k], recv_sem=recv_sem.at[k],
                device_id=(px, py, c), device_id_type=MESH))
    return copies


def _pair_exchange(grads):
    n = len(grads)

    def body(*refs):
        src, dst = refs[:n], refs[n:2 * n]
        send_sem, recv_sem = refs[2 * n:]
        x, y, c, _ = _mesh_place()
        copies = []
        for w in range(n):
            rh = grads[w].shape[1] // 2
            copies.append(pltpu.make_async_remote_copy(
                src_ref=src[w].at[:, pl.ds((1 - c) * rh, rh), :], dst_ref=dst[w],
                send_sem=send_sem.at[w], recv_sem=recv_sem.at[w],
                device_id=(x, y, 1 - c), device_id_type=MESH))
            copies[-1].start()
        for cpy in copies:
            cpy.wait()

    return pl.pallas_call(
        body, name="grad_pair_exchange",
        in_specs=[ANY] * n, out_specs=[ANY] * n,
        out_shape=[SDS((N_CHIPS, g.shape[1] // 2, g.shape[2]), F32) for g in grads],
        scratch_shapes=[pltpu.SemaphoreType.DMA((n,)), pltpu.SemaphoreType.DMA((n,))],
    )(*grads)


def _pair_sum(place, grad, recv, name):
    _, R, C = grad.shape
    rh = R // 2
    br = _row_block(rh)
    nbh = rh // br

    def body(place_ref, g_ref, r_ref, own_ref, sums_ref):
        s = g_ref[...] + r_ref[...]

        @pl.when(pl.program_id(1) == 0)
        def _():
            own_ref[...] = s

        sums_ref[...] = s.astype(BF16)

    slot = lambda rel, pr: jnp.bitwise_xor(pr[0], rel)
    return pl.pallas_call(
        body, name=name,
        grid_spec=pltpu.PrefetchScalarGridSpec(
            num_scalar_prefetch=1, grid=(nbh, N_CHIPS),
            in_specs=[pl.BlockSpec((None, br, C), lambda i, rel, pr: (slot(rel, pr), pr[1] * nbh + i, 0)),
                      pl.BlockSpec((None, br, C), lambda i, rel, pr: (slot(rel, pr), i, 0))],
            out_specs=[pl.BlockSpec((br, C), lambda i, rel, pr: (i, 0)),
                       pl.BlockSpec((None, br, C), lambda i, rel, pr: (rel, i, 0))]),
        out_shape=[SDS((rh, C), F32), SDS((N_CHIPS, rh, C), BF16)],
        compiler_params=_cp("parallel", "arbitrary"),
    )(place, grad, recv)


def _chip_sum(place, own, recv, name):
    rh, C = own.shape
    br = _row_block(rh)
    nbh = rh // br

    def body(place_ref, own_ref, r_ref, o_ref):
        o_ref[...] = ((own_ref[...] + r_ref[1].astype(F32)) + r_ref[2].astype(F32)) + r_ref[3].astype(F32)

    return pl.pallas_call(
        body, name=name,
        grid_spec=pltpu.PrefetchScalarGridSpec(
            num_scalar_prefetch=1, grid=(nbh,),
            in_specs=[pl.BlockSpec((br, C), lambda i, pr: (i, 0)),
                      pl.BlockSpec((N_CHIPS, br, C), lambda i, pr: (0, i, 0))],
            out_specs=pl.BlockSpec((br, C), lambda i, pr: (pr[1] * nbh + i, 0))),
        out_shape=SDS((2 * rh, C), F32),
        compiler_params=_cp("parallel"),
    )(place, own, recv)


def _finish_exchange(grads, small_all):
    n = len(grads)

    def body(*refs):
        dst, all_ref = refs[n + 1:2 * n + 1], refs[2 * n + 1]
        send_sem, recv_sem, ssend_sem, srecv_sem = refs[2 * n + 2:]
        x, y, c, chips = _mesh_place()
        sib = (x, y, 1 - c)

        def pack(dev, k, to):
            slot = 4 * dev[0] + 2 * dev[1] + dev[2]
            return pltpu.make_async_remote_copy(
                src_ref=all_ref.at[slot], dst_ref=all_ref.at[slot], send_sem=ssend_sem.at[k],
                recv_sem=srecv_sem.at[k], device_id=to, device_id_type=MESH)

        pack_copies = [pack((x, y, c), 0, sib)] + [pack((x, y, c), 1 + r, (px, py, c))
                                                   for r, (px, py) in enumerate(chips)]
        for cpy in pack_copies:
            cpy.start()
        sends, landings = [], []
        for w in range(n):
            rh = grads[w].shape[0] // 2
            mk = lambda cc: pltpu.make_async_remote_copy(
                src_ref=dst[w].at[pl.ds(cc * rh, rh), :], dst_ref=dst[w].at[pl.ds(cc * rh, rh), :],
                send_sem=send_sem.at[w], recv_sem=recv_sem.at[w], device_id=(x, y, 1 - c), device_id_type=MESH)
            sends.append(mk(c))
            landings.append(mk(1 - c))
            sends[-1].start()
        for r, (px, py) in enumerate(chips):
            pack((px, py, c), 1 + r, (px, py, c)).wait_recv()
            pack_copies.append(pack((px, py, c), 4 + r, sib))
            pack_copies[-1].start()
        pack(sib, 0, sib).wait_recv()
        for r, (px, py) in enumerate(chips):
            pack((px, py, 1 - c), 4 + r, sib).wait_recv()
        for cpy in landings:
            cpy.wait_recv()
        for cpy in sends + pack_copies:
            cpy.wait_send()

    res = pl.pallas_call(
        body, name="grad_finish_exchange",
        in_specs=[ANY] * (n + 1), out_specs=[ANY] * (n + 1),
        out_shape=[SDS(g.shape, g.dtype) for g in grads] + [SDS(small_all.shape, small_all.dtype)],
        scratch_shapes=[pltpu.SemaphoreType.DMA((n,)), pltpu.SemaphoreType.DMA((n,)),
                        pltpu.SemaphoreType.DMA((N_DEV - 1,)), pltpu.SemaphoreType.DMA((N_DEV - 1,))],
        input_output_aliases={w: w for w in range(n + 1)},
    )(*grads, small_all)
    return res[:n], res[n]


def _adamw_math(w, g, m, v):
    m = ADAM_B1 * m + (1.0 - ADAM_B1) * g
    v = ADAM_B2 * v + (1.0 - ADAM_B2) * jnp.square(g)
    m_hat = m / (1.0 - ADAM_B1 ** ADAM_STEP)
    v_hat = v / (1.0 - ADAM_B2 ** ADAM_STEP)
    delta = -ADAM_LR * (m_hat / (jnp.sqrt(v_hat) + ADAM_EPS) + ADAM_WD * w)
    return delta, m, v


def _adamw(w, g, m, v, name):
    R, C = w.shape
    br = _row_block(R, 512)
    if g.ndim == 3:
        n_chunks, cw = g.shape[0], g.shape[2]
        g_spec = pl.BlockSpec((None, br, cw), lambda t, i: (t, i, 0))
    else:
        n_chunks, cw = 1, C
        g_spec = pl.BlockSpec((br, cw), lambda t, i: (i, t))

    def body(w_ref, g_ref, m_ref, v_ref, g_out_ref, d_ref, nm_ref, nv_ref):
        gv = g_ref[...]
        g_out_ref[...] = gv
        d_ref[...], nm_ref[...], nv_ref[...] = _adamw_math(w_ref[...], gv, m_ref[...], v_ref[...])

    spec = pl.BlockSpec((br, cw), lambda t, i: (i, t))
    return pl.pallas_call(
        body, grid=(n_chunks, R // br), name=name, in_specs=[spec, g_spec, spec, spec], out_specs=[spec] * 4,
        out_shape=[SDS((R, C), F32)] * 4, compiler_params=_cp("parallel", "parallel"),
    )(w, g, m, v)


def _small_sum_adamw(all_small, w, m, v):
    loss_row = PACK_ROWS - 8

    def body(all_ref, w_ref, m_ref, v_ref, g_ref, d_ref, nm_ref, nv_ref, loss_ref):
        g = all_ref[0]
        for k in range(1, N_DEV):
            g = g + all_ref[k]
        g_ref[...] = g
        d_ref[...], nm_ref[...], nv_ref[...] = _adamw_math(w_ref[...], g, m_ref[...], v_ref[...])
        total = jnp.sum(g[loss_row:loss_row + 1, :]) * (0.5 / D_MODEL)
        loss_ref[...] = jnp.full(loss_ref.shape, total, F32)

    full = lambda s: pl.BlockSpec(s, lambda i: (0,) * len(s))
    pack = (PACK_ROWS, D_MODEL)
    return pl.pallas_call(
        body, grid=(1,), name="small_sum_adamw",
        in_specs=[full((N_DEV,) + pack), full(pack), full(pack), full(pack)],
        out_specs=[full(pack)] * 4 + [full((8, 128))],
        out_shape=[SDS(pack, F32)] * 4 + [SDS((8, 128), F32)],
        compiler_params=_cp("arbitrary"),
    )(all_small, w, m, v)


def _pack_small(grp, scale, g_mix, g_mlp, g_f, loss_lanes):
    def part(vec):
        vec = vec.reshape(1, -1)
        return jnp.pad(vec, ((0, 7), (0, D_MODEL - vec.shape[1])))
    return jnp.concatenate([grp.reshape(-1, D_MODEL), part(scale), part(g_mix), part(g_mlp), part(g_f),
                            part(loss_lanes)], axis=0)


def _unpack_small(pack):
    n_grp = len(POOL_WINDOWS) * POOL_GROUP_W * POOL_GROUP_W // D_MODEL
    grp = pack[:n_grp].reshape(1, len(POOL_WINDOWS), POOL_GROUP_W, POOL_GROUP_W)
    scale = pack[n_grp, :POOL_W].reshape(1, POOL_W)
    g_mix = pack[n_grp + 8].reshape(1, D_MODEL)
    g_mlp = pack[n_grp + 16].reshape(1, D_MODEL)
    g_f = pack[n_grp + 24].reshape(D_MODEL)
    return grp, scale, g_mix, g_mlp, g_f


def _block_diag(grp):
    out = jnp.zeros((POOL_W, POOL_W), grp.dtype)
    for k in range(len(POOL_WINDOWS)):
        out = lax.dynamic_update_slice(out, grp[k], (k * POOL_GROUP_W, k * POOL_GROUP_W))
    return out


def kernel(x, norm_mix_g, w_in, w_att_out, w_pool_grp, pool_scale, w_pool_out, w_out, norm_mlp_g, w_mlp_in, w_mlp_out, norm_final_g, loss_target, m_norm_mix_g, m_w_in, m_w_att_out, m_w_pool_grp, m_pool_scale, m_w_pool_out, m_w_out, m_norm_mlp_g, m_w_mlp_in, m_w_mlp_out, m_norm_final_g, v_norm_mix_g, v_w_in, v_w_att_out, v_w_pool_grp, v_pool_scale, v_w_pool_out, v_w_out, v_norm_mlp_g, v_w_mlp_in, v_w_mlp_out, v_norm_final_g):
    S = x.shape[1]
    xs, target = x[0], loss_target[0]
    big = [w_in[0], w_att_out[0], w_pool_out[0], w_out[0], w_mlp_in[0], w_mlp_out[0]]
    big_m = [m_w_in[0], m_w_att_out[0], m_w_pool_out[0], m_w_out[0], m_w_mlp_in[0], m_w_mlp_out[0]]
    big_v = [v_w_in[0], v_w_att_out[0], v_w_pool_out[0], v_w_out[0], v_w_mlp_in[0], v_w_mlp_out[0]]

    chip = 2 * lax.axis_index("x") + lax.axis_index("y")
    core = lax.axis_index("c")
    place = jnp.stack([chip, core]).astype(jnp.int32)
    names = ("w_in", "w_att_out", "w_pool_out", "w_out", "w_mlp_in", "w_mlp_out")

    shards = [w.astype(BF16) for w in big]
    bufs = [_place_own(sh, N_CHIPS, chip) for sh in shards]
    (wg_in,) = _allgather_weights(shards[:1], bufs[:1])
    wbd = _block_diag(w_pool_grp[0]).astype(BF16)
    g_final = norm_final_g.reshape(1, D_MODEL)
    lane = lax.broadcasted_iota(jnp.int32, (GROUP_W, GROUP_W), 0) // HEAD_W
    head_ones = (lane == lane.T).astype(BF16)

    (u, qkv0, qkv1, qkv2, pz, gates), landed = _norm_inproj(xs, norm_mix_g, wg_in, shards[1:], bufs[1:])
    wg_ao, wg_po, wg_out, wg_mi, wg_mo = _pair_forward(landed, [sh.shape[0] for sh in shards[1:]])
    wg_out = wg_out.reshape(D_MODEL, D_MODEL)
    qkv = (qkv0, qkv1, qkv2)
    att = [_attn_fwd(qkv[grp], grp) for grp in range(3)]
    a, lt0, lt1, lt2, pooled, mixed, p, merged, h1, m = _mixer_out(
        [o for o, _ in att], [l for _, l in att], pz, gates, xs, wg_ao, wg_po, wbd, pool_scale, wg_out, norm_mlp_g)
    hid, dh2, dh2b, loss_lanes, dg_final = _mlp_fwd_loss(m, h1, target, wg_mi, wg_mo, g_final)

    def pair_reduce(grads, grad_names):
        recv = _pair_exchange(grads)
        pair = [_pair_sum(place, g, r, f"pair_sum_{nm}") for g, r, nm in zip(grads, recv, grad_names)]
        return [own for own, _ in pair], [s for _, s in pair]

    def chip_reduce(owns, landed_sums, grad_names):
        return [_chip_sum(place, own, r, f"chip_sum_{nm}") for own, r, nm in zip(owns, landed_sums, grad_names)]

    dpre, dh1, dh1b, dg_mlp = _mlp_bwd(dh2, dh2b, hid, h1, wg_mi, wg_mo, norm_mlp_g)
    own_mlp, sums_mlp = pair_reduce(_wgrads_mlp(m, dpre, hid, dh2b), names[4:])
    (da1, dp1, dgates, da0, dag1, dag2, dd0, dd1, dd2, dmixed, dqp, dscale), landed_mlp = _mixer_bwd(
        dh1b, a, p, mixed, gates, wg_out, wg_ao, wg_po, wbd, pool_scale, head_ones, sums_mlp)
    g_mi, g_mo = chip_reduce(own_mlp, landed_mlp, names[4:])
    grads_mixer, g_bd = _wgrads_mixer(a, da1, p, dp1, merged, dh1b, pooled, dmixed)
    dqkv = [_attn_bwd(qkv[grp], da_g, lt_g, dd_g, grp)
            for grp, (da_g, lt_g, dd_g) in enumerate(((da0, lt0, dd0), (dag1, lt1, dd1), (dag2, lt2, dd2)))]
    dz = _dz_assemble(dqkv, dqp)
    g_in_chunks = _wgrad_in(u, dz, dgates).reshape(N_CHIPS, CHUNKS_PER_SHARD * D_MODEL, CHUNK)
    own_in, sums_in = pair_reduce([g_in_chunks] + grads_mixer, names[:4])
    dx, dg_mix, landed_in = _inproj_dx(dz, dgates, dh1, xs, norm_mix_g, wg_in, sums_in)
    g_in, g_ao, g_po, g_out = chip_reduce(own_in, landed_in, names[:4])

    g_grp = jnp.stack([g_bd[k * POOL_GROUP_W:(k + 1) * POOL_GROUP_W, k * POOL_GROUP_W:(k + 1) * POOL_GROUP_W]
                       for k in range(len(POOL_WINDOWS))])
    small = _pack_small(g_grp, dscale, dg_mix, dg_mlp, dg_final, loss_lanes)
    full, small_all = _finish_exchange([g_in, g_ao, g_po, g_out, g_mi, g_mo],
                                       _place_own(small, N_DEV, 2 * chip + core))

    zero = jnp.zeros((D_MODEL,), F32)
    small_w = _pack_small(w_pool_grp[0], pool_scale, norm_mix_g, norm_mlp_g, norm_final_g, zero)
    small_m = _pack_small(m_w_pool_grp[0], m_pool_scale, m_norm_mix_g, m_norm_mlp_g, m_norm_final_g, zero)
    small_v = _pack_small(v_w_pool_grp[0], v_pool_scale, v_norm_mix_g, v_norm_mlp_g, v_norm_final_g, zero)
    sg, sd, sm, sv, loss_tile = _small_sum_adamw(small_all, small_w, small_m, small_v)
    full = [full[0].reshape(CHUNKS_PER_SHARD, D_MODEL, CHUNK)] + list(full[1:])
    upd = [_adamw(w, g, mm, vv, f"adamw_{nm}") for w, g, mm, vv, nm in zip(big, full, big_m, big_v, names)]

    def ordered(small_pack, bigs):
        grp, scale, g_mix, g_mlp, g_f = _unpack_small(small_pack)
        b_in, b_ao, b_po, b_out, b_mi, b_mo = [b[None] for b in bigs]
        return (g_mix, b_in, b_ao, grp, scale, b_po, b_out, g_mlp, b_mi, b_mo, g_f)

    return (loss_tile[0, 0], dx[None],
            *ordered(sg, [t[0] for t in upd]),
            *ordered(sd, [t[1] for t in upd]),
            *ordered(sm, [t[2] for t in upd]),
            *ordered(sv, [t[3] for t in upd]))
```

```python
import functools

import jax
import jax.numpy as jnp
from jax import lax
from jax.experimental import pallas as pl
from jax.experimental.pallas import tpu as pltpu

F32 = jnp.float32
BF16 = jnp.bfloat16
SDS = jax.ShapeDtypeStruct
MESH = pl.DeviceIdType.MESH

D_MODEL = 1024
D_FF = 4096
N_CHIPS = 4
N_DEV = 8
DILATIONS = (1, 4, 16)
BAND = 128
GROUP_W = 256
PAIR_W = 128
HEAD_W = 64
POOL_W = 768
POOL_GROUP_W = 192
POOL_WINDOWS = (2, 4, 8, 16)
POOL_HALO = 16
N_IN = 5120
CHUNK = 256
N_CHUNKS = N_IN // CHUNK
N_DZ_CHUNKS = 12
CHUNKS_PER_SHARD = 5
WGRAD_IN_GROUP = 4
NORM_EPS = 1e-6
ALIBI_MAX_BIAS = 8.0
N_HEADS = 12
NEG = -1e30

ADAM_LR, ADAM_B1, ADAM_B2, ADAM_EPS, ADAM_WD, ADAM_STEP = 0.001, 0.9, 0.999, 1e-08, 0.01, 10

TM = 512
TMB = 256
ATT_TILE = ((1, 4), (4, 1), (4, 1))
BK = 4096
ELEMENTWISE_BLOCK = 1 << 20
VMEM_LIMIT = 56 * 1024 * 1024
PACK_ROWS = 184

NT = (((1,), (1,)), ((), ()))
TN = (((0,), (0,)), ((), ()))


def _cp(*sem):
    return pltpu.CompilerParams(dimension_semantics=sem, vmem_limit_bytes=VMEM_LIMIT)


def _resident(shape):
    nd = len(shape)
    return pl.BlockSpec(shape, lambda *_: (0,) * nd, pipeline_mode=pl.Buffered(1))


def _row_block(rows, cap=256):
    return max(b for b in range(16, min(rows, cap) + 1, 16) if rows % b == 0)


def _dot(a, b):
    return jnp.dot(a, b, preferred_element_type=F32)


def _dot_nt(a, b):
    return lax.dot_general(a, b, NT, preferred_element_type=F32)


def _dot_tn(a, b):
    return lax.dot_general(a, b, TN, preferred_element_type=F32)


def _w_in_chunk(w_ref, n):
    return w_ref[n // CHUNKS_PER_SHARD, :, (n % CHUNKS_PER_SHARD) * CHUNK:(n % CHUNKS_PER_SHARD + 1) * CHUNK]


def _sigmoid(x):
    return 0.5 * jnp.tanh(0.5 * x.astype(F32)) + 0.5


def _rms_fwd(x, g):
    r = lax.rsqrt(jnp.mean(x * x, axis=-1, keepdims=True) + NORM_EPS)
    xh = x * r
    return xh * g, xh, r


def _rms_bwd(dy, xh, r, g):
    dxh = dy * g
    return r * (dxh - xh * jnp.mean(dxh * xh, axis=-1, keepdims=True))


def _deinterleave_store(val, s_ref, out_ref, lead, d, rows, dtype):
    if d == 1:
        out_ref[lead + (0,)] = val.astype(dtype)
        return
    for h in range(2):
        s_ref[h] = val[:, h * PAIR_W:(h + 1) * PAIR_W]
    for r in range(d):
        for h in range(2):
            out_ref[lead + (r, slice(None), slice(h * PAIR_W, (h + 1) * PAIR_W))] = (
                s_ref[h, pl.ds(r, rows // d, stride=d), :].astype(dtype))


def _interleave_load(in_ref, lead, s_ref, d, rows):
    for r in range(d):
        for h in range(2):
            s_ref[h, pl.ds(r, rows // d, stride=d), :] = (
                in_ref[lead + (r, slice(None), slice(h * PAIR_W, (h + 1) * PAIR_W))].astype(F32))


def _norm_inproj(x, g, w_in, shards, bufs):
    S = x.shape[0]
    n_tiles = S // TM
    n = len(shards)

    def body(*refs):
        x_ref, g_ref, w_ref = refs[0:3]
        shard_refs = refs[3:3 + n]
        u_ref, q0_ref, q1_ref, q2_ref, pz_ref, gate_ref = refs[3 + 2 * n:9 + 2 * n]
        buf_refs = refs[9 + 2 * n:9 + 3 * n]
        s_ref, send_sem, recv_sem = refs[9 + 3 * n:]
        i = pl.program_id(0)

        def copies():
            return _weight_half_copies(shard_refs, buf_refs, [sh.shape[0] for sh in shards], send_sem, recv_sem)

        @pl.when(i == 0)
        def _():
            for cpy in copies():
                cpy.start()

        u = _rms_fwd(x_ref[...], g_ref[...])[0].astype(BF16)
        u_ref[...] = u
        qkv_refs = (q0_ref, q1_ref, q2_ref)
        for k in range(N_CHUNKS):
            zc = _dot(u, _w_in_chunk(w_ref, k))
            if k < 9:
                which, grp = k // 3, k % 3
                if which == 0:
                    zc = zc * 0.125
                _deinterleave_store(zc, s_ref, qkv_refs[grp], (which,), DILATIONS[grp], TM, BF16)
            elif k < 12:
                pz_ref[:, (k - 9) * CHUNK:(k - 8) * CHUNK] = zc
            else:
                gate_ref[:, (k - 12) * CHUNK:(k - 11) * CHUNK] = zc.astype(BF16)

        @pl.when(i == n_tiles - 1)
        def _():
            for cpy in copies():
                cpy.wait()

    row = lambda w: pl.BlockSpec((TM, w), lambda i: (i, 0))
    res = pl.pallas_call(
        body, grid=(n_tiles,), name="norm_inproj",
        in_specs=[row(D_MODEL), _resident((1, D_MODEL)), _resident(w_in.shape)] + [ANY] * (2 * n),
        out_specs=[row(D_MODEL)]
        + [pl.BlockSpec((3, d, TM // d, GROUP_W), lambda i: (0, 0, i, 0)) for d in DILATIONS]
        + [row(POOL_W), row(2 * D_MODEL)] + [ANY] * n,
        out_shape=[SDS((S, D_MODEL), BF16)]
        + [SDS((3, d, S // d, GROUP_W), BF16) for d in DILATIONS]
        + [SDS((S, POOL_W), F32), SDS((S, 2 * D_MODEL), BF16)] + [SDS(b.shape, b.dtype) for b in bufs],
        scratch_shapes=[pltpu.VMEM((2, TM, PAIR_W), F32), pltpu.SemaphoreType.DMA((3 * n,)),
                        pltpu.SemaphoreType.DMA((3 * n,))],
        input_output_aliases={3 + n + w: 6 + w for w in range(n)},
        compiler_params=_cp("arbitrary"),
    )(x, g, w_in, *shards, *bufs)
    return res[:6], res[6:]


def _band_bias(grp, d):
    row = lax.broadcasted_iota(jnp.int32, (BAND, 2 * BAND), 0)
    col = lax.broadcasted_iota(jnp.int32, (BAND, 2 * BAND), 1)
    steps = BAND + row - col
    valid = (steps >= 0) & (steps <= BAND)
    stepsf = (steps * d).astype(F32)
    biases = []
    for hh in range(4):
        slope = 2.0 ** (-ALIBI_MAX_BIAS * (grp * 4 + hh + 1) / N_HEADS)
        biases.append(jnp.where(valid, -slope * stepsf, NEG))
    return biases, col


def _attn_tiles(grp, L):
    rr, rb = ATT_TILE[grp]
    rb = min(rb, L // BAND)
    return rr, rb, L // (rb * BAND)


def _kv_tile(cur_ref, prev_ref, rr, rb, cs):
    if rb == 0:
        return jnp.concatenate([prev_ref[rr, :, cs], cur_ref[rr, 0:BAND, cs]], axis=0)
    return cur_ref[rr, (rb - 1) * BAND:(rb + 1) * BAND, cs]


def _attn_fwd(qkv, grp):
    d = DILATIONS[grp]
    L = qkv.shape[2]
    RR, RB, nb = _attn_tiles(grp, L)

    def body(q_ref, kc_ref, kp_ref, vc_ref, vp_ref, o_ref, lse_ref):
        i = pl.program_id(0)
        biases, col = _band_bias(grp, d)
        first_keys_ok = (col >= BAND) | (i > 0)
        is_a = lax.broadcasted_iota(jnp.int32, (BAND, PAIR_W), 1) < HEAD_W
        for rr in range(RR):
            for rb in range(RB):
                rows = slice(rb * BAND, (rb + 1) * BAND)
                for cp in range(2):
                    cs = slice(cp * PAIR_W, (cp + 1) * PAIR_W)
                    q2 = q_ref[rr, rows, cs]
                    kcat = _kv_tile(kc_ref, kp_ref, rr, rb, cs)
                    vcat = _kv_tile(vc_ref, vp_ref, rr, rb, cs)
                    res = []
                    for h2 in range(2):
                        sel = is_a if h2 == 0 else jnp.logical_not(is_a)
                        b = biases[cp * 2 + h2]
                        if rb == 0:
                            b = jnp.where(first_keys_ok, b, NEG)
                        s = _dot_nt(jnp.where(sel, q2, jnp.zeros_like(q2)), kcat) + b
                        m = jnp.max(s, axis=-1, keepdims=True)
                        p = jnp.exp(s - m)
                        l = jnp.sum(p, axis=-1, keepdims=True)
                        o = _dot(p.astype(BF16), vcat) * (1.0 / l)
                        res.append((o, m + jnp.log(l)))
                    o_ref[rr, rows, cs] = jnp.where(is_a, res[0][0], res[1][0]).astype(BF16)
                    lse_ref[rr, rows, cs] = jnp.where(is_a, res[0][1], res[1][1])

    cur = lambda w: pl.BlockSpec((None, RR, RB * BAND, GROUP_W), lambda i, j: (w, j, i, 0))
    prev = lambda w: pl.BlockSpec((None, RR, BAND, GROUP_W), lambda i, j: (w, j, jnp.maximum(i * RB - 1, 0), 0))
    out = pl.BlockSpec((RR, RB * BAND, GROUP_W), lambda i, j: (j, i, 0))
    return pl.pallas_call(
        body, grid=(nb, d // RR), name=f"attn_fwd_g{grp}",
        in_specs=[cur(0), cur(1), prev(1), cur(2), prev(2)],
        out_specs=[out, out],
        out_shape=[SDS((d, L, GROUP_W), BF16), SDS((d, L, GROUP_W), F32)],
        compiler_params=_cp("parallel", "parallel"),
    )(qkv, qkv, qkv, qkv, qkv)


def _pool_column_select(col, vals):
    return jnp.where(col < POOL_GROUP_W, vals[0],
                     jnp.where(col < 2 * POOL_GROUP_W, vals[1],
                               jnp.where(col < 3 * POOL_GROUP_W, vals[2], vals[3])))


def _pool_inv_count(i, rows):
    t = i * rows + lax.broadcasted_iota(jnp.int32, (rows, POOL_W), 0)
    col = lax.broadcasted_iota(jnp.int32, (rows, POOL_W), 1)
    win = _pool_column_select(col, POOL_WINDOWS)
    return 1.0 / jnp.minimum(t + 1, win).astype(F32), col


def _mixer_out(outs, lses, pz, gates, x, w_ao, w_po, wbd, scale, w_out, g_mlp):
    S = x.shape[0]
    n_tiles = S // TMB

    def body(o0_ref, l0_ref, o1_ref, l1_ref, o2_ref, l2_ref, pz_ref, halo_ref, gate_ref, x_ref,
             wao_ref, wpo_ref, wbd_ref, sc_ref, wout_ref, g_ref,
             a_ref, lt0_ref, lt1_ref, lt2_ref, pooled_ref, mixed_ref, p_ref, merged_ref, h1_ref, m_ref,
             so1, sl1, so2, sl2, slt, ext_ref):
        i = pl.program_id(0)
        _interleave_load(o1_ref, (), so1, DILATIONS[1], TMB)
        _interleave_load(l1_ref, (), sl1, DILATIONS[1], TMB)
        _interleave_load(o2_ref, (), so2, DILATIONS[2], TMB)
        _interleave_load(l2_ref, (), sl2, DILATIONS[2], TMB)
        for h in range(2):
            hs = slice(h * PAIR_W, (h + 1) * PAIR_W)
            l0, l1, l2 = l0_ref[0, :, hs], sl1[h], sl2[h]
            mx = jnp.maximum(jnp.maximum(l0, l1), l2)
            e0, e1, e2 = jnp.exp(l0 - mx), jnp.exp(l1 - mx), jnp.exp(l2 - mx)
            den = e0 + e1 + e2
            a_ref[:, hs] = ((e0 * o0_ref[0, :, hs].astype(F32) + e1 * so1[h] + e2 * so2[h])
                            * (1.0 / den)).astype(BF16)
            slt[h] = mx + jnp.log(den)
        lt = jnp.concatenate([slt[0], slt[1]], axis=1)
        lt0_ref[0] = lt
        for ref, d in ((lt1_ref, DILATIONS[1]), (lt2_ref, DILATIONS[2])):
            for r in range(d):
                for h in range(2):
                    ref[r, :, h * PAIR_W:(h + 1) * PAIR_W] = slt[h, pl.ds(r, TMB // d, stride=d), :]

        pz_t = pz_ref[...]
        ext_ref[0:POOL_HALO, :] = jnp.where(i > 0, halo_ref[...], 0.0)
        ext_ref[POOL_HALO:, :] = pz_t
        sums = []
        acc = ext_ref[...]
        for k in (1, 2, 4, 8):
            acc = acc + pltpu.roll(acc, k, 0)
            sums.append(acc[POOL_HALO:, :])
        inv_cnt, col = _pool_inv_count(i, TMB)
        pooled = (_pool_column_select(col, sums) * inv_cnt - pz_t).astype(BF16)
        pooled_ref[...] = pooled
        mixed = _dot(pooled, wbd_ref[...])
        mixed_ref[...] = mixed.astype(BF16)
        p = (mixed * sc_ref[...]).astype(BF16)
        p_ref[...] = p

        a = a_ref[...]
        for j in range(N_CHIPS):
            js = slice(j * CHUNK, (j + 1) * CHUNK)
            ga = gate_ref[:, js]
            gp = gate_ref[:, D_MODEL + j * CHUNK:D_MODEL + (j + 1) * CHUNK]
            mj = _sigmoid(ga) * _dot(a, wao_ref[j]) + _sigmoid(gp) * _dot(p, wpo_ref[j])
            merged_ref[:, js] = mj.astype(BF16)
        h1 = x_ref[...] + _dot(merged_ref[...], wout_ref[...])
        h1_ref[...] = h1
        m_ref[...] = _rms_fwd(h1, g_ref[...])[0].astype(BF16)

    row = lambda w: pl.BlockSpec((TMB, w), lambda i: (i, 0))
    grp_spec = lambda d: pl.BlockSpec((d, TMB // d, GROUP_W), lambda i: (0, i, 0))
    halo = pl.BlockSpec((POOL_HALO, POOL_W), lambda i: (jnp.maximum(i * (TMB // POOL_HALO) - 1, 0), 0))
    d0, d1, d2 = DILATIONS
    return pl.pallas_call(
        body, grid=(n_tiles,), name="mixer_out",
        in_specs=[grp_spec(d0), grp_spec(d0), grp_spec(d1), grp_spec(d1), grp_spec(d2), grp_spec(d2),
                  row(POOL_W), halo, row(2 * D_MODEL), row(D_MODEL),
                  _resident(w_ao.shape), _resident(w_po.shape), _resident(wbd.shape), _resident(scale.shape),
                  _resident(w_out.shape), _resident(g_mlp.shape)],
        out_specs=[row(GROUP_W), grp_spec(d0), grp_spec(d1), grp_spec(d2),
                   row(POOL_W), row(POOL_W), row(POOL_W), row(D_MODEL), row(D_MODEL), row(D_MODEL)],
        out_shape=[SDS((S, GROUP_W), BF16)] + [SDS((d, S // d, GROUP_W), F32) for d in DILATIONS]
        + [SDS((S, POOL_W), BF16), SDS((S, POOL_W), BF16), SDS((S, POOL_W), BF16),
           SDS((S, D_MODEL), BF16), SDS((S, D_MODEL), F32), SDS((S, D_MODEL), BF16)],
        scratch_shapes=[pltpu.VMEM((2, TMB, PAIR_W), F32) for _ in range(5)]
        + [pltpu.VMEM((TMB + POOL_HALO, POOL_W), F32)],
        compiler_params=_cp("parallel"),
    )(outs[0], lses[0], outs[1], lses[1], outs[2], lses[2], pz, pz, gates, x,
      w_ao, w_po, wbd, scale, w_out, g_mlp)


def _mlp_fwd_loss(m, h1, target, w_mi, w_mo, g_f):
    S = m.shape[0]

    def body(m_ref, h1_ref, t_ref, wmi_ref, wmo_ref, g_ref, hid_ref, dh2_ref, dh2b_ref, loss_ref, dg_ref):
        @pl.when(pl.program_id(0) == 0)
        def _():
            loss_ref[...] = jnp.zeros_like(loss_ref)
            dg_ref[...] = jnp.zeros_like(dg_ref)

        mt = m_ref[...]
        acc = h1_ref[...]
        for c in range(N_CHIPS):
            hid = jnp.square(jnp.maximum(_dot(mt, wmi_ref[c]), 0.0)).astype(BF16)
            hid_ref[:, c * D_MODEL:(c + 1) * D_MODEL] = hid
            acc = acc + _dot(hid, wmo_ref[c])
        g = g_ref[...]
        y, hh, r = _rms_fwd(acc, g)
        e = y - t_ref[...]
        loss_ref[...] += jnp.sum(e * e, axis=0, keepdims=True)
        dy = e * (1.0 / D_MODEL)
        dg_ref[...] += jnp.sum(dy * hh, axis=0, keepdims=True)
        dh2 = _rms_bwd(dy, hh, r, g)
        dh2_ref[...] = dh2
        dh2b_ref[...] = dh2.astype(BF16)

    row = lambda w: pl.BlockSpec((TM, w), lambda i: (i, 0))
    vec = pl.BlockSpec((1, D_MODEL), lambda i: (0, 0))
    return pl.pallas_call(
        body, grid=(S // TM,), name="mlp_fwd_loss",
        in_specs=[row(D_MODEL), row(D_MODEL), row(D_MODEL), _resident(w_mi.shape), _resident(w_mo.shape),
                  _resident(g_f.shape)],
        out_specs=[row(D_FF), row(D_MODEL), row(D_MODEL), vec, vec],
        out_shape=[SDS((S, D_FF), BF16), SDS((S, D_MODEL), F32), SDS((S, D_MODEL), BF16),
                   SDS((1, D_MODEL), F32), SDS((1, D_MODEL), F32)],
        compiler_params=_cp("arbitrary"),
    )(m, h1, target, w_mi, w_mo, g_f)


def _mlp_bwd(dh2, dh2b, hid, h1, w_mi, w_mo, g_mlp):
    S = dh2.shape[0]

    def body(dh2_ref, dh2b_ref, hid_ref, h1_ref, wmi_ref, wmo_ref, g_ref, dpre_ref, dh1_ref, dh1b_ref, dg_ref):
        @pl.when(pl.program_id(0) == 0)
        def _():
            dg_ref[...] = jnp.zeros_like(dg_ref)

        d2 = dh2b_ref[...]
        dm = jnp.zeros((TM, D_MODEL), F32)
        for c in range(N_CHIPS):
            cs = slice(c * D_MODEL, (c + 1) * D_MODEL)
            dhid = _dot_nt(d2, wmo_ref[c])
            dpre = (dhid * (2.0 * jnp.sqrt(hid_ref[:, cs].astype(F32)))).astype(BF16)
            dpre_ref[:, cs] = dpre
            dm = dm + _dot_nt(dpre, wmi_ref[c])
        g = g_ref[...]
        _, hh, r = _rms_fwd(h1_ref[...], g)
        dg_ref[...] += jnp.sum(dm * hh, axis=0, keepdims=True)
        dh1 = dh2_ref[...] + _rms_bwd(dm, hh, r, g)
        dh1_ref[...] = dh1
        dh1b_ref[...] = dh1.astype(BF16)

    row = lambda w: pl.BlockSpec((TM, w), lambda i: (i, 0))
    return pl.pallas_call(
        body, grid=(S // TM,), name="mlp_bwd",
        in_specs=[row(D_MODEL), row(D_MODEL), row(D_FF), row(D_MODEL), _resident(w_mi.shape),
                  _resident(w_mo.shape), _resident(g_mlp.shape)],
        out_specs=[row(D_FF), row(D_MODEL), row(D_MODEL), pl.BlockSpec((1, D_MODEL), lambda i: (0, 0))],
        out_shape=[SDS((S, D_FF), BF16), SDS((S, D_MODEL), F32), SDS((S, D_MODEL), BF16), SDS((1, D_MODEL), F32)],
        compiler_params=_cp("arbitrary"),
    )(dh2, dh2b, hid, h1, w_mi, w_mo, g_mlp)


def _mixer_bwd(dh1b, a, p, mixed, gates, w_out, w_ao, w_po, wbd, scale, head_ones, sums):
    S = a.shape[0]
    n_tiles = S // TMB
    n = len(sums)

    def body(*refs):
        (dh1b_ref, a_ref, p_ref, mixed_ref, gate_ref, wout_ref, wao_ref, wpo_ref, wbd_ref, sc_ref,
         ones_ref) = refs[0:11]
        sum_refs = refs[11:11 + n]
        (da1_ref, dp1_ref, dgate_ref, da0_ref, dag1_ref, dag2_ref, dd0_ref, dd1_ref, dd2_ref,
         dmixed_ref, dqp_ref, dscale_ref) = refs[11 + n:23 + n]
        land_refs = refs[23 + n:23 + 2 * n]
        s_da, s_dd, send_sem, recv_sem = refs[23 + 2 * n:]
        i = pl.program_id(0)

        @pl.when(i == 0)
        def _():
            dscale_ref[...] = jnp.zeros_like(dscale_ref)
            for cpy in _chip_sum_copies(sum_refs, land_refs, send_sem, recv_sem):
                cpy.start()

        dmerged = _dot_nt(dh1b_ref[...], wout_ref[...])
        a = a_ref[...]
        p = p_ref[...]
        da = jnp.zeros((TMB, GROUP_W), F32)
        dp = jnp.zeros((TMB, POOL_W), F32)
        for j in range(N_CHIPS):
            js = slice(j * CHUNK, (j + 1) * CHUNK)
            sa = _sigmoid(gate_ref[:, js])
            sp = _sigmoid(gate_ref[:, D_MODEL + j * CHUNK:D_MODEL + (j + 1) * CHUNK])
            dmj = dmerged[:, js]
            da1 = (dmj * sa).astype(BF16)
            dp1 = (dmj * sp).astype(BF16)
            da1_ref[:, js] = da1
            dp1_ref[:, js] = dp1
            dgate_ref[j] = (dmj * _dot(a, wao_ref[j]) * sa * (1.0 - sa)).astype(BF16)
            dgate_ref[N_CHIPS + j] = (dmj * _dot(p, wpo_ref[j]) * sp * (1.0 - sp)).astype(BF16)
            da = da + _dot_nt(da1, wao_ref[j])
            dp = dp + _dot_nt(dp1, wpo_ref[j])

        prod = da * a.astype(F32)
        hi = prod.astype(BF16)
        lo = (prod - hi.astype(F32)).astype(BF16)
        dd = _dot(hi, ones_ref[...]) + _dot(lo, ones_ref[...])
        for ref, val, sref, dtype in ((da0_ref, da, s_da, BF16), (dd0_ref, dd, s_dd, F32)):
            ref[0] = val.astype(dtype)
            for h in range(2):
                sref[h] = val[:, h * PAIR_W:(h + 1) * PAIR_W]
        for refs, d in (((dag1_ref, dd1_ref), DILATIONS[1]), ((dag2_ref, dd2_ref), DILATIONS[2])):
            for r in range(d):
                for h in range(2):
                    hs = slice(h * PAIR_W, (h + 1) * PAIR_W)
                    refs[0][r, :, hs] = s_da[h, pl.ds(r, TMB // d, stride=d), :].astype(BF16)
                    refs[1][r, :, hs] = s_dd[h, pl.ds(r, TMB // d, stride=d), :]

        sc = sc_ref[...]
        dscale_ref[...] += jnp.sum(dp * mixed_ref[...].astype(F32), axis=0, keepdims=True)
        dmixed = (dp * sc).astype(BF16)
        dmixed_ref[...] = dmixed
        inv_cnt, _ = _pool_inv_count(i, TMB)
        dqp_ref[...] = (_dot_nt(dmixed, wbd_ref[...]) * inv_cnt).astype(BF16)

        @pl.when(i == n_tiles - 1)
        def _():
            for cpy in _chip_sum_copies(sum_refs, land_refs, send_sem, recv_sem):
                cpy.wait()

    row = lambda w: pl.BlockSpec((TMB, w), lambda i: (i, 0))
    grp_spec = lambda d: pl.BlockSpec((d, TMB // d, GROUP_W), lambda i: (0, i, 0))
    d0, d1, d2 = DILATIONS
    res = pl.pallas_call(
        body, grid=(n_tiles,), name="mixer_bwd",
        in_specs=[row(D_MODEL), row(GROUP_W), row(POOL_W), row(POOL_W), row(2 * D_MODEL),
                  _resident(w_out.shape), _resident(w_ao.shape), _resident(w_po.shape), _resident(wbd.shape),
                  _resident(scale.shape), _resident(head_ones.shape)] + [ANY] * n,
        out_specs=[row(D_MODEL), row(D_MODEL), pl.BlockSpec((2 * N_CHIPS, TMB, CHUNK), lambda i: (0, i, 0)),
                   grp_spec(d0), grp_spec(d1), grp_spec(d2), grp_spec(d0), grp_spec(d1), grp_spec(d2),
                   row(POOL_W), row(POOL_W), pl.BlockSpec((1, POOL_W), lambda i: (0, 0))] + [ANY] * n,
        out_shape=[SDS((S, D_MODEL), BF16), SDS((S, D_MODEL), BF16), SDS((2 * N_CHIPS, S, CHUNK), BF16)]
        + [SDS((d, S // d, GROUP_W), BF16) for d in DILATIONS]
        + [SDS((d, S // d, GROUP_W), F32) for d in DILATIONS]
        + [SDS((S, POOL_W), BF16), SDS((S, POOL_W), BF16), SDS((1, POOL_W), F32)]
        + [SDS(t.shape, t.dtype) for t in sums],
        scratch_shapes=[pltpu.VMEM((2, TMB, PAIR_W), F32), pltpu.VMEM((2, TMB, PAIR_W), F32),
                        pltpu.SemaphoreType.DMA((3 * n,)), pltpu.SemaphoreType.DMA((3 * n,))],
        compiler_params=_cp("arbitrary"),
    )(dh1b, a, p, mixed, gates, w_out, w_ao, w_po, wbd, scale, head_ones, *sums)
    return res[:12], res[12:]


def _attn_bwd(qkv, da, lt, dd, grp):
    d = DILATIONS[grp]
    L = qkv.shape[2]
    RR, RB, nb = _attn_tiles(grp, L)

    def body(q_ref, kc_ref, kp_ref, vc_ref, vp_ref, da_ref, lt_ref, dd_ref, dq_ref, dk_ref, dv_ref, dk_acc, dv_acc):
        i = pl.program_id(1)
        cur = i % 2
        prv = 1 - cur

        @pl.when(i == 0)
        def _():
            dk_acc[...] = jnp.zeros_like(dk_acc)
            dv_acc[...] = jnp.zeros_like(dv_acc)

        @pl.when(i < nb)
        def _():
            dk_acc[cur] = jnp.zeros((RR, RB * BAND, GROUP_W), F32)
            dv_acc[cur] = jnp.zeros((RR, RB * BAND, GROUP_W), F32)
            biases, col = _band_bias(grp, d)
            first_keys_ok = (col >= BAND) | (i > 0)
            is_a = lax.broadcasted_iota(jnp.int32, (BAND, PAIR_W), 1) < HEAD_W
            for rr in range(RR):
                for rb in range(RB):
                    rows = slice(rb * BAND, (rb + 1) * BAND)
                    for cp in range(2):
                        cs = slice(cp * PAIR_W, (cp + 1) * PAIR_W)
                        q2 = q_ref[rr, rows, cs]
                        da2 = da_ref[rr, rows, cs]
                        lt2 = lt_ref[rr, rows, cs]
                        dd2 = dd_ref[rr, rows, cs]
                        kcat = _kv_tile(kc_ref, kp_ref, rr, rb, cs)
                        vcat = _kv_tile(vc_ref, vp_ref, rr, rb, cs)
                        q2t = q2.astype(F32).T.astype(BF16)
                        da2t = da2.astype(F32).T.astype(BF16)
                        dqs, dkts, dvts = [], [], []
                        for h2 in range(2):
                            sel = is_a if h2 == 0 else jnp.logical_not(is_a)
                            lane0 = h2 * HEAD_W
                            b = biases[cp * 2 + h2]
                            if rb == 0:
                                b = jnp.where(first_keys_ok, b, NEG)
                            s = _dot_nt(jnp.where(sel, q2, jnp.zeros_like(q2)), kcat) + b
                            p = jnp.exp(s - lt2[:, lane0:lane0 + 1])
                            dpv = _dot_nt(jnp.where(sel, da2, jnp.zeros_like(da2)), vcat)
                            ds = (p * (dpv - dd2[:, lane0:lane0 + 1])).astype(BF16)
                            dqs.append(_dot(ds, kcat))
                            dkts.append(_dot(q2t[lane0:lane0 + HEAD_W, :], ds))
                            dvts.append(_dot(da2t[lane0:lane0 + HEAD_W, :], p.astype(BF16)))
                        dq_ref[rr, rows, cs] = (jnp.where(is_a, dqs[0], dqs[1]) * 0.125).astype(BF16)
                        dkc = jnp.concatenate(dkts, axis=0).T
                        dvc = jnp.concatenate(dvts, axis=0).T
                        if rb == 0:
                            last = slice((RB - 1) * BAND, RB * BAND)
                            dk_acc[prv, rr, last, cs] += dkc[0:BAND]
                            dv_acc[prv, rr, last, cs] += dvc[0:BAND]
                            dk_acc[cur, rr, 0:BAND, cs] += dkc[BAND:]
                            dv_acc[cur, rr, 0:BAND, cs] += dvc[BAND:]
                        else:
                            both = slice((rb - 1) * BAND, (rb + 1) * BAND)
                            dk_acc[cur, rr, both, cs] += dkc
                            dv_acc[cur, rr, both, cs] += dvc

        @pl.when(i > 0)
        def _():
            dk_ref[...] = dk_acc[prv].astype(BF16)
            dv_ref[...] = dv_acc[prv].astype(BF16)

    qi = lambda i: jnp.minimum(i, nb - 1)
    cur_w = lambda w: pl.BlockSpec((None, RR, RB * BAND, GROUP_W), lambda j, i: (w, j, qi(i), 0))
    prev_w = lambda w: pl.BlockSpec((None, RR, BAND, GROUP_W),
                                    lambda j, i: (w, j, jnp.maximum(qi(i) * RB - 1, 0), 0))
    blk = pl.BlockSpec((RR, RB * BAND, GROUP_W), lambda j, i: (j, qi(i), 0))
    late = pl.BlockSpec((RR, RB * BAND, GROUP_W), lambda j, i: (j, jnp.maximum(i - 1, 0), 0))
    return pl.pallas_call(
        body, grid=(d // RR, nb + 1), name=f"attn_bwd_g{grp}",
        in_specs=[cur_w(0), cur_w(1), prev_w(1), cur_w(2), prev_w(2), blk, blk, blk],
        out_specs=[blk, late, late],
        out_shape=[SDS((d, L, GROUP_W), BF16)] * 3,
        scratch_shapes=[pltpu.VMEM((2, RR, RB * BAND, GROUP_W), F32), pltpu.VMEM((2, RR, RB * BAND, GROUP_W), F32)],
        compiler_params=_cp("parallel", "arbitrary"),
    )(qkv, qkv, qkv, qkv, qkv, da, lt, dd)


def _dz_assemble(dqkv, dqp):
    S = dqp.shape[0]
    n_tiles = S // TMB

    def body(*refs):
        dqkv_refs = refs[0:9]
        dqp_ref, halo_ref = refs[9:11]
        dz_ref, s_ref, ext_ref = refs[11:]
        i = pl.program_id(0)

        for grp in range(3):
            for which in range(3):
                n = which * 3 + grp
                ref = dqkv_refs[grp * 3 + which]
                if DILATIONS[grp] == 1:
                    dz_ref[n] = ref[0]
                else:
                    _interleave_load(ref, (), s_ref, DILATIONS[grp], TMB)
                    for h in range(2):
                        dz_ref[n, :, h * PAIR_W:(h + 1) * PAIR_W] = s_ref[h].astype(BF16)

        dqp = dqp_ref[...].astype(F32)
        ext_ref[0:TMB, :] = dqp
        ext_ref[TMB:, :] = jnp.where(i < n_tiles - 1, halo_ref[...].astype(F32), 0.0)
        sums = []
        acc = ext_ref[...]
        for k in (1, 2, 4, 8):
            acc = acc + pltpu.roll(acc, TMB + POOL_HALO - k, 0)
            sums.append(acc[0:TMB, :])
        inv_cnt, col = _pool_inv_count(i, TMB)
        dpz = _pool_column_select(col, sums) - dqp / inv_cnt
        for t in range(3):
            dz_ref[9 + t] = dpz[:, t * CHUNK:(t + 1) * CHUNK].astype(BF16)

    row = lambda w: pl.BlockSpec((TMB, w), lambda i: (i, 0))
    grp_spec = lambda d: pl.BlockSpec((d, TMB // d, GROUP_W), lambda i: (0, i, 0))
    halo = pl.BlockSpec((POOL_HALO, POOL_W),
                        lambda i: (jnp.minimum((i + 1) * (TMB // POOL_HALO), S // POOL_HALO - 1), 0))
    flat = [t for grp in range(3) for t in dqkv[grp]]
    return pl.pallas_call(
        body, grid=(n_tiles,), name="dz_assemble",
        in_specs=[grp_spec(DILATIONS[grp]) for grp in range(3) for _ in range(3)] + [row(POOL_W), halo],
        out_specs=pl.BlockSpec((N_DZ_CHUNKS, TMB, CHUNK), lambda i: (0, i, 0)),
        out_shape=SDS((N_DZ_CHUNKS, S, CHUNK), BF16),
        scratch_shapes=[pltpu.VMEM((2, TMB, PAIR_W), F32), pltpu.VMEM((TMB + POOL_HALO, POOL_W), F32)],
        compiler_params=_cp("parallel"),
    )(*flat, dqp, dqp)


def _inproj_dx(dz, dgates, dh1, x, g, w_in, sums, tiles, name, fill=None):
    S = x.shape[0]
    t0 = tiles[0]
    n_tiles = tiles[1] - tiles[0]
    n = len(sums)
    n_fill = 0 if fill is None else 1

    def body(*refs):
        dz_ref, dgate_ref, dh1_ref, x_ref, g_ref, w_ref = refs[0:6]
        sum_refs = refs[6:6 + n]
        dx_ref, dg_ref = refs[6 + n + n_fill:8 + n + n_fill]
        land_refs = refs[8 + n + n_fill:8 + 2 * n + n_fill]
        sems = refs[8 + 2 * n + n_fill:]
        i = pl.program_id(0)

        def copies():
            return _chip_sum_copies(sum_refs, land_refs, *sems) if n else []

        @pl.when(i == 0)
        def _():
            dg_ref[...] = jnp.zeros_like(dg_ref)
            for cpy in copies():
                cpy.start()

        du = jnp.zeros((TM, D_MODEL), F32)
        for k in range(N_CHUNKS):
            dzk = dz_ref[k] if k < N_DZ_CHUNKS else dgate_ref[k - N_DZ_CHUNKS]
            du = du + _dot_nt(dzk, _w_in_chunk(w_ref, k))
        gv = g_ref[...]
        _, xh, r = _rms_fwd(x_ref[...], gv)
        dg_ref[...] += jnp.sum(du * xh, axis=0, keepdims=True)
        dx_ref[...] = dh1_ref[...] + _rms_bwd(du, xh, r, gv)

        @pl.when(i == n_tiles - 1)
        def _():
            for cpy in copies():
                cpy.wait()

    row = lambda w: pl.BlockSpec((TM, w), lambda i: (t0 + i, 0))
    extra = [] if fill is None else [fill]
    res = pl.pallas_call(
        body, grid=(n_tiles,), name=name,
        in_specs=[pl.BlockSpec((N_DZ_CHUNKS, TM, CHUNK), lambda i: (0, t0 + i, 0)),
                  pl.BlockSpec((N_CHUNKS - N_DZ_CHUNKS, TM, CHUNK), lambda i: (0, t0 + i, 0)),
                  row(D_MODEL), row(D_MODEL), _resident(g.shape), _resident(w_in.shape)] + [ANY] * (n + n_fill),
        out_specs=[row(D_MODEL), pl.BlockSpec((1, D_MODEL), lambda i: (0, 0))] + [ANY] * n,
        out_shape=[SDS((S, D_MODEL), F32), SDS((1, D_MODEL), F32)] + [SDS(t.shape, t.dtype) for t in sums],
        scratch_shapes=[pltpu.SemaphoreType.DMA((3 * n,)), pltpu.SemaphoreType.DMA((3 * n,))] if n else [],
        input_output_aliases={} if fill is None else {6 + n: 0},
        compiler_params=_cp("arbitrary"),
    )(dz, dgates, dh1, x, g, w_in, *sums, *extra)
    return res[0], res[1], res[2:]


def _wgrad(a, b, name, *, out_shape, a_spec, b_spec, out_spec, grid, n_out_cols=None, fill=None):
    k_axis = len(grid) - 1

    def body(a_ref, b_ref, *rest):
        o_ref = rest[-1]

        @pl.when(pl.program_id(k_axis) == 0)
        def _():
            o_ref[...] = jnp.zeros_like(o_ref)

        at = a_ref[...]
        if n_out_cols is None:
            o_ref[...] += _dot_tn(at, b_ref[...])
        elif n_out_cols[0] == "lead_both":
            for t in range(b_ref.shape[0]):
                o_ref[t] += _dot_tn(at, b_ref[t])
        else:
            w = n_out_cols[1]
            for t in range(o_ref.shape[0]):
                o_ref[t] += _dot_tn(at, b_ref[:, t * w:(t + 1) * w])

    sem = ("parallel",) * k_axis + ("arbitrary",)
    extra = [] if fill is None else [fill]
    return pl.pallas_call(body, grid=grid, name=name, in_specs=[a_spec, b_spec] + [ANY] * len(extra),
                          out_specs=out_spec, out_shape=out_shape,
                          input_output_aliases={} if fill is None else {2: 0},
                          compiler_params=_cp(*sem))(a, b, *extra)


def _wgrad_in(u, dz, dgates):
    bk = min(BK, u.shape[0])
    nk = u.shape[0] // bk
    g = WGRAD_IN_GROUP
    kw = dict(n_out_cols=("lead_both", CHUNK), a_spec=pl.BlockSpec((bk, D_MODEL), lambda j, k: (k, 0)),
              b_spec=pl.BlockSpec((g, bk, CHUNK), lambda j, k: (j, k, 0)),
              out_shape=SDS((N_CHUNKS, D_MODEL, CHUNK), F32))
    first = _wgrad(u, dz, "wgrad_in_qkvp", grid=(N_DZ_CHUNKS // g, nk),
                   out_spec=pl.BlockSpec((g, D_MODEL, CHUNK), lambda j, k: (j, 0, 0)), **kw)
    return _wgrad(u, dgates, "wgrad_in_gates", grid=((N_CHUNKS - N_DZ_CHUNKS) // g, nk), fill=first,
                  out_spec=pl.BlockSpec((g, D_MODEL, CHUNK), lambda j, k: (N_DZ_CHUNKS // g + j, 0, 0)), **kw)


def _wgrads_mixer(a, da1, p, dp1, merged, dh1b, pooled, dmixed):
    bk = min(BK, a.shape[0])
    nk = a.shape[0] // bk
    g_ao = _wgrad(
        a, da1, "wgrad_att_out", grid=(nk,), n_out_cols=("cols_b", CHUNK),
        a_spec=pl.BlockSpec((bk,GROUP_W), lambda k: (k, 0)),
        b_spec=pl.BlockSpec((bk,D_MODEL), lambda k: (k, 0)),
        out_spec=pl.BlockSpec((N_CHIPS, GROUP_W, CHUNK), lambda k: (0, 0, 0)),
        out_shape=SDS((N_CHIPS, GROUP_W, CHUNK), F32))
    g_po = _wgrad(
        p, dp1, "wgrad_pool_out", grid=(nk,), n_out_cols=("cols_b", CHUNK),
        a_spec=pl.BlockSpec((bk,POOL_W), lambda k: (k, 0)),
        b_spec=pl.BlockSpec((bk,D_MODEL), lambda k: (k, 0)),
        out_spec=pl.BlockSpec((N_CHIPS, POOL_W, CHUNK), lambda k: (0, 0, 0)),
        out_shape=SDS((N_CHIPS, POOL_W, CHUNK), F32))
    g_out = _wgrad(
        merged, dh1b, "wgrad_out", grid=(nk,),
        a_spec=pl.BlockSpec((bk,D_MODEL), lambda k: (k, 0)),
        b_spec=pl.BlockSpec((bk,D_MODEL), lambda k: (k, 0)),
        out_spec=pl.BlockSpec((D_MODEL, D_MODEL), lambda k: (0, 0)),
        out_shape=SDS((D_MODEL, D_MODEL), F32))
    g_bd = _wgrad(
        pooled, dmixed, "wgrad_pool_grp", grid=(nk,),
        a_spec=pl.BlockSpec((bk,POOL_W), lambda k: (k, 0)),
        b_spec=pl.BlockSpec((bk,POOL_W), lambda k: (k, 0)),
        out_spec=pl.BlockSpec((POOL_W, POOL_W), lambda k: (0, 0)),
        out_shape=SDS((POOL_W, POOL_W), F32))
    return [g_ao, g_po, g_out.reshape(N_CHIPS, D_MODEL // N_CHIPS, D_MODEL)], g_bd


def _wgrads_mlp(m, dpre, hid, dh2b):
    bk = min(BK, m.shape[0])
    nk = m.shape[0] // bk
    g_mi = _wgrad(
        m, dpre, "wgrad_mlp_in", grid=(N_CHIPS, nk),
        a_spec=pl.BlockSpec((bk,D_MODEL), lambda c, k: (k, 0)),
        b_spec=pl.BlockSpec((bk,D_MODEL), lambda c, k: (k, c)),
        out_spec=pl.BlockSpec((None, D_MODEL, D_MODEL), lambda c, k: (c, 0, 0)),
        out_shape=SDS((N_CHIPS, D_MODEL, D_MODEL), F32))
    g_mo = _wgrad(
        hid, dh2b, "wgrad_mlp_out", grid=(N_CHIPS, nk),
        a_spec=pl.BlockSpec((bk,D_MODEL), lambda c, k: (k, c)),
        b_spec=pl.BlockSpec((bk,D_MODEL), lambda c, k: (k, 0)),
        out_spec=pl.BlockSpec((None, D_MODEL, D_MODEL), lambda c, k: (c, 0, 0)),
        out_shape=SDS((N_CHIPS, D_MODEL, D_MODEL), F32))
    return [g_mi, g_mo]


def _mesh_place():
    x, y, c = lax.axis_index("x"), lax.axis_index("y"), lax.axis_index("c")
    other_chips = [(x, 1 - y), (1 - x, y), (1 - x, 1 - y)]
    return x, y, c, other_chips


ANY = pl.BlockSpec(memory_space=pl.ANY)


def _weight_half_copies(shard_refs, buf_refs, rows, send_sem, recv_sem):
    x, y, c, chips = _mesh_place()
    me = 2 * x + y
    copies = []
    for w, r_full in enumerate(rows):
        rh = r_full // 2
        for r, (px, py) in enumerate(chips):
            k = w * 3 + r
            copies.append(pltpu.make_async_remote_copy(
                src_ref=shard_refs[w].at[pl.ds(c * rh, rh), :], dst_ref=buf_refs[w].at[me, pl.ds(c * rh, rh), :],
                send_sem=send_sem.at[k], recv_sem=recv_sem.at[k], device_id=(px, py, c), device_id_type=MESH))
    return copies


def _pair_forward_copies(buf_refs, rows, send_sem, recv_sem):
    x, y, c, chips = _mesh_place()
    out = []
    for w, r_full in enumerate(rows):
        rh = r_full // 2
        for r, (px, py) in enumerate(chips):
            k = w * 3 + r
            landed = buf_refs[w].at[2 * px + py, pl.ds(c * rh, rh), :]
            theirs = buf_refs[w].at[2 * px + py, pl.ds((1 - c) * rh, rh), :]
            mk = lambda ref: pltpu.make_async_remote_copy(
                src_ref=ref, dst_ref=ref, send_sem=send_sem.at[k], recv_sem=recv_sem.at[k],
                device_id=(x, y, 1 - c), device_id_type=MESH)
            out.append((mk(landed), mk(theirs)))
    return out


def _place_own(block, n_slots, slot):
    buf = jnp.zeros((n_slots,) + block.shape, block.dtype)
    return lax.dynamic_update_slice(buf, block[None], (slot,) + (0,) * block.ndim)


def _allgather_weights(shards, bufs):
    n = len(shards)
    rows = [sh.shape[0] for sh in shards]

    def body(*refs):
        src, dst = refs[:n], refs[2 * n:3 * n]
        send_sem, recv_sem, fsend_sem, frecv_sem = refs[3 * n:]
        sends = _weight_half_copies(src, dst, rows, send_sem, recv_sem)
        for cpy in sends:
            cpy.start()
        fwds = _pair_forward_copies(dst, rows, fsend_sem, frecv_sem)
        for cpy, (fwd, _) in zip(sends, fwds):
            cpy.wait_recv()
            fwd.start()
        for _, landing in fwds:
            landing.wait_recv()
        for cpy in sends + [f for f, _ in fwds]:
            cpy.wait_send()

    return pl.pallas_call(
        body, name="allgather_w_in",
        in_specs=[ANY] * (2 * n), out_specs=[ANY] * n,
        out_shape=[SDS(b.shape, b.dtype) for b in bufs],
        scratch_shapes=[pltpu.SemaphoreType.DMA((3 * n,))] * 4,
        input_output_aliases={n + w: w for w in range(n)},
    )(*shards, *bufs)


def _pair_forward(bufs, rows):
    n = len(bufs)

    def body(*refs):
        dst = refs[n:2 * n]
        send_sem, recv_sem = refs[2 * n:]
        fwds = _pair_forward_copies(dst, rows, send_sem, recv_sem)
        for fwd, _ in fwds:
            fwd.start()
        for fwd, landing in fwds:
            landing.wait_recv()
            fwd.wait_send()

    return pl.pallas_call(
        body, name="weights_pair_forward",
        in_specs=[ANY] * n, out_specs=[ANY] * n,
        out_shape=[SDS(b.shape, b.dtype) for b in bufs],
        scratch_shapes=[pltpu.SemaphoreType.DMA((3 * n,))] * 2,
        input_output_aliases={w: w for w in range(n)},
    )(*bufs)


def _chip_sum_copies(src, dst, send_sem, recv_sem):
    x, y, c, chips = _mesh_place()
    copies = []
    for w in range(len(src)):
        for r, (px, py) in enumerate(chips):
            k = w * 3 + r
            copies.append(pltpu.make_async_remote_copy(
                src_ref=src[w].at[r + 1], dst_ref=dst[w].at[r + 1], send_sem=send_sem.at[k], recv_sem=recv_sem.at[k],
                device_id=(px, py, c), device_id_type=MESH))
    return copies


def _pair_exchange(grads):
    n = len(grads)

    def body(*refs):
        src, dst = refs[:n], refs[n:2 * n]
        send_sem, recv_sem = refs[2 * n:]
        x, y, c, _ = _mesh_place()
        copies = []
        for w in range(n):
            rh = grads[w].shape[1] // 2
            copies.append(pltpu.make_async_remote_copy(
                src_ref=src[w].at[:, pl.ds((1 - c) * rh, rh), :], dst_ref=dst[w],
                send_sem=send_sem.at[w], recv_sem=recv_sem.at[w],
                device_id=(x, y, 1 - c), device_id_type=MESH))
            copies[-1].start()
        for cpy in copies:
            cpy.wait()

    return pl.pallas_call(
        body, name="grad_pair_exchange",
        in_specs=[ANY] * n, out_specs=[ANY] * n,
        out_shape=[SDS((N_CHIPS, g.shape[1] // 2, g.shape[2]), F32) for g in grads],
        scratch_shapes=[pltpu.SemaphoreType.DMA((n,)), pltpu.SemaphoreType.DMA((n,))],
    )(*grads)


def _pair_sum(place, grad, recv, name):
    _, R, C = grad.shape
    rh = R // 2
    br = _row_block(rh, max(256, ELEMENTWISE_BLOCK // C))
    nbh = rh // br

    def body(place_ref, g_ref, r_ref, own_ref, sums_ref):
        s = g_ref[...] + r_ref[...]

        @pl.when(pl.program_id(1) == 0)
        def _():
            own_ref[...] = s

        sums_ref[...] = s.astype(BF16)

    slot = lambda rel, pr: jnp.bitwise_xor(pr[0], rel)
    return pl.pallas_call(
        body, name=name,
        grid_spec=pltpu.PrefetchScalarGridSpec(
            num_scalar_prefetch=1, grid=(nbh, N_CHIPS),
            in_specs=[pl.BlockSpec((None, br, C), lambda i, rel, pr: (slot(rel, pr), pr[1] * nbh + i, 0)),
                      pl.BlockSpec((None, br, C), lambda i, rel, pr: (slot(rel, pr), i, 0))],
            out_specs=[pl.BlockSpec((br, C), lambda i, rel, pr: (i, 0)),
                       pl.BlockSpec((None, br, C), lambda i, rel, pr: (rel, i, 0))]),
        out_shape=[SDS((rh, C), F32), SDS((N_CHIPS, rh, C), BF16)],
        compiler_params=_cp("parallel", "arbitrary"),
    )(place, grad, recv)


def _chip_sum(place, own, recv, name):
    rh, C = own.shape
    br = _row_block(rh, max(256, ELEMENTWISE_BLOCK // C))
    nbh = rh // br

    def body(place_ref, own_ref, r_ref, o_ref):
        o_ref[...] = ((own_ref[...] + r_ref[1].astype(F32)) + r_ref[2].astype(F32)) + r_ref[3].astype(F32)

    return pl.pallas_call(
        body, name=name,
        grid_spec=pltpu.PrefetchScalarGridSpec(
            num_scalar_prefetch=1, grid=(nbh,),
            in_specs=[pl.BlockSpec((br, C), lambda i, pr: (i, 0)),
                      pl.BlockSpec((N_CHIPS, br, C), lambda i, pr: (0, i, 0))],
            out_specs=pl.BlockSpec((br, C), lambda i, pr: (pr[1] * nbh + i, 0))),
        out_shape=SDS((2 * rh, C), F32),
        compiler_params=_cp("parallel"),
    )(place, own, recv)


def _finish_exchange(grads, small_all):
    n = len(grads)

    def body(*refs):
        dst, all_ref = refs[n + 1:2 * n + 1], refs[2 * n + 1]
        send_sem, recv_sem, ssend_sem, srecv_sem = refs[2 * n + 2:]
        x, y, c, chips = _mesh_place()
        sib = (x, y, 1 - c)

        def pack(dev, k, to):
            slot = 4 * dev[0] + 2 * dev[1] + dev[2]
            return pltpu.make_async_remote_copy(
                src_ref=all_ref.at[slot], dst_ref=all_ref.at[slot], send_sem=ssend_sem.at[k],
                recv_sem=srecv_sem.at[k], device_id=to, device_id_type=MESH)

        pack_copies = [pack((x, y, c), 0, sib)] + [pack((x, y, c), 1 + r, (px, py, c))
                                                   for r, (px, py) in enumerate(chips)]
        for cpy in pack_copies:
            cpy.start()
        sends, landings = [], []
        for w in range(n):
            rh = grads[w].shape[0] // 2
            mk = lambda cc: pltpu.make_async_remote_copy(
                src_ref=dst[w].at[pl.ds(cc * rh, rh), :], dst_ref=dst[w].at[pl.ds(cc * rh, rh), :],
                send_sem=send_sem.at[w], recv_sem=recv_sem.at[w], device_id=(x, y, 1 - c), device_id_type=MESH)
            sends.append(mk(c))
            landings.append(mk(1 - c))
            sends[-1].start()
        for r, (px, py) in enumerate(chips):
            pack((px, py, c), 1 + r, (px, py, c)).wait_recv()
            pack_copies.append(pack((px, py, c), 4 + r, sib))
            pack_copies[-1].start()
        pack(sib, 0, sib).wait_recv()
        for r, (px, py) in enumerate(chips):
            pack((px, py, 1 - c), 4 + r, sib).wait_recv()
        for cpy in landings:
            cpy.wait_recv()
        for cpy in sends + pack_copies:
            cpy.wait_send()

    res = pl.pallas_call(
        body, name="grad_finish_exchange",
        in_specs=[ANY] * (n + 1), out_specs=[ANY] * (n + 1),
        out_shape=[SDS(g.shape, g.dtype) for g in grads] + [SDS(small_all.shape, small_all.dtype)],
        scratch_shapes=[pltpu.SemaphoreType.DMA((n,)), pltpu.SemaphoreType.DMA((n,)),
                        pltpu.SemaphoreType.DMA((N_DEV - 1,)), pltpu.SemaphoreType.DMA((N_DEV - 1,))],
        input_output_aliases={w: w for w in range(n + 1)},
    )(*grads, small_all)
    return res[:n], res[n]


def _adamw_math(w, g, m, v):
    m = ADAM_B1 * m + (1.0 - ADAM_B1) * g
    v = ADAM_B2 * v + (1.0 - ADAM_B2) * jnp.square(g)
    m_hat = m / (1.0 - ADAM_B1 ** ADAM_STEP)
    v_hat = v / (1.0 - ADAM_B2 ** ADAM_STEP)
    delta = -ADAM_LR * (m_hat / (jnp.sqrt(v_hat) + ADAM_EPS) + ADAM_WD * w)
    return delta, m, v


def _adamw(w, g, m, v, name):
    R, C = w.shape
    br = _row_block(R, 512)
    if g.ndim == 3:
        n_chunks, cw = g.shape[0], g.shape[2]
        g_spec = pl.BlockSpec((None, br, cw), lambda t, i: (t, i, 0))
    else:
        n_chunks, cw = 1, C
        g_spec = pl.BlockSpec((br, cw), lambda t, i: (i, t))

    def body(w_ref, g_ref, m_ref, v_ref, g_out_ref, d_ref, nm_ref, nv_ref):
        gv = g_ref[...]
        g_out_ref[...] = gv
        d_ref[...], nm_ref[...], nv_ref[...] = _adamw_math(w_ref[...], gv, m_ref[...], v_ref[...])

    spec = pl.BlockSpec((br, cw), lambda t, i: (i, t))
    return pl.pallas_call(
        body, grid=(n_chunks, R // br), name=name, in_specs=[spec, g_spec, spec, spec], out_specs=[spec] * 4,
        out_shape=[SDS((R, C), F32)] * 4, compiler_params=_cp("parallel", "parallel"),
    )(w, g, m, v)


def _small_sum_adamw(all_small, w, m, v):
    loss_row = PACK_ROWS - 8

    def body(all_ref, w_ref, m_ref, v_ref, g_ref, d_ref, nm_ref, nv_ref, loss_ref):
        g = all_ref[0]
        for k in range(1, N_DEV):
            g = g + all_ref[k]
        g_ref[...] = g
        d_ref[...], nm_ref[...], nv_ref[...] = _adamw_math(w_ref[...], g, m_ref[...], v_ref[...])
        total = jnp.sum(g[loss_row:loss_row + 1, :]) * (0.5 / D_MODEL)
        loss_ref[...] = jnp.full(loss_ref.shape, total, F32)

    full = lambda s: pl.BlockSpec(s, lambda i: (0,) * len(s))
    pack = (PACK_ROWS, D_MODEL)
    return pl.pallas_call(
        body, grid=(1,), name="small_sum_adamw",
        in_specs=[full((N_DEV,) + pack), full(pack), full(pack), full(pack)],
        out_specs=[full(pack)] * 4 + [full((8, 128))],
        out_shape=[SDS(pack, F32)] * 4 + [SDS((8, 128), F32)],
        compiler_params=_cp("arbitrary"),
    )(all_small, w, m, v)


def _pack_small(grp, scale, g_mix, g_mlp, g_f, loss_lanes):
    def part(vec):
        vec = vec.reshape(1, -1)
        return jnp.pad(vec, ((0, 7), (0, D_MODEL - vec.shape[1])))
    return jnp.concatenate([grp.reshape(-1, D_MODEL), part(scale), part(g_mix), part(g_mlp), part(g_f),
                            part(loss_lanes)], axis=0)


def _unpack_small(pack):
    n_grp = len(POOL_WINDOWS) * POOL_GROUP_W * POOL_GROUP_W // D_MODEL
    grp = pack[:n_grp].reshape(1, len(POOL_WINDOWS), POOL_GROUP_W, POOL_GROUP_W)
    scale = pack[n_grp, :POOL_W].reshape(1, POOL_W)
    g_mix = pack[n_grp + 8].reshape(1, D_MODEL)
    g_mlp = pack[n_grp + 16].reshape(1, D_MODEL)
    g_f = pack[n_grp + 24].reshape(D_MODEL)
    return grp, scale, g_mix, g_mlp, g_f


def _block_diag(grp):
    out = jnp.zeros((POOL_W, POOL_W), grp.dtype)
    for k in range(len(POOL_WINDOWS)):
        out = lax.dynamic_update_slice(out, grp[k], (k * POOL_GROUP_W, k * POOL_GROUP_W))
    return out


def kernel(x, norm_mix_g, w_in, w_att_out, w_pool_grp, pool_scale, w_pool_out, w_out, norm_mlp_g, w_mlp_in, w_mlp_out, norm_final_g, loss_target, m_norm_mix_g, m_w_in, m_w_att_out, m_w_pool_grp, m_pool_scale, m_w_pool_out, m_w_out, m_norm_mlp_g, m_w_mlp_in, m_w_mlp_out, m_norm_final_g, v_norm_mix_g, v_w_in, v_w_att_out, v_w_pool_grp, v_pool_scale, v_w_pool_out, v_w_out, v_norm_mlp_g, v_w_mlp_in, v_w_mlp_out, v_norm_final_g):
    S = x.shape[1]
    xs, target = x[0], loss_target[0]
    big = [w_in[0], w_att_out[0], w_pool_out[0], w_out[0], w_mlp_in[0], w_mlp_out[0]]
    big_m = [m_w_in[0], m_w_att_out[0], m_w_pool_out[0], m_w_out[0], m_w_mlp_in[0], m_w_mlp_out[0]]
    big_v = [v_w_in[0], v_w_att_out[0], v_w_pool_out[0], v_w_out[0], v_w_mlp_in[0], v_w_mlp_out[0]]

    chip = 2 * lax.axis_index("x") + lax.axis_index("y")
    core = lax.axis_index("c")
    place = jnp.stack([chip, core]).astype(jnp.int32)
    names = ("w_in", "w_att_out", "w_pool_out", "w_out", "w_mlp_in", "w_mlp_out")

    shards = [w.astype(BF16) for w in big]
    bufs = [_place_own(sh, N_CHIPS, chip) for sh in shards]
    (wg_in,) = _allgather_weights(shards[:1], bufs[:1])
    wbd = _block_diag(w_pool_grp[0]).astype(BF16)
    g_final = norm_final_g.reshape(1, D_MODEL)
    lane = lax.broadcasted_iota(jnp.int32, (GROUP_W, GROUP_W), 0) // HEAD_W
    head_ones = (lane == lane.T).astype(BF16)

    (u, qkv0, qkv1, qkv2, pz, gates), landed = _norm_inproj(xs, norm_mix_g, wg_in, shards[1:], bufs[1:])
    wg_ao, wg_po, wg_out, wg_mi, wg_mo = _pair_forward(landed, [sh.shape[0] for sh in shards[1:]])
    wg_out = wg_out.reshape(D_MODEL, D_MODEL)
    qkv = (qkv0, qkv1, qkv2)
    att = [_attn_fwd(qkv[grp], grp) for grp in range(3)]
    a, lt0, lt1, lt2, pooled, mixed, p, merged, h1, m = _mixer_out(
        [o for o, _ in att], [l for _, l in att], pz, gates, xs, wg_ao, wg_po, wbd, pool_scale, wg_out, norm_mlp_g)
    hid, dh2, dh2b, loss_lanes, dg_final = _mlp_fwd_loss(m, h1, target, wg_mi, wg_mo, g_final)

    def pair_reduce(grads, grad_names):
        recv = _pair_exchange(grads)
        pair = [_pair_sum(place, g, r, f"pair_sum_{nm}") for g, r, nm in zip(grads, recv, grad_names)]
        return [own for own, _ in pair], [s for _, s in pair]

    def chip_reduce(owns, landed_sums, grad_names):
        return [_chip_sum(place, own, r, f"chip_sum_{nm}") for own, r, nm in zip(owns, landed_sums, grad_names)]

    dpre, dh1, dh1b, dg_mlp = _mlp_bwd(dh2, dh2b, hid, h1, wg_mi, wg_mo, norm_mlp_g)
    own_mlp, sums_mlp = pair_reduce(_wgrads_mlp(m, dpre, hid, dh2b), names[4:])
    (da1, dp1, dgates, da0, dag1, dag2, dd0, dd1, dd2, dmixed, dqp, dscale), landed_mlp = _mixer_bwd(
        dh1b, a, p, mixed, gates, wg_out, wg_ao, wg_po, wbd, pool_scale, head_ones, sums_mlp)
    g_mi, g_mo = chip_reduce(own_mlp, landed_mlp, names[4:])
    grads_mixer, g_bd = _wgrads_mixer(a, da1, p, dp1, merged, dh1b, pooled, dmixed)
    dqkv = [_attn_bwd(qkv[grp], da_g, lt_g, dd_g, grp)
            for grp, (da_g, lt_g, dd_g) in enumerate(((da0, lt0, dd0), (dag1, lt1, dd1), (dag2, lt2, dd2)))]
    dz = _dz_assemble(dqkv, dqp)
    g_in_chunks = _wgrad_in(u, dz, dgates).reshape(N_CHIPS, CHUNKS_PER_SHARD * D_MODEL, CHUNK)
    own_in, sums_in = pair_reduce([g_in_chunks] + grads_mixer, names[:4])
    n_dx_tiles = S // TM
    n_host = (3 * n_dx_tiles) // 4
    dx_part, dg_mix_a, landed_in = _inproj_dx(dz, dgates, dh1, xs, norm_mix_g, wg_in, sums_in,
                                               (0, n_host), "inproj_dx_host")
    dx, dg_mix_b, _ = _inproj_dx(dz, dgates, dh1, xs, norm_mix_g, wg_in, [],
                                 (n_host, n_dx_tiles), "inproj_dx_rest", fill=dx_part)
    dg_mix = dg_mix_a + dg_mix_b
    g_in, g_ao, g_po, g_out = chip_reduce(own_in, landed_in, names[:4])

    g_grp = jnp.stack([g_bd[k * POOL_GROUP_W:(k + 1) * POOL_GROUP_W, k * POOL_GROUP_W:(k + 1) * POOL_GROUP_W]
                       for k in range(len(POOL_WINDOWS))])
    small = _pack_small(g_grp, dscale, dg_mix, dg_mlp, dg_final, loss_lanes)
    full, small_all = _finish_exchange([g_in, g_ao, g_po, g_out, g_mi, g_mo],
                                       _place_own(small, N_DEV, 2 * chip + core))

    zero = jnp.zeros((D_MODEL,), F32)
    small_w = _pack_small(w_pool_grp[0], pool_scale, norm_mix_g, norm_mlp_g, norm_final_g, zero)
    small_m = _pack_small(m_w_pool_grp[0], m_pool_scale, m_norm_mix_g, m_norm_mlp_g, m_norm_final_g, zero)
    small_v = _pack_small(v_w_pool_grp[0], v_pool_scale, v_norm_mix_g, v_norm_mlp_g, v_norm_final_g, zero)
    sg, sd, sm, sv, loss_tile = _small_sum_adamw(small_all, small_w, small_m, small_v)
    full = [full[0].reshape(CHUNKS_PER_SHARD, D_MODEL, CHUNK)] + list(full[1:])
    upd = [_adamw(w, g, mm, vv, f"adamw_{nm}") for w, g, mm, vv, nm in zip(big, full, big_m, big_v, names)]

    def ordered(small_pack, bigs):
        grp, scale, g_mix, g_mlp, g_f = _unpack_small(small_pack)
        b_in, b_ao, b_po, b_out, b_mi, b_mo = [b[None] for b in bigs]
        return (g_mix, b_in, b_ao, grp, scale, b_po, b_out, g_mlp, b_mi, b_mo, g_f)

    return (loss_tile[0, 0], dx[None],
            *ordered(sg, [t[0] for t in upd]),
            *ordered(sd, [t[1] for t in upd]),
            *ordered(sm, [t[2] for t in upd]),
            *ordered(sv, [t[3] for t in upd]))
```

```python
import functools

import jax
import jax.numpy as jnp
from jax import lax
from jax.experimental import pallas as pl
from jax.experimental.pallas import tpu as pltpu

F32 = jnp.float32
BF16 = jnp.bfloat16
SDS = jax.ShapeDtypeStruct
MESH = pl.DeviceIdType.MESH

D_MODEL = 1024
D_FF = 4096
N_CHIPS = 4
N_DEV = 8
DILATIONS = (1, 4, 16)
BAND = 128
GROUP_W = 256
PAIR_W = 128
HEAD_W = 64
POOL_W = 768
POOL_GROUP_W = 192
POOL_WINDOWS = (2, 4, 8, 16)
POOL_HALO = 16
N_IN = 5120
CHUNK = 256
N_CHUNKS = N_IN // CHUNK
N_DZ_CHUNKS = 12
CHUNKS_PER_SHARD = 5
WGRAD_IN_GROUP = 4
NORM_EPS = 1e-6
ALIBI_MAX_BIAS = 8.0
N_HEADS = 12
NEG = -1e30

ADAM_LR, ADAM_B1, ADAM_B2, ADAM_EPS, ADAM_WD, ADAM_STEP = 0.001, 0.9, 0.999, 1e-08, 0.01, 10

TM = 512
TMB = 256
ATT_TILE = ((1, 4), (4, 1), (4, 1))
BK = 4096
ELEMENTWISE_BLOCK = 1 << 20
VMEM_LIMIT = 56 * 1024 * 1024
PACK_ROWS = 184

NT = (((1,), (1,)), ((), ()))
TN = (((0,), (0,)), ((), ()))


def _cp(*sem):
    return pltpu.CompilerParams(dimension_semantics=sem, vmem_limit_bytes=VMEM_LIMIT)


def _resident(shape):
    nd = len(shape)
    return pl.BlockSpec(shape, lambda *_: (0,) * nd, pipeline_mode=pl.Buffered(1))


def _row_block(rows, cap=256):
    return max(b for b in range(16, min(rows, cap) + 1, 16) if rows % b == 0)


def _dot(a, b):
    return jnp.dot(a, b, preferred_element_type=F32)


def _dot_nt(a, b):
    return lax.dot_general(a, b, NT, preferred_element_type=F32)


def _dot_tn(a, b):
    return lax.dot_general(a, b, TN, preferred_element_type=F32)


def _w_in_chunk(w_ref, n):
    return w_ref[n // CHUNKS_PER_SHARD, :, (n % CHUNKS_PER_SHARD) * CHUNK:(n % CHUNKS_PER_SHARD + 1) * CHUNK]


def _sigmoid(x):
    return 0.5 * jnp.tanh(0.5 * x.astype(F32)) + 0.5


def _rms_fwd(x, g):
    r = lax.rsqrt(jnp.mean(x * x, axis=-1, keepdims=True) + NORM_EPS)
    xh = x * r
    return xh * g, xh, r


def _rms_bwd(dy, xh, r, g):
    dxh = dy * g
    return r * (dxh - xh * jnp.mean(dxh * xh, axis=-1, keepdims=True))


def _deinterleave_store(val, s_ref, out_ref, lead, d, rows, dtype):
    if d == 1:
        out_ref[lead + (0,)] = val.astype(dtype)
        return
    for h in range(2):
        s_ref[h] = val[:, h * PAIR_W:(h + 1) * PAIR_W]
    for r in range(d):
        for h in range(2):
            out_ref[lead + (r, slice(None), slice(h * PAIR_W, (h + 1) * PAIR_W))] = (
                s_ref[h, pl.ds(r, rows // d, stride=d), :].astype(dtype))


def _interleave_load(in_ref, lead, s_ref, d, rows):
    for r in range(d):
        for h in range(2):
            s_ref[h, pl.ds(r, rows // d, stride=d), :] = (
                in_ref[lead + (r, slice(None), slice(h * PAIR_W, (h + 1) * PAIR_W))].astype(F32))


def _norm_inproj(x, g, w_in, shards, bufs):
    S = x.shape[0]
    n_tiles = S // TM
    n = len(shards)

    def body(*refs):
        x_ref, g_ref, w_ref = refs[0:3]
        shard_refs = refs[3:3 + n]
        u_ref, q0_ref, q1_ref, q2_ref, pz_ref, gate_ref = refs[3 + 2 * n:9 + 2 * n]
        buf_refs = refs[9 + 2 * n:9 + 3 * n]
        s_ref, send_sem, recv_sem = refs[9 + 3 * n:]
        i = pl.program_id(0)

        def copies():
            return _weight_half_copies(shard_refs, buf_refs, [sh.shape[0] for sh in shards], send_sem, recv_sem)

        @pl.when(i == 0)
        def _():
            for cpy in copies():
                cpy.start()

        u = _rms_fwd(x_ref[...], g_ref[...])[0].astype(BF16)
        u_ref[...] = u
        qkv_refs = (q0_ref, q1_ref, q2_ref)
        for k in range(N_CHUNKS):
            zc = _dot(u, _w_in_chunk(w_ref, k))
            if k < 9:
                which, grp = k // 3, k % 3
                if which == 0:
                    zc = zc * 0.125
                _deinterleave_store(zc, s_ref, qkv_refs[grp], (which,), DILATIONS[grp], TM, BF16)
            elif k < 12:
                pz_ref[:, (k - 9) * CHUNK:(k - 8) * CHUNK] = zc
            else:
                gate_ref[:, (k - 12) * CHUNK:(k - 11) * CHUNK] = zc.astype(BF16)

        @pl.when(i == n_tiles - 1)
        def _():
            for cpy in copies():
                cpy.wait()

    row = lambda w: pl.BlockSpec((TM, w), lambda i: (i, 0))
    res = pl.pallas_call(
        body, grid=(n_tiles,), name="norm_inproj",
        in_specs=[row(D_MODEL), _resident((1, D_MODEL)), _resident(w_in.shape)] + [ANY] * (2 * n),
        out_specs=[row(D_MODEL)]
        + [pl.BlockSpec((3, d, TM // d, GROUP_W), lambda i: (0, 0, i, 0)) for d in DILATIONS]
        + [row(POOL_W), row(2 * D_MODEL)] + [ANY] * n,
        out_shape=[SDS((S, D_MODEL), BF16)]
        + [SDS((3, d, S // d, GROUP_W), BF16) for d in DILATIONS]
        + [SDS((S, POOL_W), F32), SDS((S, 2 * D_MODEL), BF16)] + [SDS(b.shape, b.dtype) for b in bufs],
        scratch_shapes=[pltpu.VMEM((2, TM, PAIR_W), F32), pltpu.SemaphoreType.DMA((3 * n,)),
                        pltpu.SemaphoreType.DMA((3 * n,))],
        input_output_aliases={3 + n + w: 6 + w for w in range(n)},
        compiler_params=_cp("arbitrary"),
    )(x, g, w_in, *shards, *bufs)
    return res[:6], res[6:]


def _band_bias(grp, d):
    row = lax.broadcasted_iota(jnp.int32, (BAND, 2 * BAND), 0)
    col = lax.broadcasted_iota(jnp.int32, (BAND, 2 * BAND), 1)
    steps = BAND + row - col
    valid = (steps >= 0) & (steps <= BAND)
    stepsf = (steps * d).astype(F32)
    biases = []
    for hh in range(4):
        slope = 2.0 ** (-ALIBI_MAX_BIAS * (grp * 4 + hh + 1) / N_HEADS)
        biases.append(jnp.where(valid, -slope * stepsf, NEG))
    return biases, col


def _attn_tiles(grp, L):
    rr, rb = ATT_TILE[grp]
    rb = min(rb, L // BAND)
    return rr, rb, L // (rb * BAND)


def _kv_tile(cur_ref, prev_ref, rr, rb, cs):
    if rb == 0:
        return jnp.concatenate([prev_ref[rr, :, cs], cur_ref[rr, 0:BAND, cs]], axis=0)
    return cur_ref[rr, (rb - 1) * BAND:(rb + 1) * BAND, cs]


def _attn_fwd(qkv, grp):
    d = DILATIONS[grp]
    L = qkv.shape[2]
    RR, RB, nb = _attn_tiles(grp, L)

    def body(q_ref, kc_ref, kp_ref, vc_ref, vp_ref, o_ref, lse_ref):
        i = pl.program_id(0)
        biases, col = _band_bias(grp, d)
        first_keys_ok = (col >= BAND) | (i > 0)
        is_a = lax.broadcasted_iota(jnp.int32, (BAND, PAIR_W), 1) < HEAD_W
        for rr in range(RR):
            for rb in range(RB):
                rows = slice(rb * BAND, (rb + 1) * BAND)
                for cp in range(2):
                    cs = slice(cp * PAIR_W, (cp + 1) * PAIR_W)
                    q2 = q_ref[rr, rows, cs]
                    kcat = _kv_tile(kc_ref, kp_ref, rr, rb, cs)
                    vcat = _kv_tile(vc_ref, vp_ref, rr, rb, cs)
                    res = []
                    for h2 in range(2):
                        sel = is_a if h2 == 0 else jnp.logical_not(is_a)
                        b = biases[cp * 2 + h2]
                        if rb == 0:
                            b = jnp.where(first_keys_ok, b, NEG)
                        s = _dot_nt(jnp.where(sel, q2, jnp.zeros_like(q2)), kcat) + b
                        m = jnp.max(s, axis=-1, keepdims=True)
                        p = jnp.exp(s - m)
                        l = jnp.sum(p, axis=-1, keepdims=True)
                        o = _dot(p.astype(BF16), vcat) * (1.0 / l)
                        res.append((o, m + jnp.log(l)))
                    o_ref[rr, rows, cs] = jnp.where(is_a, res[0][0], res[1][0]).astype(BF16)
                    lse_ref[rr, rows, cs] = jnp.where(is_a, res[0][1], res[1][1])

    cur = lambda w: pl.BlockSpec((None, RR, RB * BAND, GROUP_W), lambda i, j: (w, j, i, 0))
    prev = lambda w: pl.BlockSpec((None, RR, BAND, GROUP_W), lambda i, j: (w, j, jnp.maximum(i * RB - 1, 0), 0))
    out = pl.BlockSpec((RR, RB * BAND, GROUP_W), lambda i, j: (j, i, 0))
    return pl.pallas_call(
        body, grid=(nb, d // RR), name=f"attn_fwd_g{grp}",
        in_specs=[cur(0), cur(1), prev(1), cur(2), prev(2)],
        out_specs=[out, out],
        out_shape=[SDS((d, L, GROUP_W), BF16), SDS((d, L, GROUP_W), F32)],
        compiler_params=_cp("parallel", "parallel"),
    )(qkv, qkv, qkv, qkv, qkv)


def _pool_column_select(col, vals):
    return jnp.where(col < POOL_GROUP_W, vals[0],
                     jnp.where(col < 2 * POOL_GROUP_W, vals[1],
                               jnp.where(col < 3 * POOL_GROUP_W, vals[2], vals[3])))


def _pool_inv_count(i, rows):
    t = i * rows + lax.broadcasted_iota(jnp.int32, (rows, POOL_W), 0)
    col = lax.broadcasted_iota(jnp.int32, (rows, POOL_W), 1)
    win = _pool_column_select(col, POOL_WINDOWS)
    return 1.0 / jnp.minimum(t + 1, win).astype(F32), col


def _mixer_out(outs, lses, pz, gates, x, w_ao, w_po, wbd, scale, w_out, g_mlp):
    S = x.shape[0]
    n_tiles = S // TMB

    def body(o0_ref, l0_ref, o1_ref, l1_ref, o2_ref, l2_ref, pz_ref, halo_ref, gate_ref, x_ref,
             wao_ref, wpo_ref, wbd_ref, sc_ref, wout_ref, g_ref,
             a_ref, lt0_ref, lt1_ref, lt2_ref, pooled_ref, mixed_ref, p_ref, merged_ref, h1_ref, m_ref,
             so1, sl1, so2, sl2, slt, ext_ref):
        i = pl.program_id(0)
        _interleave_load(o1_ref, (), so1, DILATIONS[1], TMB)
        _interleave_load(l1_ref, (), sl1, DILATIONS[1], TMB)
        _interleave_load(o2_ref, (), so2, DILATIONS[2], TMB)
        _interleave_load(l2_ref, (), sl2, DILATIONS[2], TMB)
        for h in range(2):
            hs = slice(h * PAIR_W, (h + 1) * PAIR_W)
            l0, l1, l2 = l0_ref[0, :, hs], sl1[h], sl2[h]
            mx = jnp.maximum(jnp.maximum(l0, l1), l2)
            e0, e1, e2 = jnp.exp(l0 - mx), jnp.exp(l1 - mx), jnp.exp(l2 - mx)
            den = e0 + e1 + e2
            a_ref[:, hs] = ((e0 * o0_ref[0, :, hs].astype(F32) + e1 * so1[h] + e2 * so2[h])
                            * (1.0 / den)).astype(BF16)
            slt[h] = mx + jnp.log(den)
        lt = jnp.concatenate([slt[0], slt[1]], axis=1)
        lt0_ref[0] = lt
        for ref, d in ((lt1_ref, DILATIONS[1]), (lt2_ref, DILATIONS[2])):
            for r in range(d):
                for h in range(2):
                    ref[r, :, h * PAIR_W:(h + 1) * PAIR_W] = slt[h, pl.ds(r, TMB // d, stride=d), :]

        pz_t = pz_ref[...]
        ext_ref[0:POOL_HALO, :] = jnp.where(i > 0, halo_ref[...], 0.0)
        ext_ref[POOL_HALO:, :] = pz_t
        sums = []
        acc = ext_ref[...]
        for k in (1, 2, 4, 8):
            acc = acc + pltpu.roll(acc, k, 0)
            sums.append(acc[POOL_HALO:, :])
        inv_cnt, col = _pool_inv_count(i, TMB)
        pooled = (_pool_column_select(col, sums) * inv_cnt - pz_t).astype(BF16)
        pooled_ref[...] = pooled
        mixed = _dot(pooled, wbd_ref[...])
        mixed_ref[...] = mixed.astype(BF16)
        p = (mixed * sc_ref[...]).astype(BF16)
        p_ref[...] = p

        a = a_ref[...]
        for j in range(N_CHIPS):
            js = slice(j * CHUNK, (j + 1) * CHUNK)
            ga = gate_ref[:, js]
            gp = gate_ref[:, D_MODEL + j * CHUNK:D_MODEL + (j + 1) * CHUNK]
            mj = _sigmoid(ga) * _dot(a, wao_ref[j]) + _sigmoid(gp) * _dot(p, wpo_ref[j])
            merged_ref[:, js] = mj.astype(BF16)
        h1 = x_ref[...] + _dot(merged_ref[...], wout_ref[...])
        h1_ref[...] = h1
        m_ref[...] = _rms_fwd(h1, g_ref[...])[0].astype(BF16)

    row = lambda w: pl.BlockSpec((TMB, w), lambda i: (i, 0))
    grp_spec = lambda d: pl.BlockSpec((d, TMB // d, GROUP_W), lambda i: (0, i, 0))
    halo = pl.BlockSpec((POOL_HALO, POOL_W), lambda i: (jnp.maximum(i * (TMB // POOL_HALO) - 1, 0), 0))
    d0, d1, d2 = DILATIONS
    return pl.pallas_call(
        body, grid=(n_tiles,), name="mixer_out",
        in_specs=[grp_spec(d0), grp_spec(d0), grp_spec(d1), grp_spec(d1), grp_spec(d2), grp_spec(d2),
                  row(POOL_W), halo, row(2 * D_MODEL), row(D_MODEL),
                  _resident(w_ao.shape), _resident(w_po.shape), _resident(wbd.shape), _resident(scale.shape),
                  _resident(w_out.shape), _resident(g_mlp.shape)],
        out_specs=[row(GROUP_W), grp_spec(d0), grp_spec(d1), grp_spec(d2),
                   row(POOL_W), row(POOL_W), row(POOL_W), row(D_MODEL), row(D_MODEL), row(D_MODEL)],
        out_shape=[SDS((S, GROUP_W), BF16)] + [SDS((d, S // d, GROUP_W), F32) for d in DILATIONS]
        + [SDS((S, POOL_W), BF16), SDS((S, POOL_W), BF16), SDS((S, POOL_W), BF16),
           SDS((S, D_MODEL), BF16), SDS((S, D_MODEL), F32), SDS((S, D_MODEL), BF16)],
        scratch_shapes=[pltpu.VMEM((2, TMB, PAIR_W), F32) for _ in range(5)]
        + [pltpu.VMEM((TMB + POOL_HALO, POOL_W), F32)],
        compiler_params=_cp("parallel"),
    )(outs[0], lses[0], outs[1], lses[1], outs[2], lses[2], pz, pz, gates, x,
      w_ao, w_po, wbd, scale, w_out, g_mlp)


def _mlp_fwd_loss(m, h1, target, w_mi, w_mo, g_f):
    S = m.shape[0]

    def body(m_ref, h1_ref, t_ref, wmi_ref, wmo_ref, g_ref, hid_ref, dh2_ref, dh2b_ref, loss_ref, dg_ref):
        @pl.when(pl.program_id(0) == 0)
        def _():
            loss_ref[...] = jnp.zeros_like(loss_ref)
            dg_ref[...] = jnp.zeros_like(dg_ref)

        mt = m_ref[...]
        acc = h1_ref[...]
        for c in range(N_CHIPS):
            hid = jnp.square(jnp.maximum(_dot(mt, wmi_ref[c]), 0.0)).astype(BF16)
            hid_ref[:, c * D_MODEL:(c + 1) * D_MODEL] = hid
            acc = acc + _dot(hid, wmo_ref[c])
        g = g_ref[...]
        y, hh, r = _rms_fwd(acc, g)
        e = y - t_ref[...]
        loss_ref[...] += jnp.sum(e * e, axis=0, keepdims=True)
        dy = e * (1.0 / D_MODEL)
        dg_ref[...] += jnp.sum(dy * hh, axis=0, keepdims=True)
        dh2 = _rms_bwd(dy, hh, r, g)
        dh2_ref[...] = dh2
        dh2b_ref[...] = dh2.astype(BF16)

    row = lambda w: pl.BlockSpec((TM, w), lambda i: (i, 0))
    vec = pl.BlockSpec((1, D_MODEL), lambda i: (0, 0))
    return pl.pallas_call(
        body, grid=(S // TM,), name="mlp_fwd_loss",
        in_specs=[row(D_MODEL), row(D_MODEL), row(D_MODEL), _resident(w_mi.shape), _resident(w_mo.shape),
                  _resident(g_f.shape)],
        out_specs=[row(D_FF), row(D_MODEL), row(D_MODEL), vec, vec],
        out_shape=[SDS((S, D_FF), BF16), SDS((S, D_MODEL), F32), SDS((S, D_MODEL), BF16),
                   SDS((1, D_MODEL), F32), SDS((1, D_MODEL), F32)],
        compiler_params=_cp("arbitrary"),
    )(m, h1, target, w_mi, w_mo, g_f)


def _mlp_bwd(dh2, dh2b, hid, h1, w_mi, w_mo, g_mlp):
    S = dh2.shape[0]

    def body(dh2_ref, dh2b_ref, hid_ref, h1_ref, wmi_ref, wmo_ref, g_ref, dpre_ref, dh1_ref, dh1b_ref, dg_ref):
        @pl.when(pl.program_id(0) == 0)
        def _():
            dg_ref[...] = jnp.zeros_like(dg_ref)

        d2 = dh2b_ref[...]
        dm = jnp.zeros((TM, D_MODEL), F32)
        for c in range(N_CHIPS):
            cs = slice(c * D_MODEL, (c + 1) * D_MODEL)
            dhid = _dot_nt(d2, wmo_ref[c])
            dpre = (dhid * (2.0 * jnp.sqrt(hid_ref[:, cs].astype(F32)))).astype(BF16)
            dpre_ref[:, cs] = dpre
            dm = dm + _dot_nt(dpre, wmi_ref[c])
        g = g_ref[...]
        _, hh, r = _rms_fwd(h1_ref[...], g)
        dg_ref[...] += jnp.sum(dm * hh, axis=0, keepdims=True)
        dh1 = dh2_ref[...] + _rms_bwd(dm, hh, r, g)
        dh1_ref[...] = dh1
        dh1b_ref[...] = dh1.astype(BF16)

    row = lambda w: pl.BlockSpec((TM, w), lambda i: (i, 0))
    return pl.pallas_call(
        body, grid=(S // TM,), name="mlp_bwd",
        in_specs=[row(D_MODEL), row(D_MODEL), row(D_FF), row(D_MODEL), _resident(w_mi.shape),
                  _resident(w_mo.shape), _resident(g_mlp.shape)],
        out_specs=[row(D_FF), row(D_MODEL), row(D_MODEL), pl.BlockSpec((1, D_MODEL), lambda i: (0, 0))],
        out_shape=[SDS((S, D_FF), BF16), SDS((S, D_MODEL), F32), SDS((S, D_MODEL), BF16), SDS((1, D_MODEL), F32)],
        compiler_params=_cp("arbitrary"),
    )(dh2, dh2b, hid, h1, w_mi, w_mo, g_mlp)


def _mixer_bwd(dh1b, a, p, mixed, gates, w_out, w_ao, w_po, wbd, scale, head_ones, sums):
    S = a.shape[0]
    n_tiles = S // TMB
    n = len(sums)

    def body(*refs):
        (dh1b_ref, a_ref, p_ref, mixed_ref, gate_ref, wout_ref, wao_ref, wpo_ref, wbd_ref, sc_ref,
         ones_ref) = refs[0:11]
        sum_refs = refs[11:11 + n]
        (da1_ref, dp1_ref, dgate_ref, da0_ref, dag1_ref, dag2_ref, dd0_ref, dd1_ref, dd2_ref,
         dmixed_ref, dqp_ref, dscale_ref) = refs[11 + n:23 + n]
        land_refs = refs[23 + n:23 + 2 * n]
        s_da, s_dd, send_sem, recv_sem = refs[23 + 2 * n:]
        i = pl.program_id(0)

        @pl.when(i == 0)
        def _():
            dscale_ref[...] = jnp.zeros_like(dscale_ref)
            for cpy in _chip_sum_copies(sum_refs, land_refs, send_sem, recv_sem):
                cpy.start()

        dmerged = _dot_nt(dh1b_ref[...], wout_ref[...])
        a = a_ref[...]
        p = p_ref[...]
        da = jnp.zeros((TMB, GROUP_W), F32)
        dp = jnp.zeros((TMB, POOL_W), F32)
        for j in range(N_CHIPS):
            js = slice(j * CHUNK, (j + 1) * CHUNK)
            sa = _sigmoid(gate_ref[:, js])
            sp = _sigmoid(gate_ref[:, D_MODEL + j * CHUNK:D_MODEL + (j + 1) * CHUNK])
            dmj = dmerged[:, js]
            da1 = (dmj * sa).astype(BF16)
            dp1 = (dmj * sp).astype(BF16)
            da1_ref[:, js] = da1
            dp1_ref[:, js] = dp1
            dgate_ref[j] = (dmj * _dot(a, wao_ref[j]) * sa * (1.0 - sa)).astype(BF16)
            dgate_ref[N_CHIPS + j] = (dmj * _dot(p, wpo_ref[j]) * sp * (1.0 - sp)).astype(BF16)
            da = da + _dot_nt(da1, wao_ref[j])
            dp = dp + _dot_nt(dp1, wpo_ref[j])

        prod = da * a.astype(F32)
        hi = prod.astype(BF16)
        lo = (prod - hi.astype(F32)).astype(BF16)
        dd = _dot(hi, ones_ref[...]) + _dot(lo, ones_ref[...])
        for ref, val, sref, dtype in ((da0_ref, da, s_da, BF16), (dd0_ref, dd, s_dd, F32)):
            ref[0] = val.astype(dtype)
            for h in range(2):
                sref[h] = val[:, h * PAIR_W:(h + 1) * PAIR_W]
        for refs, d in (((dag1_ref, dd1_ref), DILATIONS[1]), ((dag2_ref, dd2_ref), DILATIONS[2])):
            for r in range(d):
                for h in range(2):
                    hs = slice(h * PAIR_W, (h + 1) * PAIR_W)
                    refs[0][r, :, hs] = s_da[h, pl.ds(r, TMB // d, stride=d), :].astype(BF16)
                    refs[1][r, :, hs] = s_dd[h, pl.ds(r, TMB // d, stride=d), :]

        sc = sc_ref[...]
        dscale_ref[...] += jnp.sum(dp * mixed_ref[...].astype(F32), axis=0, keepdims=True)
        dmixed = (dp * sc).astype(BF16)
        dmixed_ref[...] = dmixed
        inv_cnt, _ = _pool_inv_count(i, TMB)
        dqp_ref[...] = (_dot_nt(dmixed, wbd_ref[...]) * inv_cnt).astype(BF16)

        @pl.when(i == n_tiles - 1)
        def _():
            for cpy in _chip_sum_copies(sum_refs, land_refs, send_sem, recv_sem):
                cpy.wait()

    row = lambda w: pl.BlockSpec((TMB, w), lambda i: (i, 0))
    grp_spec = lambda d: pl.BlockSpec((d, TMB // d, GROUP_W), lambda i: (0, i, 0))
    d0, d1, d2 = DILATIONS
    res = pl.pallas_call(
        body, grid=(n_tiles,), name="mixer_bwd",
        in_specs=[row(D_MODEL), row(GROUP_W), row(POOL_W), row(POOL_W), row(2 * D_MODEL),
                  _resident(w_out.shape), _resident(w_ao.shape), _resident(w_po.shape), _resident(wbd.shape),
                  _resident(scale.shape), _resident(head_ones.shape)] + [ANY] * n,
        out_specs=[row(D_MODEL), row(D_MODEL), pl.BlockSpec((2 * N_CHIPS, TMB, CHUNK), lambda i: (0, i, 0)),
                   grp_spec(d0), grp_spec(d1), grp_spec(d2), grp_spec(d0), grp_spec(d1), grp_spec(d2),
                   row(POOL_W), row(POOL_W), pl.BlockSpec((1, POOL_W), lambda i: (0, 0))] + [ANY] * n,
        out_shape=[SDS((S, D_MODEL), BF16), SDS((S, D_MODEL), BF16), SDS((2 * N_CHIPS, S, CHUNK), BF16)]
        + [SDS((d, S // d, GROUP_W), BF16) for d in DILATIONS]
        + [SDS((d, S // d, GROUP_W), F32) for d in DILATIONS]
        + [SDS((S, POOL_W), BF16), SDS((S, POOL_W), BF16), SDS((1, POOL_W), F32)]
        + [SDS(t.shape, t.dtype) for t in sums],
        scratch_shapes=[pltpu.VMEM((2, TMB, PAIR_W), F32), pltpu.VMEM((2, TMB, PAIR_W), F32),
                        pltpu.SemaphoreType.DMA((3 * n,)), pltpu.SemaphoreType.DMA((3 * n,))],
        compiler_params=_cp("arbitrary"),
    )(dh1b, a, p, mixed, gates, w_out, w_ao, w_po, wbd, scale, head_ones, *sums)
    return res[:12], res[12:]


def _attn_bwd(qkv, da, lt, dd, grp):
    d = DILATIONS[grp]
    L = qkv.shape[2]
    RR, RB, nb = _attn_tiles(grp, L)

    def body(q_ref, kc_ref, kp_ref, vc_ref, vp_ref, da_ref, lt_ref, dd_ref, dq_ref, dk_ref, dv_ref, dk_acc, dv_acc):
        i = pl.program_id(1)

        @pl.when(i == 0)
        def _():
            dk_acc[...] = jnp.zeros_like(dk_acc)
            dv_acc[...] = jnp.zeros_like(dv_acc)

        def compute(cur, prv):
            dk_acc[cur] = jnp.zeros((RR, RB * BAND, GROUP_W), F32)
            dv_acc[cur] = jnp.zeros((RR, RB * BAND, GROUP_W), F32)
            biases, col = _band_bias(grp, d)
            first_keys_ok = (col >= BAND) | (i > 0)
            is_a = lax.broadcasted_iota(jnp.int32, (BAND, PAIR_W), 1) < HEAD_W
            for rr in range(RR):
                for rb in range(RB):
                    rows = slice(rb * BAND, (rb + 1) * BAND)
                    for cp in range(2):
                        cs = slice(cp * PAIR_W, (cp + 1) * PAIR_W)
                        q2 = q_ref[rr, rows, cs]
                        da2 = da_ref[rr, rows, cs]
                        lt2 = lt_ref[rr, rows, cs]
                        dd2 = dd_ref[rr, rows, cs]
                        kcat = _kv_tile(kc_ref, kp_ref, rr, rb, cs)
                        vcat = _kv_tile(vc_ref, vp_ref, rr, rb, cs)
                        q2t = q2.astype(F32).T.astype(BF16)
                        da2t = da2.astype(F32).T.astype(BF16)
                        dqs, dkts, dvts = [], [], []
                        for h2 in range(2):
                            sel = is_a if h2 == 0 else jnp.logical_not(is_a)
                            lane0 = h2 * HEAD_W
                            b = biases[cp * 2 + h2]
                            if rb == 0:
                                b = jnp.where(first_keys_ok, b, NEG)
                            s = _dot_nt(jnp.where(sel, q2, jnp.zeros_like(q2)), kcat) + b
                            p = jnp.exp(s - lt2[:, lane0:lane0 + 1])
                            dpv = _dot_nt(jnp.where(sel, da2, jnp.zeros_like(da2)), vcat)
                            ds = (p * (dpv - dd2[:, lane0:lane0 + 1])).astype(BF16)
                            dqs.append(_dot(ds, kcat))
                            dkts.append(_dot(q2t[lane0:lane0 + HEAD_W, :], ds))
                            dvts.append(_dot(da2t[lane0:lane0 + HEAD_W, :], p.astype(BF16)))
                        dq_ref[rr, rows, cs] = (jnp.where(is_a, dqs[0], dqs[1]) * 0.125).astype(BF16)
                        dkc = jnp.concatenate(dkts, axis=0).T
                        dvc = jnp.concatenate(dvts, axis=0).T
                        if rb == 0:
                            last = slice((RB - 1) * BAND, RB * BAND)
                            dk_acc[prv, rr, last, cs] += dkc[0:BAND]
                            dv_acc[prv, rr, last, cs] += dvc[0:BAND]
                            dk_acc[cur, rr, 0:BAND, cs] += dkc[BAND:]
                            dv_acc[cur, rr, 0:BAND, cs] += dvc[BAND:]
                        else:
                            both = slice((rb - 1) * BAND, (rb + 1) * BAND)
                            dk_acc[cur, rr, both, cs] += dkc
                            dv_acc[cur, rr, both, cs] += dvc

        def flush(prv):
            dk_ref[...] = dk_acc[prv].astype(BF16)
            dv_ref[...] = dv_acc[prv].astype(BF16)

        for parity in (0, 1):
            on = (i % 2) == parity
            pl.when(on & (i < nb))(functools.partial(compute, parity, 1 - parity))
            pl.when(on & (i > 0))(functools.partial(flush, 1 - parity))

    qi = lambda i: jnp.minimum(i, nb - 1)
    cur_w = lambda w: pl.BlockSpec((None, RR, RB * BAND, GROUP_W), lambda j, i: (w, j, qi(i), 0))
    prev_w = lambda w: pl.BlockSpec((None, RR, BAND, GROUP_W),
                                    lambda j, i: (w, j, jnp.maximum(qi(i) * RB - 1, 0), 0))
    blk = pl.BlockSpec((RR, RB * BAND, GROUP_W), lambda j, i: (j, qi(i), 0))
    late = pl.BlockSpec((RR, RB * BAND, GROUP_W), lambda j, i: (j, jnp.maximum(i - 1, 0), 0))
    return pl.pallas_call(
        body, grid=(d // RR, nb + 1), name=f"attn_bwd_g{grp}",
        in_specs=[cur_w(0), cur_w(1), prev_w(1), cur_w(2), prev_w(2), blk, blk, blk],
        out_specs=[blk, late, late],
        out_shape=[SDS((d, L, GROUP_W), BF16)] * 3,
        scratch_shapes=[pltpu.VMEM((2, RR, RB * BAND, GROUP_W), F32), pltpu.VMEM((2, RR, RB * BAND, GROUP_W), F32)],
        compiler_params=_cp("parallel", "arbitrary"),
    )(qkv, qkv, qkv, qkv, qkv, da, lt, dd)


def _dz_assemble(dqkv, dqp):
    S = dqp.shape[0]
    n_tiles = S // TMB

    def body(*refs):
        dqkv_refs = refs[0:9]
        dqp_ref, halo_ref = refs[9:11]
        dz_ref, s_ref, ext_ref = refs[11:]
        i = pl.program_id(0)

        for grp in range(3):
            for which in range(3):
                n = which * 3 + grp
                ref = dqkv_refs[grp * 3 + which]
                if DILATIONS[grp] == 1:
                    dz_ref[n] = ref[0]
                else:
                    _interleave_load(ref, (), s_ref, DILATIONS[grp], TMB)
                    for h in range(2):
                        dz_ref[n, :, h * PAIR_W:(h + 1) * PAIR_W] = s_ref[h].astype(BF16)

        dqp = dqp_ref[...].astype(F32)
        ext_ref[0:TMB, :] = dqp
        ext_ref[TMB:, :] = jnp.where(i < n_tiles - 1, halo_ref[...].astype(F32), 0.0)
        sums = []
        acc = ext_ref[...]
        for k in (1, 2, 4, 8):
            acc = acc + pltpu.roll(acc, TMB + POOL_HALO - k, 0)
            sums.append(acc[0:TMB, :])
        inv_cnt, col = _pool_inv_count(i, TMB)
        dpz = _pool_column_select(col, sums) - dqp / inv_cnt
        for t in range(3):
            dz_ref[9 + t] = dpz[:, t * CHUNK:(t + 1) * CHUNK].astype(BF16)

    row = lambda w: pl.BlockSpec((TMB, w), lambda i: (i, 0))
    grp_spec = lambda d: pl.BlockSpec((d, TMB // d, GROUP_W), lambda i: (0, i, 0))
    halo = pl.BlockSpec((POOL_HALO, POOL_W),
                        lambda i: (jnp.minimum((i + 1) * (TMB // POOL_HALO), S // POOL_HALO - 1), 0))
    flat = [t for grp in range(3) for t in dqkv[grp]]
    return pl.pallas_call(
        body, grid=(n_tiles,), name="dz_assemble",
        in_specs=[grp_spec(DILATIONS[grp]) for grp in range(3) for _ in range(3)] + [row(POOL_W), halo],
        out_specs=pl.BlockSpec((N_DZ_CHUNKS, TMB, CHUNK), lambda i: (0, i, 0)),
        out_shape=SDS((N_DZ_CHUNKS, S, CHUNK), BF16),
        scratch_shapes=[pltpu.VMEM((2, TMB, PAIR_W), F32), pltpu.VMEM((TMB + POOL_HALO, POOL_W), F32)],
        compiler_params=_cp("parallel"),
    )(*flat, dqp, dqp)


def _inproj_dx(dz, dgates, dh1, x, g, w_in, sums):
    S = x.shape[0]
    n_tiles = S // TM
    n = len(sums)

    def body(*refs):
        dz_ref, dgate_ref, dh1_ref, x_ref, g_ref, w_ref = refs[0:6]
        sum_refs = refs[6:6 + n]
        dx_ref, dg_ref = refs[6 + n:8 + n]
        land_refs = refs[8 + n:8 + 2 * n]
        sems = refs[8 + 2 * n:]
        i = pl.program_id(0)

        def copies():
            return _chip_sum_copies(sum_refs, land_refs, *sems)

        @pl.when(i == 0)
        def _():
            dg_ref[...] = jnp.zeros_like(dg_ref)
            for cpy in copies():
                cpy.start()

        du = jnp.zeros((TM, D_MODEL), F32)
        for k in range(N_CHUNKS):
            dzk = dz_ref[k] if k < N_DZ_CHUNKS else dgate_ref[k - N_DZ_CHUNKS]
            du = du + _dot_nt(dzk, _w_in_chunk(w_ref, k))
        gv = g_ref[...]
        _, xh, r = _rms_fwd(x_ref[...], gv)
        dg_ref[...] += jnp.sum(du * xh, axis=0, keepdims=True)
        dx_ref[...] = dh1_ref[...] + _rms_bwd(du, xh, r, gv)

        @pl.when(i == n_tiles - 1)
        def _():
            for cpy in copies():
                cpy.wait()

    row = lambda w: pl.BlockSpec((TM, w), lambda i: (i, 0))
    res = pl.pallas_call(
        body, grid=(n_tiles,), name="inproj_dx",
        in_specs=[pl.BlockSpec((N_DZ_CHUNKS, TM, CHUNK), lambda i: (0, i, 0)),
                  pl.BlockSpec((N_CHUNKS - N_DZ_CHUNKS, TM, CHUNK), lambda i: (0, i, 0)),
                  row(D_MODEL), row(D_MODEL), _resident(g.shape), _resident(w_in.shape)] + [ANY] * n,
        out_specs=[row(D_MODEL), pl.BlockSpec((1, D_MODEL), lambda i: (0, 0))] + [ANY] * n,
        out_shape=[SDS((S, D_MODEL), F32), SDS((1, D_MODEL), F32)] + [SDS(t.shape, t.dtype) for t in sums],
        scratch_shapes=[pltpu.SemaphoreType.DMA((3 * n,)), pltpu.SemaphoreType.DMA((3 * n,))],
        compiler_params=_cp("arbitrary"),
    )(dz, dgates, dh1, x, g, w_in, *sums)
    return res[0], res[1], res[2:]


def _wgrad(a, b, name, *, out_shape, a_spec, b_spec, out_spec, grid, n_out_cols=None, fill=None, narrow=True):
    k_axis = len(grid) - 1
    n_k = grid[k_axis]
    n_out = 2 if narrow else 1

    def body(a_ref, b_ref, *rest):
        o_ref = rest[-n_out]

        @pl.when(pl.program_id(k_axis) == 0)
        def _():
            o_ref[...] = jnp.zeros_like(o_ref)

        at = a_ref[...]
        if n_out_cols is None:
            o_ref[...] += _dot_tn(at, b_ref[...])
        elif n_out_cols[0] == "lead_both":
            for t in range(b_ref.shape[0]):
                o_ref[t] += _dot_tn(at, b_ref[t])
        else:
            w = n_out_cols[1]
            for t in range(o_ref.shape[0]):
                o_ref[t] += _dot_tn(at, b_ref[:, t * w:(t + 1) * w])

        if narrow:
            @pl.when(pl.program_id(k_axis) == n_k - 1)
            def _():
                rest[-1][...] = o_ref[...].astype(BF16)

    sem = ("parallel",) * k_axis + ("arbitrary",)
    extra = [] if fill is None else list(fill) if narrow else [fill]
    shapes = [out_shape, SDS(out_shape.shape, BF16)] if narrow else out_shape
    return pl.pallas_call(body, grid=grid, name=name, in_specs=[a_spec, b_spec] + [ANY] * len(extra),
                          out_specs=[out_spec] * n_out if narrow else out_spec, out_shape=shapes,
                          input_output_aliases={2 + t: t for t in range(len(extra))},
                          compiler_params=_cp(*sem))(a, b, *extra)


def _wgrad_in(u, dz, dgates):
    bk = min(BK, u.shape[0])
    nk = u.shape[0] // bk
    g = WGRAD_IN_GROUP
    kw = dict(n_out_cols=("lead_both", CHUNK), a_spec=pl.BlockSpec((bk, D_MODEL), lambda j, k: (k, 0)),
              b_spec=pl.BlockSpec((g, bk, CHUNK), lambda j, k: (j, k, 0)),
              out_shape=SDS((N_CHUNKS, D_MODEL, CHUNK), F32))
    first = _wgrad(u, dz, "wgrad_in_qkvp", grid=(N_DZ_CHUNKS // g, nk),
                   out_spec=pl.BlockSpec((g, D_MODEL, CHUNK), lambda j, k: (j, 0, 0)), **kw)
    both = _wgrad(u, dgates, "wgrad_in_gates", grid=((N_CHUNKS - N_DZ_CHUNKS) // g, nk), fill=first,
                  out_spec=pl.BlockSpec((g, D_MODEL, CHUNK), lambda j, k: (N_DZ_CHUNKS // g + j, 0, 0)), **kw)
    return [t.reshape(N_CHIPS, CHUNKS_PER_SHARD * D_MODEL, CHUNK) for t in both]


def _wgrads_mixer(a, da1, p, dp1, merged, dh1b, pooled, dmixed):
    bk = min(BK, a.shape[0])
    nk = a.shape[0] // bk
    g_ao = _wgrad(
        a, da1, "wgrad_att_out", grid=(nk,), n_out_cols=("cols_b", CHUNK),
        a_spec=pl.BlockSpec((bk,GROUP_W), lambda k: (k, 0)),
        b_spec=pl.BlockSpec((bk,D_MODEL), lambda k: (k, 0)),
        out_spec=pl.BlockSpec((N_CHIPS, GROUP_W, CHUNK), lambda k: (0, 0, 0)),
        out_shape=SDS((N_CHIPS, GROUP_W, CHUNK), F32))
    g_po = _wgrad(
        p, dp1, "wgrad_pool_out", grid=(nk,), n_out_cols=("cols_b", CHUNK),
        a_spec=pl.BlockSpec((bk,POOL_W), lambda k: (k, 0)),
        b_spec=pl.BlockSpec((bk,D_MODEL), lambda k: (k, 0)),
        out_spec=pl.BlockSpec((N_CHIPS, POOL_W, CHUNK), lambda k: (0, 0, 0)),
        out_shape=SDS((N_CHIPS, POOL_W, CHUNK), F32))
    g_out = _wgrad(
        merged, dh1b, "wgrad_out", grid=(nk,),
        a_spec=pl.BlockSpec((bk,D_MODEL), lambda k: (k, 0)),
        b_spec=pl.BlockSpec((bk,D_MODEL), lambda k: (k, 0)),
        out_spec=pl.BlockSpec((D_MODEL, D_MODEL), lambda k: (0, 0)),
        out_shape=SDS((D_MODEL, D_MODEL), F32))
    g_bd = _wgrad(
        pooled, dmixed, "wgrad_pool_grp", grid=(nk,),
        a_spec=pl.BlockSpec((bk,POOL_W), lambda k: (k, 0)),
        b_spec=pl.BlockSpec((bk,POOL_W), lambda k: (k, 0)),
        out_spec=pl.BlockSpec((POOL_W, POOL_W), lambda k: (0, 0)),
        out_shape=SDS((POOL_W, POOL_W), F32), narrow=False)
    g_out = [t.reshape(N_CHIPS, D_MODEL // N_CHIPS, D_MODEL) for t in g_out]
    return [g_ao, g_po, g_out], g_bd


def _wgrads_mlp(m, dpre, hid, dh2b):
    bk = min(BK, m.shape[0])
    nk = m.shape[0] // bk
    g_mi = _wgrad(
        m, dpre, "wgrad_mlp_in", grid=(N_CHIPS, nk),
        a_spec=pl.BlockSpec((bk,D_MODEL), lambda c, k: (k, 0)),
        b_spec=pl.BlockSpec((bk,D_MODEL), lambda c, k: (k, c)),
        out_spec=pl.BlockSpec((None, D_MODEL, D_MODEL), lambda c, k: (c, 0, 0)),
        out_shape=SDS((N_CHIPS, D_MODEL, D_MODEL), F32))
    g_mo = _wgrad(
        hid, dh2b, "wgrad_mlp_out", grid=(N_CHIPS, nk),
        a_spec=pl.BlockSpec((bk,D_MODEL), lambda c, k: (k, c)),
        b_spec=pl.BlockSpec((bk,D_MODEL), lambda c, k: (k, 0)),
        out_spec=pl.BlockSpec((None, D_MODEL, D_MODEL), lambda c, k: (c, 0, 0)),
        out_shape=SDS((N_CHIPS, D_MODEL, D_MODEL), F32))
    return [g_mi, g_mo]


def _mesh_place():
    x, y, c = lax.axis_index("x"), lax.axis_index("y"), lax.axis_index("c")
    other_chips = [(x, 1 - y), (1 - x, y), (1 - x, 1 - y)]
    return x, y, c, other_chips


ANY = pl.BlockSpec(memory_space=pl.ANY)


def _weight_half_copies(shard_refs, buf_refs, rows, send_sem, recv_sem):
    x, y, c, chips = _mesh_place()
    me = 2 * x + y
    copies = []
    for w, r_full in enumerate(rows):
        rh = r_full // 2
        for r, (px, py) in enumerate(chips):
            k = w * 3 + r
            copies.append(pltpu.make_async_remote_copy(
                src_ref=shard_refs[w].at[pl.ds(c * rh, rh), :], dst_ref=buf_refs[w].at[me, pl.ds(c * rh, rh), :],
                send_sem=send_sem.at[k], recv_sem=recv_sem.at[k], device_id=(px, py, c), device_id_type=MESH))
    return copies


def _pair_forward_copies(buf_refs, rows, send_sem, recv_sem):
    x, y, c, chips = _mesh_place()
    out = []
    for w, r_full in enumerate(rows):
        rh = r_full // 2
        for r, (px, py) in enumerate(chips):
            k = w * 3 + r
            landed = buf_refs[w].at[2 * px + py, pl.ds(c * rh, rh), :]
            theirs = buf_refs[w].at[2 * px + py, pl.ds((1 - c) * rh, rh), :]
            mk = lambda ref: pltpu.make_async_remote_copy(
                src_ref=ref, dst_ref=ref, send_sem=send_sem.at[k], recv_sem=recv_sem.at[k],
                device_id=(x, y, 1 - c), device_id_type=MESH)
            out.append((mk(landed), mk(theirs)))
    return out


def _place_own(block, n_slots, slot):
    buf = lax.empty((n_slots,) + block.shape, block.dtype)
    return lax.dynamic_update_slice(buf, block[None], (slot,) + (0,) * block.ndim)


def _allgather_weights(shards, bufs):
    n = len(shards)
    rows = [sh.shape[0] for sh in shards]

    def body(*refs):
        src, dst = refs[:n], refs[2 * n:3 * n]
        send_sem, recv_sem, fsend_sem, frecv_sem = refs[3 * n:]
        sends = _weight_half_copies(src, dst, rows, send_sem, recv_sem)
        for cpy in sends:
            cpy.start()
        fwds = _pair_forward_copies(dst, rows, fsend_sem, frecv_sem)
        for cpy, (fwd, _) in zip(sends, fwds):
            cpy.wait_recv()
            fwd.start()
        for _, landing in fwds:
            landing.wait_recv()
        for cpy in sends + [f for f, _ in fwds]:
            cpy.wait_send()

    return pl.pallas_call(
        body, name="allgather_w_in",
        in_specs=[ANY] * (2 * n), out_specs=[ANY] * n,
        out_shape=[SDS(b.shape, b.dtype) for b in bufs],
        scratch_shapes=[pltpu.SemaphoreType.DMA((3 * n,))] * 4,
        input_output_aliases={n + w: w for w in range(n)},
    )(*shards, *bufs)


def _pair_forward(bufs, rows):
    n = len(bufs)

    def body(*refs):
        dst = refs[n:2 * n]
        send_sem, recv_sem = refs[2 * n:]
        fwds = _pair_forward_copies(dst, rows, send_sem, recv_sem)
        for fwd, _ in fwds:
            fwd.start()
        for fwd, landing in fwds:
            landing.wait_recv()
            fwd.wait_send()

    return pl.pallas_call(
        body, name="weights_pair_forward",
        in_specs=[ANY] * n, out_specs=[ANY] * n,
        out_shape=[SDS(b.shape, b.dtype) for b in bufs],
        scratch_shapes=[pltpu.SemaphoreType.DMA((3 * n,))] * 2,
        input_output_aliases={w: w for w in range(n)},
    )(*bufs)


def _chip_sum_copies(src, dst, send_sem, recv_sem):
    x, y, c, chips = _mesh_place()
    copies = []
    for w in range(len(src)):
        for r, (px, py) in enumerate(chips):
            k = w * 3 + r
            copies.append(pltpu.make_async_remote_copy(
                src_ref=src[w].at[r + 1], dst_ref=dst[w].at[r + 1], send_sem=send_sem.at[k], recv_sem=recv_sem.at[k],
                device_id=(px, py, c), device_id_type=MESH))
    return copies


def _pair_exchange(grads):
    n = len(grads)

    def body(*refs):
        src, dst = refs[:n], refs[n:2 * n]
        send_sem, recv_sem = refs[2 * n:]
        x, y, c, _ = _mesh_place()
        copies = []
        for w in range(n):
            rh = grads[w].shape[1] // 2
            copies.append(pltpu.make_async_remote_copy(
                src_ref=src[w].at[:, pl.ds((1 - c) * rh, rh), :], dst_ref=dst[w],
                send_sem=send_sem.at[w], recv_sem=recv_sem.at[w],
                device_id=(x, y, 1 - c), device_id_type=MESH))
            copies[-1].start()
        for cpy in copies:
            cpy.wait()

    return pl.pallas_call(
        body, name="grad_pair_exchange",
        in_specs=[ANY] * n, out_specs=[ANY] * n,
        out_shape=[SDS((N_CHIPS, g.shape[1] // 2, g.shape[2]), g.dtype) for g in grads],
        scratch_shapes=[pltpu.SemaphoreType.DMA((n,)), pltpu.SemaphoreType.DMA((n,))],
    )(*grads)


def _pair_sum(place, grad, recv, name):
    _, R, C = grad.shape
    rh = R // 2
    br = _row_block(rh, max(256, ELEMENTWISE_BLOCK // C))
    nbh = rh // br

    def body(place_ref, g_ref, r_ref, own_ref, sums_ref):
        s = g_ref[...] + r_ref[...].astype(F32)

        @pl.when(pl.program_id(1) == 0)
        def _():
            own_ref[...] = s

        sums_ref[...] = s.astype(BF16)

    slot = lambda rel, pr: jnp.bitwise_xor(pr[0], rel)
    return pl.pallas_call(
        body, name=name,
        grid_spec=pltpu.PrefetchScalarGridSpec(
            num_scalar_prefetch=1, grid=(nbh, N_CHIPS),
            in_specs=[pl.BlockSpec((None, br, C), lambda i, rel, pr: (slot(rel, pr), pr[1] * nbh + i, 0)),
                      pl.BlockSpec((None, br, C), lambda i, rel, pr: (slot(rel, pr), i, 0))],
            out_specs=[pl.BlockSpec((br, C), lambda i, rel, pr: (i, 0)),
                       pl.BlockSpec((None, br, C), lambda i, rel, pr: (rel, i, 0))]),
        out_shape=[SDS((rh, C), F32), SDS((N_CHIPS, rh, C), BF16)],
        compiler_params=_cp("parallel", "arbitrary"),
    )(place, grad, recv)


def _chip_sum(place, own, recv, name):
    rh, C = own.shape
    br = _row_block(rh, max(256, ELEMENTWISE_BLOCK // C))
    nbh = rh // br

    def body(place_ref, own_ref, r_ref, o_ref):
        o_ref[...] = ((own_ref[...] + r_ref[1].astype(F32)) + r_ref[2].astype(F32)) + r_ref[3].astype(F32)

    return pl.pallas_call(
        body, name=name,
        grid_spec=pltpu.PrefetchScalarGridSpec(
            num_scalar_prefetch=1, grid=(nbh,),
            in_specs=[pl.BlockSpec((br, C), lambda i, pr: (i, 0)),
                      pl.BlockSpec((N_CHIPS, br, C), lambda i, pr: (0, i, 0))],
            out_specs=pl.BlockSpec((br, C), lambda i, pr: (pr[1] * nbh + i, 0))),
        out_shape=SDS((2 * rh, C), F32),
        compiler_params=_cp("parallel"),
    )(place, own, recv)


def _finish_exchange(grads, small_all):
    n = len(grads)

    def body(*refs):
        dst, all_ref = refs[n + 1:2 * n + 1], refs[2 * n + 1]
        send_sem, recv_sem, ssend_sem, srecv_sem = refs[2 * n + 2:]
        x, y, c, chips = _mesh_place()
        sib = (x, y, 1 - c)

        def pack(dev, k, to):
            slot = 4 * dev[0] + 2 * dev[1] + dev[2]
            return pltpu.make_async_remote_copy(
                src_ref=all_ref.at[slot], dst_ref=all_ref.at[slot], send_sem=ssend_sem.at[k],
                recv_sem=srecv_sem.at[k], device_id=to, device_id_type=MESH)

        pack_copies = [pack((x, y, c), 0, sib)] + [pack((x, y, c), 1 + r, (px, py, c))
                                                   for r, (px, py) in enumerate(chips)]
        for cpy in pack_copies:
            cpy.start()
        sends, landings = [], []
        for w in range(n):
            rh = grads[w].shape[0] // 2
            mk = lambda cc: pltpu.make_async_remote_copy(
                src_ref=dst[w].at[pl.ds(cc * rh, rh), :], dst_ref=dst[w].at[pl.ds(cc * rh, rh), :],
                send_sem=send_sem.at[w], recv_sem=recv_sem.at[w], device_id=(x, y, 1 - c), device_id_type=MESH)
            sends.append(mk(c))
            landings.append(mk(1 - c))
            sends[-1].start()
        for r, (px, py) in enumerate(chips):
            pack((px, py, c), 1 + r, (px, py, c)).wait_recv()
            pack_copies.append(pack((px, py, c), 4 + r, sib))
            pack_copies[-1].start()
        pack(sib, 0, sib).wait_recv()
        for r, (px, py) in enumerate(chips):
            pack((px, py, 1 - c), 4 + r, sib).wait_recv()
        for cpy in landings:
            cpy.wait_recv()
        for cpy in sends + pack_copies:
            cpy.wait_send()

    res = pl.pallas_call(
        body, name="grad_finish_exchange",
        in_specs=[ANY] * (n + 1), out_specs=[ANY] * (n + 1),
        out_shape=[SDS(g.shape, g.dtype) for g in grads] + [SDS(small_all.shape, small_all.dtype)],
        scratch_shapes=[pltpu.SemaphoreType.DMA((n,)), pltpu.SemaphoreType.DMA((n,)),
                        pltpu.SemaphoreType.DMA((N_DEV - 1,)), pltpu.SemaphoreType.DMA((N_DEV - 1,))],
        input_output_aliases={w: w for w in range(n + 1)},
    )(*grads, small_all)
    return res[:n], res[n]


def _adamw_math(w, g, m, v):
    m = ADAM_B1 * m + (1.0 - ADAM_B1) * g
    v = ADAM_B2 * v + (1.0 - ADAM_B2) * jnp.square(g)
    m_hat = m / (1.0 - ADAM_B1 ** ADAM_STEP)
    v_hat = v / (1.0 - ADAM_B2 ** ADAM_STEP)
    delta = -ADAM_LR * (m_hat / (jnp.sqrt(v_hat) + ADAM_EPS) + ADAM_WD * w)
    return delta, m, v


def _adamw(w, g, m, v, name):
    R, C = w.shape
    br = _row_block(R, 512)
    if g.ndim == 3:
        n_chunks, cw = g.shape[0], g.shape[2]
        g_spec = pl.BlockSpec((None, br, cw), lambda t, i: (t, i, 0))
    else:
        n_chunks, cw = 1, C
        g_spec = pl.BlockSpec((br, cw), lambda t, i: (i, t))

    def body(w_ref, g_ref, m_ref, v_ref, g_out_ref, d_ref, nm_ref, nv_ref):
        gv = g_ref[...]
        g_out_ref[...] = gv
        d_ref[...], nm_ref[...], nv_ref[...] = _adamw_math(w_ref[...], gv, m_ref[...], v_ref[...])

    spec = pl.BlockSpec((br, cw), lambda t, i: (i, t))
    return pl.pallas_call(
        body, grid=(n_chunks, R // br), name=name, in_specs=[spec, g_spec, spec, spec], out_specs=[spec] * 4,
        out_shape=[SDS((R, C), F32)] * 4, compiler_params=_cp("parallel", "parallel"),
    )(w, g, m, v)


def _small_sum_adamw(all_small, w, m, v):
    loss_row = PACK_ROWS - 8

    def body(all_ref, w_ref, m_ref, v_ref, g_ref, d_ref, nm_ref, nv_ref, loss_ref):
        g = all_ref[0]
        for k in range(1, N_DEV):
            g = g + all_ref[k]
        g_ref[...] = g
        d_ref[...], nm_ref[...], nv_ref[...] = _adamw_math(w_ref[...], g, m_ref[...], v_ref[...])
        total = jnp.sum(g[loss_row:loss_row + 1, :]) * (0.5 / D_MODEL)
        loss_ref[...] = jnp.full(loss_ref.shape, total, F32)

    full = lambda s: pl.BlockSpec(s, lambda i: (0,) * len(s))
    pack = (PACK_ROWS, D_MODEL)
    return pl.pallas_call(
        body, grid=(1,), name="small_sum_adamw",
        in_specs=[full((N_DEV,) + pack), full(pack), full(pack), full(pack)],
        out_specs=[full(pack)] * 4 + [full((8, 128))],
        out_shape=[SDS(pack, F32)] * 4 + [SDS((8, 128), F32)],
        compiler_params=_cp("arbitrary"),
    )(all_small, w, m, v)


def _pack_small(grp, scale, g_mix, g_mlp, g_f, loss_lanes):
    def part(vec):
        vec = vec.reshape(1, -1)
        return jnp.pad(vec, ((0, 7), (0, D_MODEL - vec.shape[1])))
    return jnp.concatenate([grp.reshape(-1, D_MODEL), part(scale), part(g_mix), part(g_mlp), part(g_f),
                            part(loss_lanes)], axis=0)


def _unpack_small(pack):
    n_grp = len(POOL_WINDOWS) * POOL_GROUP_W * POOL_GROUP_W // D_MODEL
    grp = pack[:n_grp].reshape(1, len(POOL_WINDOWS), POOL_GROUP_W, POOL_GROUP_W)
    scale = pack[n_grp, :POOL_W].reshape(1, POOL_W)
    g_mix = pack[n_grp + 8].reshape(1, D_MODEL)
    g_mlp = pack[n_grp + 16].reshape(1, D_MODEL)
    g_f = pack[n_grp + 24].reshape(D_MODEL)
    return grp, scale, g_mix, g_mlp, g_f


def _block_diag(grp):
    out = jnp.zeros((POOL_W, POOL_W), grp.dtype)
    for k in range(len(POOL_WINDOWS)):
        out = lax.dynamic_update_slice(out, grp[k], (k * POOL_GROUP_W, k * POOL_GROUP_W))
    return out


def kernel(x, norm_mix_g, w_in, w_att_out, w_pool_grp, pool_scale, w_pool_out, w_out, norm_mlp_g, w_mlp_in, w_mlp_out, norm_final_g, loss_target, m_norm_mix_g, m_w_in, m_w_att_out, m_w_pool_grp, m_pool_scale, m_w_pool_out, m_w_out, m_norm_mlp_g, m_w_mlp_in, m_w_mlp_out, m_norm_final_g, v_norm_mix_g, v_w_in, v_w_att_out, v_w_pool_grp, v_pool_scale, v_w_pool_out, v_w_out, v_norm_mlp_g, v_w_mlp_in, v_w_mlp_out, v_norm_final_g):
    S = x.shape[1]
    xs, target = x[0], loss_target[0]
    big = [w_in[0], w_att_out[0], w_pool_out[0], w_out[0], w_mlp_in[0], w_mlp_out[0]]
    big_m = [m_w_in[0], m_w_att_out[0], m_w_pool_out[0], m_w_out[0], m_w_mlp_in[0], m_w_mlp_out[0]]
    big_v = [v_w_in[0], v_w_att_out[0], v_w_pool_out[0], v_w_out[0], v_w_mlp_in[0], v_w_mlp_out[0]]

    chip = 2 * lax.axis_index("x") + lax.axis_index("y")
    core = lax.axis_index("c")
    place = jnp.stack([chip, core]).astype(jnp.int32)
    names = ("w_in", "w_att_out", "w_pool_out", "w_out", "w_mlp_in", "w_mlp_out")

    shards = [w.astype(BF16) for w in big]
    bufs = [_place_own(sh, N_CHIPS, chip) for sh in shards]
    (wg_in,) = _allgather_weights(shards[:1], bufs[:1])
    wbd = _block_diag(w_pool_grp[0]).astype(BF16)
    g_final = norm_final_g.reshape(1, D_MODEL)
    lane = lax.broadcasted_iota(jnp.int32, (GROUP_W, GROUP_W), 0) // HEAD_W
    head_ones = (lane == lane.T).astype(BF16)

    (u, qkv0, qkv1, qkv2, pz, gates), landed = _norm_inproj(xs, norm_mix_g, wg_in, shards[1:], bufs[1:])
    wg_ao, wg_po, wg_out, wg_mi, wg_mo = _pair_forward(landed, [sh.shape[0] for sh in shards[1:]])
    wg_out = wg_out.reshape(D_MODEL, D_MODEL)
    qkv = (qkv0, qkv1, qkv2)
    att = [_attn_fwd(qkv[grp], grp) for grp in range(3)]
    a, lt0, lt1, lt2, pooled, mixed, p, merged, h1, m = _mixer_out(
        [o for o, _ in att], [l for _, l in att], pz, gates, xs, wg_ao, wg_po, wbd, pool_scale, wg_out, norm_mlp_g)
    hid, dh2, dh2b, loss_lanes, dg_final = _mlp_fwd_loss(m, h1, target, wg_mi, wg_mo, g_final)

    def pair_reduce(grads, grad_names):
        recv = _pair_exchange([narrow for _, narrow in grads])
        pair = [_pair_sum(place, g, r, f"pair_sum_{nm}") for (g, _), r, nm in zip(grads, recv, grad_names)]
        return [own for own, _ in pair], [s for _, s in pair]

    def chip_reduce(owns, landed_sums, grad_names):
        return [_chip_sum(place, own, r, f"chip_sum_{nm}") for own, r, nm in zip(owns, landed_sums, grad_names)]

    dpre, dh1, dh1b, dg_mlp = _mlp_bwd(dh2, dh2b, hid, h1, wg_mi, wg_mo, norm_mlp_g)
    own_mlp, sums_mlp = pair_reduce(_wgrads_mlp(m, dpre, hid, dh2b), names[4:])
    (da1, dp1, dgates, da0, dag1, dag2, dd0, dd1, dd2, dmixed, dqp, dscale), landed_mlp = _mixer_bwd(
        dh1b, a, p, mixed, gates, wg_out, wg_ao, wg_po, wbd, pool_scale, head_ones, sums_mlp)
    g_mi, g_mo = chip_reduce(own_mlp, landed_mlp, names[4:])
    grads_mixer, g_bd = _wgrads_mixer(a, da1, p, dp1, merged, dh1b, pooled, dmixed)
    dqkv = [_attn_bwd(qkv[grp], da_g, lt_g, dd_g, grp)
            for grp, (da_g, lt_g, dd_g) in enumerate(((da0, lt0, dd0), (dag1, lt1, dd1), (dag2, lt2, dd2)))]
    dz = _dz_assemble(dqkv, dqp)
    own_in, sums_in = pair_reduce([_wgrad_in(u, dz, dgates)] + grads_mixer, names[:4])
    dx, dg_mix, landed_in = _inproj_dx(dz, dgates, dh1, xs, norm_mix_g, wg_in, sums_in)
    g_in, g_ao, g_po, g_out = chip_reduce(own_in, landed_in, names[:4])

    g_grp = jnp.stack([g_bd[k * POOL_GROUP_W:(k + 1) * POOL_GROUP_W, k * POOL_GROUP_W:(k + 1) * POOL_GROUP_W]
                       for k in range(len(POOL_WINDOWS))])
    small = _pack_small(g_grp, dscale, dg_mix, dg_mlp, dg_final, loss_lanes)
    full, small_all = _finish_exchange([g_in, g_ao, g_po, g_out, g_mi, g_mo],
                                       _place_own(small, N_DEV, 2 * chip + core))

    zero = jnp.zeros((D_MODEL,), F32)
    small_w = _pack_small(w_pool_grp[0], pool_scale, norm_mix_g, norm_mlp_g, norm_final_g, zero)
    small_m = _pack_small(m_w_pool_grp[0], m_pool_scale, m_norm_mix_g, m_norm_mlp_g, m_norm_final_g, zero)
    small_v = _pack_small(v_w_pool_grp[0], v_pool_scale, v_norm_mix_g, v_norm_mlp_g, v_norm_final_g, zero)
    sg, sd, sm, sv, loss_tile = _small_sum_adamw(small_all, small_w, small_m, small_v)
    full = [full[0].reshape(CHUNKS_PER_SHARD, D_MODEL, CHUNK)] + list(full[1:])
    upd = [_adamw(w, g, mm, vv, f"adamw_{nm}") for w, g, mm, vv, nm in zip(big, full, big_m, big_v, names)]

    def ordered(small_pack, bigs):
        grp, scale, g_mix, g_mlp, g_f = _unpack_small(small_pack)
        b_in, b_ao, b_po, b_out, b_mi, b_mo = [b[None] for b in bigs]
        return (g_mix, b_in, b_ao, grp, scale, b_po, b_out, g_mlp, b_mi, b_mo, g_f)

    return (loss_tile[0, 0], dx[None],
            *ordered(sg, [t[0] for t in upd]),
            *ordered(sd, [t[1] for t in upd]),
            *ordered(sm, [t[2] for t in upd]),
            *ordered(sv, [t[3] for t in upd]))
```

```python
import functools

import jax
import jax.numpy as jnp
from jax import lax
from jax.experimental import pallas as pl
from jax.experimental.pallas import tpu as pltpu

F32 = jnp.float32
BF16 = jnp.bfloat16
SDS = jax.ShapeDtypeStruct
MESH = pl.DeviceIdType.MESH

D_MODEL = 1024
D_FF = 4096
N_CHIPS = 4
N_DEV = 8
DILATIONS = (1, 4, 16)
BAND = 128
GROUP_W = 256
PAIR_W = 128
HEAD_W = 64
POOL_W = 768
POOL_GROUP_W = 192
POOL_WINDOWS = (2, 4, 8, 16)
POOL_HALO = 16
N_IN = 5120
CHUNK = 256
N_CHUNKS = N_IN // CHUNK
N_DZ_CHUNKS = 12
CHUNKS_PER_SHARD = 5
WGRAD_IN_GROUP = 4
NORM_EPS = 1e-6
ALIBI_MAX_BIAS = 8.0
N_HEADS = 12
NEG = -1e30

ADAM_LR, ADAM_B1, ADAM_B2, ADAM_EPS, ADAM_WD, ADAM_STEP = 0.001, 0.9, 0.999, 1e-08, 0.01, 10

TM = 512
TMB = 512
ATT_TILE = ((1, 4), (4, 1), (4, 1))
BK = 4096
ELEMENTWISE_BLOCK = 1 << 20
VMEM_LIMIT = 56 * 1024 * 1024
PACK_ROWS = 184

NT = (((1,), (1,)), ((), ()))
TN = (((0,), (0,)), ((), ()))


def _cp(*sem):
    return pltpu.CompilerParams(dimension_semantics=sem, vmem_limit_bytes=VMEM_LIMIT)


def _resident(shape):
    nd = len(shape)
    return pl.BlockSpec(shape, lambda *_: (0,) * nd, pipeline_mode=pl.Buffered(1))


def _row_block(rows, cap=256):
    return max(b for b in range(16, min(rows, cap) + 1, 16) if rows % b == 0)


def _dot(a, b):
    return jnp.dot(a, b, preferred_element_type=F32)


def _dot_nt(a, b):
    return lax.dot_general(a, b, NT, preferred_element_type=F32)


def _dot_tn(a, b):
    return lax.dot_general(a, b, TN, preferred_element_type=F32)


def _w_in_chunk(w_ref, n):
    return w_ref[n // CHUNKS_PER_SHARD, :, (n % CHUNKS_PER_SHARD) * CHUNK:(n % CHUNKS_PER_SHARD + 1) * CHUNK]


def _sigmoid(x):
    return 0.5 * jnp.tanh(0.5 * x.astype(F32)) + 0.5


def _rms_fwd(x, g):
    r = lax.rsqrt(jnp.mean(x * x, axis=-1, keepdims=True) + NORM_EPS)
    xh = x * r
    return xh * g, xh, r


def _rms_bwd(dy, xh, r, g):
    dxh = dy * g
    return r * (dxh - xh * jnp.mean(dxh * xh, axis=-1, keepdims=True))


def _deinterleave_store(val, s_ref, out_ref, lead, d, rows, dtype):
    if d == 1:
        out_ref[lead + (0,)] = val.astype(dtype)
        return
    for h in range(2):
        s_ref[h] = val[:, h * PAIR_W:(h + 1) * PAIR_W]
    for r in range(d):
        for h in range(2):
            out_ref[lead + (r, slice(None), slice(h * PAIR_W, (h + 1) * PAIR_W))] = (
                s_ref[h, pl.ds(r, rows // d, stride=d), :].astype(dtype))


def _interleave_load(in_ref, lead, s_ref, d, rows):
    for r in range(d):
        for h in range(2):
            s_ref[h, pl.ds(r, rows // d, stride=d), :] = (
                in_ref[lead + (r, slice(None), slice(h * PAIR_W, (h + 1) * PAIR_W))].astype(F32))


def _norm_inproj(x, g, w_in, shards, bufs):
    S = x.shape[0]
    n_tiles = S // TM
    n = len(shards)

    def body(*refs):
        x_ref, g_ref, w_ref = refs[0:3]
        shard_refs = refs[3:3 + n]
        u_ref, q0_ref, q1_ref, q2_ref, pz_ref, gate_ref = refs[3 + 2 * n:9 + 2 * n]
        buf_refs = refs[9 + 2 * n:9 + 3 * n]
        s_ref, send_sem, recv_sem = refs[9 + 3 * n:]
        i = pl.program_id(0)

        def copies():
            return _weight_half_copies(shard_refs, buf_refs, [sh.shape[0] for sh in shards], send_sem, recv_sem)

        @pl.when(i == 0)
        def _():
            for cpy in copies():
                cpy.start()

        u = _rms_fwd(x_ref[...], g_ref[...])[0].astype(BF16)
        u_ref[...] = u
        qkv_refs = (q0_ref, q1_ref, q2_ref)
        for k in range(N_CHUNKS):
            zc = _dot(u, _w_in_chunk(w_ref, k))
            if k < 9:
                which, grp = k // 3, k % 3
                if which == 0:
                    zc = zc * 0.125
                _deinterleave_store(zc, s_ref, qkv_refs[grp], (which,), DILATIONS[grp], TM, BF16)
            elif k < 12:
                pz_ref[:, (k - 9) * CHUNK:(k - 8) * CHUNK] = zc
            else:
                gate_ref[:, (k - 12) * CHUNK:(k - 11) * CHUNK] = zc.astype(BF16)

        @pl.when(i == n_tiles - 1)
        def _():
            for cpy in copies():
                cpy.wait()

    row = lambda w: pl.BlockSpec((TM, w), lambda i: (i, 0))
    res = pl.pallas_call(
        body, grid=(n_tiles,), name="norm_inproj",
        in_specs=[row(D_MODEL), _resident((1, D_MODEL)), _resident(w_in.shape)] + [ANY] * (2 * n),
        out_specs=[row(D_MODEL)]
        + [pl.BlockSpec((3, d, TM // d, GROUP_W), lambda i: (0, 0, i, 0)) for d in DILATIONS]
        + [row(POOL_W), row(2 * D_MODEL)] + [ANY] * n,
        out_shape=[SDS((S, D_MODEL), BF16)]
        + [SDS((3, d, S // d, GROUP_W), BF16) for d in DILATIONS]
        + [SDS((S, POOL_W), F32), SDS((S, 2 * D_MODEL), BF16)] + [SDS(b.shape, b.dtype) for b in bufs],
        scratch_shapes=[pltpu.VMEM((2, TM, PAIR_W), F32), pltpu.SemaphoreType.DMA((3 * n,)),
                        pltpu.SemaphoreType.DMA((3 * n,))],
        input_output_aliases={3 + n + w: 6 + w for w in range(n)},
        compiler_params=_cp("arbitrary"),
    )(x, g, w_in, *shards, *bufs)
    return res[:6], res[6:]


def _band_bias(grp, d):
    row = lax.broadcasted_iota(jnp.int32, (BAND, 2 * BAND), 0)
    col = lax.broadcasted_iota(jnp.int32, (BAND, 2 * BAND), 1)
    steps = BAND + row - col
    valid = (steps >= 0) & (steps <= BAND)
    stepsf = (steps * d).astype(F32)
    biases = []
    for hh in range(4):
        slope = 2.0 ** (-ALIBI_MAX_BIAS * (grp * 4 + hh + 1) / N_HEADS)
        biases.append(jnp.where(valid, -slope * stepsf, NEG))
    return biases, col


def _attn_tiles(grp, L):
    rr, rb = ATT_TILE[grp]
    rb = min(rb, L // BAND)
    return rr, rb, L // (rb * BAND)


def _kv_tile(cur_ref, prev_ref, rr, rb, cs):
    if rb == 0:
        return jnp.concatenate([prev_ref[rr, :, cs], cur_ref[rr, 0:BAND, cs]], axis=0)
    return cur_ref[rr, (rb - 1) * BAND:(rb + 1) * BAND, cs]


def _attn_fwd(qkv, grp):
    d = DILATIONS[grp]
    L = qkv.shape[2]
    RR, RB, nb = _attn_tiles(grp, L)

    def body(q_ref, kc_ref, kp_ref, vc_ref, vp_ref, o_ref, lse_ref):
        i = pl.program_id(0)
        biases, col = _band_bias(grp, d)
        first_keys_ok = (col >= BAND) | (i > 0)
        is_a = lax.broadcasted_iota(jnp.int32, (BAND, PAIR_W), 1) < HEAD_W
        for rr in range(RR):
            for rb in range(RB):
                rows = slice(rb * BAND, (rb + 1) * BAND)
                for cp in range(2):
                    cs = slice(cp * PAIR_W, (cp + 1) * PAIR_W)
                    q2 = q_ref[rr, rows, cs]
                    kcat = _kv_tile(kc_ref, kp_ref, rr, rb, cs)
                    vcat = _kv_tile(vc_ref, vp_ref, rr, rb, cs)
                    res = []
                    for h2 in range(2):
                        sel = is_a if h2 == 0 else jnp.logical_not(is_a)
                        b = biases[cp * 2 + h2]
                        if rb == 0:
                            b = jnp.where(first_keys_ok, b, NEG)
                        s = _dot_nt(jnp.where(sel, q2, jnp.zeros_like(q2)), kcat) + b
                        m = jnp.max(s, axis=-1, keepdims=True)
                        p = jnp.exp(s - m)
                        l = jnp.sum(p, axis=-1, keepdims=True)
                        o = _dot(p.astype(BF16), vcat) * (1.0 / l)
                        res.append((o, m + jnp.log(l)))
                    o_ref[rr, rows, cs] = jnp.where(is_a, res[0][0], res[1][0]).astype(BF16)
                    lse_ref[rr, rows, cs] = jnp.where(is_a, res[0][1], res[1][1])

    cur = lambda w: pl.BlockSpec((None, RR, RB * BAND, GROUP_W), lambda i, j: (w, j, i, 0))
    prev = lambda w: pl.BlockSpec((None, RR, BAND, GROUP_W), lambda i, j: (w, j, jnp.maximum(i * RB - 1, 0), 0))
    out = pl.BlockSpec((RR, RB * BAND, GROUP_W), lambda i, j: (j, i, 0))
    return pl.pallas_call(
        body, grid=(nb, d // RR), name=f"attn_fwd_g{grp}",
        in_specs=[cur(0), cur(1), prev(1), cur(2), prev(2)],
        out_specs=[out, out],
        out_shape=[SDS((d, L, GROUP_W), BF16), SDS((d, L, GROUP_W), F32)],
        compiler_params=_cp("parallel", "parallel"),
    )(qkv, qkv, qkv, qkv, qkv)


def _pool_column_select(col, vals):
    return jnp.where(col < POOL_GROUP_W, vals[0],
                     jnp.where(col < 2 * POOL_GROUP_W, vals[1],
                               jnp.where(col < 3 * POOL_GROUP_W, vals[2], vals[3])))


def _pool_inv_count(i, rows):
    t = i * rows + lax.broadcasted_iota(jnp.int32, (rows, POOL_W), 0)
    col = lax.broadcasted_iota(jnp.int32, (rows, POOL_W), 1)
    win = _pool_column_select(col, POOL_WINDOWS)
    return 1.0 / jnp.minimum(t + 1, win).astype(F32), col


def _mixer_out(outs, lses, pz, gates, x, w_ao, w_po, wbd, scale, w_out, g_mlp):
    S = x.shape[0]
    n_tiles = S // TMB

    def body(o0_ref, l0_ref, o1_ref, l1_ref, o2_ref, l2_ref, pz_ref, halo_ref, gate_ref, x_ref,
             wao_ref, wpo_ref, wbd_ref, sc_ref, wout_ref, g_ref,
             a_ref, lt0_ref, lt1_ref, lt2_ref, pooled_ref, mixed_ref, p_ref, merged_ref, h1_ref, m_ref,
             so1, sl1, so2, sl2, slt, ext_ref):
        i = pl.program_id(0)
        _interleave_load(o1_ref, (), so1, DILATIONS[1], TMB)
        _interleave_load(l1_ref, (), sl1, DILATIONS[1], TMB)
        _interleave_load(o2_ref, (), so2, DILATIONS[2], TMB)
        _interleave_load(l2_ref, (), sl2, DILATIONS[2], TMB)
        for h in range(2):
            hs = slice(h * PAIR_W, (h + 1) * PAIR_W)
            l0, l1, l2 = l0_ref[0, :, hs], sl1[h], sl2[h]
            mx = jnp.maximum(jnp.maximum(l0, l1), l2)
            e0, e1, e2 = jnp.exp(l0 - mx), jnp.exp(l1 - mx), jnp.exp(l2 - mx)
            den = e0 + e1 + e2
            a_ref[:, hs] = ((e0 * o0_ref[0, :, hs].astype(F32) + e1 * so1[h] + e2 * so2[h])
                            * (1.0 / den)).astype(BF16)
            slt[h] = mx + jnp.log(den)
        lt = jnp.concatenate([slt[0], slt[1]], axis=1)
        lt0_ref[0] = lt
        for ref, d in ((lt1_ref, DILATIONS[1]), (lt2_ref, DILATIONS[2])):
            for r in range(d):
                for h in range(2):
                    ref[r, :, h * PAIR_W:(h + 1) * PAIR_W] = slt[h, pl.ds(r, TMB // d, stride=d), :]

        pz_t = pz_ref[...]
        ext_ref[0:POOL_HALO, :] = jnp.where(i > 0, halo_ref[...], 0.0)
        ext_ref[POOL_HALO:, :] = pz_t
        sums = []
        acc = ext_ref[...]
        for k in (1, 2, 4, 8):
            acc = acc + pltpu.roll(acc, k, 0)
            sums.append(acc[POOL_HALO:, :])
        inv_cnt, col = _pool_inv_count(i, TMB)
        pooled = (_pool_column_select(col, sums) * inv_cnt - pz_t).astype(BF16)
        pooled_ref[...] = pooled
        mixed = _dot(pooled, wbd_ref[...])
        mixed_ref[...] = mixed.astype(BF16)
        p = (mixed * sc_ref[...]).astype(BF16)
        p_ref[...] = p

        a = a_ref[...]
        for j in range(N_CHIPS):
            js = slice(j * CHUNK, (j + 1) * CHUNK)
            ga = gate_ref[:, js]
            gp = gate_ref[:, D_MODEL + j * CHUNK:D_MODEL + (j + 1) * CHUNK]
            mj = _sigmoid(ga) * _dot(a, wao_ref[j]) + _sigmoid(gp) * _dot(p, wpo_ref[j])
            merged_ref[:, js] = mj.astype(BF16)
        h1 = x_ref[...] + _dot(merged_ref[...], wout_ref[...])
        h1_ref[...] = h1
        m_ref[...] = _rms_fwd(h1, g_ref[...])[0].astype(BF16)

    row = lambda w: pl.BlockSpec((TMB, w), lambda i: (i, 0))
    grp_spec = lambda d: pl.BlockSpec((d, TMB // d, GROUP_W), lambda i: (0, i, 0))
    halo = pl.BlockSpec((POOL_HALO, POOL_W), lambda i: (jnp.maximum(i * (TMB // POOL_HALO) - 1, 0), 0))
    d0, d1, d2 = DILATIONS
    return pl.pallas_call(
        body, grid=(n_tiles,), name="mixer_out",
        in_specs=[grp_spec(d0), grp_spec(d0), grp_spec(d1), grp_spec(d1), grp_spec(d2), grp_spec(d2),
                  row(POOL_W), halo, row(2 * D_MODEL), row(D_MODEL),
                  _resident(w_ao.shape), _resident(w_po.shape), _resident(wbd.shape), _resident(scale.shape),
                  _resident(w_out.shape), _resident(g_mlp.shape)],
        out_specs=[row(GROUP_W), grp_spec(d0), grp_spec(d1), grp_spec(d2),
                   row(POOL_W), row(POOL_W), row(POOL_W), row(D_MODEL), row(D_MODEL), row(D_MODEL)],
        out_shape=[SDS((S, GROUP_W), BF16)] + [SDS((d, S // d, GROUP_W), F32) for d in DILATIONS]
        + [SDS((S, POOL_W), BF16), SDS((S, POOL_W), BF16), SDS((S, POOL_W), BF16),
           SDS((S, D_MODEL), BF16), SDS((S, D_MODEL), F32), SDS((S, D_MODEL), BF16)],
        scratch_shapes=[pltpu.VMEM((2, TMB, PAIR_W), F32) for _ in range(5)]
        + [pltpu.VMEM((TMB + POOL_HALO, POOL_W), F32)],
        compiler_params=_cp("parallel"),
    )(outs[0], lses[0], outs[1], lses[1], outs[2], lses[2], pz, pz, gates, x,
      w_ao, w_po, wbd, scale, w_out, g_mlp)


def _mlp_fwd_loss(m, h1, target, w_mi, w_mo, g_f):
    S = m.shape[0]

    def body(m_ref, h1_ref, t_ref, wmi_ref, wmo_ref, g_ref, hid_ref, dh2_ref, dh2b_ref, loss_ref, dg_ref):
        @pl.when(pl.program_id(0) == 0)
        def _():
            loss_ref[...] = jnp.zeros_like(loss_ref)
            dg_ref[...] = jnp.zeros_like(dg_ref)

        mt = m_ref[...]
        acc = h1_ref[...]
        for c in range(N_CHIPS):
            hid = jnp.square(jnp.maximum(_dot(mt, wmi_ref[c]), 0.0)).astype(BF16)
            hid_ref[:, c * D_MODEL:(c + 1) * D_MODEL] = hid
            acc = acc + _dot(hid, wmo_ref[c])
        g = g_ref[...]
        y, hh, r = _rms_fwd(acc, g)
        e = y - t_ref[...]
        loss_ref[...] += jnp.sum(e * e, axis=0, keepdims=True)
        dy = e * (1.0 / D_MODEL)
        dg_ref[...] += jnp.sum(dy * hh, axis=0, keepdims=True)
        dh2 = _rms_bwd(dy, hh, r, g)
        dh2_ref[...] = dh2
        dh2b_ref[...] = dh2.astype(BF16)

    row = lambda w: pl.BlockSpec((TM, w), lambda i: (i, 0))
    vec = pl.BlockSpec((1, D_MODEL), lambda i: (0, 0))
    return pl.pallas_call(
        body, grid=(S // TM,), name="mlp_fwd_loss",
        in_specs=[row(D_MODEL), row(D_MODEL), row(D_MODEL), _resident(w_mi.shape), _resident(w_mo.shape),
                  _resident(g_f.shape)],
        out_specs=[row(D_FF), row(D_MODEL), row(D_MODEL), vec, vec],
        out_shape=[SDS((S, D_FF), BF16), SDS((S, D_MODEL), F32), SDS((S, D_MODEL), BF16),
                   SDS((1, D_MODEL), F32), SDS((1, D_MODEL), F32)],
        compiler_params=_cp("arbitrary"),
    )(m, h1, target, w_mi, w_mo, g_f)


def _mlp_bwd(dh2, dh2b, hid, h1, w_mi, w_mo, g_mlp):
    S = dh2.shape[0]

    def body(dh2_ref, dh2b_ref, hid_ref, h1_ref, wmi_ref, wmo_ref, g_ref, dpre_ref, dh1_ref, dh1b_ref, dg_ref):
        @pl.when(pl.program_id(0) == 0)
        def _():
            dg_ref[...] = jnp.zeros_like(dg_ref)

        d2 = dh2b_ref[...]
        dm = jnp.zeros((TM, D_MODEL), F32)
        for c in range(N_CHIPS):
            cs = slice(c * D_MODEL, (c + 1) * D_MODEL)
            dhid = _dot_nt(d2, wmo_ref[c])
            dpre = (dhid * (2.0 * jnp.sqrt(hid_ref[:, cs].astype(F32)))).astype(BF16)
            dpre_ref[:, cs] = dpre
            dm = dm + _dot_nt(dpre, wmi_ref[c])
        g = g_ref[...]
        _, hh, r = _rms_fwd(h1_ref[...], g)
        dg_ref[...] += jnp.sum(dm * hh, axis=0, keepdims=True)
        dh1 = dh2_ref[...] + _rms_bwd(dm, hh, r, g)
        dh1_ref[...] = dh1
        dh1b_ref[...] = dh1.astype(BF16)

    row = lambda w: pl.BlockSpec((TM, w), lambda i: (i, 0))
    return pl.pallas_call(
        body, grid=(S // TM,), name="mlp_bwd",
        in_specs=[row(D_MODEL), row(D_MODEL), row(D_FF), row(D_MODEL), _resident(w_mi.shape),
                  _resident(w_mo.shape), _resident(g_mlp.shape)],
        out_specs=[row(D_FF), row(D_MODEL), row(D_MODEL), pl.BlockSpec((1, D_MODEL), lambda i: (0, 0))],
        out_shape=[SDS((S, D_FF), BF16), SDS((S, D_MODEL), F32), SDS((S, D_MODEL), BF16), SDS((1, D_MODEL), F32)],
        compiler_params=_cp("arbitrary"),
    )(dh2, dh2b, hid, h1, w_mi, w_mo, g_mlp)


def _mixer_bwd(dh1b, a, p, mixed, gates, w_out, w_ao, w_po, wbd, scale, head_ones, sums):
    S = a.shape[0]
    n_tiles = S // TMB
    n = len(sums)

    def body(*refs):
        (dh1b_ref, a_ref, p_ref, mixed_ref, gate_ref, wout_ref, wao_ref, wpo_ref, wbd_ref, sc_ref,
         ones_ref) = refs[0:11]
        sum_refs = refs[11:11 + n]
        (da1_ref, dp1_ref, dgate_ref, da0_ref, dag1_ref, dag2_ref, dd0_ref, dd1_ref, dd2_ref,
         dmixed_ref, dqp_ref, dscale_ref) = refs[11 + n:23 + n]
        land_refs = refs[23 + n:23 + 2 * n]
        s_da, s_dd, send_sem, recv_sem = refs[23 + 2 * n:]
        i = pl.program_id(0)

        @pl.when(i == 0)
        def _():
            dscale_ref[...] = jnp.zeros_like(dscale_ref)
            for cpy in _chip_sum_copies(sum_refs, land_refs, send_sem, recv_sem):
                cpy.start()

        dmerged = _dot_nt(dh1b_ref[...], wout_ref[...])
        a = a_ref[...]
        p = p_ref[...]
        da = jnp.zeros((TMB, GROUP_W), F32)
        dp = jnp.zeros((TMB, POOL_W), F32)
        for j in range(N_CHIPS):
            js = slice(j * CHUNK, (j + 1) * CHUNK)
            sa = _sigmoid(gate_ref[:, js])
            sp = _sigmoid(gate_ref[:, D_MODEL + j * CHUNK:D_MODEL + (j + 1) * CHUNK])
            dmj = dmerged[:, js]
            da1 = (dmj * sa).astype(BF16)
            dp1 = (dmj * sp).astype(BF16)
            da1_ref[:, js] = da1
            dp1_ref[:, js] = dp1
            dgate_ref[j] = (dmj * _dot(a, wao_ref[j]) * sa * (1.0 - sa)).astype(BF16)
            dgate_ref[N_CHIPS + j] = (dmj * _dot(p, wpo_ref[j]) * sp * (1.0 - sp)).astype(BF16)
            da = da + _dot_nt(da1, wao_ref[j])
            dp = dp + _dot_nt(dp1, wpo_ref[j])

        prod = da * a.astype(F32)
        hi = prod.astype(BF16)
        lo = (prod - hi.astype(F32)).astype(BF16)
        dd = _dot(hi, ones_ref[...]) + _dot(lo, ones_ref[...])
        for ref, val, sref, dtype in ((da0_ref, da, s_da, BF16), (dd0_ref, dd, s_dd, F32)):
            ref[0] = val.astype(dtype)
            for h in range(2):
                sref[h] = val[:, h * PAIR_W:(h + 1) * PAIR_W]
        for refs, d in (((dag1_ref, dd1_ref), DILATIONS[1]), ((dag2_ref, dd2_ref), DILATIONS[2])):
            for r in range(d):
                for h in range(2):
                    hs = slice(h * PAIR_W, (h + 1) * PAIR_W)
                    refs[0][r, :, hs] = s_da[h, pl.ds(r, TMB // d, stride=d), :].astype(BF16)
                    refs[1][r, :, hs] = s_dd[h, pl.ds(r, TMB // d, stride=d), :]

        sc = sc_ref[...]
        dscale_ref[...] += jnp.sum(dp * mixed_ref[...].astype(F32), axis=0, keepdims=True)
        dmixed = (dp * sc).astype(BF16)
        dmixed_ref[...] = dmixed
        inv_cnt, _ = _pool_inv_count(i, TMB)
        dqp_ref[...] = (_dot_nt(dmixed, wbd_ref[...]) * inv_cnt).astype(BF16)

        @pl.when(i == n_tiles - 1)
        def _():
            for cpy in _chip_sum_copies(sum_refs, land_refs, send_sem, recv_sem):
                cpy.wait()

    row = lambda w: pl.BlockSpec((TMB, w), lambda i: (i, 0))
    grp_spec = lambda d: pl.BlockSpec((d, TMB // d, GROUP_W), lambda i: (0, i, 0))
    d0, d1, d2 = DILATIONS
    res = pl.pallas_call(
        body, grid=(n_tiles,), name="mixer_bwd",
        in_specs=[row(D_MODEL), row(GROUP_W), row(POOL_W), row(POOL_W), row(2 * D_MODEL),
                  _resident(w_out.shape), _resident(w_ao.shape), _resident(w_po.shape), _resident(wbd.shape),
                  _resident(scale.shape), _resident(head_ones.shape)] + [ANY] * n,
        out_specs=[row(D_MODEL), row(D_MODEL), pl.BlockSpec((2 * N_CHIPS, TMB, CHUNK), lambda i: (0, i, 0)),
                   grp_spec(d0), grp_spec(d1), grp_spec(d2), grp_spec(d0), grp_spec(d1), grp_spec(d2),
                   row(POOL_W), row(POOL_W), pl.BlockSpec((1, POOL_W), lambda i: (0, 0))] + [ANY] * n,
        out_shape=[SDS((S, D_MODEL), BF16), SDS((S, D_MODEL), BF16), SDS((2 * N_CHIPS, S, CHUNK), BF16)]
        + [SDS((d, S // d, GROUP_W), BF16) for d in DILATIONS]
        + [SDS((d, S // d, GROUP_W), F32) for d in DILATIONS]
        + [SDS((S, POOL_W), BF16), SDS((S, POOL_W), BF16), SDS((1, POOL_W), F32)]
        + [SDS(t.shape, t.dtype) for t in sums],
        scratch_shapes=[pltpu.VMEM((2, TMB, PAIR_W), F32), pltpu.VMEM((2, TMB, PAIR_W), F32),
                        pltpu.SemaphoreType.DMA((3 * n,)), pltpu.SemaphoreType.DMA((3 * n,))],
        compiler_params=_cp("arbitrary"),
    )(dh1b, a, p, mixed, gates, w_out, w_ao, w_po, wbd, scale, head_ones, *sums)
    return res[:12], res[12:]


def _attn_bwd(qkv, da, lt, dd, grp):
    d = DILATIONS[grp]
    L = qkv.shape[2]
    RR, RB, nb = _attn_tiles(grp, L)

    def body(q_ref, kc_ref, kp_ref, vc_ref, vp_ref, da_ref, lt_ref, dd_ref, dq_ref, dk_ref, dv_ref, dk_acc, dv_acc):
        i = pl.program_id(1)

        @pl.when(i == 0)
        def _():
            dk_acc[...] = jnp.zeros_like(dk_acc)
            dv_acc[...] = jnp.zeros_like(dv_acc)

        def compute(cur, prv):
            dk_acc[cur] = jnp.zeros((RR, RB * BAND, GROUP_W), F32)
            dv_acc[cur] = jnp.zeros((RR, RB * BAND, GROUP_W), F32)
            biases, col = _band_bias(grp, d)
            first_keys_ok = (col >= BAND) | (i > 0)
            is_a = lax.broadcasted_iota(jnp.int32, (BAND, PAIR_W), 1) < HEAD_W
            for rr in range(RR):
                for rb in range(RB):
                    rows = slice(rb * BAND, (rb + 1) * BAND)
                    for cp in range(2):
                        cs = slice(cp * PAIR_W, (cp + 1) * PAIR_W)
                        q2 = q_ref[rr, rows, cs]
                        da2 = da_ref[rr, rows, cs]
                        lt2 = lt_ref[rr, rows, cs]
                        dd2 = dd_ref[rr, rows, cs]
                        kcat = _kv_tile(kc_ref, kp_ref, rr, rb, cs)
                        vcat = _kv_tile(vc_ref, vp_ref, rr, rb, cs)
                        q2t = q2.astype(F32).T.astype(BF16)
                        da2t = da2.astype(F32).T.astype(BF16)
                        dqs, dkts, dvts = [], [], []
                        for h2 in range(2):
                            sel = is_a if h2 == 0 else jnp.logical_not(is_a)
                            lane0 = h2 * HEAD_W
                            b = biases[cp * 2 + h2]
                            if rb == 0:
                                b = jnp.where(first_keys_ok, b, NEG)
                            s = _dot_nt(jnp.where(sel, q2, jnp.zeros_like(q2)), kcat) + b
                            p = jnp.exp(s - lt2[:, lane0:lane0 + 1])
                            dpv = _dot_nt(jnp.where(sel, da2, jnp.zeros_like(da2)), vcat)
                            ds = (p * (dpv - dd2[:, lane0:lane0 + 1])).astype(BF16)
                            dqs.append(_dot(ds, kcat))
                            dkts.append(_dot(q2t[lane0:lane0 + HEAD_W, :], ds))
                            dvts.append(_dot(da2t[lane0:lane0 + HEAD_W, :], p.astype(BF16)))
                        dq_ref[rr, rows, cs] = (jnp.where(is_a, dqs[0], dqs[1]) * 0.125).astype(BF16)
                        dkc = jnp.concatenate(dkts, axis=0).T
                        dvc = jnp.concatenate(dvts, axis=0).T
                        if rb == 0:
                            last = slice((RB - 1) * BAND, RB * BAND)
                            dk_acc[prv, rr, last, cs] += dkc[0:BAND]
                            dv_acc[prv, rr, last, cs] += dvc[0:BAND]
                            dk_acc[cur, rr, 0:BAND, cs] += dkc[BAND:]
                            dv_acc[cur, rr, 0:BAND, cs] += dvc[BAND:]
                        else:
                            both = slice((rb - 1) * BAND, (rb + 1) * BAND)
                            dk_acc[cur, rr, both, cs] += dkc
                            dv_acc[cur, rr, both, cs] += dvc

        def flush(prv):
            dk_ref[...] = dk_acc[prv].astype(BF16)
            dv_ref[...] = dv_acc[prv].astype(BF16)

        for parity in (0, 1):
            on = (i % 2) == parity
            pl.when(on & (i < nb))(functools.partial(compute, parity, 1 - parity))
            pl.when(on & (i > 0))(functools.partial(flush, 1 - parity))

    qi = lambda i: jnp.minimum(i, nb - 1)
    cur_w = lambda w: pl.BlockSpec((None, RR, RB * BAND, GROUP_W), lambda j, i: (w, j, qi(i), 0))
    prev_w = lambda w: pl.BlockSpec((None, RR, BAND, GROUP_W),
                                    lambda j, i: (w, j, jnp.maximum(qi(i) * RB - 1, 0), 0))
    blk = pl.BlockSpec((RR, RB * BAND, GROUP_W), lambda j, i: (j, qi(i), 0))
    late = pl.BlockSpec((RR, RB * BAND, GROUP_W), lambda j, i: (j, jnp.maximum(i - 1, 0), 0))
    return pl.pallas_call(
        body, grid=(d // RR, nb + 1), name=f"attn_bwd_g{grp}",
        in_specs=[cur_w(0), cur_w(1), prev_w(1), cur_w(2), prev_w(2), blk, blk, blk],
        out_specs=[blk, late, late],
        out_shape=[SDS((d, L, GROUP_W), BF16)] * 3,
        scratch_shapes=[pltpu.VMEM((2, RR, RB * BAND, GROUP_W), F32), pltpu.VMEM((2, RR, RB * BAND, GROUP_W), F32)],
        compiler_params=_cp("parallel", "arbitrary"),
    )(qkv, qkv, qkv, qkv, qkv, da, lt, dd)


def _dz_assemble(dqkv, dqp):
    S = dqp.shape[0]
    n_tiles = S // TMB

    def body(*refs):
        dqkv_refs = refs[0:9]
        dqp_ref, halo_ref = refs[9:11]
        dz_ref, s_ref, ext_ref = refs[11:]
        i = pl.program_id(0)

        for grp in range(3):
            for which in range(3):
                n = which * 3 + grp
                ref = dqkv_refs[grp * 3 + which]
                if DILATIONS[grp] == 1:
                    dz_ref[n] = ref[0]
                else:
                    _interleave_load(ref, (), s_ref, DILATIONS[grp], TMB)
                    for h in range(2):
                        dz_ref[n, :, h * PAIR_W:(h + 1) * PAIR_W] = s_ref[h].astype(BF16)

        dqp = dqp_ref[...].astype(F32)
        ext_ref[0:TMB, :] = dqp
        ext_ref[TMB:, :] = jnp.where(i < n_tiles - 1, halo_ref[...].astype(F32), 0.0)
        sums = []
        acc = ext_ref[...]
        for k in (1, 2, 4, 8):
            acc = acc + pltpu.roll(acc, TMB + POOL_HALO - k, 0)
            sums.append(acc[0:TMB, :])
        inv_cnt, col = _pool_inv_count(i, TMB)
        dpz = _pool_column_select(col, sums) - dqp / inv_cnt
        for t in range(3):
            dz_ref[9 + t] = dpz[:, t * CHUNK:(t + 1) * CHUNK].astype(BF16)

    row = lambda w: pl.BlockSpec((TMB, w), lambda i: (i, 0))
    grp_spec = lambda d: pl.BlockSpec((d, TMB // d, GROUP_W), lambda i: (0, i, 0))
    halo = pl.BlockSpec((POOL_HALO, POOL_W),
                        lambda i: (jnp.minimum((i + 1) * (TMB // POOL_HALO), S // POOL_HALO - 1), 0))
    flat = [t for grp in range(3) for t in dqkv[grp]]
    return pl.pallas_call(
        body, grid=(n_tiles,), name="dz_assemble",
        in_specs=[grp_spec(DILATIONS[grp]) for grp in range(3) for _ in range(3)] + [row(POOL_W), halo],
        out_specs=pl.BlockSpec((N_DZ_CHUNKS, TMB, CHUNK), lambda i: (0, i, 0)),
        out_shape=SDS((N_DZ_CHUNKS, S, CHUNK), BF16),
        scratch_shapes=[pltpu.VMEM((2, TMB, PAIR_W), F32), pltpu.VMEM((TMB + POOL_HALO, POOL_W), F32)],
        compiler_params=_cp("parallel"),
    )(*flat, dqp, dqp)


def _inproj_dx(dz, dgates, dh1, x, g, w_in, sums):
    S = x.shape[0]
    n_tiles = S // TM
    n = len(sums)

    def body(*refs):
        dz_ref, dgate_ref, dh1_ref, x_ref, g_ref, w_ref = refs[0:6]
        sum_refs = refs[6:6 + n]
        dx_ref, dg_ref = refs[6 + n:8 + n]
        land_refs = refs[8 + n:8 + 2 * n]
        sems = refs[8 + 2 * n:]
        i = pl.program_id(0)

        def copies():
            return _chip_sum_copies(sum_refs, land_refs, *sems)

        @pl.when(i == 0)
        def _():
            dg_ref[...] = jnp.zeros_like(dg_ref)
            for cpy in copies():
                cpy.start()

        du = jnp.zeros((TM, D_MODEL), F32)
        for k in range(N_CHUNKS):
            dzk = dz_ref[k] if k < N_DZ_CHUNKS else dgate_ref[k - N_DZ_CHUNKS]
            du = du + _dot_nt(dzk, _w_in_chunk(w_ref, k))
        gv = g_ref[...]
        _, xh, r = _rms_fwd(x_ref[...], gv)
        dg_ref[...] += jnp.sum(du * xh, axis=0, keepdims=True)
        dx_ref[...] = dh1_ref[...] + _rms_bwd(du, xh, r, gv)

        @pl.when(i == n_tiles - 1)
        def _():
            for cpy in copies():
                cpy.wait()

    row = lambda w: pl.BlockSpec((TM, w), lambda i: (i, 0))
    res = pl.pallas_call(
        body, grid=(n_tiles,), name="inproj_dx",
        in_specs=[pl.BlockSpec((N_DZ_CHUNKS, TM, CHUNK), lambda i: (0, i, 0)),
                  pl.BlockSpec((N_CHUNKS - N_DZ_CHUNKS, TM, CHUNK), lambda i: (0, i, 0)),
                  row(D_MODEL), row(D_MODEL), _resident(g.shape), _resident(w_in.shape)] + [ANY] * n,
        out_specs=[row(D_MODEL), pl.BlockSpec((1, D_MODEL), lambda i: (0, 0))] + [ANY] * n,
        out_shape=[SDS((S, D_MODEL), F32), SDS((1, D_MODEL), F32)] + [SDS(t.shape, t.dtype) for t in sums],
        scratch_shapes=[pltpu.SemaphoreType.DMA((3 * n,)), pltpu.SemaphoreType.DMA((3 * n,))],
        compiler_params=_cp("arbitrary"),
    )(dz, dgates, dh1, x, g, w_in, *sums)
    return res[0], res[1], res[2:]


def _wgrad(a, b, name, *, out_shape, a_spec, b_spec, out_spec, grid, n_out_cols=None, fill=None, narrow=True):
    k_axis = len(grid) - 1
    n_k = grid[k_axis]
    n_out = 2 if narrow else 1

    def body(a_ref, b_ref, *rest):
        o_ref = rest[-n_out]

        @pl.when(pl.program_id(k_axis) == 0)
        def _():
            o_ref[...] = jnp.zeros_like(o_ref)

        at = a_ref[...]
        if n_out_cols is None:
            o_ref[...] += _dot_tn(at, b_ref[...])
        elif n_out_cols[0] == "lead_both":
            for t in range(b_ref.shape[0]):
                o_ref[t] += _dot_tn(at, b_ref[t])
        else:
            w = n_out_cols[1]
            for t in range(o_ref.shape[0]):
                o_ref[t] += _dot_tn(at, b_ref[:, t * w:(t + 1) * w])

        if narrow:
            @pl.when(pl.program_id(k_axis) == n_k - 1)
            def _():
                rest[-1][...] = o_ref[...].astype(BF16)

    sem = ("parallel",) * k_axis + ("arbitrary",)
    extra = [] if fill is None else list(fill) if narrow else [fill]
    shapes = [out_shape, SDS(out_shape.shape, BF16)] if narrow else out_shape
    return pl.pallas_call(body, grid=grid, name=name, in_specs=[a_spec, b_spec] + [ANY] * len(extra),
                          out_specs=[out_spec] * n_out if narrow else out_spec, out_shape=shapes,
                          input_output_aliases={2 + t: t for t in range(len(extra))},
                          compiler_params=_cp(*sem))(a, b, *extra)


def _wgrad_in(u, dz, dgates):
    bk = min(BK, u.shape[0])
    nk = u.shape[0] // bk
    g = WGRAD_IN_GROUP
    kw = dict(n_out_cols=("lead_both", CHUNK), a_spec=pl.BlockSpec((bk, D_MODEL), lambda j, k: (k, 0)),
              b_spec=pl.BlockSpec((g, bk, CHUNK), lambda j, k: (j, k, 0)),
              out_shape=SDS((N_CHUNKS, D_MODEL, CHUNK), F32))
    first = _wgrad(u, dz, "wgrad_in_qkvp", grid=(N_DZ_CHUNKS // g, nk),
                   out_spec=pl.BlockSpec((g, D_MODEL, CHUNK), lambda j, k: (j, 0, 0)), **kw)
    both = _wgrad(u, dgates, "wgrad_in_gates", grid=((N_CHUNKS - N_DZ_CHUNKS) // g, nk), fill=first,
                  out_spec=pl.BlockSpec((g, D_MODEL, CHUNK), lambda j, k: (N_DZ_CHUNKS // g + j, 0, 0)), **kw)
    return [t.reshape(N_CHIPS, CHUNKS_PER_SHARD * D_MODEL, CHUNK) for t in both]


def _wgrads_mixer(a, da1, p, dp1, merged, dh1b, pooled, dmixed):
    bk = min(BK, a.shape[0])
    nk = a.shape[0] // bk
    g_ao = _wgrad(
        a, da1, "wgrad_att_out", grid=(nk,), n_out_cols=("cols_b", CHUNK),
        a_spec=pl.BlockSpec((bk,GROUP_W), lambda k: (k, 0)),
        b_spec=pl.BlockSpec((bk,D_MODEL), lambda k: (k, 0)),
        out_spec=pl.BlockSpec((N_CHIPS, GROUP_W, CHUNK), lambda k: (0, 0, 0)),
        out_shape=SDS((N_CHIPS, GROUP_W, CHUNK), F32))
    g_po = _wgrad(
        p, dp1, "wgrad_pool_out", grid=(nk,), n_out_cols=("cols_b", CHUNK),
        a_spec=pl.BlockSpec((bk,POOL_W), lambda k: (k, 0)),
        b_spec=pl.BlockSpec((bk,D_MODEL), lambda k: (k, 0)),
        out_spec=pl.BlockSpec((N_CHIPS, POOL_W, CHUNK), lambda k: (0, 0, 0)),
        out_shape=SDS((N_CHIPS, POOL_W, CHUNK), F32))
    g_out = _wgrad(
        merged, dh1b, "wgrad_out", grid=(nk,),
        a_spec=pl.BlockSpec((bk,D_MODEL), lambda k: (k, 0)),
        b_spec=pl.BlockSpec((bk,D_MODEL), lambda k: (k, 0)),
        out_spec=pl.BlockSpec((D_MODEL, D_MODEL), lambda k: (0, 0)),
        out_shape=SDS((D_MODEL, D_MODEL), F32))
    g_bd = _wgrad(
        pooled, dmixed, "wgrad_pool_grp", grid=(nk,),
        a_spec=pl.BlockSpec((bk,POOL_W), lambda k: (k, 0)),
        b_spec=pl.BlockSpec((bk,POOL_W), lambda k: (k, 0)),
        out_spec=pl.BlockSpec((POOL_W, POOL_W), lambda k: (0, 0)),
        out_shape=SDS((POOL_W, POOL_W), F32), narrow=False)
    g_out = [t.reshape(N_CHIPS, D_MODEL // N_CHIPS, D_MODEL) for t in g_out]
    return [g_ao, g_po, g_out], g_bd


def _wgrads_mlp(m, dpre, hid, dh2b):
    bk = min(BK, m.shape[0])
    nk = m.shape[0] // bk
    g_mi = _wgrad(
        m, dpre, "wgrad_mlp_in", grid=(N_CHIPS, nk),
        a_spec=pl.BlockSpec((bk,D_MODEL), lambda c, k: (k, 0)),
        b_spec=pl.BlockSpec((bk,D_MODEL), lambda c, k: (k, c)),
        out_spec=pl.BlockSpec((None, D_MODEL, D_MODEL), lambda c, k: (c, 0, 0)),
        out_shape=SDS((N_CHIPS, D_MODEL, D_MODEL), F32))
    g_mo = _wgrad(
        hid, dh2b, "wgrad_mlp_out", grid=(N_CHIPS, nk),
        a_spec=pl.BlockSpec((bk,D_MODEL), lambda c, k: (k, c)),
        b_spec=pl.BlockSpec((bk,D_MODEL), lambda c, k: (k, 0)),
        out_spec=pl.BlockSpec((None, D_MODEL, D_MODEL), lambda c, k: (c, 0, 0)),
        out_shape=SDS((N_CHIPS, D_MODEL, D_MODEL), F32))
    return [g_mi, g_mo]


def _mesh_place():
    x, y, c = lax.axis_index("x"), lax.axis_index("y"), lax.axis_index("c")
    other_chips = [(x, 1 - y), (1 - x, y), (1 - x, 1 - y)]
    return x, y, c, other_chips


ANY = pl.BlockSpec(memory_space=pl.ANY)


def _weight_half_copies(shard_refs, buf_refs, rows, send_sem, recv_sem):
    x, y, c, chips = _mesh_place()
    me = 2 * x + y
    copies = []
    for w, r_full in enumerate(rows):
        rh = r_full // 2
        for r, (px, py) in enumerate(chips):
            k = w * 3 + r
            copies.append(pltpu.make_async_remote_copy(
                src_ref=shard_refs[w].at[pl.ds(c * rh, rh), :], dst_ref=buf_refs[w].at[me, pl.ds(c * rh, rh), :],
                send_sem=send_sem.at[k], recv_sem=recv_sem.at[k], device_id=(px, py, c), device_id_type=MESH))
    return copies


def _pair_forward_copies(buf_refs, rows, send_sem, recv_sem):
    x, y, c, chips = _mesh_place()
    out = []
    for w, r_full in enumerate(rows):
        rh = r_full // 2
        for r, (px, py) in enumerate(chips):
            k = w * 3 + r
            landed = buf_refs[w].at[2 * px + py, pl.ds(c * rh, rh), :]
            theirs = buf_refs[w].at[2 * px + py, pl.ds((1 - c) * rh, rh), :]
            mk = lambda ref: pltpu.make_async_remote_copy(
                src_ref=ref, dst_ref=ref, send_sem=send_sem.at[k], recv_sem=recv_sem.at[k],
                device_id=(x, y, 1 - c), device_id_type=MESH)
            out.append((mk(landed), mk(theirs)))
    return out


def _place_own(block, n_slots, slot):
    buf = lax.empty((n_slots,) + block.shape, block.dtype)
    return lax.dynamic_update_slice(buf, block[None], (slot,) + (0,) * block.ndim)


def _allgather_weights(shards, bufs):
    n = len(shards)
    rows = [sh.shape[0] for sh in shards]

    def body(*refs):
        src, dst = refs[:n], refs[2 * n:3 * n]
        send_sem, recv_sem, fsend_sem, frecv_sem = refs[3 * n:]
        sends = _weight_half_copies(src, dst, rows, send_sem, recv_sem)
        for cpy in sends:
            cpy.start()
        fwds = _pair_forward_copies(dst, rows, fsend_sem, frecv_sem)
        for cpy, (fwd, _) in zip(sends, fwds):
            cpy.wait_recv()
            fwd.start()
        for _, landing in fwds:
            landing.wait_recv()
        for cpy in sends + [f for f, _ in fwds]:
            cpy.wait_send()

    return pl.pallas_call(
        body, name="allgather_w_in",
        in_specs=[ANY] * (2 * n), out_specs=[ANY] * n,
        out_shape=[SDS(b.shape, b.dtype) for b in bufs],
        scratch_shapes=[pltpu.SemaphoreType.DMA((3 * n,))] * 4,
        input_output_aliases={n + w: w for w in range(n)},
    )(*shards, *bufs)


def _pair_forward(bufs, rows):
    n = len(bufs)

    def body(*refs):
        dst = refs[n:2 * n]
        send_sem, recv_sem = refs[2 * n:]
        fwds = _pair_forward_copies(dst, rows, send_sem, recv_sem)
        for fwd, _ in fwds:
            fwd.start()
        for fwd, landing in fwds:
            landing.wait_recv()
            fwd.wait_send()

    return pl.pallas_call(
        body, name="weights_pair_forward",
        in_specs=[ANY] * n, out_specs=[ANY] * n,
        out_shape=[SDS(b.shape, b.dtype) for b in bufs],
        scratch_shapes=[pltpu.SemaphoreType.DMA((3 * n,))] * 2,
        input_output_aliases={w: w for w in range(n)},
    )(*bufs)


def _chip_sum_copies(src, dst, send_sem, recv_sem):
    x, y, c, chips = _mesh_place()
    copies = []
    for w in range(len(src)):
        for r, (px, py) in enumerate(chips):
            k = w * 3 + r
            copies.append(pltpu.make_async_remote_copy(
                src_ref=src[w].at[r + 1], dst_ref=dst[w].at[r + 1], send_sem=send_sem.at[k], recv_sem=recv_sem.at[k],
                device_id=(px, py, c), device_id_type=MESH))
    return copies


def _pair_exchange(grads):
    n = len(grads)

    def body(*refs):
        src, dst = refs[:n], refs[n:2 * n]
        send_sem, recv_sem = refs[2 * n:]
        x, y, c, _ = _mesh_place()
        copies = []
        for w in range(n):
            rh = grads[w].shape[1] // 2
            copies.append(pltpu.make_async_remote_copy(
                src_ref=src[w].at[:, pl.ds((1 - c) * rh, rh), :], dst_ref=dst[w],
                send_sem=send_sem.at[w], recv_sem=recv_sem.at[w],
                device_id=(x, y, 1 - c), device_id_type=MESH))
            copies[-1].start()
        for cpy in copies:
            cpy.wait()

    return pl.pallas_call(
        body, name="grad_pair_exchange",
        in_specs=[ANY] * n, out_specs=[ANY] * n,
        out_shape=[SDS((N_CHIPS, g.shape[1] // 2, g.shape[2]), g.dtype) for g in grads],
        scratch_shapes=[pltpu.SemaphoreType.DMA((n,)), pltpu.SemaphoreType.DMA((n,))],
    )(*grads)


def _pair_sum(place, grad, recv, name):
    _, R, C = grad.shape
    rh = R // 2
    br = _row_block(rh, max(256, ELEMENTWISE_BLOCK // C))
    nbh = rh // br

    def body(place_ref, g_ref, r_ref, own_ref, sums_ref):
        s = g_ref[...] + r_ref[...].astype(F32)

        @pl.when(pl.program_id(1) == 0)
        def _():
            own_ref[...] = s

        sums_ref[...] = s.astype(BF16)

    slot = lambda rel, pr: jnp.bitwise_xor(pr[0], rel)
    return pl.pallas_call(
        body, name=name,
        grid_spec=pltpu.PrefetchScalarGridSpec(
            num_scalar_prefetch=1, grid=(nbh, N_CHIPS),
            in_specs=[pl.BlockSpec((None, br, C), lambda i, rel, pr: (slot(rel, pr), pr[1] * nbh + i, 0)),
                      pl.BlockSpec((None, br, C), lambda i, rel, pr: (slot(rel, pr), i, 0))],
            out_specs=[pl.BlockSpec((br, C), lambda i, rel, pr: (i, 0)),
                       pl.BlockSpec((None, br, C), lambda i, rel, pr: (rel, i, 0))]),
        out_shape=[SDS((rh, C), F32), SDS((N_CHIPS, rh, C), BF16)],
        compiler_params=_cp("parallel", "arbitrary"),
    )(place, grad, recv)


def _chip_sum(place, own, recv, name):
    rh, C = own.shape
    br = _row_block(rh, max(256, ELEMENTWISE_BLOCK // C))
    nbh = rh // br

    def body(place_ref, own_ref, r_ref, o_ref):
        o_ref[...] = ((own_ref[...] + r_ref[1].astype(F32)) + r_ref[2].astype(F32)) + r_ref[3].astype(F32)

    return pl.pallas_call(
        body, name=name,
        grid_spec=pltpu.PrefetchScalarGridSpec(
            num_scalar_prefetch=1, grid=(nbh,),
            in_specs=[pl.BlockSpec((br, C), lambda i, pr: (i, 0)),
                      pl.BlockSpec((N_CHIPS, br, C), lambda i, pr: (0, i, 0))],
            out_specs=pl.BlockSpec((br, C), lambda i, pr: (pr[1] * nbh + i, 0))),
        out_shape=SDS((2 * rh, C), F32),
        compiler_params=_cp("parallel"),
    )(place, own, recv)


def _finish_exchange(grads, small_all):
    n = len(grads)

    def body(*refs):
        dst, all_ref = refs[n + 1:2 * n + 1], refs[2 * n + 1]
        send_sem, recv_sem, ssend_sem, srecv_sem = refs[2 * n + 2:]
        x, y, c, chips = _mesh_place()
        sib = (x, y, 1 - c)

        def pack(dev, k, to):
            slot = 4 * dev[0] + 2 * dev[1] + dev[2]
            return pltpu.make_async_remote_copy(
                src_ref=all_ref.at[slot], dst_ref=all_ref.at[slot], send_sem=ssend_sem.at[k],
                recv_sem=srecv_sem.at[k], device_id=to, device_id_type=MESH)

        pack_copies = [pack((x, y, c), 0, sib)] + [pack((x, y, c), 1 + r, (px, py, c))
                                                   for r, (px, py) in enumerate(chips)]
        for cpy in pack_copies:
            cpy.start()
        sends, landings = [], []
        for w in range(n):
            rh = grads[w].shape[0] // 2
            mk = lambda cc: pltpu.make_async_remote_copy(
                src_ref=dst[w].at[pl.ds(cc * rh, rh), :], dst_ref=dst[w].at[pl.ds(cc * rh, rh), :],
                send_sem=send_sem.at[w], recv_sem=recv_sem.at[w], device_id=(x, y, 1 - c), device_id_type=MESH)
            sends.append(mk(c))
            landings.append(mk(1 - c))
            sends[-1].start()
        for r, (px, py) in enumerate(chips):
            pack((px, py, c), 1 + r, (px, py, c)).wait_recv()
            pack_copies.append(pack((px, py, c), 4 + r, sib))
            pack_copies[-1].start()
        pack(sib, 0, sib).wait_recv()
        for r, (px, py) in enumerate(chips):
            pack((px, py, 1 - c), 4 + r, sib).wait_recv()
        for cpy in landings:
            cpy.wait_recv()
        for cpy in sends + pack_copies:
            cpy.wait_send()

    res = pl.pallas_call(
        body, name="grad_finish_exchange",
        in_specs=[ANY] * (n + 1), out_specs=[ANY] * (n + 1),
        out_shape=[SDS(g.shape, g.dtype) for g in grads] + [SDS(small_all.shape, small_all.dtype)],
        scratch_shapes=[pltpu.SemaphoreType.DMA((n,)), pltpu.SemaphoreType.DMA((n,)),
                        pltpu.SemaphoreType.DMA((N_DEV - 1,)), pltpu.SemaphoreType.DMA((N_DEV - 1,))],
        input_output_aliases={w: w for w in range(n + 1)},
    )(*grads, small_all)
    return res[:n], res[n]


def _adamw_math(w, g, m, v):
    m = ADAM_B1 * m + (1.0 - ADAM_B1) * g
    v = ADAM_B2 * v + (1.0 - ADAM_B2) * jnp.square(g)
    m_hat = m / (1.0 - ADAM_B1 ** ADAM_STEP)
    v_hat = v / (1.0 - ADAM_B2 ** ADAM_STEP)
    delta = -ADAM_LR * (m_hat / (jnp.sqrt(v_hat) + ADAM_EPS) + ADAM_WD * w)
    return delta, m, v


def _adamw(w, g, m, v, name):
    R, C = w.shape
    br = _row_block(R, 512)
    if g.ndim == 3:
        n_chunks, cw = g.shape[0], g.shape[2]
        g_spec = pl.BlockSpec((None, br, cw), lambda t, i: (t, i, 0))
    else:
        n_chunks, cw = 1, C
        g_spec = pl.BlockSpec((br, cw), lambda t, i: (i, t))

    def body(w_ref, g_ref, m_ref, v_ref, g_out_ref, d_ref, nm_ref, nv_ref):
        gv = g_ref[...]
        g_out_ref[...] = gv
        d_ref[...], nm_ref[...], nv_ref[...] = _adamw_math(w_ref[...], gv, m_ref[...], v_ref[...])

    spec = pl.BlockSpec((br, cw), lambda t, i: (i, t))
    return pl.pallas_call(
        body, grid=(n_chunks, R // br), name=name, in_specs=[spec, g_spec, spec, spec], out_specs=[spec] * 4,
        out_shape=[SDS((R, C), F32)] * 4, compiler_params=_cp("parallel", "parallel"),
    )(w, g, m, v)


def _small_sum_adamw(all_small, w, m, v):
    loss_row = PACK_ROWS - 8

    def body(all_ref, w_ref, m_ref, v_ref, g_ref, d_ref, nm_ref, nv_ref, loss_ref):
        g = all_ref[0]
        for k in range(1, N_DEV):
            g = g + all_ref[k]
        g_ref[...] = g
        d_ref[...], nm_ref[...], nv_ref[...] = _adamw_math(w_ref[...], g, m_ref[...], v_ref[...])
        total = jnp.sum(g[loss_row:loss_row + 1, :]) * (0.5 / D_MODEL)
        loss_ref[...] = jnp.full(loss_ref.shape, total, F32)

    full = lambda s: pl.BlockSpec(s, lambda i: (0,) * len(s))
    pack = (PACK_ROWS, D_MODEL)
    return pl.pallas_call(
        body, grid=(1,), name="small_sum_adamw",
        in_specs=[full((N_DEV,) + pack), full(pack), full(pack), full(pack)],
        out_specs=[full(pack)] * 4 + [full((8, 128))],
        out_shape=[SDS(pack, F32)] * 4 + [SDS((8, 128), F32)],
        compiler_params=_cp("arbitrary"),
    )(all_small, w, m, v)


def _pack_small(grp, scale, g_mix, g_mlp, g_f, loss_lanes):
    def part(vec):
        vec = vec.reshape(1, -1)
        return jnp.pad(vec, ((0, 7), (0, D_MODEL - vec.shape[1])))
    return jnp.concatenate([grp.reshape(-1, D_MODEL), part(scale), part(g_mix), part(g_mlp), part(g_f),
                            part(loss_lanes)], axis=0)


def _unpack_small(pack):
    n_grp = len(POOL_WINDOWS) * POOL_GROUP_W * POOL_GROUP_W // D_MODEL
    grp = pack[:n_grp].reshape(1, len(POOL_WINDOWS), POOL_GROUP_W, POOL_GROUP_W)
    scale = pack[n_grp, :POOL_W].reshape(1, POOL_W)
    g_mix = pack[n_grp + 8].reshape(1, D_MODEL)
    g_mlp = pack[n_grp + 16].reshape(1, D_MODEL)
    g_f = pack[n_grp + 24].reshape(D_MODEL)
    return grp, scale, g_mix, g_mlp, g_f


def _block_diag(grp):
    out = jnp.zeros((POOL_W, POOL_W), grp.dtype)
    for k in range(len(POOL_WINDOWS)):
        out = lax.dynamic_update_slice(out, grp[k], (k * POOL_GROUP_W, k * POOL_GROUP_W))
    return out


def kernel(x, norm_mix_g, w_in, w_att_out, w_pool_grp, pool_scale, w_pool_out, w_out, norm_mlp_g, w_mlp_in, w_mlp_out, norm_final_g, loss_target, m_norm_mix_g, m_w_in, m_w_att_out, m_w_pool_grp, m_pool_scale, m_w_pool_out, m_w_out, m_norm_mlp_g, m_w_mlp_in, m_w_mlp_out, m_norm_final_g, v_norm_mix_g, v_w_in, v_w_att_out, v_w_pool_grp, v_pool_scale, v_w_pool_out, v_w_out, v_norm_mlp_g, v_w_mlp_in, v_w_mlp_out, v_norm_final_g):
    S = x.shape[1]
    xs, target = x[0], loss_target[0]
    big = [w_in[0], w_att_out[0], w_pool_out[0], w_out[0], w_mlp_in[0], w_mlp_out[0]]
    big_m = [m_w_in[0], m_w_att_out[0], m_w_pool_out[0], m_w_out[0], m_w_mlp_in[0], m_w_mlp_out[0]]
    big_v = [v_w_in[0], v_w_att_out[0], v_w_pool_out[0], v_w_out[0], v_w_mlp_in[0], v_w_mlp_out[0]]

    chip = 2 * lax.axis_index("x") + lax.axis_index("y")
    core = lax.axis_index("c")
    place = jnp.stack([chip, core]).astype(jnp.int32)
    names = ("w_in", "w_att_out", "w_pool_out", "w_out", "w_mlp_in", "w_mlp_out")

    shards = [w.astype(BF16) for w in big]
    bufs = [_place_own(sh, N_CHIPS, chip) for sh in shards]
    (wg_in,) = _allgather_weights(shards[:1], bufs[:1])
    wbd = _block_diag(w_pool_grp[0]).astype(BF16)
    g_final = norm_final_g.reshape(1, D_MODEL)
    lane = lax.broadcasted_iota(jnp.int32, (GROUP_W, GROUP_W), 0) // HEAD_W
    head_ones = (lane == lane.T).astype(BF16)

    (u, qkv0, qkv1, qkv2, pz, gates), landed = _norm_inproj(xs, norm_mix_g, wg_in, shards[1:], bufs[1:])
    wg_ao, wg_po, wg_out, wg_mi, wg_mo = _pair_forward(landed, [sh.shape[0] for sh in shards[1:]])
    wg_out = wg_out.reshape(D_MODEL, D_MODEL)
    qkv = (qkv0, qkv1, qkv2)
    att = [_attn_fwd(qkv[grp], grp) for grp in range(3)]
    a, lt0, lt1, lt2, pooled, mixed, p, merged, h1, m = _mixer_out(
        [o for o, _ in att], [l for _, l in att], pz, gates, xs, wg_ao, wg_po, wbd, pool_scale, wg_out, norm_mlp_g)
    hid, dh2, dh2b, loss_lanes, dg_final = _mlp_fwd_loss(m, h1, target, wg_mi, wg_mo, g_final)

    def pair_reduce(grads, grad_names):
        recv = _pair_exchange([narrow for _, narrow in grads])
        pair = [_pair_sum(place, g, r, f"pair_sum_{nm}") for (g, _), r, nm in zip(grads, recv, grad_names)]
        return [own for own, _ in pair], [s for _, s in pair]

    def chip_reduce(owns, landed_sums, grad_names):
        return [_chip_sum(place, own, r, f"chip_sum_{nm}") for own, r, nm in zip(owns, landed_sums, grad_names)]

    dpre, dh1, dh1b, dg_mlp = _mlp_bwd(dh2, dh2b, hid, h1, wg_mi, wg_mo, norm_mlp_g)
    own_mlp, sums_mlp = pair_reduce(_wgrads_mlp(m, dpre, hid, dh2b), names[4:])
    (da1, dp1, dgates, da0, dag1, dag2, dd0, dd1, dd2, dmixed, dqp, dscale), landed_mlp = _mixer_bwd(
        dh1b, a, p, mixed, gates, wg_out, wg_ao, wg_po, wbd, pool_scale, head_ones, sums_mlp)
    g_mi, g_mo = chip_reduce(own_mlp, landed_mlp, names[4:])
    grads_mixer, g_bd = _wgrads_mixer(a, da1, p, dp1, merged, dh1b, pooled, dmixed)
    dqkv = [_attn_bwd(qkv[grp], da_g, lt_g, dd_g, grp)
            for grp, (da_g, lt_g, dd_g) in enumerate(((da0, lt0, dd0), (dag1, lt1, dd1), (dag2, lt2, dd2)))]
    dz = _dz_assemble(dqkv, dqp)
    own_in, sums_in = pair_reduce([_wgrad_in(u, dz, dgates)] + grads_mixer, names[:4])
    dx, dg_mix, landed_in = _inproj_dx(dz, dgates, dh1, xs, norm_mix_g, wg_in, sums_in)
    g_in, g_ao, g_po, g_out = chip_reduce(own_in, landed_in, names[:4])

    g_grp = jnp.stack([g_bd[k * POOL_GROUP_W:(k + 1) * POOL_GROUP_W, k * POOL_GROUP_W:(k + 1) * POOL_GROUP_W]
                       for k in range(len(POOL_WINDOWS))])
    small = _pack_small(g_grp, dscale, dg_mix, dg_mlp, dg_final, loss_lanes)
    full, small_all = _finish_exchange([g_in, g_ao, g_po, g_out, g_mi, g_mo],
                                       _place_own(small, N_DEV, 2 * chip + core))

    zero = jnp.zeros((D_MODEL,), F32)
    small_w = _pack_small(w_pool_grp[0], pool_scale, norm_mix_g, norm_mlp_g, norm_final_g, zero)
    small_m = _pack_small(m_w_pool_grp[0], m_pool_scale, m_norm_mix_g, m_norm_mlp_g, m_norm_final_g, zero)
    small_v = _pack_small(v_w_pool_grp[0], v_pool_scale, v_norm_mix_g, v_norm_mlp_g, v_norm_final_g, zero)
    sg, sd, sm, sv, loss_tile = _small_sum_adamw(small_all, small_w, small_m, small_v)
    full = [full[0].reshape(CHUNKS_PER_SHARD, D_MODEL, CHUNK)] + list(full[1:])
    upd = [_adamw(w, g, mm, vv, f"adamw_{nm}") for w, g, mm, vv, nm in zip(big, full, big_m, big_v, names)]

    def ordered(small_pack, bigs):
        grp, scale, g_mix, g_mlp, g_f = _unpack_small(small_pack)
        b_in, b_ao, b_po, b_out, b_mi, b_mo = [b[None] for b in bigs]
        return (g_mix, b_in, b_ao, grp, scale, b_po, b_out, g_mlp, b_mi, b_mo, g_f)

    return (loss_tile[0, 0], dx[None],
            *ordered(sg, [t[0] for t in upd]),
            *ordered(sd, [t[1] for t in upd]),
            *ordered(sm, [t[2] for t in upd]),
            *ordered(sv, [t[3] for t in upd]))
```

```python
import functools

import jax
import jax.numpy as jnp
from jax import lax
from jax.experimental import pallas as pl
from jax.experimental.pallas import tpu as pltpu

F32 = jnp.float32
BF16 = jnp.bfloat16
SDS = jax.ShapeDtypeStruct
MESH = pl.DeviceIdType.MESH

D_MODEL = 1024
D_FF = 4096
N_CHIPS = 4
N_DEV = 8
DILATIONS = (1, 4, 16)
BAND = 128
GROUP_W = 256
PAIR_W = 128
HEAD_W = 64
POOL_W = 768
POOL_GROUP_W = 192
POOL_WINDOWS = (2, 4, 8, 16)
POOL_HALO = 16
N_IN = 5120
CHUNK = 256
N_CHUNKS = N_IN // CHUNK
N_DZ_CHUNKS = 12
CHUNKS_PER_SHARD = 5
WGRAD_IN_GROUP = 4
NORM_EPS = 1e-6
ALIBI_MAX_BIAS = 8.0
N_HEADS = 12
NEG = -1e30

ADAM_LR, ADAM_B1, ADAM_B2, ADAM_EPS, ADAM_WD, ADAM_STEP = 0.001, 0.9, 0.999, 1e-08, 0.01, 10

TM = 512
TMB = 512
ATT_TILE = ((1, 4), (4, 1), (4, 1))
BK = 4096
ELEMENTWISE_BLOCK = 1 << 20
VMEM_LIMIT = 56 * 1024 * 1024
PACK_ROWS = 184

NT = (((1,), (1,)), ((), ()))
TN = (((0,), (0,)), ((), ()))


def _cp(*sem):
    return pltpu.CompilerParams(dimension_semantics=sem, vmem_limit_bytes=VMEM_LIMIT)


def _resident(shape):
    nd = len(shape)
    return pl.BlockSpec(shape, lambda *_: (0,) * nd, pipeline_mode=pl.Buffered(1))


def _row_block(rows, cap=256):
    return max(b for b in range(16, min(rows, cap) + 1, 16) if rows % b == 0)


def _dot(a, b):
    return jnp.dot(a, b, preferred_element_type=F32)


def _dot_nt(a, b):
    return lax.dot_general(a, b, NT, preferred_element_type=F32)


def _dot_tn(a, b):
    return lax.dot_general(a, b, TN, preferred_element_type=F32)


def _w_in_chunk(w_ref, n):
    return w_ref[n // CHUNKS_PER_SHARD, :, (n % CHUNKS_PER_SHARD) * CHUNK:(n % CHUNKS_PER_SHARD + 1) * CHUNK]


def _sigmoid(x):
    return 0.5 * jnp.tanh(0.5 * x.astype(F32)) + 0.5


def _rms_fwd(x, g):
    r = lax.rsqrt(jnp.mean(x * x, axis=-1, keepdims=True) + NORM_EPS)
    xh = x * r
    return xh * g, xh, r


def _rms_bwd(dy, xh, r, g):
    dxh = dy * g
    return r * (dxh - xh * jnp.mean(dxh * xh, axis=-1, keepdims=True))


def _deinterleave_store(val, s_ref, out_ref, lead, d, rows, dtype):
    if d == 1:
        out_ref[lead + (0,)] = val.astype(dtype)
        return
    for h in range(2):
        s_ref[h] = val[:, h * PAIR_W:(h + 1) * PAIR_W]
    for r in range(d):
        for h in range(2):
            out_ref[lead + (r, slice(None), slice(h * PAIR_W, (h + 1) * PAIR_W))] = (
                s_ref[h, pl.ds(r, rows // d, stride=d), :].astype(dtype))


def _interleave_load(in_ref, lead, s_ref, d, rows):
    for r in range(d):
        for h in range(2):
            s_ref[h, pl.ds(r, rows // d, stride=d), :] = (
                in_ref[lead + (r, slice(None), slice(h * PAIR_W, (h + 1) * PAIR_W))].astype(F32))


def _norm_inproj(x, g, w_in, shards, bufs):
    S = x.shape[0]
    n_tiles = S // TM
    n = len(shards)

    def body(*refs):
        x_ref, g_ref, w_ref = refs[0:3]
        shard_refs = refs[3:3 + n]
        u_ref, q0_ref, q1_ref, q2_ref, pz_ref, gate_ref = refs[3 + 2 * n:9 + 2 * n]
        buf_refs = refs[9 + 2 * n:9 + 3 * n]
        s_ref, send_sem, recv_sem = refs[9 + 3 * n:]
        i = pl.program_id(0)

        def copies():
            return _weight_half_copies(shard_refs, buf_refs, [sh.shape[0] for sh in shards], send_sem, recv_sem)

        @pl.when(i == 0)
        def _():
            for cpy in copies():
                cpy.start()

        u = _rms_fwd(x_ref[...], g_ref[...])[0].astype(BF16)
        u_ref[...] = u
        qkv_refs = (q0_ref, q1_ref, q2_ref)
        for k in range(N_CHUNKS):
            zc = _dot(u, _w_in_chunk(w_ref, k))
            if k < 9:
                which, grp = k // 3, k % 3
                if which == 0:
                    zc = zc * 0.125
                _deinterleave_store(zc, s_ref, qkv_refs[grp], (which,), DILATIONS[grp], TM, BF16)
            elif k < 12:
                pz_ref[:, (k - 9) * CHUNK:(k - 8) * CHUNK] = zc
            else:
                gate_ref[:, (k - 12) * CHUNK:(k - 11) * CHUNK] = zc.astype(BF16)

        @pl.when(i == n_tiles - 1)
        def _():
            for cpy in copies():
                cpy.wait()

    row = lambda w: pl.BlockSpec((TM, w), lambda i: (i, 0))
    res = pl.pallas_call(
        body, grid=(n_tiles,), name="norm_inproj",
        in_specs=[row(D_MODEL), _resident((1, D_MODEL)), _resident(w_in.shape)] + [ANY] * (2 * n),
        out_specs=[row(D_MODEL)]
        + [pl.BlockSpec((3, d, TM // d, GROUP_W), lambda i: (0, 0, i, 0)) for d in DILATIONS]
        + [row(POOL_W), row(2 * D_MODEL)] + [ANY] * n,
        out_shape=[SDS((S, D_MODEL), BF16)]
        + [SDS((3, d, S // d, GROUP_W), BF16) for d in DILATIONS]
        + [SDS((S, POOL_W), F32), SDS((S, 2 * D_MODEL), BF16)] + [SDS(b.shape, b.dtype) for b in bufs],
        scratch_shapes=[pltpu.VMEM((2, TM, PAIR_W), F32), pltpu.SemaphoreType.DMA((3 * n,)),
                        pltpu.SemaphoreType.DMA((3 * n,))],
        input_output_aliases={3 + n + w: 6 + w for w in range(n)},
        compiler_params=_cp("arbitrary"),
    )(x, g, w_in, *shards, *bufs)
    return res[:6], res[6:]


def _band_bias(grp, d):
    row = lax.broadcasted_iota(jnp.int32, (BAND, 2 * BAND), 0)
    col = lax.broadcasted_iota(jnp.int32, (BAND, 2 * BAND), 1)
    steps = BAND + row - col
    valid = (steps >= 0) & (steps <= BAND)
    stepsf = (steps * d).astype(F32)
    biases = []
    for hh in range(4):
        slope = 2.0 ** (-ALIBI_MAX_BIAS * (grp * 4 + hh + 1) / N_HEADS)
        biases.append(jnp.where(valid, -slope * stepsf, NEG))
    return biases, col


def _attn_tiles(grp, L):
    rr, rb = ATT_TILE[grp]
    rb = min(rb, L // BAND)
    return rr, rb, L // (rb * BAND)


def _kv_tile(cur_ref, prev_ref, rr, rb, cs):
    if rb == 0:
        return jnp.concatenate([prev_ref[rr, :, cs], cur_ref[rr, 0:BAND, cs]], axis=0)
    return cur_ref[rr, (rb - 1) * BAND:(rb + 1) * BAND, cs]


def _attn_fwd(qkv, grp):
    d = DILATIONS[grp]
    L = qkv.shape[2]
    RR, RB, nb = _attn_tiles(grp, L)

    def body(q_ref, kc_ref, kp_ref, vc_ref, vp_ref, o_ref, lse_ref):
        i = pl.program_id(0)
        biases, col = _band_bias(grp, d)
        first_keys_ok = (col >= BAND) | (i > 0)
        is_a = lax.broadcasted_iota(jnp.int32, (BAND, PAIR_W), 1) < HEAD_W
        heads = [(rr, rb, cp, h2) for rr in range(RR) for rb in range(RB) for cp in range(2) for h2 in range(2)]

        def tile(head):
            rr, rb, cp, _ = head
            return rr, rb, slice(rb * BAND, (rb + 1) * BAND), slice(cp * PAIR_W, (cp + 1) * PAIR_W)

        def scores(head):
            rr, rb, rows, cs = tile(head)
            q2 = q_ref[rr, rows, cs]
            b = biases[head[2] * 2 + head[3]]
            if rb == 0:
                b = jnp.where(first_keys_ok, b, NEG)
            sel = is_a if head[3] == 0 else jnp.logical_not(is_a)
            return _dot_nt(jnp.where(sel, q2, jnp.zeros_like(q2)), _kv_tile(kc_ref, kp_ref, rr, rb, cs)) + b

        s_next = scores(heads[0])
        res = {}
        for idx, head in enumerate(heads):
            s = s_next
            if idx + 1 < len(heads):
                s_next = scores(heads[idx + 1])
            rr, rb, rows, cs = tile(head)
            m = jnp.max(s, axis=-1, keepdims=True)
            p = jnp.exp(s - m)
            l = jnp.sum(p, axis=-1, keepdims=True)
            o = _dot(p.astype(BF16), _kv_tile(vc_ref, vp_ref, rr, rb, cs)) * (1.0 / l)
            res[head[3]] = (o, m + jnp.log(l))
            if head[3] == 1:
                o_ref[rr, rows, cs] = jnp.where(is_a, res[0][0], res[1][0]).astype(BF16)
                lse_ref[rr, rows, cs] = jnp.where(is_a, res[0][1], res[1][1])

    cur = lambda w: pl.BlockSpec((None, RR, RB * BAND, GROUP_W), lambda i, j: (w, j, i, 0))
    prev = lambda w: pl.BlockSpec((None, RR, BAND, GROUP_W), lambda i, j: (w, j, jnp.maximum(i * RB - 1, 0), 0))
    out = pl.BlockSpec((RR, RB * BAND, GROUP_W), lambda i, j: (j, i, 0))
    return pl.pallas_call(
        body, grid=(nb, d // RR), name=f"attn_fwd_g{grp}",
        in_specs=[cur(0), cur(1), prev(1), cur(2), prev(2)],
        out_specs=[out, out],
        out_shape=[SDS((d, L, GROUP_W), BF16), SDS((d, L, GROUP_W), F32)],
        compiler_params=_cp("parallel", "parallel"),
    )(qkv, qkv, qkv, qkv, qkv)


def _pool_column_select(col, vals):
    return jnp.where(col < POOL_GROUP_W, vals[0],
                     jnp.where(col < 2 * POOL_GROUP_W, vals[1],
                               jnp.where(col < 3 * POOL_GROUP_W, vals[2], vals[3])))


def _pool_inv_count(i, rows):
    t = i * rows + lax.broadcasted_iota(jnp.int32, (rows, POOL_W), 0)
    col = lax.broadcasted_iota(jnp.int32, (rows, POOL_W), 1)
    win = _pool_column_select(col, POOL_WINDOWS)
    return 1.0 / jnp.minimum(t + 1, win).astype(F32), col


def _mixer_out(outs, lses, pz, gates, x, w_ao, w_po, wbd, scale, w_out, g_mlp):
    S = x.shape[0]
    n_tiles = S // TMB

    def body(o0_ref, l0_ref, o1_ref, l1_ref, o2_ref, l2_ref, pz_ref, halo_ref, gate_ref, x_ref,
             wao_ref, wpo_ref, wbd_ref, sc_ref, wout_ref, g_ref,
             a_ref, lt0_ref, lt1_ref, lt2_ref, pooled_ref, mixed_ref, p_ref, merged_ref, h1_ref, m_ref,
             so1, sl1, so2, sl2, slt, ext_ref):
        i = pl.program_id(0)
        _interleave_load(o1_ref, (), so1, DILATIONS[1], TMB)
        _interleave_load(l1_ref, (), sl1, DILATIONS[1], TMB)
        _interleave_load(o2_ref, (), so2, DILATIONS[2], TMB)
        _interleave_load(l2_ref, (), sl2, DILATIONS[2], TMB)
        for h in range(2):
            hs = slice(h * PAIR_W, (h + 1) * PAIR_W)
            l0, l1, l2 = l0_ref[0, :, hs], sl1[h], sl2[h]
            mx = jnp.maximum(jnp.maximum(l0, l1), l2)
            e0, e1, e2 = jnp.exp(l0 - mx), jnp.exp(l1 - mx), jnp.exp(l2 - mx)
            den = e0 + e1 + e2
            a_ref[:, hs] = ((e0 * o0_ref[0, :, hs].astype(F32) + e1 * so1[h] + e2 * so2[h])
                            * (1.0 / den)).astype(BF16)
            slt[h] = mx + jnp.log(den)
        lt = jnp.concatenate([slt[0], slt[1]], axis=1)
        lt0_ref[0] = lt
        for ref, d in ((lt1_ref, DILATIONS[1]), (lt2_ref, DILATIONS[2])):
            for r in range(d):
                for h in range(2):
                    ref[r, :, h * PAIR_W:(h + 1) * PAIR_W] = slt[h, pl.ds(r, TMB // d, stride=d), :]

        pz_t = pz_ref[...]
        ext_ref[0:POOL_HALO, :] = jnp.where(i > 0, halo_ref[...], 0.0)
        ext_ref[POOL_HALO:, :] = pz_t
        sums = []
        acc = ext_ref[...]
        for k in (1, 2, 4, 8):
            acc = acc + pltpu.roll(acc, k, 0)
            sums.append(acc[POOL_HALO:, :])
        inv_cnt, col = _pool_inv_count(i, TMB)
        pooled = (_pool_column_select(col, sums) * inv_cnt - pz_t).astype(BF16)
        pooled_ref[...] = pooled
        mixed = _dot(pooled, wbd_ref[...])
        mixed_ref[...] = mixed.astype(BF16)
        p = (mixed * sc_ref[...]).astype(BF16)
        p_ref[...] = p

        a = a_ref[...]
        for j in range(N_CHIPS):
            js = slice(j * CHUNK, (j + 1) * CHUNK)
            ga = gate_ref[:, js]
            gp = gate_ref[:, D_MODEL + j * CHUNK:D_MODEL + (j + 1) * CHUNK]
            mj = _sigmoid(ga) * _dot(a, wao_ref[j]) + _sigmoid(gp) * _dot(p, wpo_ref[j])
            merged_ref[:, js] = mj.astype(BF16)
        h1 = x_ref[...] + _dot(merged_ref[...], wout_ref[...])
        h1_ref[...] = h1
        m_ref[...] = _rms_fwd(h1, g_ref[...])[0].astype(BF16)

    row = lambda w: pl.BlockSpec((TMB, w), lambda i: (i, 0))
    grp_spec = lambda d: pl.BlockSpec((d, TMB // d, GROUP_W), lambda i: (0, i, 0))
    halo = pl.BlockSpec((POOL_HALO, POOL_W), lambda i: (jnp.maximum(i * (TMB // POOL_HALO) - 1, 0), 0))
    d0, d1, d2 = DILATIONS
    return pl.pallas_call(
        body, grid=(n_tiles,), name="mixer_out",
        in_specs=[grp_spec(d0), grp_spec(d0), grp_spec(d1), grp_spec(d1), grp_spec(d2), grp_spec(d2),
                  row(POOL_W), halo, row(2 * D_MODEL), row(D_MODEL),
                  _resident(w_ao.shape), _resident(w_po.shape), _resident(wbd.shape), _resident(scale.shape),
                  _resident(w_out.shape), _resident(g_mlp.shape)],
        out_specs=[row(GROUP_W), grp_spec(d0), grp_spec(d1), grp_spec(d2),
                   row(POOL_W), row(POOL_W), row(POOL_W), row(D_MODEL), row(D_MODEL), row(D_MODEL)],
        out_shape=[SDS((S, GROUP_W), BF16)] + [SDS((d, S // d, GROUP_W), F32) for d in DILATIONS]
        + [SDS((S, POOL_W), BF16), SDS((S, POOL_W), BF16), SDS((S, POOL_W), BF16),
           SDS((S, D_MODEL), BF16), SDS((S, D_MODEL), F32), SDS((S, D_MODEL), BF16)],
        scratch_shapes=[pltpu.VMEM((2, TMB, PAIR_W), F32) for _ in range(5)]
        + [pltpu.VMEM((TMB + POOL_HALO, POOL_W), F32)],
        compiler_params=_cp("parallel"),
    )(outs[0], lses[0], outs[1], lses[1], outs[2], lses[2], pz, pz, gates, x,
      w_ao, w_po, wbd, scale, w_out, g_mlp)


def _mlp_fwd_loss(m, h1, target, w_mi, w_mo, g_f):
    S = m.shape[0]

    def body(m_ref, h1_ref, t_ref, wmi_ref, wmo_ref, g_ref, hid_ref, dh2_ref, dh2b_ref, loss_ref, dg_ref):
        @pl.when(pl.program_id(0) == 0)
        def _():
            loss_ref[...] = jnp.zeros_like(loss_ref)
            dg_ref[...] = jnp.zeros_like(dg_ref)

        mt = m_ref[...]
        acc = h1_ref[...]
        for c in range(N_CHIPS):
            hid = jnp.square(jnp.maximum(_dot(mt, wmi_ref[c]), 0.0)).astype(BF16)
            hid_ref[:, c * D_MODEL:(c + 1) * D_MODEL] = hid
            acc = acc + _dot(hid, wmo_ref[c])
        g = g_ref[...]
        y, hh, r = _rms_fwd(acc, g)
        e = y - t_ref[...]
        loss_ref[...] += jnp.sum(e * e, axis=0, keepdims=True)
        dy = e * (1.0 / D_MODEL)
        dg_ref[...] += jnp.sum(dy * hh, axis=0, keepdims=True)
        dh2 = _rms_bwd(dy, hh, r, g)
        dh2_ref[...] = dh2
        dh2b_ref[...] = dh2.astype(BF16)

    row = lambda w: pl.BlockSpec((TM, w), lambda i: (i, 0))
    vec = pl.BlockSpec((1, D_MODEL), lambda i: (0, 0))
    return pl.pallas_call(
        body, grid=(S // TM,), name="mlp_fwd_loss",
        in_specs=[row(D_MODEL), row(D_MODEL), row(D_MODEL), _resident(w_mi.shape), _resident(w_mo.shape),
                  _resident(g_f.shape)],
        out_specs=[row(D_FF), row(D_MODEL), row(D_MODEL), vec, vec],
        out_shape=[SDS((S, D_FF), BF16), SDS((S, D_MODEL), F32), SDS((S, D_MODEL), BF16),
                   SDS((1, D_MODEL), F32), SDS((1, D_MODEL), F32)],
        compiler_params=_cp("arbitrary"),
    )(m, h1, target, w_mi, w_mo, g_f)


def _mlp_bwd(dh2, dh2b, hid, h1, w_mi, w_mo, g_mlp):
    S = dh2.shape[0]

    def body(dh2_ref, dh2b_ref, hid_ref, h1_ref, wmi_ref, wmo_ref, g_ref, dpre_ref, dh1_ref, dh1b_ref, dg_ref):
        @pl.when(pl.program_id(0) == 0)
        def _():
            dg_ref[...] = jnp.zeros_like(dg_ref)

        d2 = dh2b_ref[...]
        dm = jnp.zeros((TM, D_MODEL), F32)
        for c in range(N_CHIPS):
            cs = slice(c * D_MODEL, (c + 1) * D_MODEL)
            dhid = _dot_nt(d2, wmo_ref[c])
            dpre = (dhid * (2.0 * jnp.sqrt(hid_ref[:, cs].astype(F32)))).astype(BF16)
            dpre_ref[:, cs] = dpre
            dm = dm + _dot_nt(dpre, wmi_ref[c])
        g = g_ref[...]
        _, hh, r = _rms_fwd(h1_ref[...], g)
        dg_ref[...] += jnp.sum(dm * hh, axis=0, keepdims=True)
        dh1 = dh2_ref[...] + _rms_bwd(dm, hh, r, g)
        dh1_ref[...] = dh1
        dh1b_ref[...] = dh1.astype(BF16)

    row = lambda w: pl.BlockSpec((TM, w), lambda i: (i, 0))
    return pl.pallas_call(
        body, grid=(S // TM,), name="mlp_bwd",
        in_specs=[row(D_MODEL), row(D_MODEL), row(D_FF), row(D_MODEL), _resident(w_mi.shape),
                  _resident(w_mo.shape), _resident(g_mlp.shape)],
        out_specs=[row(D_FF), row(D_MODEL), row(D_MODEL), pl.BlockSpec((1, D_MODEL), lambda i: (0, 0))],
        out_shape=[SDS((S, D_FF), BF16), SDS((S, D_MODEL), F32), SDS((S, D_MODEL), BF16), SDS((1, D_MODEL), F32)],
        compiler_params=_cp("arbitrary"),
    )(dh2, dh2b, hid, h1, w_mi, w_mo, g_mlp)


def _mixer_bwd(dh1b, a, p, mixed, gates, w_out, w_ao, w_po, wbd, scale, head_ones, sums):
    S = a.shape[0]
    n_tiles = S // TMB
    n = len(sums)

    def body(*refs):
        (dh1b_ref, a_ref, p_ref, mixed_ref, gate_ref, wout_ref, wao_ref, wpo_ref, wbd_ref, sc_ref,
         ones_ref) = refs[0:11]
        sum_refs = refs[11:11 + n]
        (da1_ref, dp1_ref, dgate_ref, da0_ref, dag1_ref, dag2_ref, dd0_ref, dd1_ref, dd2_ref,
         dmixed_ref, dqp_ref, dscale_ref) = refs[11 + n:23 + n]
        land_refs = refs[23 + n:23 + 2 * n]
        s_da, s_dd, send_sem, recv_sem = refs[23 + 2 * n:]
        i = pl.program_id(0)

        @pl.when(i == 0)
        def _():
            dscale_ref[...] = jnp.zeros_like(dscale_ref)
            for cpy in _chip_sum_copies(sum_refs, land_refs, send_sem, recv_sem):
                cpy.start()

        dmerged = _dot_nt(dh1b_ref[...], wout_ref[...])
        a = a_ref[...]
        p = p_ref[...]
        da = jnp.zeros((TMB, GROUP_W), F32)
        dp = jnp.zeros((TMB, POOL_W), F32)
        for j in range(N_CHIPS):
            js = slice(j * CHUNK, (j + 1) * CHUNK)
            sa = _sigmoid(gate_ref[:, js])
            sp = _sigmoid(gate_ref[:, D_MODEL + j * CHUNK:D_MODEL + (j + 1) * CHUNK])
            dmj = dmerged[:, js]
            da1 = (dmj * sa).astype(BF16)
            dp1 = (dmj * sp).astype(BF16)
            da1_ref[:, js] = da1
            dp1_ref[:, js] = dp1
            dgate_ref[j] = (dmj * _dot(a, wao_ref[j]) * sa * (1.0 - sa)).astype(BF16)
            dgate_ref[N_CHIPS + j] = (dmj * _dot(p, wpo_ref[j]) * sp * (1.0 - sp)).astype(BF16)
            da = da + _dot_nt(da1, wao_ref[j])
            dp = dp + _dot_nt(dp1, wpo_ref[j])

        prod = da * a.astype(F32)
        hi = prod.astype(BF16)
        lo = (prod - hi.astype(F32)).astype(BF16)
        dd = _dot(hi, ones_ref[...]) + _dot(lo, ones_ref[...])
        for ref, val, sref, dtype in ((da0_ref, da, s_da, BF16), (dd0_ref, dd, s_dd, F32)):
            ref[0] = val.astype(dtype)
            for h in range(2):
                sref[h] = val[:, h * PAIR_W:(h + 1) * PAIR_W]
        for refs, d in (((dag1_ref, dd1_ref), DILATIONS[1]), ((dag2_ref, dd2_ref), DILATIONS[2])):
            for r in range(d):
                for h in range(2):
                    hs = slice(h * PAIR_W, (h + 1) * PAIR_W)
                    refs[0][r, :, hs] = s_da[h, pl.ds(r, TMB // d, stride=d), :].astype(BF16)
                    refs[1][r, :, hs] = s_dd[h, pl.ds(r, TMB // d, stride=d), :]

        sc = sc_ref[...]
        dscale_ref[...] += jnp.sum(dp * mixed_ref[...].astype(F32), axis=0, keepdims=True)
        dmixed = (dp * sc).astype(BF16)
        dmixed_ref[...] = dmixed
        inv_cnt, _ = _pool_inv_count(i, TMB)
        dqp_ref[...] = (_dot_nt(dmixed, wbd_ref[...]) * inv_cnt).astype(BF16)

        @pl.when(i == n_tiles - 1)
        def _():
            for cpy in _chip_sum_copies(sum_refs, land_refs, send_sem, recv_sem):
                cpy.wait()

    row = lambda w: pl.BlockSpec((TMB, w), lambda i: (i, 0))
    grp_spec = lambda d: pl.BlockSpec((d, TMB // d, GROUP_W), lambda i: (0, i, 0))
    d0, d1, d2 = DILATIONS
    res = pl.pallas_call(
        body, grid=(n_tiles,), name="mixer_bwd",
        in_specs=[row(D_MODEL), row(GROUP_W), row(POOL_W), row(POOL_W), row(2 * D_MODEL),
                  _resident(w_out.shape), _resident(w_ao.shape), _resident(w_po.shape), _resident(wbd.shape),
                  _resident(scale.shape), _resident(head_ones.shape)] + [ANY] * n,
        out_specs=[row(D_MODEL), row(D_MODEL), pl.BlockSpec((2 * N_CHIPS, TMB, CHUNK), lambda i: (0, i, 0)),
                   grp_spec(d0), grp_spec(d1), grp_spec(d2), grp_spec(d0), grp_spec(d1), grp_spec(d2),
                   row(POOL_W), row(POOL_W), pl.BlockSpec((1, POOL_W), lambda i: (0, 0))] + [ANY] * n,
        out_shape=[SDS((S, D_MODEL), BF16), SDS((S, D_MODEL), BF16), SDS((2 * N_CHIPS, S, CHUNK), BF16)]
        + [SDS((d, S // d, GROUP_W), BF16) for d in DILATIONS]
        + [SDS((d, S // d, GROUP_W), F32) for d in DILATIONS]
        + [SDS((S, POOL_W), BF16), SDS((S, POOL_W), BF16), SDS((1, POOL_W), F32)]
        + [SDS(t.shape, t.dtype) for t in sums],
        scratch_shapes=[pltpu.VMEM((2, TMB, PAIR_W), F32), pltpu.VMEM((2, TMB, PAIR_W), F32),
                        pltpu.SemaphoreType.DMA((3 * n,)), pltpu.SemaphoreType.DMA((3 * n,))],
        compiler_params=_cp("arbitrary"),
    )(dh1b, a, p, mixed, gates, w_out, w_ao, w_po, wbd, scale, head_ones, *sums)
    return res[:12], res[12:]


def _attn_bwd(qkv, da, lt, dd, grp):
    d = DILATIONS[grp]
    L = qkv.shape[2]
    RR, RB, nb = _attn_tiles(grp, L)

    def body(q_ref, kc_ref, kp_ref, vc_ref, vp_ref, da_ref, lt_ref, dd_ref, dq_ref, dk_ref, dv_ref, dk_acc, dv_acc):
        i = pl.program_id(1)

        @pl.when(i == 0)
        def _():
            dk_acc[...] = jnp.zeros_like(dk_acc)
            dv_acc[...] = jnp.zeros_like(dv_acc)

        def compute(cur, prv):
            dk_acc[cur] = jnp.zeros((RR, RB * BAND, GROUP_W), F32)
            dv_acc[cur] = jnp.zeros((RR, RB * BAND, GROUP_W), F32)
            biases, col = _band_bias(grp, d)
            first_keys_ok = (col >= BAND) | (i > 0)
            is_a = lax.broadcasted_iota(jnp.int32, (BAND, PAIR_W), 1) < HEAD_W
            for rr in range(RR):
                for rb in range(RB):
                    rows = slice(rb * BAND, (rb + 1) * BAND)
                    for cp in range(2):
                        cs = slice(cp * PAIR_W, (cp + 1) * PAIR_W)
                        q2 = q_ref[rr, rows, cs]
                        da2 = da_ref[rr, rows, cs]
                        lt2 = lt_ref[rr, rows, cs]
                        dd2 = dd_ref[rr, rows, cs]
                        kcat = _kv_tile(kc_ref, kp_ref, rr, rb, cs)
                        vcat = _kv_tile(vc_ref, vp_ref, rr, rb, cs)
                        q2t = q2.astype(F32).T.astype(BF16)
                        da2t = da2.astype(F32).T.astype(BF16)
                        dqs, dkts, dvts, scores, dpvs = [], [], [], [], []
                        for h2 in range(2):
                            sel = is_a if h2 == 0 else jnp.logical_not(is_a)
                            b = biases[cp * 2 + h2]
                            if rb == 0:
                                b = jnp.where(first_keys_ok, b, NEG)
                            scores.append(_dot_nt(jnp.where(sel, q2, jnp.zeros_like(q2)), kcat) + b)
                            dpvs.append(_dot_nt(jnp.where(sel, da2, jnp.zeros_like(da2)), vcat))
                        for h2 in range(2):
                            lane0 = h2 * HEAD_W
                            p = jnp.exp(scores[h2] - lt2[:, lane0:lane0 + 1])
                            ds = (p * (dpvs[h2] - dd2[:, lane0:lane0 + 1])).astype(BF16)
                            dqs.append(_dot(ds, kcat))
                            dkts.append(_dot(q2t[lane0:lane0 + HEAD_W, :], ds))
                            dvts.append(_dot(da2t[lane0:lane0 + HEAD_W, :], p.astype(BF16)))
                        dq_ref[rr, rows, cs] = (jnp.where(is_a, dqs[0], dqs[1]) * 0.125).astype(BF16)
                        dkc = jnp.concatenate(dkts, axis=0).T
                        dvc = jnp.concatenate(dvts, axis=0).T
                        if rb == 0:
                            last = slice((RB - 1) * BAND, RB * BAND)
                            dk_acc[prv, rr, last, cs] += dkc[0:BAND]
                            dv_acc[prv, rr, last, cs] += dvc[0:BAND]
                            dk_acc[cur, rr, 0:BAND, cs] += dkc[BAND:]
                            dv_acc[cur, rr, 0:BAND, cs] += dvc[BAND:]
                        else:
                            both = slice((rb - 1) * BAND, (rb + 1) * BAND)
                            dk_acc[cur, rr, both, cs] += dkc
                            dv_acc[cur, rr, both, cs] += dvc

        def flush(prv):
            dk_ref[...] = dk_acc[prv].astype(BF16)
            dv_ref[...] = dv_acc[prv].astype(BF16)

        for parity in (0, 1):
            on = (i % 2) == parity
            pl.when(on & (i < nb))(functools.partial(compute, parity, 1 - parity))
            pl.when(on & (i > 0))(functools.partial(flush, 1 - parity))

    qi = lambda i: jnp.minimum(i, nb - 1)
    cur_w = lambda w: pl.BlockSpec((None, RR, RB * BAND, GROUP_W), lambda j, i: (w, j, qi(i), 0))
    prev_w = lambda w: pl.BlockSpec((None, RR, BAND, GROUP_W),
                                    lambda j, i: (w, j, jnp.maximum(qi(i) * RB - 1, 0), 0))
    blk = pl.BlockSpec((RR, RB * BAND, GROUP_W), lambda j, i: (j, qi(i), 0))
    late = pl.BlockSpec((RR, RB * BAND, GROUP_W), lambda j, i: (j, jnp.maximum(i - 1, 0), 0))
    return pl.pallas_call(
        body, grid=(d // RR, nb + 1), name=f"attn_bwd_g{grp}",
        in_specs=[cur_w(0), cur_w(1), prev_w(1), cur_w(2), prev_w(2), blk, blk, blk],
        out_specs=[blk, late, late],
        out_shape=[SDS((d, L, GROUP_W), BF16)] * 3,
        scratch_shapes=[pltpu.VMEM((2, RR, RB * BAND, GROUP_W), F32), pltpu.VMEM((2, RR, RB * BAND, GROUP_W), F32)],
        compiler_params=_cp("parallel", "arbitrary"),
    )(qkv, qkv, qkv, qkv, qkv, da, lt, dd)


def _dz_assemble(dqkv, dqp):
    S = dqp.shape[0]
    n_tiles = S // TMB

    def body(*refs):
        dqkv_refs = refs[0:9]
        dqp_ref, halo_ref = refs[9:11]
        dz_ref, s_ref, ext_ref = refs[11:]
        i = pl.program_id(0)

        for grp in range(3):
            for which in range(3):
                n = which * 3 + grp
                ref = dqkv_refs[grp * 3 + which]
                if DILATIONS[grp] == 1:
                    dz_ref[n] = ref[0]
                else:
                    _interleave_load(ref, (), s_ref, DILATIONS[grp], TMB)
                    for h in range(2):
                        dz_ref[n, :, h * PAIR_W:(h + 1) * PAIR_W] = s_ref[h].astype(BF16)

        dqp = dqp_ref[...].astype(F32)
        ext_ref[0:TMB, :] = dqp
        ext_ref[TMB:, :] = jnp.where(i < n_tiles - 1, halo_ref[...].astype(F32), 0.0)
        sums = []
        acc = ext_ref[...]
        for k in (1, 2, 4, 8):
            acc = acc + pltpu.roll(acc, TMB + POOL_HALO - k, 0)
            sums.append(acc[0:TMB, :])
        inv_cnt, col = _pool_inv_count(i, TMB)
        dpz = _pool_column_select(col, sums) - dqp / inv_cnt
        for t in range(3):
            dz_ref[9 + t] = dpz[:, t * CHUNK:(t + 1) * CHUNK].astype(BF16)

    row = lambda w: pl.BlockSpec((TMB, w), lambda i: (i, 0))
    grp_spec = lambda d: pl.BlockSpec((d, TMB // d, GROUP_W), lambda i: (0, i, 0))
    halo = pl.BlockSpec((POOL_HALO, POOL_W),
                        lambda i: (jnp.minimum((i + 1) * (TMB // POOL_HALO), S // POOL_HALO - 1), 0))
    flat = [t for grp in range(3) for t in dqkv[grp]]
    return pl.pallas_call(
        body, grid=(n_tiles,), name="dz_assemble",
        in_specs=[grp_spec(DILATIONS[grp]) for grp in range(3) for _ in range(3)] + [row(POOL_W), halo],
        out_specs=pl.BlockSpec((N_DZ_CHUNKS, TMB, CHUNK), lambda i: (0, i, 0)),
        out_shape=SDS((N_DZ_CHUNKS, S, CHUNK), BF16),
        scratch_shapes=[pltpu.VMEM((2, TMB, PAIR_W), F32), pltpu.VMEM((TMB + POOL_HALO, POOL_W), F32)],
        compiler_params=_cp("parallel"),
    )(*flat, dqp, dqp)


def _inproj_dx(dz, dgates, dh1, x, g, w_in, sums):
    S = x.shape[0]
    n_tiles = S // TM
    n = len(sums)

    def body(*refs):
        dz_ref, dgate_ref, dh1_ref, x_ref, g_ref, w_ref = refs[0:6]
        sum_refs = refs[6:6 + n]
        dx_ref, dg_ref = refs[6 + n:8 + n]
        land_refs = refs[8 + n:8 + 2 * n]
        sems = refs[8 + 2 * n:]
        i = pl.program_id(0)

        def copies():
            return _chip_sum_copies(sum_refs, land_refs, *sems)

        @pl.when(i == 0)
        def _():
            dg_ref[...] = jnp.zeros_like(dg_ref)
            for cpy in copies():
                cpy.start()

        du = jnp.zeros((TM, D_MODEL), F32)
        for k in range(N_CHUNKS):
            dzk = dz_ref[k] if k < N_DZ_CHUNKS else dgate_ref[k - N_DZ_CHUNKS]
            du = du + _dot_nt(dzk, _w_in_chunk(w_ref, k))
        gv = g_ref[...]
        _, xh, r = _rms_fwd(x_ref[...], gv)
        dg_ref[...] += jnp.sum(du * xh, axis=0, keepdims=True)
        dx_ref[...] = dh1_ref[...] + _rms_bwd(du, xh, r, gv)

        @pl.when(i == n_tiles - 1)
        def _():
            for cpy in copies():
                cpy.wait()

    row = lambda w: pl.BlockSpec((TM, w), lambda i: (i, 0))
    res = pl.pallas_call(
        body, grid=(n_tiles,), name="inproj_dx",
        in_specs=[pl.BlockSpec((N_DZ_CHUNKS, TM, CHUNK), lambda i: (0, i, 0)),
                  pl.BlockSpec((N_CHUNKS - N_DZ_CHUNKS, TM, CHUNK), lambda i: (0, i, 0)),
                  row(D_MODEL), row(D_MODEL), _resident(g.shape), _resident(w_in.shape)] + [ANY] * n,
        out_specs=[row(D_MODEL), pl.BlockSpec((1, D_MODEL), lambda i: (0, 0))] + [ANY] * n,
        out_shape=[SDS((S, D_MODEL), F32), SDS((1, D_MODEL), F32)] + [SDS(t.shape, t.dtype) for t in sums],
        scratch_shapes=[pltpu.SemaphoreType.DMA((3 * n,)), pltpu.SemaphoreType.DMA((3 * n,))],
        compiler_params=_cp("arbitrary"),
    )(dz, dgates, dh1, x, g, w_in, *sums)
    return res[0], res[1], res[2:]


def _wgrad(a, b, name, *, out_shape, a_spec, b_spec, out_spec, grid, n_out_cols=None, fill=None, narrow=True):
    k_axis = len(grid) - 1
    n_k = grid[k_axis]
    n_out = 2 if narrow else 1

    def body(a_ref, b_ref, *rest):
        o_ref = rest[-n_out]

        @pl.when(pl.program_id(k_axis) == 0)
        def _():
            o_ref[...] = jnp.zeros_like(o_ref)

        at = a_ref[...]
        if n_out_cols is None:
            o_ref[...] += _dot_tn(at, b_ref[...])
        elif n_out_cols[0] == "lead_both":
            for t in range(b_ref.shape[0]):
                o_ref[t] += _dot_tn(at, b_ref[t])
        else:
            w = n_out_cols[1]
            for t in range(o_ref.shape[0]):
                o_ref[t] += _dot_tn(at, b_ref[:, t * w:(t + 1) * w])

        if narrow:
            @pl.when(pl.program_id(k_axis) == n_k - 1)
            def _():
                rest[-1][...] = o_ref[...].astype(BF16)

    sem = ("parallel",) * k_axis + ("arbitrary",)
    extra = [] if fill is None else list(fill) if narrow else [fill]
    shapes = [out_shape, SDS(out_shape.shape, BF16)] if narrow else out_shape
    return pl.pallas_call(body, grid=grid, name=name, in_specs=[a_spec, b_spec] + [ANY] * len(extra),
                          out_specs=[out_spec] * n_out if narrow else out_spec, out_shape=shapes,
                          input_output_aliases={2 + t: t for t in range(len(extra))},
                          compiler_params=_cp(*sem))(a, b, *extra)


def _wgrad_in(u, dz, dgates):
    bk = min(BK, u.shape[0])
    nk = u.shape[0] // bk
    g = WGRAD_IN_GROUP
    kw = dict(n_out_cols=("lead_both", CHUNK), a_spec=pl.BlockSpec((bk, D_MODEL), lambda j, k: (k, 0)),
              b_spec=pl.BlockSpec((g, bk, CHUNK), lambda j, k: (j, k, 0)),
              out_shape=SDS((N_CHUNKS, D_MODEL, CHUNK), F32))
    first = _wgrad(u, dz, "wgrad_in_qkvp", grid=(N_DZ_CHUNKS // g, nk),
                   out_spec=pl.BlockSpec((g, D_MODEL, CHUNK), lambda j, k: (j, 0, 0)), **kw)
    both = _wgrad(u, dgates, "wgrad_in_gates", grid=((N_CHUNKS - N_DZ_CHUNKS) // g, nk), fill=first,
                  out_spec=pl.BlockSpec((g, D_MODEL, CHUNK), lambda j, k: (N_DZ_CHUNKS // g + j, 0, 0)), **kw)
    return [t.reshape(N_CHIPS, CHUNKS_PER_SHARD * D_MODEL, CHUNK) for t in both]


def _wgrads_mixer(a, da1, p, dp1, merged, dh1b, pooled, dmixed):
    bk = min(BK, a.shape[0])
    nk = a.shape[0] // bk
    g_ao = _wgrad(
        a, da1, "wgrad_att_out", grid=(nk,), n_out_cols=("cols_b", CHUNK),
        a_spec=pl.BlockSpec((bk,GROUP_W), lambda k: (k, 0)),
        b_spec=pl.BlockSpec((bk,D_MODEL), lambda k: (k, 0)),
        out_spec=pl.BlockSpec((N_CHIPS, GROUP_W, CHUNK), lambda k: (0, 0, 0)),
        out_shape=SDS((N_CHIPS, GROUP_W, CHUNK), F32))
    g_po = _wgrad(
        p, dp1, "wgrad_pool_out", grid=(nk,), n_out_cols=("cols_b", CHUNK),
        a_spec=pl.BlockSpec((bk,POOL_W), lambda k: (k, 0)),
        b_spec=pl.BlockSpec((bk,D_MODEL), lambda k: (k, 0)),
        out_spec=pl.BlockSpec((N_CHIPS, POOL_W, CHUNK), lambda k: (0, 0, 0)),
        out_shape=SDS((N_CHIPS, POOL_W, CHUNK), F32))
    g_out = _wgrad(
        merged, dh1b, "wgrad_out", grid=(nk,),
        a_spec=pl.BlockSpec((bk,D_MODEL), lambda k: (k, 0)),
        b_spec=pl.BlockSpec((bk,D_MODEL), lambda k: (k, 0)),
        out_spec=pl.BlockSpec((D_MODEL, D_MODEL), lambda k: (0, 0)),
        out_shape=SDS((D_MODEL, D_MODEL), F32))
    g_bd = _wgrad(
        pooled, dmixed, "wgrad_pool_grp", grid=(nk,),
        a_spec=pl.BlockSpec((bk,POOL_W), lambda k: (k, 0)),
        b_spec=pl.BlockSpec((bk,POOL_W), lambda k: (k, 0)),
        out_spec=pl.BlockSpec((POOL_W, POOL_W), lambda k: (0, 0)),
        out_shape=SDS((POOL_W, POOL_W), F32), narrow=False)
    g_out = [t.reshape(N_CHIPS, D_MODEL // N_CHIPS, D_MODEL) for t in g_out]
    return [g_ao, g_po, g_out], g_bd


def _wgrads_mlp(m, dpre, hid, dh2b):
    bk = min(BK, m.shape[0])
    nk = m.shape[0] // bk
    g_mi = _wgrad(
        m, dpre, "wgrad_mlp_in", grid=(N_CHIPS, nk),
        a_spec=pl.BlockSpec((bk,D_MODEL), lambda c, k: (k, 0)),
        b_spec=pl.BlockSpec((bk,D_MODEL), lambda c, k: (k, c)),
        out_spec=pl.BlockSpec((None, D_MODEL, D_MODEL), lambda c, k: (c, 0, 0)),
        out_shape=SDS((N_CHIPS, D_MODEL, D_MODEL), F32))
    g_mo = _wgrad(
        hid, dh2b, "wgrad_mlp_out", grid=(N_CHIPS, nk),
        a_spec=pl.BlockSpec((bk,D_MODEL), lambda c, k: (k, c)),
        b_spec=pl.BlockSpec((bk,D_MODEL), lambda c, k: (k, 0)),
        out_spec=pl.BlockSpec((None, D_MODEL, D_MODEL), lambda c, k: (c, 0, 0)),
        out_shape=SDS((N_CHIPS, D_MODEL, D_MODEL), F32))
    return [g_mi, g_mo]


def _mesh_place():
    x, y, c = lax.axis_index("x"), lax.axis_index("y"), lax.axis_index("c")
    other_chips = [(x, 1 - y), (1 - x, y), (1 - x, 1 - y)]
    return x, y, c, other_chips


ANY = pl.BlockSpec(memory_space=pl.ANY)


def _weight_half_copies(shard_refs, buf_refs, rows, send_sem, recv_sem):
    x, y, c, chips = _mesh_place()
    me = 2 * x + y
    copies = []
    for w, r_full in enumerate(rows):
        rh = r_full // 2
        for r, (px, py) in enumerate(chips):
            k = w * 3 + r
            copies.append(pltpu.make_async_remote_copy(
                src_ref=shard_refs[w].at[pl.ds(c * rh, rh), :], dst_ref=buf_refs[w].at[me, pl.ds(c * rh, rh), :],
                send_sem=send_sem.at[k], recv_sem=recv_sem.at[k], device_id=(px, py, c), device_id_type=MESH))
    return copies


def _pair_forward_copies(buf_refs, rows, send_sem, recv_sem):
    x, y, c, chips = _mesh_place()
    out = []
    for w, r_full in enumerate(rows):
        rh = r_full // 2
        for r, (px, py) in enumerate(chips):
            k = w * 3 + r
            landed = buf_refs[w].at[2 * px + py, pl.ds(c * rh, rh), :]
            theirs = buf_refs[w].at[2 * px + py, pl.ds((1 - c) * rh, rh), :]
            mk = lambda ref: pltpu.make_async_remote_copy(
                src_ref=ref, dst_ref=ref, send_sem=send_sem.at[k], recv_sem=recv_sem.at[k],
                device_id=(x, y, 1 - c), device_id_type=MESH)
            out.append((mk(landed), mk(theirs)))
    return out


def _place_own(block, n_slots, slot):
    buf = lax.empty((n_slots,) + block.shape, block.dtype)
    return lax.dynamic_update_slice(buf, block[None], (slot,) + (0,) * block.ndim)


def _allgather_weights(shards, bufs):
    n = len(shards)
    rows = [sh.shape[0] for sh in shards]

    def body(*refs):
        src, dst = refs[:n], refs[2 * n:3 * n]
        send_sem, recv_sem, fsend_sem, frecv_sem = refs[3 * n:]
        sends = _weight_half_copies(src, dst, rows, send_sem, recv_sem)
        for cpy in sends:
            cpy.start()
        fwds = _pair_forward_copies(dst, rows, fsend_sem, frecv_sem)
        for cpy, (fwd, _) in zip(sends, fwds):
            cpy.wait_recv()
            fwd.start()
        for _, landing in fwds:
            landing.wait_recv()
        for cpy in sends + [f for f, _ in fwds]:
            cpy.wait_send()

    return pl.pallas_call(
        body, name="allgather_w_in",
        in_specs=[ANY] * (2 * n), out_specs=[ANY] * n,
        out_shape=[SDS(b.shape, b.dtype) for b in bufs],
        scratch_shapes=[pltpu.SemaphoreType.DMA((3 * n,))] * 4,
        input_output_aliases={n + w: w for w in range(n)},
    )(*shards, *bufs)


def _pair_forward(bufs, rows):
    n = len(bufs)

    def body(*refs):
        dst = refs[n:2 * n]
        send_sem, recv_sem = refs[2 * n:]
        fwds = _pair_forward_copies(dst, rows, send_sem, recv_sem)
        for fwd, _ in fwds:
            fwd.start()
        for fwd, landing in fwds:
            landing.wait_recv()
            fwd.wait_send()

    return pl.pallas_call(
        body, name="weights_pair_forward",
        in_specs=[ANY] * n, out_specs=[ANY] * n,
        out_shape=[SDS(b.shape, b.dtype) for b in bufs],
        scratch_shapes=[pltpu.SemaphoreType.DMA((3 * n,))] * 2,
        input_output_aliases={w: w for w in range(n)},
    )(*bufs)


def _chip_sum_copies(src, dst, send_sem, recv_sem):
    x, y, c, chips = _mesh_place()
    copies = []
    for w in range(len(src)):
        for r, (px, py) in enumerate(chips):
            k = w * 3 + r
            copies.append(pltpu.make_async_remote_copy(
                src_ref=src[w].at[r + 1], dst_ref=dst[w].at[r + 1], send_sem=send_sem.at[k], recv_sem=recv_sem.at[k],
                device_id=(px, py, c), device_id_type=MESH))
    return copies


def _pair_exchange(grads):
    n = len(grads)

    def body(*refs):
        src, dst = refs[:n], refs[n:2 * n]
        send_sem, recv_sem = refs[2 * n:]
        x, y, c, _ = _mesh_place()
        copies = []
        for w in range(n):
            rh = grads[w].shape[1] // 2
            copies.append(pltpu.make_async_remote_copy(
                src_ref=src[w].at[:, pl.ds((1 - c) * rh, rh), :], dst_ref=dst[w],
                send_sem=send_sem.at[w], recv_sem=recv_sem.at[w],
                device_id=(x, y, 1 - c), device_id_type=MESH))
            copies[-1].start()
        for cpy in copies:
            cpy.wait()

    return pl.pallas_call(
        body, name="grad_pair_exchange",
        in_specs=[ANY] * n, out_specs=[ANY] * n,
        out_shape=[SDS((N_CHIPS, g.shape[1] // 2, g.shape[2]), g.dtype) for g in grads],
        scratch_shapes=[pltpu.SemaphoreType.DMA((n,)), pltpu.SemaphoreType.DMA((n,))],
    )(*grads)


def _pair_sum(place, grad, recv, name):
    _, R, C = grad.shape
    rh = R // 2
    br = _row_block(rh, max(256, ELEMENTWISE_BLOCK // C))
    nbh = rh // br

    def body(place_ref, g_ref, r_ref, own_ref, sums_ref):
        s = g_ref[...] + r_ref[...].astype(F32)

        @pl.when(pl.program_id(1) == 0)
        def _():
            own_ref[...] = s

        sums_ref[...] = s.astype(BF16)

    slot = lambda rel, pr: jnp.bitwise_xor(pr[0], rel)
    return pl.pallas_call(
        body, name=name,
        grid_spec=pltpu.PrefetchScalarGridSpec(
            num_scalar_prefetch=1, grid=(nbh, N_CHIPS),
            in_specs=[pl.BlockSpec((None, br, C), lambda i, rel, pr: (slot(rel, pr), pr[1] * nbh + i, 0)),
                      pl.BlockSpec((None, br, C), lambda i, rel, pr: (slot(rel, pr), i, 0))],
            out_specs=[pl.BlockSpec((br, C), lambda i, rel, pr: (i, 0)),
                       pl.BlockSpec((None, br, C), lambda i, rel, pr: (rel, i, 0))]),
        out_shape=[SDS((rh, C), F32), SDS((N_CHIPS, rh, C), BF16)],
        compiler_params=_cp("parallel", "arbitrary"),
    )(place, grad, recv)


def _chip_sum(place, own, recv, name):
    rh, C = own.shape
    br = _row_block(rh, max(256, ELEMENTWISE_BLOCK // C))
    nbh = rh // br

    def body(place_ref, own_ref, r_ref, o_ref):
        o_ref[...] = ((own_ref[...] + r_ref[1].astype(F32)) + r_ref[2].astype(F32)) + r_ref[3].astype(F32)

    return pl.pallas_call(
        body, name=name,
        grid_spec=pltpu.PrefetchScalarGridSpec(
            num_scalar_prefetch=1, grid=(nbh,),
            in_specs=[pl.BlockSpec((br, C), lambda i, pr: (i, 0)),
                      pl.BlockSpec((N_CHIPS, br, C), lambda i, pr: (0, i, 0))],
            out_specs=pl.BlockSpec((br, C), lambda i, pr: (pr[1] * nbh + i, 0))),
        out_shape=SDS((2 * rh, C), F32),
        compiler_params=_cp("parallel"),
    )(place, own, recv)


def _finish_exchange(grads, small_all):
    n = len(grads)

    def body(*refs):
        dst, all_ref = refs[n + 1:2 * n + 1], refs[2 * n + 1]
        send_sem, recv_sem, ssend_sem, srecv_sem = refs[2 * n + 2:]
        x, y, c, chips = _mesh_place()
        sib = (x, y, 1 - c)

        def pack(dev, k, to):
            slot = 4 * dev[0] + 2 * dev[1] + dev[2]
            return pltpu.make_async_remote_copy(
                src_ref=all_ref.at[slot], dst_ref=all_ref.at[slot], send_sem=ssend_sem.at[k],
                recv_sem=srecv_sem.at[k], device_id=to, device_id_type=MESH)

        pack_copies = [pack((x, y, c), 0, sib)] + [pack((x, y, c), 1 + r, (px, py, c))
                                                   for r, (px, py) in enumerate(chips)]
        for cpy in pack_copies:
            cpy.start()
        sends, landings = [], []
        for w in range(n):
            rh = grads[w].shape[0] // 2
            mk = lambda cc: pltpu.make_async_remote_copy(
                src_ref=dst[w].at[pl.ds(cc * rh, rh), :], dst_ref=dst[w].at[pl.ds(cc * rh, rh), :],
                send_sem=send_sem.at[w], recv_sem=recv_sem.at[w], device_id=(x, y, 1 - c), device_id_type=MESH)
            sends.append(mk(c))
            landings.append(mk(1 - c))
            sends[-1].start()
        for r, (px, py) in enumerate(chips):
            pack((px, py, c), 1 + r, (px, py, c)).wait_recv()
            pack_copies.append(pack((px, py, c), 4 + r, sib))
            pack_copies[-1].start()
        pack(sib, 0, sib).wait_recv()
        for r, (px, py) in enumerate(chips):
            pack((px, py, 1 - c), 4 + r, sib).wait_recv()
        for cpy in landings:
            cpy.wait_recv()
        for cpy in sends + pack_copies:
            cpy.wait_send()

    res = pl.pallas_call(
        body, name="grad_finish_exchange",
        in_specs=[ANY] * (n + 1), out_specs=[ANY] * (n + 1),
        out_shape=[SDS(g.shape, g.dtype) for g in grads] + [SDS(small_all.shape, small_all.dtype)],
        scratch_shapes=[pltpu.SemaphoreType.DMA((n,)), pltpu.SemaphoreType.DMA((n,)),
                        pltpu.SemaphoreType.DMA((N_DEV - 1,)), pltpu.SemaphoreType.DMA((N_DEV - 1,))],
        input_output_aliases={w: w for w in range(n + 1)},
    )(*grads, small_all)
    return res[:n], res[n]


def _adamw_math(w, g, m, v):
    m = ADAM_B1 * m + (1.0 - ADAM_B1) * g
    v = ADAM_B2 * v + (1.0 - ADAM_B2) * jnp.square(g)
    m_hat = m / (1.0 - ADAM_B1 ** ADAM_STEP)
    v_hat = v / (1.0 - ADAM_B2 ** ADAM_STEP)
    delta = -ADAM_LR * (m_hat / (jnp.sqrt(v_hat) + ADAM_EPS) + ADAM_WD * w)
    return delta, m, v


def _adamw(w, g, m, v, name):
    R, C = w.shape
    br = _row_block(R, 512)
    if g.ndim == 3:
        n_chunks, cw = g.shape[0], g.shape[2]
        g_spec = pl.BlockSpec((None, br, cw), lambda t, i: (t, i, 0))
    else:
        n_chunks, cw = 1, C
        g_spec = pl.BlockSpec((br, cw), lambda t, i: (i, t))

    def body(w_ref, g_ref, m_ref, v_ref, g_out_ref, d_ref, nm_ref, nv_ref):
        gv = g_ref[...]
        g_out_ref[...] = gv
        d_ref[...], nm_ref[...], nv_ref[...] = _adamw_math(w_ref[...], gv, m_ref[...], v_ref[...])

    spec = pl.BlockSpec((br, cw), lambda t, i: (i, t))
    return pl.pallas_call(
        body, grid=(n_chunks, R // br), name=name, in_specs=[spec, g_spec, spec, spec], out_specs=[spec] * 4,
        out_shape=[SDS((R, C), F32)] * 4, compiler_params=_cp("parallel", "parallel"),
    )(w, g, m, v)


def _small_sum_adamw(all_small, w, m, v):
    loss_row = PACK_ROWS - 8

    def body(all_ref, w_ref, m_ref, v_ref, g_ref, d_ref, nm_ref, nv_ref, loss_ref):
        g = all_ref[0]
        for k in range(1, N_DEV):
            g = g + all_ref[k]
        g_ref[...] = g
        d_ref[...], nm_ref[...], nv_ref[...] = _adamw_math(w_ref[...], g, m_ref[...], v_ref[...])
        total = jnp.sum(g[loss_row:loss_row + 1, :]) * (0.5 / D_MODEL)
        loss_ref[...] = jnp.full(loss_ref.shape, total, F32)

    full = lambda s: pl.BlockSpec(s, lambda i: (0,) * len(s))
    pack = (PACK_ROWS, D_MODEL)
    return pl.pallas_call(
        body, grid=(1,), name="small_sum_adamw",
        in_specs=[full((N_DEV,) + pack), full(pack), full(pack), full(pack)],
        out_specs=[full(pack)] * 4 + [full((8, 128))],
        out_shape=[SDS(pack, F32)] * 4 + [SDS((8, 128), F32)],
        compiler_params=_cp("arbitrary"),
    )(all_small, w, m, v)


def _pack_small(grp, scale, g_mix, g_mlp, g_f, loss_lanes):
    def part(vec):
        vec = vec.reshape(1, -1)
        return jnp.pad(vec, ((0, 7), (0, D_MODEL - vec.shape[1])))
    return jnp.concatenate([grp.reshape(-1, D_MODEL), part(scale), part(g_mix), part(g_mlp), part(g_f),
                            part(loss_lanes)], axis=0)


def _unpack_small(pack):
    n_grp = len(POOL_WINDOWS) * POOL_GROUP_W * POOL_GROUP_W // D_MODEL
    grp = pack[:n_grp].reshape(1, len(POOL_WINDOWS), POOL_GROUP_W, POOL_GROUP_W)
    scale = pack[n_grp, :POOL_W].reshape(1, POOL_W)
    g_mix = pack[n_grp + 8].reshape(1, D_MODEL)
    g_mlp = pack[n_grp + 16].reshape(1, D_MODEL)
    g_f = pack[n_grp + 24].reshape(D_MODEL)
    return grp, scale, g_mix, g_mlp, g_f


def _block_diag(grp):
    out = jnp.zeros((POOL_W, POOL_W), grp.dtype)
    for k in range(len(POOL_WINDOWS)):
        out = lax.dynamic_update_slice(out, grp[k], (k * POOL_GROUP_W, k * POOL_GROUP_W))
    return out


def kernel(x, norm_mix_g, w_in, w_att_out, w_pool_grp, pool_scale, w_pool_out, w_out, norm_mlp_g, w_mlp_in, w_mlp_out, norm_final_g, loss_target, m_norm_mix_g, m_w_in, m_w_att_out, m_w_pool_grp, m_pool_scale, m_w_pool_out, m_w_out, m_norm_mlp_g, m_w_mlp_in, m_w_mlp_out, m_norm_final_g, v_norm_mix_g, v_w_in, v_w_att_out, v_w_pool_grp, v_pool_scale, v_w_pool_out, v_w_out, v_norm_mlp_g, v_w_mlp_in, v_w_mlp_out, v_norm_final_g):
    S = x.shape[1]
    xs, target = x[0], loss_target[0]
    big = [w_in[0], w_att_out[0], w_pool_out[0], w_out[0], w_mlp_in[0], w_mlp_out[0]]
    big_m = [m_w_in[0], m_w_att_out[0], m_w_pool_out[0], m_w_out[0], m_w_mlp_in[0], m_w_mlp_out[0]]
    big_v = [v_w_in[0], v_w_att_out[0], v_w_pool_out[0], v_w_out[0], v_w_mlp_in[0], v_w_mlp_out[0]]

    chip = 2 * lax.axis_index("x") + lax.axis_index("y")
    core = lax.axis_index("c")
    place = jnp.stack([chip, core]).astype(jnp.int32)
    names = ("w_in", "w_att_out", "w_pool_out", "w_out", "w_mlp_in", "w_mlp_out")

    shards = [w.astype(BF16) for w in big]
    bufs = [_place_own(sh, N_CHIPS, chip) for sh in shards]
    (wg_in,) = _allgather_weights(shards[:1], bufs[:1])
    wbd = _block_diag(w_pool_grp[0]).astype(BF16)
    g_final = norm_final_g.reshape(1, D_MODEL)
    lane = lax.broadcasted_iota(jnp.int32, (GROUP_W, GROUP_W), 0) // HEAD_W
    head_ones = (lane == lane.T).astype(BF16)

    (u, qkv0, qkv1, qkv2, pz, gates), landed = _norm_inproj(xs, norm_mix_g, wg_in, shards[1:], bufs[1:])
    wg_ao, wg_po, wg_out, wg_mi, wg_mo = _pair_forward(landed, [sh.shape[0] for sh in shards[1:]])
    wg_out = wg_out.reshape(D_MODEL, D_MODEL)
    qkv = (qkv0, qkv1, qkv2)
    att = [_attn_fwd(qkv[grp], grp) for grp in range(3)]
    a, lt0, lt1, lt2, pooled, mixed, p, merged, h1, m = _mixer_out(
        [o for o, _ in att], [l for _, l in att], pz, gates, xs, wg_ao, wg_po, wbd, pool_scale, wg_out, norm_mlp_g)
    hid, dh2, dh2b, loss_lanes, dg_final = _mlp_fwd_loss(m, h1, target, wg_mi, wg_mo, g_final)

    def pair_reduce(grads, grad_names):
        recv = _pair_exchange([narrow for _, narrow in grads])
        pair = [_pair_sum(place, g, r, f"pair_sum_{nm}") for (g, _), r, nm in zip(grads, recv, grad_names)]
        return [own for own, _ in pair], [s for _, s in pair]

    def chip_reduce(owns, landed_sums, grad_names):
        return [_chip_sum(place, own, r, f"chip_sum_{nm}") for own, r, nm in zip(owns, landed_sums, grad_names)]

    dpre, dh1, dh1b, dg_mlp = _mlp_bwd(dh2, dh2b, hid, h1, wg_mi, wg_mo, norm_mlp_g)
    own_mlp, sums_mlp = pair_reduce(_wgrads_mlp(m, dpre, hid, dh2b), names[4:])
    (da1, dp1, dgates, da0, dag1, dag2, dd0, dd1, dd2, dmixed, dqp, dscale), landed_mlp = _mixer_bwd(
        dh1b, a, p, mixed, gates, wg_out, wg_ao, wg_po, wbd, pool_scale, head_ones, sums_mlp)
    g_mi, g_mo = chip_reduce(own_mlp, landed_mlp, names[4:])
    grads_mixer, g_bd = _wgrads_mixer(a, da1, p, dp1, merged, dh1b, pooled, dmixed)
    dqkv = [_attn_bwd(qkv[grp], da_g, lt_g, dd_g, grp)
            for grp, (da_g, lt_g, dd_g) in enumerate(((da0, lt0, dd0), (dag1, lt1, dd1), (dag2, lt2, dd2)))]
    dz = _dz_assemble(dqkv, dqp)
    own_in, sums_in = pair_reduce([_wgrad_in(u, dz, dgates)] + grads_mixer, names[:4])
    dx, dg_mix, landed_in = _inproj_dx(dz, dgates, dh1, xs, norm_mix_g, wg_in, sums_in)
    g_in, g_ao, g_po, g_out = chip_reduce(own_in, landed_in, names[:4])

    g_grp = jnp.stack([g_bd[k * POOL_GROUP_W:(k + 1) * POOL_GROUP_W, k * POOL_GROUP_W:(k + 1) * POOL_GROUP_W]
                       for k in range(len(POOL_WINDOWS))])
    small = _pack_small(g_grp, dscale, dg_mix, dg_mlp, dg_final, loss_lanes)
    full, small_all = _finish_exchange([g_in, g_ao, g_po, g_out, g_mi, g_mo],
                                       _place_own(small, N_DEV, 2 * chip + core))

    zero = jnp.zeros((D_MODEL,), F32)
    small_w = _pack_small(w_pool_grp[0], pool_scale, norm_mix_g, norm_mlp_g, norm_final_g, zero)
    small_m = _pack_small(m_w_pool_grp[0], m_pool_scale, m_norm_mix_g, m_norm_mlp_g, m_norm_final_g, zero)
    small_v = _pack_small(v_w_pool_grp[0], v_pool_scale, v_norm_mix_g, v_norm_mlp_g, v_norm_final_g, zero)
    sg, sd, sm, sv, loss_tile = _small_sum_adamw(small_all, small_w, small_m, small_v)
    full = [full[0].reshape(CHUNKS_PER_SHARD, D_MODEL, CHUNK)] + list(full[1:])
    upd = [_adamw(w, g, mm, vv, f"adamw_{nm}") for w, g, mm, vv, nm in zip(big, full, big_m, big_v, names)]

    def ordered(small_pack, bigs):
        grp, scale, g_mix, g_mlp, g_f = _unpack_small(small_pack)
        b_in, b_ao, b_po, b_out, b_mi, b_mo = [b[None] for b in bigs]
        return (g_mix, b_in, b_ao, grp, scale, b_po, b_out, g_mlp, b_mi, b_mo, g_f)

    return (loss_tile[0, 0], dx[None],
            *ordered(sg, [t[0] for t in upd]),
            *ordered(sd, [t[1] for t in upd]),
            *ordered(sm, [t[2] for t in upd]),
            *ordered(sv, [t[3] for t in upd]))
```

```python
import functools

import jax
import jax.numpy as jnp
from jax import lax
from jax.experimental import pallas as pl
from jax.experimental.pallas import tpu as pltpu

F32 = jnp.float32
BF16 = jnp.bfloat16
SDS = jax.ShapeDtypeStruct
MESH = pl.DeviceIdType.MESH

D_MODEL = 1024
D_FF = 4096
N_CHIPS = 4
N_DEV = 8
DILATIONS = (1, 4, 16)
BAND = 128
GROUP_W = 256
PAIR_W = 128
HEAD_W = 64
POOL_W = 768
POOL_GROUP_W = 192
POOL_WINDOWS = (2, 4, 8, 16)
POOL_HALO = 16
N_IN = 5120
CHUNK = 256
N_CHUNKS = N_IN // CHUNK
N_DZ_CHUNKS = 12
CHUNKS_PER_SHARD = 5
WGRAD_IN_GROUP = 4
NORM_EPS = 1e-6
ALIBI_MAX_BIAS = 8.0
N_HEADS = 12
NEG = -1e30

ADAM_LR, ADAM_B1, ADAM_B2, ADAM_EPS, ADAM_WD, ADAM_STEP = 0.001, 0.9, 0.999, 1e-08, 0.01, 10

TM = 512
TMB = 512
ATT_TILE = ((1, 4), (4, 1), (4, 1))
BK = 4096
ELEMENTWISE_BLOCK = 1 << 20
VMEM_LIMIT = 56 * 1024 * 1024
PACK_ROWS = 184

NT = (((1,), (1,)), ((), ()))
TN = (((0,), (0,)), ((), ()))


def _cp(*sem):
    return pltpu.CompilerParams(dimension_semantics=sem, vmem_limit_bytes=VMEM_LIMIT)


def _resident(shape):
    nd = len(shape)
    return pl.BlockSpec(shape, lambda *_: (0,) * nd, pipeline_mode=pl.Buffered(1))


def _row_block(rows, cap=256):
    return max(b for b in range(16, min(rows, cap) + 1, 16) if rows % b == 0)


def _dot(a, b):
    return jnp.dot(a, b, preferred_element_type=F32)


def _dot_nt(a, b):
    return lax.dot_general(a, b, NT, preferred_element_type=F32)


def _dot_tn(a, b):
    return lax.dot_general(a, b, TN, preferred_element_type=F32)


def _w_in_chunk(w_ref, n):
    return w_ref[n // CHUNKS_PER_SHARD, :, (n % CHUNKS_PER_SHARD) * CHUNK:(n % CHUNKS_PER_SHARD + 1) * CHUNK]


def _sigmoid(x):
    return 0.5 * jnp.tanh(0.5 * x.astype(F32)) + 0.5


def _rms_fwd(x, g):
    r = lax.rsqrt(jnp.mean(x * x, axis=-1, keepdims=True) + NORM_EPS)
    xh = x * r
    return xh * g, xh, r


def _rms_bwd(dy, xh, r, g):
    dxh = dy * g
    return r * (dxh - xh * jnp.mean(dxh * xh, axis=-1, keepdims=True))


def _deinterleave_store(val, s_ref, out_ref, lead, d, rows, dtype):
    if d == 1:
        out_ref[lead + (0,)] = val.astype(dtype)
        return
    for h in range(2):
        s_ref[h] = val[:, h * PAIR_W:(h + 1) * PAIR_W]
    for r in range(d):
        for h in range(2):
            out_ref[lead + (r, slice(None), slice(h * PAIR_W, (h + 1) * PAIR_W))] = (
                s_ref[h, pl.ds(r, rows // d, stride=d), :].astype(dtype))


def _interleave_load(in_ref, lead, s_ref, d, rows):
    for r in range(d):
        for h in range(2):
            s_ref[h, pl.ds(r, rows // d, stride=d), :] = (
                in_ref[lead + (r, slice(None), slice(h * PAIR_W, (h + 1) * PAIR_W))].astype(F32))


def _norm_inproj_own(x, g, w_own, buf):
    S = x.shape[0]
    n_tiles = S // TM

    def body(x_ref, g_ref, w_ref, shard_ref, buf_in, u_ref, z_ref, buf_ref, send_sem, recv_sem):
        i = pl.program_id(0)

        def copies():
            return _weight_half_copies([shard_ref], [buf_ref], [w_own.shape[0]], send_sem, recv_sem)

        @pl.when(i == 0)
        def _():
            for cpy in copies():
                cpy.start()

        u = _rms_fwd(x_ref[...], g_ref[...])[0].astype(BF16)
        u_ref[...] = u
        for t in range(CHUNKS_PER_SHARD):
            z_ref[t] = _dot(u, w_ref[:, t * CHUNK:(t + 1) * CHUNK])

        @pl.when(i == n_tiles - 1)
        def _():
            for cpy in copies():
                cpy.wait()

    row = lambda w: pl.BlockSpec((TM, w), lambda i: (i, 0))
    return pl.pallas_call(
        body, grid=(n_tiles,), name="norm_inproj_own",
        in_specs=[row(D_MODEL), _resident((1, D_MODEL)), _resident(w_own.shape), ANY, ANY],
        out_specs=[row(D_MODEL), pl.BlockSpec((CHUNKS_PER_SHARD, TM, CHUNK), lambda i: (0, i, 0)), ANY],
        out_shape=[SDS((S, D_MODEL), BF16), SDS((CHUNKS_PER_SHARD, S, CHUNK), F32), SDS(buf.shape, buf.dtype)],
        scratch_shapes=[pltpu.SemaphoreType.DMA((3,)), pltpu.SemaphoreType.DMA((3,))],
        input_output_aliases={4: 2},
        compiler_params=_cp("arbitrary"),
    )(x, g, w_own, w_own, buf)


def _inproj_rest(u, z_own, w_in, shards, bufs):
    S = u.shape[0]
    n_tiles = S // TM
    n = len(shards)

    def body(*refs):
        u_ref, zown_ref, w_ref = refs[0:3]
        shard_refs = refs[3:3 + n]
        q0_ref, q1_ref, q2_ref, pz_ref, gate_ref = refs[3 + 2 * n:8 + 2 * n]
        buf_refs = refs[8 + 2 * n:8 + 3 * n]
        s_ref, send_sem, recv_sem = refs[8 + 3 * n:]
        i = pl.program_id(0)
        chip = 2 * lax.axis_index("x") + lax.axis_index("y")

        def copies():
            return _weight_half_copies(shard_refs, buf_refs, [sh.shape[0] for sh in shards], send_sem, recv_sem)

        @pl.when(i == 0)
        def _():
            for cpy in copies():
                cpy.start()

        u = u_ref[...]
        qkv_refs = (q0_ref, q1_ref, q2_ref)

        def emit(k, zc):
            if k < 9:
                which, grp = k // 3, k % 3
                if which == 0:
                    zc = zc * 0.125
                _deinterleave_store(zc, s_ref, qkv_refs[grp], (which,), DILATIONS[grp], TM, BF16)
            elif k < N_DZ_CHUNKS:
                pz_ref[:, (k - 9) * CHUNK:(k - 8) * CHUNK] = zc
            else:
                gate_ref[:, (k - N_DZ_CHUNKS) * CHUNK:(k - N_DZ_CHUNKS + 1) * CHUNK] = zc.astype(BF16)

        for k in range(N_CHUNKS):
            own = chip == k // CHUNKS_PER_SHARD
            pl.when(own)(lambda k=k: emit(k, zown_ref[k % CHUNKS_PER_SHARD]))
            pl.when(jnp.logical_not(own))(lambda k=k: emit(k, _dot(u, _w_in_chunk(w_ref, k))))

        @pl.when(i == n_tiles - 1)
        def _():
            for cpy in copies():
                cpy.wait()

    row = lambda w: pl.BlockSpec((TM, w), lambda i: (i, 0))
    res = pl.pallas_call(
        body, grid=(n_tiles,), name="inproj_rest",
        in_specs=[row(D_MODEL), pl.BlockSpec((CHUNKS_PER_SHARD, TM, CHUNK), lambda i: (0, i, 0)),
                  _resident(w_in.shape)] + [ANY] * (2 * n),
        out_specs=[pl.BlockSpec((3, d, TM // d, GROUP_W), lambda i: (0, 0, i, 0)) for d in DILATIONS]
        + [row(POOL_W), row(2 * D_MODEL)] + [ANY] * n,
        out_shape=[SDS((3, d, S // d, GROUP_W), BF16) for d in DILATIONS]
        + [SDS((S, POOL_W), F32), SDS((S, 2 * D_MODEL), BF16)] + [SDS(b.shape, b.dtype) for b in bufs],
        scratch_shapes=[pltpu.VMEM((2, TM, PAIR_W), F32), pltpu.SemaphoreType.DMA((3 * n,)),
                        pltpu.SemaphoreType.DMA((3 * n,))],
        input_output_aliases={3 + n + w: 5 + w for w in range(n)},
        compiler_params=_cp("arbitrary"),
    )(u, z_own, w_in, *shards, *bufs)
    return res[:5], res[5:]


def _band_bias(grp, d):
    row = lax.broadcasted_iota(jnp.int32, (BAND, 2 * BAND), 0)
    col = lax.broadcasted_iota(jnp.int32, (BAND, 2 * BAND), 1)
    steps = BAND + row - col
    valid = (steps >= 0) & (steps <= BAND)
    stepsf = (steps * d).astype(F32)
    biases = []
    for hh in range(4):
        slope = 2.0 ** (-ALIBI_MAX_BIAS * (grp * 4 + hh + 1) / N_HEADS)
        biases.append(jnp.where(valid, -slope * stepsf, NEG))
    return biases, col


def _attn_tiles(grp, L):
    rr, rb = ATT_TILE[grp]
    rb = min(rb, L // BAND)
    return rr, rb, L // (rb * BAND)


def _kv_tile(cur_ref, prev_ref, rr, rb, cs):
    if rb == 0:
        return jnp.concatenate([prev_ref[rr, :, cs], cur_ref[rr, 0:BAND, cs]], axis=0)
    return cur_ref[rr, (rb - 1) * BAND:(rb + 1) * BAND, cs]


def _attn_fwd(qkv, grp):
    d = DILATIONS[grp]
    L = qkv.shape[2]
    RR, RB, nb = _attn_tiles(grp, L)

    def body(q_ref, kc_ref, kp_ref, vc_ref, vp_ref, o_ref, lse_ref):
        i = pl.program_id(0)
        biases, col = _band_bias(grp, d)
        first_keys_ok = (col >= BAND) | (i > 0)
        is_a = lax.broadcasted_iota(jnp.int32, (BAND, PAIR_W), 1) < HEAD_W
        heads = [(rr, rb, cp, h2) for rr in range(RR) for rb in range(RB) for cp in range(2) for h2 in range(2)]

        def tile(head):
            rr, rb, cp, _ = head
            return rr, rb, slice(rb * BAND, (rb + 1) * BAND), slice(cp * PAIR_W, (cp + 1) * PAIR_W)

        def scores(head):
            rr, rb, rows, cs = tile(head)
            q2 = q_ref[rr, rows, cs]
            b = biases[head[2] * 2 + head[3]]
            if rb == 0:
                b = jnp.where(first_keys_ok, b, NEG)
            sel = is_a if head[3] == 0 else jnp.logical_not(is_a)
            return _dot_nt(jnp.where(sel, q2, jnp.zeros_like(q2)), _kv_tile(kc_ref, kp_ref, rr, rb, cs)) + b

        s_next = scores(heads[0])
        res = {}
        for idx, head in enumerate(heads):
            s = s_next
            if idx + 1 < len(heads):
                s_next = scores(heads[idx + 1])
            rr, rb, rows, cs = tile(head)
            m = jnp.max(s, axis=-1, keepdims=True)
            p = jnp.exp(s - m)
            l = jnp.sum(p, axis=-1, keepdims=True)
            o = _dot(p.astype(BF16), _kv_tile(vc_ref, vp_ref, rr, rb, cs)) * (1.0 / l)
            res[head[3]] = (o, m + jnp.log(l))
            if head[3] == 1:
                o_ref[rr, rows, cs] = jnp.where(is_a, res[0][0], res[1][0]).astype(BF16)
                lse_ref[rr, rows, cs] = jnp.where(is_a, res[0][1], res[1][1])

    cur = lambda w: pl.BlockSpec((None, RR, RB * BAND, GROUP_W), lambda i, j: (w, j, i, 0))
    prev = lambda w: pl.BlockSpec((None, RR, BAND, GROUP_W), lambda i, j: (w, j, jnp.maximum(i * RB - 1, 0), 0))
    out = pl.BlockSpec((RR, RB * BAND, GROUP_W), lambda i, j: (j, i, 0))
    return pl.pallas_call(
        body, grid=(nb, d // RR), name=f"attn_fwd_g{grp}",
        in_specs=[cur(0), cur(1), prev(1), cur(2), prev(2)],
        out_specs=[out, out],
        out_shape=[SDS((d, L, GROUP_W), BF16), SDS((d, L, GROUP_W), F32)],
        compiler_params=_cp("parallel", "parallel"),
    )(qkv, qkv, qkv, qkv, qkv)


def _pool_column_select(col, vals):
    return jnp.where(col < POOL_GROUP_W, vals[0],
                     jnp.where(col < 2 * POOL_GROUP_W, vals[1],
                               jnp.where(col < 3 * POOL_GROUP_W, vals[2], vals[3])))


def _pool_inv_count(i, rows):
    t = i * rows + lax.broadcasted_iota(jnp.int32, (rows, POOL_W), 0)
    col = lax.broadcasted_iota(jnp.int32, (rows, POOL_W), 1)
    win = _pool_column_select(col, POOL_WINDOWS)
    return 1.0 / jnp.minimum(t + 1, win).astype(F32), col


def _mixer_out(outs, lses, pz, gates, x, w_ao, w_po, wbd, scale, w_out, g_mlp):
    S = x.shape[0]
    n_tiles = S // TMB

    def body(o0_ref, l0_ref, o1_ref, l1_ref, o2_ref, l2_ref, pz_ref, halo_ref, gate_ref, x_ref,
             wao_ref, wpo_ref, wbd_ref, sc_ref, wout_ref, g_ref,
             a_ref, lt0_ref, lt1_ref, lt2_ref, pooled_ref, mixed_ref, p_ref, merged_ref, h1_ref, m_ref,
             so1, sl1, so2, sl2, slt, ext_ref):
        i = pl.program_id(0)
        _interleave_load(o1_ref, (), so1, DILATIONS[1], TMB)
        _interleave_load(l1_ref, (), sl1, DILATIONS[1], TMB)
        _interleave_load(o2_ref, (), so2, DILATIONS[2], TMB)
        _interleave_load(l2_ref, (), sl2, DILATIONS[2], TMB)
        for h in range(2):
            hs = slice(h * PAIR_W, (h + 1) * PAIR_W)
            l0, l1, l2 = l0_ref[0, :, hs], sl1[h], sl2[h]
            mx = jnp.maximum(jnp.maximum(l0, l1), l2)
            e0, e1, e2 = jnp.exp(l0 - mx), jnp.exp(l1 - mx), jnp.exp(l2 - mx)
            den = e0 + e1 + e2
            a_ref[:, hs] = ((e0 * o0_ref[0, :, hs].astype(F32) + e1 * so1[h] + e2 * so2[h])
                            * (1.0 / den)).astype(BF16)
            slt[h] = mx + jnp.log(den)
        lt = jnp.concatenate([slt[0], slt[1]], axis=1)
        lt0_ref[0] = lt
        for ref, d in ((lt1_ref, DILATIONS[1]), (lt2_ref, DILATIONS[2])):
            for r in range(d):
                for h in range(2):
                    ref[r, :, h * PAIR_W:(h + 1) * PAIR_W] = slt[h, pl.ds(r, TMB // d, stride=d), :]

        pz_t = pz_ref[...]
        ext_ref[0:POOL_HALO, :] = jnp.where(i > 0, halo_ref[...], 0.0)
        ext_ref[POOL_HALO:, :] = pz_t
        sums = []
        acc = ext_ref[...]
        for k in (1, 2, 4, 8):
            acc = acc + pltpu.roll(acc, k, 0)
            sums.append(acc[POOL_HALO:, :])
        inv_cnt, col = _pool_inv_count(i, TMB)
        pooled = (_pool_column_select(col, sums) * inv_cnt - pz_t).astype(BF16)
        pooled_ref[...] = pooled
        mixed = _dot(pooled, wbd_ref[...])
        mixed_ref[...] = mixed.astype(BF16)
        p = (mixed * sc_ref[...]).astype(BF16)
        p_ref[...] = p

        a = a_ref[...]
        for j in range(N_CHIPS):
            js = slice(j * CHUNK, (j + 1) * CHUNK)
            ga = gate_ref[:, js]
            gp = gate_ref[:, D_MODEL + j * CHUNK:D_MODEL + (j + 1) * CHUNK]
            mj = _sigmoid(ga) * _dot(a, wao_ref[j]) + _sigmoid(gp) * _dot(p, wpo_ref[j])
            merged_ref[:, js] = mj.astype(BF16)
        h1 = x_ref[...] + _dot(merged_ref[...], wout_ref[...])
        h1_ref[...] = h1
        m_ref[...] = _rms_fwd(h1, g_ref[...])[0].astype(BF16)

    row = lambda w: pl.BlockSpec((TMB, w), lambda i: (i, 0))
    grp_spec = lambda d: pl.BlockSpec((d, TMB // d, GROUP_W), lambda i: (0, i, 0))
    halo = pl.BlockSpec((POOL_HALO, POOL_W), lambda i: (jnp.maximum(i * (TMB // POOL_HALO) - 1, 0), 0))
    d0, d1, d2 = DILATIONS
    return pl.pallas_call(
        body, grid=(n_tiles,), name="mixer_out",
        in_specs=[grp_spec(d0), grp_spec(d0), grp_spec(d1), grp_spec(d1), grp_spec(d2), grp_spec(d2),
                  row(POOL_W), halo, row(2 * D_MODEL), row(D_MODEL),
                  _resident(w_ao.shape), _resident(w_po.shape), _resident(wbd.shape), _resident(scale.shape),
                  _resident(w_out.shape), _resident(g_mlp.shape)],
        out_specs=[row(GROUP_W), grp_spec(d0), grp_spec(d1), grp_spec(d2),
                   row(POOL_W), row(POOL_W), row(POOL_W), row(D_MODEL), row(D_MODEL), row(D_MODEL)],
        out_shape=[SDS((S, GROUP_W), BF16)] + [SDS((d, S // d, GROUP_W), F32) for d in DILATIONS]
        + [SDS((S, POOL_W), BF16), SDS((S, POOL_W), BF16), SDS((S, POOL_W), BF16),
           SDS((S, D_MODEL), BF16), SDS((S, D_MODEL), F32), SDS((S, D_MODEL), BF16)],
        scratch_shapes=[pltpu.VMEM((2, TMB, PAIR_W), F32) for _ in range(5)]
        + [pltpu.VMEM((TMB + POOL_HALO, POOL_W), F32)],
        compiler_params=_cp("parallel"),
    )(outs[0], lses[0], outs[1], lses[1], outs[2], lses[2], pz, pz, gates, x,
      w_ao, w_po, wbd, scale, w_out, g_mlp)


def _mlp_fwd_loss(m, h1, target, w_mi, w_mo, g_f):
    S = m.shape[0]

    def body(m_ref, h1_ref, t_ref, wmi_ref, wmo_ref, g_ref, hid_ref, dh2_ref, dh2b_ref, loss_ref, dg_ref):
        @pl.when(pl.program_id(0) == 0)
        def _():
            loss_ref[...] = jnp.zeros_like(loss_ref)
            dg_ref[...] = jnp.zeros_like(dg_ref)

        mt = m_ref[...]
        acc = h1_ref[...]
        for c in range(N_CHIPS):
            hid = jnp.square(jnp.maximum(_dot(mt, wmi_ref[c]), 0.0)).astype(BF16)
            hid_ref[:, c * D_MODEL:(c + 1) * D_MODEL] = hid
            acc = acc + _dot(hid, wmo_ref[c])
        g = g_ref[...]
        y, hh, r = _rms_fwd(acc, g)
        e = y - t_ref[...]
        loss_ref[...] += jnp.sum(e * e, axis=0, keepdims=True)
        dy = e * (1.0 / D_MODEL)
        dg_ref[...] += jnp.sum(dy * hh, axis=0, keepdims=True)
        dh2 = _rms_bwd(dy, hh, r, g)
        dh2_ref[...] = dh2
        dh2b_ref[...] = dh2.astype(BF16)

    row = lambda w: pl.BlockSpec((TM, w), lambda i: (i, 0))
    vec = pl.BlockSpec((1, D_MODEL), lambda i: (0, 0))
    return pl.pallas_call(
        body, grid=(S // TM,), name="mlp_fwd_loss",
        in_specs=[row(D_MODEL), row(D_MODEL), row(D_MODEL), _resident(w_mi.shape), _resident(w_mo.shape),
                  _resident(g_f.shape)],
        out_specs=[row(D_FF), row(D_MODEL), row(D_MODEL), vec, vec],
        out_shape=[SDS((S, D_FF), BF16), SDS((S, D_MODEL), F32), SDS((S, D_MODEL), BF16),
                   SDS((1, D_MODEL), F32), SDS((1, D_MODEL), F32)],
        compiler_params=_cp("arbitrary"),
    )(m, h1, target, w_mi, w_mo, g_f)


def _mlp_bwd(dh2, dh2b, hid, h1, w_mi, w_mo, g_mlp):
    S = dh2.shape[0]

    def body(dh2_ref, dh2b_ref, hid_ref, h1_ref, wmi_ref, wmo_ref, g_ref, dpre_ref, dh1_ref, dh1b_ref, dg_ref):
        @pl.when(pl.program_id(0) == 0)
        def _():
            dg_ref[...] = jnp.zeros_like(dg_ref)

        d2 = dh2b_ref[...]
        dm = jnp.zeros((TM, D_MODEL), F32)
        dhid_next = _dot_nt(d2, wmo_ref[0])
        for c in range(N_CHIPS):
            cs = slice(c * D_MODEL, (c + 1) * D_MODEL)
            dhid = dhid_next
            if c + 1 < N_CHIPS:
                dhid_next = _dot_nt(d2, wmo_ref[c + 1])
            dpre = (dhid * (2.0 * jnp.sqrt(hid_ref[:, cs].astype(F32)))).astype(BF16)
            dpre_ref[:, cs] = dpre
            dm = dm + _dot_nt(dpre, wmi_ref[c])
        g = g_ref[...]
        _, hh, r = _rms_fwd(h1_ref[...], g)
        dg_ref[...] += jnp.sum(dm * hh, axis=0, keepdims=True)
        dh1 = dh2_ref[...] + _rms_bwd(dm, hh, r, g)
        dh1_ref[...] = dh1
        dh1b_ref[...] = dh1.astype(BF16)

    row = lambda w: pl.BlockSpec((TM, w), lambda i: (i, 0))
    return pl.pallas_call(
        body, grid=(S // TM,), name="mlp_bwd",
        in_specs=[row(D_MODEL), row(D_MODEL), row(D_FF), row(D_MODEL), _resident(w_mi.shape),
                  _resident(w_mo.shape), _resident(g_mlp.shape)],
        out_specs=[row(D_FF), row(D_MODEL), row(D_MODEL), pl.BlockSpec((1, D_MODEL), lambda i: (0, 0))],
        out_shape=[SDS((S, D_FF), BF16), SDS((S, D_MODEL), F32), SDS((S, D_MODEL), BF16), SDS((1, D_MODEL), F32)],
        compiler_params=_cp("arbitrary"),
    )(dh2, dh2b, hid, h1, w_mi, w_mo, g_mlp)


def _mixer_bwd(dh1b, a, p, mixed, gates, w_out, w_ao, w_po, wbd, scale, head_ones, sums):
    S = a.shape[0]
    n_tiles = S // TMB
    n = len(sums)

    def body(*refs):
        (dh1b_ref, a_ref, p_ref, mixed_ref, gate_ref, wout_ref, wao_ref, wpo_ref, wbd_ref, sc_ref,
         ones_ref) = refs[0:11]
        sum_refs = refs[11:11 + n]
        (da1_ref, dp1_ref, dgate_ref, da0_ref, dag1_ref, dag2_ref, dd0_ref, dd1_ref, dd2_ref,
         dmixed_ref, dqp_ref, dscale_ref) = refs[11 + n:23 + n]
        land_refs = refs[23 + n:23 + 2 * n]
        s_da, s_dd, send_sem, recv_sem = refs[23 + 2 * n:]
        i = pl.program_id(0)

        @pl.when(i == 0)
        def _():
            dscale_ref[...] = jnp.zeros_like(dscale_ref)
            for cpy in _chip_sum_copies(sum_refs, land_refs, send_sem, recv_sem):
                cpy.start()

        dmerged = _dot_nt(dh1b_ref[...], wout_ref[...])
        a = a_ref[...]
        p = p_ref[...]
        da = jnp.zeros((TMB, GROUP_W), F32)
        dp = jnp.zeros((TMB, POOL_W), F32)
        for j in range(N_CHIPS):
            js = slice(j * CHUNK, (j + 1) * CHUNK)
            sa = _sigmoid(gate_ref[:, js])
            sp = _sigmoid(gate_ref[:, D_MODEL + j * CHUNK:D_MODEL + (j + 1) * CHUNK])
            dmj = dmerged[:, js]
            da1 = (dmj * sa).astype(BF16)
            dp1 = (dmj * sp).astype(BF16)
            da1_ref[:, js] = da1
            dp1_ref[:, js] = dp1
            dgate_ref[j] = (dmj * _dot(a, wao_ref[j]) * sa * (1.0 - sa)).astype(BF16)
            dgate_ref[N_CHIPS + j] = (dmj * _dot(p, wpo_ref[j]) * sp * (1.0 - sp)).astype(BF16)
            da = da + _dot_nt(da1, wao_ref[j])
            dp = dp + _dot_nt(dp1, wpo_ref[j])

        prod = da * a.astype(F32)
        hi = prod.astype(BF16)
        lo = (prod - hi.astype(F32)).astype(BF16)
        dd = _dot(hi, ones_ref[...]) + _dot(lo, ones_ref[...])
        for ref, val, sref, dtype in ((da0_ref, da, s_da, BF16), (dd0_ref, dd, s_dd, F32)):
            ref[0] = val.astype(dtype)
            for h in range(2):
                sref[h] = val[:, h * PAIR_W:(h + 1) * PAIR_W]
        for refs, d in (((dag1_ref, dd1_ref), DILATIONS[1]), ((dag2_ref, dd2_ref), DILATIONS[2])):
            for r in range(d):
                for h in range(2):
                    hs = slice(h * PAIR_W, (h + 1) * PAIR_W)
                    refs[0][r, :, hs] = s_da[h, pl.ds(r, TMB // d, stride=d), :].astype(BF16)
                    refs[1][r, :, hs] = s_dd[h, pl.ds(r, TMB // d, stride=d), :]

        sc = sc_ref[...]
        dscale_ref[...] += jnp.sum(dp * mixed_ref[...].astype(F32), axis=0, keepdims=True)
        dmixed = (dp * sc).astype(BF16)
        dmixed_ref[...] = dmixed
        inv_cnt, _ = _pool_inv_count(i, TMB)
        dqp_ref[...] = (_dot_nt(dmixed, wbd_ref[...]) * inv_cnt).astype(BF16)

        @pl.when(i == n_tiles - 1)
        def _():
            for cpy in _chip_sum_copies(sum_refs, land_refs, send_sem, recv_sem):
                cpy.wait()

    row = lambda w: pl.BlockSpec((TMB, w), lambda i: (i, 0))
    grp_spec = lambda d: pl.BlockSpec((d, TMB // d, GROUP_W), lambda i: (0, i, 0))
    d0, d1, d2 = DILATIONS
    res = pl.pallas_call(
        body, grid=(n_tiles,), name="mixer_bwd",
        in_specs=[row(D_MODEL), row(GROUP_W), row(POOL_W), row(POOL_W), row(2 * D_MODEL),
                  _resident(w_out.shape), _resident(w_ao.shape), _resident(w_po.shape), _resident(wbd.shape),
                  _resident(scale.shape), _resident(head_ones.shape)] + [ANY] * n,
        out_specs=[row(D_MODEL), row(D_MODEL), pl.BlockSpec((2 * N_CHIPS, TMB, CHUNK), lambda i: (0, i, 0)),
                   grp_spec(d0), grp_spec(d1), grp_spec(d2), grp_spec(d0), grp_spec(d1), grp_spec(d2),
                   row(POOL_W), row(POOL_W), pl.BlockSpec((1, POOL_W), lambda i: (0, 0))] + [ANY] * n,
        out_shape=[SDS((S, D_MODEL), BF16), SDS((S, D_MODEL), BF16), SDS((2 * N_CHIPS, S, CHUNK), BF16)]
        + [SDS((d, S // d, GROUP_W), BF16) for d in DILATIONS]
        + [SDS((d, S // d, GROUP_W), F32) for d in DILATIONS]
        + [SDS((S, POOL_W), BF16), SDS((S, POOL_W), BF16), SDS((1, POOL_W), F32)]
        + [SDS(t.shape, t.dtype) for t in sums],
        scratch_shapes=[pltpu.VMEM((2, TMB, PAIR_W), F32), pltpu.VMEM((2, TMB, PAIR_W), F32),
                        pltpu.SemaphoreType.DMA((3 * n,)), pltpu.SemaphoreType.DMA((3 * n,))],
        compiler_params=_cp("arbitrary"),
    )(dh1b, a, p, mixed, gates, w_out, w_ao, w_po, wbd, scale, head_ones, *sums)
    return res[:12], res[12:]


def _attn_bwd(qkv, da, lt, dd, grp):
    d = DILATIONS[grp]
    L = qkv.shape[2]
    RR, RB, nb = _attn_tiles(grp, L)

    def body(q_ref, kc_ref, kp_ref, vc_ref, vp_ref, da_ref, lt_ref, dd_ref, dq_ref, dk_ref, dv_ref, dk_acc, dv_acc):
        i = pl.program_id(1)

        @pl.when(i == 0)
        def _():
            dk_acc[...] = jnp.zeros_like(dk_acc)
            dv_acc[...] = jnp.zeros_like(dv_acc)

        def compute(cur, prv):
            dk_acc[cur] = jnp.zeros((RR, RB * BAND, GROUP_W), F32)
            dv_acc[cur] = jnp.zeros((RR, RB * BAND, GROUP_W), F32)
            biases, col = _band_bias(grp, d)
            first_keys_ok = (col >= BAND) | (i > 0)
            is_a = lax.broadcasted_iota(jnp.int32, (BAND, PAIR_W), 1) < HEAD_W
            for rr in range(RR):
                for rb in range(RB):
                    rows = slice(rb * BAND, (rb + 1) * BAND)
                    for cp in range(2):
                        cs = slice(cp * PAIR_W, (cp + 1) * PAIR_W)
                        q2 = q_ref[rr, rows, cs]
                        da2 = da_ref[rr, rows, cs]
                        lt2 = lt_ref[rr, rows, cs]
                        dd2 = dd_ref[rr, rows, cs]
                        kcat = _kv_tile(kc_ref, kp_ref, rr, rb, cs)
                        vcat = _kv_tile(vc_ref, vp_ref, rr, rb, cs)
                        q2t = q2.astype(F32).T.astype(BF16)
                        da2t = da2.astype(F32).T.astype(BF16)
                        dqs, dkts, dvts, scores, dpvs = [], [], [], [], []
                        for h2 in range(2):
                            sel = is_a if h2 == 0 else jnp.logical_not(is_a)
                            b = biases[cp * 2 + h2]
                            if rb == 0:
                                b = jnp.where(first_keys_ok, b, NEG)
                            scores.append(_dot_nt(jnp.where(sel, q2, jnp.zeros_like(q2)), kcat) + b)
                            dpvs.append(_dot_nt(jnp.where(sel, da2, jnp.zeros_like(da2)), vcat))
                        for h2 in range(2):
                            lane0 = h2 * HEAD_W
                            p = jnp.exp(scores[h2] - lt2[:, lane0:lane0 + 1])
                            ds = (p * (dpvs[h2] - dd2[:, lane0:lane0 + 1])).astype(BF16)
                            dqs.append(_dot(ds, kcat))
                            dkts.append(_dot(q2t[lane0:lane0 + HEAD_W, :], ds))
                            dvts.append(_dot(da2t[lane0:lane0 + HEAD_W, :], p.astype(BF16)))
                        dq_ref[rr, rows, cs] = (jnp.where(is_a, dqs[0], dqs[1]) * 0.125).astype(BF16)
                        dkc = jnp.concatenate(dkts, axis=0).T
                        dvc = jnp.concatenate(dvts, axis=0).T
                        if rb == 0:
                            last = slice((RB - 1) * BAND, RB * BAND)
                            dk_acc[prv, rr, last, cs] += dkc[0:BAND]
                            dv_acc[prv, rr, last, cs] += dvc[0:BAND]
                            dk_acc[cur, rr, 0:BAND, cs] += dkc[BAND:]
                            dv_acc[cur, rr, 0:BAND, cs] += dvc[BAND:]
                        else:
                            both = slice((rb - 1) * BAND, (rb + 1) * BAND)
                            dk_acc[cur, rr, both, cs] += dkc
                            dv_acc[cur, rr, both, cs] += dvc

        def flush(prv):
            dk_ref[...] = dk_acc[prv].astype(BF16)
            dv_ref[...] = dv_acc[prv].astype(BF16)

        for parity in (0, 1):
            on = (i % 2) == parity
            pl.when(on & (i < nb))(functools.partial(compute, parity, 1 - parity))
            pl.when(on & (i > 0))(functools.partial(flush, 1 - parity))

    qi = lambda i: jnp.minimum(i, nb - 1)
    cur_w = lambda w: pl.BlockSpec((None, RR, RB * BAND, GROUP_W), lambda j, i: (w, j, qi(i), 0))
    prev_w = lambda w: pl.BlockSpec((None, RR, BAND, GROUP_W),
                                    lambda j, i: (w, j, jnp.maximum(qi(i) * RB - 1, 0), 0))
    blk = pl.BlockSpec((RR, RB * BAND, GROUP_W), lambda j, i: (j, qi(i), 0))
    late = pl.BlockSpec((RR, RB * BAND, GROUP_W), lambda j, i: (j, jnp.maximum(i - 1, 0), 0))
    return pl.pallas_call(
        body, grid=(d // RR, nb + 1), name=f"attn_bwd_g{grp}",
        in_specs=[cur_w(0), cur_w(1), prev_w(1), cur_w(2), prev_w(2), blk, blk, blk],
        out_specs=[blk, late, late],
        out_shape=[SDS((d, L, GROUP_W), BF16)] * 3,
        scratch_shapes=[pltpu.VMEM((2, RR, RB * BAND, GROUP_W), F32), pltpu.VMEM((2, RR, RB * BAND, GROUP_W), F32)],
        compiler_params=_cp("parallel", "arbitrary"),
    )(qkv, qkv, qkv, qkv, qkv, da, lt, dd)


def _dz_assemble(dqkv, dqp):
    S = dqp.shape[0]
    n_tiles = S // TMB

    def body(*refs):
        dqkv_refs = refs[0:9]
        dqp_ref, halo_ref = refs[9:11]
        dz_ref, s_ref, ext_ref = refs[11:]
        i = pl.program_id(0)

        for grp in range(3):
            for which in range(3):
                n = which * 3 + grp
                ref = dqkv_refs[grp * 3 + which]
                if DILATIONS[grp] == 1:
                    dz_ref[n] = ref[0]
                else:
                    _interleave_load(ref, (), s_ref, DILATIONS[grp], TMB)
                    for h in range(2):
                        dz_ref[n, :, h * PAIR_W:(h + 1) * PAIR_W] = s_ref[h].astype(BF16)

        dqp = dqp_ref[...].astype(F32)
        ext_ref[0:TMB, :] = dqp
        ext_ref[TMB:, :] = jnp.where(i < n_tiles - 1, halo_ref[...].astype(F32), 0.0)
        sums = []
        acc = ext_ref[...]
        for k in (1, 2, 4, 8):
            acc = acc + pltpu.roll(acc, TMB + POOL_HALO - k, 0)
            sums.append(acc[0:TMB, :])
        inv_cnt, col = _pool_inv_count(i, TMB)
        dpz = _pool_column_select(col, sums) - dqp / inv_cnt
        for t in range(3):
            dz_ref[9 + t] = dpz[:, t * CHUNK:(t + 1) * CHUNK].astype(BF16)

    row = lambda w: pl.BlockSpec((TMB, w), lambda i: (i, 0))
    grp_spec = lambda d: pl.BlockSpec((d, TMB // d, GROUP_W), lambda i: (0, i, 0))
    halo = pl.BlockSpec((POOL_HALO, POOL_W),
                        lambda i: (jnp.minimum((i + 1) * (TMB // POOL_HALO), S // POOL_HALO - 1), 0))
    flat = [t for grp in range(3) for t in dqkv[grp]]
    return pl.pallas_call(
        body, grid=(n_tiles,), name="dz_assemble",
        in_specs=[grp_spec(DILATIONS[grp]) for grp in range(3) for _ in range(3)] + [row(POOL_W), halo],
        out_specs=pl.BlockSpec((N_DZ_CHUNKS, TMB, CHUNK), lambda i: (0, i, 0)),
        out_shape=SDS((N_DZ_CHUNKS, S, CHUNK), BF16),
        scratch_shapes=[pltpu.VMEM((2, TMB, PAIR_W), F32), pltpu.VMEM((TMB + POOL_HALO, POOL_W), F32)],
        compiler_params=_cp("parallel"),
    )(*flat, dqp, dqp)


def _inproj_dx(dz, dgates, dh1, x, g, w_in, sums):
    S = x.shape[0]
    n_tiles = S // TM
    n = len(sums)

    def body(*refs):
        dz_ref, dgate_ref, dh1_ref, x_ref, g_ref, w_ref = refs[0:6]
        sum_refs = refs[6:6 + n]
        dx_ref, dg_ref = refs[6 + n:8 + n]
        land_refs = refs[8 + n:8 + 2 * n]
        sems = refs[8 + 2 * n:]
        i = pl.program_id(0)

        def copies():
            return _chip_sum_copies(sum_refs, land_refs, *sems)

        @pl.when(i == 0)
        def _():
            dg_ref[...] = jnp.zeros_like(dg_ref)
            for cpy in copies():
                cpy.start()

        du = jnp.zeros((TM, D_MODEL), F32)
        for k in range(N_CHUNKS):
            dzk = dz_ref[k] if k < N_DZ_CHUNKS else dgate_ref[k - N_DZ_CHUNKS]
            du = du + _dot_nt(dzk, _w_in_chunk(w_ref, k))
        gv = g_ref[...]
        _, xh, r = _rms_fwd(x_ref[...], gv)
        dg_ref[...] += jnp.sum(du * xh, axis=0, keepdims=True)
        dx_ref[...] = dh1_ref[...] + _rms_bwd(du, xh, r, gv)

        @pl.when(i == n_tiles - 1)
        def _():
            for cpy in copies():
                cpy.wait()

    row = lambda w: pl.BlockSpec((TM, w), lambda i: (i, 0))
    res = pl.pallas_call(
        body, grid=(n_tiles,), name="inproj_dx",
        in_specs=[pl.BlockSpec((N_DZ_CHUNKS, TM, CHUNK), lambda i: (0, i, 0)),
                  pl.BlockSpec((N_CHUNKS - N_DZ_CHUNKS, TM, CHUNK), lambda i: (0, i, 0)),
                  row(D_MODEL), row(D_MODEL), _resident(g.shape), _resident(w_in.shape)] + [ANY] * n,
        out_specs=[row(D_MODEL), pl.BlockSpec((1, D_MODEL), lambda i: (0, 0))] + [ANY] * n,
        out_shape=[SDS((S, D_MODEL), F32), SDS((1, D_MODEL), F32)] + [SDS(t.shape, t.dtype) for t in sums],
        scratch_shapes=[pltpu.SemaphoreType.DMA((3 * n,)), pltpu.SemaphoreType.DMA((3 * n,))],
        compiler_params=_cp("arbitrary"),
    )(dz, dgates, dh1, x, g, w_in, *sums)
    return res[0], res[1], res[2:]


def _wgrad(a, b, name, *, out_shape, a_spec, b_spec, out_spec, grid, n_out_cols=None, fill=None, narrow=True):
    k_axis = len(grid) - 1
    n_k = grid[k_axis]
    n_out = 2 if narrow else 1

    def body(a_ref, b_ref, *rest):
        o_ref = rest[-n_out]

        @pl.when(pl.program_id(k_axis) == 0)
        def _():
            o_ref[...] = jnp.zeros_like(o_ref)

        at = a_ref[...]
        if n_out_cols is None:
            o_ref[...] += _dot_tn(at, b_ref[...])
        elif n_out_cols[0] == "lead_both":
            for t in range(b_ref.shape[0]):
                o_ref[t] += _dot_tn(at, b_ref[t])
        else:
            w = n_out_cols[1]
            for t in range(o_ref.shape[0]):
                o_ref[t] += _dot_tn(at, b_ref[:, t * w:(t + 1) * w])

        if narrow:
            @pl.when(pl.program_id(k_axis) == n_k - 1)
            def _():
                rest[-1][...] = o_ref[...].astype(BF16)

    sem = ("parallel",) * k_axis + ("arbitrary",)
    extra = [] if fill is None else list(fill) if narrow else [fill]
    shapes = [out_shape, SDS(out_shape.shape, BF16)] if narrow else out_shape
    return pl.pallas_call(body, grid=grid, name=name, in_specs=[a_spec, b_spec] + [ANY] * len(extra),
                          out_specs=[out_spec] * n_out if narrow else out_spec, out_shape=shapes,
                          input_output_aliases={2 + t: t for t in range(len(extra))},
                          compiler_params=_cp(*sem))(a, b, *extra)


def _wgrad_in(u, dz, dgates):
    bk = min(BK, u.shape[0])
    nk = u.shape[0] // bk
    g = WGRAD_IN_GROUP
    kw = dict(n_out_cols=("lead_both", CHUNK), a_spec=pl.BlockSpec((bk, D_MODEL), lambda j, k: (k, 0)),
              b_spec=pl.BlockSpec((g, bk, CHUNK), lambda j, k: (j, k, 0)),
              out_shape=SDS((N_CHUNKS, D_MODEL, CHUNK), F32))
    first = _wgrad(u, dz, "wgrad_in_qkvp", grid=(N_DZ_CHUNKS // g, nk),
                   out_spec=pl.BlockSpec((g, D_MODEL, CHUNK), lambda j, k: (j, 0, 0)), **kw)
    both = _wgrad(u, dgates, "wgrad_in_gates", grid=((N_CHUNKS - N_DZ_CHUNKS) // g, nk), fill=first,
                  out_spec=pl.BlockSpec((g, D_MODEL, CHUNK), lambda j, k: (N_DZ_CHUNKS // g + j, 0, 0)), **kw)
    return [t.reshape(N_CHIPS, CHUNKS_PER_SHARD * D_MODEL, CHUNK) for t in both]


def _wgrads_mixer(a, da1, p, dp1, merged, dh1b, pooled, dmixed):
    bk = min(BK, a.shape[0])
    nk = a.shape[0] // bk
    g_ao = _wgrad(
        a, da1, "wgrad_att_out", grid=(nk,), n_out_cols=("cols_b", CHUNK),
        a_spec=pl.BlockSpec((bk,GROUP_W), lambda k: (k, 0)),
        b_spec=pl.BlockSpec((bk,D_MODEL), lambda k: (k, 0)),
        out_spec=pl.BlockSpec((N_CHIPS, GROUP_W, CHUNK), lambda k: (0, 0, 0)),
        out_shape=SDS((N_CHIPS, GROUP_W, CHUNK), F32))
    g_po = _wgrad(
        p, dp1, "wgrad_pool_out", grid=(nk,), n_out_cols=("cols_b", CHUNK),
        a_spec=pl.BlockSpec((bk,POOL_W), lambda k: (k, 0)),
        b_spec=pl.BlockSpec((bk,D_MODEL), lambda k: (k, 0)),
        out_spec=pl.BlockSpec((N_CHIPS, POOL_W, CHUNK), lambda k: (0, 0, 0)),
        out_shape=SDS((N_CHIPS, POOL_W, CHUNK), F32))
    g_out = _wgrad(
        merged, dh1b, "wgrad_out", grid=(nk,),
        a_spec=pl.BlockSpec((bk,D_MODEL), lambda k: (k, 0)),
        b_spec=pl.BlockSpec((bk,D_MODEL), lambda k: (k, 0)),
        out_spec=pl.BlockSpec((D_MODEL, D_MODEL), lambda k: (0, 0)),
        out_shape=SDS((D_MODEL, D_MODEL), F32))
    g_bd = _wgrad(
        pooled, dmixed, "wgrad_pool_grp", grid=(nk,),
        a_spec=pl.BlockSpec((bk,POOL_W), lambda k: (k, 0)),
        b_spec=pl.BlockSpec((bk,POOL_W), lambda k: (k, 0)),
        out_spec=pl.BlockSpec((POOL_W, POOL_W), lambda k: (0, 0)),
        out_shape=SDS((POOL_W, POOL_W), F32), narrow=False)
    g_out = [t.reshape(N_CHIPS, D_MODEL // N_CHIPS, D_MODEL) for t in g_out]
    return [g_ao, g_po, g_out], g_bd


def _wgrads_mlp(m, dpre, hid, dh2b):
    bk = min(BK, m.shape[0])
    nk = m.shape[0] // bk
    g_mi = _wgrad(
        m, dpre, "wgrad_mlp_in", grid=(N_CHIPS, nk),
        a_spec=pl.BlockSpec((bk,D_MODEL), lambda c, k: (k, 0)),
        b_spec=pl.BlockSpec((bk,D_MODEL), lambda c, k: (k, c)),
        out_spec=pl.BlockSpec((None, D_MODEL, D_MODEL), lambda c, k: (c, 0, 0)),
        out_shape=SDS((N_CHIPS, D_MODEL, D_MODEL), F32))
    g_mo = _wgrad(
        hid, dh2b, "wgrad_mlp_out", grid=(N_CHIPS, nk),
        a_spec=pl.BlockSpec((bk,D_MODEL), lambda c, k: (k, c)),
        b_spec=pl.BlockSpec((bk,D_MODEL), lambda c, k: (k, 0)),
        out_spec=pl.BlockSpec((None, D_MODEL, D_MODEL), lambda c, k: (c, 0, 0)),
        out_shape=SDS((N_CHIPS, D_MODEL, D_MODEL), F32))
    return [g_mi, g_mo]


def _mesh_place():
    x, y, c = lax.axis_index("x"), lax.axis_index("y"), lax.axis_index("c")
    other_chips = [(x, 1 - y), (1 - x, y), (1 - x, 1 - y)]
    return x, y, c, other_chips


ANY = pl.BlockSpec(memory_space=pl.ANY)


def _weight_half_copies(shard_refs, buf_refs, rows, send_sem, recv_sem):
    x, y, c, chips = _mesh_place()
    me = 2 * x + y
    copies = []
    for w, r_full in enumerate(rows):
        rh = r_full // 2
        for r, (px, py) in enumerate(chips):
            k = w * 3 + r
            copies.append(pltpu.make_async_remote_copy(
                src_ref=shard_refs[w].at[pl.ds(c * rh, rh), :], dst_ref=buf_refs[w].at[me, pl.ds(c * rh, rh), :],
                send_sem=send_sem.at[k], recv_sem=recv_sem.at[k], device_id=(px, py, c), device_id_type=MESH))
    return copies


def _pair_forward_copies(buf_refs, rows, send_sem, recv_sem):
    x, y, c, chips = _mesh_place()
    out = []
    for w, r_full in enumerate(rows):
        rh = r_full // 2
        for r, (px, py) in enumerate(chips):
            k = w * 3 + r
            landed = buf_refs[w].at[2 * px + py, pl.ds(c * rh, rh), :]
            theirs = buf_refs[w].at[2 * px + py, pl.ds((1 - c) * rh, rh), :]
            mk = lambda ref: pltpu.make_async_remote_copy(
                src_ref=ref, dst_ref=ref, send_sem=send_sem.at[k], recv_sem=recv_sem.at[k],
                device_id=(x, y, 1 - c), device_id_type=MESH)
            out.append((mk(landed), mk(theirs)))
    return out


def _place_own(block, n_slots, slot):
    buf = lax.empty((n_slots,) + block.shape, block.dtype)
    return lax.dynamic_update_slice(buf, block[None], (slot,) + (0,) * block.ndim)


def _pair_forward(bufs, rows, name):
    n = len(bufs)

    def body(*refs):
        dst = refs[n:2 * n]
        send_sem, recv_sem = refs[2 * n:]
        fwds = _pair_forward_copies(dst, rows, send_sem, recv_sem)
        for fwd, _ in fwds:
            fwd.start()
        for fwd, landing in fwds:
            landing.wait_recv()
            fwd.wait_send()

    return pl.pallas_call(
        body, name=name,
        in_specs=[ANY] * n, out_specs=[ANY] * n,
        out_shape=[SDS(b.shape, b.dtype) for b in bufs],
        scratch_shapes=[pltpu.SemaphoreType.DMA((3 * n,))] * 2,
        input_output_aliases={w: w for w in range(n)},
    )(*bufs)


def _chip_sum_copies(src, dst, send_sem, recv_sem):
    x, y, c, chips = _mesh_place()
    copies = []
    for w in range(len(src)):
        for r, (px, py) in enumerate(chips):
            k = w * 3 + r
            copies.append(pltpu.make_async_remote_copy(
                src_ref=src[w].at[r + 1], dst_ref=dst[w].at[r + 1], send_sem=send_sem.at[k], recv_sem=recv_sem.at[k],
                device_id=(px, py, c), device_id_type=MESH))
    return copies


def _pair_exchange(grads):
    n = len(grads)

    def body(*refs):
        src, dst = refs[:n], refs[n:2 * n]
        send_sem, recv_sem = refs[2 * n:]
        x, y, c, _ = _mesh_place()
        copies = []
        for w in range(n):
            rh = grads[w].shape[1] // 2
            copies.append(pltpu.make_async_remote_copy(
                src_ref=src[w].at[:, pl.ds((1 - c) * rh, rh), :], dst_ref=dst[w],
                send_sem=send_sem.at[w], recv_sem=recv_sem.at[w],
                device_id=(x, y, 1 - c), device_id_type=MESH))
            copies[-1].start()
        for cpy in copies:
            cpy.wait()

    return pl.pallas_call(
        body, name="grad_pair_exchange",
        in_specs=[ANY] * n, out_specs=[ANY] * n,
        out_shape=[SDS((N_CHIPS, g.shape[1] // 2, g.shape[2]), g.dtype) for g in grads],
        scratch_shapes=[pltpu.SemaphoreType.DMA((n,)), pltpu.SemaphoreType.DMA((n,))],
    )(*grads)


def _pair_sum(place, grad, recv, name):
    _, R, C = grad.shape
    rh = R // 2
    br = _row_block(rh, max(256, ELEMENTWISE_BLOCK // C))
    nbh = rh // br

    def body(place_ref, g_ref, r_ref, own_ref, sums_ref):
        s = g_ref[...] + r_ref[...].astype(F32)

        @pl.when(pl.program_id(1) == 0)
        def _():
            own_ref[...] = s

        sums_ref[...] = s.astype(BF16)

    slot = lambda rel, pr: jnp.bitwise_xor(pr[0], rel)
    return pl.pallas_call(
        body, name=name,
        grid_spec=pltpu.PrefetchScalarGridSpec(
            num_scalar_prefetch=1, grid=(nbh, N_CHIPS),
            in_specs=[pl.BlockSpec((None, br, C), lambda i, rel, pr: (slot(rel, pr), pr[1] * nbh + i, 0)),
                      pl.BlockSpec((None, br, C), lambda i, rel, pr: (slot(rel, pr), i, 0))],
            out_specs=[pl.BlockSpec((br, C), lambda i, rel, pr: (i, 0)),
                       pl.BlockSpec((None, br, C), lambda i, rel, pr: (rel, i, 0))]),
        out_shape=[SDS((rh, C), F32), SDS((N_CHIPS, rh, C), BF16)],
        compiler_params=_cp("parallel", "arbitrary"),
    )(place, grad, recv)


def _chip_sum(place, own, recv, name):
    rh, C = own.shape
    br = _row_block(rh, max(256, ELEMENTWISE_BLOCK // C))
    nbh = rh // br

    def body(place_ref, own_ref, r_ref, o_ref):
        o_ref[...] = ((own_ref[...] + r_ref[1].astype(F32)) + r_ref[2].astype(F32)) + r_ref[3].astype(F32)

    return pl.pallas_call(
        body, name=name,
        grid_spec=pltpu.PrefetchScalarGridSpec(
            num_scalar_prefetch=1, grid=(nbh,),
            in_specs=[pl.BlockSpec((br, C), lambda i, pr: (i, 0)),
                      pl.BlockSpec((N_CHIPS, br, C), lambda i, pr: (0, i, 0))],
            out_specs=pl.BlockSpec((br, C), lambda i, pr: (pr[1] * nbh + i, 0))),
        out_shape=SDS((2 * rh, C), F32),
        compiler_params=_cp("parallel"),
    )(place, own, recv)


def _finish_exchange(grads, small_all):
    n = len(grads)

    def body(*refs):
        dst, all_ref = refs[n + 1:2 * n + 1], refs[2 * n + 1]
        send_sem, recv_sem, ssend_sem, srecv_sem = refs[2 * n + 2:]
        x, y, c, chips = _mesh_place()
        sib = (x, y, 1 - c)

        def pack(dev, k, to):
            slot = 4 * dev[0] + 2 * dev[1] + dev[2]
            return pltpu.make_async_remote_copy(
                src_ref=all_ref.at[slot], dst_ref=all_ref.at[slot], send_sem=ssend_sem.at[k],
                recv_sem=srecv_sem.at[k], device_id=to, device_id_type=MESH)

        pack_copies = [pack((x, y, c), 0, sib)] + [pack((x, y, c), 1 + r, (px, py, c))
                                                   for r, (px, py) in enumerate(chips)]
        for cpy in pack_copies:
            cpy.start()
        sends, landings = [], []
        for w in range(n):
            rh = grads[w].shape[0] // 2
            mk = lambda cc: pltpu.make_async_remote_copy(
                src_ref=dst[w].at[pl.ds(cc * rh, rh), :], dst_ref=dst[w].at[pl.ds(cc * rh, rh), :],
                send_sem=send_sem.at[w], recv_sem=recv_sem.at[w], device_id=(x, y, 1 - c), device_id_type=MESH)
            sends.append(mk(c))
            landings.append(mk(1 - c))
            sends[-1].start()
        for r, (px, py) in enumerate(chips):
            pack((px, py, c), 1 + r, (px, py, c)).wait_recv()
            pack_copies.append(pack((px, py, c), 4 + r, sib))
            pack_copies[-1].start()
        pack(sib, 0, sib).wait_recv()
        for r, (px, py) in enumerate(chips):
            pack((px, py, 1 - c), 4 + r, sib).wait_recv()
        for cpy in landings:
            cpy.wait_recv()
        for cpy in sends + pack_copies:
            cpy.wait_send()

    res = pl.pallas_call(
        body, name="grad_finish_exchange",
        in_specs=[ANY] * (n + 1), out_specs=[ANY] * (n + 1),
        out_shape=[SDS(g.shape, g.dtype) for g in grads] + [SDS(small_all.shape, small_all.dtype)],
        scratch_shapes=[pltpu.SemaphoreType.DMA((n,)), pltpu.SemaphoreType.DMA((n,)),
                        pltpu.SemaphoreType.DMA((N_DEV - 1,)), pltpu.SemaphoreType.DMA((N_DEV - 1,))],
        input_output_aliases={w: w for w in range(n + 1)},
    )(*grads, small_all)
    return res[:n], res[n]


def _adamw_math(w, g, m, v):
    m = ADAM_B1 * m + (1.0 - ADAM_B1) * g
    v = ADAM_B2 * v + (1.0 - ADAM_B2) * jnp.square(g)
    m_hat = m / (1.0 - ADAM_B1 ** ADAM_STEP)
    v_hat = v / (1.0 - ADAM_B2 ** ADAM_STEP)
    delta = -ADAM_LR * (m_hat / (jnp.sqrt(v_hat) + ADAM_EPS) + ADAM_WD * w)
    return delta, m, v


def _adamw(w, g, m, v, name):
    R, C = w.shape
    br = _row_block(R, 512)
    if g.ndim == 3:
        n_chunks, cw = g.shape[0], g.shape[2]
        g_spec = pl.BlockSpec((None, br, cw), lambda t, i: (t, i, 0))
    else:
        n_chunks, cw = 1, C
        g_spec = pl.BlockSpec((br, cw), lambda t, i: (i, t))

    def body(w_ref, g_ref, m_ref, v_ref, g_out_ref, d_ref, nm_ref, nv_ref):
        gv = g_ref[...]
        g_out_ref[...] = gv
        d_ref[...], nm_ref[...], nv_ref[...] = _adamw_math(w_ref[...], gv, m_ref[...], v_ref[...])

    spec = pl.BlockSpec((br, cw), lambda t, i: (i, t))
    return pl.pallas_call(
        body, grid=(n_chunks, R // br), name=name, in_specs=[spec, g_spec, spec, spec], out_specs=[spec] * 4,
        out_shape=[SDS((R, C), F32)] * 4, compiler_params=_cp("parallel", "parallel"),
    )(w, g, m, v)


def _small_sum_adamw(all_small, w, m, v):
    loss_row = PACK_ROWS - 8

    def body(all_ref, w_ref, m_ref, v_ref, g_ref, d_ref, nm_ref, nv_ref, loss_ref):
        g = all_ref[0]
        for k in range(1, N_DEV):
            g = g + all_ref[k]
        g_ref[...] = g
        d_ref[...], nm_ref[...], nv_ref[...] = _adamw_math(w_ref[...], g, m_ref[...], v_ref[...])
        total = jnp.sum(g[loss_row:loss_row + 1, :]) * (0.5 / D_MODEL)
        loss_ref[...] = jnp.full(loss_ref.shape, total, F32)

    full = lambda s: pl.BlockSpec(s, lambda i: (0,) * len(s))
    pack = (PACK_ROWS, D_MODEL)
    return pl.pallas_call(
        body, grid=(1,), name="small_sum_adamw",
        in_specs=[full((N_DEV,) + pack), full(pack), full(pack), full(pack)],
        out_specs=[full(pack)] * 4 + [full((8, 128))],
        out_shape=[SDS(pack, F32)] * 4 + [SDS((8, 128), F32)],
        compiler_params=_cp("arbitrary"),
    )(all_small, w, m, v)


def _pack_small(grp, scale, g_mix, g_mlp, g_f, loss_lanes):
    def part(vec):
        vec = vec.reshape(1, -1)
        return jnp.pad(vec, ((0, 7), (0, D_MODEL - vec.shape[1])))
    return jnp.concatenate([grp.reshape(-1, D_MODEL), part(scale), part(g_mix), part(g_mlp), part(g_f),
                            part(loss_lanes)], axis=0)


def _unpack_small(pack):
    n_grp = len(POOL_WINDOWS) * POOL_GROUP_W * POOL_GROUP_W // D_MODEL
    grp = pack[:n_grp].reshape(1, len(POOL_WINDOWS), POOL_GROUP_W, POOL_GROUP_W)
    scale = pack[n_grp, :POOL_W].reshape(1, POOL_W)
    g_mix = pack[n_grp + 8].reshape(1, D_MODEL)
    g_mlp = pack[n_grp + 16].reshape(1, D_MODEL)
    g_f = pack[n_grp + 24].reshape(D_MODEL)
    return grp, scale, g_mix, g_mlp, g_f


def _block_diag(grp):
    out = jnp.zeros((POOL_W, POOL_W), grp.dtype)
    for k in range(len(POOL_WINDOWS)):
        out = lax.dynamic_update_slice(out, grp[k], (k * POOL_GROUP_W, k * POOL_GROUP_W))
    return out


def kernel(x, norm_mix_g, w_in, w_att_out, w_pool_grp, pool_scale, w_pool_out, w_out, norm_mlp_g, w_mlp_in, w_mlp_out, norm_final_g, loss_target, m_norm_mix_g, m_w_in, m_w_att_out, m_w_pool_grp, m_pool_scale, m_w_pool_out, m_w_out, m_norm_mlp_g, m_w_mlp_in, m_w_mlp_out, m_norm_final_g, v_norm_mix_g, v_w_in, v_w_att_out, v_w_pool_grp, v_pool_scale, v_w_pool_out, v_w_out, v_norm_mlp_g, v_w_mlp_in, v_w_mlp_out, v_norm_final_g):
    S = x.shape[1]
    xs, target = x[0], loss_target[0]
    big = [w_in[0], w_att_out[0], w_pool_out[0], w_out[0], w_mlp_in[0], w_mlp_out[0]]
    big_m = [m_w_in[0], m_w_att_out[0], m_w_pool_out[0], m_w_out[0], m_w_mlp_in[0], m_w_mlp_out[0]]
    big_v = [v_w_in[0], v_w_att_out[0], v_w_pool_out[0], v_w_out[0], v_w_mlp_in[0], v_w_mlp_out[0]]

    chip = 2 * lax.axis_index("x") + lax.axis_index("y")
    core = lax.axis_index("c")
    place = jnp.stack([chip, core]).astype(jnp.int32)
    names = ("w_in", "w_att_out", "w_pool_out", "w_out", "w_mlp_in", "w_mlp_out")

    shards = [w.astype(BF16) for w in big]
    bufs = [_place_own(sh, N_CHIPS, chip) for sh in shards]
    wbd = _block_diag(w_pool_grp[0]).astype(BF16)
    g_final = norm_final_g.reshape(1, D_MODEL)
    lane = lax.broadcasted_iota(jnp.int32, (GROUP_W, GROUP_W), 0) // HEAD_W
    head_ones = (lane == lane.T).astype(BF16)

    u, z_own, landed_in = _norm_inproj_own(xs, norm_mix_g, shards[0], bufs[0])
    (wg_in,) = _pair_forward([landed_in], [shards[0].shape[0]], "w_in_pair_forward")
    (qkv0, qkv1, qkv2, pz, gates), landed = _inproj_rest(u, z_own, wg_in, shards[1:], bufs[1:])
    wg_ao, wg_po, wg_out, wg_mi, wg_mo = _pair_forward(landed, [sh.shape[0] for sh in shards[1:]],
                                                       "weights_pair_forward")
    wg_out = wg_out.reshape(D_MODEL, D_MODEL)
    qkv = (qkv0, qkv1, qkv2)
    att = [_attn_fwd(qkv[grp], grp) for grp in range(3)]
    a, lt0, lt1, lt2, pooled, mixed, p, merged, h1, m = _mixer_out(
        [o for o, _ in att], [l for _, l in att], pz, gates, xs, wg_ao, wg_po, wbd, pool_scale, wg_out, norm_mlp_g)
    hid, dh2, dh2b, loss_lanes, dg_final = _mlp_fwd_loss(m, h1, target, wg_mi, wg_mo, g_final)

    def pair_reduce(grads, grad_names):
        recv = _pair_exchange([narrow for _, narrow in grads])
        pair = [_pair_sum(place, g, r, f"pair_sum_{nm}") for (g, _), r, nm in zip(grads, recv, grad_names)]
        return [own for own, _ in pair], [s for _, s in pair]

    def chip_reduce(owns, landed_sums, grad_names):
        return [_chip_sum(place, own, r, f"chip_sum_{nm}") for own, r, nm in zip(owns, landed_sums, grad_names)]

    dpre, dh1, dh1b, dg_mlp = _mlp_bwd(dh2, dh2b, hid, h1, wg_mi, wg_mo, norm_mlp_g)
    own_mlp, sums_mlp = pair_reduce(_wgrads_mlp(m, dpre, hid, dh2b), names[4:])
    (da1, dp1, dgates, da0, dag1, dag2, dd0, dd1, dd2, dmixed, dqp, dscale), landed_mlp = _mixer_bwd(
        dh1b, a, p, mixed, gates, wg_out, wg_ao, wg_po, wbd, pool_scale, head_ones, sums_mlp)
    g_mi, g_mo = chip_reduce(own_mlp, landed_mlp, names[4:])
    grads_mixer, g_bd = _wgrads_mixer(a, da1, p, dp1, merged, dh1b, pooled, dmixed)
    dqkv = [_attn_bwd(qkv[grp], da_g, lt_g, dd_g, grp)
            for grp, (da_g, lt_g, dd_g) in enumerate(((da0, lt0, dd0), (dag1, lt1, dd1), (dag2, lt2, dd2)))]
    dz = _dz_assemble(dqkv, dqp)
    own_in, sums_in = pair_reduce([_wgrad_in(u, dz, dgates)] + grads_mixer, names[:4])
    dx, dg_mix, landed_in = _inproj_dx(dz, dgates, dh1, xs, norm_mix_g, wg_in, sums_in)
    g_in, g_ao, g_po, g_out = chip_reduce(own_in, landed_in, names[:4])

    g_grp = jnp.stack([g_bd[k * POOL_GROUP_W:(k + 1) * POOL_GROUP_W, k * POOL_GROUP_W:(k + 1) * POOL_GROUP_W]
                       for k in range(len(POOL_WINDOWS))])
    small = _pack_small(g_grp, dscale, dg_mix, dg_mlp, dg_final, loss_lanes)
    full, small_all = _finish_exchange([g_in, g_ao, g_po, g_out, g_mi, g_mo],
                                       _place_own(small, N_DEV, 2 * chip + core))

    zero = jnp.zeros((D_MODEL,), F32)
    small_w = _pack_small(w_pool_grp[0], pool_scale, norm_mix_g, norm_mlp_g, norm_final_g, zero)
    small_m = _pack_small(m_w_pool_grp[0], m_pool_scale, m_norm_mix_g, m_norm_mlp_g, m_norm_final_g, zero)
    small_v = _pack_small(v_w_pool_grp[0], v_pool_scale, v_norm_mix_g, v_norm_mlp_g, v_norm_final_g, zero)
    sg, sd, sm, sv, loss_tile = _small_sum_adamw(small_all, small_w, small_m, small_v)
    full = [full[0].reshape(CHUNKS_PER_SHARD, D_MODEL, CHUNK)] + list(full[1:])
    upd = [_adamw(w, g, mm, vv, f"adamw_{nm}") for w, g, mm, vv, nm in zip(big, full, big_m, big_v, names)]

    def ordered(small_pack, bigs):
        grp, scale, g_mix, g_mlp, g_f = _unpack_small(small_pack)
        b_in, b_ao, b_po, b_out, b_mi, b_mo = [b[None] for b in bigs]
        return (g_mix, b_in, b_ao, grp, scale, b_po, b_out, g_mlp, b_mi, b_mo, g_f)

    return (loss_tile[0, 0], dx[None],
            *ordered(sg, [t[0] for t in upd]),
            *ordered(sd, [t[1] for t in upd]),
            *ordered(sm, [t[2] for t in upd]),
            *ordered(sv, [t[3] for t in upd]))
```

```python
import functools

import jax
import jax.numpy as jnp
from jax import lax
from jax.experimental import pallas as pl
from jax.experimental.pallas import tpu as pltpu

F32 = jnp.float32
BF16 = jnp.bfloat16
SDS = jax.ShapeDtypeStruct
MESH = pl.DeviceIdType.MESH

D_MODEL = 1024
D_FF = 4096
N_CHIPS = 4
N_DEV = 8
DILATIONS = (1, 4, 16)
BAND = 128
GROUP_W = 256
PAIR_W = 128
HEAD_W = 64
POOL_W = 768
POOL_GROUP_W = 192
POOL_WINDOWS = (2, 4, 8, 16)
POOL_HALO = 16
N_IN = 5120
CHUNK = 256
N_CHUNKS = N_IN // CHUNK
N_DZ_CHUNKS = 12
CHUNKS_PER_SHARD = 5
WGRAD_IN_GROUP = 4
NORM_EPS = 1e-6
ALIBI_MAX_BIAS = 8.0
N_HEADS = 12
NEG = -1e30

ADAM_LR, ADAM_B1, ADAM_B2, ADAM_EPS, ADAM_WD, ADAM_STEP = 0.001, 0.9, 0.999, 1e-08, 0.01, 10

TM = 512
TMB = 512
ATT_TILE = ((1, 4), (4, 1), (4, 1))
BK = 4096
ELEMENTWISE_BLOCK = 1 << 20
VMEM_LIMIT = 56 * 1024 * 1024
PACK_ROWS = 184

NT = (((1,), (1,)), ((), ()))
TN = (((0,), (0,)), ((), ()))


def _cp(*sem):
    return pltpu.CompilerParams(dimension_semantics=sem, vmem_limit_bytes=VMEM_LIMIT)


def _resident(shape):
    nd = len(shape)
    return pl.BlockSpec(shape, lambda *_: (0,) * nd, pipeline_mode=pl.Buffered(1))


def _row_block(rows, cap=256):
    return max(b for b in range(16, min(rows, cap) + 1, 16) if rows % b == 0)


def _dot(a, b):
    return jnp.dot(a, b, preferred_element_type=F32)


def _dot_nt(a, b):
    return lax.dot_general(a, b, NT, preferred_element_type=F32)


def _dot_tn(a, b):
    return lax.dot_general(a, b, TN, preferred_element_type=F32)


def _w_in_chunk(w_ref, n):
    return w_ref[n // CHUNKS_PER_SHARD, :, (n % CHUNKS_PER_SHARD) * CHUNK:(n % CHUNKS_PER_SHARD + 1) * CHUNK]


def _sigmoid(x):
    return 0.5 * jnp.tanh(0.5 * x.astype(F32)) + 0.5


def _rms_fwd(x, g):
    r = lax.rsqrt(jnp.mean(x * x, axis=-1, keepdims=True) + NORM_EPS)
    xh = x * r
    return xh * g, xh, r


def _rms_bwd(dy, xh, r, g):
    dxh = dy * g
    return r * (dxh - xh * jnp.mean(dxh * xh, axis=-1, keepdims=True))


def _deinterleave_store(val, s_ref, out_ref, lead, d, rows, dtype):
    if d == 1:
        out_ref[lead + (0,)] = val.astype(dtype)
        return
    for h in range(2):
        s_ref[h] = val[:, h * PAIR_W:(h + 1) * PAIR_W]
    for r in range(d):
        for h in range(2):
            out_ref[lead + (r, slice(None), slice(h * PAIR_W, (h + 1) * PAIR_W))] = (
                s_ref[h, pl.ds(r, rows // d, stride=d), :].astype(dtype))


def _interleave_load(in_ref, lead, s_ref, d, rows):
    for r in range(d):
        for h in range(2):
            s_ref[h, pl.ds(r, rows // d, stride=d), :] = (
                in_ref[lead + (r, slice(None), slice(h * PAIR_W, (h + 1) * PAIR_W))].astype(F32))


def _norm_inproj_own(x, g, w_own, buf):
    S = x.shape[0]
    n_tiles = S // TM

    def body(x_ref, g_ref, w_ref, shard_ref, buf_in, u_ref, z_ref, buf_ref, send_sem, recv_sem):
        i = pl.program_id(0)

        def copies():
            return _weight_half_copies([shard_ref], [buf_ref], [w_own.shape[0]], send_sem, recv_sem)

        @pl.when(i == 0)
        def _():
            for cpy in copies():
                cpy.start()

        u = _rms_fwd(x_ref[...], g_ref[...])[0].astype(BF16)
        u_ref[...] = u
        for t in range(CHUNKS_PER_SHARD):
            z_ref[t] = _dot(u, w_ref[:, t * CHUNK:(t + 1) * CHUNK])

        @pl.when(i == n_tiles - 1)
        def _():
            for cpy in copies():
                cpy.wait()

    row = lambda w: pl.BlockSpec((TM, w), lambda i: (i, 0))
    return pl.pallas_call(
        body, grid=(n_tiles,), name="norm_inproj_own",
        in_specs=[row(D_MODEL), _resident((1, D_MODEL)), _resident(w_own.shape), ANY, ANY],
        out_specs=[row(D_MODEL), pl.BlockSpec((CHUNKS_PER_SHARD, TM, CHUNK), lambda i: (0, i, 0)), ANY],
        out_shape=[SDS((S, D_MODEL), BF16), SDS((CHUNKS_PER_SHARD, S, CHUNK), F32), SDS(buf.shape, buf.dtype)],
        scratch_shapes=[pltpu.SemaphoreType.DMA((3,)), pltpu.SemaphoreType.DMA((3,))],
        input_output_aliases={4: 2},
        compiler_params=_cp("arbitrary"),
    )(x, g, w_own, w_own, buf)


def _inproj_rest(u, z_own, w_in, shards, bufs):
    S = u.shape[0]
    n_tiles = S // TM
    n = len(shards)

    def body(*refs):
        u_ref, zown_ref, w_ref = refs[0:3]
        shard_refs = refs[3:3 + n]
        q0_ref, q1_ref, q2_ref, pz_ref, gate_ref = refs[3 + 2 * n:8 + 2 * n]
        buf_refs = refs[8 + 2 * n:8 + 3 * n]
        s_ref, send_sem, recv_sem = refs[8 + 3 * n:]
        i = pl.program_id(0)
        chip = 2 * lax.axis_index("x") + lax.axis_index("y")

        def copies():
            return _weight_half_copies(shard_refs, buf_refs, [sh.shape[0] for sh in shards], send_sem, recv_sem)

        @pl.when(i == 0)
        def _():
            for cpy in copies():
                cpy.start()

        u = u_ref[...]
        qkv_refs = (q0_ref, q1_ref, q2_ref)

        def emit(k, zc):
            if k < 9:
                which, grp = k // 3, k % 3
                if which == 0:
                    zc = zc * 0.125
                _deinterleave_store(zc, s_ref, qkv_refs[grp], (which,), DILATIONS[grp], TM, BF16)
            elif k < N_DZ_CHUNKS:
                pz_ref[:, (k - 9) * CHUNK:(k - 8) * CHUNK] = zc
            else:
                gate_ref[:, (k - N_DZ_CHUNKS) * CHUNK:(k - N_DZ_CHUNKS + 1) * CHUNK] = zc.astype(BF16)

        def all_chunks(own_shard):
            for k in range(N_CHUNKS):
                if k // CHUNKS_PER_SHARD == own_shard:
                    emit(k, zown_ref[k % CHUNKS_PER_SHARD])
                else:
                    emit(k, _dot(u, _w_in_chunk(w_ref, k)))

        for shard in range(N_CHIPS):
            pl.when(chip == shard)(functools.partial(all_chunks, shard))

        @pl.when(i == n_tiles - 1)
        def _():
            for cpy in copies():
                cpy.wait()

    row = lambda w: pl.BlockSpec((TM, w), lambda i: (i, 0))
    res = pl.pallas_call(
        body, grid=(n_tiles,), name="inproj_rest",
        in_specs=[row(D_MODEL), pl.BlockSpec((CHUNKS_PER_SHARD, TM, CHUNK), lambda i: (0, i, 0)),
                  _resident(w_in.shape)] + [ANY] * (2 * n),
        out_specs=[pl.BlockSpec((3, d, TM // d, GROUP_W), lambda i: (0, 0, i, 0)) for d in DILATIONS]
        + [row(POOL_W), row(2 * D_MODEL)] + [ANY] * n,
        out_shape=[SDS((3, d, S // d, GROUP_W), BF16) for d in DILATIONS]
        + [SDS((S, POOL_W), F32), SDS((S, 2 * D_MODEL), BF16)] + [SDS(b.shape, b.dtype) for b in bufs],
        scratch_shapes=[pltpu.VMEM((2, TM, PAIR_W), F32), pltpu.SemaphoreType.DMA((3 * n,)),
                        pltpu.SemaphoreType.DMA((3 * n,))],
        input_output_aliases={3 + n + w: 5 + w for w in range(n)},
        compiler_params=_cp("arbitrary"),
    )(u, z_own, w_in, *shards, *bufs)
    return res[:5], res[5:]


def _band_bias(grp, d):
    row = lax.broadcasted_iota(jnp.int32, (BAND, 2 * BAND), 0)
    col = lax.broadcasted_iota(jnp.int32, (BAND, 2 * BAND), 1)
    steps = BAND + row - col
    valid = (steps >= 0) & (steps <= BAND)
    stepsf = (steps * d).astype(F32)
    biases = []
    for hh in range(4):
        slope = 2.0 ** (-ALIBI_MAX_BIAS * (grp * 4 + hh + 1) / N_HEADS)
        biases.append(jnp.where(valid, -slope * stepsf, NEG))
    return biases, col


def _attn_tiles(grp, L):
    rr, rb = ATT_TILE[grp]
    rb = min(rb, L // BAND)
    return rr, rb, L // (rb * BAND)


def _kv_tile(cur_ref, prev_ref, rr, rb, cs):
    if rb == 0:
        return jnp.concatenate([prev_ref[rr, :, cs], cur_ref[rr, 0:BAND, cs]], axis=0)
    return cur_ref[rr, (rb - 1) * BAND:(rb + 1) * BAND, cs]


def _attn_fwd(qkv, grp):
    d = DILATIONS[grp]
    L = qkv.shape[2]
    RR, RB, nb = _attn_tiles(grp, L)

    def body(q_ref, kc_ref, kp_ref, vc_ref, vp_ref, o_ref, lse_ref):
        i = pl.program_id(0)
        biases, col = _band_bias(grp, d)
        first_keys_ok = (col >= BAND) | (i > 0)
        is_a = lax.broadcasted_iota(jnp.int32, (BAND, PAIR_W), 1) < HEAD_W
        heads = [(rr, rb, cp, h2) for rr in range(RR) for rb in range(RB) for cp in range(2) for h2 in range(2)]

        def tile(head):
            rr, rb, cp, _ = head
            return rr, rb, slice(rb * BAND, (rb + 1) * BAND), slice(cp * PAIR_W, (cp + 1) * PAIR_W)

        def scores(head):
            rr, rb, rows, cs = tile(head)
            q2 = q_ref[rr, rows, cs]
            b = biases[head[2] * 2 + head[3]]
            if rb == 0:
                b = jnp.where(first_keys_ok, b, NEG)
            sel = is_a if head[3] == 0 else jnp.logical_not(is_a)
            return _dot_nt(jnp.where(sel, q2, jnp.zeros_like(q2)), _kv_tile(kc_ref, kp_ref, rr, rb, cs)) + b

        s_next = scores(heads[0])
        res = {}
        for idx, head in enumerate(heads):
            s = s_next
            if idx + 1 < len(heads):
                s_next = scores(heads[idx + 1])
            rr, rb, rows, cs = tile(head)
            m = jnp.max(s, axis=-1, keepdims=True)
            p = jnp.exp(s - m)
            l = jnp.sum(p, axis=-1, keepdims=True)
            o = _dot(p.astype(BF16), _kv_tile(vc_ref, vp_ref, rr, rb, cs)) * (1.0 / l)
            res[head[3]] = (o, m + jnp.log(l))
            if head[3] == 1:
                o_ref[rr, rows, cs] = jnp.where(is_a, res[0][0], res[1][0]).astype(BF16)
                lse_ref[rr, rows, cs] = jnp.where(is_a, res[0][1], res[1][1])

    cur = lambda w: pl.BlockSpec((None, RR, RB * BAND, GROUP_W), lambda i, j: (w, j, i, 0))
    prev = lambda w: pl.BlockSpec((None, RR, BAND, GROUP_W), lambda i, j: (w, j, jnp.maximum(i * RB - 1, 0), 0))
    out = pl.BlockSpec((RR, RB * BAND, GROUP_W), lambda i, j: (j, i, 0))
    return pl.pallas_call(
        body, grid=(nb, d // RR), name=f"attn_fwd_g{grp}",
        in_specs=[cur(0), cur(1), prev(1), cur(2), prev(2)],
        out_specs=[out, out],
        out_shape=[SDS((d, L, GROUP_W), BF16), SDS((d, L, GROUP_W), F32)],
        compiler_params=_cp("parallel", "parallel"),
    )(qkv, qkv, qkv, qkv, qkv)


def _pool_column_select(col, vals):
    return jnp.where(col < POOL_GROUP_W, vals[0],
                     jnp.where(col < 2 * POOL_GROUP_W, vals[1],
                               jnp.where(col < 3 * POOL_GROUP_W, vals[2], vals[3])))


def _pool_inv_count(i, rows):
    t = i * rows + lax.broadcasted_iota(jnp.int32, (rows, POOL_W), 0)
    col = lax.broadcasted_iota(jnp.int32, (rows, POOL_W), 1)
    win = _pool_column_select(col, POOL_WINDOWS)
    return 1.0 / jnp.minimum(t + 1, win).astype(F32), col


def _mixer_out(outs, lses, pz, gates, x, w_ao, w_po, wbd, scale, w_out, g_mlp):
    S = x.shape[0]
    n_tiles = S // TMB

    def body(o0_ref, l0_ref, o1_ref, l1_ref, o2_ref, l2_ref, pz_ref, halo_ref, gate_ref, x_ref,
             wao_ref, wpo_ref, wbd_ref, sc_ref, wout_ref, g_ref,
             a_ref, lt0_ref, lt1_ref, lt2_ref, pooled_ref, mixed_ref, p_ref, merged_ref, h1_ref, m_ref,
             so1, sl1, so2, sl2, slt, ext_ref):
        i = pl.program_id(0)
        _interleave_load(o1_ref, (), so1, DILATIONS[1], TMB)
        _interleave_load(l1_ref, (), sl1, DILATIONS[1], TMB)
        _interleave_load(o2_ref, (), so2, DILATIONS[2], TMB)
        _interleave_load(l2_ref, (), sl2, DILATIONS[2], TMB)
        for h in range(2):
            hs = slice(h * PAIR_W, (h + 1) * PAIR_W)
            l0, l1, l2 = l0_ref[0, :, hs], sl1[h], sl2[h]
            mx = jnp.maximum(jnp.maximum(l0, l1), l2)
            e0, e1, e2 = jnp.exp(l0 - mx), jnp.exp(l1 - mx), jnp.exp(l2 - mx)
            den = e0 + e1 + e2
            a_ref[:, hs] = ((e0 * o0_ref[0, :, hs].astype(F32) + e1 * so1[h] + e2 * so2[h])
                            * (1.0 / den)).astype(BF16)
            slt[h] = mx + jnp.log(den)
        lt = jnp.concatenate([slt[0], slt[1]], axis=1)
        lt0_ref[0] = lt
        for ref, d in ((lt1_ref, DILATIONS[1]), (lt2_ref, DILATIONS[2])):
            for r in range(d):
                for h in range(2):
                    ref[r, :, h * PAIR_W:(h + 1) * PAIR_W] = slt[h, pl.ds(r, TMB // d, stride=d), :]

        pz_t = pz_ref[...]
        ext_ref[0:POOL_HALO, :] = jnp.where(i > 0, halo_ref[...], 0.0)
        ext_ref[POOL_HALO:, :] = pz_t
        sums = []
        acc = ext_ref[...]
        for k in (1, 2, 4, 8):
            acc = acc + pltpu.roll(acc, k, 0)
            sums.append(acc[POOL_HALO:, :])
        inv_cnt, col = _pool_inv_count(i, TMB)
        pooled = (_pool_column_select(col, sums) * inv_cnt - pz_t).astype(BF16)
        pooled_ref[...] = pooled
        mixed = _dot(pooled, wbd_ref[...])
        mixed_ref[...] = mixed.astype(BF16)
        p = (mixed * sc_ref[...]).astype(BF16)
        p_ref[...] = p

        a = a_ref[...]
        for j in range(N_CHIPS):
            js = slice(j * CHUNK, (j + 1) * CHUNK)
            ga = gate_ref[:, js]
            gp = gate_ref[:, D_MODEL + j * CHUNK:D_MODEL + (j + 1) * CHUNK]
            mj = _sigmoid(ga) * _dot(a, wao_ref[j]) + _sigmoid(gp) * _dot(p, wpo_ref[j])
            merged_ref[:, js] = mj.astype(BF16)
        h1 = x_ref[...] + _dot(merged_ref[...], wout_ref[...])
        h1_ref[...] = h1
        m_ref[...] = _rms_fwd(h1, g_ref[...])[0].astype(BF16)

    row = lambda w: pl.BlockSpec((TMB, w), lambda i: (i, 0))
    grp_spec = lambda d: pl.BlockSpec((d, TMB // d, GROUP_W), lambda i: (0, i, 0))
    halo = pl.BlockSpec((POOL_HALO, POOL_W), lambda i: (jnp.maximum(i * (TMB // POOL_HALO) - 1, 0), 0))
    d0, d1, d2 = DILATIONS
    return pl.pallas_call(
        body, grid=(n_tiles,), name="mixer_out",
        in_specs=[grp_spec(d0), grp_spec(d0), grp_spec(d1), grp_spec(d1), grp_spec(d2), grp_spec(d2),
                  row(POOL_W), halo, row(2 * D_MODEL), row(D_MODEL),
                  _resident(w_ao.shape), _resident(w_po.shape), _resident(wbd.shape), _resident(scale.shape),
                  _resident(w_out.shape), _resident(g_mlp.shape)],
        out_specs=[row(GROUP_W), grp_spec(d0), grp_spec(d1), grp_spec(d2),
                   row(POOL_W), row(POOL_W), row(POOL_W), row(D_MODEL), row(D_MODEL), row(D_MODEL)],
        out_shape=[SDS((S, GROUP_W), BF16)] + [SDS((d, S // d, GROUP_W), F32) for d in DILATIONS]
        + [SDS((S, POOL_W), BF16), SDS((S, POOL_W), BF16), SDS((S, POOL_W), BF16),
           SDS((S, D_MODEL), BF16), SDS((S, D_MODEL), F32), SDS((S, D_MODEL), BF16)],
        scratch_shapes=[pltpu.VMEM((2, TMB, PAIR_W), F32) for _ in range(5)]
        + [pltpu.VMEM((TMB + POOL_HALO, POOL_W), F32)],
        compiler_params=_cp("parallel"),
    )(outs[0], lses[0], outs[1], lses[1], outs[2], lses[2], pz, pz, gates, x,
      w_ao, w_po, wbd, scale, w_out, g_mlp)


def _mlp_fwd_loss(m, h1, target, w_mi, w_mo, g_f):
    S = m.shape[0]

    def body(m_ref, h1_ref, t_ref, wmi_ref, wmo_ref, g_ref, hid_ref, dh2_ref, dh2b_ref, loss_ref, dg_ref):
        @pl.when(pl.program_id(0) == 0)
        def _():
            loss_ref[...] = jnp.zeros_like(loss_ref)
            dg_ref[...] = jnp.zeros_like(dg_ref)

        mt = m_ref[...]
        acc = h1_ref[...]
        for c in range(N_CHIPS):
            hid = jnp.square(jnp.maximum(_dot(mt, wmi_ref[c]), 0.0)).astype(BF16)
            hid_ref[:, c * D_MODEL:(c + 1) * D_MODEL] = hid
            acc = acc + _dot(hid, wmo_ref[c])
        g = g_ref[...]
        y, hh, r = _rms_fwd(acc, g)
        e = y - t_ref[...]
        loss_ref[...] += jnp.sum(e * e, axis=0, keepdims=True)
        dy = e * (1.0 / D_MODEL)
        dg_ref[...] += jnp.sum(dy * hh, axis=0, keepdims=True)
        dh2 = _rms_bwd(dy, hh, r, g)
        dh2_ref[...] = dh2
        dh2b_ref[...] = dh2.astype(BF16)

    row = lambda w: pl.BlockSpec((TM, w), lambda i: (i, 0))
    vec = pl.BlockSpec((1, D_MODEL), lambda i: (0, 0))
    return pl.pallas_call(
        body, grid=(S // TM,), name="mlp_fwd_loss",
        in_specs=[row(D_MODEL), row(D_MODEL), row(D_MODEL), _resident(w_mi.shape), _resident(w_mo.shape),
                  _resident(g_f.shape)],
        out_specs=[row(D_FF), row(D_MODEL), row(D_MODEL), vec, vec],
        out_shape=[SDS((S, D_FF), BF16), SDS((S, D_MODEL), F32), SDS((S, D_MODEL), BF16),
                   SDS((1, D_MODEL), F32), SDS((1, D_MODEL), F32)],
        compiler_params=_cp("arbitrary"),
    )(m, h1, target, w_mi, w_mo, g_f)


def _mlp_bwd(dh2, dh2b, hid, h1, w_mi, w_mo, g_mlp):
    S = dh2.shape[0]

    def body(dh2_ref, dh2b_ref, hid_ref, h1_ref, wmi_ref, wmo_ref, g_ref, dpre_ref, dh1_ref, dh1b_ref, dg_ref):
        @pl.when(pl.program_id(0) == 0)
        def _():
            dg_ref[...] = jnp.zeros_like(dg_ref)

        d2 = dh2b_ref[...]
        dm = jnp.zeros((TM, D_MODEL), F32)
        dhid_next = _dot_nt(d2, wmo_ref[0])
        for c in range(N_CHIPS):
            cs = slice(c * D_MODEL, (c + 1) * D_MODEL)
            dhid = dhid_next
            if c + 1 < N_CHIPS:
                dhid_next = _dot_nt(d2, wmo_ref[c + 1])
            dpre = (dhid * (2.0 * jnp.sqrt(hid_ref[:, cs].astype(F32)))).astype(BF16)
            dpre_ref[:, cs] = dpre
            dm = dm + _dot_nt(dpre, wmi_ref[c])
        g = g_ref[...]
        _, hh, r = _rms_fwd(h1_ref[...], g)
        dg_ref[...] += jnp.sum(dm * hh, axis=0, keepdims=True)
        dh1 = dh2_ref[...] + _rms_bwd(dm, hh, r, g)
        dh1_ref[...] = dh1
        dh1b_ref[...] = dh1.astype(BF16)

    row = lambda w: pl.BlockSpec((TM, w), lambda i: (i, 0))
    return pl.pallas_call(
        body, grid=(S // TM,), name="mlp_bwd",
        in_specs=[row(D_MODEL), row(D_MODEL), row(D_FF), row(D_MODEL), _resident(w_mi.shape),
                  _resident(w_mo.shape), _resident(g_mlp.shape)],
        out_specs=[row(D_FF), row(D_MODEL), row(D_MODEL), pl.BlockSpec((1, D_MODEL), lambda i: (0, 0))],
        out_shape=[SDS((S, D_FF), BF16), SDS((S, D_MODEL), F32), SDS((S, D_MODEL), BF16), SDS((1, D_MODEL), F32)],
        compiler_params=_cp("arbitrary"),
    )(dh2, dh2b, hid, h1, w_mi, w_mo, g_mlp)


def _mixer_bwd(dh1b, a, p, mixed, gates, w_out, w_ao, w_po, wbd, scale, head_ones, sums):
    S = a.shape[0]
    n_tiles = S // TMB
    n = len(sums)

    def body(*refs):
        (dh1b_ref, a_ref, p_ref, mixed_ref, gate_ref, wout_ref, wao_ref, wpo_ref, wbd_ref, sc_ref,
         ones_ref) = refs[0:11]
        sum_refs = refs[11:11 + n]
        (da1_ref, dp1_ref, dgate_ref, da0_ref, dag1_ref, dag2_ref, dd0_ref, dd1_ref, dd2_ref,
         dmixed_ref, dqp_ref, dscale_ref) = refs[11 + n:23 + n]
        land_refs = refs[23 + n:23 + 2 * n]
        s_da, s_dd, send_sem, recv_sem = refs[23 + 2 * n:]
        i = pl.program_id(0)

        @pl.when(i == 0)
        def _():
            dscale_ref[...] = jnp.zeros_like(dscale_ref)
            for cpy in _chip_sum_copies(sum_refs, land_refs, send_sem, recv_sem):
                cpy.start()

        dmerged = _dot_nt(dh1b_ref[...], wout_ref[...])
        a = a_ref[...]
        p = p_ref[...]
        da = jnp.zeros((TMB, GROUP_W), F32)
        dp = jnp.zeros((TMB, POOL_W), F32)
        for j in range(N_CHIPS):
            js = slice(j * CHUNK, (j + 1) * CHUNK)
            sa = _sigmoid(gate_ref[:, js])
            sp = _sigmoid(gate_ref[:, D_MODEL + j * CHUNK:D_MODEL + (j + 1) * CHUNK])
            dmj = dmerged[:, js]
            da1 = (dmj * sa).astype(BF16)
            dp1 = (dmj * sp).astype(BF16)
            da1_ref[:, js] = da1
            dp1_ref[:, js] = dp1
            dgate_ref[j] = (dmj * _dot(a, wao_ref[j]) * sa * (1.0 - sa)).astype(BF16)
            dgate_ref[N_CHIPS + j] = (dmj * _dot(p, wpo_ref[j]) * sp * (1.0 - sp)).astype(BF16)
            da = da + _dot_nt(da1, wao_ref[j])
            dp = dp + _dot_nt(dp1, wpo_ref[j])

        prod = da * a.astype(F32)
        hi = prod.astype(BF16)
        lo = (prod - hi.astype(F32)).astype(BF16)
        dd = _dot(hi, ones_ref[...]) + _dot(lo, ones_ref[...])
        for ref, val, sref, dtype in ((da0_ref, da, s_da, BF16), (dd0_ref, dd, s_dd, F32)):
            ref[0] = val.astype(dtype)
            for h in range(2):
                sref[h] = val[:, h * PAIR_W:(h + 1) * PAIR_W]
        for refs, d in (((dag1_ref, dd1_ref), DILATIONS[1]), ((dag2_ref, dd2_ref), DILATIONS[2])):
            for r in range(d):
                for h in range(2):
                    hs = slice(h * PAIR_W, (h + 1) * PAIR_W)
                    refs[0][r, :, hs] = s_da[h, pl.ds(r, TMB // d, stride=d), :].astype(BF16)
                    refs[1][r, :, hs] = s_dd[h, pl.ds(r, TMB // d, stride=d), :]

        sc = sc_ref[...]
        dscale_ref[...] += jnp.sum(dp * mixed_ref[...].astype(F32), axis=0, keepdims=True)
        dmixed = (dp * sc).astype(BF16)
        dmixed_ref[...] = dmixed
        inv_cnt, _ = _pool_inv_count(i, TMB)
        dqp_ref[...] = (_dot_nt(dmixed, wbd_ref[...]) * inv_cnt).astype(BF16)

        @pl.when(i == n_tiles - 1)
        def _():
            for cpy in _chip_sum_copies(sum_refs, land_refs, send_sem, recv_sem):
                cpy.wait()

    row = lambda w: pl.BlockSpec((TMB, w), lambda i: (i, 0))
    grp_spec = lambda d: pl.BlockSpec((d, TMB // d, GROUP_W), lambda i: (0, i, 0))
    d0, d1, d2 = DILATIONS
    res = pl.pallas_call(
        body, grid=(n_tiles,), name="mixer_bwd",
        in_specs=[row(D_MODEL), row(GROUP_W), row(POOL_W), row(POOL_W), row(2 * D_MODEL),
                  _resident(w_out.shape), _resident(w_ao.shape), _resident(w_po.shape), _resident(wbd.shape),
                  _resident(scale.shape), _resident(head_ones.shape)] + [ANY] * n,
        out_specs=[row(D_MODEL), row(D_MODEL), pl.BlockSpec((2 * N_CHIPS, TMB, CHUNK), lambda i: (0, i, 0)),
                   grp_spec(d0), grp_spec(d1), grp_spec(d2), grp_spec(d0), grp_spec(d1), grp_spec(d2),
                   row(POOL_W), row(POOL_W), pl.BlockSpec((1, POOL_W), lambda i: (0, 0))] + [ANY] * n,
        out_shape=[SDS((S, D_MODEL), BF16), SDS((S, D_MODEL), BF16), SDS((2 * N_CHIPS, S, CHUNK), BF16)]
        + [SDS((d, S // d, GROUP_W), BF16) for d in DILATIONS]
        + [SDS((d, S // d, GROUP_W), F32) for d in DILATIONS]
        + [SDS((S, POOL_W), BF16), SDS((S, POOL_W), BF16), SDS((1, POOL_W), F32)]
        + [SDS(t.shape, t.dtype) for t in sums],
        scratch_shapes=[pltpu.VMEM((2, TMB, PAIR_W), F32), pltpu.VMEM((2, TMB, PAIR_W), F32),
                        pltpu.SemaphoreType.DMA((3 * n,)), pltpu.SemaphoreType.DMA((3 * n,))],
        compiler_params=_cp("arbitrary"),
    )(dh1b, a, p, mixed, gates, w_out, w_ao, w_po, wbd, scale, head_ones, *sums)
    return res[:12], res[12:]


def _attn_bwd(qkv, da, lt, dd, grp):
    d = DILATIONS[grp]
    L = qkv.shape[2]
    RR, RB, nb = _attn_tiles(grp, L)

    def body(q_ref, kc_ref, kp_ref, vc_ref, vp_ref, da_ref, lt_ref, dd_ref, dq_ref, dk_ref, dv_ref, dk_acc, dv_acc):
        i = pl.program_id(1)

        @pl.when(i == 0)
        def _():
            dk_acc[...] = jnp.zeros_like(dk_acc)
            dv_acc[...] = jnp.zeros_like(dv_acc)

        def compute(cur, prv):
            dk_acc[cur] = jnp.zeros((RR, RB * BAND, GROUP_W), F32)
            dv_acc[cur] = jnp.zeros((RR, RB * BAND, GROUP_W), F32)
            biases, col = _band_bias(grp, d)
            first_keys_ok = (col >= BAND) | (i > 0)
            is_a = lax.broadcasted_iota(jnp.int32, (BAND, PAIR_W), 1) < HEAD_W
            for rr in range(RR):
                for rb in range(RB):
                    rows = slice(rb * BAND, (rb + 1) * BAND)
                    for cp in range(2):
                        cs = slice(cp * PAIR_W, (cp + 1) * PAIR_W)
                        q2 = q_ref[rr, rows, cs]
                        da2 = da_ref[rr, rows, cs]
                        lt2 = lt_ref[rr, rows, cs]
                        dd2 = dd_ref[rr, rows, cs]
                        kcat = _kv_tile(kc_ref, kp_ref, rr, rb, cs)
                        vcat = _kv_tile(vc_ref, vp_ref, rr, rb, cs)
                        q2t = q2.astype(F32).T.astype(BF16)
                        da2t = da2.astype(F32).T.astype(BF16)
                        dqs, dkts, dvts, scores, dpvs = [], [], [], [], []
                        for h2 in range(2):
                            sel = is_a if h2 == 0 else jnp.logical_not(is_a)
                            b = biases[cp * 2 + h2]
                            if rb == 0:
                                b = jnp.where(first_keys_ok, b, NEG)
                            scores.append(_dot_nt(jnp.where(sel, q2, jnp.zeros_like(q2)), kcat) + b)
                            dpvs.append(_dot_nt(jnp.where(sel, da2, jnp.zeros_like(da2)), vcat))
                        for h2 in range(2):
                            lane0 = h2 * HEAD_W
                            p = jnp.exp(scores[h2] - lt2[:, lane0:lane0 + 1])
                            ds = (p * (dpvs[h2] - dd2[:, lane0:lane0 + 1])).astype(BF16)
                            dqs.append(_dot(ds, kcat))
                            dkts.append(_dot(q2t[lane0:lane0 + HEAD_W, :], ds))
                            dvts.append(_dot(da2t[lane0:lane0 + HEAD_W, :], p.astype(BF16)))
                        dq_ref[rr, rows, cs] = (jnp.where(is_a, dqs[0], dqs[1]) * 0.125).astype(BF16)
                        dkc = jnp.concatenate(dkts, axis=0).T
                        dvc = jnp.concatenate(dvts, axis=0).T
                        if rb == 0:
                            last = slice((RB - 1) * BAND, RB * BAND)
                            dk_acc[prv, rr, last, cs] += dkc[0:BAND]
                            dv_acc[prv, rr, last, cs] += dvc[0:BAND]
                            dk_acc[cur, rr, 0:BAND, cs] += dkc[BAND:]
                            dv_acc[cur, rr, 0:BAND, cs] += dvc[BAND:]
                        else:
                            both = slice((rb - 1) * BAND, (rb + 1) * BAND)
                            dk_acc[cur, rr, both, cs] += dkc
                            dv_acc[cur, rr, both, cs] += dvc

        def flush(prv):
            dk_ref[...] = dk_acc[prv].astype(BF16)
            dv_ref[...] = dv_acc[prv].astype(BF16)

        for parity in (0, 1):
            on = (i % 2) == parity
            pl.when(on & (i < nb))(functools.partial(compute, parity, 1 - parity))
            pl.when(on & (i > 0))(functools.partial(flush, 1 - parity))

    qi = lambda i: jnp.minimum(i, nb - 1)
    cur_w = lambda w: pl.BlockSpec((None, RR, RB * BAND, GROUP_W), lambda j, i: (w, j, qi(i), 0))
    prev_w = lambda w: pl.BlockSpec((None, RR, BAND, GROUP_W),
                                    lambda j, i: (w, j, jnp.maximum(qi(i) * RB - 1, 0), 0))
    blk = pl.BlockSpec((RR, RB * BAND, GROUP_W), lambda j, i: (j, qi(i), 0))
    late = pl.BlockSpec((RR, RB * BAND, GROUP_W), lambda j, i: (j, jnp.maximum(i - 1, 0), 0))
    return pl.pallas_call(
        body, grid=(d // RR, nb + 1), name=f"attn_bwd_g{grp}",
        in_specs=[cur_w(0), cur_w(1), prev_w(1), cur_w(2), prev_w(2), blk, blk, blk],
        out_specs=[blk, late, late],
        out_shape=[SDS((d, L, GROUP_W), BF16)] * 3,
        scratch_shapes=[pltpu.VMEM((2, RR, RB * BAND, GROUP_W), F32), pltpu.VMEM((2, RR, RB * BAND, GROUP_W), F32)],
        compiler_params=_cp("parallel", "arbitrary"),
    )(qkv, qkv, qkv, qkv, qkv, da, lt, dd)


def _dz_assemble(dqkv, dqp):
    S = dqp.shape[0]
    n_tiles = S // TMB

    def body(*refs):
        dqkv_refs = refs[0:9]
        dqp_ref, halo_ref = refs[9:11]
        dz_ref, s_ref, ext_ref = refs[11:]
        i = pl.program_id(0)

        for grp in range(3):
            for which in range(3):
                n = which * 3 + grp
                ref = dqkv_refs[grp * 3 + which]
                if DILATIONS[grp] == 1:
                    dz_ref[n] = ref[0]
                else:
                    _interleave_load(ref, (), s_ref, DILATIONS[grp], TMB)
                    for h in range(2):
                        dz_ref[n, :, h * PAIR_W:(h + 1) * PAIR_W] = s_ref[h].astype(BF16)

        dqp = dqp_ref[...].astype(F32)
        ext_ref[0:TMB, :] = dqp
        ext_ref[TMB:, :] = jnp.where(i < n_tiles - 1, halo_ref[...].astype(F32), 0.0)
        sums = []
        acc = ext_ref[...]
        for k in (1, 2, 4, 8):
            acc = acc + pltpu.roll(acc, TMB + POOL_HALO - k, 0)
            sums.append(acc[0:TMB, :])
        inv_cnt, col = _pool_inv_count(i, TMB)
        dpz = _pool_column_select(col, sums) - dqp / inv_cnt
        for t in range(3):
            dz_ref[9 + t] = dpz[:, t * CHUNK:(t + 1) * CHUNK].astype(BF16)

    row = lambda w: pl.BlockSpec((TMB, w), lambda i: (i, 0))
    grp_spec = lambda d: pl.BlockSpec((d, TMB // d, GROUP_W), lambda i: (0, i, 0))
    halo = pl.BlockSpec((POOL_HALO, POOL_W),
                        lambda i: (jnp.minimum((i + 1) * (TMB // POOL_HALO), S // POOL_HALO - 1), 0))
    flat = [t for grp in range(3) for t in dqkv[grp]]
    return pl.pallas_call(
        body, grid=(n_tiles,), name="dz_assemble",
        in_specs=[grp_spec(DILATIONS[grp]) for grp in range(3) for _ in range(3)] + [row(POOL_W), halo],
        out_specs=pl.BlockSpec((N_DZ_CHUNKS, TMB, CHUNK), lambda i: (0, i, 0)),
        out_shape=SDS((N_DZ_CHUNKS, S, CHUNK), BF16),
        scratch_shapes=[pltpu.VMEM((2, TMB, PAIR_W), F32), pltpu.VMEM((TMB + POOL_HALO, POOL_W), F32)],
        compiler_params=_cp("parallel"),
    )(*flat, dqp, dqp)


def _inproj_dx(dz, dgates, dh1, x, g, w_in, sums):
    S = x.shape[0]
    n_tiles = S // TM
    n = len(sums)

    def body(*refs):
        dz_ref, dgate_ref, dh1_ref, x_ref, g_ref, w_ref = refs[0:6]
        sum_refs = refs[6:6 + n]
        dx_ref, dg_ref = refs[6 + n:8 + n]
        land_refs = refs[8 + n:8 + 2 * n]
        sems = refs[8 + 2 * n:]
        i = pl.program_id(0)

        def copies():
            return _chip_sum_copies(sum_refs, land_refs, *sems)

        @pl.when(i == 0)
        def _():
            dg_ref[...] = jnp.zeros_like(dg_ref)
            for cpy in copies():
                cpy.start()

        du = jnp.zeros((TM, D_MODEL), F32)
        for k in range(N_CHUNKS):
            dzk = dz_ref[k] if k < N_DZ_CHUNKS else dgate_ref[k - N_DZ_CHUNKS]
            du = du + _dot_nt(dzk, _w_in_chunk(w_ref, k))
        gv = g_ref[...]
        _, xh, r = _rms_fwd(x_ref[...], gv)
        dg_ref[...] += jnp.sum(du * xh, axis=0, keepdims=True)
        dx_ref[...] = dh1_ref[...] + _rms_bwd(du, xh, r, gv)

        @pl.when(i == n_tiles - 1)
        def _():
            for cpy in copies():
                cpy.wait()

    row = lambda w: pl.BlockSpec((TM, w), lambda i: (i, 0))
    res = pl.pallas_call(
        body, grid=(n_tiles,), name="inproj_dx",
        in_specs=[pl.BlockSpec((N_DZ_CHUNKS, TM, CHUNK), lambda i: (0, i, 0)),
                  pl.BlockSpec((N_CHUNKS - N_DZ_CHUNKS, TM, CHUNK), lambda i: (0, i, 0)),
                  row(D_MODEL), row(D_MODEL), _resident(g.shape), _resident(w_in.shape)] + [ANY] * n,
        out_specs=[row(D_MODEL), pl.BlockSpec((1, D_MODEL), lambda i: (0, 0))] + [ANY] * n,
        out_shape=[SDS((S, D_MODEL), F32), SDS((1, D_MODEL), F32)] + [SDS(t.shape, t.dtype) for t in sums],
        scratch_shapes=[pltpu.SemaphoreType.DMA((3 * n,)), pltpu.SemaphoreType.DMA((3 * n,))],
        compiler_params=_cp("arbitrary"),
    )(dz, dgates, dh1, x, g, w_in, *sums)
    return res[0], res[1], res[2:]


def _wgrad(a, b, name, *, out_shape, a_spec, b_spec, out_spec, grid, n_out_cols=None, fill=None, narrow=True):
    k_axis = len(grid) - 1
    n_k = grid[k_axis]
    n_out = 2 if narrow else 1

    def body(a_ref, b_ref, *rest):
        o_ref = rest[-n_out]

        @pl.when(pl.program_id(k_axis) == 0)
        def _():
            o_ref[...] = jnp.zeros_like(o_ref)

        at = a_ref[...]
        if n_out_cols is None:
            o_ref[...] += _dot_tn(at, b_ref[...])
        elif n_out_cols[0] == "lead_both":
            for t in range(b_ref.shape[0]):
                o_ref[t] += _dot_tn(at, b_ref[t])
        else:
            w = n_out_cols[1]
            for t in range(o_ref.shape[0]):
                o_ref[t] += _dot_tn(at, b_ref[:, t * w:(t + 1) * w])

        if narrow:
            @pl.when(pl.program_id(k_axis) == n_k - 1)
            def _():
                rest[-1][...] = o_ref[...].astype(BF16)

    sem = ("parallel",) * k_axis + ("arbitrary",)
    extra = [] if fill is None else list(fill) if narrow else [fill]
    shapes = [out_shape, SDS(out_shape.shape, BF16)] if narrow else out_shape
    return pl.pallas_call(body, grid=grid, name=name, in_specs=[a_spec, b_spec] + [ANY] * len(extra),
                          out_specs=[out_spec] * n_out if narrow else out_spec, out_shape=shapes,
                          input_output_aliases={2 + t: t for t in range(len(extra))},
                          compiler_params=_cp(*sem))(a, b, *extra)


def _wgrad_in(u, dz, dgates):
    bk = min(BK, u.shape[0])
    nk = u.shape[0] // bk
    g = WGRAD_IN_GROUP
    kw = dict(n_out_cols=("lead_both", CHUNK), a_spec=pl.BlockSpec((bk, D_MODEL), lambda j, k: (k, 0)),
              b_spec=pl.BlockSpec((g, bk, CHUNK), lambda j, k: (j, k, 0)),
              out_shape=SDS((N_CHUNKS, D_MODEL, CHUNK), F32))
    first = _wgrad(u, dz, "wgrad_in_qkvp", grid=(N_DZ_CHUNKS // g, nk),
                   out_spec=pl.BlockSpec((g, D_MODEL, CHUNK), lambda j, k: (j, 0, 0)), **kw)
    both = _wgrad(u, dgates, "wgrad_in_gates", grid=((N_CHUNKS - N_DZ_CHUNKS) // g, nk), fill=first,
                  out_spec=pl.BlockSpec((g, D_MODEL, CHUNK), lambda j, k: (N_DZ_CHUNKS // g + j, 0, 0)), **kw)
    return [t.reshape(N_CHIPS, CHUNKS_PER_SHARD * D_MODEL, CHUNK) for t in both]


def _wgrads_mixer(a, da1, p, dp1, merged, dh1b, pooled, dmixed):
    bk = min(BK, a.shape[0])
    nk = a.shape[0] // bk
    g_ao = _wgrad(
        a, da1, "wgrad_att_out", grid=(nk,), n_out_cols=("cols_b", CHUNK),
        a_spec=pl.BlockSpec((bk,GROUP_W), lambda k: (k, 0)),
        b_spec=pl.BlockSpec((bk,D_MODEL), lambda k: (k, 0)),
        out_spec=pl.BlockSpec((N_CHIPS, GROUP_W, CHUNK), lambda k: (0, 0, 0)),
        out_shape=SDS((N_CHIPS, GROUP_W, CHUNK), F32))
    g_po = _wgrad(
        p, dp1, "wgrad_pool_out", grid=(nk,), n_out_cols=("cols_b", CHUNK),
        a_spec=pl.BlockSpec((bk,POOL_W), lambda k: (k, 0)),
        b_spec=pl.BlockSpec((bk,D_MODEL), lambda k: (k, 0)),
        out_spec=pl.BlockSpec((N_CHIPS, POOL_W, CHUNK), lambda k: (0, 0, 0)),
        out_shape=SDS((N_CHIPS, POOL_W, CHUNK), F32))
    g_out = _wgrad(
        merged, dh1b, "wgrad_out", grid=(nk,),
        a_spec=pl.BlockSpec((bk,D_MODEL), lambda k: (k, 0)),
        b_spec=pl.BlockSpec((bk,D_MODEL), lambda k: (k, 0)),
        out_spec=pl.BlockSpec((D_MODEL, D_MODEL), lambda k: (0, 0)),
        out_shape=SDS((D_MODEL, D_MODEL), F32))
    g_bd = _wgrad(
        pooled, dmixed, "wgrad_pool_grp", grid=(nk,),
        a_spec=pl.BlockSpec((bk,POOL_W), lambda k: (k, 0)),
        b_spec=pl.BlockSpec((bk,POOL_W), lambda k: (k, 0)),
        out_spec=pl.BlockSpec((POOL_W, POOL_W), lambda k: (0, 0)),
        out_shape=SDS((POOL_W, POOL_W), F32), narrow=False)
    g_out = [t.reshape(N_CHIPS, D_MODEL // N_CHIPS, D_MODEL) for t in g_out]
    return [g_ao, g_po, g_out], g_bd


def _wgrads_mlp(m, dpre, hid, dh2b):
    bk = min(BK, m.shape[0])
    nk = m.shape[0] // bk
    g_mi = _wgrad(
        m, dpre, "wgrad_mlp_in", grid=(N_CHIPS, nk),
        a_spec=pl.BlockSpec((bk,D_MODEL), lambda c, k: (k, 0)),
        b_spec=pl.BlockSpec((bk,D_MODEL), lambda c, k: (k, c)),
        out_spec=pl.BlockSpec((None, D_MODEL, D_MODEL), lambda c, k: (c, 0, 0)),
        out_shape=SDS((N_CHIPS, D_MODEL, D_MODEL), F32))
    g_mo = _wgrad(
        hid, dh2b, "wgrad_mlp_out", grid=(N_CHIPS, nk),
        a_spec=pl.BlockSpec((bk,D_MODEL), lambda c, k: (k, c)),
        b_spec=pl.BlockSpec((bk,D_MODEL), lambda c, k: (k, 0)),
        out_spec=pl.BlockSpec((None, D_MODEL, D_MODEL), lambda c, k: (c, 0, 0)),
        out_shape=SDS((N_CHIPS, D_MODEL, D_MODEL), F32))
    return [g_mi, g_mo]


def _mesh_place():
    x, y, c = lax.axis_index("x"), lax.axis_index("y"), lax.axis_index("c")
    other_chips = [(x, 1 - y), (1 - x, y), (1 - x, 1 - y)]
    return x, y, c, other_chips


ANY = pl.BlockSpec(memory_space=pl.ANY)


def _weight_half_copies(shard_refs, buf_refs, rows, send_sem, recv_sem):
    x, y, c, chips = _mesh_place()
    me = 2 * x + y
    copies = []
    for w, r_full in enumerate(rows):
        rh = r_full // 2
        for r, (px, py) in enumerate(chips):
            k = w * 3 + r
            copies.append(pltpu.make_async_remote_copy(
                src_ref=shard_refs[w].at[pl.ds(c * rh, rh), :], dst_ref=buf_refs[w].at[me, pl.ds(c * rh, rh), :],
                send_sem=send_sem.at[k], recv_sem=recv_sem.at[k], device_id=(px, py, c), device_id_type=MESH))
    return copies


def _pair_forward_copies(buf_refs, rows, send_sem, recv_sem):
    x, y, c, chips = _mesh_place()
    out = []
    for w, r_full in enumerate(rows):
        rh = r_full // 2
        for r, (px, py) in enumerate(chips):
            k = w * 3 + r
            landed = buf_refs[w].at[2 * px + py, pl.ds(c * rh, rh), :]
            theirs = buf_refs[w].at[2 * px + py, pl.ds((1 - c) * rh, rh), :]
            mk = lambda ref: pltpu.make_async_remote_copy(
                src_ref=ref, dst_ref=ref, send_sem=send_sem.at[k], recv_sem=recv_sem.at[k],
                device_id=(x, y, 1 - c), device_id_type=MESH)
            out.append((mk(landed), mk(theirs)))
    return out


def _place_own(block, n_slots, slot):
    buf = lax.empty((n_slots,) + block.shape, block.dtype)
    return lax.dynamic_update_slice(buf, block[None], (slot,) + (0,) * block.ndim)


def _pair_forward(bufs, rows, name):
    n = len(bufs)

    def body(*refs):
        dst = refs[n:2 * n]
        send_sem, recv_sem = refs[2 * n:]
        fwds = _pair_forward_copies(dst, rows, send_sem, recv_sem)
        for fwd, _ in fwds:
            fwd.start()
        for fwd, landing in fwds:
            landing.wait_recv()
            fwd.wait_send()

    return pl.pallas_call(
        body, name=name,
        in_specs=[ANY] * n, out_specs=[ANY] * n,
        out_shape=[SDS(b.shape, b.dtype) for b in bufs],
        scratch_shapes=[pltpu.SemaphoreType.DMA((3 * n,))] * 2,
        input_output_aliases={w: w for w in range(n)},
    )(*bufs)


def _chip_sum_copies(src, dst, send_sem, recv_sem):
    x, y, c, chips = _mesh_place()
    copies = []
    for w in range(len(src)):
        for r, (px, py) in enumerate(chips):
            k = w * 3 + r
            copies.append(pltpu.make_async_remote_copy(
                src_ref=src[w].at[r + 1], dst_ref=dst[w].at[r + 1], send_sem=send_sem.at[k], recv_sem=recv_sem.at[k],
                device_id=(px, py, c), device_id_type=MESH))
    return copies


def _pair_exchange(grads):
    n = len(grads)

    def body(*refs):
        src, dst = refs[:n], refs[n:2 * n]
        send_sem, recv_sem = refs[2 * n:]
        x, y, c, _ = _mesh_place()
        copies = []
        for w in range(n):
            rh = grads[w].shape[1] // 2
            copies.append(pltpu.make_async_remote_copy(
                src_ref=src[w].at[:, pl.ds((1 - c) * rh, rh), :], dst_ref=dst[w],
                send_sem=send_sem.at[w], recv_sem=recv_sem.at[w],
                device_id=(x, y, 1 - c), device_id_type=MESH))
            copies[-1].start()
        for cpy in copies:
            cpy.wait()

    return pl.pallas_call(
        body, name="grad_pair_exchange",
        in_specs=[ANY] * n, out_specs=[ANY] * n,
        out_shape=[SDS((N_CHIPS, g.shape[1] // 2, g.shape[2]), g.dtype) for g in grads],
        scratch_shapes=[pltpu.SemaphoreType.DMA((n,)), pltpu.SemaphoreType.DMA((n,))],
    )(*grads)


def _pair_sum(place, grad, recv, name):
    _, R, C = grad.shape
    rh = R // 2
    br = _row_block(rh, max(256, ELEMENTWISE_BLOCK // C))
    nbh = rh // br

    def body(place_ref, g_ref, r_ref, own_ref, sums_ref):
        s = g_ref[...] + r_ref[...].astype(F32)

        @pl.when(pl.program_id(1) == 0)
        def _():
            own_ref[...] = s

        sums_ref[...] = s.astype(BF16)

    slot = lambda rel, pr: jnp.bitwise_xor(pr[0], rel)
    return pl.pallas_call(
        body, name=name,
        grid_spec=pltpu.PrefetchScalarGridSpec(
            num_scalar_prefetch=1, grid=(nbh, N_CHIPS),
            in_specs=[pl.BlockSpec((None, br, C), lambda i, rel, pr: (slot(rel, pr), pr[1] * nbh + i, 0)),
                      pl.BlockSpec((None, br, C), lambda i, rel, pr: (slot(rel, pr), i, 0))],
            out_specs=[pl.BlockSpec((br, C), lambda i, rel, pr: (i, 0)),
                       pl.BlockSpec((None, br, C), lambda i, rel, pr: (rel, i, 0))]),
        out_shape=[SDS((rh, C), F32), SDS((N_CHIPS, rh, C), BF16)],
        compiler_params=_cp("parallel", "arbitrary"),
    )(place, grad, recv)


def _chip_sum(place, own, recv, name):
    rh, C = own.shape
    br = _row_block(rh, max(256, ELEMENTWISE_BLOCK // C))
    nbh = rh // br

    def body(place_ref, own_ref, r_ref, o_ref):
        o_ref[...] = ((own_ref[...] + r_ref[1].astype(F32)) + r_ref[2].astype(F32)) + r_ref[3].astype(F32)

    return pl.pallas_call(
        body, name=name,
        grid_spec=pltpu.PrefetchScalarGridSpec(
            num_scalar_prefetch=1, grid=(nbh,),
            in_specs=[pl.BlockSpec((br, C), lambda i, pr: (i, 0)),
                      pl.BlockSpec((N_CHIPS, br, C), lambda i, pr: (0, i, 0))],
            out_specs=pl.BlockSpec((br, C), lambda i, pr: (pr[1] * nbh + i, 0))),
        out_shape=SDS((2 * rh, C), F32),
        compiler_params=_cp("parallel"),
    )(place, own, recv)


def _finish_exchange(grads, small_all):
    n = len(grads)

    def body(*refs):
        dst, all_ref = refs[n + 1:2 * n + 1], refs[2 * n + 1]
        send_sem, recv_sem, ssend_sem, srecv_sem = refs[2 * n + 2:]
        x, y, c, chips = _mesh_place()
        sib = (x, y, 1 - c)

        def pack(dev, k, to):
            slot = 4 * dev[0] + 2 * dev[1] + dev[2]
            return pltpu.make_async_remote_copy(
                src_ref=all_ref.at[slot], dst_ref=all_ref.at[slot], send_sem=ssend_sem.at[k],
                recv_sem=srecv_sem.at[k], device_id=to, device_id_type=MESH)

        pack_copies = [pack((x, y, c), 0, sib)] + [pack((x, y, c), 1 + r, (px, py, c))
                                                   for r, (px, py) in enumerate(chips)]
        for cpy in pack_copies:
            cpy.start()
        sends, landings = [], []
        for w in range(n):
            rh = grads[w].shape[0] // 2
            mk = lambda cc: pltpu.make_async_remote_copy(
                src_ref=dst[w].at[pl.ds(cc * rh, rh), :], dst_ref=dst[w].at[pl.ds(cc * rh, rh), :],
                send_sem=send_sem.at[w], recv_sem=recv_sem.at[w], device_id=(x, y, 1 - c), device_id_type=MESH)
            sends.append(mk(c))
            landings.append(mk(1 - c))
            sends[-1].start()
        for r, (px, py) in enumerate(chips):
            pack((px, py, c), 1 + r, (px, py, c)).wait_recv()
            pack_copies.append(pack((px, py, c), 4 + r, sib))
            pack_copies[-1].start()
        pack(sib, 0, sib).wait_recv()
        for r, (px, py) in enumerate(chips):
            pack((px, py, 1 - c), 4 + r, sib).wait_recv()
        for cpy in landings:
            cpy.wait_recv()
        for cpy in sends + pack_copies:
            cpy.wait_send()

    res = pl.pallas_call(
        body, name="grad_finish_exchange",
        in_specs=[ANY] * (n + 1), out_specs=[ANY] * (n + 1),
        out_shape=[SDS(g.shape, g.dtype) for g in grads] + [SDS(small_all.shape, small_all.dtype)],
        scratch_shapes=[pltpu.SemaphoreType.DMA((n,)), pltpu.SemaphoreType.DMA((n,)),
                        pltpu.SemaphoreType.DMA((N_DEV - 1,)), pltpu.SemaphoreType.DMA((N_DEV - 1,))],
        input_output_aliases={w: w for w in range(n + 1)},
    )(*grads, small_all)
    return res[:n], res[n]


def _adamw_math(w, g, m, v):
    m = ADAM_B1 * m + (1.0 - ADAM_B1) * g
    v = ADAM_B2 * v + (1.0 - ADAM_B2) * jnp.square(g)
    m_hat = m / (1.0 - ADAM_B1 ** ADAM_STEP)
    v_hat = v / (1.0 - ADAM_B2 ** ADAM_STEP)
    delta = -ADAM_LR * (m_hat / (jnp.sqrt(v_hat) + ADAM_EPS) + ADAM_WD * w)
    return delta, m, v


def _adamw(w, g, m, v, name):
    R, C = w.shape
    br = _row_block(R, 512)
    if g.ndim == 3:
        n_chunks, cw = g.shape[0], g.shape[2]
        g_spec = pl.BlockSpec((None, br, cw), lambda t, i: (t, i, 0))
    else:
        n_chunks, cw = 1, C
        g_spec = pl.BlockSpec((br, cw), lambda t, i: (i, t))

    def body(w_ref, g_ref, m_ref, v_ref, g_out_ref, d_ref, nm_ref, nv_ref):
        gv = g_ref[...]
        g_out_ref[...] = gv
        d_ref[...], nm_ref[...], nv_ref[...] = _adamw_math(w_ref[...], gv, m_ref[...], v_ref[...])

    spec = pl.BlockSpec((br, cw), lambda t, i: (i, t))
    return pl.pallas_call(
        body, grid=(n_chunks, R // br), name=name, in_specs=[spec, g_spec, spec, spec], out_specs=[spec] * 4,
        out_shape=[SDS((R, C), F32)] * 4, compiler_params=_cp("parallel", "parallel"),
    )(w, g, m, v)


def _small_sum_adamw(all_small, w, m, v):
    loss_row = PACK_ROWS - 8

    def body(all_ref, w_ref, m_ref, v_ref, g_ref, d_ref, nm_ref, nv_ref, loss_ref):
        g = all_ref[0]
        for k in range(1, N_DEV):
            g = g + all_ref[k]
        g_ref[...] = g
        d_ref[...], nm_ref[...], nv_ref[...] = _adamw_math(w_ref[...], g, m_ref[...], v_ref[...])
        total = jnp.sum(g[loss_row:loss_row + 1, :]) * (0.5 / D_MODEL)
        loss_ref[...] = jnp.full(loss_ref.shape, total, F32)

    full = lambda s: pl.BlockSpec(s, lambda i: (0,) * len(s))
    pack = (PACK_ROWS, D_MODEL)
    return pl.pallas_call(
        body, grid=(1,), name="small_sum_adamw",
        in_specs=[full((N_DEV,) + pack), full(pack), full(pack), full(pack)],
        out_specs=[full(pack)] * 4 + [full((8, 128))],
        out_shape=[SDS(pack, F32)] * 4 + [SDS((8, 128), F32)],
        compiler_params=_cp("arbitrary"),
    )(all_small, w, m, v)


def _pack_small(grp, scale, g_mix, g_mlp, g_f, loss_lanes):
    def part(vec):
        vec = vec.reshape(1, -1)
        return jnp.pad(vec, ((0, 7), (0, D_MODEL - vec.shape[1])))
    return jnp.concatenate([grp.reshape(-1, D_MODEL), part(scale), part(g_mix), part(g_mlp), part(g_f),
                            part(loss_lanes)], axis=0)


def _unpack_small(pack):
    n_grp = len(POOL_WINDOWS) * POOL_GROUP_W * POOL_GROUP_W // D_MODEL
    grp = pack[:n_grp].reshape(1, len(POOL_WINDOWS), POOL_GROUP_W, POOL_GROUP_W)
    scale = pack[n_grp, :POOL_W].reshape(1, POOL_W)
    g_mix = pack[n_grp + 8].reshape(1, D_MODEL)
    g_mlp = pack[n_grp + 16].reshape(1, D_MODEL)
    g_f = pack[n_grp + 24].reshape(D_MODEL)
    return grp, scale, g_mix, g_mlp, g_f


def _block_diag(grp):
    out = jnp.zeros((POOL_W, POOL_W), grp.dtype)
    for k in range(len(POOL_WINDOWS)):
        out = lax.dynamic_update_slice(out, grp[k], (k * POOL_GROUP_W, k * POOL_GROUP_W))
    return out


def kernel(x, norm_mix_g, w_in, w_att_out, w_pool_grp, pool_scale, w_pool_out, w_out, norm_mlp_g, w_mlp_in, w_mlp_out, norm_final_g, loss_target, m_norm_mix_g, m_w_in, m_w_att_out, m_w_pool_grp, m_pool_scale, m_w_pool_out, m_w_out, m_norm_mlp_g, m_w_mlp_in, m_w_mlp_out, m_norm_final_g, v_norm_mix_g, v_w_in, v_w_att_out, v_w_pool_grp, v_pool_scale, v_w_pool_out, v_w_out, v_norm_mlp_g, v_w_mlp_in, v_w_mlp_out, v_norm_final_g):
    S = x.shape[1]
    xs, target = x[0], loss_target[0]
    big = [w_in[0], w_att_out[0], w_pool_out[0], w_out[0], w_mlp_in[0], w_mlp_out[0]]
    big_m = [m_w_in[0], m_w_att_out[0], m_w_pool_out[0], m_w_out[0], m_w_mlp_in[0], m_w_mlp_out[0]]
    big_v = [v_w_in[0], v_w_att_out[0], v_w_pool_out[0], v_w_out[0], v_w_mlp_in[0], v_w_mlp_out[0]]

    chip = 2 * lax.axis_index("x") + lax.axis_index("y")
    core = lax.axis_index("c")
    place = jnp.stack([chip, core]).astype(jnp.int32)
    names = ("w_in", "w_att_out", "w_pool_out", "w_out", "w_mlp_in", "w_mlp_out")

    shards = [w.astype(BF16) for w in big]
    bufs = [_place_own(sh, N_CHIPS, chip) for sh in shards]
    wbd = _block_diag(w_pool_grp[0]).astype(BF16)
    g_final = norm_final_g.reshape(1, D_MODEL)
    lane = lax.broadcasted_iota(jnp.int32, (GROUP_W, GROUP_W), 0) // HEAD_W
    head_ones = (lane == lane.T).astype(BF16)

    u, z_own, landed_in = _norm_inproj_own(xs, norm_mix_g, shards[0], bufs[0])
    (wg_in,) = _pair_forward([landed_in], [shards[0].shape[0]], "w_in_pair_forward")
    (qkv0, qkv1, qkv2, pz, gates), landed = _inproj_rest(u, z_own, wg_in, shards[1:], bufs[1:])
    wg_ao, wg_po, wg_out, wg_mi, wg_mo = _pair_forward(landed, [sh.shape[0] for sh in shards[1:]],
                                                       "weights_pair_forward")
    wg_out = wg_out.reshape(D_MODEL, D_MODEL)
    qkv = (qkv0, qkv1, qkv2)
    att = [_attn_fwd(qkv[grp], grp) for grp in range(3)]
    a, lt0, lt1, lt2, pooled, mixed, p, merged, h1, m = _mixer_out(
        [o for o, _ in att], [l for _, l in att], pz, gates, xs, wg_ao, wg_po, wbd, pool_scale, wg_out, norm_mlp_g)
    hid, dh2, dh2b, loss_lanes, dg_final = _mlp_fwd_loss(m, h1, target, wg_mi, wg_mo, g_final)

    def pair_reduce(grads, grad_names):
        recv = _pair_exchange([narrow for _, narrow in grads])
        pair = [_pair_sum(place, g, r, f"pair_sum_{nm}") for (g, _), r, nm in zip(grads, recv, grad_names)]
        return [own for own, _ in pair], [s for _, s in pair]

    def chip_reduce(owns, landed_sums, grad_names):
        return [_chip_sum(place, own, r, f"chip_sum_{nm}") for own, r, nm in zip(owns, landed_sums, grad_names)]

    dpre, dh1, dh1b, dg_mlp = _mlp_bwd(dh2, dh2b, hid, h1, wg_mi, wg_mo, norm_mlp_g)
    own_mlp, sums_mlp = pair_reduce(_wgrads_mlp(m, dpre, hid, dh2b), names[4:])
    (da1, dp1, dgates, da0, dag1, dag2, dd0, dd1, dd2, dmixed, dqp, dscale), landed_mlp = _mixer_bwd(
        dh1b, a, p, mixed, gates, wg_out, wg_ao, wg_po, wbd, pool_scale, head_ones, sums_mlp)
    g_mi, g_mo = chip_reduce(own_mlp, landed_mlp, names[4:])
    grads_mixer, g_bd = _wgrads_mixer(a, da1, p, dp1, merged, dh1b, pooled, dmixed)
    dqkv = [_attn_bwd(qkv[grp], da_g, lt_g, dd_g, grp)
            for grp, (da_g, lt_g, dd_g) in enumerate(((da0, lt0, dd0), (dag1, lt1, dd1), (dag2, lt2, dd2)))]
    dz = _dz_assemble(dqkv, dqp)
    own_in, sums_in = pair_reduce([_wgrad_in(u, dz, dgates)] + grads_mixer, names[:4])
    dx, dg_mix, landed_in = _inproj_dx(dz, dgates, dh1, xs, norm_mix_g, wg_in, sums_in)
    g_in, g_ao, g_po, g_out = chip_reduce(own_in, landed_in, names[:4])

    g_grp = jnp.stack([g_bd[k * POOL_GROUP_W:(k + 1) * POOL_GROUP_W, k * POOL_GROUP_W:(k + 1) * POOL_GROUP_W]
                       for k in range(len(POOL_WINDOWS))])
    small = _pack_small(g_grp, dscale, dg_mix, dg_mlp, dg_final, loss_lanes)
    full, small_all = _finish_exchange([g_in, g_ao, g_po, g_out, g_mi, g_mo],
                                       _place_own(small, N_DEV, 2 * chip + core))

    zero = jnp.zeros((D_MODEL,), F32)
    small_w = _pack_small(w_pool_grp[0], pool_scale, norm_mix_g, norm_mlp_g, norm_final_g, zero)
    small_m = _pack_small(m_w_pool_grp[0], m_pool_scale, m_norm_mix_g, m_norm_mlp_g, m_norm_final_g, zero)
    small_v = _pack_small(v_w_pool_grp[0], v_pool_scale, v_norm_mix_g, v_norm_mlp_g, v_norm_final_g, zero)
    sg, sd, sm, sv, loss_tile = _small_sum_adamw(small_all, small_w, small_m, small_v)
    full = [full[0].reshape(CHUNKS_PER_SHARD, D_MODEL, CHUNK)] + list(full[1:])
    upd = [_adamw(w, g, mm, vv, f"adamw_{nm}") for w, g, mm, vv, nm in zip(big, full, big_m, big_v, names)]

    def ordered(small_pack, bigs):
        grp, scale, g_mix, g_mlp, g_f = _unpack_small(small_pack)
        b_in, b_ao, b_po, b_out, b_mi, b_mo = [b[None] for b in bigs]
        return (g_mix, b_in, b_ao, grp, scale, b_po, b_out, g_mlp, b_mi, b_mo, g_f)

    return (loss_tile[0, 0], dx[None],
            *ordered(sg, [t[0] for t in upd]),
            *ordered(sd, [t[1] for t in upd]),
            *ordered(sm, [t[2] for t in upd]),
            *ordered(sv, [t[3] for t in upd]))
```

```python
import functools

import jax
import jax.numpy as jnp
from jax import lax
from jax.experimental import pallas as pl
from jax.experimental.pallas import tpu as pltpu

F32 = jnp.float32
BF16 = jnp.bfloat16
SDS = jax.ShapeDtypeStruct
MESH = pl.DeviceIdType.MESH

D_MODEL = 1024
D_FF = 4096
N_CHIPS = 4
N_DEV = 8
DILATIONS = (1, 4, 16)
BAND = 128
GROUP_W = 256
PAIR_W = 128
HEAD_W = 64
POOL_W = 768
POOL_GROUP_W = 192
POOL_WINDOWS = (2, 4, 8, 16)
POOL_HALO = 16
N_IN = 5120
CHUNK = 256
N_CHUNKS = N_IN // CHUNK
N_DZ_CHUNKS = 12
CHUNKS_PER_SHARD = 5
WGRAD_IN_GROUP = 4
NORM_EPS = 1e-6
ALIBI_MAX_BIAS = 8.0
N_HEADS = 12
NEG = -1e30

ADAM_LR, ADAM_B1, ADAM_B2, ADAM_EPS, ADAM_WD, ADAM_STEP = 0.001, 0.9, 0.999, 1e-08, 0.01, 10

TM = 512
TMB = 512
ATT_TILE = ((1, 4), (4, 1), (4, 1))
BK = 4096
ELEMENTWISE_BLOCK = 1 << 20
VMEM_LIMIT = 56 * 1024 * 1024
PACK_ROWS = 184

NT = (((1,), (1,)), ((), ()))
TN = (((0,), (0,)), ((), ()))


def _cp(*sem):
    return pltpu.CompilerParams(dimension_semantics=sem, vmem_limit_bytes=VMEM_LIMIT)


def _resident(shape):
    nd = len(shape)
    return pl.BlockSpec(shape, lambda *_: (0,) * nd, pipeline_mode=pl.Buffered(1))


def _row_block(rows, cap=256):
    return max(b for b in range(16, min(rows, cap) + 1, 16) if rows % b == 0)


def _dot(a, b):
    return jnp.dot(a, b, preferred_element_type=F32)


def _dot_nt(a, b):
    return lax.dot_general(a, b, NT, preferred_element_type=F32)


def _dot_tn(a, b):
    return lax.dot_general(a, b, TN, preferred_element_type=F32)


def _w_in_chunk(w_ref, n):
    return w_ref[n // CHUNKS_PER_SHARD, :, (n % CHUNKS_PER_SHARD) * CHUNK:(n % CHUNKS_PER_SHARD + 1) * CHUNK]


def _sigmoid(x):
    return 0.5 * jnp.tanh(0.5 * x.astype(F32)) + 0.5


def _rms_fwd(x, g):
    r = lax.rsqrt(jnp.mean(x * x, axis=-1, keepdims=True) + NORM_EPS)
    xh = x * r
    return xh * g, xh, r


def _rms_bwd(dy, xh, r, g):
    dxh = dy * g
    return r * (dxh - xh * jnp.mean(dxh * xh, axis=-1, keepdims=True))


def _deinterleave_store(val, s_ref, out_ref, lead, d, rows, dtype):
    if d == 1:
        out_ref[lead + (0,)] = val.astype(dtype)
        return
    for h in range(2):
        s_ref[h] = val[:, h * PAIR_W:(h + 1) * PAIR_W]
    for r in range(d):
        for h in range(2):
            out_ref[lead + (r, slice(None), slice(h * PAIR_W, (h + 1) * PAIR_W))] = (
                s_ref[h, pl.ds(r, rows // d, stride=d), :].astype(dtype))


def _interleave_load(in_ref, lead, s_ref, d, rows):
    for r in range(d):
        for h in range(2):
            s_ref[h, pl.ds(r, rows // d, stride=d), :] = (
                in_ref[lead + (r, slice(None), slice(h * PAIR_W, (h + 1) * PAIR_W))].astype(F32))


def _norm_inproj_own(x, g, w_own, buf):
    S = x.shape[0]
    n_tiles = S // TM

    def body(x_ref, g_ref, w_ref, shard_ref, buf_in, u_ref, z_ref, buf_ref, send_sem, recv_sem):
        i = pl.program_id(0)

        def copies():
            return _weight_half_copies([shard_ref], [buf_ref], [w_own.shape[0]], send_sem, recv_sem)

        @pl.when(i == 0)
        def _():
            for cpy in copies():
                cpy.start()

        u = _rms_fwd(x_ref[...], g_ref[...])[0].astype(BF16)
        u_ref[...] = u
        for t in range(CHUNKS_PER_SHARD):
            z_ref[t] = _dot(u, w_ref[:, t * CHUNK:(t + 1) * CHUNK])

        @pl.when(i == n_tiles - 1)
        def _():
            for cpy in copies():
                cpy.wait()

    row = lambda w: pl.BlockSpec((TM, w), lambda i: (i, 0))
    return pl.pallas_call(
        body, grid=(n_tiles,), name="norm_inproj_own",
        in_specs=[row(D_MODEL), _resident((1, D_MODEL)), _resident(w_own.shape), ANY, ANY],
        out_specs=[row(D_MODEL), pl.BlockSpec((CHUNKS_PER_SHARD, TM, CHUNK), lambda i: (0, i, 0)), ANY],
        out_shape=[SDS((S, D_MODEL), BF16), SDS((CHUNKS_PER_SHARD, S, CHUNK), F32), SDS(buf.shape, buf.dtype)],
        scratch_shapes=[pltpu.SemaphoreType.DMA((3,)), pltpu.SemaphoreType.DMA((3,))],
        input_output_aliases={4: 2},
        compiler_params=_cp("arbitrary"),
    )(x, g, w_own, w_own, buf)


def _hosted_allgather(i, n_steps, shard_refs, buf_refs, rows, sems):
    send_sem, recv_sem, fsend_sem, frecv_sem = sems
    ici = lambda: _weight_half_copies(shard_refs, buf_refs, rows, send_sem, recv_sem)
    forward = lambda: _pair_forward_copies(buf_refs, rows, fsend_sem, frecv_sem)

    def begin():
        @pl.when(i == 0)
        def _():
            for cpy in ici():
                cpy.start()

        @pl.when(i == n_steps // 2)
        def _():
            for cpy, (fwd, _) in zip(ici(), forward()):
                cpy.wait_recv()
                fwd.start()

    def end():
        @pl.when(i == n_steps - 1)
        def _():
            for cpy, (fwd, landing) in zip(ici(), forward()):
                landing.wait_recv()
                fwd.wait_send()
                cpy.wait_send()

    return begin, end


def _inproj_rest(u, z_own, w_in, shards, bufs):
    S = u.shape[0]
    n_tiles = S // TM
    n = len(shards)

    def body(*refs):
        u_ref, zown_ref, w_ref = refs[0:3]
        shard_refs = refs[3:3 + n]
        q0_ref, q1_ref, q2_ref, pz_ref, gate_ref = refs[3 + 2 * n:8 + 2 * n]
        buf_refs = refs[8 + 2 * n:8 + 3 * n]
        s_ref = refs[8 + 3 * n]
        i = pl.program_id(0)
        chip = 2 * lax.axis_index("x") + lax.axis_index("y")
        begin, end = _hosted_allgather(i, n_tiles, shard_refs, buf_refs, [sh.shape[0] for sh in shards],
                                       refs[9 + 3 * n:])
        begin()

        u = u_ref[...]
        qkv_refs = (q0_ref, q1_ref, q2_ref)

        def emit(k, zc):
            if k < 9:
                which, grp = k // 3, k % 3
                if which == 0:
                    zc = zc * 0.125
                _deinterleave_store(zc, s_ref, qkv_refs[grp], (which,), DILATIONS[grp], TM, BF16)
            elif k < N_DZ_CHUNKS:
                pz_ref[:, (k - 9) * CHUNK:(k - 8) * CHUNK] = zc
            else:
                gate_ref[:, (k - N_DZ_CHUNKS) * CHUNK:(k - N_DZ_CHUNKS + 1) * CHUNK] = zc.astype(BF16)

        def all_chunks(own_shard):
            for k in range(N_CHUNKS):
                if k // CHUNKS_PER_SHARD == own_shard:
                    emit(k, zown_ref[k % CHUNKS_PER_SHARD])
                else:
                    emit(k, _dot(u, _w_in_chunk(w_ref, k)))

        for shard in range(N_CHIPS):
            pl.when(chip == shard)(functools.partial(all_chunks, shard))
        end()

    row = lambda w: pl.BlockSpec((TM, w), lambda i: (i, 0))
    res = pl.pallas_call(
        body, grid=(n_tiles,), name="inproj_rest",
        in_specs=[row(D_MODEL), pl.BlockSpec((CHUNKS_PER_SHARD, TM, CHUNK), lambda i: (0, i, 0)),
                  _resident(w_in.shape)] + [ANY] * (2 * n),
        out_specs=[pl.BlockSpec((3, d, TM // d, GROUP_W), lambda i: (0, 0, i, 0)) for d in DILATIONS]
        + [row(POOL_W), row(2 * D_MODEL)] + [ANY] * n,
        out_shape=[SDS((3, d, S // d, GROUP_W), BF16) for d in DILATIONS]
        + [SDS((S, POOL_W), F32), SDS((S, 2 * D_MODEL), BF16)] + [SDS(b.shape, b.dtype) for b in bufs],
        scratch_shapes=[pltpu.VMEM((2, TM, PAIR_W), F32)] + [pltpu.SemaphoreType.DMA((3 * n,))] * 4,
        input_output_aliases={3 + n + w: 5 + w for w in range(n)},
        compiler_params=_cp("arbitrary"),
    )(u, z_own, w_in, *shards, *bufs)
    return res[:5], res[5:]


def _band_bias(grp, d):
    row = lax.broadcasted_iota(jnp.int32, (BAND, 2 * BAND), 0)
    col = lax.broadcasted_iota(jnp.int32, (BAND, 2 * BAND), 1)
    steps = BAND + row - col
    valid = (steps >= 0) & (steps <= BAND)
    stepsf = (steps * d).astype(F32)
    biases = []
    for hh in range(4):
        slope = 2.0 ** (-ALIBI_MAX_BIAS * (grp * 4 + hh + 1) / N_HEADS)
        biases.append(jnp.where(valid, -slope * stepsf, NEG))
    return biases, col


def _attn_tiles(grp, L):
    rr, rb = ATT_TILE[grp]
    rb = min(rb, L // BAND)
    return rr, rb, L // (rb * BAND)


def _kv_tile(cur_ref, prev_ref, rr, rb, cs):
    if rb == 0:
        return jnp.concatenate([prev_ref[rr, :, cs], cur_ref[rr, 0:BAND, cs]], axis=0)
    return cur_ref[rr, (rb - 1) * BAND:(rb + 1) * BAND, cs]


def _attn_fwd(qkv, grp):
    d = DILATIONS[grp]
    L = qkv.shape[2]
    RR, RB, nb = _attn_tiles(grp, L)

    def body(q_ref, kc_ref, kp_ref, vc_ref, vp_ref, o_ref, lse_ref):
        i = pl.program_id(0)
        biases, col = _band_bias(grp, d)
        first_keys_ok = (col >= BAND) | (i > 0)
        is_a = lax.broadcasted_iota(jnp.int32, (BAND, PAIR_W), 1) < HEAD_W
        heads = [(rr, rb, cp, h2) for rr in range(RR) for rb in range(RB) for cp in range(2) for h2 in range(2)]

        def tile(head):
            rr, rb, cp, _ = head
            return rr, rb, slice(rb * BAND, (rb + 1) * BAND), slice(cp * PAIR_W, (cp + 1) * PAIR_W)

        def scores(head):
            rr, rb, rows, cs = tile(head)
            q2 = q_ref[rr, rows, cs]
            b = biases[head[2] * 2 + head[3]]
            if rb == 0:
                b = jnp.where(first_keys_ok, b, NEG)
            sel = is_a if head[3] == 0 else jnp.logical_not(is_a)
            return _dot_nt(jnp.where(sel, q2, jnp.zeros_like(q2)), _kv_tile(kc_ref, kp_ref, rr, rb, cs)) + b

        s_next = scores(heads[0])
        res = {}
        for idx, head in enumerate(heads):
            s = s_next
            if idx + 1 < len(heads):
                s_next = scores(heads[idx + 1])
            rr, rb, rows, cs = tile(head)
            m = jnp.max(s, axis=-1, keepdims=True)
            p = jnp.exp(s - m)
            l = jnp.sum(p, axis=-1, keepdims=True)
            o = _dot(p.astype(BF16), _kv_tile(vc_ref, vp_ref, rr, rb, cs)) * (1.0 / l)
            res[head[3]] = (o, m + jnp.log(l))
            if head[3] == 1:
                o_ref[rr, rows, cs] = jnp.where(is_a, res[0][0], res[1][0]).astype(BF16)
                lse_ref[rr, rows, cs] = jnp.where(is_a, res[0][1], res[1][1])

    cur = lambda w: pl.BlockSpec((None, RR, RB * BAND, GROUP_W), lambda i, j: (w, j, i, 0))
    prev = lambda w: pl.BlockSpec((None, RR, BAND, GROUP_W), lambda i, j: (w, j, jnp.maximum(i * RB - 1, 0), 0))
    out = pl.BlockSpec((RR, RB * BAND, GROUP_W), lambda i, j: (j, i, 0))
    return pl.pallas_call(
        body, grid=(nb, d // RR), name=f"attn_fwd_g{grp}",
        in_specs=[cur(0), cur(1), prev(1), cur(2), prev(2)],
        out_specs=[out, out],
        out_shape=[SDS((d, L, GROUP_W), BF16), SDS((d, L, GROUP_W), F32)],
        compiler_params=_cp("parallel", "parallel"),
    )(qkv, qkv, qkv, qkv, qkv)


def _pool_column_select(col, vals):
    return jnp.where(col < POOL_GROUP_W, vals[0],
                     jnp.where(col < 2 * POOL_GROUP_W, vals[1],
                               jnp.where(col < 3 * POOL_GROUP_W, vals[2], vals[3])))


def _pool_inv_count(i, rows):
    t = i * rows + lax.broadcasted_iota(jnp.int32, (rows, POOL_W), 0)
    col = lax.broadcasted_iota(jnp.int32, (rows, POOL_W), 1)
    win = _pool_column_select(col, POOL_WINDOWS)
    return 1.0 / jnp.minimum(t + 1, win).astype(F32), col


def _mixer_out(outs, lses, pz, gates, x, w_ao, w_po, wbd, scale, w_out, g_mlp, shards, bufs):
    S = x.shape[0]
    n_tiles = S // TMB
    n = len(shards)

    def body(*refs):
        (o0_ref, l0_ref, o1_ref, l1_ref, o2_ref, l2_ref, pz_ref, halo_ref, gate_ref, x_ref,
         wao_ref, wpo_ref, wbd_ref, sc_ref, wout_ref, g_ref) = refs[0:16]
        shard_refs = refs[16:16 + n]
        (a_ref, lt0_ref, lt1_ref, lt2_ref, pooled_ref, mixed_ref, p_ref, merged_ref, h1_ref,
         m_ref) = refs[16 + 2 * n:26 + 2 * n]
        buf_refs = refs[26 + 2 * n:26 + 3 * n]
        so1, sl1, so2, sl2, slt, ext_ref = refs[26 + 3 * n:32 + 3 * n]
        i = pl.program_id(0)
        begin, end = _hosted_allgather(i, n_tiles, shard_refs, buf_refs, [sh.shape[0] for sh in shards],
                                       refs[32 + 3 * n:])
        begin()
        _interleave_load(o1_ref, (), so1, DILATIONS[1], TMB)
        _interleave_load(l1_ref, (), sl1, DILATIONS[1], TMB)
        _interleave_load(o2_ref, (), so2, DILATIONS[2], TMB)
        _interleave_load(l2_ref, (), sl2, DILATIONS[2], TMB)
        for h in range(2):
            hs = slice(h * PAIR_W, (h + 1) * PAIR_W)
            l0, l1, l2 = l0_ref[0, :, hs], sl1[h], sl2[h]
            mx = jnp.maximum(jnp.maximum(l0, l1), l2)
            e0, e1, e2 = jnp.exp(l0 - mx), jnp.exp(l1 - mx), jnp.exp(l2 - mx)
            den = e0 + e1 + e2
            a_ref[:, hs] = ((e0 * o0_ref[0, :, hs].astype(F32) + e1 * so1[h] + e2 * so2[h])
                            * (1.0 / den)).astype(BF16)
            slt[h] = mx + jnp.log(den)
        lt = jnp.concatenate([slt[0], slt[1]], axis=1)
        lt0_ref[0] = lt
        for ref, d in ((lt1_ref, DILATIONS[1]), (lt2_ref, DILATIONS[2])):
            for r in range(d):
                for h in range(2):
                    ref[r, :, h * PAIR_W:(h + 1) * PAIR_W] = slt[h, pl.ds(r, TMB // d, stride=d), :]

        pz_t = pz_ref[...]
        ext_ref[0:POOL_HALO, :] = jnp.where(i > 0, halo_ref[...], 0.0)
        ext_ref[POOL_HALO:, :] = pz_t
        sums = []
        acc = ext_ref[...]
        for k in (1, 2, 4, 8):
            acc = acc + pltpu.roll(acc, k, 0)
            sums.append(acc[POOL_HALO:, :])
        inv_cnt, col = _pool_inv_count(i, TMB)
        pooled = (_pool_column_select(col, sums) * inv_cnt - pz_t).astype(BF16)
        pooled_ref[...] = pooled
        mixed = _dot(pooled, wbd_ref[...])
        mixed_ref[...] = mixed.astype(BF16)
        p = (mixed * sc_ref[...]).astype(BF16)
        p_ref[...] = p

        a = a_ref[...]
        for j in range(N_CHIPS):
            js = slice(j * CHUNK, (j + 1) * CHUNK)
            ga = gate_ref[:, js]
            gp = gate_ref[:, D_MODEL + j * CHUNK:D_MODEL + (j + 1) * CHUNK]
            mj = _sigmoid(ga) * _dot(a, wao_ref[j]) + _sigmoid(gp) * _dot(p, wpo_ref[j])
            merged_ref[:, js] = mj.astype(BF16)
        h1 = x_ref[...] + _dot(merged_ref[...], wout_ref[...])
        h1_ref[...] = h1
        m_ref[...] = _rms_fwd(h1, g_ref[...])[0].astype(BF16)
        end()

    row = lambda w: pl.BlockSpec((TMB, w), lambda i: (i, 0))
    grp_spec = lambda d: pl.BlockSpec((d, TMB // d, GROUP_W), lambda i: (0, i, 0))
    halo = pl.BlockSpec((POOL_HALO, POOL_W), lambda i: (jnp.maximum(i * (TMB // POOL_HALO) - 1, 0), 0))
    d0, d1, d2 = DILATIONS
    res = pl.pallas_call(
        body, grid=(n_tiles,), name="mixer_out",
        in_specs=[grp_spec(d0), grp_spec(d0), grp_spec(d1), grp_spec(d1), grp_spec(d2), grp_spec(d2),
                  row(POOL_W), halo, row(2 * D_MODEL), row(D_MODEL),
                  _resident(w_ao.shape), _resident(w_po.shape), _resident(wbd.shape), _resident(scale.shape),
                  _resident(w_out.shape), _resident(g_mlp.shape)] + [ANY] * (2 * n),
        out_specs=[row(GROUP_W), grp_spec(d0), grp_spec(d1), grp_spec(d2),
                   row(POOL_W), row(POOL_W), row(POOL_W), row(D_MODEL), row(D_MODEL), row(D_MODEL)] + [ANY] * n,
        out_shape=[SDS((S, GROUP_W), BF16)] + [SDS((d, S // d, GROUP_W), F32) for d in DILATIONS]
        + [SDS((S, POOL_W), BF16), SDS((S, POOL_W), BF16), SDS((S, POOL_W), BF16),
           SDS((S, D_MODEL), BF16), SDS((S, D_MODEL), F32), SDS((S, D_MODEL), BF16)]
        + [SDS(b.shape, b.dtype) for b in bufs],
        scratch_shapes=[pltpu.VMEM((2, TMB, PAIR_W), F32) for _ in range(5)]
        + [pltpu.VMEM((TMB + POOL_HALO, POOL_W), F32)] + [pltpu.SemaphoreType.DMA((3 * n,))] * 4,
        input_output_aliases={16 + n + w: 10 + w for w in range(n)},
        compiler_params=_cp("arbitrary"),
    )(outs[0], lses[0], outs[1], lses[1], outs[2], lses[2], pz, pz, gates, x,
      w_ao, w_po, wbd, scale, w_out, g_mlp, *shards, *bufs)
    return res[:10], res[10:]


def _mlp_fwd_loss(m, h1, target, w_mi, w_mo, g_f):
    S = m.shape[0]

    def body(m_ref, h1_ref, t_ref, wmi_ref, wmo_ref, g_ref, hid_ref, dh2_ref, dh2b_ref, loss_ref, dg_ref):
        @pl.when(pl.program_id(0) == 0)
        def _():
            loss_ref[...] = jnp.zeros_like(loss_ref)
            dg_ref[...] = jnp.zeros_like(dg_ref)

        mt = m_ref[...]
        acc = h1_ref[...]
        for c in range(N_CHIPS):
            hid = jnp.square(jnp.maximum(_dot(mt, wmi_ref[c]), 0.0)).astype(BF16)
            hid_ref[:, c * D_MODEL:(c + 1) * D_MODEL] = hid
            acc = acc + _dot(hid, wmo_ref[c])
        g = g_ref[...]
        y, hh, r = _rms_fwd(acc, g)
        e = y - t_ref[...]
        loss_ref[...] += jnp.sum(e * e, axis=0, keepdims=True)
        dy = e * (1.0 / D_MODEL)
        dg_ref[...] += jnp.sum(dy * hh, axis=0, keepdims=True)
        dh2 = _rms_bwd(dy, hh, r, g)
        dh2_ref[...] = dh2
        dh2b_ref[...] = dh2.astype(BF16)

    row = lambda w: pl.BlockSpec((TM, w), lambda i: (i, 0))
    vec = pl.BlockSpec((1, D_MODEL), lambda i: (0, 0))
    return pl.pallas_call(
        body, grid=(S // TM,), name="mlp_fwd_loss",
        in_specs=[row(D_MODEL), row(D_MODEL), row(D_MODEL), _resident(w_mi.shape), _resident(w_mo.shape),
                  _resident(g_f.shape)],
        out_specs=[row(D_FF), row(D_MODEL), row(D_MODEL), vec, vec],
        out_shape=[SDS((S, D_FF), BF16), SDS((S, D_MODEL), F32), SDS((S, D_MODEL), BF16),
                   SDS((1, D_MODEL), F32), SDS((1, D_MODEL), F32)],
        compiler_params=_cp("arbitrary"),
    )(m, h1, target, w_mi, w_mo, g_f)


def _mlp_bwd(dh2, dh2b, hid, h1, w_mi, w_mo, g_mlp):
    S = dh2.shape[0]

    def body(dh2_ref, dh2b_ref, hid_ref, h1_ref, wmi_ref, wmo_ref, g_ref, dpre_ref, dh1_ref, dh1b_ref, dg_ref):
        @pl.when(pl.program_id(0) == 0)
        def _():
            dg_ref[...] = jnp.zeros_like(dg_ref)

        d2 = dh2b_ref[...]
        dm = jnp.zeros((TM, D_MODEL), F32)
        dhid_next = _dot_nt(d2, wmo_ref[0])
        for c in range(N_CHIPS):
            cs = slice(c * D_MODEL, (c + 1) * D_MODEL)
            dhid = dhid_next
            if c + 1 < N_CHIPS:
                dhid_next = _dot_nt(d2, wmo_ref[c + 1])
            dpre = (dhid * (2.0 * jnp.sqrt(hid_ref[:, cs].astype(F32)))).astype(BF16)
            dpre_ref[:, cs] = dpre
            dm = dm + _dot_nt(dpre, wmi_ref[c])
        g = g_ref[...]
        _, hh, r = _rms_fwd(h1_ref[...], g)
        dg_ref[...] += jnp.sum(dm * hh, axis=0, keepdims=True)
        dh1 = dh2_ref[...] + _rms_bwd(dm, hh, r, g)
        dh1_ref[...] = dh1
        dh1b_ref[...] = dh1.astype(BF16)

    row = lambda w: pl.BlockSpec((TM, w), lambda i: (i, 0))
    return pl.pallas_call(
        body, grid=(S // TM,), name="mlp_bwd",
        in_specs=[row(D_MODEL), row(D_MODEL), row(D_FF), row(D_MODEL), _resident(w_mi.shape),
                  _resident(w_mo.shape), _resident(g_mlp.shape)],
        out_specs=[row(D_FF), row(D_MODEL), row(D_MODEL), pl.BlockSpec((1, D_MODEL), lambda i: (0, 0))],
        out_shape=[SDS((S, D_FF), BF16), SDS((S, D_MODEL), F32), SDS((S, D_MODEL), BF16), SDS((1, D_MODEL), F32)],
        compiler_params=_cp("arbitrary"),
    )(dh2, dh2b, hid, h1, w_mi, w_mo, g_mlp)


def _mixer_bwd(dh1b, a, p, mixed, gates, w_out, w_ao, w_po, wbd, scale, head_ones, sums):
    S = a.shape[0]
    n_tiles = S // TMB
    n = len(sums)

    def body(*refs):
        (dh1b_ref, a_ref, p_ref, mixed_ref, gate_ref, wout_ref, wao_ref, wpo_ref, wbd_ref, sc_ref,
         ones_ref) = refs[0:11]
        sum_refs = refs[11:11 + n]
        (da1_ref, dp1_ref, dgate_ref, da0_ref, dag1_ref, dag2_ref, dd0_ref, dd1_ref, dd2_ref,
         dmixed_ref, dqp_ref, dscale_ref) = refs[11 + n:23 + n]
        land_refs = refs[23 + n:23 + 2 * n]
        s_da, s_dd, send_sem, recv_sem = refs[23 + 2 * n:]
        i = pl.program_id(0)

        @pl.when(i == 0)
        def _():
            dscale_ref[...] = jnp.zeros_like(dscale_ref)
            for cpy in _chip_sum_copies(sum_refs, land_refs, send_sem, recv_sem):
                cpy.start()

        dmerged = _dot_nt(dh1b_ref[...], wout_ref[...])
        a = a_ref[...]
        p = p_ref[...]
        da = jnp.zeros((TMB, GROUP_W), F32)
        dp = jnp.zeros((TMB, POOL_W), F32)
        for j in range(N_CHIPS):
            js = slice(j * CHUNK, (j + 1) * CHUNK)
            sa = _sigmoid(gate_ref[:, js])
            sp = _sigmoid(gate_ref[:, D_MODEL + j * CHUNK:D_MODEL + (j + 1) * CHUNK])
            dmj = dmerged[:, js]
            da1 = (dmj * sa).astype(BF16)
            dp1 = (dmj * sp).astype(BF16)
            da1_ref[:, js] = da1
            dp1_ref[:, js] = dp1
            dgate_ref[j] = (dmj * _dot(a, wao_ref[j]) * sa * (1.0 - sa)).astype(BF16)
            dgate_ref[N_CHIPS + j] = (dmj * _dot(p, wpo_ref[j]) * sp * (1.0 - sp)).astype(BF16)
            da = da + _dot_nt(da1, wao_ref[j])
            dp = dp + _dot_nt(dp1, wpo_ref[j])

        prod = da * a.astype(F32)
        hi = prod.astype(BF16)
        lo = (prod - hi.astype(F32)).astype(BF16)
        dd = _dot(hi, ones_ref[...]) + _dot(lo, ones_ref[...])
        for ref, val, sref, dtype in ((da0_ref, da, s_da, BF16), (dd0_ref, dd, s_dd, F32)):
            ref[0] = val.astype(dtype)
            for h in range(2):
                sref[h] = val[:, h * PAIR_W:(h + 1) * PAIR_W]
        for refs, d in (((dag1_ref, dd1_ref), DILATIONS[1]), ((dag2_ref, dd2_ref), DILATIONS[2])):
            for r in range(d):
                for h in range(2):
                    hs = slice(h * PAIR_W, (h + 1) * PAIR_W)
                    refs[0][r, :, hs] = s_da[h, pl.ds(r, TMB // d, stride=d), :].astype(BF16)
                    refs[1][r, :, hs] = s_dd[h, pl.ds(r, TMB // d, stride=d), :]

        sc = sc_ref[...]
        dscale_ref[...] += jnp.sum(dp * mixed_ref[...].astype(F32), axis=0, keepdims=True)
        dmixed = (dp * sc).astype(BF16)
        dmixed_ref[...] = dmixed
        inv_cnt, _ = _pool_inv_count(i, TMB)
        dqp_ref[...] = (_dot_nt(dmixed, wbd_ref[...]) * inv_cnt).astype(BF16)

        @pl.when(i == n_tiles - 1)
        def _():
            for cpy in _chip_sum_copies(sum_refs, land_refs, send_sem, recv_sem):
                cpy.wait()

    row = lambda w: pl.BlockSpec((TMB, w), lambda i: (i, 0))
    grp_spec = lambda d: pl.BlockSpec((d, TMB // d, GROUP_W), lambda i: (0, i, 0))
    d0, d1, d2 = DILATIONS
    res = pl.pallas_call(
        body, grid=(n_tiles,), name="mixer_bwd",
        in_specs=[row(D_MODEL), row(GROUP_W), row(POOL_W), row(POOL_W), row(2 * D_MODEL),
                  _resident(w_out.shape), _resident(w_ao.shape), _resident(w_po.shape), _resident(wbd.shape),
                  _resident(scale.shape), _resident(head_ones.shape)] + [ANY] * n,
        out_specs=[row(D_MODEL), row(D_MODEL), pl.BlockSpec((2 * N_CHIPS, TMB, CHUNK), lambda i: (0, i, 0)),
                   grp_spec(d0), grp_spec(d1), grp_spec(d2), grp_spec(d0), grp_spec(d1), grp_spec(d2),
                   row(POOL_W), row(POOL_W), pl.BlockSpec((1, POOL_W), lambda i: (0, 0))] + [ANY] * n,
        out_shape=[SDS((S, D_MODEL), BF16), SDS((S, D_MODEL), BF16), SDS((2 * N_CHIPS, S, CHUNK), BF16)]
        + [SDS((d, S // d, GROUP_W), BF16) for d in DILATIONS]
        + [SDS((d, S // d, GROUP_W), F32) for d in DILATIONS]
        + [SDS((S, POOL_W), BF16), SDS((S, POOL_W), BF16), SDS((1, POOL_W), F32)]
        + [SDS(t.shape, t.dtype) for t in sums],
        scratch_shapes=[pltpu.VMEM((2, TMB, PAIR_W), F32), pltpu.VMEM((2, TMB, PAIR_W), F32),
                        pltpu.SemaphoreType.DMA((3 * n,)), pltpu.SemaphoreType.DMA((3 * n,))],
        compiler_params=_cp("arbitrary"),
    )(dh1b, a, p, mixed, gates, w_out, w_ao, w_po, wbd, scale, head_ones, *sums)
    return res[:12], res[12:]


def _attn_bwd(qkv, da, lt, dd, grp):
    d = DILATIONS[grp]
    L = qkv.shape[2]
    RR, RB, nb = _attn_tiles(grp, L)

    def body(q_ref, kc_ref, kp_ref, vc_ref, vp_ref, da_ref, lt_ref, dd_ref, dq_ref, dk_ref, dv_ref, dk_acc, dv_acc):
        i = pl.program_id(1)

        @pl.when(i == 0)
        def _():
            dk_acc[...] = jnp.zeros_like(dk_acc)
            dv_acc[...] = jnp.zeros_like(dv_acc)

        def compute(cur, prv):
            dk_acc[cur] = jnp.zeros((RR, RB * BAND, GROUP_W), F32)
            dv_acc[cur] = jnp.zeros((RR, RB * BAND, GROUP_W), F32)
            biases, col = _band_bias(grp, d)
            first_keys_ok = (col >= BAND) | (i > 0)
            is_a = lax.broadcasted_iota(jnp.int32, (BAND, PAIR_W), 1) < HEAD_W
            for rr in range(RR):
                for rb in range(RB):
                    rows = slice(rb * BAND, (rb + 1) * BAND)
                    for cp in range(2):
                        cs = slice(cp * PAIR_W, (cp + 1) * PAIR_W)
                        q2 = q_ref[rr, rows, cs]
                        da2 = da_ref[rr, rows, cs]
                        lt2 = lt_ref[rr, rows, cs]
                        dd2 = dd_ref[rr, rows, cs]
                        kcat = _kv_tile(kc_ref, kp_ref, rr, rb, cs)
                        vcat = _kv_tile(vc_ref, vp_ref, rr, rb, cs)
                        q2t = q2.astype(F32).T.astype(BF16)
                        da2t = da2.astype(F32).T.astype(BF16)
                        dqs, dkts, dvts, scores, dpvs = [], [], [], [], []
                        for h2 in range(2):
                            sel = is_a if h2 == 0 else jnp.logical_not(is_a)
                            b = biases[cp * 2 + h2]
                            if rb == 0:
                                b = jnp.where(first_keys_ok, b, NEG)
                            scores.append(_dot_nt(jnp.where(sel, q2, jnp.zeros_like(q2)), kcat) + b)
                            dpvs.append(_dot_nt(jnp.where(sel, da2, jnp.zeros_like(da2)), vcat))
                        for h2 in range(2):
                            lane0 = h2 * HEAD_W
                            p = jnp.exp(scores[h2] - lt2[:, lane0:lane0 + 1])
                            ds = (p * (dpvs[h2] - dd2[:, lane0:lane0 + 1])).astype(BF16)
                            dqs.append(_dot(ds, kcat))
                            dkts.append(_dot(q2t[lane0:lane0 + HEAD_W, :], ds))
                            dvts.append(_dot(da2t[lane0:lane0 + HEAD_W, :], p.astype(BF16)))
                        dq_ref[rr, rows, cs] = (jnp.where(is_a, dqs[0], dqs[1]) * 0.125).astype(BF16)
                        dkc = jnp.concatenate(dkts, axis=0).T
                        dvc = jnp.concatenate(dvts, axis=0).T
                        if rb == 0:
                            last = slice((RB - 1) * BAND, RB * BAND)
                            dk_acc[prv, rr, last, cs] += dkc[0:BAND]
                            dv_acc[prv, rr, last, cs] += dvc[0:BAND]
                            dk_acc[cur, rr, 0:BAND, cs] += dkc[BAND:]
                            dv_acc[cur, rr, 0:BAND, cs] += dvc[BAND:]
                        else:
                            both = slice((rb - 1) * BAND, (rb + 1) * BAND)
                            dk_acc[cur, rr, both, cs] += dkc
                            dv_acc[cur, rr, both, cs] += dvc

        def flush(prv):
            dk_ref[...] = dk_acc[prv].astype(BF16)
            dv_ref[...] = dv_acc[prv].astype(BF16)

        for parity in (0, 1):
            on = (i % 2) == parity
            pl.when(on & (i < nb))(functools.partial(compute, parity, 1 - parity))
            pl.when(on & (i > 0))(functools.partial(flush, 1 - parity))

    qi = lambda i: jnp.minimum(i, nb - 1)
    cur_w = lambda w: pl.BlockSpec((None, RR, RB * BAND, GROUP_W), lambda j, i: (w, j, qi(i), 0))
    prev_w = lambda w: pl.BlockSpec((None, RR, BAND, GROUP_W),
                                    lambda j, i: (w, j, jnp.maximum(qi(i) * RB - 1, 0), 0))
    blk = pl.BlockSpec((RR, RB * BAND, GROUP_W), lambda j, i: (j, qi(i), 0))
    late = pl.BlockSpec((RR, RB * BAND, GROUP_W), lambda j, i: (j, jnp.maximum(i - 1, 0), 0))
    return pl.pallas_call(
        body, grid=(d // RR, nb + 1), name=f"attn_bwd_g{grp}",
        in_specs=[cur_w(0), cur_w(1), prev_w(1), cur_w(2), prev_w(2), blk, blk, blk],
        out_specs=[blk, late, late],
        out_shape=[SDS((d, L, GROUP_W), BF16)] * 3,
        scratch_shapes=[pltpu.VMEM((2, RR, RB * BAND, GROUP_W), F32), pltpu.VMEM((2, RR, RB * BAND, GROUP_W), F32)],
        compiler_params=_cp("parallel", "arbitrary"),
    )(qkv, qkv, qkv, qkv, qkv, da, lt, dd)


def _dz_assemble(dqkv, dqp):
    S = dqp.shape[0]
    n_tiles = S // TMB

    def body(*refs):
        dqkv_refs = refs[0:9]
        dqp_ref, halo_ref = refs[9:11]
        dz_ref, s_ref, ext_ref = refs[11:]
        i = pl.program_id(0)

        for grp in range(3):
            for which in range(3):
                n = which * 3 + grp
                ref = dqkv_refs[grp * 3 + which]
                if DILATIONS[grp] == 1:
                    dz_ref[n] = ref[0]
                else:
                    _interleave_load(ref, (), s_ref, DILATIONS[grp], TMB)
                    for h in range(2):
                        dz_ref[n, :, h * PAIR_W:(h + 1) * PAIR_W] = s_ref[h].astype(BF16)

        dqp = dqp_ref[...].astype(F32)
        ext_ref[0:TMB, :] = dqp
        ext_ref[TMB:, :] = jnp.where(i < n_tiles - 1, halo_ref[...].astype(F32), 0.0)
        sums = []
        acc = ext_ref[...]
        for k in (1, 2, 4, 8):
            acc = acc + pltpu.roll(acc, TMB + POOL_HALO - k, 0)
            sums.append(acc[0:TMB, :])
        inv_cnt, col = _pool_inv_count(i, TMB)
        dpz = _pool_column_select(col, sums) - dqp / inv_cnt
        for t in range(3):
            dz_ref[9 + t] = dpz[:, t * CHUNK:(t + 1) * CHUNK].astype(BF16)

    row = lambda w: pl.BlockSpec((TMB, w), lambda i: (i, 0))
    grp_spec = lambda d: pl.BlockSpec((d, TMB // d, GROUP_W), lambda i: (0, i, 0))
    halo = pl.BlockSpec((POOL_HALO, POOL_W),
                        lambda i: (jnp.minimum((i + 1) * (TMB // POOL_HALO), S // POOL_HALO - 1), 0))
    flat = [t for grp in range(3) for t in dqkv[grp]]
    return pl.pallas_call(
        body, grid=(n_tiles,), name="dz_assemble",
        in_specs=[grp_spec(DILATIONS[grp]) for grp in range(3) for _ in range(3)] + [row(POOL_W), halo],
        out_specs=pl.BlockSpec((N_DZ_CHUNKS, TMB, CHUNK), lambda i: (0, i, 0)),
        out_shape=SDS((N_DZ_CHUNKS, S, CHUNK), BF16),
        scratch_shapes=[pltpu.VMEM((2, TMB, PAIR_W), F32), pltpu.VMEM((TMB + POOL_HALO, POOL_W), F32)],
        compiler_params=_cp("parallel"),
    )(*flat, dqp, dqp)


def _inproj_dx(dz, dgates, dh1, x, g, w_in, sums):
    S = x.shape[0]
    n_tiles = S // TM
    n = len(sums)

    def body(*refs):
        dz_ref, dgate_ref, dh1_ref, x_ref, g_ref, w_ref = refs[0:6]
        sum_refs = refs[6:6 + n]
        dx_ref, dg_ref = refs[6 + n:8 + n]
        land_refs = refs[8 + n:8 + 2 * n]
        sems = refs[8 + 2 * n:]
        i = pl.program_id(0)

        def copies():
            return _chip_sum_copies(sum_refs, land_refs, *sems)

        @pl.when(i == 0)
        def _():
            dg_ref[...] = jnp.zeros_like(dg_ref)
            for cpy in copies():
                cpy.start()

        du = jnp.zeros((TM, D_MODEL), F32)
        for k in range(N_CHUNKS):
            dzk = dz_ref[k] if k < N_DZ_CHUNKS else dgate_ref[k - N_DZ_CHUNKS]
            du = du + _dot_nt(dzk, _w_in_chunk(w_ref, k))
        gv = g_ref[...]
        _, xh, r = _rms_fwd(x_ref[...], gv)
        dg_ref[...] += jnp.sum(du * xh, axis=0, keepdims=True)
        dx_ref[...] = dh1_ref[...] + _rms_bwd(du, xh, r, gv)

        @pl.when(i == n_tiles - 1)
        def _():
            for cpy in copies():
                cpy.wait()

    row = lambda w: pl.BlockSpec((TM, w), lambda i: (i, 0))
    res = pl.pallas_call(
        body, grid=(n_tiles,), name="inproj_dx",
        in_specs=[pl.BlockSpec((N_DZ_CHUNKS, TM, CHUNK), lambda i: (0, i, 0)),
                  pl.BlockSpec((N_CHUNKS - N_DZ_CHUNKS, TM, CHUNK), lambda i: (0, i, 0)),
                  row(D_MODEL), row(D_MODEL), _resident(g.shape), _resident(w_in.shape)] + [ANY] * n,
        out_specs=[row(D_MODEL), pl.BlockSpec((1, D_MODEL), lambda i: (0, 0))] + [ANY] * n,
        out_shape=[SDS((S, D_MODEL), F32), SDS((1, D_MODEL), F32)] + [SDS(t.shape, t.dtype) for t in sums],
        scratch_shapes=[pltpu.SemaphoreType.DMA((3 * n,)), pltpu.SemaphoreType.DMA((3 * n,))],
        compiler_params=_cp("arbitrary"),
    )(dz, dgates, dh1, x, g, w_in, *sums)
    return res[0], res[1], res[2:]


def _wgrad(a, b, name, *, out_shape, a_spec, b_spec, out_spec, grid, n_out_cols=None, fill=None, narrow=True):
    k_axis = len(grid) - 1
    n_k = grid[k_axis]
    n_out = 2 if narrow else 1

    def body(a_ref, b_ref, *rest):
        o_ref = rest[-n_out]

        @pl.when(pl.program_id(k_axis) == 0)
        def _():
            o_ref[...] = jnp.zeros_like(o_ref)

        at = a_ref[...]
        if n_out_cols is None:
            o_ref[...] += _dot_tn(at, b_ref[...])
        elif n_out_cols[0] == "lead_both":
            for t in range(b_ref.shape[0]):
                o_ref[t] += _dot_tn(at, b_ref[t])
        else:
            w = n_out_cols[1]
            for t in range(o_ref.shape[0]):
                o_ref[t] += _dot_tn(at, b_ref[:, t * w:(t + 1) * w])

        if narrow:
            @pl.when(pl.program_id(k_axis) == n_k - 1)
            def _():
                rest[-1][...] = o_ref[...].astype(BF16)

    sem = ("parallel",) * k_axis + ("arbitrary",)
    extra = [] if fill is None else list(fill) if narrow else [fill]
    shapes = [out_shape, SDS(out_shape.shape, BF16)] if narrow else out_shape
    return pl.pallas_call(body, grid=grid, name=name, in_specs=[a_spec, b_spec] + [ANY] * len(extra),
                          out_specs=[out_spec] * n_out if narrow else out_spec, out_shape=shapes,
                          input_output_aliases={2 + t: t for t in range(len(extra))},
                          compiler_params=_cp(*sem))(a, b, *extra)


def _wgrad_in(u, dz, dgates):
    bk = min(BK, u.shape[0])
    nk = u.shape[0] // bk
    g = WGRAD_IN_GROUP
    kw = dict(n_out_cols=("lead_both", CHUNK), a_spec=pl.BlockSpec((bk, D_MODEL), lambda j, k: (k, 0)),
              b_spec=pl.BlockSpec((g, bk, CHUNK), lambda j, k: (j, k, 0)),
              out_shape=SDS((N_CHUNKS, D_MODEL, CHUNK), F32))
    first = _wgrad(u, dz, "wgrad_in_qkvp", grid=(N_DZ_CHUNKS // g, nk),
                   out_spec=pl.BlockSpec((g, D_MODEL, CHUNK), lambda j, k: (j, 0, 0)), **kw)
    both = _wgrad(u, dgates, "wgrad_in_gates", grid=((N_CHUNKS - N_DZ_CHUNKS) // g, nk), fill=first,
                  out_spec=pl.BlockSpec((g, D_MODEL, CHUNK), lambda j, k: (N_DZ_CHUNKS // g + j, 0, 0)), **kw)
    return [t.reshape(N_CHIPS, CHUNKS_PER_SHARD * D_MODEL, CHUNK) for t in both]


def _wgrads_mixer(a, da1, p, dp1, merged, dh1b, pooled, dmixed):
    bk = min(BK, a.shape[0])
    nk = a.shape[0] // bk
    g_ao = _wgrad(
        a, da1, "wgrad_att_out", grid=(nk,), n_out_cols=("cols_b", CHUNK),
        a_spec=pl.BlockSpec((bk,GROUP_W), lambda k: (k, 0)),
        b_spec=pl.BlockSpec((bk,D_MODEL), lambda k: (k, 0)),
        out_spec=pl.BlockSpec((N_CHIPS, GROUP_W, CHUNK), lambda k: (0, 0, 0)),
        out_shape=SDS((N_CHIPS, GROUP_W, CHUNK), F32))
    g_po = _wgrad(
        p, dp1, "wgrad_pool_out", grid=(nk,), n_out_cols=("cols_b", CHUNK),
        a_spec=pl.BlockSpec((bk,POOL_W), lambda k: (k, 0)),
        b_spec=pl.BlockSpec((bk,D_MODEL), lambda k: (k, 0)),
        out_spec=pl.BlockSpec((N_CHIPS, POOL_W, CHUNK), lambda k: (0, 0, 0)),
        out_shape=SDS((N_CHIPS, POOL_W, CHUNK), F32))
    g_out = _wgrad(
        merged, dh1b, "wgrad_out", grid=(nk,),
        a_spec=pl.BlockSpec((bk,D_MODEL), lambda k: (k, 0)),
        b_spec=pl.BlockSpec((bk,D_MODEL), lambda k: (k, 0)),
        out_spec=pl.BlockSpec((D_MODEL, D_MODEL), lambda k: (0, 0)),
        out_shape=SDS((D_MODEL, D_MODEL), F32))
    g_bd = _wgrad(
        pooled, dmixed, "wgrad_pool_grp", grid=(nk,),
        a_spec=pl.BlockSpec((bk,POOL_W), lambda k: (k, 0)),
        b_spec=pl.BlockSpec((bk,POOL_W), lambda k: (k, 0)),
        out_spec=pl.BlockSpec((POOL_W, POOL_W), lambda k: (0, 0)),
        out_shape=SDS((POOL_W, POOL_W), F32), narrow=False)
    g_out = [t.reshape(N_CHIPS, D_MODEL // N_CHIPS, D_MODEL) for t in g_out]
    return [g_ao, g_po, g_out], g_bd


def _wgrads_mlp(m, dpre, hid, dh2b):
    bk = min(BK, m.shape[0])
    nk = m.shape[0] // bk
    g_mi = _wgrad(
        m, dpre, "wgrad_mlp_in", grid=(N_CHIPS, nk),
        a_spec=pl.BlockSpec((bk,D_MODEL), lambda c, k: (k, 0)),
        b_spec=pl.BlockSpec((bk,D_MODEL), lambda c, k: (k, c)),
        out_spec=pl.BlockSpec((None, D_MODEL, D_MODEL), lambda c, k: (c, 0, 0)),
        out_shape=SDS((N_CHIPS, D_MODEL, D_MODEL), F32))
    g_mo = _wgrad(
        hid, dh2b, "wgrad_mlp_out", grid=(N_CHIPS, nk),
        a_spec=pl.BlockSpec((bk,D_MODEL), lambda c, k: (k, c)),
        b_spec=pl.BlockSpec((bk,D_MODEL), lambda c, k: (k, 0)),
        out_spec=pl.BlockSpec((None, D_MODEL, D_MODEL), lambda c, k: (c, 0, 0)),
        out_shape=SDS((N_CHIPS, D_MODEL, D_MODEL), F32))
    return [g_mi, g_mo]


def _mesh_place():
    x, y, c = lax.axis_index("x"), lax.axis_index("y"), lax.axis_index("c")
    other_chips = [(x, 1 - y), (1 - x, y), (1 - x, 1 - y)]
    return x, y, c, other_chips


ANY = pl.BlockSpec(memory_space=pl.ANY)


def _weight_half_copies(shard_refs, buf_refs, rows, send_sem, recv_sem):
    x, y, c, chips = _mesh_place()
    me = 2 * x + y
    copies = []
    for w, r_full in enumerate(rows):
        rh = r_full // 2
        for r, (px, py) in enumerate(chips):
            k = w * 3 + r
            copies.append(pltpu.make_async_remote_copy(
                src_ref=shard_refs[w].at[pl.ds(c * rh, rh), :], dst_ref=buf_refs[w].at[me, pl.ds(c * rh, rh), :],
                send_sem=send_sem.at[k], recv_sem=recv_sem.at[k], device_id=(px, py, c), device_id_type=MESH))
    return copies


def _pair_forward_copies(buf_refs, rows, send_sem, recv_sem):
    x, y, c, chips = _mesh_place()
    out = []
    for w, r_full in enumerate(rows):
        rh = r_full // 2
        for r, (px, py) in enumerate(chips):
            k = w * 3 + r
            landed = buf_refs[w].at[2 * px + py, pl.ds(c * rh, rh), :]
            theirs = buf_refs[w].at[2 * px + py, pl.ds((1 - c) * rh, rh), :]
            mk = lambda ref: pltpu.make_async_remote_copy(
                src_ref=ref, dst_ref=ref, send_sem=send_sem.at[k], recv_sem=recv_sem.at[k],
                device_id=(x, y, 1 - c), device_id_type=MESH)
            out.append((mk(landed), mk(theirs)))
    return out


def _place_own(block, n_slots, slot):
    buf = lax.empty((n_slots,) + block.shape, block.dtype)
    return lax.dynamic_update_slice(buf, block[None], (slot,) + (0,) * block.ndim)


def _pair_forward(bufs, rows, name):
    n = len(bufs)

    def body(*refs):
        dst = refs[n:2 * n]
        send_sem, recv_sem = refs[2 * n:]
        fwds = _pair_forward_copies(dst, rows, send_sem, recv_sem)
        for fwd, _ in fwds:
            fwd.start()
        for fwd, landing in fwds:
            landing.wait_recv()
            fwd.wait_send()

    return pl.pallas_call(
        body, name=name,
        in_specs=[ANY] * n, out_specs=[ANY] * n,
        out_shape=[SDS(b.shape, b.dtype) for b in bufs],
        scratch_shapes=[pltpu.SemaphoreType.DMA((3 * n,))] * 2,
        input_output_aliases={w: w for w in range(n)},
    )(*bufs)


def _chip_sum_copies(src, dst, send_sem, recv_sem):
    x, y, c, chips = _mesh_place()
    copies = []
    for w in range(len(src)):
        for r, (px, py) in enumerate(chips):
            k = w * 3 + r
            copies.append(pltpu.make_async_remote_copy(
                src_ref=src[w].at[r + 1], dst_ref=dst[w].at[r + 1], send_sem=send_sem.at[k], recv_sem=recv_sem.at[k],
                device_id=(px, py, c), device_id_type=MESH))
    return copies


def _pair_exchange(grads):
    n = len(grads)

    def body(*refs):
        src, dst = refs[:n], refs[n:2 * n]
        send_sem, recv_sem = refs[2 * n:]
        x, y, c, _ = _mesh_place()
        copies = []
        for w in range(n):
            rh = grads[w].shape[1] // 2
            copies.append(pltpu.make_async_remote_copy(
                src_ref=src[w].at[:, pl.ds((1 - c) * rh, rh), :], dst_ref=dst[w],
                send_sem=send_sem.at[w], recv_sem=recv_sem.at[w],
                device_id=(x, y, 1 - c), device_id_type=MESH))
            copies[-1].start()
        for cpy in copies:
            cpy.wait()

    return pl.pallas_call(
        body, name="grad_pair_exchange",
        in_specs=[ANY] * n, out_specs=[ANY] * n,
        out_shape=[SDS((N_CHIPS, g.shape[1] // 2, g.shape[2]), g.dtype) for g in grads],
        scratch_shapes=[pltpu.SemaphoreType.DMA((n,)), pltpu.SemaphoreType.DMA((n,))],
    )(*grads)


def _pair_sum(place, grad, recv, name):
    _, R, C = grad.shape
    rh = R // 2
    br = _row_block(rh, max(256, ELEMENTWISE_BLOCK // C))
    nbh = rh // br

    def body(place_ref, g_ref, r_ref, own_ref, sums_ref):
        s = g_ref[...] + r_ref[...].astype(F32)

        @pl.when(pl.program_id(1) == 0)
        def _():
            own_ref[...] = s

        sums_ref[...] = s.astype(BF16)

    slot = lambda rel, pr: jnp.bitwise_xor(pr[0], rel)
    return pl.pallas_call(
        body, name=name,
        grid_spec=pltpu.PrefetchScalarGridSpec(
            num_scalar_prefetch=1, grid=(nbh, N_CHIPS),
            in_specs=[pl.BlockSpec((None, br, C), lambda i, rel, pr: (slot(rel, pr), pr[1] * nbh + i, 0)),
                      pl.BlockSpec((None, br, C), lambda i, rel, pr: (slot(rel, pr), i, 0))],
            out_specs=[pl.BlockSpec((br, C), lambda i, rel, pr: (i, 0)),
                       pl.BlockSpec((None, br, C), lambda i, rel, pr: (rel, i, 0))]),
        out_shape=[SDS((rh, C), F32), SDS((N_CHIPS, rh, C), BF16)],
        compiler_params=_cp("parallel", "arbitrary"),
    )(place, grad, recv)


def _chip_sum(place, own, recv, name):
    rh, C = own.shape
    br = _row_block(rh, max(256, ELEMENTWISE_BLOCK // C))
    nbh = rh // br

    def body(place_ref, own_ref, r_ref, o_ref):
        o_ref[...] = ((own_ref[...] + r_ref[1].astype(F32)) + r_ref[2].astype(F32)) + r_ref[3].astype(F32)

    return pl.pallas_call(
        body, name=name,
        grid_spec=pltpu.PrefetchScalarGridSpec(
            num_scalar_prefetch=1, grid=(nbh,),
            in_specs=[pl.BlockSpec((br, C), lambda i, pr: (i, 0)),
                      pl.BlockSpec((N_CHIPS, br, C), lambda i, pr: (0, i, 0))],
            out_specs=pl.BlockSpec((br, C), lambda i, pr: (pr[1] * nbh + i, 0))),
        out_shape=SDS((2 * rh, C), F32),
        compiler_params=_cp("parallel"),
    )(place, own, recv)


def _finish_exchange(grads, small_all):
    n = len(grads)

    def body(*refs):
        dst, all_ref = refs[n + 1:2 * n + 1], refs[2 * n + 1]
        send_sem, recv_sem, ssend_sem, srecv_sem = refs[2 * n + 2:]
        x, y, c, chips = _mesh_place()
        sib = (x, y, 1 - c)

        def pack(dev, k, to):
            slot = 4 * dev[0] + 2 * dev[1] + dev[2]
            return pltpu.make_async_remote_copy(
                src_ref=all_ref.at[slot], dst_ref=all_ref.at[slot], send_sem=ssend_sem.at[k],
                recv_sem=srecv_sem.at[k], device_id=to, device_id_type=MESH)

        pack_copies = [pack((x, y, c), 0, sib)] + [pack((x, y, c), 1 + r, (px, py, c))
                                                   for r, (px, py) in enumerate(chips)]
        for cpy in pack_copies:
            cpy.start()
        sends, landings = [], []
        for w in range(n):
            rh = grads[w].shape[0] // 2
            mk = lambda cc: pltpu.make_async_remote_copy(
                src_ref=dst[w].at[pl.ds(cc * rh, rh), :], dst_ref=dst[w].at[pl.ds(cc * rh, rh), :],
                send_sem=send_sem.at[w], recv_sem=recv_sem.at[w], device_id=(x, y, 1 - c), device_id_type=MESH)
            sends.append(mk(c))
            landings.append(mk(1 - c))
            sends[-1].start()
        for r, (px, py) in enumerate(chips):
            pack((px, py, c), 1 + r, (px, py, c)).wait_recv()
            pack_copies.append(pack((px, py, c), 4 + r, sib))
            pack_copies[-1].start()
        pack(sib, 0, sib).wait_recv()
        for r, (px, py) in enumerate(chips):
            pack((px, py, 1 - c), 4 + r, sib).wait_recv()
        for cpy in landings:
            cpy.wait_recv()
        for cpy in sends + pack_copies:
            cpy.wait_send()

    res = pl.pallas_call(
        body, name="grad_finish_exchange",
        in_specs=[ANY] * (n + 1), out_specs=[ANY] * (n + 1),
        out_shape=[SDS(g.shape, g.dtype) for g in grads] + [SDS(small_all.shape, small_all.dtype)],
        scratch_shapes=[pltpu.SemaphoreType.DMA((n,)), pltpu.SemaphoreType.DMA((n,)),
                        pltpu.SemaphoreType.DMA((N_DEV - 1,)), pltpu.SemaphoreType.DMA((N_DEV - 1,))],
        input_output_aliases={w: w for w in range(n + 1)},
    )(*grads, small_all)
    return res[:n], res[n]


def _adamw_math(w, g, m, v):
    m = ADAM_B1 * m + (1.0 - ADAM_B1) * g
    v = ADAM_B2 * v + (1.0 - ADAM_B2) * jnp.square(g)
    m_hat = m / (1.0 - ADAM_B1 ** ADAM_STEP)
    v_hat = v / (1.0 - ADAM_B2 ** ADAM_STEP)
    delta = -ADAM_LR * (m_hat / (jnp.sqrt(v_hat) + ADAM_EPS) + ADAM_WD * w)
    return delta, m, v


def _adamw(w, g, m, v, name):
    R, C = w.shape
    br = _row_block(R, 512)
    if g.ndim == 3:
        n_chunks, cw = g.shape[0], g.shape[2]
        g_spec = pl.BlockSpec((None, br, cw), lambda t, i: (t, i, 0))
    else:
        n_chunks, cw = 1, C
        g_spec = pl.BlockSpec((br, cw), lambda t, i: (i, t))

    def body(w_ref, g_ref, m_ref, v_ref, g_out_ref, d_ref, nm_ref, nv_ref):
        gv = g_ref[...]
        g_out_ref[...] = gv
        d_ref[...], nm_ref[...], nv_ref[...] = _adamw_math(w_ref[...], gv, m_ref[...], v_ref[...])

    spec = pl.BlockSpec((br, cw), lambda t, i: (i, t))
    return pl.pallas_call(
        body, grid=(n_chunks, R // br), name=name, in_specs=[spec, g_spec, spec, spec], out_specs=[spec] * 4,
        out_shape=[SDS((R, C), F32)] * 4, compiler_params=_cp("parallel", "parallel"),
    )(w, g, m, v)


def _small_sum_adamw(all_small, w, m, v):
    loss_row = PACK_ROWS - 8

    def body(all_ref, w_ref, m_ref, v_ref, g_ref, d_ref, nm_ref, nv_ref, loss_ref):
        g = all_ref[0]
        for k in range(1, N_DEV):
            g = g + all_ref[k]
        g_ref[...] = g
        d_ref[...], nm_ref[...], nv_ref[...] = _adamw_math(w_ref[...], g, m_ref[...], v_ref[...])
        total = jnp.sum(g[loss_row:loss_row + 1, :]) * (0.5 / D_MODEL)
        loss_ref[...] = jnp.full(loss_ref.shape, total, F32)

    full = lambda s: pl.BlockSpec(s, lambda i: (0,) * len(s))
    pack = (PACK_ROWS, D_MODEL)
    return pl.pallas_call(
        body, grid=(1,), name="small_sum_adamw",
        in_specs=[full((N_DEV,) + pack), full(pack), full(pack), full(pack)],
        out_specs=[full(pack)] * 4 + [full((8, 128))],
        out_shape=[SDS(pack, F32)] * 4 + [SDS((8, 128), F32)],
        compiler_params=_cp("arbitrary"),
    )(all_small, w, m, v)


def _pack_small(grp, scale, g_mix, g_mlp, g_f, loss_lanes):
    def part(vec):
        vec = vec.reshape(1, -1)
        return jnp.pad(vec, ((0, 7), (0, D_MODEL - vec.shape[1])))
    return jnp.concatenate([grp.reshape(-1, D_MODEL), part(scale), part(g_mix), part(g_mlp), part(g_f),
                            part(loss_lanes)], axis=0)


def _unpack_small(pack):
    n_grp = len(POOL_WINDOWS) * POOL_GROUP_W * POOL_GROUP_W // D_MODEL
    grp = pack[:n_grp].reshape(1, len(POOL_WINDOWS), POOL_GROUP_W, POOL_GROUP_W)
    scale = pack[n_grp, :POOL_W].reshape(1, POOL_W)
    g_mix = pack[n_grp + 8].reshape(1, D_MODEL)
    g_mlp = pack[n_grp + 16].reshape(1, D_MODEL)
    g_f = pack[n_grp + 24].reshape(D_MODEL)
    return grp, scale, g_mix, g_mlp, g_f


def _block_diag(grp):
    out = jnp.zeros((POOL_W, POOL_W), grp.dtype)
    for k in range(len(POOL_WINDOWS)):
        out = lax.dynamic_update_slice(out, grp[k], (k * POOL_GROUP_W, k * POOL_GROUP_W))
    return out


def kernel(x, norm_mix_g, w_in, w_att_out, w_pool_grp, pool_scale, w_pool_out, w_out, norm_mlp_g, w_mlp_in, w_mlp_out, norm_final_g, loss_target, m_norm_mix_g, m_w_in, m_w_att_out, m_w_pool_grp, m_pool_scale, m_w_pool_out, m_w_out, m_norm_mlp_g, m_w_mlp_in, m_w_mlp_out, m_norm_final_g, v_norm_mix_g, v_w_in, v_w_att_out, v_w_pool_grp, v_pool_scale, v_w_pool_out, v_w_out, v_norm_mlp_g, v_w_mlp_in, v_w_mlp_out, v_norm_final_g):
    S = x.shape[1]
    xs, target = x[0], loss_target[0]
    big = [w_in[0], w_att_out[0], w_pool_out[0], w_out[0], w_mlp_in[0], w_mlp_out[0]]
    big_m = [m_w_in[0], m_w_att_out[0], m_w_pool_out[0], m_w_out[0], m_w_mlp_in[0], m_w_mlp_out[0]]
    big_v = [v_w_in[0], v_w_att_out[0], v_w_pool_out[0], v_w_out[0], v_w_mlp_in[0], v_w_mlp_out[0]]

    chip = 2 * lax.axis_index("x") + lax.axis_index("y")
    core = lax.axis_index("c")
    place = jnp.stack([chip, core]).astype(jnp.int32)
    names = ("w_in", "w_att_out", "w_pool_out", "w_out", "w_mlp_in", "w_mlp_out")

    shards = [w.astype(BF16) for w in big]
    bufs = [_place_own(sh, N_CHIPS, chip) for sh in shards]
    wbd = _block_diag(w_pool_grp[0]).astype(BF16)
    g_final = norm_final_g.reshape(1, D_MODEL)
    lane = lax.broadcasted_iota(jnp.int32, (GROUP_W, GROUP_W), 0) // HEAD_W
    head_ones = (lane == lane.T).astype(BF16)

    u, z_own, landed_in = _norm_inproj_own(xs, norm_mix_g, shards[0], bufs[0])
    (wg_in,) = _pair_forward([landed_in], [shards[0].shape[0]], "w_in_pair_forward")
    (qkv0, qkv1, qkv2, pz, gates), (wg_ao, wg_po, wg_out) = _inproj_rest(u, z_own, wg_in, shards[1:4], bufs[1:4])
    wg_out = wg_out.reshape(D_MODEL, D_MODEL)
    qkv = (qkv0, qkv1, qkv2)
    att = [_attn_fwd(qkv[grp], grp) for grp in range(3)]
    (a, lt0, lt1, lt2, pooled, mixed, p, merged, h1, m), (wg_mi, wg_mo) = _mixer_out(
        [o for o, _ in att], [l for _, l in att], pz, gates, xs, wg_ao, wg_po, wbd, pool_scale, wg_out, norm_mlp_g,
        shards[4:], bufs[4:])
    hid, dh2, dh2b, loss_lanes, dg_final = _mlp_fwd_loss(m, h1, target, wg_mi, wg_mo, g_final)

    def pair_reduce(grads, grad_names):
        recv = _pair_exchange([narrow for _, narrow in grads])
        pair = [_pair_sum(place, g, r, f"pair_sum_{nm}") for (g, _), r, nm in zip(grads, recv, grad_names)]
        return [own for own, _ in pair], [s for _, s in pair]

    def chip_reduce(owns, landed_sums, grad_names):
        return [_chip_sum(place, own, r, f"chip_sum_{nm}") for own, r, nm in zip(owns, landed_sums, grad_names)]

    dpre, dh1, dh1b, dg_mlp = _mlp_bwd(dh2, dh2b, hid, h1, wg_mi, wg_mo, norm_mlp_g)
    own_mlp, sums_mlp = pair_reduce(_wgrads_mlp(m, dpre, hid, dh2b), names[4:])
    (da1, dp1, dgates, da0, dag1, dag2, dd0, dd1, dd2, dmixed, dqp, dscale), landed_mlp = _mixer_bwd(
        dh1b, a, p, mixed, gates, wg_out, wg_ao, wg_po, wbd, pool_scale, head_ones, sums_mlp)
    g_mi, g_mo = chip_reduce(own_mlp, landed_mlp, names[4:])
    grads_mixer, g_bd = _wgrads_mixer(a, da1, p, dp1, merged, dh1b, pooled, dmixed)
    dqkv = [_attn_bwd(qkv[grp], da_g, lt_g, dd_g, grp)
            for grp, (da_g, lt_g, dd_g) in enumerate(((da0, lt0, dd0), (dag1, lt1, dd1), (dag2, lt2, dd2)))]
    dz = _dz_assemble(dqkv, dqp)
    own_in, sums_in = pair_reduce([_wgrad_in(u, dz, dgates)] + grads_mixer, names[:4])
    dx, dg_mix, landed_in = _inproj_dx(dz, dgates, dh1, xs, norm_mix_g, wg_in, sums_in)
    g_in, g_ao, g_po, g_out = chip_reduce(own_in, landed_in, names[:4])

    g_grp = jnp.stack([g_bd[k * POOL_GROUP_W:(k + 1) * POOL_GROUP_W, k * POOL_GROUP_W:(k + 1) * POOL_GROUP_W]
                       for k in range(len(POOL_WINDOWS))])
    small = _pack_small(g_grp, dscale, dg_mix, dg_mlp, dg_final, loss_lanes)
    full, small_all = _finish_exchange([g_in, g_ao, g_po, g_out, g_mi, g_mo],
                                       _place_own(small, N_DEV, 2 * chip + core))

    zero = jnp.zeros((D_MODEL,), F32)
    small_w = _pack_small(w_pool_grp[0], pool_scale, norm_mix_g, norm_mlp_g, norm_final_g, zero)
    small_m = _pack_small(m_w_pool_grp[0], m_pool_scale, m_norm_mix_g, m_norm_mlp_g, m_norm_final_g, zero)
    small_v = _pack_small(v_w_pool_grp[0], v_pool_scale, v_norm_mix_g, v_norm_mlp_g, v_norm_final_g, zero)
    sg, sd, sm, sv, loss_tile = _small_sum_adamw(small_all, small_w, small_m, small_v)
    full = [full[0].reshape(CHUNKS_PER_SHARD, D_MODEL, CHUNK)] + list(full[1:])
    upd = [_adamw(w, g, mm, vv, f"adamw_{nm}") for w, g, mm, vv, nm in zip(big, full, big_m, big_v, names)]

    def ordered(small_pack, bigs):
        grp, scale, g_mix, g_mlp, g_f = _unpack_small(small_pack)
        b_in, b_ao, b_po, b_out, b_mi, b_mo = [b[None] for b in bigs]
        return (g_mix, b_in, b_ao, grp, scale, b_po, b_out, g_mlp, b_mi, b_mo, g_f)

    return (loss_tile[0, 0], dx[None],
            *ordered(sg, [t[0] for t in upd]),
            *ordered(sd, [t[1] for t in upd]),
            *ordered(sm, [t[2] for t in upd]),
            *ordered(sv, [t[3] for t in upd]))
```

```python
import functools

import jax
import jax.numpy as jnp
from jax import lax
from jax.experimental import pallas as pl
from jax.experimental.pallas import tpu as pltpu

F32 = jnp.float32
BF16 = jnp.bfloat16
SDS = jax.ShapeDtypeStruct
MESH = pl.DeviceIdType.MESH

D_MODEL = 1024
D_FF = 4096
N_CHIPS = 4
N_DEV = 8
DILATIONS = (1, 4, 16)
BAND = 128
GROUP_W = 256
PAIR_W = 128
HEAD_W = 64
STAT_W = 128
STAT_HEAD_W = 32
POOL_W = 768
POOL_GROUP_W = 192
POOL_WINDOWS = (2, 4, 8, 16)
POOL_HALO = 16
N_IN = 5120
CHUNK = 256
N_CHUNKS = N_IN // CHUNK
N_DZ_CHUNKS = 12
CHUNKS_PER_SHARD = 5
WGRAD_IN_GROUP = 4
NORM_EPS = 1e-6
ALIBI_MAX_BIAS = 8.0
N_HEADS = 12
NEG = -1e30

ADAM_LR, ADAM_B1, ADAM_B2, ADAM_EPS, ADAM_WD, ADAM_STEP = 0.001, 0.9, 0.999, 1e-08, 0.01, 10

TM = 512
TMB = 512
ATT_TILE = ((1, 4), (4, 1), (4, 1))
BK = 4096
ELEMENTWISE_BLOCK = 1 << 20
VMEM_LIMIT = 56 * 1024 * 1024
PACK_ROWS = 184

NT = (((1,), (1,)), ((), ()))
TN = (((0,), (0,)), ((), ()))


def _cp(*sem):
    return pltpu.CompilerParams(dimension_semantics=sem, vmem_limit_bytes=VMEM_LIMIT)


def _resident(shape):
    nd = len(shape)
    return pl.BlockSpec(shape, lambda *_: (0,) * nd, pipeline_mode=pl.Buffered(1))


def _row_block(rows, cap=256):
    return max(b for b in range(16, min(rows, cap) + 1, 16) if rows % b == 0)


def _dot(a, b):
    return jnp.dot(a, b, preferred_element_type=F32)


def _dot_nt(a, b):
    return lax.dot_general(a, b, NT, preferred_element_type=F32)


def _dot_tn(a, b):
    return lax.dot_general(a, b, TN, preferred_element_type=F32)


def _w_in_chunk(w_ref, n):
    return w_ref[n // CHUNKS_PER_SHARD, :, (n % CHUNKS_PER_SHARD) * CHUNK:(n % CHUNKS_PER_SHARD + 1) * CHUNK]


def _sigmoid(x):
    return 0.5 * jnp.tanh(0.5 * x.astype(F32)) + 0.5


def _rms_fwd(x, g):
    r = lax.rsqrt(jnp.mean(x * x, axis=-1, keepdims=True) + NORM_EPS)
    xh = x * r
    return xh * g, xh, r


def _rms_bwd(dy, xh, r, g):
    dxh = dy * g
    return r * (dxh - xh * jnp.mean(dxh * xh, axis=-1, keepdims=True))


def _per_head_lanes(cols):
    rows = cols[0].shape[0]
    lane = lax.broadcasted_iota(jnp.int32, (rows, STAT_W), 1)
    out = cols[3]
    for h in (2, 1, 0):
        out = jnp.where(lane < (h + 1) * STAT_HEAD_W, cols[h], out)
    return out


def _head_col(stat, h):
    return stat[:, h * STAT_HEAD_W:h * STAT_HEAD_W + 1]


def _stat_matrices():
    s = lax.broadcasted_iota(jnp.int32, (STAT_W, GROUP_W), 0)
    c = lax.broadcasted_iota(jnp.int32, (STAT_W, GROUP_W), 1)
    expand = (s == (c // HEAD_W) * STAT_HEAD_W).astype(BF16)
    reduce = (s // STAT_HEAD_W == c // HEAD_W).astype(BF16).T
    return expand, reduce


def _dot_split(x, m):
    hi = x.astype(BF16)
    lo = (x - hi.astype(F32)).astype(BF16)
    return _dot(hi, m) + _dot(lo, m)


def _deinterleave_store(val, s_ref, out_ref, lead, d, rows, dtype):
    if d == 1:
        out_ref[lead + (0,)] = val.astype(dtype)
        return
    for h in range(2):
        s_ref[h] = val[:, h * PAIR_W:(h + 1) * PAIR_W]
    for r in range(d):
        for h in range(2):
            out_ref[lead + (r, slice(None), slice(h * PAIR_W, (h + 1) * PAIR_W))] = (
                s_ref[h, pl.ds(r, rows // d, stride=d), :].astype(dtype))


def _interleave_load(in_ref, lead, s_ref, d, rows):
    for r in range(d):
        for h in range(2):
            s_ref[h, pl.ds(r, rows // d, stride=d), :] = (
                in_ref[lead + (r, slice(None), slice(h * PAIR_W, (h + 1) * PAIR_W))].astype(F32))


def _norm_inproj_own(x, g, w_own, buf):
    S = x.shape[0]
    n_tiles = S // TM

    def body(x_ref, g_ref, w_ref, shard_ref, buf_in, u_ref, z_ref, buf_ref, send_sem, recv_sem):
        i = pl.program_id(0)

        def copies():
            return _weight_half_copies([shard_ref], [buf_ref], [w_own.shape[0]], send_sem, recv_sem)

        @pl.when(i == 0)
        def _():
            for cpy in copies():
                cpy.start()

        u = _rms_fwd(x_ref[...], g_ref[...])[0].astype(BF16)
        u_ref[...] = u
        for t in range(CHUNKS_PER_SHARD):
            z_ref[t] = _dot(u, w_ref[:, t * CHUNK:(t + 1) * CHUNK])

        @pl.when(i == n_tiles - 1)
        def _():
            for cpy in copies():
                cpy.wait()

    row = lambda w: pl.BlockSpec((TM, w), lambda i: (i, 0))
    return pl.pallas_call(
        body, grid=(n_tiles,), name="norm_inproj_own",
        in_specs=[row(D_MODEL), _resident((1, D_MODEL)), _resident(w_own.shape), ANY, ANY],
        out_specs=[row(D_MODEL), pl.BlockSpec((CHUNKS_PER_SHARD, TM, CHUNK), lambda i: (0, i, 0)), ANY],
        out_shape=[SDS((S, D_MODEL), BF16), SDS((CHUNKS_PER_SHARD, S, CHUNK), F32), SDS(buf.shape, buf.dtype)],
        scratch_shapes=[pltpu.SemaphoreType.DMA((3,)), pltpu.SemaphoreType.DMA((3,))],
        input_output_aliases={4: 2},
        compiler_params=_cp("arbitrary"),
    )(x, g, w_own, w_own, buf)


def _hosted_allgather(i, n_steps, shard_refs, buf_refs, rows, sems):
    send_sem, recv_sem, fsend_sem, frecv_sem = sems
    ici = lambda: _weight_half_copies(shard_refs, buf_refs, rows, send_sem, recv_sem)
    forward = lambda: _pair_forward_copies(buf_refs, rows, fsend_sem, frecv_sem)

    def begin():
        @pl.when(i == 0)
        def _():
            for cpy in ici():
                cpy.start()

        @pl.when(i == n_steps // 2)
        def _():
            for cpy, (fwd, _) in zip(ici(), forward()):
                cpy.wait_recv()
                fwd.start()

    def end():
        @pl.when(i == n_steps - 1)
        def _():
            for cpy, (fwd, landing) in zip(ici(), forward()):
                landing.wait_recv()
                fwd.wait_send()
                cpy.wait_send()

    return begin, end


def _inproj_rest(u, z_own, w_in, shards, bufs):
    S = u.shape[0]
    n_tiles = S // TM
    n = len(shards)

    def body(*refs):
        u_ref, zown_ref, w_ref = refs[0:3]
        shard_refs = refs[3:3 + n]
        q0_ref, q1_ref, q2_ref, pz_ref, gate_ref = refs[3 + 2 * n:8 + 2 * n]
        buf_refs = refs[8 + 2 * n:8 + 3 * n]
        s_ref = refs[8 + 3 * n]
        i = pl.program_id(0)
        chip = 2 * lax.axis_index("x") + lax.axis_index("y")
        begin, end = _hosted_allgather(i, n_tiles, shard_refs, buf_refs, [sh.shape[0] for sh in shards],
                                       refs[9 + 3 * n:])
        begin()

        u = u_ref[...]
        qkv_refs = (q0_ref, q1_ref, q2_ref)

        def emit(k, zc):
            if k < 9:
                which, grp = k // 3, k % 3
                if which == 0:
                    zc = zc * 0.125
                _deinterleave_store(zc, s_ref, qkv_refs[grp], (which,), DILATIONS[grp], TM, BF16)
            elif k < N_DZ_CHUNKS:
                pz_ref[:, (k - 9) * CHUNK:(k - 8) * CHUNK] = zc
            else:
                gate_ref[:, (k - N_DZ_CHUNKS) * CHUNK:(k - N_DZ_CHUNKS + 1) * CHUNK] = zc.astype(BF16)

        def all_chunks(own_shard):
            for k in range(N_CHUNKS):
                if k // CHUNKS_PER_SHARD == own_shard:
                    emit(k, zown_ref[k % CHUNKS_PER_SHARD])
                else:
                    emit(k, _dot(u, _w_in_chunk(w_ref, k)))

        for shard in range(N_CHIPS):
            pl.when(chip == shard)(functools.partial(all_chunks, shard))
        end()

    row = lambda w: pl.BlockSpec((TM, w), lambda i: (i, 0))
    res = pl.pallas_call(
        body, grid=(n_tiles,), name="inproj_rest",
        in_specs=[row(D_MODEL), pl.BlockSpec((CHUNKS_PER_SHARD, TM, CHUNK), lambda i: (0, i, 0)),
                  _resident(w_in.shape)] + [ANY] * (2 * n),
        out_specs=[pl.BlockSpec((3, d, TM // d, GROUP_W), lambda i: (0, 0, i, 0)) for d in DILATIONS]
        + [row(POOL_W), row(2 * D_MODEL)] + [ANY] * n,
        out_shape=[SDS((3, d, S // d, GROUP_W), BF16) for d in DILATIONS]
        + [SDS((S, POOL_W), F32), SDS((S, 2 * D_MODEL), BF16)] + [SDS(b.shape, b.dtype) for b in bufs],
        scratch_shapes=[pltpu.VMEM((2, TM, PAIR_W), F32)] + [pltpu.SemaphoreType.DMA((3 * n,))] * 4,
        input_output_aliases={3 + n + w: 5 + w for w in range(n)},
        compiler_params=_cp("arbitrary"),
    )(u, z_own, w_in, *shards, *bufs)
    return res[:5], res[5:]


def _band_bias(grp, d):
    row = lax.broadcasted_iota(jnp.int32, (BAND, 2 * BAND), 0)
    col = lax.broadcasted_iota(jnp.int32, (BAND, 2 * BAND), 1)
    steps = BAND + row - col
    valid = (steps >= 0) & (steps <= BAND)
    stepsf = (steps * d).astype(F32)
    biases = []
    for hh in range(4):
        slope = 2.0 ** (-ALIBI_MAX_BIAS * (grp * 4 + hh + 1) / N_HEADS)
        biases.append(jnp.where(valid, -slope * stepsf, NEG))
    return biases, col


def _attn_tiles(grp, L):
    rr, rb = ATT_TILE[grp]
    rb = min(rb, L // BAND)
    return rr, rb, L // (rb * BAND)


def _kv_tile(cur_ref, prev_ref, rr, rb, cs):
    if rb == 0:
        return jnp.concatenate([prev_ref[rr, :, cs], cur_ref[rr, 0:BAND, cs]], axis=0)
    return cur_ref[rr, (rb - 1) * BAND:(rb + 1) * BAND, cs]


def _attn_fwd(qkv, grp):
    d = DILATIONS[grp]
    L = qkv.shape[2]
    RR, RB, nb = _attn_tiles(grp, L)

    def body(q_ref, kc_ref, kp_ref, vc_ref, vp_ref, o_ref, lse_ref):
        i = pl.program_id(0)
        biases, col = _band_bias(grp, d)
        first_keys_ok = (col >= BAND) | (i > 0)
        is_a = lax.broadcasted_iota(jnp.int32, (BAND, PAIR_W), 1) < HEAD_W
        heads = [(rr, rb, cp, h2) for rr in range(RR) for rb in range(RB) for cp in range(2) for h2 in range(2)]

        def tile(head):
            rr, rb, cp, _ = head
            return rr, rb, slice(rb * BAND, (rb + 1) * BAND), slice(cp * PAIR_W, (cp + 1) * PAIR_W)

        def scores(head):
            rr, rb, rows, cs = tile(head)
            q2 = q_ref[rr, rows, cs]
            b = biases[head[2] * 2 + head[3]]
            if rb == 0:
                b = jnp.where(first_keys_ok, b, NEG)
            sel = is_a if head[3] == 0 else jnp.logical_not(is_a)
            return _dot_nt(jnp.where(sel, q2, jnp.zeros_like(q2)), _kv_tile(kc_ref, kp_ref, rr, rb, cs)) + b

        s_next = scores(heads[0])
        outs, lses = {}, {}
        for idx, head in enumerate(heads):
            s = s_next
            if idx + 1 < len(heads):
                s_next = scores(heads[idx + 1])
            rr, rb, rows, cs = tile(head)
            m = jnp.max(s, axis=-1, keepdims=True)
            p = jnp.exp(s - m)
            l = jnp.sum(p, axis=-1, keepdims=True)
            outs[head[3]] = _dot(p.astype(BF16), _kv_tile(vc_ref, vp_ref, rr, rb, cs)) * (1.0 / l)
            lses[head[2] * 2 + head[3]] = m + jnp.log(l)
            if head[3] == 1:
                o_ref[rr, rows, cs] = jnp.where(is_a, outs[0], outs[1]).astype(BF16)
            if head[2] == 1 and head[3] == 1:
                lse_ref[rr, rows, :] = _per_head_lanes(lses)

    cur = lambda w: pl.BlockSpec((None, RR, RB * BAND, GROUP_W), lambda i, j: (w, j, i, 0))
    prev = lambda w: pl.BlockSpec((None, RR, BAND, GROUP_W), lambda i, j: (w, j, jnp.maximum(i * RB - 1, 0), 0))
    return pl.pallas_call(
        body, grid=(nb, d // RR), name=f"attn_fwd_g{grp}",
        in_specs=[cur(0), cur(1), prev(1), cur(2), prev(2)],
        out_specs=[pl.BlockSpec((RR, RB * BAND, GROUP_W), lambda i, j: (j, i, 0)),
                   pl.BlockSpec((RR, RB * BAND, STAT_W), lambda i, j: (j, i, 0))],
        out_shape=[SDS((d, L, GROUP_W), BF16), SDS((d, L, STAT_W), F32)],
        compiler_params=_cp("parallel", "parallel"),
    )(qkv, qkv, qkv, qkv, qkv)


def _pool_column_select(col, vals):
    return jnp.where(col < POOL_GROUP_W, vals[0],
                     jnp.where(col < 2 * POOL_GROUP_W, vals[1],
                               jnp.where(col < 3 * POOL_GROUP_W, vals[2], vals[3])))


def _pool_inv_count(i, rows):
    t = i * rows + lax.broadcasted_iota(jnp.int32, (rows, POOL_W), 0)
    col = lax.broadcasted_iota(jnp.int32, (rows, POOL_W), 1)
    win = _pool_column_select(col, POOL_WINDOWS)
    return 1.0 / jnp.minimum(t + 1, win).astype(F32), col


def _mixer_out(outs, lses, pz, gates, x, w_ao, w_po, wbd, scale, w_out, g_mlp, expand, shards, bufs):
    S = x.shape[0]
    n_tiles = S // TMB
    n = len(shards)

    def body(*refs):
        (o0_ref, l0_ref, o1_ref, l1_ref, o2_ref, l2_ref, pz_ref, halo_ref, gate_ref, x_ref,
         wao_ref, wpo_ref, wbd_ref, sc_ref, wout_ref, g_ref, expand_ref) = refs[0:17]
        shard_refs = refs[17:17 + n]
        (a_ref, lt0_ref, lt1_ref, lt2_ref, pooled_ref, mixed_ref, p_ref, merged_ref, h1_ref,
         m_ref) = refs[17 + 2 * n:27 + 2 * n]
        buf_refs = refs[27 + 2 * n:27 + 3 * n]
        so1, sl1, so2, sl2, slt, ext_ref = refs[27 + 3 * n:33 + 3 * n]
        i = pl.program_id(0)
        begin, end = _hosted_allgather(i, n_tiles, shard_refs, buf_refs, [sh.shape[0] for sh in shards],
                                       refs[33 + 3 * n:])
        begin()
        _interleave_load(o1_ref, (), so1, DILATIONS[1], TMB)
        _interleave_load(o2_ref, (), so2, DILATIONS[2], TMB)
        for ref, sref, d in ((l1_ref, sl1, DILATIONS[1]), (l2_ref, sl2, DILATIONS[2])):
            for r in range(d):
                sref[0, pl.ds(r, TMB // d, stride=d), :] = ref[r]
        l0, l1, l2 = l0_ref[0], sl1[0], sl2[0]
        mx = jnp.maximum(jnp.maximum(l0, l1), l2)
        e0, e1, e2 = jnp.exp(l0 - mx), jnp.exp(l1 - mx), jnp.exp(l2 - mx)
        den = e0 + e1 + e2
        inv = 1.0 / den
        slt[0] = mx + jnp.log(den)
        w0, w1, w2 = [_dot_split(e * inv, expand_ref[...]) for e in (e0, e1, e2)]
        for h in range(2):
            hs = slice(h * PAIR_W, (h + 1) * PAIR_W)
            a_ref[:, hs] = (w0[:, hs] * o0_ref[0, :, hs].astype(F32) + w1[:, hs] * so1[h]
                            + w2[:, hs] * so2[h]).astype(BF16)
        lt0_ref[0] = slt[0]
        for ref, d in ((lt1_ref, DILATIONS[1]), (lt2_ref, DILATIONS[2])):
            for r in range(d):
                ref[r] = slt[0, pl.ds(r, TMB // d, stride=d), :]

        pz_t = pz_ref[...]
        ext_ref[0:POOL_HALO, :] = jnp.where(i > 0, halo_ref[...], 0.0)
        ext_ref[POOL_HALO:, :] = pz_t
        sums = []
        acc = ext_ref[...]
        for k in (1, 2, 4, 8):
            acc = acc + pltpu.roll(acc, k, 0)
            sums.append(acc[POOL_HALO:, :])
        inv_cnt, col = _pool_inv_count(i, TMB)
        pooled = (_pool_column_select(col, sums) * inv_cnt - pz_t).astype(BF16)
        pooled_ref[...] = pooled
        mixed = _dot(pooled, wbd_ref[...])
        mixed_ref[...] = mixed.astype(BF16)
        p = (mixed * sc_ref[...]).astype(BF16)
        p_ref[...] = p

        a = a_ref[...]
        for j in range(N_CHIPS):
            js = slice(j * CHUNK, (j + 1) * CHUNK)
            ga = gate_ref[:, js]
            gp = gate_ref[:, D_MODEL + j * CHUNK:D_MODEL + (j + 1) * CHUNK]
            mj = _sigmoid(ga) * _dot(a, wao_ref[j]) + _sigmoid(gp) * _dot(p, wpo_ref[j])
            merged_ref[:, js] = mj.astype(BF16)
        h1 = x_ref[...] + _dot(merged_ref[...], wout_ref[...])
        h1_ref[...] = h1
        m_ref[...] = _rms_fwd(h1, g_ref[...])[0].astype(BF16)
        end()

    row = lambda w: pl.BlockSpec((TMB, w), lambda i: (i, 0))
    grp_spec = lambda d: pl.BlockSpec((d, TMB // d, GROUP_W), lambda i: (0, i, 0))
    stat_spec = lambda d: pl.BlockSpec((d, TMB // d, STAT_W), lambda i: (0, i, 0))
    halo = pl.BlockSpec((POOL_HALO, POOL_W), lambda i: (jnp.maximum(i * (TMB // POOL_HALO) - 1, 0), 0))
    d0, d1, d2 = DILATIONS
    pair_scratch = pltpu.VMEM((2, TMB, PAIR_W), F32)
    stat_scratch = pltpu.VMEM((1, TMB, STAT_W), F32)
    res = pl.pallas_call(
        body, grid=(n_tiles,), name="mixer_out",
        in_specs=[grp_spec(d0), stat_spec(d0), grp_spec(d1), stat_spec(d1), grp_spec(d2), stat_spec(d2),
                  row(POOL_W), halo, row(2 * D_MODEL), row(D_MODEL),
                  _resident(w_ao.shape), _resident(w_po.shape), _resident(wbd.shape), _resident(scale.shape),
                  _resident(w_out.shape), _resident(g_mlp.shape), _resident(expand.shape)] + [ANY] * (2 * n),
        out_specs=[row(GROUP_W), stat_spec(d0), stat_spec(d1), stat_spec(d2),
                   row(POOL_W), row(POOL_W), row(POOL_W), row(D_MODEL), row(D_MODEL), row(D_MODEL)] + [ANY] * n,
        out_shape=[SDS((S, GROUP_W), BF16)] + [SDS((d, S // d, STAT_W), F32) for d in DILATIONS]
        + [SDS((S, POOL_W), BF16), SDS((S, POOL_W), BF16), SDS((S, POOL_W), BF16),
           SDS((S, D_MODEL), BF16), SDS((S, D_MODEL), F32), SDS((S, D_MODEL), BF16)]
        + [SDS(b.shape, b.dtype) for b in bufs],
        scratch_shapes=[pair_scratch, stat_scratch, pair_scratch, stat_scratch, stat_scratch,
                        pltpu.VMEM((TMB + POOL_HALO, POOL_W), F32)] + [pltpu.SemaphoreType.DMA((3 * n,))] * 4,
        input_output_aliases={17 + n + w: 10 + w for w in range(n)},
        compiler_params=_cp("arbitrary"),
    )(outs[0], lses[0], outs[1], lses[1], outs[2], lses[2], pz, pz, gates, x,
      w_ao, w_po, wbd, scale, w_out, g_mlp, expand, *shards, *bufs)
    return res[:10], res[10:]


def _mlp_fwd_loss(m, h1, target, w_mi, w_mo, g_f):
    S = m.shape[0]

    def body(m_ref, h1_ref, t_ref, wmi_ref, wmo_ref, g_ref, hid_ref, dh2_ref, dh2b_ref, loss_ref, dg_ref):
        @pl.when(pl.program_id(0) == 0)
        def _():
            loss_ref[...] = jnp.zeros_like(loss_ref)
            dg_ref[...] = jnp.zeros_like(dg_ref)

        mt = m_ref[...]
        acc = h1_ref[...]
        for c in range(N_CHIPS):
            hid = jnp.square(jnp.maximum(_dot(mt, wmi_ref[c]), 0.0)).astype(BF16)
            hid_ref[:, c * D_MODEL:(c + 1) * D_MODEL] = hid
            acc = acc + _dot(hid, wmo_ref[c])
        g = g_ref[...]
        y, hh, r = _rms_fwd(acc, g)
        e = y - t_ref[...]
        loss_ref[...] += jnp.sum(e * e, axis=0, keepdims=True)
        dy = e * (1.0 / D_MODEL)
        dg_ref[...] += jnp.sum(dy * hh, axis=0, keepdims=True)
        dh2 = _rms_bwd(dy, hh, r, g)
        dh2_ref[...] = dh2
        dh2b_ref[...] = dh2.astype(BF16)

    row = lambda w: pl.BlockSpec((TM, w), lambda i: (i, 0))
    vec = pl.BlockSpec((1, D_MODEL), lambda i: (0, 0))
    return pl.pallas_call(
        body, grid=(S // TM,), name="mlp_fwd_loss",
        in_specs=[row(D_MODEL), row(D_MODEL), row(D_MODEL), _resident(w_mi.shape), _resident(w_mo.shape),
                  _resident(g_f.shape)],
        out_specs=[row(D_FF), row(D_MODEL), row(D_MODEL), vec, vec],
        out_shape=[SDS((S, D_FF), BF16), SDS((S, D_MODEL), F32), SDS((S, D_MODEL), BF16),
                   SDS((1, D_MODEL), F32), SDS((1, D_MODEL), F32)],
        compiler_params=_cp("arbitrary"),
    )(m, h1, target, w_mi, w_mo, g_f)


def _mlp_bwd(dh2, dh2b, hid, h1, w_mi, w_mo, g_mlp):
    S = dh2.shape[0]

    def body(dh2_ref, dh2b_ref, hid_ref, h1_ref, wmi_ref, wmo_ref, g_ref, dpre_ref, dh1_ref, dh1b_ref, dg_ref):
        @pl.when(pl.program_id(0) == 0)
        def _():
            dg_ref[...] = jnp.zeros_like(dg_ref)

        d2 = dh2b_ref[...]
        dm = jnp.zeros((TM, D_MODEL), F32)
        dhid_next = _dot_nt(d2, wmo_ref[0])
        for c in range(N_CHIPS):
            cs = slice(c * D_MODEL, (c + 1) * D_MODEL)
            dhid = dhid_next
            if c + 1 < N_CHIPS:
                dhid_next = _dot_nt(d2, wmo_ref[c + 1])
            dpre = (dhid * (2.0 * jnp.sqrt(hid_ref[:, cs].astype(F32)))).astype(BF16)
            dpre_ref[:, cs] = dpre
            dm = dm + _dot_nt(dpre, wmi_ref[c])
        g = g_ref[...]
        _, hh, r = _rms_fwd(h1_ref[...], g)
        dg_ref[...] += jnp.sum(dm * hh, axis=0, keepdims=True)
        dh1 = dh2_ref[...] + _rms_bwd(dm, hh, r, g)
        dh1_ref[...] = dh1
        dh1b_ref[...] = dh1.astype(BF16)

    row = lambda w: pl.BlockSpec((TM, w), lambda i: (i, 0))
    return pl.pallas_call(
        body, grid=(S // TM,), name="mlp_bwd",
        in_specs=[row(D_MODEL), row(D_MODEL), row(D_FF), row(D_MODEL), _resident(w_mi.shape),
                  _resident(w_mo.shape), _resident(g_mlp.shape)],
        out_specs=[row(D_FF), row(D_MODEL), row(D_MODEL), pl.BlockSpec((1, D_MODEL), lambda i: (0, 0))],
        out_shape=[SDS((S, D_FF), BF16), SDS((S, D_MODEL), F32), SDS((S, D_MODEL), BF16), SDS((1, D_MODEL), F32)],
        compiler_params=_cp("arbitrary"),
    )(dh2, dh2b, hid, h1, w_mi, w_mo, g_mlp)


def _mixer_bwd(dh1b, a, p, mixed, gates, w_out, w_ao, w_po, wbd, scale, head_ones, sums):
    S = a.shape[0]
    n_tiles = S // TMB
    n = len(sums)

    def body(*refs):
        (dh1b_ref, a_ref, p_ref, mixed_ref, gate_ref, wout_ref, wao_ref, wpo_ref, wbd_ref, sc_ref,
         ones_ref) = refs[0:11]
        sum_refs = refs[11:11 + n]
        (da1_ref, dp1_ref, dgate_ref, da0_ref, dag1_ref, dag2_ref, dd0_ref, dd1_ref, dd2_ref,
         dmixed_ref, dqp_ref, dscale_ref) = refs[11 + n:23 + n]
        land_refs = refs[23 + n:23 + 2 * n]
        s_da, s_dd, send_sem, recv_sem = refs[23 + 2 * n:]
        i = pl.program_id(0)

        @pl.when(i == 0)
        def _():
            dscale_ref[...] = jnp.zeros_like(dscale_ref)
            for cpy in _chip_sum_copies(sum_refs, land_refs, send_sem, recv_sem):
                cpy.start()

        dmerged = _dot_nt(dh1b_ref[...], wout_ref[...])
        a = a_ref[...]
        p = p_ref[...]
        da = jnp.zeros((TMB, GROUP_W), F32)
        dp = jnp.zeros((TMB, POOL_W), F32)
        for j in range(N_CHIPS):
            js = slice(j * CHUNK, (j + 1) * CHUNK)
            sa = _sigmoid(gate_ref[:, js])
            sp = _sigmoid(gate_ref[:, D_MODEL + j * CHUNK:D_MODEL + (j + 1) * CHUNK])
            dmj = dmerged[:, js]
            da1 = (dmj * sa).astype(BF16)
            dp1 = (dmj * sp).astype(BF16)
            da1_ref[:, js] = da1
            dp1_ref[:, js] = dp1
            dgate_ref[j] = (dmj * _dot(a, wao_ref[j]) * sa * (1.0 - sa)).astype(BF16)
            dgate_ref[N_CHIPS + j] = (dmj * _dot(p, wpo_ref[j]) * sp * (1.0 - sp)).astype(BF16)
            da = da + _dot_nt(da1, wao_ref[j])
            dp = dp + _dot_nt(dp1, wpo_ref[j])

        dd = _dot_split(da * a.astype(F32), ones_ref[...])
        da0_ref[0] = da.astype(BF16)
        dd0_ref[0] = dd
        for h in range(2):
            s_da[h] = da[:, h * PAIR_W:(h + 1) * PAIR_W]
        s_dd[0] = dd
        for refs, d in (((dag1_ref, dd1_ref), DILATIONS[1]), ((dag2_ref, dd2_ref), DILATIONS[2])):
            for r in range(d):
                for h in range(2):
                    hs = slice(h * PAIR_W, (h + 1) * PAIR_W)
                    refs[0][r, :, hs] = s_da[h, pl.ds(r, TMB // d, stride=d), :].astype(BF16)
                refs[1][r] = s_dd[0, pl.ds(r, TMB // d, stride=d), :]

        sc = sc_ref[...]
        dscale_ref[...] += jnp.sum(dp * mixed_ref[...].astype(F32), axis=0, keepdims=True)
        dmixed = (dp * sc).astype(BF16)
        dmixed_ref[...] = dmixed
        inv_cnt, _ = _pool_inv_count(i, TMB)
        dqp_ref[...] = (_dot_nt(dmixed, wbd_ref[...]) * inv_cnt).astype(BF16)

        @pl.when(i == n_tiles - 1)
        def _():
            for cpy in _chip_sum_copies(sum_refs, land_refs, send_sem, recv_sem):
                cpy.wait()

    row = lambda w: pl.BlockSpec((TMB, w), lambda i: (i, 0))
    grp_spec = lambda d: pl.BlockSpec((d, TMB // d, GROUP_W), lambda i: (0, i, 0))
    stat_spec = lambda d: pl.BlockSpec((d, TMB // d, STAT_W), lambda i: (0, i, 0))
    d0, d1, d2 = DILATIONS
    res = pl.pallas_call(
        body, grid=(n_tiles,), name="mixer_bwd",
        in_specs=[row(D_MODEL), row(GROUP_W), row(POOL_W), row(POOL_W), row(2 * D_MODEL),
                  _resident(w_out.shape), _resident(w_ao.shape), _resident(w_po.shape), _resident(wbd.shape),
                  _resident(scale.shape), _resident(head_ones.shape)] + [ANY] * n,
        out_specs=[row(D_MODEL), row(D_MODEL), pl.BlockSpec((2 * N_CHIPS, TMB, CHUNK), lambda i: (0, i, 0)),
                   grp_spec(d0), grp_spec(d1), grp_spec(d2), stat_spec(d0), stat_spec(d1), stat_spec(d2),
                   row(POOL_W), row(POOL_W), pl.BlockSpec((1, POOL_W), lambda i: (0, 0))] + [ANY] * n,
        out_shape=[SDS((S, D_MODEL), BF16), SDS((S, D_MODEL), BF16), SDS((2 * N_CHIPS, S, CHUNK), BF16)]
        + [SDS((d, S // d, GROUP_W), BF16) for d in DILATIONS]
        + [SDS((d, S // d, STAT_W), F32) for d in DILATIONS]
        + [SDS((S, POOL_W), BF16), SDS((S, POOL_W), BF16), SDS((1, POOL_W), F32)]
        + [SDS(t.shape, t.dtype) for t in sums],
        scratch_shapes=[pltpu.VMEM((2, TMB, PAIR_W), F32), pltpu.VMEM((1, TMB, STAT_W), F32),
                        pltpu.SemaphoreType.DMA((3 * n,)), pltpu.SemaphoreType.DMA((3 * n,))],
        compiler_params=_cp("arbitrary"),
    )(dh1b, a, p, mixed, gates, w_out, w_ao, w_po, wbd, scale, head_ones, *sums)
    return res[:12], res[12:]


def _attn_bwd(qkv, da, lt, dd, grp):
    d = DILATIONS[grp]
    L = qkv.shape[2]
    RR, RB, nb = _attn_tiles(grp, L)

    def body(q_ref, kc_ref, kp_ref, vc_ref, vp_ref, da_ref, lt_ref, dd_ref, dq_ref, dk_ref, dv_ref, dk_acc, dv_acc):
        i = pl.program_id(1)

        @pl.when(i == 0)
        def _():
            dk_acc[...] = jnp.zeros_like(dk_acc)
            dv_acc[...] = jnp.zeros_like(dv_acc)

        def compute(cur, prv):
            dk_acc[cur] = jnp.zeros((RR, RB * BAND, GROUP_W), F32)
            dv_acc[cur] = jnp.zeros((RR, RB * BAND, GROUP_W), F32)
            biases, col = _band_bias(grp, d)
            first_keys_ok = (col >= BAND) | (i > 0)
            is_a = lax.broadcasted_iota(jnp.int32, (BAND, PAIR_W), 1) < HEAD_W
            for rr in range(RR):
                for rb in range(RB):
                    rows = slice(rb * BAND, (rb + 1) * BAND)
                    for cp in range(2):
                        cs = slice(cp * PAIR_W, (cp + 1) * PAIR_W)
                        q2 = q_ref[rr, rows, cs]
                        da2 = da_ref[rr, rows, cs]
                        lt2 = lt_ref[rr, rows, :]
                        dd2 = dd_ref[rr, rows, :]
                        kcat = _kv_tile(kc_ref, kp_ref, rr, rb, cs)
                        vcat = _kv_tile(vc_ref, vp_ref, rr, rb, cs)
                        q2t = q2.astype(F32).T.astype(BF16)
                        da2t = da2.astype(F32).T.astype(BF16)
                        dqs, dkts, dvts, scores, dpvs = [], [], [], [], []
                        for h2 in range(2):
                            sel = is_a if h2 == 0 else jnp.logical_not(is_a)
                            b = biases[cp * 2 + h2]
                            if rb == 0:
                                b = jnp.where(first_keys_ok, b, NEG)
                            scores.append(_dot_nt(jnp.where(sel, q2, jnp.zeros_like(q2)), kcat) + b)
                            dpvs.append(_dot_nt(jnp.where(sel, da2, jnp.zeros_like(da2)), vcat))
                        for h2 in range(2):
                            lane0 = h2 * HEAD_W
                            p = jnp.exp(scores[h2] - _head_col(lt2, cp * 2 + h2))
                            ds = (p * (dpvs[h2] - _head_col(dd2, cp * 2 + h2))).astype(BF16)
                            dqs.append(_dot(ds, kcat))
                            dkts.append(_dot(q2t[lane0:lane0 + HEAD_W, :], ds))
                            dvts.append(_dot(da2t[lane0:lane0 + HEAD_W, :], p.astype(BF16)))
                        dq_ref[rr, rows, cs] = (jnp.where(is_a, dqs[0], dqs[1]) * 0.125).astype(BF16)
                        dkc = jnp.concatenate(dkts, axis=0).T
                        dvc = jnp.concatenate(dvts, axis=0).T
                        if rb == 0:
                            last = slice((RB - 1) * BAND, RB * BAND)
                            dk_acc[prv, rr, last, cs] += dkc[0:BAND]
                            dv_acc[prv, rr, last, cs] += dvc[0:BAND]
                            dk_acc[cur, rr, 0:BAND, cs] += dkc[BAND:]
                            dv_acc[cur, rr, 0:BAND, cs] += dvc[BAND:]
                        else:
                            both = slice((rb - 1) * BAND, (rb + 1) * BAND)
                            dk_acc[cur, rr, both, cs] += dkc
                            dv_acc[cur, rr, both, cs] += dvc

        def flush(prv):
            dk_ref[...] = dk_acc[prv].astype(BF16)
            dv_ref[...] = dv_acc[prv].astype(BF16)

        for parity in (0, 1):
            on = (i % 2) == parity
            pl.when(on & (i < nb))(functools.partial(compute, parity, 1 - parity))
            pl.when(on & (i > 0))(functools.partial(flush, 1 - parity))

    qi = lambda i: jnp.minimum(i, nb - 1)
    cur_w = lambda w: pl.BlockSpec((None, RR, RB * BAND, GROUP_W), lambda j, i: (w, j, qi(i), 0))
    prev_w = lambda w: pl.BlockSpec((None, RR, BAND, GROUP_W),
                                    lambda j, i: (w, j, jnp.maximum(qi(i) * RB - 1, 0), 0))
    blk = pl.BlockSpec((RR, RB * BAND, GROUP_W), lambda j, i: (j, qi(i), 0))
    stat_blk = pl.BlockSpec((RR, RB * BAND, STAT_W), lambda j, i: (j, qi(i), 0))
    late = pl.BlockSpec((RR, RB * BAND, GROUP_W), lambda j, i: (j, jnp.maximum(i - 1, 0), 0))
    return pl.pallas_call(
        body, grid=(d // RR, nb + 1), name=f"attn_bwd_g{grp}",
        in_specs=[cur_w(0), cur_w(1), prev_w(1), cur_w(2), prev_w(2), blk, stat_blk, stat_blk],
        out_specs=[blk, late, late],
        out_shape=[SDS((d, L, GROUP_W), BF16)] * 3,
        scratch_shapes=[pltpu.VMEM((2, RR, RB * BAND, GROUP_W), F32), pltpu.VMEM((2, RR, RB * BAND, GROUP_W), F32)],
        compiler_params=_cp("parallel", "arbitrary"),
    )(qkv, qkv, qkv, qkv, qkv, da, lt, dd)


def _dz_assemble(dqkv, dqp):
    S = dqp.shape[0]
    n_tiles = S // TMB

    def body(*refs):
        dqkv_refs = refs[0:9]
        dqp_ref, halo_ref = refs[9:11]
        dz_ref, s_ref, ext_ref = refs[11:]
        i = pl.program_id(0)

        for grp in range(3):
            for which in range(3):
                n = which * 3 + grp
                ref = dqkv_refs[grp * 3 + which]
                if DILATIONS[grp] == 1:
                    dz_ref[n] = ref[0]
                else:
                    _interleave_load(ref, (), s_ref, DILATIONS[grp], TMB)
                    for h in range(2):
                        dz_ref[n, :, h * PAIR_W:(h + 1) * PAIR_W] = s_ref[h].astype(BF16)

        dqp = dqp_ref[...].astype(F32)
        ext_ref[0:TMB, :] = dqp
        ext_ref[TMB:, :] = jnp.where(i < n_tiles - 1, halo_ref[...].astype(F32), 0.0)
        sums = []
        acc = ext_ref[...]
        for k in (1, 2, 4, 8):
            acc = acc + pltpu.roll(acc, TMB + POOL_HALO - k, 0)
            sums.append(acc[0:TMB, :])
        inv_cnt, col = _pool_inv_count(i, TMB)
        dpz = _pool_column_select(col, sums) - dqp / inv_cnt
        for t in range(3):
            dz_ref[9 + t] = dpz[:, t * CHUNK:(t + 1) * CHUNK].astype(BF16)

    row = lambda w: pl.BlockSpec((TMB, w), lambda i: (i, 0))
    grp_spec = lambda d: pl.BlockSpec((d, TMB // d, GROUP_W), lambda i: (0, i, 0))
    halo = pl.BlockSpec((POOL_HALO, POOL_W),
                        lambda i: (jnp.minimum((i + 1) * (TMB // POOL_HALO), S // POOL_HALO - 1), 0))
    flat = [t for grp in range(3) for t in dqkv[grp]]
    return pl.pallas_call(
        body, grid=(n_tiles,), name="dz_assemble",
        in_specs=[grp_spec(DILATIONS[grp]) for grp in range(3) for _ in range(3)] + [row(POOL_W), halo],
        out_specs=pl.BlockSpec((N_DZ_CHUNKS, TMB, CHUNK), lambda i: (0, i, 0)),
        out_shape=SDS((N_DZ_CHUNKS, S, CHUNK), BF16),
        scratch_shapes=[pltpu.VMEM((2, TMB, PAIR_W), F32), pltpu.VMEM((TMB + POOL_HALO, POOL_W), F32)],
        compiler_params=_cp("parallel"),
    )(*flat, dqp, dqp)


def _inproj_dx(dz, dgates, dh1, x, g, w_in, sums):
    S = x.shape[0]
    n_tiles = S // TM
    n = len(sums)

    def body(*refs):
        dz_ref, dgate_ref, dh1_ref, x_ref, g_ref, w_ref = refs[0:6]
        sum_refs = refs[6:6 + n]
        dx_ref, dg_ref = refs[6 + n:8 + n]
        land_refs = refs[8 + n:8 + 2 * n]
        sems = refs[8 + 2 * n:]
        i = pl.program_id(0)

        def copies():
            return _chip_sum_copies(sum_refs, land_refs, *sems)

        @pl.when(i == 0)
        def _():
            dg_ref[...] = jnp.zeros_like(dg_ref)
            for cpy in copies():
                cpy.start()

        du = jnp.zeros((TM, D_MODEL), F32)
        for k in range(N_CHUNKS):
            dzk = dz_ref[k] if k < N_DZ_CHUNKS else dgate_ref[k - N_DZ_CHUNKS]
            du = du + _dot_nt(dzk, _w_in_chunk(w_ref, k))
        gv = g_ref[...]
        _, xh, r = _rms_fwd(x_ref[...], gv)
        dg_ref[...] += jnp.sum(du * xh, axis=0, keepdims=True)
        dx_ref[...] = dh1_ref[...] + _rms_bwd(du, xh, r, gv)

        @pl.when(i == n_tiles - 1)
        def _():
            for cpy in copies():
                cpy.wait()

    row = lambda w: pl.BlockSpec((TM, w), lambda i: (i, 0))
    res = pl.pallas_call(
        body, grid=(n_tiles,), name="inproj_dx",
        in_specs=[pl.BlockSpec((N_DZ_CHUNKS, TM, CHUNK), lambda i: (0, i, 0)),
                  pl.BlockSpec((N_CHUNKS - N_DZ_CHUNKS, TM, CHUNK), lambda i: (0, i, 0)),
                  row(D_MODEL), row(D_MODEL), _resident(g.shape), _resident(w_in.shape)] + [ANY] * n,
        out_specs=[row(D_MODEL), pl.BlockSpec((1, D_MODEL), lambda i: (0, 0))] + [ANY] * n,
        out_shape=[SDS((S, D_MODEL), F32), SDS((1, D_MODEL), F32)] + [SDS(t.shape, t.dtype) for t in sums],
        scratch_shapes=[pltpu.SemaphoreType.DMA((3 * n,)), pltpu.SemaphoreType.DMA((3 * n,))],
        compiler_params=_cp("arbitrary"),
    )(dz, dgates, dh1, x, g, w_in, *sums)
    return res[0], res[1], res[2:]


def _wgrad(a, b, name, *, out_shape, a_spec, b_spec, out_spec, grid, n_out_cols=None, fill=None, narrow=True):
    k_axis = len(grid) - 1
    n_k = grid[k_axis]
    n_out = 2 if narrow else 1

    def body(a_ref, b_ref, *rest):
        o_ref = rest[-n_out]

        @pl.when(pl.program_id(k_axis) == 0)
        def _():
            o_ref[...] = jnp.zeros_like(o_ref)

        at = a_ref[...]
        if n_out_cols is None:
            o_ref[...] += _dot_tn(at, b_ref[...])
        elif n_out_cols[0] == "lead_both":
            for t in range(b_ref.shape[0]):
                o_ref[t] += _dot_tn(at, b_ref[t])
        else:
            w = n_out_cols[1]
            for t in range(o_ref.shape[0]):
                o_ref[t] += _dot_tn(at, b_ref[:, t * w:(t + 1) * w])

        if narrow:
            @pl.when(pl.program_id(k_axis) == n_k - 1)
            def _():
                rest[-1][...] = o_ref[...].astype(BF16)

    sem = ("parallel",) * k_axis + ("arbitrary",)
    extra = [] if fill is None else list(fill) if narrow else [fill]
    shapes = [out_shape, SDS(out_shape.shape, BF16)] if narrow else out_shape
    return pl.pallas_call(body, grid=grid, name=name, in_specs=[a_spec, b_spec] + [ANY] * len(extra),
                          out_specs=[out_spec] * n_out if narrow else out_spec, out_shape=shapes,
                          input_output_aliases={2 + t: t for t in range(len(extra))},
                          compiler_params=_cp(*sem))(a, b, *extra)


def _wgrad_in(u, dz, dgates):
    bk = min(BK, u.shape[0])
    nk = u.shape[0] // bk
    g = WGRAD_IN_GROUP
    kw = dict(n_out_cols=("lead_both", CHUNK), a_spec=pl.BlockSpec((bk, D_MODEL), lambda j, k: (k, 0)),
              b_spec=pl.BlockSpec((g, bk, CHUNK), lambda j, k: (j, k, 0)),
              out_shape=SDS((N_CHUNKS, D_MODEL, CHUNK), F32))
    first = _wgrad(u, dz, "wgrad_in_qkvp", grid=(N_DZ_CHUNKS // g, nk),
                   out_spec=pl.BlockSpec((g, D_MODEL, CHUNK), lambda j, k: (j, 0, 0)), **kw)
    both = _wgrad(u, dgates, "wgrad_in_gates", grid=((N_CHUNKS - N_DZ_CHUNKS) // g, nk), fill=first,
                  out_spec=pl.BlockSpec((g, D_MODEL, CHUNK), lambda j, k: (N_DZ_CHUNKS // g + j, 0, 0)), **kw)
    return [t.reshape(N_CHIPS, CHUNKS_PER_SHARD * D_MODEL, CHUNK) for t in both]


def _wgrads_mixer(a, da1, p, dp1, merged, dh1b, pooled, dmixed):
    bk = min(BK, a.shape[0])
    nk = a.shape[0] // bk
    g_ao = _wgrad(
        a, da1, "wgrad_att_out", grid=(nk,), n_out_cols=("cols_b", CHUNK),
        a_spec=pl.BlockSpec((bk, GROUP_W), lambda k: (k, 0)),
        b_spec=pl.BlockSpec((bk, D_MODEL), lambda k: (k, 0)),
        out_spec=pl.BlockSpec((N_CHIPS, GROUP_W, CHUNK), lambda k: (0, 0, 0)),
        out_shape=SDS((N_CHIPS, GROUP_W, CHUNK), F32))
    g_po = _wgrad(
        p, dp1, "wgrad_pool_out", grid=(nk,), n_out_cols=("cols_b", CHUNK),
        a_spec=pl.BlockSpec((bk, POOL_W), lambda k: (k, 0)),
        b_spec=pl.BlockSpec((bk, D_MODEL), lambda k: (k, 0)),
        out_spec=pl.BlockSpec((N_CHIPS, POOL_W, CHUNK), lambda k: (0, 0, 0)),
        out_shape=SDS((N_CHIPS, POOL_W, CHUNK), F32))
    g_out = _wgrad(
        merged, dh1b, "wgrad_out", grid=(nk,),
        a_spec=pl.BlockSpec((bk, D_MODEL), lambda k: (k, 0)),
        b_spec=pl.BlockSpec((bk, D_MODEL), lambda k: (k, 0)),
        out_spec=pl.BlockSpec((D_MODEL, D_MODEL), lambda k: (0, 0)),
        out_shape=SDS((D_MODEL, D_MODEL), F32))
    g_bd = _wgrad(
        pooled, dmixed, "wgrad_pool_grp", grid=(nk,),
        a_spec=pl.BlockSpec((bk, POOL_W), lambda k: (k, 0)),
        b_spec=pl.BlockSpec((bk, POOL_W), lambda k: (k, 0)),
        out_spec=pl.BlockSpec((POOL_W, POOL_W), lambda k: (0, 0)),
        out_shape=SDS((POOL_W, POOL_W), F32), narrow=False)
    g_out = [t.reshape(N_CHIPS, D_MODEL // N_CHIPS, D_MODEL) for t in g_out]
    return [g_ao, g_po, g_out], g_bd


def _wgrads_mlp(m, dpre, hid, dh2b):
    bk = min(BK, m.shape[0])
    nk = m.shape[0] // bk
    g_mi = _wgrad(
        m, dpre, "wgrad_mlp_in", grid=(N_CHIPS, nk),
        a_spec=pl.BlockSpec((bk, D_MODEL), lambda c, k: (k, 0)),
        b_spec=pl.BlockSpec((bk, D_MODEL), lambda c, k: (k, c)),
        out_spec=pl.BlockSpec((None, D_MODEL, D_MODEL), lambda c, k: (c, 0, 0)),
        out_shape=SDS((N_CHIPS, D_MODEL, D_MODEL), F32))
    g_mo = _wgrad(
        hid, dh2b, "wgrad_mlp_out", grid=(N_CHIPS, nk),
        a_spec=pl.BlockSpec((bk, D_MODEL), lambda c, k: (k, c)),
        b_spec=pl.BlockSpec((bk, D_MODEL), lambda c, k: (k, 0)),
        out_spec=pl.BlockSpec((None, D_MODEL, D_MODEL), lambda c, k: (c, 0, 0)),
        out_shape=SDS((N_CHIPS, D_MODEL, D_MODEL), F32))
    return [g_mi, g_mo]


def _mesh_place():
    x, y, c = lax.axis_index("x"), lax.axis_index("y"), lax.axis_index("c")
    other_chips = [(x, 1 - y), (1 - x, y), (1 - x, 1 - y)]
    return x, y, c, other_chips


ANY = pl.BlockSpec(memory_space=pl.ANY)


def _weight_half_copies(shard_refs, buf_refs, rows, send_sem, recv_sem):
    x, y, c, chips = _mesh_place()
    me = 2 * x + y
    copies = []
    for w, r_full in enumerate(rows):
        rh = r_full // 2
        for r, (px, py) in enumerate(chips):
            k = w * 3 + r
            copies.append(pltpu.make_async_remote_copy(
                src_ref=shard_refs[w].at[pl.ds(c * rh, rh), :], dst_ref=buf_refs[w].at[me, pl.ds(c * rh, rh), :],
                send_sem=send_sem.at[k], recv_sem=recv_sem.at[k], device_id=(px, py, c), device_id_type=MESH))
    return copies


def _pair_forward_copies(buf_refs, rows, send_sem, recv_sem):
    x, y, c, chips = _mesh_place()
    out = []
    for w, r_full in enumerate(rows):
        rh = r_full // 2
        for r, (px, py) in enumerate(chips):
            k = w * 3 + r
            landed = buf_refs[w].at[2 * px + py, pl.ds(c * rh, rh), :]
            theirs = buf_refs[w].at[2 * px + py, pl.ds((1 - c) * rh, rh), :]
            mk = lambda ref: pltpu.make_async_remote_copy(
                src_ref=ref, dst_ref=ref, send_sem=send_sem.at[k], recv_sem=recv_sem.at[k],
                device_id=(x, y, 1 - c), device_id_type=MESH)
            out.append((mk(landed), mk(theirs)))
    return out


def _place_own(block, n_slots, slot):
    buf = lax.empty((n_slots,) + block.shape, block.dtype)
    return lax.dynamic_update_slice(buf, block[None], (slot,) + (0,) * block.ndim)


def _pair_forward(bufs, rows, name):
    n = len(bufs)

    def body(*refs):
        dst = refs[n:2 * n]
        send_sem, recv_sem = refs[2 * n:]
        fwds = _pair_forward_copies(dst, rows, send_sem, recv_sem)
        for fwd, _ in fwds:
            fwd.start()
        for fwd, landing in fwds:
            landing.wait_recv()
            fwd.wait_send()

    return pl.pallas_call(
        body, name=name,
        in_specs=[ANY] * n, out_specs=[ANY] * n,
        out_shape=[SDS(b.shape, b.dtype) for b in bufs],
        scratch_shapes=[pltpu.SemaphoreType.DMA((3 * n,))] * 2,
        input_output_aliases={w: w for w in range(n)},
    )(*bufs)


def _chip_sum_copies(src, dst, send_sem, recv_sem):
    x, y, c, chips = _mesh_place()
    copies = []
    for w in range(len(src)):
        for r, (px, py) in enumerate(chips):
            k = w * 3 + r
            copies.append(pltpu.make_async_remote_copy(
                src_ref=src[w].at[r + 1], dst_ref=dst[w].at[r + 1], send_sem=send_sem.at[k], recv_sem=recv_sem.at[k],
                device_id=(px, py, c), device_id_type=MESH))
    return copies


def _pair_exchange(grads):
    n = len(grads)

    def body(*refs):
        src, dst = refs[:n], refs[n:2 * n]
        send_sem, recv_sem = refs[2 * n:]
        x, y, c, _ = _mesh_place()
        copies = []
        for w in range(n):
            rh = grads[w].shape[1] // 2
            copies.append(pltpu.make_async_remote_copy(
                src_ref=src[w].at[:, pl.ds((1 - c) * rh, rh), :], dst_ref=dst[w],
                send_sem=send_sem.at[w], recv_sem=recv_sem.at[w],
                device_id=(x, y, 1 - c), device_id_type=MESH))
            copies[-1].start()
        for cpy in copies:
            cpy.wait()

    return pl.pallas_call(
        body, name="grad_pair_exchange",
        in_specs=[ANY] * n, out_specs=[ANY] * n,
        out_shape=[SDS((N_CHIPS, g.shape[1] // 2, g.shape[2]), g.dtype) for g in grads],
        scratch_shapes=[pltpu.SemaphoreType.DMA((n,)), pltpu.SemaphoreType.DMA((n,))],
    )(*grads)


def _pair_sum(place, grad, recv, name):
    _, R, C = grad.shape
    rh = R // 2
    br = _row_block(rh, max(256, ELEMENTWISE_BLOCK // C))
    nbh = rh // br

    def body(place_ref, g_ref, r_ref, own_ref, sums_ref):
        s = g_ref[...] + r_ref[...].astype(F32)

        @pl.when(pl.program_id(1) == 0)
        def _():
            own_ref[...] = s

        sums_ref[...] = s.astype(BF16)

    slot = lambda rel, pr: jnp.bitwise_xor(pr[0], rel)
    return pl.pallas_call(
        body, name=name,
        grid_spec=pltpu.PrefetchScalarGridSpec(
            num_scalar_prefetch=1, grid=(nbh, N_CHIPS),
            in_specs=[pl.BlockSpec((None, br, C), lambda i, rel, pr: (slot(rel, pr), pr[1] * nbh + i, 0)),
                      pl.BlockSpec((None, br, C), lambda i, rel, pr: (slot(rel, pr), i, 0))],
            out_specs=[pl.BlockSpec((br, C), lambda i, rel, pr: (i, 0)),
                       pl.BlockSpec((None, br, C), lambda i, rel, pr: (rel, i, 0))]),
        out_shape=[SDS((rh, C), F32), SDS((N_CHIPS, rh, C), BF16)],
        compiler_params=_cp("parallel", "arbitrary"),
    )(place, grad, recv)


def _chip_sum(place, own, recv, name):
    rh, C = own.shape
    br = _row_block(rh, max(256, ELEMENTWISE_BLOCK // C))
    nbh = rh // br

    def body(place_ref, own_ref, r_ref, o_ref):
        o_ref[...] = ((own_ref[...] + r_ref[1].astype(F32)) + r_ref[2].astype(F32)) + r_ref[3].astype(F32)

    return pl.pallas_call(
        body, name=name,
        grid_spec=pltpu.PrefetchScalarGridSpec(
            num_scalar_prefetch=1, grid=(nbh,),
            in_specs=[pl.BlockSpec((br, C), lambda i, pr: (i, 0)),
                      pl.BlockSpec((N_CHIPS, br, C), lambda i, pr: (0, i, 0))],
            out_specs=pl.BlockSpec((br, C), lambda i, pr: (pr[1] * nbh + i, 0))),
        out_shape=SDS((2 * rh, C), F32),
        compiler_params=_cp("parallel"),
    )(place, own, recv)


def _finish_exchange(grads, small_all):
    n = len(grads)

    def body(*refs):
        dst, all_ref = refs[n + 1:2 * n + 1], refs[2 * n + 1]
        send_sem, recv_sem, ssend_sem, srecv_sem = refs[2 * n + 2:]
        x, y, c, chips = _mesh_place()
        sib = (x, y, 1 - c)

        def pack(dev, k, to):
            slot = 4 * dev[0] + 2 * dev[1] + dev[2]
            return pltpu.make_async_remote_copy(
                src_ref=all_ref.at[slot], dst_ref=all_ref.at[slot], send_sem=ssend_sem.at[k],
                recv_sem=srecv_sem.at[k], device_id=to, device_id_type=MESH)

        pack_copies = [pack((x, y, c), 0, sib)] + [pack((x, y, c), 1 + r, (px, py, c))
                                                   for r, (px, py) in enumerate(chips)]
        for cpy in pack_copies:
            cpy.start()
        sends, landings = [], []
        for w in range(n):
            rh = grads[w].shape[0] // 2
            mk = lambda cc: pltpu.make_async_remote_copy(
                src_ref=dst[w].at[pl.ds(cc * rh, rh), :], dst_ref=dst[w].at[pl.ds(cc * rh, rh), :],
                send_sem=send_sem.at[w], recv_sem=recv_sem.at[w], device_id=(x, y, 1 - c), device_id_type=MESH)
            sends.append(mk(c))
            landings.append(mk(1 - c))
            sends[-1].start()
        for r, (px, py) in enumerate(chips):
            pack((px, py, c), 1 + r, (px, py, c)).wait_recv()
            pack_copies.append(pack((px, py, c), 4 + r, sib))
            pack_copies[-1].start()
        pack(sib, 0, sib).wait_recv()
        for r, (px, py) in enumerate(chips):
            pack((px, py, 1 - c), 4 + r, sib).wait_recv()
        for cpy in landings:
            cpy.wait_recv()
        for cpy in sends + pack_copies:
            cpy.wait_send()

    res = pl.pallas_call(
        body, name="grad_finish_exchange",
        in_specs=[ANY] * (n + 1), out_specs=[ANY] * (n + 1),
        out_shape=[SDS(g.shape, g.dtype) for g in grads] + [SDS(small_all.shape, small_all.dtype)],
        scratch_shapes=[pltpu.SemaphoreType.DMA((n,)), pltpu.SemaphoreType.DMA((n,)),
                        pltpu.SemaphoreType.DMA((N_DEV - 1,)), pltpu.SemaphoreType.DMA((N_DEV - 1,))],
        input_output_aliases={w: w for w in range(n + 1)},
    )(*grads, small_all)
    return res[:n], res[n]


def _adamw_math(w, g, m, v):
    m = ADAM_B1 * m + (1.0 - ADAM_B1) * g
    v = ADAM_B2 * v + (1.0 - ADAM_B2) * jnp.square(g)
    m_hat = m / (1.0 - ADAM_B1 ** ADAM_STEP)
    v_hat = v / (1.0 - ADAM_B2 ** ADAM_STEP)
    delta = -ADAM_LR * (m_hat / (jnp.sqrt(v_hat) + ADAM_EPS) + ADAM_WD * w)
    return delta, m, v


def _adamw(w, g, m, v, name):
    R, C = w.shape
    br = _row_block(R, 512)
    if g.ndim == 3:
        n_chunks, cw = g.shape[0], g.shape[2]
        g_spec = pl.BlockSpec((None, br, cw), lambda t, i: (t, i, 0))
    else:
        n_chunks, cw = 1, C
        g_spec = pl.BlockSpec((br, cw), lambda t, i: (i, t))

    def body(w_ref, g_ref, m_ref, v_ref, g_out_ref, d_ref, nm_ref, nv_ref):
        gv = g_ref[...]
        g_out_ref[...] = gv
        d_ref[...], nm_ref[...], nv_ref[...] = _adamw_math(w_ref[...], gv, m_ref[...], v_ref[...])

    spec = pl.BlockSpec((br, cw), lambda t, i: (i, t))
    return pl.pallas_call(
        body, grid=(n_chunks, R // br), name=name, in_specs=[spec, g_spec, spec, spec], out_specs=[spec] * 4,
        out_shape=[SDS((R, C), F32)] * 4, compiler_params=_cp("parallel", "parallel"),
    )(w, g, m, v)


def _small_sum_adamw(all_small, w, m, v):
    loss_row = PACK_ROWS - 8

    def body(all_ref, w_ref, m_ref, v_ref, g_ref, d_ref, nm_ref, nv_ref, loss_ref):
        g = all_ref[0]
        for k in range(1, N_DEV):
            g = g + all_ref[k]
        g_ref[...] = g
        d_ref[...], nm_ref[...], nv_ref[...] = _adamw_math(w_ref[...], g, m_ref[...], v_ref[...])
        total = jnp.sum(g[loss_row:loss_row + 1, :]) * (0.5 / D_MODEL)
        loss_ref[...] = jnp.full(loss_ref.shape, total, F32)

    full = lambda s: pl.BlockSpec(s, lambda i: (0,) * len(s))
    pack = (PACK_ROWS, D_MODEL)
    return pl.pallas_call(
        body, grid=(1,), name="small_sum_adamw",
        in_specs=[full((N_DEV,) + pack), full(pack), full(pack), full(pack)],
        out_specs=[full(pack)] * 4 + [full((8, 128))],
        out_shape=[SDS(pack, F32)] * 4 + [SDS((8, 128), F32)],
        compiler_params=_cp("arbitrary"),
    )(all_small, w, m, v)


def _pack_small(grp, scale, g_mix, g_mlp, g_f, loss_lanes):
    def part(vec):
        vec = vec.reshape(1, -1)
        return jnp.pad(vec, ((0, 7), (0, D_MODEL - vec.shape[1])))
    return jnp.concatenate([grp.reshape(-1, D_MODEL), part(scale), part(g_mix), part(g_mlp), part(g_f),
                            part(loss_lanes)], axis=0)


def _unpack_small(pack):
    n_grp = len(POOL_WINDOWS) * POOL_GROUP_W * POOL_GROUP_W // D_MODEL
    grp = pack[:n_grp].reshape(1, len(POOL_WINDOWS), POOL_GROUP_W, POOL_GROUP_W)
    scale = pack[n_grp, :POOL_W].reshape(1, POOL_W)
    g_mix = pack[n_grp + 8].reshape(1, D_MODEL)
    g_mlp = pack[n_grp + 16].reshape(1, D_MODEL)
    g_f = pack[n_grp + 24].reshape(D_MODEL)
    return grp, scale, g_mix, g_mlp, g_f


def _block_diag(grp):
    out = jnp.zeros((POOL_W, POOL_W), grp.dtype)
    for k in range(len(POOL_WINDOWS)):
        out = lax.dynamic_update_slice(out, grp[k], (k * POOL_GROUP_W, k * POOL_GROUP_W))
    return out


def kernel(x, norm_mix_g, w_in, w_att_out, w_pool_grp, pool_scale, w_pool_out, w_out, norm_mlp_g, w_mlp_in, w_mlp_out, norm_final_g, loss_target, m_norm_mix_g, m_w_in, m_w_att_out, m_w_pool_grp, m_pool_scale, m_w_pool_out, m_w_out, m_norm_mlp_g, m_w_mlp_in, m_w_mlp_out, m_norm_final_g, v_norm_mix_g, v_w_in, v_w_att_out, v_w_pool_grp, v_pool_scale, v_w_pool_out, v_w_out, v_norm_mlp_g, v_w_mlp_in, v_w_mlp_out, v_norm_final_g):
    S = x.shape[1]
    xs, target = x[0], loss_target[0]
    big = [w_in[0], w_att_out[0], w_pool_out[0], w_out[0], w_mlp_in[0], w_mlp_out[0]]
    big_m = [m_w_in[0], m_w_att_out[0], m_w_pool_out[0], m_w_out[0], m_w_mlp_in[0], m_w_mlp_out[0]]
    big_v = [v_w_in[0], v_w_att_out[0], v_w_pool_out[0], v_w_out[0], v_w_mlp_in[0], v_w_mlp_out[0]]

    chip = 2 * lax.axis_index("x") + lax.axis_index("y")
    core = lax.axis_index("c")
    place = jnp.stack([chip, core]).astype(jnp.int32)
    names = ("w_in", "w_att_out", "w_pool_out", "w_out", "w_mlp_in", "w_mlp_out")

    shards = [w.astype(BF16) for w in big]
    bufs = [_place_own(sh, N_CHIPS, chip) for sh in shards]
    wbd = _block_diag(w_pool_grp[0]).astype(BF16)
    g_final = norm_final_g.reshape(1, D_MODEL)
    stat_expand, stat_reduce = _stat_matrices()

    u, z_own, landed_in = _norm_inproj_own(xs, norm_mix_g, shards[0], bufs[0])
    (wg_in,) = _pair_forward([landed_in], [shards[0].shape[0]], "w_in_pair_forward")
    (qkv0, qkv1, qkv2, pz, gates), (wg_ao, wg_po, wg_out) = _inproj_rest(u, z_own, wg_in, shards[1:4], bufs[1:4])
    wg_out = wg_out.reshape(D_MODEL, D_MODEL)
    qkv = (qkv0, qkv1, qkv2)
    att = [_attn_fwd(qkv[grp], grp) for grp in range(3)]
    (a, lt0, lt1, lt2, pooled, mixed, p, merged, h1, m), (wg_mi, wg_mo) = _mixer_out(
        [o for o, _ in att], [l for _, l in att], pz, gates, xs, wg_ao, wg_po, wbd, pool_scale, wg_out, norm_mlp_g,
        stat_expand, shards[4:], bufs[4:])
    hid, dh2, dh2b, loss_lanes, dg_final = _mlp_fwd_loss(m, h1, target, wg_mi, wg_mo, g_final)

    def pair_reduce(grads, grad_names):
        recv = _pair_exchange([narrow for _, narrow in grads])
        pair = [_pair_sum(place, g, r, f"pair_sum_{nm}") for (g, _), r, nm in zip(grads, recv, grad_names)]
        return [own for own, _ in pair], [s for _, s in pair]

    def chip_reduce(owns, landed_sums, grad_names):
        return [_chip_sum(place, own, r, f"chip_sum_{nm}") for own, r, nm in zip(owns, landed_sums, grad_names)]

    dpre, dh1, dh1b, dg_mlp = _mlp_bwd(dh2, dh2b, hid, h1, wg_mi, wg_mo, norm_mlp_g)
    own_mlp, sums_mlp = pair_reduce(_wgrads_mlp(m, dpre, hid, dh2b), names[4:])
    (da1, dp1, dgates, da0, dag1, dag2, dd0, dd1, dd2, dmixed, dqp, dscale), landed_mlp = _mixer_bwd(
        dh1b, a, p, mixed, gates, wg_out, wg_ao, wg_po, wbd, pool_scale, stat_reduce, sums_mlp)
    g_mi, g_mo = chip_reduce(own_mlp, landed_mlp, names[4:])
    grads_mixer, g_bd = _wgrads_mixer(a, da1, p, dp1, merged, dh1b, pooled, dmixed)
    dqkv = [_attn_bwd(qkv[grp], da_g, lt_g, dd_g, grp)
            for grp, (da_g, lt_g, dd_g) in enumerate(((da0, lt0, dd0), (dag1, lt1, dd1), (dag2, lt2, dd2)))]
    dz = _dz_assemble(dqkv, dqp)
    own_in, sums_in = pair_reduce([_wgrad_in(u, dz, dgates)] + grads_mixer, names[:4])
    dx, dg_mix, landed_in = _inproj_dx(dz, dgates, dh1, xs, norm_mix_g, wg_in, sums_in)
    g_in, g_ao, g_po, g_out = chip_reduce(own_in, landed_in, names[:4])

    g_grp = jnp.stack([g_bd[k * POOL_GROUP_W:(k + 1) * POOL_GROUP_W, k * POOL_GROUP_W:(k + 1) * POOL_GROUP_W]
                       for k in range(len(POOL_WINDOWS))])
    small = _pack_small(g_grp, dscale, dg_mix, dg_mlp, dg_final, loss_lanes)
    full, small_all = _finish_exchange([g_in, g_ao, g_po, g_out, g_mi, g_mo],
                                       _place_own(small, N_DEV, 2 * chip + core))

    zero = jnp.zeros((D_MODEL,), F32)
    small_w = _pack_small(w_pool_grp[0], pool_scale, norm_mix_g, norm_mlp_g, norm_final_g, zero)
    small_m = _pack_small(m_w_pool_grp[0], m_pool_scale, m_norm_mix_g, m_norm_mlp_g, m_norm_final_g, zero)
    small_v = _pack_small(v_w_pool_grp[0], v_pool_scale, v_norm_mix_g, v_norm_mlp_g, v_norm_final_g, zero)
    sg, sd, sm, sv, loss_tile = _small_sum_adamw(small_all, small_w, small_m, small_v)
    full = [full[0].reshape(CHUNKS_PER_SHARD, D_MODEL, CHUNK)] + list(full[1:])
    upd = [_adamw(w, g, mm, vv, f"adamw_{nm}") for w, g, mm, vv, nm in zip(big, full, big_m, big_v, names)]

    def ordered(small_pack, bigs):
        grp, scale, g_mix, g_mlp, g_f = _unpack_small(small_pack)
        b_in, b_ao, b_po, b_out, b_mi, b_mo = [b[None] for b in bigs]
        return (g_mix, b_in, b_ao, grp, scale, b_po, b_out, g_mlp, b_mi, b_mo, g_f)

    return (loss_tile[0, 0], dx[None],
            *ordered(sg, [t[0] for t in upd]),
            *ordered(sd, [t[1] for t in upd]),
            *ordered(sm, [t[2] for t in upd]),
            *ordered(sv, [t[3] for t in upd]))
```

```python
import functools

import jax
import jax.numpy as jnp
from jax import lax
from jax.experimental import pallas as pl
from jax.experimental.pallas import tpu as pltpu

F32 = jnp.float32
BF16 = jnp.bfloat16
SDS = jax.ShapeDtypeStruct
MESH = pl.DeviceIdType.MESH

D_MODEL = 1024
D_FF = 4096
N_CHIPS = 4
N_DEV = 8
DILATIONS = (1, 4, 16)
BAND = 128
GROUP_W = 256
PAIR_W = 128
HEAD_W = 64
STAT_W = 128
STAT_HEAD_W = 32
POOL_W = 768
POOL_GROUP_W = 192
POOL_WINDOWS = (2, 4, 8, 16)
POOL_HALO = 16
N_IN = 5120
CHUNK = 256
N_CHUNKS = N_IN // CHUNK
N_DZ_CHUNKS = 12
CHUNKS_PER_SHARD = 5
WGRAD_IN_GROUP = 4
NORM_EPS = 1e-6
ALIBI_MAX_BIAS = 8.0
N_HEADS = 12
NEG = -1e30

ADAM_LR, ADAM_B1, ADAM_B2, ADAM_EPS, ADAM_WD, ADAM_STEP = 0.001, 0.9, 0.999, 1e-08, 0.01, 10

TM = 512
TMB = 512
ATT_TILE = ((1, 4), (4, 1), (4, 1))
BK = 4096
ELEMENTWISE_BLOCK = 1 << 20
VMEM_LIMIT = 56 * 1024 * 1024
PACK_ROWS = 184

NT = (((1,), (1,)), ((), ()))
TN = (((0,), (0,)), ((), ()))


def _cp(*sem):
    return pltpu.CompilerParams(dimension_semantics=sem, vmem_limit_bytes=VMEM_LIMIT)


def _resident(shape):
    nd = len(shape)
    return pl.BlockSpec(shape, lambda *_: (0,) * nd, pipeline_mode=pl.Buffered(1))


def _row_block(rows, cap=256):
    return max(b for b in range(16, min(rows, cap) + 1, 16) if rows % b == 0)


def _dot(a, b):
    return jnp.dot(a, b, preferred_element_type=F32)


def _dot_nt(a, b):
    return lax.dot_general(a, b, NT, preferred_element_type=F32)


def _dot_tn(a, b):
    return lax.dot_general(a, b, TN, preferred_element_type=F32)


def _w_in_chunk(w_ref, n):
    return w_ref[n // CHUNKS_PER_SHARD, :, (n % CHUNKS_PER_SHARD) * CHUNK:(n % CHUNKS_PER_SHARD + 1) * CHUNK]


def _sigmoid(x):
    return 0.5 * jnp.tanh(0.5 * x.astype(F32)) + 0.5


def _rms_fwd(x, g):
    r = lax.rsqrt(jnp.mean(x * x, axis=-1, keepdims=True) + NORM_EPS)
    xh = x * r
    return xh * g, xh, r


def _rms_bwd(dy, xh, r, g):
    dxh = dy * g
    return r * (dxh - xh * jnp.mean(dxh * xh, axis=-1, keepdims=True))


def _per_head_lanes(cols):
    rows = cols[0].shape[0]
    lane = lax.broadcasted_iota(jnp.int32, (rows, STAT_W), 1)
    out = cols[3]
    for h in (2, 1, 0):
        out = jnp.where(lane < (h + 1) * STAT_HEAD_W, cols[h], out)
    return out


def _head_col(stat, h):
    return stat[:, h * STAT_HEAD_W:h * STAT_HEAD_W + 1]


def _stat_matrices():
    s = lax.broadcasted_iota(jnp.int32, (STAT_W, GROUP_W), 0)
    c = lax.broadcasted_iota(jnp.int32, (STAT_W, GROUP_W), 1)
    expand = (s == (c // HEAD_W) * STAT_HEAD_W).astype(BF16)
    reduce = (s // STAT_HEAD_W == c // HEAD_W).astype(BF16).T
    return expand, reduce


def _dot_split(x, m):
    hi = x.astype(BF16)
    lo = (x - hi.astype(F32)).astype(BF16)
    return _dot(hi, m) + _dot(lo, m)


def _deinterleave_store(val, s_ref, out_ref, lead, d, rows, dtype):
    if d == 1:
        out_ref[lead + (0,)] = val.astype(dtype)
        return
    for h in range(2):
        s_ref[h] = val[:, h * PAIR_W:(h + 1) * PAIR_W]
    for r in range(d):
        for h in range(2):
            out_ref[lead + (r, slice(None), slice(h * PAIR_W, (h + 1) * PAIR_W))] = (
                s_ref[h, pl.ds(r, rows // d, stride=d), :].astype(dtype))


def _interleave_load(in_ref, lead, s_ref, d, rows):
    for r in range(d):
        for h in range(2):
            s_ref[h, pl.ds(r, rows // d, stride=d), :] = (
                in_ref[lead + (r, slice(None), slice(h * PAIR_W, (h + 1) * PAIR_W))].astype(F32))


def _norm_inproj_own(x, g, w_own, buf):
    S = x.shape[0]
    n_tiles = S // TM

    def body(x_ref, g_ref, w_ref, shard_ref, buf_in, u_ref, z_ref, buf_ref, send_sem, recv_sem):
        i = pl.program_id(0)

        def copies():
            return _relayed_half_copies(shard_ref, buf_ref, w_own.shape[0], send_sem, recv_sem)

        @pl.when(i == 0)
        def _():
            for cpy in copies()[0]:
                cpy.start()

        @pl.when(i == n_tiles // 2)
        def _():
            direct, relays = copies()
            for cpy in direct:
                cpy.wait_recv()
            for cpy in relays:
                cpy.start()

        u = _rms_fwd(x_ref[...], g_ref[...])[0].astype(BF16)
        u_ref[...] = u
        for t in range(CHUNKS_PER_SHARD):
            z_ref[t] = _dot(u, w_ref[:, t * CHUNK:(t + 1) * CHUNK])

        @pl.when(i == n_tiles - 1)
        def _():
            direct, relays = copies()
            for cpy in relays:
                cpy.wait()
            for cpy in direct:
                cpy.wait_send()

    row = lambda w: pl.BlockSpec((TM, w), lambda i: (i, 0))
    return pl.pallas_call(
        body, grid=(n_tiles,), name="norm_inproj_own",
        in_specs=[row(D_MODEL), _resident((1, D_MODEL)), _resident(w_own.shape), ANY, ANY],
        out_specs=[row(D_MODEL), pl.BlockSpec((CHUNKS_PER_SHARD, TM, CHUNK), lambda i: (0, i, 0)), ANY],
        out_shape=[SDS((S, D_MODEL), BF16), SDS((CHUNKS_PER_SHARD, S, CHUNK), F32), SDS(buf.shape, buf.dtype)],
        scratch_shapes=[pltpu.SemaphoreType.DMA((4,)), pltpu.SemaphoreType.DMA((4,))],
        input_output_aliases={4: 2},
        compiler_params=_cp("arbitrary"),
    )(x, g, w_own, w_own, buf)


def _hosted_allgather(i, n_steps, shard_refs, buf_refs, rows, sems):
    send_sem, recv_sem, fsend_sem, frecv_sem = sems
    ici = lambda: _weight_half_copies(shard_refs, buf_refs, rows, send_sem, recv_sem)
    forward = lambda: _pair_forward_copies(buf_refs, rows, fsend_sem, frecv_sem)

    def begin():
        @pl.when(i == 0)
        def _():
            for cpy in ici():
                cpy.start()

        @pl.when(i == n_steps // 2)
        def _():
            for cpy, (fwd, _) in zip(ici(), forward()):
                cpy.wait_recv()
                fwd.start()

    def end():
        @pl.when(i == n_steps - 1)
        def _():
            for cpy, (fwd, landing) in zip(ici(), forward()):
                landing.wait_recv()
                fwd.wait_send()
                cpy.wait_send()

    return begin, end


def _inproj_rest(u, z_own, w_in, shards, bufs):
    S = u.shape[0]
    n_tiles = S // TM
    n = len(shards)

    def body(*refs):
        u_ref, zown_ref, w_ref = refs[0:3]
        shard_refs = refs[3:3 + n]
        q0_ref, q1_ref, q2_ref, pz_ref, gate_ref = refs[3 + 2 * n:8 + 2 * n]
        buf_refs = refs[8 + 2 * n:8 + 3 * n]
        s_ref = refs[8 + 3 * n]
        i = pl.program_id(0)
        chip = 2 * lax.axis_index("x") + lax.axis_index("y")
        begin, end = _hosted_allgather(i, n_tiles, shard_refs, buf_refs, [sh.shape[0] for sh in shards],
                                       refs[9 + 3 * n:])
        begin()

        u = u_ref[...]
        qkv_refs = (q0_ref, q1_ref, q2_ref)

        def emit(k, zc):
            if k < 9:
                which, grp = k // 3, k % 3
                if which == 0:
                    zc = zc * 0.125
                _deinterleave_store(zc, s_ref, qkv_refs[grp], (which,), DILATIONS[grp], TM, BF16)
            elif k < N_DZ_CHUNKS:
                pz_ref[:, (k - 9) * CHUNK:(k - 8) * CHUNK] = zc
            else:
                gate_ref[:, (k - N_DZ_CHUNKS) * CHUNK:(k - N_DZ_CHUNKS + 1) * CHUNK] = zc.astype(BF16)

        def all_chunks(own_shard):
            for k in range(N_CHUNKS):
                if k // CHUNKS_PER_SHARD == own_shard:
                    emit(k, zown_ref[k % CHUNKS_PER_SHARD])
                else:
                    emit(k, _dot(u, _w_in_chunk(w_ref, k)))

        for shard in range(N_CHIPS):
            pl.when(chip == shard)(functools.partial(all_chunks, shard))
        end()

    row = lambda w: pl.BlockSpec((TM, w), lambda i: (i, 0))
    res = pl.pallas_call(
        body, grid=(n_tiles,), name="inproj_rest",
        in_specs=[row(D_MODEL), pl.BlockSpec((CHUNKS_PER_SHARD, TM, CHUNK), lambda i: (0, i, 0)),
                  _resident(w_in.shape)] + [ANY] * (2 * n),
        out_specs=[pl.BlockSpec((3, d, TM // d, GROUP_W), lambda i: (0, 0, i, 0)) for d in DILATIONS]
        + [row(POOL_W), row(2 * D_MODEL)] + [ANY] * n,
        out_shape=[SDS((3, d, S // d, GROUP_W), BF16) for d in DILATIONS]
        + [SDS((S, POOL_W), F32), SDS((S, 2 * D_MODEL), BF16)] + [SDS(b.shape, b.dtype) for b in bufs],
        scratch_shapes=[pltpu.VMEM((2, TM, PAIR_W), F32)] + [pltpu.SemaphoreType.DMA((3 * n,))] * 4,
        input_output_aliases={3 + n + w: 5 + w for w in range(n)},
        compiler_params=_cp("arbitrary"),
    )(u, z_own, w_in, *shards, *bufs)
    return res[:5], res[5:]


def _band_bias(grp, d):
    row = lax.broadcasted_iota(jnp.int32, (BAND, 2 * BAND), 0)
    col = lax.broadcasted_iota(jnp.int32, (BAND, 2 * BAND), 1)
    steps = BAND + row - col
    valid = (steps >= 0) & (steps <= BAND)
    stepsf = (steps * d).astype(F32)
    biases = []
    for hh in range(4):
        slope = 2.0 ** (-ALIBI_MAX_BIAS * (grp * 4 + hh + 1) / N_HEADS)
        biases.append(jnp.where(valid, -slope * stepsf, NEG))
    return biases, col


def _attn_tiles(grp, L):
    rr, rb = ATT_TILE[grp]
    rb = min(rb, L // BAND)
    return rr, rb, L // (rb * BAND)


def _kv_tile(cur_ref, prev_ref, rr, rb, cs):
    if rb == 0:
        return jnp.concatenate([prev_ref[rr, :, cs], cur_ref[rr, 0:BAND, cs]], axis=0)
    return cur_ref[rr, (rb - 1) * BAND:(rb + 1) * BAND, cs]


def _attn_fwd(qkv, grp):
    d = DILATIONS[grp]
    L = qkv.shape[2]
    RR, RB, nb = _attn_tiles(grp, L)

    def body(q_ref, kc_ref, kp_ref, vc_ref, vp_ref, o_ref, lse_ref):
        i = pl.program_id(0)
        biases, col = _band_bias(grp, d)
        first_keys_ok = (col >= BAND) | (i > 0)
        is_a = lax.broadcasted_iota(jnp.int32, (BAND, PAIR_W), 1) < HEAD_W
        heads = [(rr, rb, cp, h2) for rr in range(RR) for rb in range(RB) for cp in range(2) for h2 in range(2)]

        def tile(head):
            rr, rb, cp, _ = head
            return rr, rb, slice(rb * BAND, (rb + 1) * BAND), slice(cp * PAIR_W, (cp + 1) * PAIR_W)

        def scores(head):
            rr, rb, rows, cs = tile(head)
            q2 = q_ref[rr, rows, cs]
            b = biases[head[2] * 2 + head[3]]
            if rb == 0:
                b = jnp.where(first_keys_ok, b, NEG)
            sel = is_a if head[3] == 0 else jnp.logical_not(is_a)
            return _dot_nt(jnp.where(sel, q2, jnp.zeros_like(q2)), _kv_tile(kc_ref, kp_ref, rr, rb, cs)) + b

        s_next = scores(heads[0])
        outs, lses = {}, {}
        for idx, head in enumerate(heads):
            s = s_next
            if idx + 1 < len(heads):
                s_next = scores(heads[idx + 1])
            rr, rb, rows, cs = tile(head)
            m = jnp.max(s, axis=-1, keepdims=True)
            p = jnp.exp(s - m)
            l = jnp.sum(p, axis=-1, keepdims=True)
            outs[head[3]] = _dot(p.astype(BF16), _kv_tile(vc_ref, vp_ref, rr, rb, cs)) * (1.0 / l)
            lses[head[2] * 2 + head[3]] = m + jnp.log(l)
            if head[3] == 1:
                o_ref[rr, rows, cs] = jnp.where(is_a, outs[0], outs[1]).astype(BF16)
            if head[2] == 1 and head[3] == 1:
                lse_ref[rr, rows, :] = _per_head_lanes(lses)

    cur = lambda w: pl.BlockSpec((None, RR, RB * BAND, GROUP_W), lambda i, j: (w, j, i, 0))
    prev = lambda w: pl.BlockSpec((None, RR, BAND, GROUP_W), lambda i, j: (w, j, jnp.maximum(i * RB - 1, 0), 0))
    return pl.pallas_call(
        body, grid=(nb, d // RR), name=f"attn_fwd_g{grp}",
        in_specs=[cur(0), cur(1), prev(1), cur(2), prev(2)],
        out_specs=[pl.BlockSpec((RR, RB * BAND, GROUP_W), lambda i, j: (j, i, 0)),
                   pl.BlockSpec((RR, RB * BAND, STAT_W), lambda i, j: (j, i, 0))],
        out_shape=[SDS((d, L, GROUP_W), BF16), SDS((d, L, STAT_W), F32)],
        compiler_params=_cp("parallel", "parallel"),
    )(qkv, qkv, qkv, qkv, qkv)


def _pool_column_select(col, vals):
    return jnp.where(col < POOL_GROUP_W, vals[0],
                     jnp.where(col < 2 * POOL_GROUP_W, vals[1],
                               jnp.where(col < 3 * POOL_GROUP_W, vals[2], vals[3])))


def _pool_inv_count(i, rows):
    t = i * rows + lax.broadcasted_iota(jnp.int32, (rows, POOL_W), 0)
    col = lax.broadcasted_iota(jnp.int32, (rows, POOL_W), 1)
    win = _pool_column_select(col, POOL_WINDOWS)
    return 1.0 / jnp.minimum(t + 1, win).astype(F32), col


def _mixer_out(outs, lses, pz, gates, x, w_ao, w_po, wbd, scale, w_out, g_mlp, expand, shards, bufs):
    S = x.shape[0]
    n_tiles = S // TMB
    n = len(shards)

    def body(*refs):
        (o0_ref, l0_ref, o1_ref, l1_ref, o2_ref, l2_ref, pz_ref, halo_ref, gate_ref, x_ref,
         wao_ref, wpo_ref, wbd_ref, sc_ref, wout_ref, g_ref, expand_ref) = refs[0:17]
        shard_refs = refs[17:17 + n]
        (a_ref, lt0_ref, lt1_ref, lt2_ref, pooled_ref, mixed_ref, p_ref, merged_ref, h1_ref,
         m_ref) = refs[17 + 2 * n:27 + 2 * n]
        buf_refs = refs[27 + 2 * n:27 + 3 * n]
        so1, sl1, so2, sl2, slt, ext_ref = refs[27 + 3 * n:33 + 3 * n]
        i = pl.program_id(0)
        begin, end = _hosted_allgather(i, n_tiles, shard_refs, buf_refs, [sh.shape[0] for sh in shards],
                                       refs[33 + 3 * n:])
        begin()
        _interleave_load(o1_ref, (), so1, DILATIONS[1], TMB)
        _interleave_load(o2_ref, (), so2, DILATIONS[2], TMB)
        for ref, sref, d in ((l1_ref, sl1, DILATIONS[1]), (l2_ref, sl2, DILATIONS[2])):
            for r in range(d):
                sref[0, pl.ds(r, TMB // d, stride=d), :] = ref[r]
        l0, l1, l2 = l0_ref[0], sl1[0], sl2[0]
        mx = jnp.maximum(jnp.maximum(l0, l1), l2)
        e0, e1, e2 = jnp.exp(l0 - mx), jnp.exp(l1 - mx), jnp.exp(l2 - mx)
        den = e0 + e1 + e2
        inv = 1.0 / den
        slt[0] = mx + jnp.log(den)
        w0, w1, w2 = [_dot_split(e * inv, expand_ref[...]) for e in (e0, e1, e2)]
        for h in range(2):
            hs = slice(h * PAIR_W, (h + 1) * PAIR_W)
            a_ref[:, hs] = (w0[:, hs] * o0_ref[0, :, hs].astype(F32) + w1[:, hs] * so1[h]
                            + w2[:, hs] * so2[h]).astype(BF16)
        lt0_ref[0] = slt[0]
        for ref, d in ((lt1_ref, DILATIONS[1]), (lt2_ref, DILATIONS[2])):
            for r in range(d):
                ref[r] = slt[0, pl.ds(r, TMB // d, stride=d), :]

        pz_t = pz_ref[...]
        ext_ref[0:POOL_HALO, :] = jnp.where(i > 0, halo_ref[...], 0.0)
        ext_ref[POOL_HALO:, :] = pz_t
        sums = []
        acc = ext_ref[...]
        for k in (1, 2, 4, 8):
            acc = acc + pltpu.roll(acc, k, 0)
            sums.append(acc[POOL_HALO:, :])
        inv_cnt, col = _pool_inv_count(i, TMB)
        pooled = (_pool_column_select(col, sums) * inv_cnt - pz_t).astype(BF16)
        pooled_ref[...] = pooled
        mixed = _dot(pooled, wbd_ref[...])
        mixed_ref[...] = mixed.astype(BF16)
        p = (mixed * sc_ref[...]).astype(BF16)
        p_ref[...] = p

        a = a_ref[...]
        for j in range(N_CHIPS):
            js = slice(j * CHUNK, (j + 1) * CHUNK)
            ga = gate_ref[:, js]
            gp = gate_ref[:, D_MODEL + j * CHUNK:D_MODEL + (j + 1) * CHUNK]
            mj = _sigmoid(ga) * _dot(a, wao_ref[j]) + _sigmoid(gp) * _dot(p, wpo_ref[j])
            merged_ref[:, js] = mj.astype(BF16)
        h1 = x_ref[...] + _dot(merged_ref[...], wout_ref[...])
        h1_ref[...] = h1
        m_ref[...] = _rms_fwd(h1, g_ref[...])[0].astype(BF16)
        end()

    row = lambda w: pl.BlockSpec((TMB, w), lambda i: (i, 0))
    grp_spec = lambda d: pl.BlockSpec((d, TMB // d, GROUP_W), lambda i: (0, i, 0))
    stat_spec = lambda d: pl.BlockSpec((d, TMB // d, STAT_W), lambda i: (0, i, 0))
    halo = pl.BlockSpec((POOL_HALO, POOL_W), lambda i: (jnp.maximum(i * (TMB // POOL_HALO) - 1, 0), 0))
    d0, d1, d2 = DILATIONS
    pair_scratch = pltpu.VMEM((2, TMB, PAIR_W), F32)
    stat_scratch = pltpu.VMEM((1, TMB, STAT_W), F32)
    res = pl.pallas_call(
        body, grid=(n_tiles,), name="mixer_out",
        in_specs=[grp_spec(d0), stat_spec(d0), grp_spec(d1), stat_spec(d1), grp_spec(d2), stat_spec(d2),
                  row(POOL_W), halo, row(2 * D_MODEL), row(D_MODEL),
                  _resident(w_ao.shape), _resident(w_po.shape), _resident(wbd.shape), _resident(scale.shape),
                  _resident(w_out.shape), _resident(g_mlp.shape), _resident(expand.shape)] + [ANY] * (2 * n),
        out_specs=[row(GROUP_W), stat_spec(d0), stat_spec(d1), stat_spec(d2),
                   row(POOL_W), row(POOL_W), row(POOL_W), row(D_MODEL), row(D_MODEL), row(D_MODEL)] + [ANY] * n,
        out_shape=[SDS((S, GROUP_W), BF16)] + [SDS((d, S // d, STAT_W), F32) for d in DILATIONS]
        + [SDS((S, POOL_W), BF16), SDS((S, POOL_W), BF16), SDS((S, POOL_W), BF16),
           SDS((S, D_MODEL), BF16), SDS((S, D_MODEL), F32), SDS((S, D_MODEL), BF16)]
        + [SDS(b.shape, b.dtype) for b in bufs],
        scratch_shapes=[pair_scratch, stat_scratch, pair_scratch, stat_scratch, stat_scratch,
                        pltpu.VMEM((TMB + POOL_HALO, POOL_W), F32)] + [pltpu.SemaphoreType.DMA((3 * n,))] * 4,
        input_output_aliases={17 + n + w: 10 + w for w in range(n)},
        compiler_params=_cp("arbitrary"),
    )(outs[0], lses[0], outs[1], lses[1], outs[2], lses[2], pz, pz, gates, x,
      w_ao, w_po, wbd, scale, w_out, g_mlp, expand, *shards, *bufs)
    return res[:10], res[10:]


def _mlp_fwd_loss(m, h1, target, w_mi, w_mo, g_f):
    S = m.shape[0]

    def body(m_ref, h1_ref, t_ref, wmi_ref, wmo_ref, g_ref, hid_ref, dh2_ref, dh2b_ref, loss_ref, dg_ref):
        @pl.when(pl.program_id(0) == 0)
        def _():
            loss_ref[...] = jnp.zeros_like(loss_ref)
            dg_ref[...] = jnp.zeros_like(dg_ref)

        mt = m_ref[...]
        acc = h1_ref[...]
        for c in range(N_CHIPS):
            hid = jnp.square(jnp.maximum(_dot(mt, wmi_ref[c]), 0.0)).astype(BF16)
            hid_ref[:, c * D_MODEL:(c + 1) * D_MODEL] = hid
            acc = acc + _dot(hid, wmo_ref[c])
        g = g_ref[...]
        y, hh, r = _rms_fwd(acc, g)
        e = y - t_ref[...]
        loss_ref[...] += jnp.sum(e * e, axis=0, keepdims=True)
        dy = e * (1.0 / D_MODEL)
        dg_ref[...] += jnp.sum(dy * hh, axis=0, keepdims=True)
        dh2 = _rms_bwd(dy, hh, r, g)
        dh2_ref[...] = dh2
        dh2b_ref[...] = dh2.astype(BF16)

    row = lambda w: pl.BlockSpec((TM, w), lambda i: (i, 0))
    vec = pl.BlockSpec((1, D_MODEL), lambda i: (0, 0))
    return pl.pallas_call(
        body, grid=(S // TM,), name="mlp_fwd_loss",
        in_specs=[row(D_MODEL), row(D_MODEL), row(D_MODEL), _resident(w_mi.shape), _resident(w_mo.shape),
                  _resident(g_f.shape)],
        out_specs=[row(D_FF), row(D_MODEL), row(D_MODEL), vec, vec],
        out_shape=[SDS((S, D_FF), BF16), SDS((S, D_MODEL), F32), SDS((S, D_MODEL), BF16),
                   SDS((1, D_MODEL), F32), SDS((1, D_MODEL), F32)],
        compiler_params=_cp("arbitrary"),
    )(m, h1, target, w_mi, w_mo, g_f)


def _mlp_bwd(dh2, dh2b, hid, h1, w_mi, w_mo, g_mlp):
    S = dh2.shape[0]

    def body(dh2_ref, dh2b_ref, hid_ref, h1_ref, wmi_ref, wmo_ref, g_ref, dpre_ref, dh1_ref, dh1b_ref, dg_ref):
        @pl.when(pl.program_id(0) == 0)
        def _():
            dg_ref[...] = jnp.zeros_like(dg_ref)

        d2 = dh2b_ref[...]
        dm = jnp.zeros((TM, D_MODEL), F32)
        dhid_next = _dot_nt(d2, wmo_ref[0])
        for c in range(N_CHIPS):
            cs = slice(c * D_MODEL, (c + 1) * D_MODEL)
            dhid = dhid_next
            if c + 1 < N_CHIPS:
                dhid_next = _dot_nt(d2, wmo_ref[c + 1])
            dpre = (dhid * (2.0 * jnp.sqrt(hid_ref[:, cs].astype(F32)))).astype(BF16)
            dpre_ref[:, cs] = dpre
            dm = dm + _dot_nt(dpre, wmi_ref[c])
        g = g_ref[...]
        _, hh, r = _rms_fwd(h1_ref[...], g)
        dg_ref[...] += jnp.sum(dm * hh, axis=0, keepdims=True)
        dh1 = dh2_ref[...] + _rms_bwd(dm, hh, r, g)
        dh1_ref[...] = dh1
        dh1b_ref[...] = dh1.astype(BF16)

    row = lambda w: pl.BlockSpec((TM, w), lambda i: (i, 0))
    return pl.pallas_call(
        body, grid=(S // TM,), name="mlp_bwd",
        in_specs=[row(D_MODEL), row(D_MODEL), row(D_FF), row(D_MODEL), _resident(w_mi.shape),
                  _resident(w_mo.shape), _resident(g_mlp.shape)],
        out_specs=[row(D_FF), row(D_MODEL), row(D_MODEL), pl.BlockSpec((1, D_MODEL), lambda i: (0, 0))],
        out_shape=[SDS((S, D_FF), BF16), SDS((S, D_MODEL), F32), SDS((S, D_MODEL), BF16), SDS((1, D_MODEL), F32)],
        compiler_params=_cp("arbitrary"),
    )(dh2, dh2b, hid, h1, w_mi, w_mo, g_mlp)


def _mixer_bwd(dh1b, a, p, mixed, gates, w_out, w_ao, w_po, wbd, scale, head_ones, sums):
    S = a.shape[0]
    n_tiles = S // TMB
    n = len(sums)

    def body(*refs):
        (dh1b_ref, a_ref, p_ref, mixed_ref, gate_ref, wout_ref, wao_ref, wpo_ref, wbd_ref, sc_ref,
         ones_ref) = refs[0:11]
        sum_refs = refs[11:11 + n]
        (da1_ref, dp1_ref, dgate_ref, da0_ref, dag1_ref, dag2_ref, dd0_ref, dd1_ref, dd2_ref,
         dmixed_ref, dqp_ref, dscale_ref) = refs[11 + n:23 + n]
        land_refs = refs[23 + n:23 + 2 * n]
        s_da, s_dd, send_sem, recv_sem = refs[23 + 2 * n:]
        i = pl.program_id(0)

        @pl.when(i == 0)
        def _():
            dscale_ref[...] = jnp.zeros_like(dscale_ref)
            for cpy in _chip_sum_copies(sum_refs, land_refs, send_sem, recv_sem):
                cpy.start()

        dmerged = _dot_nt(dh1b_ref[...], wout_ref[...])
        a = a_ref[...]
        p = p_ref[...]
        da = jnp.zeros((TMB, GROUP_W), F32)
        dp = jnp.zeros((TMB, POOL_W), F32)
        for j in range(N_CHIPS):
            js = slice(j * CHUNK, (j + 1) * CHUNK)
            sa = _sigmoid(gate_ref[:, js])
            sp = _sigmoid(gate_ref[:, D_MODEL + j * CHUNK:D_MODEL + (j + 1) * CHUNK])
            dmj = dmerged[:, js]
            da1 = (dmj * sa).astype(BF16)
            dp1 = (dmj * sp).astype(BF16)
            da1_ref[:, js] = da1
            dp1_ref[:, js] = dp1
            dgate_ref[j] = (dmj * _dot(a, wao_ref[j]) * sa * (1.0 - sa)).astype(BF16)
            dgate_ref[N_CHIPS + j] = (dmj * _dot(p, wpo_ref[j]) * sp * (1.0 - sp)).astype(BF16)
            da = da + _dot_nt(da1, wao_ref[j])
            dp = dp + _dot_nt(dp1, wpo_ref[j])

        dd = _dot_split(da * a.astype(F32), ones_ref[...])
        da0_ref[0] = da.astype(BF16)
        dd0_ref[0] = dd
        for h in range(2):
            s_da[h] = da[:, h * PAIR_W:(h + 1) * PAIR_W]
        s_dd[0] = dd
        for refs, d in (((dag1_ref, dd1_ref), DILATIONS[1]), ((dag2_ref, dd2_ref), DILATIONS[2])):
            for r in range(d):
                for h in range(2):
                    hs = slice(h * PAIR_W, (h + 1) * PAIR_W)
                    refs[0][r, :, hs] = s_da[h, pl.ds(r, TMB // d, stride=d), :].astype(BF16)
                refs[1][r] = s_dd[0, pl.ds(r, TMB // d, stride=d), :]

        sc = sc_ref[...]
        dscale_ref[...] += jnp.sum(dp * mixed_ref[...].astype(F32), axis=0, keepdims=True)
        dmixed = (dp * sc).astype(BF16)
        dmixed_ref[...] = dmixed
        inv_cnt, _ = _pool_inv_count(i, TMB)
        dqp_ref[...] = (_dot_nt(dmixed, wbd_ref[...]) * inv_cnt).astype(BF16)

        @pl.when(i == n_tiles - 1)
        def _():
            for cpy in _chip_sum_copies(sum_refs, land_refs, send_sem, recv_sem):
                cpy.wait()

    row = lambda w: pl.BlockSpec((TMB, w), lambda i: (i, 0))
    grp_spec = lambda d: pl.BlockSpec((d, TMB // d, GROUP_W), lambda i: (0, i, 0))
    stat_spec = lambda d: pl.BlockSpec((d, TMB // d, STAT_W), lambda i: (0, i, 0))
    d0, d1, d2 = DILATIONS
    res = pl.pallas_call(
        body, grid=(n_tiles,), name="mixer_bwd",
        in_specs=[row(D_MODEL), row(GROUP_W), row(POOL_W), row(POOL_W), row(2 * D_MODEL),
                  _resident(w_out.shape), _resident(w_ao.shape), _resident(w_po.shape), _resident(wbd.shape),
                  _resident(scale.shape), _resident(head_ones.shape)] + [ANY] * n,
        out_specs=[row(D_MODEL), row(D_MODEL), pl.BlockSpec((2 * N_CHIPS, TMB, CHUNK), lambda i: (0, i, 0)),
                   grp_spec(d0), grp_spec(d1), grp_spec(d2), stat_spec(d0), stat_spec(d1), stat_spec(d2),
                   row(POOL_W), row(POOL_W), pl.BlockSpec((1, POOL_W), lambda i: (0, 0))] + [ANY] * n,
        out_shape=[SDS((S, D_MODEL), BF16), SDS((S, D_MODEL), BF16), SDS((2 * N_CHIPS, S, CHUNK), BF16)]
        + [SDS((d, S // d, GROUP_W), BF16) for d in DILATIONS]
        + [SDS((d, S // d, STAT_W), F32) for d in DILATIONS]
        + [SDS((S, POOL_W), BF16), SDS((S, POOL_W), BF16), SDS((1, POOL_W), F32)]
        + [SDS(t.shape, t.dtype) for t in sums],
        scratch_shapes=[pltpu.VMEM((2, TMB, PAIR_W), F32), pltpu.VMEM((1, TMB, STAT_W), F32),
                        pltpu.SemaphoreType.DMA((3 * n,)), pltpu.SemaphoreType.DMA((3 * n,))],
        compiler_params=_cp("arbitrary"),
    )(dh1b, a, p, mixed, gates, w_out, w_ao, w_po, wbd, scale, head_ones, *sums)
    return res[:12], res[12:]


def _attn_bwd(qkv, da, lt, dd, grp):
    d = DILATIONS[grp]
    L = qkv.shape[2]
    RR, RB, nb = _attn_tiles(grp, L)

    def body(q_ref, kc_ref, kp_ref, vc_ref, vp_ref, da_ref, lt_ref, dd_ref, dq_ref, dk_ref, dv_ref, dk_acc, dv_acc):
        i = pl.program_id(1)

        @pl.when(i == 0)
        def _():
            dk_acc[...] = jnp.zeros_like(dk_acc)
            dv_acc[...] = jnp.zeros_like(dv_acc)

        def compute(cur, prv):
            dk_acc[cur] = jnp.zeros((RR, RB * BAND, GROUP_W), F32)
            dv_acc[cur] = jnp.zeros((RR, RB * BAND, GROUP_W), F32)
            biases, col = _band_bias(grp, d)
            first_keys_ok = (col >= BAND) | (i > 0)
            is_a = lax.broadcasted_iota(jnp.int32, (BAND, PAIR_W), 1) < HEAD_W
            for rr in range(RR):
                for rb in range(RB):
                    rows = slice(rb * BAND, (rb + 1) * BAND)
                    for cp in range(2):
                        cs = slice(cp * PAIR_W, (cp + 1) * PAIR_W)
                        q2 = q_ref[rr, rows, cs]
                        da2 = da_ref[rr, rows, cs]
                        lt2 = lt_ref[rr, rows, :]
                        dd2 = dd_ref[rr, rows, :]
                        kcat = _kv_tile(kc_ref, kp_ref, rr, rb, cs)
                        vcat = _kv_tile(vc_ref, vp_ref, rr, rb, cs)
                        q2t = q2.astype(F32).T.astype(BF16)
                        da2t = da2.astype(F32).T.astype(BF16)
                        dqs, dkts, dvts, scores, dpvs = [], [], [], [], []
                        for h2 in range(2):
                            sel = is_a if h2 == 0 else jnp.logical_not(is_a)
                            b = biases[cp * 2 + h2]
                            if rb == 0:
                                b = jnp.where(first_keys_ok, b, NEG)
                            scores.append(_dot_nt(jnp.where(sel, q2, jnp.zeros_like(q2)), kcat) + b)
                            dpvs.append(_dot_nt(jnp.where(sel, da2, jnp.zeros_like(da2)), vcat))
                        for h2 in range(2):
                            lane0 = h2 * HEAD_W
                            p = jnp.exp(scores[h2] - _head_col(lt2, cp * 2 + h2))
                            ds = (p * (dpvs[h2] - _head_col(dd2, cp * 2 + h2))).astype(BF16)
                            dqs.append(_dot(ds, kcat))
                            dkts.append(_dot(q2t[lane0:lane0 + HEAD_W, :], ds))
                            dvts.append(_dot(da2t[lane0:lane0 + HEAD_W, :], p.astype(BF16)))
                        dq_ref[rr, rows, cs] = (jnp.where(is_a, dqs[0], dqs[1]) * 0.125).astype(BF16)
                        dkc = jnp.concatenate(dkts, axis=0).T
                        dvc = jnp.concatenate(dvts, axis=0).T
                        if rb == 0:
                            last = slice((RB - 1) * BAND, RB * BAND)
                            dk_acc[prv, rr, last, cs] += dkc[0:BAND]
                            dv_acc[prv, rr, last, cs] += dvc[0:BAND]
                            dk_acc[cur, rr, 0:BAND, cs] += dkc[BAND:]
                            dv_acc[cur, rr, 0:BAND, cs] += dvc[BAND:]
                        else:
                            both = slice((rb - 1) * BAND, (rb + 1) * BAND)
                            dk_acc[cur, rr, both, cs] += dkc
                            dv_acc[cur, rr, both, cs] += dvc

        def flush(prv):
            dk_ref[...] = dk_acc[prv].astype(BF16)
            dv_ref[...] = dv_acc[prv].astype(BF16)

        for parity in (0, 1):
            on = (i % 2) == parity
            pl.when(on & (i < nb))(functools.partial(compute, parity, 1 - parity))
            pl.when(on & (i > 0))(functools.partial(flush, 1 - parity))

    qi = lambda i: jnp.minimum(i, nb - 1)
    cur_w = lambda w: pl.BlockSpec((None, RR, RB * BAND, GROUP_W), lambda j, i: (w, j, qi(i), 0))
    prev_w = lambda w: pl.BlockSpec((None, RR, BAND, GROUP_W),
                                    lambda j, i: (w, j, jnp.maximum(qi(i) * RB - 1, 0), 0))
    blk = pl.BlockSpec((RR, RB * BAND, GROUP_W), lambda j, i: (j, qi(i), 0))
    stat_blk = pl.BlockSpec((RR, RB * BAND, STAT_W), lambda j, i: (j, qi(i), 0))
    late = pl.BlockSpec((RR, RB * BAND, GROUP_W), lambda j, i: (j, jnp.maximum(i - 1, 0), 0))
    return pl.pallas_call(
        body, grid=(d // RR, nb + 1), name=f"attn_bwd_g{grp}",
        in_specs=[cur_w(0), cur_w(1), prev_w(1), cur_w(2), prev_w(2), blk, stat_blk, stat_blk],
        out_specs=[blk, late, late],
        out_shape=[SDS((d, L, GROUP_W), BF16)] * 3,
        scratch_shapes=[pltpu.VMEM((2, RR, RB * BAND, GROUP_W), F32), pltpu.VMEM((2, RR, RB * BAND, GROUP_W), F32)],
        compiler_params=_cp("parallel", "arbitrary"),
    )(qkv, qkv, qkv, qkv, qkv, da, lt, dd)


def _dz_assemble(dqkv, dqp):
    S = dqp.shape[0]
    n_tiles = S // TMB

    def body(*refs):
        dqkv_refs = refs[0:9]
        dqp_ref, halo_ref = refs[9:11]
        dz_ref, s_ref, ext_ref = refs[11:]
        i = pl.program_id(0)

        for grp in range(3):
            for which in range(3):
                n = which * 3 + grp
                ref = dqkv_refs[grp * 3 + which]
                if DILATIONS[grp] == 1:
                    dz_ref[n] = ref[0]
                else:
                    _interleave_load(ref, (), s_ref, DILATIONS[grp], TMB)
                    for h in range(2):
                        dz_ref[n, :, h * PAIR_W:(h + 1) * PAIR_W] = s_ref[h].astype(BF16)

        dqp = dqp_ref[...].astype(F32)
        ext_ref[0:TMB, :] = dqp
        ext_ref[TMB:, :] = jnp.where(i < n_tiles - 1, halo_ref[...].astype(F32), 0.0)
        sums = []
        acc = ext_ref[...]
        for k in (1, 2, 4, 8):
            acc = acc + pltpu.roll(acc, TMB + POOL_HALO - k, 0)
            sums.append(acc[0:TMB, :])
        inv_cnt, col = _pool_inv_count(i, TMB)
        dpz = _pool_column_select(col, sums) - dqp / inv_cnt
        for t in range(3):
            dz_ref[9 + t] = dpz[:, t * CHUNK:(t + 1) * CHUNK].astype(BF16)

    row = lambda w: pl.BlockSpec((TMB, w), lambda i: (i, 0))
    grp_spec = lambda d: pl.BlockSpec((d, TMB // d, GROUP_W), lambda i: (0, i, 0))
    halo = pl.BlockSpec((POOL_HALO, POOL_W),
                        lambda i: (jnp.minimum((i + 1) * (TMB // POOL_HALO), S // POOL_HALO - 1), 0))
    flat = [t for grp in range(3) for t in dqkv[grp]]
    return pl.pallas_call(
        body, grid=(n_tiles,), name="dz_assemble",
        in_specs=[grp_spec(DILATIONS[grp]) for grp in range(3) for _ in range(3)] + [row(POOL_W), halo],
        out_specs=pl.BlockSpec((N_DZ_CHUNKS, TMB, CHUNK), lambda i: (0, i, 0)),
        out_shape=SDS((N_DZ_CHUNKS, S, CHUNK), BF16),
        scratch_shapes=[pltpu.VMEM((2, TMB, PAIR_W), F32), pltpu.VMEM((TMB + POOL_HALO, POOL_W), F32)],
        compiler_params=_cp("parallel"),
    )(*flat, dqp, dqp)


def _inproj_dx(dz, dgates, dh1, x, g, w_in, sums):
    S = x.shape[0]
    n_tiles = S // TM
    n = len(sums)

    def body(*refs):
        dz_ref, dgate_ref, dh1_ref, x_ref, g_ref, w_ref = refs[0:6]
        sum_refs = refs[6:6 + n]
        dx_ref, dg_ref = refs[6 + n:8 + n]
        land_refs = refs[8 + n:8 + 2 * n]
        sems = refs[8 + 2 * n:]
        i = pl.program_id(0)

        def copies():
            return _chip_sum_copies(sum_refs, land_refs, *sems)

        @pl.when(i == 0)
        def _():
            dg_ref[...] = jnp.zeros_like(dg_ref)
            for cpy in copies():
                cpy.start()

        du = jnp.zeros((TM, D_MODEL), F32)
        for k in range(N_CHUNKS):
            dzk = dz_ref[k] if k < N_DZ_CHUNKS else dgate_ref[k - N_DZ_CHUNKS]
            du = du + _dot_nt(dzk, _w_in_chunk(w_ref, k))
        gv = g_ref[...]
        _, xh, r = _rms_fwd(x_ref[...], gv)
        dg_ref[...] += jnp.sum(du * xh, axis=0, keepdims=True)
        dx_ref[...] = dh1_ref[...] + _rms_bwd(du, xh, r, gv)

        @pl.when(i == n_tiles - 1)
        def _():
            for cpy in copies():
                cpy.wait()

    row = lambda w: pl.BlockSpec((TM, w), lambda i: (i, 0))
    res = pl.pallas_call(
        body, grid=(n_tiles,), name="inproj_dx",
        in_specs=[pl.BlockSpec((N_DZ_CHUNKS, TM, CHUNK), lambda i: (0, i, 0)),
                  pl.BlockSpec((N_CHUNKS - N_DZ_CHUNKS, TM, CHUNK), lambda i: (0, i, 0)),
                  row(D_MODEL), row(D_MODEL), _resident(g.shape), _resident(w_in.shape)] + [ANY] * n,
        out_specs=[row(D_MODEL), pl.BlockSpec((1, D_MODEL), lambda i: (0, 0))] + [ANY] * n,
        out_shape=[SDS((S, D_MODEL), F32), SDS((1, D_MODEL), F32)] + [SDS(t.shape, t.dtype) for t in sums],
        scratch_shapes=[pltpu.SemaphoreType.DMA((3 * n,)), pltpu.SemaphoreType.DMA((3 * n,))],
        compiler_params=_cp("arbitrary"),
    )(dz, dgates, dh1, x, g, w_in, *sums)
    return res[0], res[1], res[2:]


def _wgrad(a, b, name, *, out_shape, a_spec, b_spec, out_spec, grid, n_out_cols=None, fill=None, narrow=True):
    k_axis = len(grid) - 1
    n_k = grid[k_axis]
    n_out = 2 if narrow else 1

    def body(a_ref, b_ref, *rest):
        o_ref = rest[-n_out]

        @pl.when(pl.program_id(k_axis) == 0)
        def _():
            o_ref[...] = jnp.zeros_like(o_ref)

        at = a_ref[...]
        if n_out_cols is None:
            o_ref[...] += _dot_tn(at, b_ref[...])
        elif n_out_cols[0] == "lead_both":
            for t in range(b_ref.shape[0]):
                o_ref[t] += _dot_tn(at, b_ref[t])
        else:
            w = n_out_cols[1]
            for t in range(o_ref.shape[0]):
                o_ref[t] += _dot_tn(at, b_ref[:, t * w:(t + 1) * w])

        if narrow:
            @pl.when(pl.program_id(k_axis) == n_k - 1)
            def _():
                rest[-1][...] = o_ref[...].astype(BF16)

    sem = ("parallel",) * k_axis + ("arbitrary",)
    extra = [] if fill is None else list(fill) if narrow else [fill]
    shapes = [out_shape, SDS(out_shape.shape, BF16)] if narrow else out_shape
    return pl.pallas_call(body, grid=grid, name=name, in_specs=[a_spec, b_spec] + [ANY] * len(extra),
                          out_specs=[out_spec] * n_out if narrow else out_spec, out_shape=shapes,
                          input_output_aliases={2 + t: t for t in range(len(extra))},
                          compiler_params=_cp(*sem))(a, b, *extra)


def _wgrad_in(u, dz, dgates):
    bk = min(BK, u.shape[0])
    nk = u.shape[0] // bk
    g = WGRAD_IN_GROUP
    kw = dict(n_out_cols=("lead_both", CHUNK), a_spec=pl.BlockSpec((bk, D_MODEL), lambda j, k: (k, 0)),
              b_spec=pl.BlockSpec((g, bk, CHUNK), lambda j, k: (j, k, 0)),
              out_shape=SDS((N_CHUNKS, D_MODEL, CHUNK), F32))
    first = _wgrad(u, dz, "wgrad_in_qkvp", grid=(N_DZ_CHUNKS // g, nk),
                   out_spec=pl.BlockSpec((g, D_MODEL, CHUNK), lambda j, k: (j, 0, 0)), **kw)
    both = _wgrad(u, dgates, "wgrad_in_gates", grid=((N_CHUNKS - N_DZ_CHUNKS) // g, nk), fill=first,
                  out_spec=pl.BlockSpec((g, D_MODEL, CHUNK), lambda j, k: (N_DZ_CHUNKS // g + j, 0, 0)), **kw)
    return [t.reshape(N_CHIPS, CHUNKS_PER_SHARD * D_MODEL, CHUNK) for t in both]


def _wgrads_mixer(a, da1, p, dp1, merged, dh1b, pooled, dmixed):
    bk = min(BK, a.shape[0])
    nk = a.shape[0] // bk
    g_ao = _wgrad(
        a, da1, "wgrad_att_out", grid=(nk,), n_out_cols=("cols_b", CHUNK),
        a_spec=pl.BlockSpec((bk, GROUP_W), lambda k: (k, 0)),
        b_spec=pl.BlockSpec((bk, D_MODEL), lambda k: (k, 0)),
        out_spec=pl.BlockSpec((N_CHIPS, GROUP_W, CHUNK), lambda k: (0, 0, 0)),
        out_shape=SDS((N_CHIPS, GROUP_W, CHUNK), F32))
    g_po = _wgrad(
        p, dp1, "wgrad_pool_out", grid=(nk,), n_out_cols=("cols_b", CHUNK),
        a_spec=pl.BlockSpec((bk, POOL_W), lambda k: (k, 0)),
        b_spec=pl.BlockSpec((bk, D_MODEL), lambda k: (k, 0)),
        out_spec=pl.BlockSpec((N_CHIPS, POOL_W, CHUNK), lambda k: (0, 0, 0)),
        out_shape=SDS((N_CHIPS, POOL_W, CHUNK), F32))
    g_out = _wgrad(
        merged, dh1b, "wgrad_out", grid=(nk,),
        a_spec=pl.BlockSpec((bk, D_MODEL), lambda k: (k, 0)),
        b_spec=pl.BlockSpec((bk, D_MODEL), lambda k: (k, 0)),
        out_spec=pl.BlockSpec((D_MODEL, D_MODEL), lambda k: (0, 0)),
        out_shape=SDS((D_MODEL, D_MODEL), F32))
    g_bd = _wgrad(
        pooled, dmixed, "wgrad_pool_grp", grid=(nk,),
        a_spec=pl.BlockSpec((bk, POOL_W), lambda k: (k, 0)),
        b_spec=pl.BlockSpec((bk, POOL_W), lambda k: (k, 0)),
        out_spec=pl.BlockSpec((POOL_W, POOL_W), lambda k: (0, 0)),
        out_shape=SDS((POOL_W, POOL_W), F32), narrow=False)
    g_out = [t.reshape(N_CHIPS, D_MODEL // N_CHIPS, D_MODEL) for t in g_out]
    return [g_ao, g_po, g_out], g_bd


def _wgrads_mlp(m, dpre, hid, dh2b):
    bk = min(BK, m.shape[0])
    nk = m.shape[0] // bk
    g_mi = _wgrad(
        m, dpre, "wgrad_mlp_in", grid=(N_CHIPS, nk),
        a_spec=pl.BlockSpec((bk, D_MODEL), lambda c, k: (k, 0)),
        b_spec=pl.BlockSpec((bk, D_MODEL), lambda c, k: (k, c)),
        out_spec=pl.BlockSpec((None, D_MODEL, D_MODEL), lambda c, k: (c, 0, 0)),
        out_shape=SDS((N_CHIPS, D_MODEL, D_MODEL), F32))
    g_mo = _wgrad(
        hid, dh2b, "wgrad_mlp_out", grid=(N_CHIPS, nk),
        a_spec=pl.BlockSpec((bk, D_MODEL), lambda c, k: (k, c)),
        b_spec=pl.BlockSpec((bk, D_MODEL), lambda c, k: (k, 0)),
        out_spec=pl.BlockSpec((None, D_MODEL, D_MODEL), lambda c, k: (c, 0, 0)),
        out_shape=SDS((N_CHIPS, D_MODEL, D_MODEL), F32))
    return [g_mi, g_mo]


def _mesh_place():
    x, y, c = lax.axis_index("x"), lax.axis_index("y"), lax.axis_index("c")
    other_chips = [(x, 1 - y), (1 - x, y), (1 - x, 1 - y)]
    return x, y, c, other_chips


ANY = pl.BlockSpec(memory_space=pl.ANY)


def _weight_half_copies(shard_refs, buf_refs, rows, send_sem, recv_sem):
    x, y, c, chips = _mesh_place()
    me = 2 * x + y
    copies = []
    for w, r_full in enumerate(rows):
        rh = r_full // 2
        for r, (px, py) in enumerate(chips):
            k = w * 3 + r
            copies.append(pltpu.make_async_remote_copy(
                src_ref=shard_refs[w].at[pl.ds(c * rh, rh), :], dst_ref=buf_refs[w].at[me, pl.ds(c * rh, rh), :],
                send_sem=send_sem.at[k], recv_sem=recv_sem.at[k], device_id=(px, py, c), device_id_type=MESH))
    return copies


def _relayed_half_copies(shard_ref, buf_ref, n_rows, send_sem, recv_sem):
    x, y, c, _ = _mesh_place()
    y_nb, x_nb = (x, 1 - y), (1 - x, y)
    rh = n_rows // 2
    part = rh // 2
    slot = lambda chip: 2 * chip[0] + chip[1]

    def copy(k, src, dst, to):
        return pltpu.make_async_remote_copy(src_ref=src, dst_ref=dst, send_sem=send_sem.at[k], recv_sem=recv_sem.at[k],
                                            device_id=(to[0], to[1], c), device_id_type=MESH)

    mine_src = shard_ref.at[pl.ds(c * rh, rh), :]
    mine_dst = buf_ref.at[slot((x, y)), pl.ds(c * rh, rh), :]
    direct = [copy(0, mine_src, mine_dst, y_nb), copy(1, mine_src, mine_dst, x_nb)]
    first = buf_ref.at[slot(x_nb), pl.ds(c * rh, part), :]
    second = buf_ref.at[slot(y_nb), pl.ds(c * rh + part, part), :]
    relays = [copy(2, first, first, y_nb), copy(3, second, second, x_nb)]
    return direct, relays


def _pair_forward_copies(buf_refs, rows, send_sem, recv_sem):
    x, y, c, chips = _mesh_place()
    out = []
    for w, r_full in enumerate(rows):
        rh = r_full // 2
        for r, (px, py) in enumerate(chips):
            k = w * 3 + r
            landed = buf_refs[w].at[2 * px + py, pl.ds(c * rh, rh), :]
            theirs = buf_refs[w].at[2 * px + py, pl.ds((1 - c) * rh, rh), :]
            mk = lambda ref: pltpu.make_async_remote_copy(
                src_ref=ref, dst_ref=ref, send_sem=send_sem.at[k], recv_sem=recv_sem.at[k],
                device_id=(x, y, 1 - c), device_id_type=MESH)
            out.append((mk(landed), mk(theirs)))
    return out


def _place_own(block, n_slots, slot):
    buf = lax.empty((n_slots,) + block.shape, block.dtype)
    return lax.dynamic_update_slice(buf, block[None], (slot,) + (0,) * block.ndim)


def _pair_forward(bufs, rows, name):
    n = len(bufs)

    def body(*refs):
        dst = refs[n:2 * n]
        send_sem, recv_sem = refs[2 * n:]
        fwds = _pair_forward_copies(dst, rows, send_sem, recv_sem)
        for fwd, _ in fwds:
            fwd.start()
        for fwd, landing in fwds:
            landing.wait_recv()
            fwd.wait_send()

    return pl.pallas_call(
        body, name=name,
        in_specs=[ANY] * n, out_specs=[ANY] * n,
        out_shape=[SDS(b.shape, b.dtype) for b in bufs],
        scratch_shapes=[pltpu.SemaphoreType.DMA((3 * n,))] * 2,
        input_output_aliases={w: w for w in range(n)},
    )(*bufs)


def _chip_sum_copies(src, dst, send_sem, recv_sem):
    x, y, c, chips = _mesh_place()
    copies = []
    for w in range(len(src)):
        for r, (px, py) in enumerate(chips):
            k = w * 3 + r
            copies.append(pltpu.make_async_remote_copy(
                src_ref=src[w].at[r + 1], dst_ref=dst[w].at[r + 1], send_sem=send_sem.at[k], recv_sem=recv_sem.at[k],
                device_id=(px, py, c), device_id_type=MESH))
    return copies


def _pair_exchange(grads):
    n = len(grads)

    def body(*refs):
        src, dst = refs[:n], refs[n:2 * n]
        send_sem, recv_sem = refs[2 * n:]
        x, y, c, _ = _mesh_place()
        copies = []
        for w in range(n):
            rh = grads[w].shape[1] // 2
            copies.append(pltpu.make_async_remote_copy(
                src_ref=src[w].at[:, pl.ds((1 - c) * rh, rh), :], dst_ref=dst[w],
                send_sem=send_sem.at[w], recv_sem=recv_sem.at[w],
                device_id=(x, y, 1 - c), device_id_type=MESH))
            copies[-1].start()
        for cpy in copies:
            cpy.wait()

    return pl.pallas_call(
        body, name="grad_pair_exchange",
        in_specs=[ANY] * n, out_specs=[ANY] * n,
        out_shape=[SDS((N_CHIPS, g.shape[1] // 2, g.shape[2]), g.dtype) for g in grads],
        scratch_shapes=[pltpu.SemaphoreType.DMA((n,)), pltpu.SemaphoreType.DMA((n,))],
    )(*grads)


def _pair_sum(place, grad, recv, name):
    _, R, C = grad.shape
    rh = R // 2
    br = _row_block(rh, max(256, ELEMENTWISE_BLOCK // C))
    nbh = rh // br

    def body(place_ref, g_ref, r_ref, own_ref, sums_ref):
        s = g_ref[...] + r_ref[...].astype(F32)

        @pl.when(pl.program_id(1) == 0)
        def _():
            own_ref[...] = s

        sums_ref[...] = s.astype(BF16)

    slot = lambda rel, pr: jnp.bitwise_xor(pr[0], rel)
    return pl.pallas_call(
        body, name=name,
        grid_spec=pltpu.PrefetchScalarGridSpec(
            num_scalar_prefetch=1, grid=(nbh, N_CHIPS),
            in_specs=[pl.BlockSpec((None, br, C), lambda i, rel, pr: (slot(rel, pr), pr[1] * nbh + i, 0)),
                      pl.BlockSpec((None, br, C), lambda i, rel, pr: (slot(rel, pr), i, 0))],
            out_specs=[pl.BlockSpec((br, C), lambda i, rel, pr: (i, 0)),
                       pl.BlockSpec((None, br, C), lambda i, rel, pr: (rel, i, 0))]),
        out_shape=[SDS((rh, C), F32), SDS((N_CHIPS, rh, C), BF16)],
        compiler_params=_cp("parallel", "arbitrary"),
    )(place, grad, recv)


def _chip_sum(place, own, recv, name):
    rh, C = own.shape
    br = _row_block(rh, max(256, ELEMENTWISE_BLOCK // C))
    nbh = rh // br

    def body(place_ref, own_ref, r_ref, o_ref):
        o_ref[...] = ((own_ref[...] + r_ref[1].astype(F32)) + r_ref[2].astype(F32)) + r_ref[3].astype(F32)

    return pl.pallas_call(
        body, name=name,
        grid_spec=pltpu.PrefetchScalarGridSpec(
            num_scalar_prefetch=1, grid=(nbh,),
            in_specs=[pl.BlockSpec((br, C), lambda i, pr: (i, 0)),
                      pl.BlockSpec((N_CHIPS, br, C), lambda i, pr: (0, i, 0))],
            out_specs=pl.BlockSpec((br, C), lambda i, pr: (pr[1] * nbh + i, 0))),
        out_shape=SDS((2 * rh, C), F32),
        compiler_params=_cp("parallel"),
    )(place, own, recv)


def _finish_exchange(grads, small_all):
    n = len(grads)

    def body(*refs):
        dst, all_ref = refs[n + 1:2 * n + 1], refs[2 * n + 1]
        send_sem, recv_sem, ssend_sem, srecv_sem = refs[2 * n + 2:]
        x, y, c, chips = _mesh_place()
        sib = (x, y, 1 - c)

        def pack(dev, k, to):
            slot = 4 * dev[0] + 2 * dev[1] + dev[2]
            return pltpu.make_async_remote_copy(
                src_ref=all_ref.at[slot], dst_ref=all_ref.at[slot], send_sem=ssend_sem.at[k],
                recv_sem=srecv_sem.at[k], device_id=to, device_id_type=MESH)

        pack_copies = [pack((x, y, c), 0, sib)] + [pack((x, y, c), 1 + r, (px, py, c))
                                                   for r, (px, py) in enumerate(chips)]
        for cpy in pack_copies:
            cpy.start()
        sends, landings = [], []
        for w in range(n):
            rh = grads[w].shape[0] // 2
            mk = lambda cc: pltpu.make_async_remote_copy(
                src_ref=dst[w].at[pl.ds(cc * rh, rh), :], dst_ref=dst[w].at[pl.ds(cc * rh, rh), :],
                send_sem=send_sem.at[w], recv_sem=recv_sem.at[w], device_id=(x, y, 1 - c), device_id_type=MESH)
            sends.append(mk(c))
            landings.append(mk(1 - c))
            sends[-1].start()
        for r, (px, py) in enumerate(chips):
            pack((px, py, c), 1 + r, (px, py, c)).wait_recv()
            pack_copies.append(pack((px, py, c), 4 + r, sib))
            pack_copies[-1].start()
        pack(sib, 0, sib).wait_recv()
        for r, (px, py) in enumerate(chips):
            pack((px, py, 1 - c), 4 + r, sib).wait_recv()
        for cpy in landings:
            cpy.wait_recv()
        for cpy in sends + pack_copies:
            cpy.wait_send()

    res = pl.pallas_call(
        body, name="grad_finish_exchange",
        in_specs=[ANY] * (n + 1), out_specs=[ANY] * (n + 1),
        out_shape=[SDS(g.shape, g.dtype) for g in grads] + [SDS(small_all.shape, small_all.dtype)],
        scratch_shapes=[pltpu.SemaphoreType.DMA((n,)), pltpu.SemaphoreType.DMA((n,)),
                        pltpu.SemaphoreType.DMA((N_DEV - 1,)), pltpu.SemaphoreType.DMA((N_DEV - 1,))],
        input_output_aliases={w: w for w in range(n + 1)},
    )(*grads, small_all)
    return res[:n], res[n]


def _adamw_math(w, g, m, v):
    m = ADAM_B1 * m + (1.0 - ADAM_B1) * g
    v = ADAM_B2 * v + (1.0 - ADAM_B2) * jnp.square(g)
    m_hat = m / (1.0 - ADAM_B1 ** ADAM_STEP)
    v_hat = v / (1.0 - ADAM_B2 ** ADAM_STEP)
    delta = -ADAM_LR * (m_hat / (jnp.sqrt(v_hat) + ADAM_EPS) + ADAM_WD * w)
    return delta, m, v


def _adamw(w, g, m, v, name):
    R, C = w.shape
    br = _row_block(R, 512)
    if g.ndim == 3:
        n_chunks, cw = g.shape[0], g.shape[2]
        g_spec = pl.BlockSpec((None, br, cw), lambda t, i: (t, i, 0))
    else:
        n_chunks, cw = 1, C
        g_spec = pl.BlockSpec((br, cw), lambda t, i: (i, t))

    def body(w_ref, g_ref, m_ref, v_ref, g_out_ref, d_ref, nm_ref, nv_ref):
        gv = g_ref[...]
        g_out_ref[...] = gv
        d_ref[...], nm_ref[...], nv_ref[...] = _adamw_math(w_ref[...], gv, m_ref[...], v_ref[...])

    spec = pl.BlockSpec((br, cw), lambda t, i: (i, t))
    return pl.pallas_call(
        body, grid=(n_chunks, R // br), name=name, in_specs=[spec, g_spec, spec, spec], out_specs=[spec] * 4,
        out_shape=[SDS((R, C), F32)] * 4, compiler_params=_cp("parallel", "parallel"),
    )(w, g, m, v)


def _small_sum_adamw(all_small, w, m, v):
    loss_row = PACK_ROWS - 8

    def body(all_ref, w_ref, m_ref, v_ref, g_ref, d_ref, nm_ref, nv_ref, loss_ref):
        g = all_ref[0]
        for k in range(1, N_DEV):
            g = g + all_ref[k]
        g_ref[...] = g
        d_ref[...], nm_ref[...], nv_ref[...] = _adamw_math(w_ref[...], g, m_ref[...], v_ref[...])
        total = jnp.sum(g[loss_row:loss_row + 1, :]) * (0.5 / D_MODEL)
        loss_ref[...] = jnp.full(loss_ref.shape, total, F32)

    full = lambda s: pl.BlockSpec(s, lambda i: (0,) * len(s))
    pack = (PACK_ROWS, D_MODEL)
    return pl.pallas_call(
        body, grid=(1,), name="small_sum_adamw",
        in_specs=[full((N_DEV,) + pack), full(pack), full(pack), full(pack)],
        out_specs=[full(pack)] * 4 + [full((8, 128))],
        out_shape=[SDS(pack, F32)] * 4 + [SDS((8, 128), F32)],
        compiler_params=_cp("arbitrary"),
    )(all_small, w, m, v)


def _pack_small(grp, scale, g_mix, g_mlp, g_f, loss_lanes):
    def part(vec):
        vec = vec.reshape(1, -1)
        return jnp.pad(vec, ((0, 7), (0, D_MODEL - vec.shape[1])))
    return jnp.concatenate([grp.reshape(-1, D_MODEL), part(scale), part(g_mix), part(g_mlp), part(g_f),
                            part(loss_lanes)], axis=0)


def _unpack_small(pack):
    n_grp = len(POOL_WINDOWS) * POOL_GROUP_W * POOL_GROUP_W // D_MODEL
    grp = pack[:n_grp].reshape(1, len(POOL_WINDOWS), POOL_GROUP_W, POOL_GROUP_W)
    scale = pack[n_grp, :POOL_W].reshape(1, POOL_W)
    g_mix = pack[n_grp + 8].reshape(1, D_MODEL)
    g_mlp = pack[n_grp + 16].reshape(1, D_MODEL)
    g_f = pack[n_grp + 24].reshape(D_MODEL)
    return grp, scale, g_mix, g_mlp, g_f


def _block_diag(grp):
    out = jnp.zeros((POOL_W, POOL_W), grp.dtype)
    for k in range(len(POOL_WINDOWS)):
        out = lax.dynamic_update_slice(out, grp[k], (k * POOL_GROUP_W, k * POOL_GROUP_W))
    return out


def kernel(x, norm_mix_g, w_in, w_att_out, w_pool_grp, pool_scale, w_pool_out, w_out, norm_mlp_g, w_mlp_in, w_mlp_out, norm_final_g, loss_target, m_norm_mix_g, m_w_in, m_w_att_out, m_w_pool_grp, m_pool_scale, m_w_pool_out, m_w_out, m_norm_mlp_g, m_w_mlp_in, m_w_mlp_out, m_norm_final_g, v_norm_mix_g, v_w_in, v_w_att_out, v_w_pool_grp, v_pool_scale, v_w_pool_out, v_w_out, v_norm_mlp_g, v_w_mlp_in, v_w_mlp_out, v_norm_final_g):
    S = x.shape[1]
    xs, target = x[0], loss_target[0]
    big = [w_in[0], w_att_out[0], w_pool_out[0], w_out[0], w_mlp_in[0], w_mlp_out[0]]
    big_m = [m_w_in[0], m_w_att_out[0], m_w_pool_out[0], m_w_out[0], m_w_mlp_in[0], m_w_mlp_out[0]]
    big_v = [v_w_in[0], v_w_att_out[0], v_w_pool_out[0], v_w_out[0], v_w_mlp_in[0], v_w_mlp_out[0]]

    chip = 2 * lax.axis_index("x") + lax.axis_index("y")
    core = lax.axis_index("c")
    place = jnp.stack([chip, core]).astype(jnp.int32)
    names = ("w_in", "w_att_out", "w_pool_out", "w_out", "w_mlp_in", "w_mlp_out")

    shards = [w.astype(BF16) for w in big]
    bufs = [_place_own(sh, N_CHIPS, chip) for sh in shards]
    wbd = _block_diag(w_pool_grp[0]).astype(BF16)
    g_final = norm_final_g.reshape(1, D_MODEL)
    stat_expand, stat_reduce = _stat_matrices()

    u, z_own, landed_in = _norm_inproj_own(xs, norm_mix_g, shards[0], bufs[0])
    (wg_in,) = _pair_forward([landed_in], [shards[0].shape[0]], "w_in_pair_forward")
    (qkv0, qkv1, qkv2, pz, gates), (wg_ao, wg_po, wg_out) = _inproj_rest(u, z_own, wg_in, shards[1:4], bufs[1:4])
    wg_out = wg_out.reshape(D_MODEL, D_MODEL)
    qkv = (qkv0, qkv1, qkv2)
    att = [_attn_fwd(qkv[grp], grp) for grp in range(3)]
    (a, lt0, lt1, lt2, pooled, mixed, p, merged, h1, m), (wg_mi, wg_mo) = _mixer_out(
        [o for o, _ in att], [l for _, l in att], pz, gates, xs, wg_ao, wg_po, wbd, pool_scale, wg_out, norm_mlp_g,
        stat_expand, shards[4:], bufs[4:])
    hid, dh2, dh2b, loss_lanes, dg_final = _mlp_fwd_loss(m, h1, target, wg_mi, wg_mo, g_final)

    def pair_reduce(grads, grad_names):
        recv = _pair_exchange([narrow for _, narrow in grads])
        pair = [_pair_sum(place, g, r, f"pair_sum_{nm}") for (g, _), r, nm in zip(grads, recv, grad_names)]
        return [own for own, _ in pair], [s for _, s in pair]

    def chip_reduce(owns, landed_sums, grad_names):
        return [_chip_sum(place, own, r, f"chip_sum_{nm}") for own, r, nm in zip(owns, landed_sums, grad_names)]

    dpre, dh1, dh1b, dg_mlp = _mlp_bwd(dh2, dh2b, hid, h1, wg_mi, wg_mo, norm_mlp_g)
    own_mlp, sums_mlp = pair_reduce(_wgrads_mlp(m, dpre, hid, dh2b), names[4:])
    (da1, dp1, dgates, da0, dag1, dag2, dd0, dd1, dd2, dmixed, dqp, dscale), landed_mlp = _mixer_bwd(
        dh1b, a, p, mixed, gates, wg_out, wg_ao, wg_po, wbd, pool_scale, stat_reduce, sums_mlp)
    g_mi, g_mo = chip_reduce(own_mlp, landed_mlp, names[4:])
    grads_mixer, g_bd = _wgrads_mixer(a, da1, p, dp1, merged, dh1b, pooled, dmixed)
    dqkv = [_attn_bwd(qkv[grp], da_g, lt_g, dd_g, grp)
            for grp, (da_g, lt_g, dd_g) in enumerate(((da0, lt0, dd0), (dag1, lt1, dd1), (dag2, lt2, dd2)))]
    dz = _dz_assemble(dqkv, dqp)
    own_in, sums_in = pair_reduce([_wgrad_in(u, dz, dgates)] + grads_mixer, names[:4])
    dx, dg_mix, landed_in = _inproj_dx(dz, dgates, dh1, xs, norm_mix_g, wg_in, sums_in)
    g_in, g_ao, g_po, g_out = chip_reduce(own_in, landed_in, names[:4])

    g_grp = jnp.stack([g_bd[k * POOL_GROUP_W:(k + 1) * POOL_GROUP_W, k * POOL_GROUP_W:(k + 1) * POOL_GROUP_W]
                       for k in range(len(POOL_WINDOWS))])
    small = _pack_small(g_grp, dscale, dg_mix, dg_mlp, dg_final, loss_lanes)
    full, small_all = _finish_exchange([g_in, g_ao, g_po, g_out, g_mi, g_mo],
                                       _place_own(small, N_DEV, 2 * chip + core))

    zero = jnp.zeros((D_MODEL,), F32)
    small_w = _pack_small(w_pool_grp[0], pool_scale, norm_mix_g, norm_mlp_g, norm_final_g, zero)
    small_m = _pack_small(m_w_pool_grp[0], m_pool_scale, m_norm_mix_g, m_norm_mlp_g, m_norm_final_g, zero)
    small_v = _pack_small(v_w_pool_grp[0], v_pool_scale, v_norm_mix_g, v_norm_mlp_g, v_norm_final_g, zero)
    sg, sd, sm, sv, loss_tile = _small_sum_adamw(small_all, small_w, small_m, small_v)
    full = [full[0].reshape(CHUNKS_PER_SHARD, D_MODEL, CHUNK)] + list(full[1:])
    upd = [_adamw(w, g, mm, vv, f"adamw_{nm}") for w, g, mm, vv, nm in zip(big, full, big_m, big_v, names)]

    def ordered(small_pack, bigs):
        grp, scale, g_mix, g_mlp, g_f = _unpack_small(small_pack)
        b_in, b_ao, b_po, b_out, b_mi, b_mo = [b[None] for b in bigs]
        return (g_mix, b_in, b_ao, grp, scale, b_po, b_out, g_mlp, b_mi, b_mo, g_f)

    return (loss_tile[0, 0], dx[None],
            *ordered(sg, [t[0] for t in upd]),
            *ordered(sd, [t[1] for t in upd]),
            *ordered(sm, [t[2] for t in upd]),
            *ordered(sv, [t[3] for t in upd]))
```

```python
import functools

import jax
import jax.numpy as jnp
from jax import lax
from jax.experimental import pallas as pl
from jax.experimental.pallas import tpu as pltpu

F32 = jnp.float32
BF16 = jnp.bfloat16
SDS = jax.ShapeDtypeStruct
MESH = pl.DeviceIdType.MESH

D_MODEL = 1024
D_FF = 4096
N_CHIPS = 4
N_DEV = 8
DILATIONS = (1, 4, 16)
BAND = 128
GROUP_W = 256
PAIR_W = 128
HEAD_W = 64
STAT_W = 128
STAT_HEAD_W = 32
POOL_W = 768
POOL_GROUP_W = 192
POOL_WINDOWS = (2, 4, 8, 16)
POOL_HALO = 16
N_IN = 5120
CHUNK = 256
N_CHUNKS = N_IN // CHUNK
N_DZ_CHUNKS = 12
CHUNKS_PER_SHARD = 5
WGRAD_IN_GROUP = 4
NORM_EPS = 1e-6
ALIBI_MAX_BIAS = 8.0
N_HEADS = 12
NEG = -1e30

ADAM_LR, ADAM_B1, ADAM_B2, ADAM_EPS, ADAM_WD, ADAM_STEP = 0.001, 0.9, 0.999, 1e-08, 0.01, 10

TM = 512
TMB = 512
ATT_TILE = ((1, 4), (4, 1), (4, 1))
BK = 4096
ELEMENTWISE_BLOCK = 1 << 20
VMEM_LIMIT = 56 * 1024 * 1024
PACK_ROWS = 184

NT = (((1,), (1,)), ((), ()))
TN = (((0,), (0,)), ((), ()))


def _cp(*sem):
    return pltpu.CompilerParams(dimension_semantics=sem, vmem_limit_bytes=VMEM_LIMIT)


def _resident(shape):
    nd = len(shape)
    return pl.BlockSpec(shape, lambda *_: (0,) * nd, pipeline_mode=pl.Buffered(1))


def _row_block(rows, cap=256):
    return max(b for b in range(16, min(rows, cap) + 1, 16) if rows % b == 0)


def _dot(a, b):
    return jnp.dot(a, b, preferred_element_type=F32)


def _dot_nt(a, b):
    return lax.dot_general(a, b, NT, preferred_element_type=F32)


def _dot_tn(a, b):
    return lax.dot_general(a, b, TN, preferred_element_type=F32)


def _w_in_chunk(w_ref, n):
    return w_ref[n // CHUNKS_PER_SHARD, :, (n % CHUNKS_PER_SHARD) * CHUNK:(n % CHUNKS_PER_SHARD + 1) * CHUNK]


def _sigmoid(x):
    return 0.5 * jnp.tanh(0.5 * x.astype(F32)) + 0.5


def _rms_fwd(x, g):
    r = lax.rsqrt(jnp.mean(x * x, axis=-1, keepdims=True) + NORM_EPS)
    xh = x * r
    return xh * g, xh, r


def _rms_bwd(dy, xh, r, g):
    dxh = dy * g
    return r * (dxh - xh * jnp.mean(dxh * xh, axis=-1, keepdims=True))


def _per_head_lanes(cols):
    rows = cols[0].shape[0]
    lane = lax.broadcasted_iota(jnp.int32, (rows, STAT_W), 1)
    out = cols[3]
    for h in (2, 1, 0):
        out = jnp.where(lane < (h + 1) * STAT_HEAD_W, cols[h], out)
    return out


def _head_col(stat, h):
    return stat[:, h * STAT_HEAD_W:h * STAT_HEAD_W + 1]


def _stat_matrices():
    s = lax.broadcasted_iota(jnp.int32, (STAT_W, GROUP_W), 0)
    c = lax.broadcasted_iota(jnp.int32, (STAT_W, GROUP_W), 1)
    expand = (s == (c // HEAD_W) * STAT_HEAD_W).astype(BF16)
    reduce = (s // STAT_HEAD_W == c // HEAD_W).astype(BF16).T
    return expand, reduce


def _dot_split(x, m):
    hi = x.astype(BF16)
    lo = (x - hi.astype(F32)).astype(BF16)
    return _dot(hi, m) + _dot(lo, m)


def _deinterleave_store(val, s_ref, out_ref, lead, d, rows, dtype):
    if d == 1:
        out_ref[lead + (0,)] = val.astype(dtype)
        return
    for h in range(2):
        s_ref[h] = val[:, h * PAIR_W:(h + 1) * PAIR_W]
    for r in range(d):
        for h in range(2):
            out_ref[lead + (r, slice(None), slice(h * PAIR_W, (h + 1) * PAIR_W))] = (
                s_ref[h, pl.ds(r, rows // d, stride=d), :].astype(dtype))


def _interleave_load(in_ref, lead, s_ref, d, rows):
    for r in range(d):
        for h in range(2):
            s_ref[h, pl.ds(r, rows // d, stride=d), :] = (
                in_ref[lead + (r, slice(None), slice(h * PAIR_W, (h + 1) * PAIR_W))].astype(F32))


def _norm_inproj_own(x, g, w_own, buf):
    S = x.shape[0]
    n_tiles = S // TM

    def body(x_ref, g_ref, w_ref, shard_ref, buf_in, u_ref, z_ref, buf_ref, send_sem, recv_sem):
        i = pl.program_id(0)

        def copies():
            return _weight_half_copies([shard_ref], [buf_ref], [w_own.shape[0]], send_sem, recv_sem)

        @pl.when(i == 0)
        def _():
            for cpy in copies():
                cpy.start()

        u = _rms_fwd(x_ref[...], g_ref[...])[0].astype(BF16)
        u_ref[...] = u
        for t in range(CHUNKS_PER_SHARD):
            z_ref[t] = _dot(u, w_ref[:, t * CHUNK:(t + 1) * CHUNK]).astype(BF16)

        @pl.when(i == n_tiles - 1)
        def _():
            for cpy in copies():
                cpy.wait()

    row = lambda w: pl.BlockSpec((TM, w), lambda i: (i, 0))
    return pl.pallas_call(
        body, grid=(n_tiles,), name="norm_inproj_own",
        in_specs=[row(D_MODEL), _resident((1, D_MODEL)), _resident(w_own.shape), ANY, ANY],
        out_specs=[row(D_MODEL), pl.BlockSpec((CHUNKS_PER_SHARD, TM, CHUNK), lambda i: (0, i, 0)), ANY],
        out_shape=[SDS((S, D_MODEL), BF16), SDS((CHUNKS_PER_SHARD, S, CHUNK), BF16), SDS(buf.shape, buf.dtype)],
        scratch_shapes=[pltpu.SemaphoreType.DMA((3,)), pltpu.SemaphoreType.DMA((3,))],
        input_output_aliases={4: 2},
        compiler_params=_cp("arbitrary"),
    )(x, g, w_own, w_own, buf)


def _hosted_allgather(i, n_steps, shard_refs, buf_refs, rows, sems):
    send_sem, recv_sem, fsend_sem, frecv_sem = sems
    ici = lambda: _weight_half_copies(shard_refs, buf_refs, rows, send_sem, recv_sem)
    forward = lambda: _pair_forward_copies(buf_refs, rows, fsend_sem, frecv_sem)

    def begin():
        @pl.when(i == 0)
        def _():
            for cpy in ici():
                cpy.start()

        @pl.when(i == n_steps // 2)
        def _():
            for cpy, (fwd, _) in zip(ici(), forward()):
                cpy.wait_recv()
                fwd.start()

    def end():
        @pl.when(i == n_steps - 1)
        def _():
            for cpy, (fwd, landing) in zip(ici(), forward()):
                landing.wait_recv()
                fwd.wait_send()
                cpy.wait_send()

    return begin, end


def _inproj_rest(u, z_own, w_in, shards, bufs):
    S = u.shape[0]
    n_tiles = S // TM
    n = len(shards)

    def body(*refs):
        u_ref, zown_ref, w_ref = refs[0:3]
        shard_refs = refs[3:3 + n]
        q0_ref, q1_ref, q2_ref, pz_ref, gate_ref = refs[3 + 2 * n:8 + 2 * n]
        buf_refs = refs[8 + 2 * n:8 + 3 * n]
        s_ref = refs[8 + 3 * n]
        i = pl.program_id(0)
        chip = 2 * lax.axis_index("x") + lax.axis_index("y")
        begin, end = _hosted_allgather(i, n_tiles, shard_refs, buf_refs, [sh.shape[0] for sh in shards],
                                       refs[9 + 3 * n:])
        begin()

        u = u_ref[...]
        qkv_refs = (q0_ref, q1_ref, q2_ref)

        def emit(k, zc):
            if k < 9:
                which, grp = k // 3, k % 3
                if which == 0:
                    zc = zc * 0.125
                _deinterleave_store(zc, s_ref, qkv_refs[grp], (which,), DILATIONS[grp], TM, BF16)
            elif k < N_DZ_CHUNKS:
                pz_ref[:, (k - 9) * CHUNK:(k - 8) * CHUNK] = zc.astype(BF16)
            else:
                gate_ref[:, (k - N_DZ_CHUNKS) * CHUNK:(k - N_DZ_CHUNKS + 1) * CHUNK] = zc.astype(BF16)

        def all_chunks(own_shard):
            for k in range(N_CHUNKS):
                if k // CHUNKS_PER_SHARD == own_shard:
                    emit(k, zown_ref[k % CHUNKS_PER_SHARD].astype(F32))
                else:
                    emit(k, _dot(u, _w_in_chunk(w_ref, k)))

        for shard in range(N_CHIPS):
            pl.when(chip == shard)(functools.partial(all_chunks, shard))
        end()

    row = lambda w: pl.BlockSpec((TM, w), lambda i: (i, 0))
    res = pl.pallas_call(
        body, grid=(n_tiles,), name="inproj_rest",
        in_specs=[row(D_MODEL), pl.BlockSpec((CHUNKS_PER_SHARD, TM, CHUNK), lambda i: (0, i, 0)),
                  _resident(w_in.shape)] + [ANY] * (2 * n),
        out_specs=[pl.BlockSpec((3, d, TM // d, GROUP_W), lambda i: (0, 0, i, 0)) for d in DILATIONS]
        + [row(POOL_W), row(2 * D_MODEL)] + [ANY] * n,
        out_shape=[SDS((3, d, S // d, GROUP_W), BF16) for d in DILATIONS]
        + [SDS((S, POOL_W), BF16), SDS((S, 2 * D_MODEL), BF16)] + [SDS(b.shape, b.dtype) for b in bufs],
        scratch_shapes=[pltpu.VMEM((2, TM, PAIR_W), F32)] + [pltpu.SemaphoreType.DMA((3 * n,))] * 4,
        input_output_aliases={3 + n + w: 5 + w for w in range(n)},
        compiler_params=_cp("arbitrary"),
    )(u, z_own, w_in, *shards, *bufs)
    return res[:5], res[5:]


def _band_bias(grp, d):
    row = lax.broadcasted_iota(jnp.int32, (BAND, 2 * BAND), 0)
    col = lax.broadcasted_iota(jnp.int32, (BAND, 2 * BAND), 1)
    steps = BAND + row - col
    valid = (steps >= 0) & (steps <= BAND)
    stepsf = (steps * d).astype(F32)
    biases = []
    for hh in range(4):
        slope = 2.0 ** (-ALIBI_MAX_BIAS * (grp * 4 + hh + 1) / N_HEADS)
        biases.append(jnp.where(valid, -slope * stepsf, NEG))
    return biases, col


def _attn_tiles(grp, L):
    rr, rb = ATT_TILE[grp]
    rb = min(rb, L // BAND)
    return rr, rb, L // (rb * BAND)


def _kv_tile(cur_ref, prev_ref, rr, rb, cs):
    if rb == 0:
        return jnp.concatenate([prev_ref[rr, :, cs], cur_ref[rr, 0:BAND, cs]], axis=0)
    return cur_ref[rr, (rb - 1) * BAND:(rb + 1) * BAND, cs]


def _attn_fwd(qkv, grp):
    d = DILATIONS[grp]
    L = qkv.shape[2]
    RR, RB, nb = _attn_tiles(grp, L)

    def body(q_ref, kc_ref, kp_ref, vc_ref, vp_ref, o_ref, lse_ref):
        i = pl.program_id(0)
        biases, col = _band_bias(grp, d)
        first_keys_ok = (col >= BAND) | (i > 0)
        is_a = lax.broadcasted_iota(jnp.int32, (BAND, PAIR_W), 1) < HEAD_W
        heads = [(rr, rb, cp, h2) for rr in range(RR) for rb in range(RB) for cp in range(2) for h2 in range(2)]

        def tile(head):
            rr, rb, cp, _ = head
            return rr, rb, slice(rb * BAND, (rb + 1) * BAND), slice(cp * PAIR_W, (cp + 1) * PAIR_W)

        def scores(head):
            rr, rb, rows, cs = tile(head)
            q2 = q_ref[rr, rows, cs]
            b = biases[head[2] * 2 + head[3]]
            if rb == 0:
                b = jnp.where(first_keys_ok, b, NEG)
            sel = is_a if head[3] == 0 else jnp.logical_not(is_a)
            return _dot_nt(jnp.where(sel, q2, jnp.zeros_like(q2)), _kv_tile(kc_ref, kp_ref, rr, rb, cs)) + b

        s_next = scores(heads[0])
        outs, lses = {}, {}
        for idx, head in enumerate(heads):
            s = s_next
            if idx + 1 < len(heads):
                s_next = scores(heads[idx + 1])
            rr, rb, rows, cs = tile(head)
            m = jnp.max(s, axis=-1, keepdims=True)
            p = jnp.exp(s - m)
            l = jnp.sum(p, axis=-1, keepdims=True)
            outs[head[3]] = _dot(p.astype(BF16), _kv_tile(vc_ref, vp_ref, rr, rb, cs)) * (1.0 / l)
            lses[head[2] * 2 + head[3]] = m + jnp.log(l)
            if head[3] == 1:
                o_ref[rr, rows, cs] = jnp.where(is_a, outs[0], outs[1]).astype(BF16)
            if head[2] == 1 and head[3] == 1:
                lse_ref[rr, rows, :] = _per_head_lanes(lses)

    cur = lambda w: pl.BlockSpec((None, RR, RB * BAND, GROUP_W), lambda i, j: (w, j, i, 0))
    prev = lambda w: pl.BlockSpec((None, RR, BAND, GROUP_W), lambda i, j: (w, j, jnp.maximum(i * RB - 1, 0), 0))
    return pl.pallas_call(
        body, grid=(nb, d // RR), name=f"attn_fwd_g{grp}",
        in_specs=[cur(0), cur(1), prev(1), cur(2), prev(2)],
        out_specs=[pl.BlockSpec((RR, RB * BAND, GROUP_W), lambda i, j: (j, i, 0)),
                   pl.BlockSpec((RR, RB * BAND, STAT_W), lambda i, j: (j, i, 0))],
        out_shape=[SDS((d, L, GROUP_W), BF16), SDS((d, L, STAT_W), F32)],
        compiler_params=_cp("parallel", "parallel"),
    )(qkv, qkv, qkv, qkv, qkv)


def _pool_column_select(col, vals):
    return jnp.where(col < POOL_GROUP_W, vals[0],
                     jnp.where(col < 2 * POOL_GROUP_W, vals[1],
                               jnp.where(col < 3 * POOL_GROUP_W, vals[2], vals[3])))


def _pool_inv_count(i, rows):
    t = i * rows + lax.broadcasted_iota(jnp.int32, (rows, POOL_W), 0)
    col = lax.broadcasted_iota(jnp.int32, (rows, POOL_W), 1)
    win = _pool_column_select(col, POOL_WINDOWS)
    return 1.0 / jnp.minimum(t + 1, win).astype(F32), col


def _mixer_out(outs, lses, pz, gates, x, w_ao, w_po, wbd, scale, w_out, g_mlp, expand, shards, bufs):
    S = x.shape[0]
    n_tiles = S // TMB
    n = len(shards)

    def body(*refs):
        (o0_ref, l0_ref, o1_ref, l1_ref, o2_ref, l2_ref, pz_ref, halo_ref, gate_ref, x_ref,
         wao_ref, wpo_ref, wbd_ref, sc_ref, wout_ref, g_ref, expand_ref) = refs[0:17]
        shard_refs = refs[17:17 + n]
        (a_ref, lt0_ref, lt1_ref, lt2_ref, pooled_ref, mixed_ref, p_ref, merged_ref, h1_ref,
         m_ref) = refs[17 + 2 * n:27 + 2 * n]
        buf_refs = refs[27 + 2 * n:27 + 3 * n]
        so1, sl1, so2, sl2, slt, ext_ref = refs[27 + 3 * n:33 + 3 * n]
        i = pl.program_id(0)
        begin, end = _hosted_allgather(i, n_tiles, shard_refs, buf_refs, [sh.shape[0] for sh in shards],
                                       refs[33 + 3 * n:])
        begin()
        _interleave_load(o1_ref, (), so1, DILATIONS[1], TMB)
        _interleave_load(o2_ref, (), so2, DILATIONS[2], TMB)
        for ref, sref, d in ((l1_ref, sl1, DILATIONS[1]), (l2_ref, sl2, DILATIONS[2])):
            for r in range(d):
                sref[0, pl.ds(r, TMB // d, stride=d), :] = ref[r]
        l0, l1, l2 = l0_ref[0], sl1[0], sl2[0]
        mx = jnp.maximum(jnp.maximum(l0, l1), l2)
        e0, e1, e2 = jnp.exp(l0 - mx), jnp.exp(l1 - mx), jnp.exp(l2 - mx)
        den = e0 + e1 + e2
        inv = 1.0 / den
        slt[0] = mx + jnp.log(den)
        w0, w1, w2 = [_dot_split(e * inv, expand_ref[...]) for e in (e0, e1, e2)]
        for h in range(2):
            hs = slice(h * PAIR_W, (h + 1) * PAIR_W)
            a_ref[:, hs] = (w0[:, hs] * o0_ref[0, :, hs].astype(F32) + w1[:, hs] * so1[h]
                            + w2[:, hs] * so2[h]).astype(BF16)
        lt0_ref[0] = slt[0]
        for ref, d in ((lt1_ref, DILATIONS[1]), (lt2_ref, DILATIONS[2])):
            for r in range(d):
                ref[r] = slt[0, pl.ds(r, TMB // d, stride=d), :]

        pz_t = pz_ref[...].astype(F32)
        ext_ref[0:POOL_HALO, :] = jnp.where(i > 0, halo_ref[...].astype(F32), 0.0)
        ext_ref[POOL_HALO:, :] = pz_t
        sums = []
        acc = ext_ref[...]
        for k in (1, 2, 4, 8):
            acc = acc + pltpu.roll(acc, k, 0)
            sums.append(acc[POOL_HALO:, :])
        inv_cnt, col = _pool_inv_count(i, TMB)
        pooled = (_pool_column_select(col, sums) * inv_cnt - pz_t).astype(BF16)
        pooled_ref[...] = pooled
        mixed = _dot(pooled, wbd_ref[...])
        mixed_ref[...] = mixed.astype(BF16)
        p = (mixed * sc_ref[...]).astype(BF16)
        p_ref[...] = p

        a = a_ref[...]
        for j in range(N_CHIPS):
            js = slice(j * CHUNK, (j + 1) * CHUNK)
            ga = gate_ref[:, js]
            gp = gate_ref[:, D_MODEL + j * CHUNK:D_MODEL + (j + 1) * CHUNK]
            mj = _sigmoid(ga) * _dot(a, wao_ref[j]) + _sigmoid(gp) * _dot(p, wpo_ref[j])
            merged_ref[:, js] = mj.astype(BF16)
        h1 = x_ref[...] + _dot(merged_ref[...], wout_ref[...])
        h1_ref[...] = h1
        m_ref[...] = _rms_fwd(h1, g_ref[...])[0].astype(BF16)
        end()

    row = lambda w: pl.BlockSpec((TMB, w), lambda i: (i, 0))
    grp_spec = lambda d: pl.BlockSpec((d, TMB // d, GROUP_W), lambda i: (0, i, 0))
    stat_spec = lambda d: pl.BlockSpec((d, TMB // d, STAT_W), lambda i: (0, i, 0))
    halo = pl.BlockSpec((POOL_HALO, POOL_W), lambda i: (jnp.maximum(i * (TMB // POOL_HALO) - 1, 0), 0))
    d0, d1, d2 = DILATIONS
    pair_scratch = pltpu.VMEM((2, TMB, PAIR_W), F32)
    stat_scratch = pltpu.VMEM((1, TMB, STAT_W), F32)
    res = pl.pallas_call(
        body, grid=(n_tiles,), name="mixer_out",
        in_specs=[grp_spec(d0), stat_spec(d0), grp_spec(d1), stat_spec(d1), grp_spec(d2), stat_spec(d2),
                  row(POOL_W), halo, row(2 * D_MODEL), row(D_MODEL),
                  _resident(w_ao.shape), _resident(w_po.shape), _resident(wbd.shape), _resident(scale.shape),
                  _resident(w_out.shape), _resident(g_mlp.shape), _resident(expand.shape)] + [ANY] * (2 * n),
        out_specs=[row(GROUP_W), stat_spec(d0), stat_spec(d1), stat_spec(d2),
                   row(POOL_W), row(POOL_W), row(POOL_W), row(D_MODEL), row(D_MODEL), row(D_MODEL)] + [ANY] * n,
        out_shape=[SDS((S, GROUP_W), BF16)] + [SDS((d, S // d, STAT_W), F32) for d in DILATIONS]
        + [SDS((S, POOL_W), BF16), SDS((S, POOL_W), BF16), SDS((S, POOL_W), BF16),
           SDS((S, D_MODEL), BF16), SDS((S, D_MODEL), F32), SDS((S, D_MODEL), BF16)]
        + [SDS(b.shape, b.dtype) for b in bufs],
        scratch_shapes=[pair_scratch, stat_scratch, pair_scratch, stat_scratch, stat_scratch,
                        pltpu.VMEM((TMB + POOL_HALO, POOL_W), F32)] + [pltpu.SemaphoreType.DMA((3 * n,))] * 4,
        input_output_aliases={17 + n + w: 10 + w for w in range(n)},
        compiler_params=_cp("arbitrary"),
    )(outs[0], lses[0], outs[1], lses[1], outs[2], lses[2], pz, pz, gates, x,
      w_ao, w_po, wbd, scale, w_out, g_mlp, expand, *shards, *bufs)
    return res[:10], res[10:]


def _mlp_fwd_loss(m, h1, target, w_mi, w_mo, g_f):
    S = m.shape[0]

    def body(m_ref, h1_ref, t_ref, wmi_ref, wmo_ref, g_ref, hid_ref, dh2_ref, dh2b_ref, loss_ref, dg_ref):
        @pl.when(pl.program_id(0) == 0)
        def _():
            loss_ref[...] = jnp.zeros_like(loss_ref)
            dg_ref[...] = jnp.zeros_like(dg_ref)

        mt = m_ref[...]
        acc = h1_ref[...]
        for c in range(N_CHIPS):
            hid = jnp.square(jnp.maximum(_dot(mt, wmi_ref[c]), 0.0)).astype(BF16)
            hid_ref[:, c * D_MODEL:(c + 1) * D_MODEL] = hid
            acc = acc + _dot(hid, wmo_ref[c])
        g = g_ref[...]
        y, hh, r = _rms_fwd(acc, g)
        e = y - t_ref[...]
        loss_ref[...] += jnp.sum(e * e, axis=0, keepdims=True)
        dy = e * (1.0 / D_MODEL)
        dg_ref[...] += jnp.sum(dy * hh, axis=0, keepdims=True)
        dh2 = _rms_bwd(dy, hh, r, g)
        dh2_ref[...] = dh2
        dh2b_ref[...] = dh2.astype(BF16)

    row = lambda w: pl.BlockSpec((TM, w), lambda i: (i, 0))
    vec = pl.BlockSpec((1, D_MODEL), lambda i: (0, 0))
    return pl.pallas_call(
        body, grid=(S // TM,), name="mlp_fwd_loss",
        in_specs=[row(D_MODEL), row(D_MODEL), row(D_MODEL), _resident(w_mi.shape), _resident(w_mo.shape),
                  _resident(g_f.shape)],
        out_specs=[row(D_FF), row(D_MODEL), row(D_MODEL), vec, vec],
        out_shape=[SDS((S, D_FF), BF16), SDS((S, D_MODEL), F32), SDS((S, D_MODEL), BF16),
                   SDS((1, D_MODEL), F32), SDS((1, D_MODEL), F32)],
        compiler_params=_cp("arbitrary"),
    )(m, h1, target, w_mi, w_mo, g_f)


def _mlp_bwd(dh2, dh2b, hid, h1, w_mi, w_mo, g_mlp):
    S = dh2.shape[0]

    def body(dh2_ref, dh2b_ref, hid_ref, h1_ref, wmi_ref, wmo_ref, g_ref, dpre_ref, dh1_ref, dh1b_ref, dg_ref):
        @pl.when(pl.program_id(0) == 0)
        def _():
            dg_ref[...] = jnp.zeros_like(dg_ref)

        d2 = dh2b_ref[...]
        dm = jnp.zeros((TM, D_MODEL), F32)
        dhid_next = _dot_nt(d2, wmo_ref[0])
        for c in range(N_CHIPS):
            cs = slice(c * D_MODEL, (c + 1) * D_MODEL)
            dhid = dhid_next
            if c + 1 < N_CHIPS:
                dhid_next = _dot_nt(d2, wmo_ref[c + 1])
            dpre = (dhid * (2.0 * jnp.sqrt(hid_ref[:, cs].astype(F32)))).astype(BF16)
            dpre_ref[:, cs] = dpre
            dm = dm + _dot_nt(dpre, wmi_ref[c])
        g = g_ref[...]
        _, hh, r = _rms_fwd(h1_ref[...], g)
        dg_ref[...] += jnp.sum(dm * hh, axis=0, keepdims=True)
        dh1 = dh2_ref[...] + _rms_bwd(dm, hh, r, g)
        dh1_ref[...] = dh1
        dh1b_ref[...] = dh1.astype(BF16)

    row = lambda w: pl.BlockSpec((TM, w), lambda i: (i, 0))
    return pl.pallas_call(
        body, grid=(S // TM,), name="mlp_bwd",
        in_specs=[row(D_MODEL), row(D_MODEL), row(D_FF), row(D_MODEL), _resident(w_mi.shape),
                  _resident(w_mo.shape), _resident(g_mlp.shape)],
        out_specs=[row(D_FF), row(D_MODEL), row(D_MODEL), pl.BlockSpec((1, D_MODEL), lambda i: (0, 0))],
        out_shape=[SDS((S, D_FF), BF16), SDS((S, D_MODEL), F32), SDS((S, D_MODEL), BF16), SDS((1, D_MODEL), F32)],
        compiler_params=_cp("arbitrary"),
    )(dh2, dh2b, hid, h1, w_mi, w_mo, g_mlp)


def _mixer_bwd(dh1b, a, p, mixed, gates, w_out, w_ao, w_po, wbd, scale, head_ones, sums):
    S = a.shape[0]
    n_tiles = S // TMB
    n = len(sums)

    def body(*refs):
        (dh1b_ref, a_ref, p_ref, mixed_ref, gate_ref, wout_ref, wao_ref, wpo_ref, wbd_ref, sc_ref,
         ones_ref) = refs[0:11]
        sum_refs = refs[11:11 + n]
        (da1_ref, dp1_ref, dgate_ref, da0_ref, dag1_ref, dag2_ref, dd0_ref, dd1_ref, dd2_ref,
         dmixed_ref, dqp_ref, dscale_ref) = refs[11 + n:23 + n]
        land_refs = refs[23 + n:23 + 2 * n]
        s_da, s_dd, send_sem, recv_sem = refs[23 + 2 * n:]
        i = pl.program_id(0)

        @pl.when(i == 0)
        def _():
            dscale_ref[...] = jnp.zeros_like(dscale_ref)
            for cpy in _chip_sum_copies(sum_refs, land_refs, send_sem, recv_sem):
                cpy.start()

        dmerged = _dot_nt(dh1b_ref[...], wout_ref[...])
        a = a_ref[...]
        p = p_ref[...]
        da = jnp.zeros((TMB, GROUP_W), F32)
        dp = jnp.zeros((TMB, POOL_W), F32)
        for j in range(N_CHIPS):
            js = slice(j * CHUNK, (j + 1) * CHUNK)
            sa = _sigmoid(gate_ref[:, js])
            sp = _sigmoid(gate_ref[:, D_MODEL + j * CHUNK:D_MODEL + (j + 1) * CHUNK])
            dmj = dmerged[:, js]
            da1 = (dmj * sa).astype(BF16)
            dp1 = (dmj * sp).astype(BF16)
            da1_ref[:, js] = da1
            dp1_ref[:, js] = dp1
            dgate_ref[j] = (dmj * _dot(a, wao_ref[j]) * sa * (1.0 - sa)).astype(BF16)
            dgate_ref[N_CHIPS + j] = (dmj * _dot(p, wpo_ref[j]) * sp * (1.0 - sp)).astype(BF16)
            da = da + _dot_nt(da1, wao_ref[j])
            dp = dp + _dot_nt(dp1, wpo_ref[j])

        dd = _dot_split(da * a.astype(F32), ones_ref[...])
        da0_ref[0] = da.astype(BF16)
        dd0_ref[0] = dd
        for h in range(2):
            s_da[h] = da[:, h * PAIR_W:(h + 1) * PAIR_W]
        s_dd[0] = dd
        for refs, d in (((dag1_ref, dd1_ref), DILATIONS[1]), ((dag2_ref, dd2_ref), DILATIONS[2])):
            for r in range(d):
                for h in range(2):
                    hs = slice(h * PAIR_W, (h + 1) * PAIR_W)
                    refs[0][r, :, hs] = s_da[h, pl.ds(r, TMB // d, stride=d), :].astype(BF16)
                refs[1][r] = s_dd[0, pl.ds(r, TMB // d, stride=d), :]

        sc = sc_ref[...]
        dscale_ref[...] += jnp.sum(dp * mixed_ref[...].astype(F32), axis=0, keepdims=True)
        dmixed = (dp * sc).astype(BF16)
        dmixed_ref[...] = dmixed
        inv_cnt, _ = _pool_inv_count(i, TMB)
        dqp_ref[...] = (_dot_nt(dmixed, wbd_ref[...]) * inv_cnt).astype(BF16)

        @pl.when(i == n_tiles - 1)
        def _():
            for cpy in _chip_sum_copies(sum_refs, land_refs, send_sem, recv_sem):
                cpy.wait()

    row = lambda w: pl.BlockSpec((TMB, w), lambda i: (i, 0))
    grp_spec = lambda d: pl.BlockSpec((d, TMB // d, GROUP_W), lambda i: (0, i, 0))
    stat_spec = lambda d: pl.BlockSpec((d, TMB // d, STAT_W), lambda i: (0, i, 0))
    d0, d1, d2 = DILATIONS
    res = pl.pallas_call(
        body, grid=(n_tiles,), name="mixer_bwd",
        in_specs=[row(D_MODEL), row(GROUP_W), row(POOL_W), row(POOL_W), row(2 * D_MODEL),
                  _resident(w_out.shape), _resident(w_ao.shape), _resident(w_po.shape), _resident(wbd.shape),
                  _resident(scale.shape), _resident(head_ones.shape)] + [ANY] * n,
        out_specs=[row(D_MODEL), row(D_MODEL), pl.BlockSpec((2 * N_CHIPS, TMB, CHUNK), lambda i: (0, i, 0)),
                   grp_spec(d0), grp_spec(d1), grp_spec(d2), stat_spec(d0), stat_spec(d1), stat_spec(d2),
                   row(POOL_W), row(POOL_W), pl.BlockSpec((1, POOL_W), lambda i: (0, 0))] + [ANY] * n,
        out_shape=[SDS((S, D_MODEL), BF16), SDS((S, D_MODEL), BF16), SDS((2 * N_CHIPS, S, CHUNK), BF16)]
        + [SDS((d, S // d, GROUP_W), BF16) for d in DILATIONS]
        + [SDS((d, S // d, STAT_W), F32) for d in DILATIONS]
        + [SDS((S, POOL_W), BF16), SDS((S, POOL_W), BF16), SDS((1, POOL_W), F32)]
        + [SDS(t.shape, t.dtype) for t in sums],
        scratch_shapes=[pltpu.VMEM((2, TMB, PAIR_W), F32), pltpu.VMEM((1, TMB, STAT_W), F32),
                        pltpu.SemaphoreType.DMA((3 * n,)), pltpu.SemaphoreType.DMA((3 * n,))],
        compiler_params=_cp("arbitrary"),
    )(dh1b, a, p, mixed, gates, w_out, w_ao, w_po, wbd, scale, head_ones, *sums)
    return res[:12], res[12:]


def _attn_bwd(qkv, da, lt, dd, grp):
    d = DILATIONS[grp]
    L = qkv.shape[2]
    RR, RB, nb = _attn_tiles(grp, L)

    def body(q_ref, kc_ref, kp_ref, vc_ref, vp_ref, da_ref, lt_ref, dd_ref, dq_ref, dk_ref, dv_ref, dk_acc, dv_acc):
        i = pl.program_id(1)

        @pl.when(i == 0)
        def _():
            dk_acc[...] = jnp.zeros_like(dk_acc)
            dv_acc[...] = jnp.zeros_like(dv_acc)

        def compute(cur, prv):
            dk_acc[cur] = jnp.zeros((RR, RB * BAND, GROUP_W), F32)
            dv_acc[cur] = jnp.zeros((RR, RB * BAND, GROUP_W), F32)
            biases, col = _band_bias(grp, d)
            first_keys_ok = (col >= BAND) | (i > 0)
            is_a = lax.broadcasted_iota(jnp.int32, (BAND, PAIR_W), 1) < HEAD_W
            for rr in range(RR):
                for rb in range(RB):
                    rows = slice(rb * BAND, (rb + 1) * BAND)
                    for cp in range(2):
                        cs = slice(cp * PAIR_W, (cp + 1) * PAIR_W)
                        q2 = q_ref[rr, rows, cs]
                        da2 = da_ref[rr, rows, cs]
                        lt2 = lt_ref[rr, rows, :]
                        dd2 = dd_ref[rr, rows, :]
                        kcat = _kv_tile(kc_ref, kp_ref, rr, rb, cs)
                        vcat = _kv_tile(vc_ref, vp_ref, rr, rb, cs)
                        q2t = q2.astype(F32).T.astype(BF16)
                        da2t = da2.astype(F32).T.astype(BF16)
                        dqs, dkts, dvts, scores, dpvs = [], [], [], [], []
                        for h2 in range(2):
                            sel = is_a if h2 == 0 else jnp.logical_not(is_a)
                            b = biases[cp * 2 + h2]
                            if rb == 0:
                                b = jnp.where(first_keys_ok, b, NEG)
                            scores.append(_dot_nt(jnp.where(sel, q2, jnp.zeros_like(q2)), kcat) + b)
                            dpvs.append(_dot_nt(jnp.where(sel, da2, jnp.zeros_like(da2)), vcat))
                        for h2 in range(2):
                            lane0 = h2 * HEAD_W
                            p = jnp.exp(scores[h2] - _head_col(lt2, cp * 2 + h2))
                            ds = (p * (dpvs[h2] - _head_col(dd2, cp * 2 + h2))).astype(BF16)
                            dqs.append(_dot(ds, kcat))
                            dkts.append(_dot(q2t[lane0:lane0 + HEAD_W, :], ds))
                            dvts.append(_dot(da2t[lane0:lane0 + HEAD_W, :], p.astype(BF16)))
                        dq_ref[rr, rows, cs] = (jnp.where(is_a, dqs[0], dqs[1]) * 0.125).astype(BF16)
                        dkc = jnp.concatenate(dkts, axis=0).T
                        dvc = jnp.concatenate(dvts, axis=0).T
                        if rb == 0:
                            last = slice((RB - 1) * BAND, RB * BAND)
                            dk_acc[prv, rr, last, cs] += dkc[0:BAND]
                            dv_acc[prv, rr, last, cs] += dvc[0:BAND]
                            dk_acc[cur, rr, 0:BAND, cs] += dkc[BAND:]
                            dv_acc[cur, rr, 0:BAND, cs] += dvc[BAND:]
                        else:
                            both = slice((rb - 1) * BAND, (rb + 1) * BAND)
                            dk_acc[cur, rr, both, cs] += dkc
                            dv_acc[cur, rr, both, cs] += dvc

        def flush(prv):
            dk_ref[...] = dk_acc[prv].astype(BF16)
            dv_ref[...] = dv_acc[prv].astype(BF16)

        for parity in (0, 1):
            on = (i % 2) == parity
            pl.when(on & (i < nb))(functools.partial(compute, parity, 1 - parity))
            pl.when(on & (i > 0))(functools.partial(flush, 1 - parity))

    qi = lambda i: jnp.minimum(i, nb - 1)
    cur_w = lambda w: pl.BlockSpec((None, RR, RB * BAND, GROUP_W), lambda j, i: (w, j, qi(i), 0))
    prev_w = lambda w: pl.BlockSpec((None, RR, BAND, GROUP_W),
                                    lambda j, i: (w, j, jnp.maximum(qi(i) * RB - 1, 0), 0))
    blk = pl.BlockSpec((RR, RB * BAND, GROUP_W), lambda j, i: (j, qi(i), 0))
    stat_blk = pl.BlockSpec((RR, RB * BAND, STAT_W), lambda j, i: (j, qi(i), 0))
    late = pl.BlockSpec((RR, RB * BAND, GROUP_W), lambda j, i: (j, jnp.maximum(i - 1, 0), 0))
    return pl.pallas_call(
        body, grid=(d // RR, nb + 1), name=f"attn_bwd_g{grp}",
        in_specs=[cur_w(0), cur_w(1), prev_w(1), cur_w(2), prev_w(2), blk, stat_blk, stat_blk],
        out_specs=[blk, late, late],
        out_shape=[SDS((d, L, GROUP_W), BF16)] * 3,
        scratch_shapes=[pltpu.VMEM((2, RR, RB * BAND, GROUP_W), F32), pltpu.VMEM((2, RR, RB * BAND, GROUP_W), F32)],
        compiler_params=_cp("parallel", "arbitrary"),
    )(qkv, qkv, qkv, qkv, qkv, da, lt, dd)


def _dz_assemble(dqkv, dqp):
    S = dqp.shape[0]
    n_tiles = S // TMB

    def body(*refs):
        dqkv_refs = refs[0:9]
        dqp_ref, halo_ref = refs[9:11]
        dz_ref, s_ref, ext_ref = refs[11:]
        i = pl.program_id(0)

        for grp in range(3):
            for which in range(3):
                n = which * 3 + grp
                ref = dqkv_refs[grp * 3 + which]
                if DILATIONS[grp] == 1:
                    dz_ref[n] = ref[0]
                else:
                    _interleave_load(ref, (), s_ref, DILATIONS[grp], TMB)
                    for h in range(2):
                        dz_ref[n, :, h * PAIR_W:(h + 1) * PAIR_W] = s_ref[h].astype(BF16)

        dqp = dqp_ref[...].astype(F32)
        ext_ref[0:TMB, :] = dqp
        ext_ref[TMB:, :] = jnp.where(i < n_tiles - 1, halo_ref[...].astype(F32), 0.0)
        sums = []
        acc = ext_ref[...]
        for k in (1, 2, 4, 8):
            acc = acc + pltpu.roll(acc, TMB + POOL_HALO - k, 0)
            sums.append(acc[0:TMB, :])
        inv_cnt, col = _pool_inv_count(i, TMB)
        dpz = _pool_column_select(col, sums) - dqp / inv_cnt
        for t in range(3):
            dz_ref[9 + t] = dpz[:, t * CHUNK:(t + 1) * CHUNK].astype(BF16)

    row = lambda w: pl.BlockSpec((TMB, w), lambda i: (i, 0))
    grp_spec = lambda d: pl.BlockSpec((d, TMB // d, GROUP_W), lambda i: (0, i, 0))
    halo = pl.BlockSpec((POOL_HALO, POOL_W),
                        lambda i: (jnp.minimum((i + 1) * (TMB // POOL_HALO), S // POOL_HALO - 1), 0))
    flat = [t for grp in range(3) for t in dqkv[grp]]
    return pl.pallas_call(
        body, grid=(n_tiles,), name="dz_assemble",
        in_specs=[grp_spec(DILATIONS[grp]) for grp in range(3) for _ in range(3)] + [row(POOL_W), halo],
        out_specs=pl.BlockSpec((N_DZ_CHUNKS, TMB, CHUNK), lambda i: (0, i, 0)),
        out_shape=SDS((N_DZ_CHUNKS, S, CHUNK), BF16),
        scratch_shapes=[pltpu.VMEM((2, TMB, PAIR_W), F32), pltpu.VMEM((TMB + POOL_HALO, POOL_W), F32)],
        compiler_params=_cp("parallel"),
    )(*flat, dqp, dqp)


def _inproj_dx(dz, dgates, dh1, x, g, w_in, sums):
    S = x.shape[0]
    n_tiles = S // TM
    n = len(sums)

    def body(*refs):
        dz_ref, dgate_ref, dh1_ref, x_ref, g_ref, w_ref = refs[0:6]
        sum_refs = refs[6:6 + n]
        dx_ref, dg_ref = refs[6 + n:8 + n]
        land_refs = refs[8 + n:8 + 2 * n]
        sems = refs[8 + 2 * n:]
        i = pl.program_id(0)

        def copies():
            return _chip_sum_copies(sum_refs, land_refs, *sems)

        @pl.when(i == 0)
        def _():
            dg_ref[...] = jnp.zeros_like(dg_ref)
            for cpy in copies():
                cpy.start()

        du = jnp.zeros((TM, D_MODEL), F32)
        for k in range(N_CHUNKS):
            dzk = dz_ref[k] if k < N_DZ_CHUNKS else dgate_ref[k - N_DZ_CHUNKS]
            du = du + _dot_nt(dzk, _w_in_chunk(w_ref, k))
        gv = g_ref[...]
        _, xh, r = _rms_fwd(x_ref[...], gv)
        dg_ref[...] += jnp.sum(du * xh, axis=0, keepdims=True)
        dx_ref[...] = dh1_ref[...] + _rms_bwd(du, xh, r, gv)

        @pl.when(i == n_tiles - 1)
        def _():
            for cpy in copies():
                cpy.wait()

    row = lambda w: pl.BlockSpec((TM, w), lambda i: (i, 0))
    res = pl.pallas_call(
        body, grid=(n_tiles,), name="inproj_dx",
        in_specs=[pl.BlockSpec((N_DZ_CHUNKS, TM, CHUNK), lambda i: (0, i, 0)),
                  pl.BlockSpec((N_CHUNKS - N_DZ_CHUNKS, TM, CHUNK), lambda i: (0, i, 0)),
                  row(D_MODEL), row(D_MODEL), _resident(g.shape), _resident(w_in.shape)] + [ANY] * n,
        out_specs=[row(D_MODEL), pl.BlockSpec((1, D_MODEL), lambda i: (0, 0))] + [ANY] * n,
        out_shape=[SDS((S, D_MODEL), F32), SDS((1, D_MODEL), F32)] + [SDS(t.shape, t.dtype) for t in sums],
        scratch_shapes=[pltpu.SemaphoreType.DMA((3 * n,)), pltpu.SemaphoreType.DMA((3 * n,))],
        compiler_params=_cp("arbitrary"),
    )(dz, dgates, dh1, x, g, w_in, *sums)
    return res[0], res[1], res[2:]


def _wgrad(a, b, name, *, out_shape, a_spec, b_spec, out_spec, grid, n_out_cols=None, fill=None, narrow=True):
    k_axis = len(grid) - 1
    n_k = grid[k_axis]
    n_out = 2 if narrow else 1

    def body(a_ref, b_ref, *rest):
        o_ref = rest[-n_out]

        @pl.when(pl.program_id(k_axis) == 0)
        def _():
            o_ref[...] = jnp.zeros_like(o_ref)

        at = a_ref[...]
        if n_out_cols is None:
            o_ref[...] += _dot_tn(at, b_ref[...])
        elif n_out_cols[0] == "lead_both":
            for t in range(b_ref.shape[0]):
                o_ref[t] += _dot_tn(at, b_ref[t])
        else:
            w = n_out_cols[1]
            for t in range(o_ref.shape[0]):
                o_ref[t] += _dot_tn(at, b_ref[:, t * w:(t + 1) * w])

        if narrow:
            @pl.when(pl.program_id(k_axis) == n_k - 1)
            def _():
                rest[-1][...] = o_ref[...].astype(BF16)

    sem = ("parallel",) * k_axis + ("arbitrary",)
    extra = [] if fill is None else list(fill) if narrow else [fill]
    shapes = [out_shape, SDS(out_shape.shape, BF16)] if narrow else out_shape
    return pl.pallas_call(body, grid=grid, name=name, in_specs=[a_spec, b_spec] + [ANY] * len(extra),
                          out_specs=[out_spec] * n_out if narrow else out_spec, out_shape=shapes,
                          input_output_aliases={2 + t: t for t in range(len(extra))},
                          compiler_params=_cp(*sem))(a, b, *extra)


def _wgrad_in(u, dz, dgates):
    bk = min(BK, u.shape[0])
    nk = u.shape[0] // bk
    g = WGRAD_IN_GROUP
    kw = dict(n_out_cols=("lead_both", CHUNK), a_spec=pl.BlockSpec((bk, D_MODEL), lambda j, k: (k, 0)),
              b_spec=pl.BlockSpec((g, bk, CHUNK), lambda j, k: (j, k, 0)),
              out_shape=SDS((N_CHUNKS, D_MODEL, CHUNK), F32))
    first = _wgrad(u, dz, "wgrad_in_qkvp", grid=(N_DZ_CHUNKS // g, nk),
                   out_spec=pl.BlockSpec((g, D_MODEL, CHUNK), lambda j, k: (j, 0, 0)), **kw)
    both = _wgrad(u, dgates, "wgrad_in_gates", grid=((N_CHUNKS - N_DZ_CHUNKS) // g, nk), fill=first,
                  out_spec=pl.BlockSpec((g, D_MODEL, CHUNK), lambda j, k: (N_DZ_CHUNKS // g + j, 0, 0)), **kw)
    return [t.reshape(N_CHIPS, CHUNKS_PER_SHARD * D_MODEL, CHUNK) for t in both]


def _wgrads_mixer(a, da1, p, dp1, merged, dh1b, pooled, dmixed):
    bk = min(BK, a.shape[0])
    nk = a.shape[0] // bk
    g_ao = _wgrad(
        a, da1, "wgrad_att_out", grid=(nk,), n_out_cols=("cols_b", CHUNK),
        a_spec=pl.BlockSpec((bk, GROUP_W), lambda k: (k, 0)),
        b_spec=pl.BlockSpec((bk, D_MODEL), lambda k: (k, 0)),
        out_spec=pl.BlockSpec((N_CHIPS, GROUP_W, CHUNK), lambda k: (0, 0, 0)),
        out_shape=SDS((N_CHIPS, GROUP_W, CHUNK), F32))
    g_po = _wgrad(
        p, dp1, "wgrad_pool_out", grid=(nk,), n_out_cols=("cols_b", CHUNK),
        a_spec=pl.BlockSpec((bk, POOL_W), lambda k: (k, 0)),
        b_spec=pl.BlockSpec((bk, D_MODEL), lambda k: (k, 0)),
        out_spec=pl.BlockSpec((N_CHIPS, POOL_W, CHUNK), lambda k: (0, 0, 0)),
        out_shape=SDS((N_CHIPS, POOL_W, CHUNK), F32))
    g_out = _wgrad(
        merged, dh1b, "wgrad_out", grid=(nk,),
        a_spec=pl.BlockSpec((bk, D_MODEL), lambda k: (k, 0)),
        b_spec=pl.BlockSpec((bk, D_MODEL), lambda k: (k, 0)),
        out_spec=pl.BlockSpec((D_MODEL, D_MODEL), lambda k: (0, 0)),
        out_shape=SDS((D_MODEL, D_MODEL), F32))
    g_bd = _wgrad(
        pooled, dmixed, "wgrad_pool_grp", grid=(nk,),
        a_spec=pl.BlockSpec((bk, POOL_W), lambda k: (k, 0)),
        b_spec=pl.BlockSpec((bk, POOL_W), lambda k: (k, 0)),
        out_spec=pl.BlockSpec((POOL_W, POOL_W), lambda k: (0, 0)),
        out_shape=SDS((POOL_W, POOL_W), F32), narrow=False)
    g_out = [t.reshape(N_CHIPS, D_MODEL // N_CHIPS, D_MODEL) for t in g_out]
    return [g_ao, g_po, g_out], g_bd


def _wgrads_mlp(m, dpre, hid, dh2b):
    bk = min(BK, m.shape[0])
    nk = m.shape[0] // bk
    g_mi = _wgrad(
        m, dpre, "wgrad_mlp_in", grid=(N_CHIPS, nk),
        a_spec=pl.BlockSpec((bk, D_MODEL), lambda c, k: (k, 0)),
        b_spec=pl.BlockSpec((bk, D_MODEL), lambda c, k: (k, c)),
        out_spec=pl.BlockSpec((None, D_MODEL, D_MODEL), lambda c, k: (c, 0, 0)),
        out_shape=SDS((N_CHIPS, D_MODEL, D_MODEL), F32))
    g_mo = _wgrad(
        hid, dh2b, "wgrad_mlp_out", grid=(N_CHIPS, nk),
        a_spec=pl.BlockSpec((bk, D_MODEL), lambda c, k: (k, c)),
        b_spec=pl.BlockSpec((bk, D_MODEL), lambda c, k: (k, 0)),
        out_spec=pl.BlockSpec((None, D_MODEL, D_MODEL), lambda c, k: (c, 0, 0)),
        out_shape=SDS((N_CHIPS, D_MODEL, D_MODEL), F32))
    return [g_mi, g_mo]


def _mesh_place():
    x, y, c = lax.axis_index("x"), lax.axis_index("y"), lax.axis_index("c")
    other_chips = [(x, 1 - y), (1 - x, y), (1 - x, 1 - y)]
    return x, y, c, other_chips


ANY = pl.BlockSpec(memory_space=pl.ANY)


def _weight_half_copies(shard_refs, buf_refs, rows, send_sem, recv_sem):
    x, y, c, chips = _mesh_place()
    me = 2 * x + y
    copies = []
    for w, r_full in enumerate(rows):
        rh = r_full // 2
        for r, (px, py) in enumerate(chips):
            k = w * 3 + r
            copies.append(pltpu.make_async_remote_copy(
                src_ref=shard_refs[w].at[pl.ds(c * rh, rh), :], dst_ref=buf_refs[w].at[me, pl.ds(c * rh, rh), :],
                send_sem=send_sem.at[k], recv_sem=recv_sem.at[k], device_id=(px, py, c), device_id_type=MESH))
    return copies


def _pair_forward_copies(buf_refs, rows, send_sem, recv_sem):
    x, y, c, chips = _mesh_place()
    out = []
    for w, r_full in enumerate(rows):
        rh = r_full // 2
        for r, (px, py) in enumerate(chips):
            k = w * 3 + r
            landed = buf_refs[w].at[2 * px + py, pl.ds(c * rh, rh), :]
            theirs = buf_refs[w].at[2 * px + py, pl.ds((1 - c) * rh, rh), :]
            mk = lambda ref: pltpu.make_async_remote_copy(
                src_ref=ref, dst_ref=ref, send_sem=send_sem.at[k], recv_sem=recv_sem.at[k],
                device_id=(x, y, 1 - c), device_id_type=MESH)
            out.append((mk(landed), mk(theirs)))
    return out


def _place_own(block, n_slots, slot):
    buf = lax.empty((n_slots,) + block.shape, block.dtype)
    return lax.dynamic_update_slice(buf, block[None], (slot,) + (0,) * block.ndim)


def _pair_forward(bufs, rows, name):
    n = len(bufs)

    def body(*refs):
        dst = refs[n:2 * n]
        send_sem, recv_sem = refs[2 * n:]
        fwds = _pair_forward_copies(dst, rows, send_sem, recv_sem)
        for fwd, _ in fwds:
            fwd.start()
        for fwd, landing in fwds:
            landing.wait_recv()
            fwd.wait_send()

    return pl.pallas_call(
        body, name=name,
        in_specs=[ANY] * n, out_specs=[ANY] * n,
        out_shape=[SDS(b.shape, b.dtype) for b in bufs],
        scratch_shapes=[pltpu.SemaphoreType.DMA((3 * n,))] * 2,
        input_output_aliases={w: w for w in range(n)},
    )(*bufs)


def _chip_sum_copies(src, dst, send_sem, recv_sem):
    x, y, c, chips = _mesh_place()
    copies = []
    for w in range(len(src)):
        for r, (px, py) in enumerate(chips):
            k = w * 3 + r
            copies.append(pltpu.make_async_remote_copy(
                src_ref=src[w].at[r + 1], dst_ref=dst[w].at[r + 1], send_sem=send_sem.at[k], recv_sem=recv_sem.at[k],
                device_id=(px, py, c), device_id_type=MESH))
    return copies


def _pair_exchange(grads):
    n = len(grads)

    def body(*refs):
        src, dst = refs[:n], refs[n:2 * n]
        send_sem, recv_sem = refs[2 * n:]
        x, y, c, _ = _mesh_place()
        copies = []
        for w in range(n):
            rh = grads[w].shape[1] // 2
            copies.append(pltpu.make_async_remote_copy(
                src_ref=src[w].at[:, pl.ds((1 - c) * rh, rh), :], dst_ref=dst[w],
                send_sem=send_sem.at[w], recv_sem=recv_sem.at[w],
                device_id=(x, y, 1 - c), device_id_type=MESH))
            copies[-1].start()
        for cpy in copies:
            cpy.wait()

    return pl.pallas_call(
        body, name="grad_pair_exchange",
        in_specs=[ANY] * n, out_specs=[ANY] * n,
        out_shape=[SDS((N_CHIPS, g.shape[1] // 2, g.shape[2]), g.dtype) for g in grads],
        scratch_shapes=[pltpu.SemaphoreType.DMA((n,)), pltpu.SemaphoreType.DMA((n,))],
    )(*grads)


def _pair_sum(place, grad, recv, name):
    _, R, C = grad.shape
    rh = R // 2
    br = _row_block(rh, max(256, ELEMENTWISE_BLOCK // C))
    nbh = rh // br

    def body(place_ref, g_ref, r_ref, own_ref, sums_ref):
        s = g_ref[...] + r_ref[...].astype(F32)

        @pl.when(pl.program_id(1) == 0)
        def _():
            own_ref[...] = s

        sums_ref[...] = s.astype(BF16)

    slot = lambda rel, pr: jnp.bitwise_xor(pr[0], rel)
    return pl.pallas_call(
        body, name=name,
        grid_spec=pltpu.PrefetchScalarGridSpec(
            num_scalar_prefetch=1, grid=(nbh, N_CHIPS),
            in_specs=[pl.BlockSpec((None, br, C), lambda i, rel, pr: (slot(rel, pr), pr[1] * nbh + i, 0)),
                      pl.BlockSpec((None, br, C), lambda i, rel, pr: (slot(rel, pr), i, 0))],
            out_specs=[pl.BlockSpec((br, C), lambda i, rel, pr: (i, 0)),
                       pl.BlockSpec((None, br, C), lambda i, rel, pr: (rel, i, 0))]),
        out_shape=[SDS((rh, C), F32), SDS((N_CHIPS, rh, C), BF16)],
        compiler_params=_cp("parallel", "arbitrary"),
    )(place, grad, recv)


def _chip_sum(place, own, recv, name):
    rh, C = own.shape
    br = _row_block(rh, max(256, ELEMENTWISE_BLOCK // C))
    nbh = rh // br

    def body(place_ref, own_ref, r_ref, o_ref):
        o_ref[...] = ((own_ref[...] + r_ref[1].astype(F32)) + r_ref[2].astype(F32)) + r_ref[3].astype(F32)

    return pl.pallas_call(
        body, name=name,
        grid_spec=pltpu.PrefetchScalarGridSpec(
            num_scalar_prefetch=1, grid=(nbh,),
            in_specs=[pl.BlockSpec((br, C), lambda i, pr: (i, 0)),
                      pl.BlockSpec((N_CHIPS, br, C), lambda i, pr: (0, i, 0))],
            out_specs=pl.BlockSpec((br, C), lambda i, pr: (pr[1] * nbh + i, 0))),
        out_shape=SDS((2 * rh, C), F32),
        compiler_params=_cp("parallel"),
    )(place, own, recv)


def _finish_exchange(grads, small_all):
    n = len(grads)

    def body(*refs):
        dst, all_ref = refs[n + 1:2 * n + 1], refs[2 * n + 1]
        send_sem, recv_sem, ssend_sem, srecv_sem = refs[2 * n + 2:]
        x, y, c, chips = _mesh_place()
        sib = (x, y, 1 - c)

        def pack(dev, k, to):
            slot = 4 * dev[0] + 2 * dev[1] + dev[2]
            return pltpu.make_async_remote_copy(
                src_ref=all_ref.at[slot], dst_ref=all_ref.at[slot], send_sem=ssend_sem.at[k],
                recv_sem=srecv_sem.at[k], device_id=to, device_id_type=MESH)

        pack_copies = [pack((x, y, c), 0, sib)] + [pack((x, y, c), 1 + r, (px, py, c))
                                                   for r, (px, py) in enumerate(chips)]
        for cpy in pack_copies:
            cpy.start()
        sends, landings = [], []
        for w in range(n):
            rh = grads[w].shape[0] // 2
            mk = lambda cc: pltpu.make_async_remote_copy(
                src_ref=dst[w].at[pl.ds(cc * rh, rh), :], dst_ref=dst[w].at[pl.ds(cc * rh, rh), :],
                send_sem=send_sem.at[w], recv_sem=recv_sem.at[w], device_id=(x, y, 1 - c), device_id_type=MESH)
            sends.append(mk(c))
            landings.append(mk(1 - c))
            sends[-1].start()
        for r, (px, py) in enumerate(chips):
            pack((px, py, c), 1 + r, (px, py, c)).wait_recv()
            pack_copies.append(pack((px, py, c), 4 + r, sib))
            pack_copies[-1].start()
        pack(sib, 0, sib).wait_recv()
        for r, (px, py) in enumerate(chips):
            pack((px, py, 1 - c), 4 + r, sib).wait_recv()
        for cpy in landings:
            cpy.wait_recv()
        for cpy in sends + pack_copies:
            cpy.wait_send()

    res = pl.pallas_call(
        body, name="grad_finish_exchange",
        in_specs=[ANY] * (n + 1), out_specs=[ANY] * (n + 1),
        out_shape=[SDS(g.shape, g.dtype) for g in grads] + [SDS(small_all.shape, small_all.dtype)],
        scratch_shapes=[pltpu.SemaphoreType.DMA((n,)), pltpu.SemaphoreType.DMA((n,)),
                        pltpu.SemaphoreType.DMA((N_DEV - 1,)), pltpu.SemaphoreType.DMA((N_DEV - 1,))],
        input_output_aliases={w: w for w in range(n + 1)},
    )(*grads, small_all)
    return res[:n], res[n]


def _adamw_math(w, g, m, v):
    m = ADAM_B1 * m + (1.0 - ADAM_B1) * g
    v = ADAM_B2 * v + (1.0 - ADAM_B2) * jnp.square(g)
    m_hat = m / (1.0 - ADAM_B1 ** ADAM_STEP)
    v_hat = v / (1.0 - ADAM_B2 ** ADAM_STEP)
    delta = -ADAM_LR * (m_hat / (jnp.sqrt(v_hat) + ADAM_EPS) + ADAM_WD * w)
    return delta, m, v


def _adamw(w, g, m, v, name):
    R, C = w.shape
    br = _row_block(R, 512)
    if g.ndim == 3:
        n_chunks, cw = g.shape[0], g.shape[2]
        g_spec = pl.BlockSpec((None, br, cw), lambda t, i: (t, i, 0))
    else:
        n_chunks, cw = 1, C
        g_spec = pl.BlockSpec((br, cw), lambda t, i: (i, t))

    def body(w_ref, g_ref, m_ref, v_ref, g_out_ref, d_ref, nm_ref, nv_ref):
        gv = g_ref[...]
        g_out_ref[...] = gv
        d_ref[...], nm_ref[...], nv_ref[...] = _adamw_math(w_ref[...], gv, m_ref[...], v_ref[...])

    spec = pl.BlockSpec((br, cw), lambda t, i: (i, t))
    return pl.pallas_call(
        body, grid=(n_chunks, R // br), name=name, in_specs=[spec, g_spec, spec, spec], out_specs=[spec] * 4,
        out_shape=[SDS((R, C), F32)] * 4, compiler_params=_cp("parallel", "parallel"),
    )(w, g, m, v)


def _small_sum_adamw(all_small, w, m, v):
    loss_row = PACK_ROWS - 8

    def body(all_ref, w_ref, m_ref, v_ref, g_ref, d_ref, nm_ref, nv_ref, loss_ref):
        g = all_ref[0]
        for k in range(1, N_DEV):
            g = g + all_ref[k]
        g_ref[...] = g
        d_ref[...], nm_ref[...], nv_ref[...] = _adamw_math(w_ref[...], g, m_ref[...], v_ref[...])
        total = jnp.sum(g[loss_row:loss_row + 1, :]) * (0.5 / D_MODEL)
        loss_ref[...] = jnp.full(loss_ref.shape, total, F32)

    full = lambda s: pl.BlockSpec(s, lambda i: (0,) * len(s))
    pack = (PACK_ROWS, D_MODEL)
    return pl.pallas_call(
        body, grid=(1,), name="small_sum_adamw",
        in_specs=[full((N_DEV,) + pack), full(pack), full(pack), full(pack)],
        out_specs=[full(pack)] * 4 + [full((8, 128))],
        out_shape=[SDS(pack, F32)] * 4 + [SDS((8, 128), F32)],
        compiler_params=_cp("arbitrary"),
    )(all_small, w, m, v)


def _pack_small(grp, scale, g_mix, g_mlp, g_f, loss_lanes):
    def part(vec):
        vec = vec.reshape(1, -1)
        return jnp.pad(vec, ((0, 7), (0, D_MODEL - vec.shape[1])))
    return jnp.concatenate([grp.reshape(-1, D_MODEL), part(scale), part(g_mix), part(g_mlp), part(g_f),
                            part(loss_lanes)], axis=0)


def _unpack_small(pack):
    n_grp = len(POOL_WINDOWS) * POOL_GROUP_W * POOL_GROUP_W // D_MODEL
    grp = pack[:n_grp].reshape(1, len(POOL_WINDOWS), POOL_GROUP_W, POOL_GROUP_W)
    scale = pack[n_grp, :POOL_W].reshape(1, POOL_W)
    g_mix = pack[n_grp + 8].reshape(1, D_MODEL)
    g_mlp = pack[n_grp + 16].reshape(1, D_MODEL)
    g_f = pack[n_grp + 24].reshape(D_MODEL)
    return grp, scale, g_mix, g_mlp, g_f


def _block_diag(grp):
    out = jnp.zeros((POOL_W, POOL_W), grp.dtype)
    for k in range(len(POOL_WINDOWS)):
        out = lax.dynamic_update_slice(out, grp[k], (k * POOL_GROUP_W, k * POOL_GROUP_W))
    return out


def kernel(x, norm_mix_g, w_in, w_att_out, w_pool_grp, pool_scale, w_pool_out, w_out, norm_mlp_g, w_mlp_in, w_mlp_out, norm_final_g, loss_target, m_norm_mix_g, m_w_in, m_w_att_out, m_w_pool_grp, m_pool_scale, m_w_pool_out, m_w_out, m_norm_mlp_g, m_w_mlp_in, m_w_mlp_out, m_norm_final_g, v_norm_mix_g, v_w_in, v_w_att_out, v_w_pool_grp, v_pool_scale, v_w_pool_out, v_w_out, v_norm_mlp_g, v_w_mlp_in, v_w_mlp_out, v_norm_final_g):
    S = x.shape[1]
    xs, target = x[0], loss_target[0]
    big = [w_in[0], w_att_out[0], w_pool_out[0], w_out[0], w_mlp_in[0], w_mlp_out[0]]
    big_m = [m_w_in[0], m_w_att_out[0], m_w_pool_out[0], m_w_out[0], m_w_mlp_in[0], m_w_mlp_out[0]]
    big_v = [v_w_in[0], v_w_att_out[0], v_w_pool_out[0], v_w_out[0], v_w_mlp_in[0], v_w_mlp_out[0]]

    chip = 2 * lax.axis_index("x") + lax.axis_index("y")
    core = lax.axis_index("c")
    place = jnp.stack([chip, core]).astype(jnp.int32)
    names = ("w_in", "w_att_out", "w_pool_out", "w_out", "w_mlp_in", "w_mlp_out")

    shards = [w.astype(BF16) for w in big]
    bufs = [_place_own(sh, N_CHIPS, chip) for sh in shards]
    wbd = _block_diag(w_pool_grp[0]).astype(BF16)
    g_final = norm_final_g.reshape(1, D_MODEL)
    stat_expand, stat_reduce = _stat_matrices()

    u, z_own, landed_in = _norm_inproj_own(xs, norm_mix_g, shards[0], bufs[0])
    (wg_in,) = _pair_forward([landed_in], [shards[0].shape[0]], "w_in_pair_forward")
    (qkv0, qkv1, qkv2, pz, gates), (wg_ao, wg_po, wg_out) = _inproj_rest(u, z_own, wg_in, shards[1:4], bufs[1:4])
    wg_out = wg_out.reshape(D_MODEL, D_MODEL)
    qkv = (qkv0, qkv1, qkv2)
    att = [_attn_fwd(qkv[grp], grp) for grp in range(3)]
    (a, lt0, lt1, lt2, pooled, mixed, p, merged, h1, m), (wg_mi, wg_mo) = _mixer_out(
        [o for o, _ in att], [l for _, l in att], pz, gates, xs, wg_ao, wg_po, wbd, pool_scale, wg_out, norm_mlp_g,
        stat_expand, shards[4:], bufs[4:])
    hid, dh2, dh2b, loss_lanes, dg_final = _mlp_fwd_loss(m, h1, target, wg_mi, wg_mo, g_final)

    def pair_reduce(grads, grad_names):
        recv = _pair_exchange([narrow for _, narrow in grads])
        pair = [_pair_sum(place, g, r, f"pair_sum_{nm}") for (g, _), r, nm in zip(grads, recv, grad_names)]
        return [own for own, _ in pair], [s for _, s in pair]

    def chip_reduce(owns, landed_sums, grad_names):
        return [_chip_sum(place, own, r, f"chip_sum_{nm}") for own, r, nm in zip(owns, landed_sums, grad_names)]

    dpre, dh1, dh1b, dg_mlp = _mlp_bwd(dh2, dh2b, hid, h1, wg_mi, wg_mo, norm_mlp_g)
    own_mlp, sums_mlp = pair_reduce(_wgrads_mlp(m, dpre, hid, dh2b), names[4:])
    (da1, dp1, dgates, da0, dag1, dag2, dd0, dd1, dd2, dmixed, dqp, dscale), landed_mlp = _mixer_bwd(
        dh1b, a, p, mixed, gates, wg_out, wg_ao, wg_po, wbd, pool_scale, stat_reduce, sums_mlp)
    g_mi, g_mo = chip_reduce(own_mlp, landed_mlp, names[4:])
    grads_mixer, g_bd = _wgrads_mixer(a, da1, p, dp1, merged, dh1b, pooled, dmixed)
    dqkv = [_attn_bwd(qkv[grp], da_g, lt_g, dd_g, grp)
            for grp, (da_g, lt_g, dd_g) in enumerate(((da0, lt0, dd0), (dag1, lt1, dd1), (dag2, lt2, dd2)))]
    dz = _dz_assemble(dqkv, dqp)
    own_in, sums_in = pair_reduce([_wgrad_in(u, dz, dgates)] + grads_mixer, names[:4])
    dx, dg_mix, landed_in = _inproj_dx(dz, dgates, dh1, xs, norm_mix_g, wg_in, sums_in)
    g_in, g_ao, g_po, g_out = chip_reduce(own_in, landed_in, names[:4])

    g_grp = jnp.stack([g_bd[k * POOL_GROUP_W:(k + 1) * POOL_GROUP_W, k * POOL_GROUP_W:(k + 1) * POOL_GROUP_W]
                       for k in range(len(POOL_WINDOWS))])
    small = _pack_small(g_grp, dscale, dg_mix, dg_mlp, dg_final, loss_lanes)
    full, small_all = _finish_exchange([g_in, g_ao, g_po, g_out, g_mi, g_mo],
                                       _place_own(small, N_DEV, 2 * chip + core))

    zero = jnp.zeros((D_MODEL,), F32)
    small_w = _pack_small(w_pool_grp[0], pool_scale, norm_mix_g, norm_mlp_g, norm_final_g, zero)
    small_m = _pack_small(m_w_pool_grp[0], m_pool_scale, m_norm_mix_g, m_norm_mlp_g, m_norm_final_g, zero)
    small_v = _pack_small(v_w_pool_grp[0], v_pool_scale, v_norm_mix_g, v_norm_mlp_g, v_norm_final_g, zero)
    sg, sd, sm, sv, loss_tile = _small_sum_adamw(small_all, small_w, small_m, small_v)
    full = [full[0].reshape(CHUNKS_PER_SHARD, D_MODEL, CHUNK)] + list(full[1:])
    upd = [_adamw(w, g, mm, vv, f"adamw_{nm}") for w, g, mm, vv, nm in zip(big, full, big_m, big_v, names)]

    def ordered(small_pack, bigs):
        grp, scale, g_mix, g_mlp, g_f = _unpack_small(small_pack)
        b_in, b_ao, b_po, b_out, b_mi, b_mo = [b[None] for b in bigs]
        return (g_mix, b_in, b_ao, grp, scale, b_po, b_out, g_mlp, b_mi, b_mo, g_f)

    return (loss_tile[0, 0], dx[None],
            *ordered(sg, [t[0] for t in upd]),
            *ordered(sd, [t[1] for t in upd]),
            *ordered(sm, [t[2] for t in upd]),
            *ordered(sv, [t[3] for t in upd]))
```

```python
import functools

import jax
import jax.numpy as jnp
from jax import lax
from jax.experimental import pallas as pl
from jax.experimental.pallas import tpu as pltpu

F32 = jnp.float32
BF16 = jnp.bfloat16
SDS = jax.ShapeDtypeStruct
MESH = pl.DeviceIdType.MESH

D_MODEL = 1024
D_FF = 4096
N_CHIPS = 4
N_DEV = 8
DILATIONS = (1, 4, 16)
BAND = 128
GROUP_W = 256
PAIR_W = 128
HEAD_W = 64
STAT_W = 128
STAT_HEAD_W = 32
POOL_W = 768
POOL_GROUP_W = 192
POOL_WINDOWS = (2, 4, 8, 16)
POOL_HALO = 16
N_IN = 5120
CHUNK = 256
N_CHUNKS = N_IN // CHUNK
N_DZ_CHUNKS = 12
CHUNKS_PER_SHARD = 5
WGRAD_IN_GROUP = 4
NORM_EPS = 1e-6
ALIBI_MAX_BIAS = 8.0
N_HEADS = 12
NEG = -1e30

ADAM_LR, ADAM_B1, ADAM_B2, ADAM_EPS, ADAM_WD, ADAM_STEP = 0.001, 0.9, 0.999, 1e-08, 0.01, 10

TM = 512
TMB = 512
ATT_TILE = ((1, 4), (4, 1), (4, 1))
BK = 4096
ELEMENTWISE_BLOCK = 1 << 20
VMEM_LIMIT = 56 * 1024 * 1024
PACK_ROWS = 184
PACK_LATE_ROW = 152

NT = (((1,), (1,)), ((), ()))
TN = (((0,), (0,)), ((), ()))


def _cp(*sem):
    return pltpu.CompilerParams(dimension_semantics=sem, vmem_limit_bytes=VMEM_LIMIT)


def _resident(shape):
    nd = len(shape)
    return pl.BlockSpec(shape, lambda *_: (0,) * nd, pipeline_mode=pl.Buffered(1))


def _row_block(rows, cap=256):
    return max(b for b in range(16, min(rows, cap) + 1, 16) if rows % b == 0)


def _dot(a, b):
    return jnp.dot(a, b, preferred_element_type=F32)


def _dot_nt(a, b):
    return lax.dot_general(a, b, NT, preferred_element_type=F32)


def _dot_tn(a, b):
    return lax.dot_general(a, b, TN, preferred_element_type=F32)


def _w_in_chunk(w_ref, n):
    return w_ref[n // CHUNKS_PER_SHARD, :, (n % CHUNKS_PER_SHARD) * CHUNK:(n % CHUNKS_PER_SHARD + 1) * CHUNK]


def _sigmoid(x):
    return 0.5 * jnp.tanh(0.5 * x.astype(F32)) + 0.5


def _rms_fwd(x, g):
    r = lax.rsqrt(jnp.mean(x * x, axis=-1, keepdims=True) + NORM_EPS)
    xh = x * r
    return xh * g, xh, r


def _rms_bwd(dy, xh, r, g):
    dxh = dy * g
    return r * (dxh - xh * jnp.mean(dxh * xh, axis=-1, keepdims=True))


def _per_head_lanes(cols):
    rows = cols[0].shape[0]
    lane = lax.broadcasted_iota(jnp.int32, (rows, STAT_W), 1)
    out = cols[3]
    for h in (2, 1, 0):
        out = jnp.where(lane < (h + 1) * STAT_HEAD_W, cols[h], out)
    return out


def _head_col(stat, h):
    return stat[:, h * STAT_HEAD_W:h * STAT_HEAD_W + 1]


def _stat_matrices():
    s = lax.broadcasted_iota(jnp.int32, (STAT_W, GROUP_W), 0)
    c = lax.broadcasted_iota(jnp.int32, (STAT_W, GROUP_W), 1)
    expand = (s == (c // HEAD_W) * STAT_HEAD_W).astype(BF16)
    reduce = (s // STAT_HEAD_W == c // HEAD_W).astype(BF16).T
    return expand, reduce


def _dot_split(x, m):
    hi = x.astype(BF16)
    lo = (x - hi.astype(F32)).astype(BF16)
    return _dot(hi, m) + _dot(lo, m)


def _deinterleave_store(val, s_ref, out_ref, lead, d, rows, dtype):
    if d == 1:
        out_ref[lead + (0,)] = val.astype(dtype)
        return
    for h in range(2):
        s_ref[h] = val[:, h * PAIR_W:(h + 1) * PAIR_W]
    for r in range(d):
        for h in range(2):
            out_ref[lead + (r, slice(None), slice(h * PAIR_W, (h + 1) * PAIR_W))] = (
                s_ref[h, pl.ds(r, rows // d, stride=d), :].astype(dtype))


def _interleave_load(in_ref, lead, s_ref, d, rows):
    for r in range(d):
        for h in range(2):
            s_ref[h, pl.ds(r, rows // d, stride=d), :] = (
                in_ref[lead + (r, slice(None), slice(h * PAIR_W, (h + 1) * PAIR_W))].astype(F32))


def _norm_inproj_own(x, g, w_own, buf):
    S = x.shape[0]
    n_tiles = S // TM

    def body(x_ref, g_ref, w_ref, shard_ref, buf_in, u_ref, z_ref, buf_ref, send_sem, recv_sem):
        i = pl.program_id(0)

        def copies():
            return _weight_half_copies([shard_ref], [buf_ref], [w_own.shape[0]], send_sem, recv_sem)

        @pl.when(i == 0)
        def _():
            for cpy in copies():
                cpy.start()

        u = _rms_fwd(x_ref[...], g_ref[...])[0].astype(BF16)
        u_ref[...] = u
        for t in range(CHUNKS_PER_SHARD):
            z_ref[t] = _dot(u, w_ref[:, t * CHUNK:(t + 1) * CHUNK]).astype(BF16)

        @pl.when(i == n_tiles - 1)
        def _():
            for cpy in copies():
                cpy.wait()

    row = lambda w: pl.BlockSpec((TM, w), lambda i: (i, 0))
    return pl.pallas_call(
        body, grid=(n_tiles,), name="norm_inproj_own",
        in_specs=[row(D_MODEL), _resident((1, D_MODEL)), _resident(w_own.shape), ANY, ANY],
        out_specs=[row(D_MODEL), pl.BlockSpec((CHUNKS_PER_SHARD, TM, CHUNK), lambda i: (0, i, 0)), ANY],
        out_shape=[SDS((S, D_MODEL), BF16), SDS((CHUNKS_PER_SHARD, S, CHUNK), BF16), SDS(buf.shape, buf.dtype)],
        scratch_shapes=[pltpu.SemaphoreType.DMA((3,)), pltpu.SemaphoreType.DMA((3,))],
        input_output_aliases={4: 2},
        compiler_params=_cp("arbitrary"),
    )(x, g, w_own, w_own, buf)


def _hosted_allgather(i, n_steps, shard_refs, buf_refs, rows, sems):
    send_sem, recv_sem, fsend_sem, frecv_sem = sems
    ici = lambda: _weight_half_copies(shard_refs, buf_refs, rows, send_sem, recv_sem)
    forward = lambda: _pair_forward_copies(buf_refs, rows, fsend_sem, frecv_sem)

    def begin():
        @pl.when(i == 0)
        def _():
            for cpy in ici():
                cpy.start()

        @pl.when(i == n_steps // 2)
        def _():
            for cpy, (fwd, _) in zip(ici(), forward()):
                cpy.wait_recv()
                fwd.start()

    def end():
        @pl.when(i == n_steps - 1)
        def _():
            for cpy, (fwd, landing) in zip(ici(), forward()):
                landing.wait_recv()
                fwd.wait_send()
                cpy.wait_send()

    return begin, end


def _inproj_rest(u, z_own, w_in, shards, bufs):
    S = u.shape[0]
    n_tiles = S // TM
    n = len(shards)

    def body(*refs):
        u_ref, zown_ref, w_ref = refs[0:3]
        shard_refs = refs[3:3 + n]
        q0_ref, q1_ref, q2_ref, pz_ref, gate_ref = refs[3 + 2 * n:8 + 2 * n]
        buf_refs = refs[8 + 2 * n:8 + 3 * n]
        s_ref = refs[8 + 3 * n]
        i = pl.program_id(0)
        chip = 2 * lax.axis_index("x") + lax.axis_index("y")
        begin, end = _hosted_allgather(i, n_tiles, shard_refs, buf_refs, [sh.shape[0] for sh in shards],
                                       refs[9 + 3 * n:])
        begin()

        u = u_ref[...]
        qkv_refs = (q0_ref, q1_ref, q2_ref)

        def emit(k, zc):
            if k < 9:
                which, grp = k // 3, k % 3
                if which == 0:
                    zc = zc * 0.125
                _deinterleave_store(zc, s_ref, qkv_refs[grp], (which,), DILATIONS[grp], TM, BF16)
            elif k < N_DZ_CHUNKS:
                pz_ref[:, (k - 9) * CHUNK:(k - 8) * CHUNK] = zc.astype(BF16)
            else:
                gate_ref[:, (k - N_DZ_CHUNKS) * CHUNK:(k - N_DZ_CHUNKS + 1) * CHUNK] = zc.astype(BF16)

        def all_chunks(own_shard):
            for k in range(N_CHUNKS):
                if k // CHUNKS_PER_SHARD == own_shard:
                    emit(k, zown_ref[k % CHUNKS_PER_SHARD].astype(F32))
                else:
                    emit(k, _dot(u, _w_in_chunk(w_ref, k)))

        for shard in range(N_CHIPS):
            pl.when(chip == shard)(functools.partial(all_chunks, shard))
        end()

    row = lambda w: pl.BlockSpec((TM, w), lambda i: (i, 0))
    res = pl.pallas_call(
        body, grid=(n_tiles,), name="inproj_rest",
        in_specs=[row(D_MODEL), pl.BlockSpec((CHUNKS_PER_SHARD, TM, CHUNK), lambda i: (0, i, 0)),
                  _resident(w_in.shape)] + [ANY] * (2 * n),
        out_specs=[pl.BlockSpec((3, d, TM // d, GROUP_W), lambda i: (0, 0, i, 0)) for d in DILATIONS]
        + [row(POOL_W), row(2 * D_MODEL)] + [ANY] * n,
        out_shape=[SDS((3, d, S // d, GROUP_W), BF16) for d in DILATIONS]
        + [SDS((S, POOL_W), BF16), SDS((S, 2 * D_MODEL), BF16)] + [SDS(b.shape, b.dtype) for b in bufs],
        scratch_shapes=[pltpu.VMEM((2, TM, PAIR_W), F32)] + [pltpu.SemaphoreType.DMA((3 * n,))] * 4,
        input_output_aliases={3 + n + w: 5 + w for w in range(n)},
        compiler_params=_cp("arbitrary"),
    )(u, z_own, w_in, *shards, *bufs)
    return res[:5], res[5:]


def _band_bias(grp, d):
    row = lax.broadcasted_iota(jnp.int32, (BAND, 2 * BAND), 0)
    col = lax.broadcasted_iota(jnp.int32, (BAND, 2 * BAND), 1)
    steps = BAND + row - col
    valid = (steps >= 0) & (steps <= BAND)
    stepsf = (steps * d).astype(F32)
    biases = []
    for hh in range(4):
        slope = 2.0 ** (-ALIBI_MAX_BIAS * (grp * 4 + hh + 1) / N_HEADS)
        biases.append(jnp.where(valid, -slope * stepsf, NEG))
    return biases, col


def _attn_tiles(grp, L):
    rr, rb = ATT_TILE[grp]
    rb = min(rb, L // BAND)
    return rr, rb, L // (rb * BAND)


def _kv_tile(cur_ref, prev_ref, rr, rb, cs):
    if rb == 0:
        return jnp.concatenate([prev_ref[rr, :, cs], cur_ref[rr, 0:BAND, cs]], axis=0)
    return cur_ref[rr, (rb - 1) * BAND:(rb + 1) * BAND, cs]


def _attn_fwd(qkv, grp):
    d = DILATIONS[grp]
    L = qkv.shape[2]
    RR, RB, nb = _attn_tiles(grp, L)

    def body(q_ref, kc_ref, kp_ref, vc_ref, vp_ref, o_ref, lse_ref):
        i = pl.program_id(0)
        biases, col = _band_bias(grp, d)
        first_keys_ok = (col >= BAND) | (i > 0)
        is_a = lax.broadcasted_iota(jnp.int32, (BAND, PAIR_W), 1) < HEAD_W
        heads = [(rr, rb, cp, h2) for rr in range(RR) for rb in range(RB) for cp in range(2) for h2 in range(2)]

        def tile(head):
            rr, rb, cp, _ = head
            return rr, rb, slice(rb * BAND, (rb + 1) * BAND), slice(cp * PAIR_W, (cp + 1) * PAIR_W)

        def scores(head):
            rr, rb, rows, cs = tile(head)
            q2 = q_ref[rr, rows, cs]
            b = biases[head[2] * 2 + head[3]]
            if rb == 0:
                b = jnp.where(first_keys_ok, b, NEG)
            sel = is_a if head[3] == 0 else jnp.logical_not(is_a)
            return _dot_nt(jnp.where(sel, q2, jnp.zeros_like(q2)), _kv_tile(kc_ref, kp_ref, rr, rb, cs)) + b

        s_next = scores(heads[0])
        outs, lses = {}, {}
        for idx, head in enumerate(heads):
            s = s_next
            if idx + 1 < len(heads):
                s_next = scores(heads[idx + 1])
            rr, rb, rows, cs = tile(head)
            m = jnp.max(s, axis=-1, keepdims=True)
            p = jnp.exp(s - m)
            l = jnp.sum(p, axis=-1, keepdims=True)
            outs[head[3]] = _dot(p.astype(BF16), _kv_tile(vc_ref, vp_ref, rr, rb, cs)) * (1.0 / l)
            lses[head[2] * 2 + head[3]] = m + jnp.log(l)
            if head[3] == 1:
                o_ref[rr, rows, cs] = jnp.where(is_a, outs[0], outs[1]).astype(BF16)
            if head[2] == 1 and head[3] == 1:
                lse_ref[rr, rows, :] = _per_head_lanes(lses)

    cur = lambda w: pl.BlockSpec((None, RR, RB * BAND, GROUP_W), lambda i, j: (w, j, i, 0))
    prev = lambda w: pl.BlockSpec((None, RR, BAND, GROUP_W), lambda i, j: (w, j, jnp.maximum(i * RB - 1, 0), 0))
    return pl.pallas_call(
        body, grid=(nb, d // RR), name=f"attn_fwd_g{grp}",
        in_specs=[cur(0), cur(1), prev(1), cur(2), prev(2)],
        out_specs=[pl.BlockSpec((RR, RB * BAND, GROUP_W), lambda i, j: (j, i, 0)),
                   pl.BlockSpec((RR, RB * BAND, STAT_W), lambda i, j: (j, i, 0))],
        out_shape=[SDS((d, L, GROUP_W), BF16), SDS((d, L, STAT_W), F32)],
        compiler_params=_cp("parallel", "parallel"),
    )(qkv, qkv, qkv, qkv, qkv)


def _pool_column_select(col, vals):
    return jnp.where(col < POOL_GROUP_W, vals[0],
                     jnp.where(col < 2 * POOL_GROUP_W, vals[1],
                               jnp.where(col < 3 * POOL_GROUP_W, vals[2], vals[3])))


def _pool_inv_count(i, rows):
    t = i * rows + lax.broadcasted_iota(jnp.int32, (rows, POOL_W), 0)
    col = lax.broadcasted_iota(jnp.int32, (rows, POOL_W), 1)
    win = _pool_column_select(col, POOL_WINDOWS)
    return 1.0 / jnp.minimum(t + 1, win).astype(F32), col


def _mixer_out(outs, lses, pz, gates, x, w_ao, w_po, wbd, scale, w_out, g_mlp, expand, shards, bufs):
    S = x.shape[0]
    n_tiles = S // TMB
    n = len(shards)

    def body(*refs):
        (o0_ref, l0_ref, o1_ref, l1_ref, o2_ref, l2_ref, pz_ref, halo_ref, gate_ref, x_ref,
         wao_ref, wpo_ref, wbd_ref, sc_ref, wout_ref, g_ref, expand_ref) = refs[0:17]
        shard_refs = refs[17:17 + n]
        (a_ref, lt0_ref, lt1_ref, lt2_ref, pooled_ref, mixed_ref, p_ref, merged_ref, h1_ref,
         m_ref) = refs[17 + 2 * n:27 + 2 * n]
        buf_refs = refs[27 + 2 * n:27 + 3 * n]
        so1, sl1, so2, sl2, slt, ext_ref = refs[27 + 3 * n:33 + 3 * n]
        i = pl.program_id(0)
        begin, end = _hosted_allgather(i, n_tiles, shard_refs, buf_refs, [sh.shape[0] for sh in shards],
                                       refs[33 + 3 * n:])
        begin()
        _interleave_load(o1_ref, (), so1, DILATIONS[1], TMB)
        _interleave_load(o2_ref, (), so2, DILATIONS[2], TMB)
        for ref, sref, d in ((l1_ref, sl1, DILATIONS[1]), (l2_ref, sl2, DILATIONS[2])):
            for r in range(d):
                sref[0, pl.ds(r, TMB // d, stride=d), :] = ref[r]
        l0, l1, l2 = l0_ref[0], sl1[0], sl2[0]
        mx = jnp.maximum(jnp.maximum(l0, l1), l2)
        e0, e1, e2 = jnp.exp(l0 - mx), jnp.exp(l1 - mx), jnp.exp(l2 - mx)
        den = e0 + e1 + e2
        inv = 1.0 / den
        slt[0] = mx + jnp.log(den)
        w0, w1, w2 = [_dot_split(e * inv, expand_ref[...]) for e in (e0, e1, e2)]
        for h in range(2):
            hs = slice(h * PAIR_W, (h + 1) * PAIR_W)
            a_ref[:, hs] = (w0[:, hs] * o0_ref[0, :, hs].astype(F32) + w1[:, hs] * so1[h]
                            + w2[:, hs] * so2[h]).astype(BF16)
        lt0_ref[0] = slt[0]
        for ref, d in ((lt1_ref, DILATIONS[1]), (lt2_ref, DILATIONS[2])):
            for r in range(d):
                ref[r] = slt[0, pl.ds(r, TMB // d, stride=d), :]

        pz_t = pz_ref[...].astype(F32)
        ext_ref[0:POOL_HALO, :] = jnp.where(i > 0, halo_ref[...].astype(F32), 0.0)
        ext_ref[POOL_HALO:, :] = pz_t
        sums = []
        acc = ext_ref[...]
        for k in (1, 2, 4, 8):
            acc = acc + pltpu.roll(acc, k, 0)
            sums.append(acc[POOL_HALO:, :])
        inv_cnt, col = _pool_inv_count(i, TMB)
        pooled = (_pool_column_select(col, sums) * inv_cnt - pz_t).astype(BF16)
        pooled_ref[...] = pooled
        mixed = _dot(pooled, wbd_ref[...])
        mixed_ref[...] = mixed.astype(BF16)
        p = (mixed * sc_ref[...]).astype(BF16)
        p_ref[...] = p

        a = a_ref[...]
        for j in range(N_CHIPS):
            js = slice(j * CHUNK, (j + 1) * CHUNK)
            ga = gate_ref[:, js]
            gp = gate_ref[:, D_MODEL + j * CHUNK:D_MODEL + (j + 1) * CHUNK]
            mj = _sigmoid(ga) * _dot(a, wao_ref[j]) + _sigmoid(gp) * _dot(p, wpo_ref[j])
            merged_ref[:, js] = mj.astype(BF16)
        h1 = x_ref[...] + _dot(merged_ref[...], wout_ref[...])
        h1_ref[...] = h1
        m_ref[...] = _rms_fwd(h1, g_ref[...])[0].astype(BF16)
        end()

    row = lambda w: pl.BlockSpec((TMB, w), lambda i: (i, 0))
    grp_spec = lambda d: pl.BlockSpec((d, TMB // d, GROUP_W), lambda i: (0, i, 0))
    stat_spec = lambda d: pl.BlockSpec((d, TMB // d, STAT_W), lambda i: (0, i, 0))
    halo = pl.BlockSpec((POOL_HALO, POOL_W), lambda i: (jnp.maximum(i * (TMB // POOL_HALO) - 1, 0), 0))
    d0, d1, d2 = DILATIONS
    pair_scratch = pltpu.VMEM((2, TMB, PAIR_W), F32)
    stat_scratch = pltpu.VMEM((1, TMB, STAT_W), F32)
    res = pl.pallas_call(
        body, grid=(n_tiles,), name="mixer_out",
        in_specs=[grp_spec(d0), stat_spec(d0), grp_spec(d1), stat_spec(d1), grp_spec(d2), stat_spec(d2),
                  row(POOL_W), halo, row(2 * D_MODEL), row(D_MODEL),
                  _resident(w_ao.shape), _resident(w_po.shape), _resident(wbd.shape), _resident(scale.shape),
                  _resident(w_out.shape), _resident(g_mlp.shape), _resident(expand.shape)] + [ANY] * (2 * n),
        out_specs=[row(GROUP_W), stat_spec(d0), stat_spec(d1), stat_spec(d2),
                   row(POOL_W), row(POOL_W), row(POOL_W), row(D_MODEL), row(D_MODEL), row(D_MODEL)] + [ANY] * n,
        out_shape=[SDS((S, GROUP_W), BF16)] + [SDS((d, S // d, STAT_W), F32) for d in DILATIONS]
        + [SDS((S, POOL_W), BF16), SDS((S, POOL_W), BF16), SDS((S, POOL_W), BF16),
           SDS((S, D_MODEL), BF16), SDS((S, D_MODEL), F32), SDS((S, D_MODEL), BF16)]
        + [SDS(b.shape, b.dtype) for b in bufs],
        scratch_shapes=[pair_scratch, stat_scratch, pair_scratch, stat_scratch, stat_scratch,
                        pltpu.VMEM((TMB + POOL_HALO, POOL_W), F32)] + [pltpu.SemaphoreType.DMA((3 * n,))] * 4,
        input_output_aliases={17 + n + w: 10 + w for w in range(n)},
        compiler_params=_cp("arbitrary"),
    )(outs[0], lses[0], outs[1], lses[1], outs[2], lses[2], pz, pz, gates, x,
      w_ao, w_po, wbd, scale, w_out, g_mlp, expand, *shards, *bufs)
    return res[:10], res[10:]


def _mlp_fwd_loss(m, h1, target, w_mi, w_mo, g_f):
    S = m.shape[0]

    def body(m_ref, h1_ref, t_ref, wmi_ref, wmo_ref, g_ref, hid_ref, dh2_ref, dh2b_ref, loss_ref, dg_ref):
        @pl.when(pl.program_id(0) == 0)
        def _():
            loss_ref[...] = jnp.zeros_like(loss_ref)
            dg_ref[...] = jnp.zeros_like(dg_ref)

        mt = m_ref[...]
        acc = h1_ref[...]
        for c in range(N_CHIPS):
            hid = jnp.square(jnp.maximum(_dot(mt, wmi_ref[c]), 0.0)).astype(BF16)
            hid_ref[:, c * D_MODEL:(c + 1) * D_MODEL] = hid
            acc = acc + _dot(hid, wmo_ref[c])
        g = g_ref[...]
        y, hh, r = _rms_fwd(acc, g)
        e = y - t_ref[...]
        loss_ref[...] += jnp.sum(e * e, axis=0, keepdims=True)
        dy = e * (1.0 / D_MODEL)
        dg_ref[...] += jnp.sum(dy * hh, axis=0, keepdims=True)
        dh2 = _rms_bwd(dy, hh, r, g)
        dh2_ref[...] = dh2
        dh2b_ref[...] = dh2.astype(BF16)

    row = lambda w: pl.BlockSpec((TM, w), lambda i: (i, 0))
    vec = pl.BlockSpec((1, D_MODEL), lambda i: (0, 0))
    return pl.pallas_call(
        body, grid=(S // TM,), name="mlp_fwd_loss",
        in_specs=[row(D_MODEL), row(D_MODEL), row(D_MODEL), _resident(w_mi.shape), _resident(w_mo.shape),
                  _resident(g_f.shape)],
        out_specs=[row(D_FF), row(D_MODEL), row(D_MODEL), vec, vec],
        out_shape=[SDS((S, D_FF), BF16), SDS((S, D_MODEL), F32), SDS((S, D_MODEL), BF16),
                   SDS((1, D_MODEL), F32), SDS((1, D_MODEL), F32)],
        compiler_params=_cp("arbitrary"),
    )(m, h1, target, w_mi, w_mo, g_f)


def _mlp_bwd(dh2, dh2b, hid, h1, w_mi, w_mo, g_mlp):
    S = dh2.shape[0]

    def body(dh2_ref, dh2b_ref, hid_ref, h1_ref, wmi_ref, wmo_ref, g_ref, dpre_ref, dh1_ref, dh1b_ref, dg_ref):
        @pl.when(pl.program_id(0) == 0)
        def _():
            dg_ref[...] = jnp.zeros_like(dg_ref)

        d2 = dh2b_ref[...]
        dm = jnp.zeros((TM, D_MODEL), F32)
        dhid_next = _dot_nt(d2, wmo_ref[0])
        for c in range(N_CHIPS):
            cs = slice(c * D_MODEL, (c + 1) * D_MODEL)
            dhid = dhid_next
            if c + 1 < N_CHIPS:
                dhid_next = _dot_nt(d2, wmo_ref[c + 1])
            dpre = (dhid * (2.0 * jnp.sqrt(hid_ref[:, cs].astype(F32)))).astype(BF16)
            dpre_ref[:, cs] = dpre
            dm = dm + _dot_nt(dpre, wmi_ref[c])
        g = g_ref[...]
        _, hh, r = _rms_fwd(h1_ref[...], g)
        dg_ref[...] += jnp.sum(dm * hh, axis=0, keepdims=True)
        dh1 = dh2_ref[...] + _rms_bwd(dm, hh, r, g)
        dh1_ref[...] = dh1
        dh1b_ref[...] = dh1.astype(BF16)

    row = lambda w: pl.BlockSpec((TM, w), lambda i: (i, 0))
    return pl.pallas_call(
        body, grid=(S // TM,), name="mlp_bwd",
        in_specs=[row(D_MODEL), row(D_MODEL), row(D_FF), row(D_MODEL), _resident(w_mi.shape),
                  _resident(w_mo.shape), _resident(g_mlp.shape)],
        out_specs=[row(D_FF), row(D_MODEL), row(D_MODEL), pl.BlockSpec((1, D_MODEL), lambda i: (0, 0))],
        out_shape=[SDS((S, D_FF), BF16), SDS((S, D_MODEL), F32), SDS((S, D_MODEL), BF16), SDS((1, D_MODEL), F32)],
        compiler_params=_cp("arbitrary"),
    )(dh2, dh2b, hid, h1, w_mi, w_mo, g_mlp)


def _mixer_bwd(dh1b, a, p, mixed, gates, w_out, w_ao, w_po, wbd, scale, head_ones, sums):
    S = a.shape[0]
    n_tiles = S // TMB
    n = len(sums)

    def body(*refs):
        (dh1b_ref, a_ref, p_ref, mixed_ref, gate_ref, wout_ref, wao_ref, wpo_ref, wbd_ref, sc_ref,
         ones_ref) = refs[0:11]
        sum_refs = refs[11:11 + n]
        (da1_ref, dp1_ref, dgate_ref, da0_ref, dag1_ref, dag2_ref, dd0_ref, dd1_ref, dd2_ref,
         dmixed_ref, dqp_ref, dscale_ref) = refs[11 + n:23 + n]
        land_refs = refs[23 + n:23 + 2 * n]
        s_da, s_dd, send_sem, recv_sem = refs[23 + 2 * n:]
        i = pl.program_id(0)

        @pl.when(i == 0)
        def _():
            dscale_ref[...] = jnp.zeros_like(dscale_ref)
            for cpy in _chip_sum_copies(sum_refs, land_refs, send_sem, recv_sem):
                cpy.start()

        dmerged = _dot_nt(dh1b_ref[...], wout_ref[...])
        a = a_ref[...]
        p = p_ref[...]
        da = jnp.zeros((TMB, GROUP_W), F32)
        dp = jnp.zeros((TMB, POOL_W), F32)
        for j in range(N_CHIPS):
            js = slice(j * CHUNK, (j + 1) * CHUNK)
            sa = _sigmoid(gate_ref[:, js])
            sp = _sigmoid(gate_ref[:, D_MODEL + j * CHUNK:D_MODEL + (j + 1) * CHUNK])
            dmj = dmerged[:, js]
            da1 = (dmj * sa).astype(BF16)
            dp1 = (dmj * sp).astype(BF16)
            da1_ref[:, js] = da1
            dp1_ref[:, js] = dp1
            dgate_ref[j] = (dmj * _dot(a, wao_ref[j]) * sa * (1.0 - sa)).astype(BF16)
            dgate_ref[N_CHIPS + j] = (dmj * _dot(p, wpo_ref[j]) * sp * (1.0 - sp)).astype(BF16)
            da = da + _dot_nt(da1, wao_ref[j])
            dp = dp + _dot_nt(dp1, wpo_ref[j])

        dd = _dot_split(da * a.astype(F32), ones_ref[...])
        da0_ref[0] = da.astype(BF16)
        dd0_ref[0] = dd
        for h in range(2):
            s_da[h] = da[:, h * PAIR_W:(h + 1) * PAIR_W]
        s_dd[0] = dd
        for refs, d in (((dag1_ref, dd1_ref), DILATIONS[1]), ((dag2_ref, dd2_ref), DILATIONS[2])):
            for r in range(d):
                for h in range(2):
                    hs = slice(h * PAIR_W, (h + 1) * PAIR_W)
                    refs[0][r, :, hs] = s_da[h, pl.ds(r, TMB // d, stride=d), :].astype(BF16)
                refs[1][r] = s_dd[0, pl.ds(r, TMB // d, stride=d), :]

        sc = sc_ref[...]
        dscale_ref[...] += jnp.sum(dp * mixed_ref[...].astype(F32), axis=0, keepdims=True)
        dmixed = (dp * sc).astype(BF16)
        dmixed_ref[...] = dmixed
        inv_cnt, _ = _pool_inv_count(i, TMB)
        dqp_ref[...] = (_dot_nt(dmixed, wbd_ref[...]) * inv_cnt).astype(BF16)

        @pl.when(i == n_tiles - 1)
        def _():
            for cpy in _chip_sum_copies(sum_refs, land_refs, send_sem, recv_sem):
                cpy.wait()

    row = lambda w: pl.BlockSpec((TMB, w), lambda i: (i, 0))
    grp_spec = lambda d: pl.BlockSpec((d, TMB // d, GROUP_W), lambda i: (0, i, 0))
    stat_spec = lambda d: pl.BlockSpec((d, TMB // d, STAT_W), lambda i: (0, i, 0))
    d0, d1, d2 = DILATIONS
    res = pl.pallas_call(
        body, grid=(n_tiles,), name="mixer_bwd",
        in_specs=[row(D_MODEL), row(GROUP_W), row(POOL_W), row(POOL_W), row(2 * D_MODEL),
                  _resident(w_out.shape), _resident(w_ao.shape), _resident(w_po.shape), _resident(wbd.shape),
                  _resident(scale.shape), _resident(head_ones.shape)] + [ANY] * n,
        out_specs=[row(D_MODEL), row(D_MODEL), pl.BlockSpec((2 * N_CHIPS, TMB, CHUNK), lambda i: (0, i, 0)),
                   grp_spec(d0), grp_spec(d1), grp_spec(d2), stat_spec(d0), stat_spec(d1), stat_spec(d2),
                   row(POOL_W), row(POOL_W), pl.BlockSpec((1, POOL_W), lambda i: (0, 0))] + [ANY] * n,
        out_shape=[SDS((S, D_MODEL), BF16), SDS((S, D_MODEL), BF16), SDS((2 * N_CHIPS, S, CHUNK), BF16)]
        + [SDS((d, S // d, GROUP_W), BF16) for d in DILATIONS]
        + [SDS((d, S // d, STAT_W), F32) for d in DILATIONS]
        + [SDS((S, POOL_W), BF16), SDS((S, POOL_W), BF16), SDS((1, POOL_W), F32)]
        + [SDS(t.shape, t.dtype) for t in sums],
        scratch_shapes=[pltpu.VMEM((2, TMB, PAIR_W), F32), pltpu.VMEM((1, TMB, STAT_W), F32),
                        pltpu.SemaphoreType.DMA((3 * n,)), pltpu.SemaphoreType.DMA((3 * n,))],
        compiler_params=_cp("arbitrary"),
    )(dh1b, a, p, mixed, gates, w_out, w_ao, w_po, wbd, scale, head_ones, *sums)
    return res[:12], res[12:]


def _attn_bwd(qkv, da, lt, dd, grp, packs=None):
    d = DILATIONS[grp]
    L = qkv.shape[2]
    RR, RB, nb = _attn_tiles(grp, L)
    n_j = d // RR
    hosted = packs is not None

    def body(*refs):
        q_ref, kc_ref, kp_ref, vc_ref, vp_ref, da_ref, lt_ref, dd_ref = refs[0:8]
        dq_ref, dk_ref, dv_ref = refs[8 + hosted:11 + hosted]
        dk_acc, dv_acc = refs[11 + 2 * hosted:13 + 2 * hosted]
        i = pl.program_id(1)
        if hosted:
            j = pl.program_id(0)
            start, relay, finish = _pack_allgather(refs[11 + hosted], *refs[13 + 2 * hosted:])
            pl.when((j == 0) & (i == 0))(start)
            pl.when((j == 0) & (i == nb // 2))(relay)

        @pl.when(i == 0)
        def _():
            dk_acc[...] = jnp.zeros_like(dk_acc)
            dv_acc[...] = jnp.zeros_like(dv_acc)

        def compute(cur, prv):
            dk_acc[cur] = jnp.zeros((RR, RB * BAND, GROUP_W), F32)
            dv_acc[cur] = jnp.zeros((RR, RB * BAND, GROUP_W), F32)
            biases, col = _band_bias(grp, d)
            first_keys_ok = (col >= BAND) | (i > 0)
            is_a = lax.broadcasted_iota(jnp.int32, (BAND, PAIR_W), 1) < HEAD_W
            for rr in range(RR):
                for rb in range(RB):
                    rows = slice(rb * BAND, (rb + 1) * BAND)
                    for cp in range(2):
                        cs = slice(cp * PAIR_W, (cp + 1) * PAIR_W)
                        q2 = q_ref[rr, rows, cs]
                        da2 = da_ref[rr, rows, cs]
                        lt2 = lt_ref[rr, rows, :]
                        dd2 = dd_ref[rr, rows, :]
                        kcat = _kv_tile(kc_ref, kp_ref, rr, rb, cs)
                        vcat = _kv_tile(vc_ref, vp_ref, rr, rb, cs)
                        q2t = q2.astype(F32).T.astype(BF16)
                        da2t = da2.astype(F32).T.astype(BF16)
                        dqs, dkts, dvts, scores, dpvs = [], [], [], [], []
                        for h2 in range(2):
                            sel = is_a if h2 == 0 else jnp.logical_not(is_a)
                            b = biases[cp * 2 + h2]
                            if rb == 0:
                                b = jnp.where(first_keys_ok, b, NEG)
                            scores.append(_dot_nt(jnp.where(sel, q2, jnp.zeros_like(q2)), kcat) + b)
                            dpvs.append(_dot_nt(jnp.where(sel, da2, jnp.zeros_like(da2)), vcat))
                        for h2 in range(2):
                            lane0 = h2 * HEAD_W
                            p = jnp.exp(scores[h2] - _head_col(lt2, cp * 2 + h2))
                            ds = (p * (dpvs[h2] - _head_col(dd2, cp * 2 + h2))).astype(BF16)
                            dqs.append(_dot(ds, kcat))
                            dkts.append(_dot(q2t[lane0:lane0 + HEAD_W, :], ds))
                            dvts.append(_dot(da2t[lane0:lane0 + HEAD_W, :], p.astype(BF16)))
                        dq_ref[rr, rows, cs] = (jnp.where(is_a, dqs[0], dqs[1]) * 0.125).astype(BF16)
                        dkc = jnp.concatenate(dkts, axis=0).T
                        dvc = jnp.concatenate(dvts, axis=0).T
                        if rb == 0:
                            last = slice((RB - 1) * BAND, RB * BAND)
                            dk_acc[prv, rr, last, cs] += dkc[0:BAND]
                            dv_acc[prv, rr, last, cs] += dvc[0:BAND]
                            dk_acc[cur, rr, 0:BAND, cs] += dkc[BAND:]
                            dv_acc[cur, rr, 0:BAND, cs] += dvc[BAND:]
                        else:
                            both = slice((rb - 1) * BAND, (rb + 1) * BAND)
                            dk_acc[cur, rr, both, cs] += dkc
                            dv_acc[cur, rr, both, cs] += dvc

        def flush(prv):
            dk_ref[...] = dk_acc[prv].astype(BF16)
            dv_ref[...] = dv_acc[prv].astype(BF16)

        for parity in (0, 1):
            on = (i % 2) == parity
            pl.when(on & (i < nb))(functools.partial(compute, parity, 1 - parity))
            pl.when(on & (i > 0))(functools.partial(flush, 1 - parity))
        if hosted:
            pl.when((j == n_j - 1) & (i == nb))(finish)

    qi = lambda i: jnp.minimum(i, nb - 1)
    cur_w = lambda w: pl.BlockSpec((None, RR, RB * BAND, GROUP_W), lambda j, i: (w, j, qi(i), 0))
    prev_w = lambda w: pl.BlockSpec((None, RR, BAND, GROUP_W),
                                    lambda j, i: (w, j, jnp.maximum(qi(i) * RB - 1, 0), 0))
    blk = pl.BlockSpec((RR, RB * BAND, GROUP_W), lambda j, i: (j, qi(i), 0))
    stat_blk = pl.BlockSpec((RR, RB * BAND, STAT_W), lambda j, i: (j, qi(i), 0))
    late = pl.BlockSpec((RR, RB * BAND, GROUP_W), lambda j, i: (j, jnp.maximum(i - 1, 0), 0))
    extra = [packs] if hosted else []
    return pl.pallas_call(
        body, grid=(n_j, nb + 1), name=f"attn_bwd_g{grp}",
        in_specs=[cur_w(0), cur_w(1), prev_w(1), cur_w(2), prev_w(2), blk, stat_blk, stat_blk] + [ANY] * hosted,
        out_specs=[blk, late, late] + [ANY] * hosted,
        out_shape=[SDS((d, L, GROUP_W), BF16)] * 3 + [SDS(t.shape, t.dtype) for t in extra],
        scratch_shapes=[pltpu.VMEM((2, RR, RB * BAND, GROUP_W), F32), pltpu.VMEM((2, RR, RB * BAND, GROUP_W), F32)]
        + [pltpu.SemaphoreType.DMA((N_DEV - 1,))] * (2 * hosted),
        input_output_aliases={8: 3} if hosted else {},
        compiler_params=_cp("arbitrary" if hosted else "parallel", "arbitrary"),
    )(qkv, qkv, qkv, qkv, qkv, da, lt, dd, *extra)


def _dz_assemble(dqkv, dqp):
    S = dqp.shape[0]
    n_tiles = S // TMB

    def body(*refs):
        dqkv_refs = refs[0:9]
        dqp_ref, halo_ref = refs[9:11]
        dz_ref, s_ref, ext_ref = refs[11:]
        i = pl.program_id(0)

        for grp in range(3):
            for which in range(3):
                n = which * 3 + grp
                ref = dqkv_refs[grp * 3 + which]
                if DILATIONS[grp] == 1:
                    dz_ref[n] = ref[0]
                else:
                    _interleave_load(ref, (), s_ref, DILATIONS[grp], TMB)
                    for h in range(2):
                        dz_ref[n, :, h * PAIR_W:(h + 1) * PAIR_W] = s_ref[h].astype(BF16)

        dqp = dqp_ref[...].astype(F32)
        ext_ref[0:TMB, :] = dqp
        ext_ref[TMB:, :] = jnp.where(i < n_tiles - 1, halo_ref[...].astype(F32), 0.0)
        sums = []
        acc = ext_ref[...]
        for k in (1, 2, 4, 8):
            acc = acc + pltpu.roll(acc, TMB + POOL_HALO - k, 0)
            sums.append(acc[0:TMB, :])
        inv_cnt, col = _pool_inv_count(i, TMB)
        dpz = _pool_column_select(col, sums) - dqp / inv_cnt
        for t in range(3):
            dz_ref[9 + t] = dpz[:, t * CHUNK:(t + 1) * CHUNK].astype(BF16)

    row = lambda w: pl.BlockSpec((TMB, w), lambda i: (i, 0))
    grp_spec = lambda d: pl.BlockSpec((d, TMB // d, GROUP_W), lambda i: (0, i, 0))
    halo = pl.BlockSpec((POOL_HALO, POOL_W),
                        lambda i: (jnp.minimum((i + 1) * (TMB // POOL_HALO), S // POOL_HALO - 1), 0))
    flat = [t for grp in range(3) for t in dqkv[grp]]
    return pl.pallas_call(
        body, grid=(n_tiles,), name="dz_assemble",
        in_specs=[grp_spec(DILATIONS[grp]) for grp in range(3) for _ in range(3)] + [row(POOL_W), halo],
        out_specs=pl.BlockSpec((N_DZ_CHUNKS, TMB, CHUNK), lambda i: (0, i, 0)),
        out_shape=SDS((N_DZ_CHUNKS, S, CHUNK), BF16),
        scratch_shapes=[pltpu.VMEM((2, TMB, PAIR_W), F32), pltpu.VMEM((TMB + POOL_HALO, POOL_W), F32)],
        compiler_params=_cp("parallel"),
    )(*flat, dqp, dqp)


def _inproj_dx(dz, dgates, dh1, x, g, w_in, sums):
    S = x.shape[0]
    n_tiles = S // TM
    n = len(sums)

    def body(*refs):
        dz_ref, dgate_ref, dh1_ref, x_ref, g_ref, w_ref = refs[0:6]
        sum_refs = refs[6:6 + n]
        dx_ref, dg_ref = refs[6 + n:8 + n]
        land_refs = refs[8 + n:8 + 2 * n]
        sems = refs[8 + 2 * n:]
        i = pl.program_id(0)

        def copies():
            return _chip_sum_copies(sum_refs, land_refs, *sems)

        @pl.when(i == 0)
        def _():
            dg_ref[...] = jnp.zeros_like(dg_ref)
            for cpy in copies():
                cpy.start()

        du = jnp.zeros((TM, D_MODEL), F32)
        for k in range(N_CHUNKS):
            dzk = dz_ref[k] if k < N_DZ_CHUNKS else dgate_ref[k - N_DZ_CHUNKS]
            du = du + _dot_nt(dzk, _w_in_chunk(w_ref, k))
        gv = g_ref[...]
        _, xh, r = _rms_fwd(x_ref[...], gv)
        dg_ref[...] += jnp.sum(du * xh, axis=0, keepdims=True)
        dx_ref[...] = dh1_ref[...] + _rms_bwd(du, xh, r, gv)

        @pl.when(i == n_tiles - 1)
        def _():
            for cpy in copies():
                cpy.wait()

    row = lambda w: pl.BlockSpec((TM, w), lambda i: (i, 0))
    res = pl.pallas_call(
        body, grid=(n_tiles,), name="inproj_dx",
        in_specs=[pl.BlockSpec((N_DZ_CHUNKS, TM, CHUNK), lambda i: (0, i, 0)),
                  pl.BlockSpec((N_CHUNKS - N_DZ_CHUNKS, TM, CHUNK), lambda i: (0, i, 0)),
                  row(D_MODEL), row(D_MODEL), _resident(g.shape), _resident(w_in.shape)] + [ANY] * n,
        out_specs=[row(D_MODEL), pl.BlockSpec((1, D_MODEL), lambda i: (0, 0))] + [ANY] * n,
        out_shape=[SDS((S, D_MODEL), F32), SDS((1, D_MODEL), F32)] + [SDS(t.shape, t.dtype) for t in sums],
        scratch_shapes=[pltpu.SemaphoreType.DMA((3 * n,)), pltpu.SemaphoreType.DMA((3 * n,))],
        compiler_params=_cp("arbitrary"),
    )(dz, dgates, dh1, x, g, w_in, *sums)
    return res[0], res[1], res[2:]


def _wgrad(a, b, name, *, out_shape, a_spec, b_spec, out_spec, grid, n_out_cols=None, fill=None, narrow=True):
    k_axis = len(grid) - 1
    n_k = grid[k_axis]
    n_out = 2 if narrow else 1

    def body(a_ref, b_ref, *rest):
        o_ref = rest[-n_out]

        @pl.when(pl.program_id(k_axis) == 0)
        def _():
            o_ref[...] = jnp.zeros_like(o_ref)

        at = a_ref[...]
        if n_out_cols is None:
            o_ref[...] += _dot_tn(at, b_ref[...])
        elif n_out_cols[0] == "lead_both":
            for t in range(b_ref.shape[0]):
                o_ref[t] += _dot_tn(at, b_ref[t])
        else:
            w = n_out_cols[1]
            for t in range(o_ref.shape[0]):
                o_ref[t] += _dot_tn(at, b_ref[:, t * w:(t + 1) * w])

        if narrow:
            @pl.when(pl.program_id(k_axis) == n_k - 1)
            def _():
                rest[-1][...] = o_ref[...].astype(BF16)

    sem = ("parallel",) * k_axis + ("arbitrary",)
    extra = [] if fill is None else list(fill) if narrow else [fill]
    shapes = [out_shape, SDS(out_shape.shape, BF16)] if narrow else out_shape
    return pl.pallas_call(body, grid=grid, name=name, in_specs=[a_spec, b_spec] + [ANY] * len(extra),
                          out_specs=[out_spec] * n_out if narrow else out_spec, out_shape=shapes,
                          input_output_aliases={2 + t: t for t in range(len(extra))},
                          compiler_params=_cp(*sem))(a, b, *extra)


def _wgrad_in(u, dz, dgates):
    bk = min(BK, u.shape[0])
    nk = u.shape[0] // bk
    g = WGRAD_IN_GROUP
    kw = dict(n_out_cols=("lead_both", CHUNK), a_spec=pl.BlockSpec((bk, D_MODEL), lambda j, k: (k, 0)),
              b_spec=pl.BlockSpec((g, bk, CHUNK), lambda j, k: (j, k, 0)),
              out_shape=SDS((N_CHUNKS, D_MODEL, CHUNK), F32))
    first = _wgrad(u, dz, "wgrad_in_qkvp", grid=(N_DZ_CHUNKS // g, nk),
                   out_spec=pl.BlockSpec((g, D_MODEL, CHUNK), lambda j, k: (j, 0, 0)), **kw)
    both = _wgrad(u, dgates, "wgrad_in_gates", grid=((N_CHUNKS - N_DZ_CHUNKS) // g, nk), fill=first,
                  out_spec=pl.BlockSpec((g, D_MODEL, CHUNK), lambda j, k: (N_DZ_CHUNKS // g + j, 0, 0)), **kw)
    return [t.reshape(N_CHIPS, CHUNKS_PER_SHARD * D_MODEL, CHUNK) for t in both]


def _wgrads_mixer(a, da1, p, dp1, merged, dh1b, pooled, dmixed):
    bk = min(BK, a.shape[0])
    nk = a.shape[0] // bk
    g_ao = _wgrad(
        a, da1, "wgrad_att_out", grid=(nk,), n_out_cols=("cols_b", CHUNK),
        a_spec=pl.BlockSpec((bk, GROUP_W), lambda k: (k, 0)),
        b_spec=pl.BlockSpec((bk, D_MODEL), lambda k: (k, 0)),
        out_spec=pl.BlockSpec((N_CHIPS, GROUP_W, CHUNK), lambda k: (0, 0, 0)),
        out_shape=SDS((N_CHIPS, GROUP_W, CHUNK), F32))
    g_po = _wgrad(
        p, dp1, "wgrad_pool_out", grid=(nk,), n_out_cols=("cols_b", CHUNK),
        a_spec=pl.BlockSpec((bk, POOL_W), lambda k: (k, 0)),
        b_spec=pl.BlockSpec((bk, D_MODEL), lambda k: (k, 0)),
        out_spec=pl.BlockSpec((N_CHIPS, POOL_W, CHUNK), lambda k: (0, 0, 0)),
        out_shape=SDS((N_CHIPS, POOL_W, CHUNK), F32))
    g_out = _wgrad(
        merged, dh1b, "wgrad_out", grid=(nk,),
        a_spec=pl.BlockSpec((bk, D_MODEL), lambda k: (k, 0)),
        b_spec=pl.BlockSpec((bk, D_MODEL), lambda k: (k, 0)),
        out_spec=pl.BlockSpec((D_MODEL, D_MODEL), lambda k: (0, 0)),
        out_shape=SDS((D_MODEL, D_MODEL), F32))
    g_bd = _wgrad(
        pooled, dmixed, "wgrad_pool_grp", grid=(nk,),
        a_spec=pl.BlockSpec((bk, POOL_W), lambda k: (k, 0)),
        b_spec=pl.BlockSpec((bk, POOL_W), lambda k: (k, 0)),
        out_spec=pl.BlockSpec((POOL_W, POOL_W), lambda k: (0, 0)),
        out_shape=SDS((POOL_W, POOL_W), F32), narrow=False)
    g_out = [t.reshape(N_CHIPS, D_MODEL // N_CHIPS, D_MODEL) for t in g_out]
    return [g_ao, g_po, g_out], g_bd


def _wgrads_mlp(m, dpre, hid, dh2b):
    bk = min(BK, m.shape[0])
    nk = m.shape[0] // bk
    g_mi = _wgrad(
        m, dpre, "wgrad_mlp_in", grid=(N_CHIPS, nk),
        a_spec=pl.BlockSpec((bk, D_MODEL), lambda c, k: (k, 0)),
        b_spec=pl.BlockSpec((bk, D_MODEL), lambda c, k: (k, c)),
        out_spec=pl.BlockSpec((None, D_MODEL, D_MODEL), lambda c, k: (c, 0, 0)),
        out_shape=SDS((N_CHIPS, D_MODEL, D_MODEL), F32))
    g_mo = _wgrad(
        hid, dh2b, "wgrad_mlp_out", grid=(N_CHIPS, nk),
        a_spec=pl.BlockSpec((bk, D_MODEL), lambda c, k: (k, c)),
        b_spec=pl.BlockSpec((bk, D_MODEL), lambda c, k: (k, 0)),
        out_spec=pl.BlockSpec((None, D_MODEL, D_MODEL), lambda c, k: (c, 0, 0)),
        out_shape=SDS((N_CHIPS, D_MODEL, D_MODEL), F32))
    return [g_mi, g_mo]


def _mesh_place():
    x, y, c = lax.axis_index("x"), lax.axis_index("y"), lax.axis_index("c")
    other_chips = [(x, 1 - y), (1 - x, y), (1 - x, 1 - y)]
    return x, y, c, other_chips


ANY = pl.BlockSpec(memory_space=pl.ANY)


def _weight_half_copies(shard_refs, buf_refs, rows, send_sem, recv_sem):
    x, y, c, chips = _mesh_place()
    me = 2 * x + y
    copies = []
    for w, r_full in enumerate(rows):
        rh = r_full // 2
        for r, (px, py) in enumerate(chips):
            k = w * 3 + r
            copies.append(pltpu.make_async_remote_copy(
                src_ref=shard_refs[w].at[pl.ds(c * rh, rh), :], dst_ref=buf_refs[w].at[me, pl.ds(c * rh, rh), :],
                send_sem=send_sem.at[k], recv_sem=recv_sem.at[k], device_id=(px, py, c), device_id_type=MESH))
    return copies


def _pair_forward_copies(buf_refs, rows, send_sem, recv_sem):
    x, y, c, chips = _mesh_place()
    out = []
    for w, r_full in enumerate(rows):
        rh = r_full // 2
        for r, (px, py) in enumerate(chips):
            k = w * 3 + r
            landed = buf_refs[w].at[2 * px + py, pl.ds(c * rh, rh), :]
            theirs = buf_refs[w].at[2 * px + py, pl.ds((1 - c) * rh, rh), :]
            mk = lambda ref: pltpu.make_async_remote_copy(
                src_ref=ref, dst_ref=ref, send_sem=send_sem.at[k], recv_sem=recv_sem.at[k],
                device_id=(x, y, 1 - c), device_id_type=MESH)
            out.append((mk(landed), mk(theirs)))
    return out


def _place_own(block, n_slots, slot):
    buf = lax.empty((n_slots,) + block.shape, block.dtype)
    return lax.dynamic_update_slice(buf, block[None], (slot,) + (0,) * block.ndim)


def _pair_forward(bufs, rows, name):
    n = len(bufs)

    def body(*refs):
        dst = refs[n:2 * n]
        send_sem, recv_sem = refs[2 * n:]
        fwds = _pair_forward_copies(dst, rows, send_sem, recv_sem)
        for fwd, _ in fwds:
            fwd.start()
        for fwd, landing in fwds:
            landing.wait_recv()
            fwd.wait_send()

    return pl.pallas_call(
        body, name=name,
        in_specs=[ANY] * n, out_specs=[ANY] * n,
        out_shape=[SDS(b.shape, b.dtype) for b in bufs],
        scratch_shapes=[pltpu.SemaphoreType.DMA((3 * n,))] * 2,
        input_output_aliases={w: w for w in range(n)},
    )(*bufs)


def _chip_sum_copies(src, dst, send_sem, recv_sem):
    x, y, c, chips = _mesh_place()
    copies = []
    for w in range(len(src)):
        for r, (px, py) in enumerate(chips):
            k = w * 3 + r
            copies.append(pltpu.make_async_remote_copy(
                src_ref=src[w].at[r + 1], dst_ref=dst[w].at[r + 1], send_sem=send_sem.at[k], recv_sem=recv_sem.at[k],
                device_id=(px, py, c), device_id_type=MESH))
    return copies


def _pair_exchange(grads):
    n = len(grads)

    def body(*refs):
        src, dst = refs[:n], refs[n:2 * n]
        send_sem, recv_sem = refs[2 * n:]
        x, y, c, _ = _mesh_place()
        copies = []
        for w in range(n):
            rh = grads[w].shape[1] // 2
            copies.append(pltpu.make_async_remote_copy(
                src_ref=src[w].at[:, pl.ds((1 - c) * rh, rh), :], dst_ref=dst[w],
                send_sem=send_sem.at[w], recv_sem=recv_sem.at[w],
                device_id=(x, y, 1 - c), device_id_type=MESH))
            copies[-1].start()
        for cpy in copies:
            cpy.wait()

    return pl.pallas_call(
        body, name="grad_pair_exchange",
        in_specs=[ANY] * n, out_specs=[ANY] * n,
        out_shape=[SDS((N_CHIPS, g.shape[1] // 2, g.shape[2]), g.dtype) for g in grads],
        scratch_shapes=[pltpu.SemaphoreType.DMA((n,)), pltpu.SemaphoreType.DMA((n,))],
    )(*grads)


def _pair_sum(place, grad, recv, name):
    _, R, C = grad.shape
    rh = R // 2
    br = _row_block(rh, max(256, ELEMENTWISE_BLOCK // C))
    nbh = rh // br

    def body(place_ref, g_ref, r_ref, own_ref, sums_ref):
        s = g_ref[...] + r_ref[...].astype(F32)

        @pl.when(pl.program_id(1) == 0)
        def _():
            own_ref[...] = s

        sums_ref[...] = s.astype(BF16)

    slot = lambda rel, pr: jnp.bitwise_xor(pr[0], rel)
    return pl.pallas_call(
        body, name=name,
        grid_spec=pltpu.PrefetchScalarGridSpec(
            num_scalar_prefetch=1, grid=(nbh, N_CHIPS),
            in_specs=[pl.BlockSpec((None, br, C), lambda i, rel, pr: (slot(rel, pr), pr[1] * nbh + i, 0)),
                      pl.BlockSpec((None, br, C), lambda i, rel, pr: (slot(rel, pr), i, 0))],
            out_specs=[pl.BlockSpec((br, C), lambda i, rel, pr: (i, 0)),
                       pl.BlockSpec((None, br, C), lambda i, rel, pr: (rel, i, 0))]),
        out_shape=[SDS((rh, C), F32), SDS((N_CHIPS, rh, C), BF16)],
        compiler_params=_cp("parallel", "arbitrary"),
    )(place, grad, recv)


def _chip_sum(place, own, recv, name):
    rh, C = own.shape
    br = _row_block(rh, max(256, ELEMENTWISE_BLOCK // C))
    nbh = rh // br

    def body(place_ref, own_ref, r_ref, o_ref):
        o_ref[...] = ((own_ref[...] + r_ref[1].astype(F32)) + r_ref[2].astype(F32)) + r_ref[3].astype(F32)

    return pl.pallas_call(
        body, name=name,
        grid_spec=pltpu.PrefetchScalarGridSpec(
            num_scalar_prefetch=1, grid=(nbh,),
            in_specs=[pl.BlockSpec((br, C), lambda i, pr: (i, 0)),
                      pl.BlockSpec((N_CHIPS, br, C), lambda i, pr: (0, i, 0))],
            out_specs=pl.BlockSpec((br, C), lambda i, pr: (pr[1] * nbh + i, 0))),
        out_shape=SDS((2 * rh, C), F32),
        compiler_params=_cp("parallel"),
    )(place, own, recv)


def _pack_allgather(all_ref, send_sem, recv_sem):
    x, y, c, chips = _mesh_place()
    sib = (x, y, 1 - c)

    def pack(dev, k, to):
        slot = 4 * dev[0] + 2 * dev[1] + dev[2]
        return pltpu.make_async_remote_copy(
            src_ref=all_ref.at[slot], dst_ref=all_ref.at[slot], send_sem=send_sem.at[k],
            recv_sem=recv_sem.at[k], device_id=to, device_id_type=MESH)

    first = [pack((x, y, c), 0, sib)] + [pack((x, y, c), 1 + r, (px, py, c)) for r, (px, py) in enumerate(chips)]
    relays = [pack((px, py, c), 4 + r, sib) for r, (px, py) in enumerate(chips)]

    def start():
        for cpy in first:
            cpy.start()

    def relay():
        for r, (px, py) in enumerate(chips):
            pack((px, py, c), 1 + r, (px, py, c)).wait_recv()
            relays[r].start()

    def finish():
        pack(sib, 0, sib).wait_recv()
        for r, (px, py) in enumerate(chips):
            pack((px, py, 1 - c), 4 + r, sib).wait_recv()
        for cpy in first + relays:
            cpy.wait_send()

    return start, relay, finish


def _finish_exchange(grads, late_all):
    n = len(grads)

    def body(*refs):
        dst, all_ref = refs[n + 1:2 * n + 1], refs[2 * n + 1]
        send_sem, recv_sem, ssend_sem, srecv_sem = refs[2 * n + 2:]
        x, y, c, _ = _mesh_place()
        start, relay, finish = _pack_allgather(all_ref, ssend_sem, srecv_sem)
        start()
        sends, landings = [], []
        for w in range(n):
            rh = grads[w].shape[0] // 2
            mk = lambda cc: pltpu.make_async_remote_copy(
                src_ref=dst[w].at[pl.ds(cc * rh, rh), :], dst_ref=dst[w].at[pl.ds(cc * rh, rh), :],
                send_sem=send_sem.at[w], recv_sem=recv_sem.at[w], device_id=(x, y, 1 - c), device_id_type=MESH)
            sends.append(mk(c))
            landings.append(mk(1 - c))
            sends[-1].start()
        relay()
        finish()
        for cpy in landings:
            cpy.wait_recv()
        for cpy in sends:
            cpy.wait_send()

    res = pl.pallas_call(
        body, name="grad_finish_exchange",
        in_specs=[ANY] * (n + 1), out_specs=[ANY] * (n + 1),
        out_shape=[SDS(g.shape, g.dtype) for g in grads] + [SDS(late_all.shape, late_all.dtype)],
        scratch_shapes=[pltpu.SemaphoreType.DMA((n,)), pltpu.SemaphoreType.DMA((n,)),
                        pltpu.SemaphoreType.DMA((N_DEV - 1,)), pltpu.SemaphoreType.DMA((N_DEV - 1,))],
        input_output_aliases={w: w for w in range(n + 1)},
    )(*grads, late_all)
    return res[:n], res[n]


def _adamw_math(w, g, m, v):
    m = ADAM_B1 * m + (1.0 - ADAM_B1) * g
    v = ADAM_B2 * v + (1.0 - ADAM_B2) * jnp.square(g)
    m_hat = m / (1.0 - ADAM_B1 ** ADAM_STEP)
    v_hat = v / (1.0 - ADAM_B2 ** ADAM_STEP)
    delta = -ADAM_LR * (m_hat / (jnp.sqrt(v_hat) + ADAM_EPS) + ADAM_WD * w)
    return delta, m, v


def _adamw(w, g, m, v, name):
    R, C = w.shape
    br = _row_block(R, 512)
    if g.ndim == 3:
        n_chunks, cw = g.shape[0], g.shape[2]
        g_spec = pl.BlockSpec((None, br, cw), lambda t, i: (t, i, 0))
    else:
        n_chunks, cw = 1, C
        g_spec = pl.BlockSpec((br, cw), lambda t, i: (i, t))

    def body(w_ref, g_ref, m_ref, v_ref, g_out_ref, d_ref, nm_ref, nv_ref):
        gv = g_ref[...]
        g_out_ref[...] = gv
        d_ref[...], nm_ref[...], nv_ref[...] = _adamw_math(w_ref[...], gv, m_ref[...], v_ref[...])

    spec = pl.BlockSpec((br, cw), lambda t, i: (i, t))
    return pl.pallas_call(
        body, grid=(n_chunks, R // br), name=name, in_specs=[spec, g_spec, spec, spec], out_specs=[spec] * 4,
        out_shape=[SDS((R, C), F32)] * 4, compiler_params=_cp("parallel", "parallel"),
    )(w, g, m, v)


def _small_sum_adamw(all_small, all_late, w, m, v):
    loss_row = PACK_ROWS - 8

    def body(all_ref, late_ref, w_ref, m_ref, v_ref, g_ref, d_ref, nm_ref, nv_ref, loss_ref):
        g = all_ref[0]
        late = late_ref[0]
        for k in range(1, N_DEV):
            g = g + all_ref[k]
            late = late + late_ref[k]
        g_ref[...] = g
        g_ref[PACK_LATE_ROW:PACK_LATE_ROW + 8, :] = late
        g = g_ref[...]
        d_ref[...], nm_ref[...], nv_ref[...] = _adamw_math(w_ref[...], g, m_ref[...], v_ref[...])
        total = jnp.sum(g[loss_row:loss_row + 1, :]) * (0.5 / D_MODEL)
        loss_ref[...] = jnp.full(loss_ref.shape, total, F32)

    full = lambda s: pl.BlockSpec(s, lambda i: (0,) * len(s))
    pack = (PACK_ROWS, D_MODEL)
    return pl.pallas_call(
        body, grid=(1,), name="small_sum_adamw",
        in_specs=[full((N_DEV,) + pack), full((N_DEV, 8, D_MODEL)), full(pack), full(pack), full(pack)],
        out_specs=[full(pack)] * 4 + [full((8, 128))],
        out_shape=[SDS(pack, F32)] * 4 + [SDS((8, 128), F32)],
        compiler_params=_cp("arbitrary"),
    )(all_small, all_late, w, m, v)


def _pack_small(grp, scale, g_mix, g_mlp, g_f, loss_lanes):
    def part(vec):
        vec = vec.reshape(1, -1)
        return jnp.pad(vec, ((0, 7), (0, D_MODEL - vec.shape[1])))
    return jnp.concatenate([grp.reshape(-1, D_MODEL), part(scale), part(g_mix), part(g_mlp), part(g_f),
                            part(loss_lanes)], axis=0)


def _unpack_small(pack):
    n_grp = len(POOL_WINDOWS) * POOL_GROUP_W * POOL_GROUP_W // D_MODEL
    grp = pack[:n_grp].reshape(1, len(POOL_WINDOWS), POOL_GROUP_W, POOL_GROUP_W)
    scale = pack[n_grp, :POOL_W].reshape(1, POOL_W)
    g_mix = pack[n_grp + 8].reshape(1, D_MODEL)
    g_mlp = pack[n_grp + 16].reshape(1, D_MODEL)
    g_f = pack[n_grp + 24].reshape(D_MODEL)
    return grp, scale, g_mix, g_mlp, g_f


def _block_diag(grp):
    out = jnp.zeros((POOL_W, POOL_W), grp.dtype)
    for k in range(len(POOL_WINDOWS)):
        out = lax.dynamic_update_slice(out, grp[k], (k * POOL_GROUP_W, k * POOL_GROUP_W))
    return out


def kernel(x, norm_mix_g, w_in, w_att_out, w_pool_grp, pool_scale, w_pool_out, w_out, norm_mlp_g, w_mlp_in, w_mlp_out, norm_final_g, loss_target, m_norm_mix_g, m_w_in, m_w_att_out, m_w_pool_grp, m_pool_scale, m_w_pool_out, m_w_out, m_norm_mlp_g, m_w_mlp_in, m_w_mlp_out, m_norm_final_g, v_norm_mix_g, v_w_in, v_w_att_out, v_w_pool_grp, v_pool_scale, v_w_pool_out, v_w_out, v_norm_mlp_g, v_w_mlp_in, v_w_mlp_out, v_norm_final_g):
    S = x.shape[1]
    xs, target = x[0], loss_target[0]
    big = [w_in[0], w_att_out[0], w_pool_out[0], w_out[0], w_mlp_in[0], w_mlp_out[0]]
    big_m = [m_w_in[0], m_w_att_out[0], m_w_pool_out[0], m_w_out[0], m_w_mlp_in[0], m_w_mlp_out[0]]
    big_v = [v_w_in[0], v_w_att_out[0], v_w_pool_out[0], v_w_out[0], v_w_mlp_in[0], v_w_mlp_out[0]]

    chip = 2 * lax.axis_index("x") + lax.axis_index("y")
    core = lax.axis_index("c")
    place = jnp.stack([chip, core]).astype(jnp.int32)
    names = ("w_in", "w_att_out", "w_pool_out", "w_out", "w_mlp_in", "w_mlp_out")

    shards = [w.astype(BF16) for w in big]
    bufs = [_place_own(sh, N_CHIPS, chip) for sh in shards]
    wbd = _block_diag(w_pool_grp[0]).astype(BF16)
    g_final = norm_final_g.reshape(1, D_MODEL)
    stat_expand, stat_reduce = _stat_matrices()

    u, z_own, landed_in = _norm_inproj_own(xs, norm_mix_g, shards[0], bufs[0])
    (wg_in,) = _pair_forward([landed_in], [shards[0].shape[0]], "w_in_pair_forward")
    (qkv0, qkv1, qkv2, pz, gates), (wg_ao, wg_po, wg_out) = _inproj_rest(u, z_own, wg_in, shards[1:4], bufs[1:4])
    wg_out = wg_out.reshape(D_MODEL, D_MODEL)
    qkv = (qkv0, qkv1, qkv2)
    att = [_attn_fwd(qkv[grp], grp) for grp in range(3)]
    (a, lt0, lt1, lt2, pooled, mixed, p, merged, h1, m), (wg_mi, wg_mo) = _mixer_out(
        [o for o, _ in att], [l for _, l in att], pz, gates, xs, wg_ao, wg_po, wbd, pool_scale, wg_out, norm_mlp_g,
        stat_expand, shards[4:], bufs[4:])
    hid, dh2, dh2b, loss_lanes, dg_final = _mlp_fwd_loss(m, h1, target, wg_mi, wg_mo, g_final)

    def pair_reduce(grads, grad_names):
        recv = _pair_exchange([narrow for _, narrow in grads])
        pair = [_pair_sum(place, g, r, f"pair_sum_{nm}") for (g, _), r, nm in zip(grads, recv, grad_names)]
        return [own for own, _ in pair], [s for _, s in pair]

    def chip_reduce(owns, landed_sums, grad_names):
        return [_chip_sum(place, own, r, f"chip_sum_{nm}") for own, r, nm in zip(owns, landed_sums, grad_names)]

    dpre, dh1, dh1b, dg_mlp = _mlp_bwd(dh2, dh2b, hid, h1, wg_mi, wg_mo, norm_mlp_g)
    own_mlp, sums_mlp = pair_reduce(_wgrads_mlp(m, dpre, hid, dh2b), names[4:])
    (da1, dp1, dgates, da0, dag1, dag2, dd0, dd1, dd2, dmixed, dqp, dscale), landed_mlp = _mixer_bwd(
        dh1b, a, p, mixed, gates, wg_out, wg_ao, wg_po, wbd, pool_scale, stat_reduce, sums_mlp)
    g_mi, g_mo = chip_reduce(own_mlp, landed_mlp, names[4:])
    grads_mixer, g_bd = _wgrads_mixer(a, da1, p, dp1, merged, dh1b, pooled, dmixed)

    zero = jnp.zeros((D_MODEL,), F32)
    g_grp = jnp.stack([g_bd[k * POOL_GROUP_W:(k + 1) * POOL_GROUP_W, k * POOL_GROUP_W:(k + 1) * POOL_GROUP_W]
                       for k in range(len(POOL_WINDOWS))])
    small = _pack_small(g_grp, dscale, zero, dg_mlp, dg_final, loss_lanes)
    *dqkv0, small_all = _attn_bwd(qkv[0], da0, lt0, dd0, 0, packs=_place_own(small, N_DEV, 2 * chip + core))
    dqkv = [dqkv0, _attn_bwd(qkv[1], dag1, lt1, dd1, 1), _attn_bwd(qkv[2], dag2, lt2, dd2, 2)]
    dz = _dz_assemble(dqkv, dqp)
    own_in, sums_in = pair_reduce([_wgrad_in(u, dz, dgates)] + grads_mixer, names[:4])
    dx, dg_mix, landed_in = _inproj_dx(dz, dgates, dh1, xs, norm_mix_g, wg_in, sums_in)
    g_in, g_ao, g_po, g_out = chip_reduce(own_in, landed_in, names[:4])
    late = jnp.pad(dg_mix, ((0, 7), (0, 0)))
    full, late_all = _finish_exchange([g_in, g_ao, g_po, g_out, g_mi, g_mo], _place_own(late, N_DEV, 2 * chip + core))

    small_w = _pack_small(w_pool_grp[0], pool_scale, norm_mix_g, norm_mlp_g, norm_final_g, zero)
    small_m = _pack_small(m_w_pool_grp[0], m_pool_scale, m_norm_mix_g, m_norm_mlp_g, m_norm_final_g, zero)
    small_v = _pack_small(v_w_pool_grp[0], v_pool_scale, v_norm_mix_g, v_norm_mlp_g, v_norm_final_g, zero)
    sg, sd, sm, sv, loss_tile = _small_sum_adamw(small_all, late_all, small_w, small_m, small_v)
    full = [full[0].reshape(CHUNKS_PER_SHARD, D_MODEL, CHUNK)] + list(full[1:])
    upd = [_adamw(w, g, mm, vv, f"adamw_{nm}") for w, g, mm, vv, nm in zip(big, full, big_m, big_v, names)]

    def ordered(small_pack, bigs):
        grp, scale, g_mix, g_mlp, g_f = _unpack_small(small_pack)
        b_in, b_ao, b_po, b_out, b_mi, b_mo = [b[None] for b in bigs]
        return (g_mix, b_in, b_ao, grp, scale, b_po, b_out, g_mlp, b_mi, b_mo, g_f)

    return (loss_tile[0, 0], dx[None],
            *ordered(sg, [t[0] for t in upd]),
            *ordered(sd, [t[1] for t in upd]),
            *ordered(sm, [t[2] for t in upd]),
            *ordered(sv, [t[3] for t in upd]))
```

```python
import functools

import jax
import jax.numpy as jnp
from jax import lax
from jax.experimental import pallas as pl
from jax.experimental.pallas import tpu as pltpu

F32 = jnp.float32
BF16 = jnp.bfloat16
SDS = jax.ShapeDtypeStruct
MESH = pl.DeviceIdType.MESH

D_MODEL = 1024
D_FF = 4096
N_CHIPS = 4
N_DEV = 8
DILATIONS = (1, 4, 16)
BAND = 128
GROUP_W = 256
PAIR_W = 128
HEAD_W = 64
STAT_W = 128
STAT_HEAD_W = 32
POOL_W = 768
POOL_GROUP_W = 192
POOL_WINDOWS = (2, 4, 8, 16)
POOL_HALO = 16
N_IN = 5120
CHUNK = 256
N_CHUNKS = N_IN // CHUNK
N_DZ_CHUNKS = 12
CHUNKS_PER_SHARD = 5
WGRAD_IN_GROUP = 4
NORM_EPS = 1e-6
ALIBI_MAX_BIAS = 8.0
N_HEADS = 12
NEG = -1e30

ADAM_LR, ADAM_B1, ADAM_B2, ADAM_EPS, ADAM_WD, ADAM_STEP = 0.001, 0.9, 0.999, 1e-08, 0.01, 10

TM = 512
TMB = 512
ATT_TILE = ((1, 8), (4, 2), (8, 1))
BK = 4096
ELEMENTWISE_BLOCK = 1 << 20
VMEM_LIMIT = 56 * 1024 * 1024
PACK_ROWS = 184
PACK_LATE_ROW = 152

NT = (((1,), (1,)), ((), ()))
TN = (((0,), (0,)), ((), ()))


def _cp(*sem):
    return pltpu.CompilerParams(dimension_semantics=sem, vmem_limit_bytes=VMEM_LIMIT)


def _resident(shape):
    nd = len(shape)
    return pl.BlockSpec(shape, lambda *_: (0,) * nd, pipeline_mode=pl.Buffered(1))


def _row_block(rows, cap=256):
    return max(b for b in range(16, min(rows, cap) + 1, 16) if rows % b == 0)


def _dot(a, b):
    return jnp.dot(a, b, preferred_element_type=F32)


def _dot_nt(a, b):
    return lax.dot_general(a, b, NT, preferred_element_type=F32)


def _dot_tn(a, b):
    return lax.dot_general(a, b, TN, preferred_element_type=F32)


def _w_in_chunk(w_ref, n):
    return w_ref[n // CHUNKS_PER_SHARD, :, (n % CHUNKS_PER_SHARD) * CHUNK:(n % CHUNKS_PER_SHARD + 1) * CHUNK]


def _sigmoid(x):
    return 0.5 * jnp.tanh(0.5 * x.astype(F32)) + 0.5


def _rms_fwd(x, g):
    r = lax.rsqrt(jnp.mean(x * x, axis=-1, keepdims=True) + NORM_EPS)
    xh = x * r
    return xh * g, xh, r


def _rms_bwd(dy, xh, r, g):
    dxh = dy * g
    return r * (dxh - xh * jnp.mean(dxh * xh, axis=-1, keepdims=True))


def _per_head_lanes(cols):
    rows = cols[0].shape[0]
    lane = lax.broadcasted_iota(jnp.int32, (rows, STAT_W), 1)
    out = cols[3]
    for h in (2, 1, 0):
        out = jnp.where(lane < (h + 1) * STAT_HEAD_W, cols[h], out)
    return out


def _head_col(stat, h):
    return stat[:, h * STAT_HEAD_W:h * STAT_HEAD_W + 1]


def _stat_matrices():
    s = lax.broadcasted_iota(jnp.int32, (STAT_W, GROUP_W), 0)
    c = lax.broadcasted_iota(jnp.int32, (STAT_W, GROUP_W), 1)
    expand = (s == (c // HEAD_W) * STAT_HEAD_W).astype(BF16)
    reduce = (s // STAT_HEAD_W == c // HEAD_W).astype(BF16).T
    return expand, reduce


def _dot_split(x, m):
    hi = x.astype(BF16)
    lo = (x - hi.astype(F32)).astype(BF16)
    return _dot(hi, m) + _dot(lo, m)


def _deinterleave_store(val, s_ref, out_ref, lead, d, rows, dtype):
    if d == 1:
        out_ref[lead + (0,)] = val.astype(dtype)
        return
    for h in range(2):
        s_ref[h] = val[:, h * PAIR_W:(h + 1) * PAIR_W]
    for r in range(d):
        for h in range(2):
            out_ref[lead + (r, slice(None), slice(h * PAIR_W, (h + 1) * PAIR_W))] = (
                s_ref[h, pl.ds(r, rows // d, stride=d), :].astype(dtype))


def _interleave_load(in_ref, lead, s_ref, d, rows):
    for r in range(d):
        for h in range(2):
            s_ref[h, pl.ds(r, rows // d, stride=d), :] = (
                in_ref[lead + (r, slice(None), slice(h * PAIR_W, (h + 1) * PAIR_W))].astype(F32))


def _norm_inproj_own(x, g, w_own, buf):
    S = x.shape[0]
    n_tiles = S // TM

    def body(x_ref, g_ref, w_ref, shard_ref, buf_in, u_ref, z_ref, buf_ref, send_sem, recv_sem):
        i = pl.program_id(0)

        def copies():
            return _weight_half_copies([shard_ref], [buf_ref], [w_own.shape[0]], send_sem, recv_sem)

        @pl.when(i == 0)
        def _():
            for cpy in copies():
                cpy.start()

        u = _rms_fwd(x_ref[...], g_ref[...])[0].astype(BF16)
        u_ref[...] = u
        for t in range(CHUNKS_PER_SHARD):
            z_ref[t] = _dot(u, w_ref[:, t * CHUNK:(t + 1) * CHUNK]).astype(BF16)

        @pl.when(i == n_tiles - 1)
        def _():
            for cpy in copies():
                cpy.wait()

    row = lambda w: pl.BlockSpec((TM, w), lambda i: (i, 0))
    return pl.pallas_call(
        body, grid=(n_tiles,), name="norm_inproj_own",
        in_specs=[row(D_MODEL), _resident((1, D_MODEL)), _resident(w_own.shape), ANY, ANY],
        out_specs=[row(D_MODEL), pl.BlockSpec((CHUNKS_PER_SHARD, TM, CHUNK), lambda i: (0, i, 0)), ANY],
        out_shape=[SDS((S, D_MODEL), BF16), SDS((CHUNKS_PER_SHARD, S, CHUNK), BF16), SDS(buf.shape, buf.dtype)],
        scratch_shapes=[pltpu.SemaphoreType.DMA((3,)), pltpu.SemaphoreType.DMA((3,))],
        input_output_aliases={4: 2},
        compiler_params=_cp("arbitrary"),
    )(x, g, w_own, w_own, buf)


def _hosted_allgather(i, n_steps, shard_refs, buf_refs, rows, sems):
    send_sem, recv_sem, fsend_sem, frecv_sem = sems
    ici = lambda: _weight_half_copies(shard_refs, buf_refs, rows, send_sem, recv_sem)
    forward = lambda: _pair_forward_copies(buf_refs, rows, fsend_sem, frecv_sem)

    def begin():
        @pl.when(i == 0)
        def _():
            for cpy in ici():
                cpy.start()

        @pl.when(i == n_steps // 2)
        def _():
            for cpy, (fwd, _) in zip(ici(), forward()):
                cpy.wait_recv()
                fwd.start()

    def end():
        @pl.when(i == n_steps - 1)
        def _():
            for cpy, (fwd, landing) in zip(ici(), forward()):
                landing.wait_recv()
                fwd.wait_send()
                cpy.wait_send()

    return begin, end


def _inproj_rest(u, z_own, w_in, shards, bufs):
    S = u.shape[0]
    n_tiles = S // TM
    n = len(shards)

    def body(*refs):
        u_ref, zown_ref, w_ref = refs[0:3]
        shard_refs = refs[3:3 + n]
        q0_ref, q1_ref, q2_ref, pz_ref, gate_ref = refs[3 + 2 * n:8 + 2 * n]
        buf_refs = refs[8 + 2 * n:8 + 3 * n]
        s_ref = refs[8 + 3 * n]
        i = pl.program_id(0)
        chip = 2 * lax.axis_index("x") + lax.axis_index("y")
        begin, end = _hosted_allgather(i, n_tiles, shard_refs, buf_refs, [sh.shape[0] for sh in shards],
                                       refs[9 + 3 * n:])
        begin()

        u = u_ref[...]
        qkv_refs = (q0_ref, q1_ref, q2_ref)

        def emit(k, zc):
            if k < 9:
                which, grp = k // 3, k % 3
                if which == 0:
                    zc = zc * 0.125
                _deinterleave_store(zc, s_ref, qkv_refs[grp], (which,), DILATIONS[grp], TM, BF16)
            elif k < N_DZ_CHUNKS:
                pz_ref[:, (k - 9) * CHUNK:(k - 8) * CHUNK] = zc.astype(BF16)
            else:
                gate_ref[:, (k - N_DZ_CHUNKS) * CHUNK:(k - N_DZ_CHUNKS + 1) * CHUNK] = zc.astype(BF16)

        def all_chunks(own_shard):
            for k in range(N_CHUNKS):
                if k // CHUNKS_PER_SHARD == own_shard:
                    emit(k, zown_ref[k % CHUNKS_PER_SHARD].astype(F32))
                else:
                    emit(k, _dot(u, _w_in_chunk(w_ref, k)))

        for shard in range(N_CHIPS):
            pl.when(chip == shard)(functools.partial(all_chunks, shard))
        end()

    row = lambda w: pl.BlockSpec((TM, w), lambda i: (i, 0))
    res = pl.pallas_call(
        body, grid=(n_tiles,), name="inproj_rest",
        in_specs=[row(D_MODEL), pl.BlockSpec((CHUNKS_PER_SHARD, TM, CHUNK), lambda i: (0, i, 0)),
                  _resident(w_in.shape)] + [ANY] * (2 * n),
        out_specs=[pl.BlockSpec((3, d, TM // d, GROUP_W), lambda i: (0, 0, i, 0)) for d in DILATIONS]
        + [row(POOL_W), row(2 * D_MODEL)] + [ANY] * n,
        out_shape=[SDS((3, d, S // d, GROUP_W), BF16) for d in DILATIONS]
        + [SDS((S, POOL_W), BF16), SDS((S, 2 * D_MODEL), BF16)] + [SDS(b.shape, b.dtype) for b in bufs],
        scratch_shapes=[pltpu.VMEM((2, TM, PAIR_W), F32)] + [pltpu.SemaphoreType.DMA((3 * n,))] * 4,
        input_output_aliases={3 + n + w: 5 + w for w in range(n)},
        compiler_params=_cp("arbitrary"),
    )(u, z_own, w_in, *shards, *bufs)
    return res[:5], res[5:]


def _band_bias(grp, d):
    row = lax.broadcasted_iota(jnp.int32, (BAND, 2 * BAND), 0)
    col = lax.broadcasted_iota(jnp.int32, (BAND, 2 * BAND), 1)
    steps = BAND + row - col
    valid = (steps >= 0) & (steps <= BAND)
    stepsf = (steps * d).astype(F32)
    biases = []
    for hh in range(4):
        slope = 2.0 ** (-ALIBI_MAX_BIAS * (grp * 4 + hh + 1) / N_HEADS)
        biases.append(jnp.where(valid, -slope * stepsf, NEG))
    return biases, col


def _attn_tiles(grp, L):
    rr, rb = ATT_TILE[grp]
    rb = min(rb, L // BAND)
    return rr, rb, L // (rb * BAND)


def _kv_tile(cur_ref, prev_ref, rr, rb, cs):
    if rb == 0:
        return jnp.concatenate([prev_ref[rr, :, cs], cur_ref[rr, 0:BAND, cs]], axis=0)
    return cur_ref[rr, (rb - 1) * BAND:(rb + 1) * BAND, cs]


def _attn_fwd(qkv, grp):
    d = DILATIONS[grp]
    L = qkv.shape[2]
    RR, RB, nb = _attn_tiles(grp, L)

    def body(q_ref, kc_ref, kp_ref, vc_ref, vp_ref, o_ref, lse_ref):
        i = pl.program_id(0)
        biases, col = _band_bias(grp, d)
        first_keys_ok = (col >= BAND) | (i > 0)
        is_a = lax.broadcasted_iota(jnp.int32, (BAND, PAIR_W), 1) < HEAD_W
        heads = [(rr, rb, cp, h2) for rr in range(RR) for rb in range(RB) for cp in range(2) for h2 in range(2)]

        def tile(head):
            rr, rb, cp, _ = head
            return rr, rb, slice(rb * BAND, (rb + 1) * BAND), slice(cp * PAIR_W, (cp + 1) * PAIR_W)

        def scores(head):
            rr, rb, rows, cs = tile(head)
            q2 = q_ref[rr, rows, cs]
            b = biases[head[2] * 2 + head[3]]
            if rb == 0:
                b = jnp.where(first_keys_ok, b, NEG)
            sel = is_a if head[3] == 0 else jnp.logical_not(is_a)
            return _dot_nt(jnp.where(sel, q2, jnp.zeros_like(q2)), _kv_tile(kc_ref, kp_ref, rr, rb, cs)) + b

        s_next = scores(heads[0])
        outs, lses = {}, {}
        for idx, head in enumerate(heads):
            s = s_next
            if idx + 1 < len(heads):
                s_next = scores(heads[idx + 1])
            rr, rb, rows, cs = tile(head)
            m = jnp.max(s, axis=-1, keepdims=True)
            p = jnp.exp(s - m)
            l = jnp.sum(p, axis=-1, keepdims=True)
            outs[head[3]] = _dot(p.astype(BF16), _kv_tile(vc_ref, vp_ref, rr, rb, cs)) * (1.0 / l)
            lses[head[2] * 2 + head[3]] = m + jnp.log(l)
            if head[3] == 1:
                o_ref[rr, rows, cs] = jnp.where(is_a, outs[0], outs[1]).astype(BF16)
            if head[2] == 1 and head[3] == 1:
                lse_ref[rr, rows, :] = _per_head_lanes(lses)

    cur = lambda w: pl.BlockSpec((None, RR, RB * BAND, GROUP_W), lambda i, j: (w, j, i, 0))
    prev = lambda w: pl.BlockSpec((None, RR, BAND, GROUP_W), lambda i, j: (w, j, jnp.maximum(i * RB - 1, 0), 0))
    return pl.pallas_call(
        body, grid=(nb, d // RR), name=f"attn_fwd_g{grp}",
        in_specs=[cur(0), cur(1), prev(1), cur(2), prev(2)],
        out_specs=[pl.BlockSpec((RR, RB * BAND, GROUP_W), lambda i, j: (j, i, 0)),
                   pl.BlockSpec((RR, RB * BAND, STAT_W), lambda i, j: (j, i, 0))],
        out_shape=[SDS((d, L, GROUP_W), BF16), SDS((d, L, STAT_W), F32)],
        compiler_params=_cp("parallel", "parallel"),
    )(qkv, qkv, qkv, qkv, qkv)


def _pool_column_select(col, vals):
    return jnp.where(col < POOL_GROUP_W, vals[0],
                     jnp.where(col < 2 * POOL_GROUP_W, vals[1],
                               jnp.where(col < 3 * POOL_GROUP_W, vals[2], vals[3])))


def _pool_inv_count(i, rows):
    t = i * rows + lax.broadcasted_iota(jnp.int32, (rows, POOL_W), 0)
    col = lax.broadcasted_iota(jnp.int32, (rows, POOL_W), 1)
    win = _pool_column_select(col, POOL_WINDOWS)
    return 1.0 / jnp.minimum(t + 1, win).astype(F32), col


def _mixer_out(outs, lses, pz, gates, x, w_ao, w_po, wbd, scale, w_out, g_mlp, expand, shards, bufs):
    S = x.shape[0]
    n_tiles = S // TMB
    n = len(shards)

    def body(*refs):
        (o0_ref, l0_ref, o1_ref, l1_ref, o2_ref, l2_ref, pz_ref, halo_ref, gate_ref, x_ref,
         wao_ref, wpo_ref, wbd_ref, sc_ref, wout_ref, g_ref, expand_ref) = refs[0:17]
        shard_refs = refs[17:17 + n]
        (a_ref, lt0_ref, lt1_ref, lt2_ref, pooled_ref, mixed_ref, p_ref, merged_ref, h1_ref,
         m_ref) = refs[17 + 2 * n:27 + 2 * n]
        buf_refs = refs[27 + 2 * n:27 + 3 * n]
        so1, sl1, so2, sl2, slt, ext_ref = refs[27 + 3 * n:33 + 3 * n]
        i = pl.program_id(0)
        begin, end = _hosted_allgather(i, n_tiles, shard_refs, buf_refs, [sh.shape[0] for sh in shards],
                                       refs[33 + 3 * n:])
        begin()
        _interleave_load(o1_ref, (), so1, DILATIONS[1], TMB)
        _interleave_load(o2_ref, (), so2, DILATIONS[2], TMB)
        for ref, sref, d in ((l1_ref, sl1, DILATIONS[1]), (l2_ref, sl2, DILATIONS[2])):
            for r in range(d):
                sref[0, pl.ds(r, TMB // d, stride=d), :] = ref[r]
        l0, l1, l2 = l0_ref[0], sl1[0], sl2[0]
        mx = jnp.maximum(jnp.maximum(l0, l1), l2)
        e0, e1, e2 = jnp.exp(l0 - mx), jnp.exp(l1 - mx), jnp.exp(l2 - mx)
        den = e0 + e1 + e2
        inv = 1.0 / den
        slt[0] = mx + jnp.log(den)
        w0, w1, w2 = [_dot_split(e * inv, expand_ref[...]) for e in (e0, e1, e2)]
        for h in range(2):
            hs = slice(h * PAIR_W, (h + 1) * PAIR_W)
            a_ref[:, hs] = (w0[:, hs] * o0_ref[0, :, hs].astype(F32) + w1[:, hs] * so1[h]
                            + w2[:, hs] * so2[h]).astype(BF16)
        lt0_ref[0] = slt[0]
        for ref, d in ((lt1_ref, DILATIONS[1]), (lt2_ref, DILATIONS[2])):
            for r in range(d):
                ref[r] = slt[0, pl.ds(r, TMB // d, stride=d), :]

        pz_t = pz_ref[...].astype(F32)
        ext_ref[0:POOL_HALO, :] = jnp.where(i > 0, halo_ref[...].astype(F32), 0.0)
        ext_ref[POOL_HALO:, :] = pz_t
        sums = []
        acc = ext_ref[...]
        for k in (1, 2, 4, 8):
            acc = acc + pltpu.roll(acc, k, 0)
            sums.append(acc[POOL_HALO:, :])
        inv_cnt, col = _pool_inv_count(i, TMB)
        pooled = (_pool_column_select(col, sums) * inv_cnt - pz_t).astype(BF16)
        pooled_ref[...] = pooled
        mixed = _dot(pooled, wbd_ref[...])
        mixed_ref[...] = mixed.astype(BF16)
        p = (mixed * sc_ref[...]).astype(BF16)
        p_ref[...] = p

        a = a_ref[...]
        for j in range(N_CHIPS):
            js = slice(j * CHUNK, (j + 1) * CHUNK)
            ga = gate_ref[:, js]
            gp = gate_ref[:, D_MODEL + j * CHUNK:D_MODEL + (j + 1) * CHUNK]
            mj = _sigmoid(ga) * _dot(a, wao_ref[j]) + _sigmoid(gp) * _dot(p, wpo_ref[j])
            merged_ref[:, js] = mj.astype(BF16)
        h1 = x_ref[...] + _dot(merged_ref[...], wout_ref[...])
        h1_ref[...] = h1
        m_ref[...] = _rms_fwd(h1, g_ref[...])[0].astype(BF16)
        end()

    row = lambda w: pl.BlockSpec((TMB, w), lambda i: (i, 0))
    grp_spec = lambda d: pl.BlockSpec((d, TMB // d, GROUP_W), lambda i: (0, i, 0))
    stat_spec = lambda d: pl.BlockSpec((d, TMB // d, STAT_W), lambda i: (0, i, 0))
    halo = pl.BlockSpec((POOL_HALO, POOL_W), lambda i: (jnp.maximum(i * (TMB // POOL_HALO) - 1, 0), 0))
    d0, d1, d2 = DILATIONS
    pair_scratch = pltpu.VMEM((2, TMB, PAIR_W), F32)
    stat_scratch = pltpu.VMEM((1, TMB, STAT_W), F32)
    res = pl.pallas_call(
        body, grid=(n_tiles,), name="mixer_out",
        in_specs=[grp_spec(d0), stat_spec(d0), grp_spec(d1), stat_spec(d1), grp_spec(d2), stat_spec(d2),
                  row(POOL_W), halo, row(2 * D_MODEL), row(D_MODEL),
                  _resident(w_ao.shape), _resident(w_po.shape), _resident(wbd.shape), _resident(scale.shape),
                  _resident(w_out.shape), _resident(g_mlp.shape), _resident(expand.shape)] + [ANY] * (2 * n),
        out_specs=[row(GROUP_W), stat_spec(d0), stat_spec(d1), stat_spec(d2),
                   row(POOL_W), row(POOL_W), row(POOL_W), row(D_MODEL), row(D_MODEL), row(D_MODEL)] + [ANY] * n,
        out_shape=[SDS((S, GROUP_W), BF16)] + [SDS((d, S // d, STAT_W), F32) for d in DILATIONS]
        + [SDS((S, POOL_W), BF16), SDS((S, POOL_W), BF16), SDS((S, POOL_W), BF16),
           SDS((S, D_MODEL), BF16), SDS((S, D_MODEL), F32), SDS((S, D_MODEL), BF16)]
        + [SDS(b.shape, b.dtype) for b in bufs],
        scratch_shapes=[pair_scratch, stat_scratch, pair_scratch, stat_scratch, stat_scratch,
                        pltpu.VMEM((TMB + POOL_HALO, POOL_W), F32)] + [pltpu.SemaphoreType.DMA((3 * n,))] * 4,
        input_output_aliases={17 + n + w: 10 + w for w in range(n)},
        compiler_params=_cp("arbitrary"),
    )(outs[0], lses[0], outs[1], lses[1], outs[2], lses[2], pz, pz, gates, x,
      w_ao, w_po, wbd, scale, w_out, g_mlp, expand, *shards, *bufs)
    return res[:10], res[10:]


def _mlp_fwd_loss(m, h1, target, w_mi, w_mo, g_f):
    S = m.shape[0]

    def body(m_ref, h1_ref, t_ref, wmi_ref, wmo_ref, g_ref, hid_ref, dh2_ref, dh2b_ref, loss_ref, dg_ref):
        @pl.when(pl.program_id(0) == 0)
        def _():
            loss_ref[...] = jnp.zeros_like(loss_ref)
            dg_ref[...] = jnp.zeros_like(dg_ref)

        mt = m_ref[...]
        acc = h1_ref[...]
        for c in range(N_CHIPS):
            hid = jnp.square(jnp.maximum(_dot(mt, wmi_ref[c]), 0.0)).astype(BF16)
            hid_ref[:, c * D_MODEL:(c + 1) * D_MODEL] = hid
            acc = acc + _dot(hid, wmo_ref[c])
        g = g_ref[...]
        y, hh, r = _rms_fwd(acc, g)
        e = y - t_ref[...]
        loss_ref[...] += jnp.sum(e * e, axis=0, keepdims=True)
        dy = e * (1.0 / D_MODEL)
        dg_ref[...] += jnp.sum(dy * hh, axis=0, keepdims=True)
        dh2 = _rms_bwd(dy, hh, r, g)
        dh2_ref[...] = dh2
        dh2b_ref[...] = dh2.astype(BF16)

    row = lambda w: pl.BlockSpec((TM, w), lambda i: (i, 0))
    vec = pl.BlockSpec((1, D_MODEL), lambda i: (0, 0))
    return pl.pallas_call(
        body, grid=(S // TM,), name="mlp_fwd_loss",
        in_specs=[row(D_MODEL), row(D_MODEL), row(D_MODEL), _resident(w_mi.shape), _resident(w_mo.shape),
                  _resident(g_f.shape)],
        out_specs=[row(D_FF), row(D_MODEL), row(D_MODEL), vec, vec],
        out_shape=[SDS((S, D_FF), BF16), SDS((S, D_MODEL), F32), SDS((S, D_MODEL), BF16),
                   SDS((1, D_MODEL), F32), SDS((1, D_MODEL), F32)],
        compiler_params=_cp("arbitrary"),
    )(m, h1, target, w_mi, w_mo, g_f)


def _mlp_bwd(dh2, dh2b, hid, h1, w_mi, w_mo, g_mlp):
    S = dh2.shape[0]

    def body(dh2_ref, dh2b_ref, hid_ref, h1_ref, wmi_ref, wmo_ref, g_ref, dpre_ref, dh1_ref, dh1b_ref, dg_ref):
        @pl.when(pl.program_id(0) == 0)
        def _():
            dg_ref[...] = jnp.zeros_like(dg_ref)

        d2 = dh2b_ref[...]
        dm = jnp.zeros((TM, D_MODEL), F32)
        dhid_next = _dot_nt(d2, wmo_ref[0])
        for c in range(N_CHIPS):
            cs = slice(c * D_MODEL, (c + 1) * D_MODEL)
            dhid = dhid_next
            if c + 1 < N_CHIPS:
                dhid_next = _dot_nt(d2, wmo_ref[c + 1])
            dpre = (dhid * (2.0 * jnp.sqrt(hid_ref[:, cs].astype(F32)))).astype(BF16)
            dpre_ref[:, cs] = dpre
            dm = dm + _dot_nt(dpre, wmi_ref[c])
        g = g_ref[...]
        _, hh, r = _rms_fwd(h1_ref[...], g)
        dg_ref[...] += jnp.sum(dm * hh, axis=0, keepdims=True)
        dh1 = dh2_ref[...] + _rms_bwd(dm, hh, r, g)
        dh1_ref[...] = dh1
        dh1b_ref[...] = dh1.astype(BF16)

    row = lambda w: pl.BlockSpec((TM, w), lambda i: (i, 0))
    return pl.pallas_call(
        body, grid=(S // TM,), name="mlp_bwd",
        in_specs=[row(D_MODEL), row(D_MODEL), row(D_FF), row(D_MODEL), _resident(w_mi.shape),
                  _resident(w_mo.shape), _resident(g_mlp.shape)],
        out_specs=[row(D_FF), row(D_MODEL), row(D_MODEL), pl.BlockSpec((1, D_MODEL), lambda i: (0, 0))],
        out_shape=[SDS((S, D_FF), BF16), SDS((S, D_MODEL), F32), SDS((S, D_MODEL), BF16), SDS((1, D_MODEL), F32)],
        compiler_params=_cp("arbitrary"),
    )(dh2, dh2b, hid, h1, w_mi, w_mo, g_mlp)


def _mixer_bwd(dh1b, a, p, mixed, gates, w_out, w_ao, w_po, wbd, scale, head_ones, sums):
    S = a.shape[0]
    n_tiles = S // TMB
    n = len(sums)

    def body(*refs):
        (dh1b_ref, a_ref, p_ref, mixed_ref, gate_ref, wout_ref, wao_ref, wpo_ref, wbd_ref, sc_ref,
         ones_ref) = refs[0:11]
        sum_refs = refs[11:11 + n]
        (da1_ref, dp1_ref, dgate_ref, da0_ref, dag1_ref, dag2_ref, dd0_ref, dd1_ref, dd2_ref,
         dmixed_ref, dqp_ref, dscale_ref) = refs[11 + n:23 + n]
        land_refs = refs[23 + n:23 + 2 * n]
        s_da, s_dd, send_sem, recv_sem = refs[23 + 2 * n:]
        i = pl.program_id(0)

        @pl.when(i == 0)
        def _():
            dscale_ref[...] = jnp.zeros_like(dscale_ref)
            for cpy in _chip_sum_copies(sum_refs, land_refs, send_sem, recv_sem):
                cpy.start()

        dmerged = _dot_nt(dh1b_ref[...], wout_ref[...])
        a = a_ref[...]
        p = p_ref[...]
        da = jnp.zeros((TMB, GROUP_W), F32)
        dp = jnp.zeros((TMB, POOL_W), F32)
        for j in range(N_CHIPS):
            js = slice(j * CHUNK, (j + 1) * CHUNK)
            sa = _sigmoid(gate_ref[:, js])
            sp = _sigmoid(gate_ref[:, D_MODEL + j * CHUNK:D_MODEL + (j + 1) * CHUNK])
            dmj = dmerged[:, js]
            da1 = (dmj * sa).astype(BF16)
            dp1 = (dmj * sp).astype(BF16)
            da1_ref[:, js] = da1
            dp1_ref[:, js] = dp1
            dgate_ref[j] = (dmj * _dot(a, wao_ref[j]) * sa * (1.0 - sa)).astype(BF16)
            dgate_ref[N_CHIPS + j] = (dmj * _dot(p, wpo_ref[j]) * sp * (1.0 - sp)).astype(BF16)
            da = da + _dot_nt(da1, wao_ref[j])
            dp = dp + _dot_nt(dp1, wpo_ref[j])

        dd = _dot_split(da * a.astype(F32), ones_ref[...])
        da0_ref[0] = da.astype(BF16)
        dd0_ref[0] = dd
        for h in range(2):
            s_da[h] = da[:, h * PAIR_W:(h + 1) * PAIR_W]
        s_dd[0] = dd
        for refs, d in (((dag1_ref, dd1_ref), DILATIONS[1]), ((dag2_ref, dd2_ref), DILATIONS[2])):
            for r in range(d):
                for h in range(2):
                    hs = slice(h * PAIR_W, (h + 1) * PAIR_W)
                    refs[0][r, :, hs] = s_da[h, pl.ds(r, TMB // d, stride=d), :].astype(BF16)
                refs[1][r] = s_dd[0, pl.ds(r, TMB // d, stride=d), :]

        sc = sc_ref[...]
        dscale_ref[...] += jnp.sum(dp * mixed_ref[...].astype(F32), axis=0, keepdims=True)
        dmixed = (dp * sc).astype(BF16)
        dmixed_ref[...] = dmixed
        inv_cnt, _ = _pool_inv_count(i, TMB)
        dqp_ref[...] = (_dot_nt(dmixed, wbd_ref[...]) * inv_cnt).astype(BF16)

        @pl.when(i == n_tiles - 1)
        def _():
            for cpy in _chip_sum_copies(sum_refs, land_refs, send_sem, recv_sem):
                cpy.wait()

    row = lambda w: pl.BlockSpec((TMB, w), lambda i: (i, 0))
    grp_spec = lambda d: pl.BlockSpec((d, TMB // d, GROUP_W), lambda i: (0, i, 0))
    stat_spec = lambda d: pl.BlockSpec((d, TMB // d, STAT_W), lambda i: (0, i, 0))
    d0, d1, d2 = DILATIONS
    res = pl.pallas_call(
        body, grid=(n_tiles,), name="mixer_bwd",
        in_specs=[row(D_MODEL), row(GROUP_W), row(POOL_W), row(POOL_W), row(2 * D_MODEL),
                  _resident(w_out.shape), _resident(w_ao.shape), _resident(w_po.shape), _resident(wbd.shape),
                  _resident(scale.shape), _resident(head_ones.shape)] + [ANY] * n,
        out_specs=[row(D_MODEL), row(D_MODEL), pl.BlockSpec((2 * N_CHIPS, TMB, CHUNK), lambda i: (0, i, 0)),
                   grp_spec(d0), grp_spec(d1), grp_spec(d2), stat_spec(d0), stat_spec(d1), stat_spec(d2),
                   row(POOL_W), row(POOL_W), pl.BlockSpec((1, POOL_W), lambda i: (0, 0))] + [ANY] * n,
        out_shape=[SDS((S, D_MODEL), BF16), SDS((S, D_MODEL), BF16), SDS((2 * N_CHIPS, S, CHUNK), BF16)]
        + [SDS((d, S // d, GROUP_W), BF16) for d in DILATIONS]
        + [SDS((d, S // d, STAT_W), F32) for d in DILATIONS]
        + [SDS((S, POOL_W), BF16), SDS((S, POOL_W), BF16), SDS((1, POOL_W), F32)]
        + [SDS(t.shape, t.dtype) for t in sums],
        scratch_shapes=[pltpu.VMEM((2, TMB, PAIR_W), F32), pltpu.VMEM((1, TMB, STAT_W), F32),
                        pltpu.SemaphoreType.DMA((3 * n,)), pltpu.SemaphoreType.DMA((3 * n,))],
        compiler_params=_cp("arbitrary"),
    )(dh1b, a, p, mixed, gates, w_out, w_ao, w_po, wbd, scale, head_ones, *sums)
    return res[:12], res[12:]


def _attn_bwd(qkv, da, lt, dd, grp, packs=None):
    d = DILATIONS[grp]
    L = qkv.shape[2]
    RR, RB, nb = _attn_tiles(grp, L)
    n_j = d // RR
    hosted = packs is not None

    def body(*refs):
        q_ref, kc_ref, kp_ref, vc_ref, vp_ref, da_ref, lt_ref, dd_ref = refs[0:8]
        dq_ref, dk_ref, dv_ref = refs[8 + hosted:11 + hosted]
        dk_acc, dv_acc = refs[11 + 2 * hosted:13 + 2 * hosted]
        i = pl.program_id(1)
        if hosted:
            j = pl.program_id(0)
            start, relay, finish = _pack_allgather(refs[11 + hosted], *refs[13 + 2 * hosted:])
            pl.when((j == 0) & (i == 0))(start)
            pl.when((j == 0) & (i == nb // 2))(relay)

        @pl.when(i == 0)
        def _():
            dk_acc[...] = jnp.zeros_like(dk_acc)
            dv_acc[...] = jnp.zeros_like(dv_acc)

        def compute(cur, prv):
            dk_acc[cur] = jnp.zeros((RR, RB * BAND, GROUP_W), F32)
            dv_acc[cur] = jnp.zeros((RR, RB * BAND, GROUP_W), F32)
            biases, col = _band_bias(grp, d)
            first_keys_ok = (col >= BAND) | (i > 0)
            is_a = lax.broadcasted_iota(jnp.int32, (BAND, PAIR_W), 1) < HEAD_W
            for rr in range(RR):
                for rb in range(RB):
                    rows = slice(rb * BAND, (rb + 1) * BAND)
                    for cp in range(2):
                        cs = slice(cp * PAIR_W, (cp + 1) * PAIR_W)
                        q2 = q_ref[rr, rows, cs]
                        da2 = da_ref[rr, rows, cs]
                        lt2 = lt_ref[rr, rows, :]
                        dd2 = dd_ref[rr, rows, :]
                        kcat = _kv_tile(kc_ref, kp_ref, rr, rb, cs)
                        vcat = _kv_tile(vc_ref, vp_ref, rr, rb, cs)
                        q2t = q2.astype(F32).T.astype(BF16)
                        da2t = da2.astype(F32).T.astype(BF16)
                        dqs, dkts, dvts, scores, dpvs = [], [], [], [], []
                        for h2 in range(2):
                            sel = is_a if h2 == 0 else jnp.logical_not(is_a)
                            b = biases[cp * 2 + h2]
                            if rb == 0:
                                b = jnp.where(first_keys_ok, b, NEG)
                            scores.append(_dot_nt(jnp.where(sel, q2, jnp.zeros_like(q2)), kcat) + b)
                            dpvs.append(_dot_nt(jnp.where(sel, da2, jnp.zeros_like(da2)), vcat))
                        for h2 in range(2):
                            lane0 = h2 * HEAD_W
                            p = jnp.exp(scores[h2] - _head_col(lt2, cp * 2 + h2))
                            ds = (p * (dpvs[h2] - _head_col(dd2, cp * 2 + h2))).astype(BF16)
                            dqs.append(_dot(ds, kcat))
                            dkts.append(_dot(q2t[lane0:lane0 + HEAD_W, :], ds))
                            dvts.append(_dot(da2t[lane0:lane0 + HEAD_W, :], p.astype(BF16)))
                        dq_ref[rr, rows, cs] = (jnp.where(is_a, dqs[0], dqs[1]) * 0.125).astype(BF16)
                        dkc = jnp.concatenate(dkts, axis=0).T
                        dvc = jnp.concatenate(dvts, axis=0).T
                        if rb == 0:
                            last = slice((RB - 1) * BAND, RB * BAND)
                            dk_acc[prv, rr, last, cs] += dkc[0:BAND]
                            dv_acc[prv, rr, last, cs] += dvc[0:BAND]
                            dk_acc[cur, rr, 0:BAND, cs] += dkc[BAND:]
                            dv_acc[cur, rr, 0:BAND, cs] += dvc[BAND:]
                        else:
                            both = slice((rb - 1) * BAND, (rb + 1) * BAND)
                            dk_acc[cur, rr, both, cs] += dkc
                            dv_acc[cur, rr, both, cs] += dvc

        def flush(prv):
            dk_ref[...] = dk_acc[prv].astype(BF16)
            dv_ref[...] = dv_acc[prv].astype(BF16)

        for parity in (0, 1):
            on = (i % 2) == parity
            pl.when(on & (i < nb))(functools.partial(compute, parity, 1 - parity))
            pl.when(on & (i > 0))(functools.partial(flush, 1 - parity))
        if hosted:
            pl.when((j == n_j - 1) & (i == nb))(finish)

    qi = lambda i: jnp.minimum(i, nb - 1)
    cur_w = lambda w: pl.BlockSpec((None, RR, RB * BAND, GROUP_W), lambda j, i: (w, j, qi(i), 0))
    prev_w = lambda w: pl.BlockSpec((None, RR, BAND, GROUP_W),
                                    lambda j, i: (w, j, jnp.maximum(qi(i) * RB - 1, 0), 0))
    blk = pl.BlockSpec((RR, RB * BAND, GROUP_W), lambda j, i: (j, qi(i), 0))
    stat_blk = pl.BlockSpec((RR, RB * BAND, STAT_W), lambda j, i: (j, qi(i), 0))
    late = pl.BlockSpec((RR, RB * BAND, GROUP_W), lambda j, i: (j, jnp.maximum(i - 1, 0), 0))
    extra = [packs] if hosted else []
    return pl.pallas_call(
        body, grid=(n_j, nb + 1), name=f"attn_bwd_g{grp}",
        in_specs=[cur_w(0), cur_w(1), prev_w(1), cur_w(2), prev_w(2), blk, stat_blk, stat_blk] + [ANY] * hosted,
        out_specs=[blk, late, late] + [ANY] * hosted,
        out_shape=[SDS((d, L, GROUP_W), BF16)] * 3 + [SDS(t.shape, t.dtype) for t in extra],
        scratch_shapes=[pltpu.VMEM((2, RR, RB * BAND, GROUP_W), F32), pltpu.VMEM((2, RR, RB * BAND, GROUP_W), F32)]
        + [pltpu.SemaphoreType.DMA((N_DEV - 1,))] * (2 * hosted),
        input_output_aliases={8: 3} if hosted else {},
        compiler_params=_cp("arbitrary" if hosted else "parallel", "arbitrary"),
    )(qkv, qkv, qkv, qkv, qkv, da, lt, dd, *extra)


def _dz_assemble(dqkv, dqp):
    S = dqp.shape[0]
    n_tiles = S // TMB

    def body(*refs):
        dqkv_refs = refs[0:9]
        dqp_ref, halo_ref = refs[9:11]
        dz_ref, s_ref, ext_ref = refs[11:]
        i = pl.program_id(0)

        for grp in range(3):
            for which in range(3):
                n = which * 3 + grp
                ref = dqkv_refs[grp * 3 + which]
                if DILATIONS[grp] == 1:
                    dz_ref[n] = ref[0]
                else:
                    _interleave_load(ref, (), s_ref, DILATIONS[grp], TMB)
                    for h in range(2):
                        dz_ref[n, :, h * PAIR_W:(h + 1) * PAIR_W] = s_ref[h].astype(BF16)

        dqp = dqp_ref[...].astype(F32)
        ext_ref[0:TMB, :] = dqp
        ext_ref[TMB:, :] = jnp.where(i < n_tiles - 1, halo_ref[...].astype(F32), 0.0)
        sums = []
        acc = ext_ref[...]
        for k in (1, 2, 4, 8):
            acc = acc + pltpu.roll(acc, TMB + POOL_HALO - k, 0)
            sums.append(acc[0:TMB, :])
        inv_cnt, col = _pool_inv_count(i, TMB)
        dpz = _pool_column_select(col, sums) - dqp / inv_cnt
        for t in range(3):
            dz_ref[9 + t] = dpz[:, t * CHUNK:(t + 1) * CHUNK].astype(BF16)

    row = lambda w: pl.BlockSpec((TMB, w), lambda i: (i, 0))
    grp_spec = lambda d: pl.BlockSpec((d, TMB // d, GROUP_W), lambda i: (0, i, 0))
    halo = pl.BlockSpec((POOL_HALO, POOL_W),
                        lambda i: (jnp.minimum((i + 1) * (TMB // POOL_HALO), S // POOL_HALO - 1), 0))
    flat = [t for grp in range(3) for t in dqkv[grp]]
    return pl.pallas_call(
        body, grid=(n_tiles,), name="dz_assemble",
        in_specs=[grp_spec(DILATIONS[grp]) for grp in range(3) for _ in range(3)] + [row(POOL_W), halo],
        out_specs=pl.BlockSpec((N_DZ_CHUNKS, TMB, CHUNK), lambda i: (0, i, 0)),
        out_shape=SDS((N_DZ_CHUNKS, S, CHUNK), BF16),
        scratch_shapes=[pltpu.VMEM((2, TMB, PAIR_W), F32), pltpu.VMEM((TMB + POOL_HALO, POOL_W), F32)],
        compiler_params=_cp("parallel"),
    )(*flat, dqp, dqp)


def _inproj_dx(dz, dgates, dh1, x, g, w_in, sums):
    S = x.shape[0]
    n_tiles = S // TM
    n = len(sums)

    def body(*refs):
        dz_ref, dgate_ref, dh1_ref, x_ref, g_ref, w_ref = refs[0:6]
        sum_refs = refs[6:6 + n]
        dx_ref, dg_ref = refs[6 + n:8 + n]
        land_refs = refs[8 + n:8 + 2 * n]
        sems = refs[8 + 2 * n:]
        i = pl.program_id(0)

        def copies():
            return _chip_sum_copies(sum_refs, land_refs, *sems)

        @pl.when(i == 0)
        def _():
            dg_ref[...] = jnp.zeros_like(dg_ref)
            for cpy in copies():
                cpy.start()

        du = jnp.zeros((TM, D_MODEL), F32)
        for k in range(N_CHUNKS):
            dzk = dz_ref[k] if k < N_DZ_CHUNKS else dgate_ref[k - N_DZ_CHUNKS]
            du = du + _dot_nt(dzk, _w_in_chunk(w_ref, k))
        gv = g_ref[...]
        _, xh, r = _rms_fwd(x_ref[...], gv)
        dg_ref[...] += jnp.sum(du * xh, axis=0, keepdims=True)
        dx_ref[...] = dh1_ref[...] + _rms_bwd(du, xh, r, gv)

        @pl.when(i == n_tiles - 1)
        def _():
            for cpy in copies():
                cpy.wait()

    row = lambda w: pl.BlockSpec((TM, w), lambda i: (i, 0))
    res = pl.pallas_call(
        body, grid=(n_tiles,), name="inproj_dx",
        in_specs=[pl.BlockSpec((N_DZ_CHUNKS, TM, CHUNK), lambda i: (0, i, 0)),
                  pl.BlockSpec((N_CHUNKS - N_DZ_CHUNKS, TM, CHUNK), lambda i: (0, i, 0)),
                  row(D_MODEL), row(D_MODEL), _resident(g.shape), _resident(w_in.shape)] + [ANY] * n,
        out_specs=[row(D_MODEL), pl.BlockSpec((1, D_MODEL), lambda i: (0, 0))] + [ANY] * n,
        out_shape=[SDS((S, D_MODEL), F32), SDS((1, D_MODEL), F32)] + [SDS(t.shape, t.dtype) for t in sums],
        scratch_shapes=[pltpu.SemaphoreType.DMA((3 * n,)), pltpu.SemaphoreType.DMA((3 * n,))],
        compiler_params=_cp("arbitrary"),
    )(dz, dgates, dh1, x, g, w_in, *sums)
    return res[0], res[1], res[2:]


def _wgrad(a, b, name, *, out_shape, a_spec, b_spec, out_spec, grid, n_out_cols=None, fill=None, narrow=True):
    k_axis = len(grid) - 1
    n_k = grid[k_axis]
    n_out = 2 if narrow else 1

    def body(a_ref, b_ref, *rest):
        o_ref = rest[-n_out]

        @pl.when(pl.program_id(k_axis) == 0)
        def _():
            o_ref[...] = jnp.zeros_like(o_ref)

        at = a_ref[...]
        if n_out_cols is None:
            o_ref[...] += _dot_tn(at, b_ref[...])
        elif n_out_cols[0] == "lead_both":
            for t in range(b_ref.shape[0]):
                o_ref[t] += _dot_tn(at, b_ref[t])
        else:
            w = n_out_cols[1]
            for t in range(o_ref.shape[0]):
                o_ref[t] += _dot_tn(at, b_ref[:, t * w:(t + 1) * w])

        if narrow:
            @pl.when(pl.program_id(k_axis) == n_k - 1)
            def _():
                rest[-1][...] = o_ref[...].astype(BF16)

    sem = ("parallel",) * k_axis + ("arbitrary",)
    extra = [] if fill is None else list(fill) if narrow else [fill]
    shapes = [out_shape, SDS(out_shape.shape, BF16)] if narrow else out_shape
    return pl.pallas_call(body, grid=grid, name=name, in_specs=[a_spec, b_spec] + [ANY] * len(extra),
                          out_specs=[out_spec] * n_out if narrow else out_spec, out_shape=shapes,
                          input_output_aliases={2 + t: t for t in range(len(extra))},
                          compiler_params=_cp(*sem))(a, b, *extra)


def _wgrad_in(u, dz, dgates):
    bk = min(BK, u.shape[0])
    nk = u.shape[0] // bk
    g = WGRAD_IN_GROUP
    kw = dict(n_out_cols=("lead_both", CHUNK), a_spec=pl.BlockSpec((bk, D_MODEL), lambda j, k: (k, 0)),
              b_spec=pl.BlockSpec((g, bk, CHUNK), lambda j, k: (j, k, 0)),
              out_shape=SDS((N_CHUNKS, D_MODEL, CHUNK), F32))
    first = _wgrad(u, dz, "wgrad_in_qkvp", grid=(N_DZ_CHUNKS // g, nk),
                   out_spec=pl.BlockSpec((g, D_MODEL, CHUNK), lambda j, k: (j, 0, 0)), **kw)
    both = _wgrad(u, dgates, "wgrad_in_gates", grid=((N_CHUNKS - N_DZ_CHUNKS) // g, nk), fill=first,
                  out_spec=pl.BlockSpec((g, D_MODEL, CHUNK), lambda j, k: (N_DZ_CHUNKS // g + j, 0, 0)), **kw)
    return [t.reshape(N_CHIPS, CHUNKS_PER_SHARD * D_MODEL, CHUNK) for t in both]


def _wgrads_mixer(a, da1, p, dp1, merged, dh1b, pooled, dmixed):
    bk = min(BK, a.shape[0])
    nk = a.shape[0] // bk
    g_ao = _wgrad(
        a, da1, "wgrad_att_out", grid=(nk,), n_out_cols=("cols_b", CHUNK),
        a_spec=pl.BlockSpec((bk, GROUP_W), lambda k: (k, 0)),
        b_spec=pl.BlockSpec((bk, D_MODEL), lambda k: (k, 0)),
        out_spec=pl.BlockSpec((N_CHIPS, GROUP_W, CHUNK), lambda k: (0, 0, 0)),
        out_shape=SDS((N_CHIPS, GROUP_W, CHUNK), F32))
    g_po = _wgrad(
        p, dp1, "wgrad_pool_out", grid=(nk,), n_out_cols=("cols_b", CHUNK),
        a_spec=pl.BlockSpec((bk, POOL_W), lambda k: (k, 0)),
        b_spec=pl.BlockSpec((bk, D_MODEL), lambda k: (k, 0)),
        out_spec=pl.BlockSpec((N_CHIPS, POOL_W, CHUNK), lambda k: (0, 0, 0)),
        out_shape=SDS((N_CHIPS, POOL_W, CHUNK), F32))
    g_out = _wgrad(
        merged, dh1b, "wgrad_out", grid=(nk,),
        a_spec=pl.BlockSpec((bk, D_MODEL), lambda k: (k, 0)),
        b_spec=pl.BlockSpec((bk, D_MODEL), lambda k: (k, 0)),
        out_spec=pl.BlockSpec((D_MODEL, D_MODEL), lambda k: (0, 0)),
        out_shape=SDS((D_MODEL, D_MODEL), F32))
    g_bd = _wgrad(
        pooled, dmixed, "wgrad_pool_grp", grid=(nk,),
        a_spec=pl.BlockSpec((bk, POOL_W), lambda k: (k, 0)),
        b_spec=pl.BlockSpec((bk, POOL_W), lambda k: (k, 0)),
        out_spec=pl.BlockSpec((POOL_W, POOL_W), lambda k: (0, 0)),
        out_shape=SDS((POOL_W, POOL_W), F32), narrow=False)
    g_out = [t.reshape(N_CHIPS, D_MODEL // N_CHIPS, D_MODEL) for t in g_out]
    return [g_ao, g_po, g_out], g_bd


def _wgrads_mlp(m, dpre, hid, dh2b):
    bk = min(BK, m.shape[0])
    nk = m.shape[0] // bk
    g_mi = _wgrad(
        m, dpre, "wgrad_mlp_in", grid=(N_CHIPS, nk),
        a_spec=pl.BlockSpec((bk, D_MODEL), lambda c, k: (k, 0)),
        b_spec=pl.BlockSpec((bk, D_MODEL), lambda c, k: (k, c)),
        out_spec=pl.BlockSpec((None, D_MODEL, D_MODEL), lambda c, k: (c, 0, 0)),
        out_shape=SDS((N_CHIPS, D_MODEL, D_MODEL), F32))
    g_mo = _wgrad(
        hid, dh2b, "wgrad_mlp_out", grid=(N_CHIPS, nk),
        a_spec=pl.BlockSpec((bk, D_MODEL), lambda c, k: (k, c)),
        b_spec=pl.BlockSpec((bk, D_MODEL), lambda c, k: (k, 0)),
        out_spec=pl.BlockSpec((None, D_MODEL, D_MODEL), lambda c, k: (c, 0, 0)),
        out_shape=SDS((N_CHIPS, D_MODEL, D_MODEL), F32))
    return [g_mi, g_mo]


def _mesh_place():
    x, y, c = lax.axis_index("x"), lax.axis_index("y"), lax.axis_index("c")
    other_chips = [(x, 1 - y), (1 - x, y), (1 - x, 1 - y)]
    return x, y, c, other_chips


ANY = pl.BlockSpec(memory_space=pl.ANY)


def _weight_half_copies(shard_refs, buf_refs, rows, send_sem, recv_sem):
    x, y, c, chips = _mesh_place()
    me = 2 * x + y
    copies = []
    for w, r_full in enumerate(rows):
        rh = r_full // 2
        for r, (px, py) in enumerate(chips):
            k = w * 3 + r
            copies.append(pltpu.make_async_remote_copy(
                src_ref=shard_refs[w].at[pl.ds(c * rh, rh), :], dst_ref=buf_refs[w].at[me, pl.ds(c * rh, rh), :],
                send_sem=send_sem.at[k], recv_sem=recv_sem.at[k], device_id=(px, py, c), device_id_type=MESH))
    return copies


def _pair_forward_copies(buf_refs, rows, send_sem, recv_sem):
    x, y, c, chips = _mesh_place()
    out = []
    for w, r_full in enumerate(rows):
        rh = r_full // 2
        for r, (px, py) in enumerate(chips):
            k = w * 3 + r
            landed = buf_refs[w].at[2 * px + py, pl.ds(c * rh, rh), :]
            theirs = buf_refs[w].at[2 * px + py, pl.ds((1 - c) * rh, rh), :]
            mk = lambda ref: pltpu.make_async_remote_copy(
                src_ref=ref, dst_ref=ref, send_sem=send_sem.at[k], recv_sem=recv_sem.at[k],
                device_id=(x, y, 1 - c), device_id_type=MESH)
            out.append((mk(landed), mk(theirs)))
    return out


def _place_own(block, n_slots, slot):
    buf = lax.empty((n_slots,) + block.shape, block.dtype)
    return lax.dynamic_update_slice(buf, block[None], (slot,) + (0,) * block.ndim)


def _pair_forward(bufs, rows, name):
    n = len(bufs)

    def body(*refs):
        dst = refs[n:2 * n]
        send_sem, recv_sem = refs[2 * n:]
        fwds = _pair_forward_copies(dst, rows, send_sem, recv_sem)
        for fwd, _ in fwds:
            fwd.start()
        for fwd, landing in fwds:
            landing.wait_recv()
            fwd.wait_send()

    return pl.pallas_call(
        body, name=name,
        in_specs=[ANY] * n, out_specs=[ANY] * n,
        out_shape=[SDS(b.shape, b.dtype) for b in bufs],
        scratch_shapes=[pltpu.SemaphoreType.DMA((3 * n,))] * 2,
        input_output_aliases={w: w for w in range(n)},
    )(*bufs)


def _chip_sum_copies(src, dst, send_sem, recv_sem):
    x, y, c, chips = _mesh_place()
    copies = []
    for w in range(len(src)):
        for r, (px, py) in enumerate(chips):
            k = w * 3 + r
            copies.append(pltpu.make_async_remote_copy(
                src_ref=src[w].at[r + 1], dst_ref=dst[w].at[r + 1], send_sem=send_sem.at[k], recv_sem=recv_sem.at[k],
                device_id=(px, py, c), device_id_type=MESH))
    return copies


def _pair_exchange(grads):
    n = len(grads)

    def body(*refs):
        src, dst = refs[:n], refs[n:2 * n]
        send_sem, recv_sem = refs[2 * n:]
        x, y, c, _ = _mesh_place()
        copies = []
        for w in range(n):
            rh = grads[w].shape[1] // 2
            copies.append(pltpu.make_async_remote_copy(
                src_ref=src[w].at[:, pl.ds((1 - c) * rh, rh), :], dst_ref=dst[w],
                send_sem=send_sem.at[w], recv_sem=recv_sem.at[w],
                device_id=(x, y, 1 - c), device_id_type=MESH))
            copies[-1].start()
        for cpy in copies:
            cpy.wait()

    return pl.pallas_call(
        body, name="grad_pair_exchange",
        in_specs=[ANY] * n, out_specs=[ANY] * n,
        out_shape=[SDS((N_CHIPS, g.shape[1] // 2, g.shape[2]), g.dtype) for g in grads],
        scratch_shapes=[pltpu.SemaphoreType.DMA((n,)), pltpu.SemaphoreType.DMA((n,))],
    )(*grads)


def _pair_sum(place, grad, recv, name):
    _, R, C = grad.shape
    rh = R // 2
    br = _row_block(rh, max(256, ELEMENTWISE_BLOCK // C))
    nbh = rh // br

    def body(place_ref, g_ref, r_ref, own_ref, sums_ref):
        s = g_ref[...] + r_ref[...].astype(F32)

        @pl.when(pl.program_id(1) == 0)
        def _():
            own_ref[...] = s

        sums_ref[...] = s.astype(BF16)

    slot = lambda rel, pr: jnp.bitwise_xor(pr[0], rel)
    return pl.pallas_call(
        body, name=name,
        grid_spec=pltpu.PrefetchScalarGridSpec(
            num_scalar_prefetch=1, grid=(nbh, N_CHIPS),
            in_specs=[pl.BlockSpec((None, br, C), lambda i, rel, pr: (slot(rel, pr), pr[1] * nbh + i, 0)),
                      pl.BlockSpec((None, br, C), lambda i, rel, pr: (slot(rel, pr), i, 0))],
            out_specs=[pl.BlockSpec((br, C), lambda i, rel, pr: (i, 0)),
                       pl.BlockSpec((None, br, C), lambda i, rel, pr: (rel, i, 0))]),
        out_shape=[SDS((rh, C), F32), SDS((N_CHIPS, rh, C), BF16)],
        compiler_params=_cp("parallel", "arbitrary"),
    )(place, grad, recv)


def _chip_sum(place, own, recv, name):
    rh, C = own.shape
    br = _row_block(rh, max(256, ELEMENTWISE_BLOCK // C))
    nbh = rh // br

    def body(place_ref, own_ref, r_ref, o_ref):
        o_ref[...] = ((own_ref[...] + r_ref[1].astype(F32)) + r_ref[2].astype(F32)) + r_ref[3].astype(F32)

    return pl.pallas_call(
        body, name=name,
        grid_spec=pltpu.PrefetchScalarGridSpec(
            num_scalar_prefetch=1, grid=(nbh,),
            in_specs=[pl.BlockSpec((br, C), lambda i, pr: (i, 0)),
                      pl.BlockSpec((N_CHIPS, br, C), lambda i, pr: (0, i, 0))],
            out_specs=pl.BlockSpec((br, C), lambda i, pr: (pr[1] * nbh + i, 0))),
        out_shape=SDS((2 * rh, C), F32),
        compiler_params=_cp("parallel"),
    )(place, own, recv)


def _pack_allgather(all_ref, send_sem, recv_sem):
    x, y, c, chips = _mesh_place()
    sib = (x, y, 1 - c)

    def pack(dev, k, to):
        slot = 4 * dev[0] + 2 * dev[1] + dev[2]
        return pltpu.make_async_remote_copy(
            src_ref=all_ref.at[slot], dst_ref=all_ref.at[slot], send_sem=send_sem.at[k],
            recv_sem=recv_sem.at[k], device_id=to, device_id_type=MESH)

    first = [pack((x, y, c), 0, sib)] + [pack((x, y, c), 1 + r, (px, py, c)) for r, (px, py) in enumerate(chips)]
    relays = [pack((px, py, c), 4 + r, sib) for r, (px, py) in enumerate(chips)]

    def start():
        for cpy in first:
            cpy.start()

    def relay():
        for r, (px, py) in enumerate(chips):
            pack((px, py, c), 1 + r, (px, py, c)).wait_recv()
            relays[r].start()

    def finish():
        pack(sib, 0, sib).wait_recv()
        for r, (px, py) in enumerate(chips):
            pack((px, py, 1 - c), 4 + r, sib).wait_recv()
        for cpy in first + relays:
            cpy.wait_send()

    return start, relay, finish


def _finish_exchange(grads, late_all):
    n = len(grads)

    def body(*refs):
        dst, all_ref = refs[n + 1:2 * n + 1], refs[2 * n + 1]
        send_sem, recv_sem, ssend_sem, srecv_sem = refs[2 * n + 2:]
        x, y, c, _ = _mesh_place()
        start, relay, finish = _pack_allgather(all_ref, ssend_sem, srecv_sem)
        start()
        sends, landings = [], []
        for w in range(n):
            rh = grads[w].shape[0] // 2
            mk = lambda cc: pltpu.make_async_remote_copy(
                src_ref=dst[w].at[pl.ds(cc * rh, rh), :], dst_ref=dst[w].at[pl.ds(cc * rh, rh), :],
                send_sem=send_sem.at[w], recv_sem=recv_sem.at[w], device_id=(x, y, 1 - c), device_id_type=MESH)
            sends.append(mk(c))
            landings.append(mk(1 - c))
            sends[-1].start()
        relay()
        finish()
        for cpy in landings:
            cpy.wait_recv()
        for cpy in sends:
            cpy.wait_send()

    res = pl.pallas_call(
        body, name="grad_finish_exchange",
        in_specs=[ANY] * (n + 1), out_specs=[ANY] * (n + 1),
        out_shape=[SDS(g.shape, g.dtype) for g in grads] + [SDS(late_all.shape, late_all.dtype)],
        scratch_shapes=[pltpu.SemaphoreType.DMA((n,)), pltpu.SemaphoreType.DMA((n,)),
                        pltpu.SemaphoreType.DMA((N_DEV - 1,)), pltpu.SemaphoreType.DMA((N_DEV - 1,))],
        input_output_aliases={w: w for w in range(n + 1)},
    )(*grads, late_all)
    return res[:n], res[n]


def _adamw_math(w, g, m, v):
    m = ADAM_B1 * m + (1.0 - ADAM_B1) * g
    v = ADAM_B2 * v + (1.0 - ADAM_B2) * jnp.square(g)
    m_hat = m / (1.0 - ADAM_B1 ** ADAM_STEP)
    v_hat = v / (1.0 - ADAM_B2 ** ADAM_STEP)
    delta = -ADAM_LR * (m_hat / (jnp.sqrt(v_hat) + ADAM_EPS) + ADAM_WD * w)
    return delta, m, v


def _adamw(w, g, m, v, name):
    R, C = w.shape
    br = _row_block(R, 512)
    if g.ndim == 3:
        n_chunks, cw = g.shape[0], g.shape[2]
        g_spec = pl.BlockSpec((None, br, cw), lambda t, i: (t, i, 0))
    else:
        n_chunks, cw = 1, C
        g_spec = pl.BlockSpec((br, cw), lambda t, i: (i, t))

    def body(w_ref, g_ref, m_ref, v_ref, g_out_ref, d_ref, nm_ref, nv_ref):
        gv = g_ref[...]
        g_out_ref[...] = gv
        d_ref[...], nm_ref[...], nv_ref[...] = _adamw_math(w_ref[...], gv, m_ref[...], v_ref[...])

    spec = pl.BlockSpec((br, cw), lambda t, i: (i, t))
    return pl.pallas_call(
        body, grid=(n_chunks, R // br), name=name, in_specs=[spec, g_spec, spec, spec], out_specs=[spec] * 4,
        out_shape=[SDS((R, C), F32)] * 4, compiler_params=_cp("parallel", "parallel"),
    )(w, g, m, v)


def _small_sum_adamw(all_small, all_late, w, m, v):
    loss_row = PACK_ROWS - 8

    def body(all_ref, late_ref, w_ref, m_ref, v_ref, g_ref, d_ref, nm_ref, nv_ref, loss_ref):
        g = all_ref[0]
        late = late_ref[0]
        for k in range(1, N_DEV):
            g = g + all_ref[k]
            late = late + late_ref[k]
        g_ref[...] = g
        g_ref[PACK_LATE_ROW:PACK_LATE_ROW + 8, :] = late
        g = g_ref[...]
        d_ref[...], nm_ref[...], nv_ref[...] = _adamw_math(w_ref[...], g, m_ref[...], v_ref[...])
        total = jnp.sum(g[loss_row:loss_row + 1, :]) * (0.5 / D_MODEL)
        loss_ref[...] = jnp.full(loss_ref.shape, total, F32)

    full = lambda s: pl.BlockSpec(s, lambda i: (0,) * len(s))
    pack = (PACK_ROWS, D_MODEL)
    return pl.pallas_call(
        body, grid=(1,), name="small_sum_adamw",
        in_specs=[full((N_DEV,) + pack), full((N_DEV, 8, D_MODEL)), full(pack), full(pack), full(pack)],
        out_specs=[full(pack)] * 4 + [full((8, 128))],
        out_shape=[SDS(pack, F32)] * 4 + [SDS((8, 128), F32)],
        compiler_params=_cp("arbitrary"),
    )(all_small, all_late, w, m, v)


def _pack_small(grp, scale, g_mix, g_mlp, g_f, loss_lanes):
    def part(vec):
        vec = vec.reshape(1, -1)
        return jnp.pad(vec, ((0, 7), (0, D_MODEL - vec.shape[1])))
    return jnp.concatenate([grp.reshape(-1, D_MODEL), part(scale), part(g_mix), part(g_mlp), part(g_f),
                            part(loss_lanes)], axis=0)


def _unpack_small(pack):
    n_grp = len(POOL_WINDOWS) * POOL_GROUP_W * POOL_GROUP_W // D_MODEL
    grp = pack[:n_grp].reshape(1, len(POOL_WINDOWS), POOL_GROUP_W, POOL_GROUP_W)
    scale = pack[n_grp, :POOL_W].reshape(1, POOL_W)
    g_mix = pack[n_grp + 8].reshape(1, D_MODEL)
    g_mlp = pack[n_grp + 16].reshape(1, D_MODEL)
    g_f = pack[n_grp + 24].reshape(D_MODEL)
    return grp, scale, g_mix, g_mlp, g_f


def _block_diag(grp):
    out = jnp.zeros((POOL_W, POOL_W), grp.dtype)
    for k in range(len(POOL_WINDOWS)):
        out = lax.dynamic_update_slice(out, grp[k], (k * POOL_GROUP_W, k * POOL_GROUP_W))
    return out


def kernel(x, norm_mix_g, w_in, w_att_out, w_pool_grp, pool_scale, w_pool_out, w_out, norm_mlp_g, w_mlp_in, w_mlp_out, norm_final_g, loss_target, m_norm_mix_g, m_w_in, m_w_att_out, m_w_pool_grp, m_pool_scale, m_w_pool_out, m_w_out, m_norm_mlp_g, m_w_mlp_in, m_w_mlp_out, m_norm_final_g, v_norm_mix_g, v_w_in, v_w_att_out, v_w_pool_grp, v_pool_scale, v_w_pool_out, v_w_out, v_norm_mlp_g, v_w_mlp_in, v_w_mlp_out, v_norm_final_g):
    S = x.shape[1]
    xs, target = x[0], loss_target[0]
    big = [w_in[0], w_att_out[0], w_pool_out[0], w_out[0], w_mlp_in[0], w_mlp_out[0]]
    big_m = [m_w_in[0], m_w_att_out[0], m_w_pool_out[0], m_w_out[0], m_w_mlp_in[0], m_w_mlp_out[0]]
    big_v = [v_w_in[0], v_w_att_out[0], v_w_pool_out[0], v_w_out[0], v_w_mlp_in[0], v_w_mlp_out[0]]

    chip = 2 * lax.axis_index("x") + lax.axis_index("y")
    core = lax.axis_index("c")
    place = jnp.stack([chip, core]).astype(jnp.int32)
    names = ("w_in", "w_att_out", "w_pool_out", "w_out", "w_mlp_in", "w_mlp_out")

    shards = [w.astype(BF16) for w in big]
    bufs = [_place_own(sh, N_CHIPS, chip) for sh in shards]
    wbd = _block_diag(w_pool_grp[0]).astype(BF16)
    g_final = norm_final_g.reshape(1, D_MODEL)
    stat_expand, stat_reduce = _stat_matrices()

    u, z_own, landed_in = _norm_inproj_own(xs, norm_mix_g, shards[0], bufs[0])
    (wg_in,) = _pair_forward([landed_in], [shards[0].shape[0]], "w_in_pair_forward")
    (qkv0, qkv1, qkv2, pz, gates), (wg_ao, wg_po, wg_out) = _inproj_rest(u, z_own, wg_in, shards[1:4], bufs[1:4])
    wg_out = wg_out.reshape(D_MODEL, D_MODEL)
    qkv = (qkv0, qkv1, qkv2)
    att = [_attn_fwd(qkv[grp], grp) for grp in range(3)]
    (a, lt0, lt1, lt2, pooled, mixed, p, merged, h1, m), (wg_mi, wg_mo) = _mixer_out(
        [o for o, _ in att], [l for _, l in att], pz, gates, xs, wg_ao, wg_po, wbd, pool_scale, wg_out, norm_mlp_g,
        stat_expand, shards[4:], bufs[4:])
    hid, dh2, dh2b, loss_lanes, dg_final = _mlp_fwd_loss(m, h1, target, wg_mi, wg_mo, g_final)

    def pair_reduce(grads, grad_names):
        recv = _pair_exchange([narrow for _, narrow in grads])
        pair = [_pair_sum(place, g, r, f"pair_sum_{nm}") for (g, _), r, nm in zip(grads, recv, grad_names)]
        return [own for own, _ in pair], [s for _, s in pair]

    def chip_reduce(owns, landed_sums, grad_names):
        return [_chip_sum(place, own, r, f"chip_sum_{nm}") for own, r, nm in zip(owns, landed_sums, grad_names)]

    dpre, dh1, dh1b, dg_mlp = _mlp_bwd(dh2, dh2b, hid, h1, wg_mi, wg_mo, norm_mlp_g)
    own_mlp, sums_mlp = pair_reduce(_wgrads_mlp(m, dpre, hid, dh2b), names[4:])
    (da1, dp1, dgates, da0, dag1, dag2, dd0, dd1, dd2, dmixed, dqp, dscale), landed_mlp = _mixer_bwd(
        dh1b, a, p, mixed, gates, wg_out, wg_ao, wg_po, wbd, pool_scale, stat_reduce, sums_mlp)
    g_mi, g_mo = chip_reduce(own_mlp, landed_mlp, names[4:])
    grads_mixer, g_bd = _wgrads_mixer(a, da1, p, dp1, merged, dh1b, pooled, dmixed)

    zero = jnp.zeros((D_MODEL,), F32)
    g_grp = jnp.stack([g_bd[k * POOL_GROUP_W:(k + 1) * POOL_GROUP_W, k * POOL_GROUP_W:(k + 1) * POOL_GROUP_W]
                       for k in range(len(POOL_WINDOWS))])
    small = _pack_small(g_grp, dscale, zero, dg_mlp, dg_final, loss_lanes)
    *dqkv0, small_all = _attn_bwd(qkv[0], da0, lt0, dd0, 0, packs=_place_own(small, N_DEV, 2 * chip + core))
    dqkv = [dqkv0, _attn_bwd(qkv[1], dag1, lt1, dd1, 1), _attn_bwd(qkv[2], dag2, lt2, dd2, 2)]
    dz = _dz_assemble(dqkv, dqp)
    own_in, sums_in = pair_reduce([_wgrad_in(u, dz, dgates)] + grads_mixer, names[:4])
    dx, dg_mix, landed_in = _inproj_dx(dz, dgates, dh1, xs, norm_mix_g, wg_in, sums_in)
    g_in, g_ao, g_po, g_out = chip_reduce(own_in, landed_in, names[:4])
    late = jnp.pad(dg_mix, ((0, 7), (0, 0)))
    full, late_all = _finish_exchange([g_in, g_ao, g_po, g_out, g_mi, g_mo], _place_own(late, N_DEV, 2 * chip + core))

    small_w = _pack_small(w_pool_grp[0], pool_scale, norm_mix_g, norm_mlp_g, norm_final_g, zero)
    small_m = _pack_small(m_w_pool_grp[0], m_pool_scale, m_norm_mix_g, m_norm_mlp_g, m_norm_final_g, zero)
    small_v = _pack_small(v_w_pool_grp[0], v_pool_scale, v_norm_mix_g, v_norm_mlp_g, v_norm_final_g, zero)
    sg, sd, sm, sv, loss_tile = _small_sum_adamw(small_all, late_all, small_w, small_m, small_v)
    full = [full[0].reshape(CHUNKS_PER_SHARD, D_MODEL, CHUNK)] + list(full[1:])
    upd = [_adamw(w, g, mm, vv, f"adamw_{nm}") for w, g, mm, vv, nm in zip(big, full, big_m, big_v, names)]

    def ordered(small_pack, bigs):
        grp, scale, g_mix, g_mlp, g_f = _unpack_small(small_pack)
        b_in, b_ao, b_po, b_out, b_mi, b_mo = [b[None] for b in bigs]
        return (g_mix, b_in, b_ao, grp, scale, b_po, b_out, g_mlp, b_mi, b_mo, g_f)

    return (loss_tile[0, 0], dx[None],
            *ordered(sg, [t[0] for t in upd]),
            *ordered(sd, [t[1] for t in upd]),
            *ordered(sm, [t[2] for t in upd]),
            *ordered(sv, [t[3] for t in upd]))
```

```python
import functools

import jax
import jax.numpy as jnp
from jax import lax
from jax.experimental import pallas as pl
from jax.experimental.pallas import tpu as pltpu

F32 = jnp.float32
BF16 = jnp.bfloat16
SDS = jax.ShapeDtypeStruct
MESH = pl.DeviceIdType.MESH

D_MODEL = 1024
D_FF = 4096
N_CHIPS = 4
N_DEV = 8
DILATIONS = (1, 4, 16)
BAND = 128
GROUP_W = 256
PAIR_W = 128
HEAD_W = 64
STAT_W = 128
STAT_HEAD_W = 32
POOL_W = 768
POOL_GROUP_W = 192
POOL_WINDOWS = (2, 4, 8, 16)
POOL_HALO = 16
N_IN = 5120
CHUNK = 256
N_CHUNKS = N_IN // CHUNK
N_DZ_CHUNKS = 12
CHUNKS_PER_SHARD = 5
WGRAD_IN_GROUP = 4
NORM_EPS = 1e-6
ALIBI_MAX_BIAS = 8.0
N_HEADS = 12
NEG = -1e30

ADAM_LR, ADAM_B1, ADAM_B2, ADAM_EPS, ADAM_WD, ADAM_STEP = 0.001, 0.9, 0.999, 1e-08, 0.01, 10

TM = 512
TMB = 512
ATT_TILE = ((1, 16), (4, 4), (16, 1))
BK = 4096
ELEMENTWISE_BLOCK = 1 << 20
VMEM_LIMIT = 56 * 1024 * 1024
PACK_ROWS = 184
PACK_LATE_ROW = 152

NT = (((1,), (1,)), ((), ()))
TN = (((0,), (0,)), ((), ()))


def _cp(*sem):
    return pltpu.CompilerParams(dimension_semantics=sem, vmem_limit_bytes=VMEM_LIMIT)


def _resident(shape):
    nd = len(shape)
    return pl.BlockSpec(shape, lambda *_: (0,) * nd, pipeline_mode=pl.Buffered(1))


def _row_block(rows, cap=256):
    return max(b for b in range(16, min(rows, cap) + 1, 16) if rows % b == 0)


def _dot(a, b):
    return jnp.dot(a, b, preferred_element_type=F32)


def _dot_nt(a, b):
    return lax.dot_general(a, b, NT, preferred_element_type=F32)


def _dot_tn(a, b):
    return lax.dot_general(a, b, TN, preferred_element_type=F32)


def _w_in_chunk(w_ref, n):
    return w_ref[n // CHUNKS_PER_SHARD, :, (n % CHUNKS_PER_SHARD) * CHUNK:(n % CHUNKS_PER_SHARD + 1) * CHUNK]


def _sigmoid(x):
    return 0.5 * jnp.tanh(0.5 * x.astype(F32)) + 0.5


def _rms_fwd(x, g):
    r = lax.rsqrt(jnp.mean(x * x, axis=-1, keepdims=True) + NORM_EPS)
    xh = x * r
    return xh * g, xh, r


def _rms_bwd(dy, xh, r, g):
    dxh = dy * g
    return r * (dxh - xh * jnp.mean(dxh * xh, axis=-1, keepdims=True))


def _per_head_lanes(cols):
    rows = cols[0].shape[0]
    lane = lax.broadcasted_iota(jnp.int32, (rows, STAT_W), 1)
    out = cols[3]
    for h in (2, 1, 0):
        out = jnp.where(lane < (h + 1) * STAT_HEAD_W, cols[h], out)
    return out


def _head_col(stat, h):
    return stat[:, h * STAT_HEAD_W:h * STAT_HEAD_W + 1]


def _stat_matrices():
    s = lax.broadcasted_iota(jnp.int32, (STAT_W, GROUP_W), 0)
    c = lax.broadcasted_iota(jnp.int32, (STAT_W, GROUP_W), 1)
    expand = (s == (c // HEAD_W) * STAT_HEAD_W).astype(BF16)
    reduce = (s // STAT_HEAD_W == c // HEAD_W).astype(BF16).T
    return expand, reduce


def _dot_split(x, m):
    hi = x.astype(BF16)
    lo = (x - hi.astype(F32)).astype(BF16)
    return _dot(hi, m) + _dot(lo, m)


def _deinterleave_store(val, s_ref, out_ref, lead, d, rows, dtype):
    if d == 1:
        out_ref[lead + (0,)] = val.astype(dtype)
        return
    for h in range(2):
        s_ref[h] = val[:, h * PAIR_W:(h + 1) * PAIR_W]
    for r in range(d):
        for h in range(2):
            out_ref[lead + (r, slice(None), slice(h * PAIR_W, (h + 1) * PAIR_W))] = (
                s_ref[h, pl.ds(r, rows // d, stride=d), :].astype(dtype))


def _interleave_load(in_ref, lead, s_ref, d, rows):
    for r in range(d):
        for h in range(2):
            s_ref[h, pl.ds(r, rows // d, stride=d), :] = (
                in_ref[lead + (r, slice(None), slice(h * PAIR_W, (h + 1) * PAIR_W))].astype(F32))


def _norm_inproj_own(x, g, w_own, buf):
    S = x.shape[0]
    n_tiles = S // TM

    def body(x_ref, g_ref, w_ref, shard_ref, buf_in, u_ref, z_ref, buf_ref, send_sem, recv_sem):
        i = pl.program_id(0)

        def copies():
            return _weight_half_copies([shard_ref], [buf_ref], [w_own.shape[0]], send_sem, recv_sem)

        @pl.when(i == 0)
        def _():
            for cpy in copies():
                cpy.start()

        u = _rms_fwd(x_ref[...], g_ref[...])[0].astype(BF16)
        u_ref[...] = u
        for t in range(CHUNKS_PER_SHARD):
            z_ref[t] = _dot(u, w_ref[:, t * CHUNK:(t + 1) * CHUNK]).astype(BF16)

        @pl.when(i == n_tiles - 1)
        def _():
            for cpy in copies():
                cpy.wait()

    row = lambda w: pl.BlockSpec((TM, w), lambda i: (i, 0))
    return pl.pallas_call(
        body, grid=(n_tiles,), name="norm_inproj_own",
        in_specs=[row(D_MODEL), _resident((1, D_MODEL)), _resident(w_own.shape), ANY, ANY],
        out_specs=[row(D_MODEL), pl.BlockSpec((CHUNKS_PER_SHARD, TM, CHUNK), lambda i: (0, i, 0)), ANY],
        out_shape=[SDS((S, D_MODEL), BF16), SDS((CHUNKS_PER_SHARD, S, CHUNK), BF16), SDS(buf.shape, buf.dtype)],
        scratch_shapes=[pltpu.SemaphoreType.DMA((3,)), pltpu.SemaphoreType.DMA((3,))],
        input_output_aliases={4: 2},
        compiler_params=_cp("arbitrary"),
    )(x, g, w_own, w_own, buf)


def _hosted_allgather(i, n_steps, shard_refs, buf_refs, rows, sems):
    send_sem, recv_sem, fsend_sem, frecv_sem = sems
    ici = lambda: _weight_half_copies(shard_refs, buf_refs, rows, send_sem, recv_sem)
    forward = lambda: _pair_forward_copies(buf_refs, rows, fsend_sem, frecv_sem)

    def begin():
        @pl.when(i == 0)
        def _():
            for cpy in ici():
                cpy.start()

        @pl.when(i == n_steps // 2)
        def _():
            for cpy, (fwd, _) in zip(ici(), forward()):
                cpy.wait_recv()
                fwd.start()

    def end():
        @pl.when(i == n_steps - 1)
        def _():
            for cpy, (fwd, landing) in zip(ici(), forward()):
                landing.wait_recv()
                fwd.wait_send()
                cpy.wait_send()

    return begin, end


def _inproj_rest(u, z_own, w_in, shards, bufs):
    S = u.shape[0]
    n_tiles = S // TM
    n = len(shards)

    def body(*refs):
        u_ref, zown_ref, w_ref = refs[0:3]
        shard_refs = refs[3:3 + n]
        q0_ref, q1_ref, q2_ref, pz_ref, gate_ref = refs[3 + 2 * n:8 + 2 * n]
        buf_refs = refs[8 + 2 * n:8 + 3 * n]
        s_ref = refs[8 + 3 * n]
        i = pl.program_id(0)
        chip = 2 * lax.axis_index("x") + lax.axis_index("y")
        begin, end = _hosted_allgather(i, n_tiles, shard_refs, buf_refs, [sh.shape[0] for sh in shards],
                                       refs[9 + 3 * n:])
        begin()

        u = u_ref[...]
        qkv_refs = (q0_ref, q1_ref, q2_ref)

        def emit(k, zc):
            if k < 9:
                which, grp = k // 3, k % 3
                if which == 0:
                    zc = zc * 0.125
                _deinterleave_store(zc, s_ref, qkv_refs[grp], (which,), DILATIONS[grp], TM, BF16)
            elif k < N_DZ_CHUNKS:
                pz_ref[:, (k - 9) * CHUNK:(k - 8) * CHUNK] = zc.astype(BF16)
            else:
                gate_ref[:, (k - N_DZ_CHUNKS) * CHUNK:(k - N_DZ_CHUNKS + 1) * CHUNK] = zc.astype(BF16)

        def all_chunks(own_shard):
            for k in range(N_CHUNKS):
                if k // CHUNKS_PER_SHARD == own_shard:
                    emit(k, zown_ref[k % CHUNKS_PER_SHARD].astype(F32))
                else:
                    emit(k, _dot(u, _w_in_chunk(w_ref, k)))

        for shard in range(N_CHIPS):
            pl.when(chip == shard)(functools.partial(all_chunks, shard))
        end()

    row = lambda w: pl.BlockSpec((TM, w), lambda i: (i, 0))
    res = pl.pallas_call(
        body, grid=(n_tiles,), name="inproj_rest",
        in_specs=[row(D_MODEL), pl.BlockSpec((CHUNKS_PER_SHARD, TM, CHUNK), lambda i: (0, i, 0)),
                  _resident(w_in.shape)] + [ANY] * (2 * n),
        out_specs=[pl.BlockSpec((3, d, TM // d, GROUP_W), lambda i: (0, 0, i, 0)) for d in DILATIONS]
        + [row(POOL_W), row(2 * D_MODEL)] + [ANY] * n,
        out_shape=[SDS((3, d, S // d, GROUP_W), BF16) for d in DILATIONS]
        + [SDS((S, POOL_W), BF16), SDS((S, 2 * D_MODEL), BF16)] + [SDS(b.shape, b.dtype) for b in bufs],
        scratch_shapes=[pltpu.VMEM((2, TM, PAIR_W), F32)] + [pltpu.SemaphoreType.DMA((3 * n,))] * 4,
        input_output_aliases={3 + n + w: 5 + w for w in range(n)},
        compiler_params=_cp("arbitrary"),
    )(u, z_own, w_in, *shards, *bufs)
    return res[:5], res[5:]


def _band_bias(grp, d):
    row = lax.broadcasted_iota(jnp.int32, (BAND, 2 * BAND), 0)
    col = lax.broadcasted_iota(jnp.int32, (BAND, 2 * BAND), 1)
    steps = BAND + row - col
    valid = (steps >= 0) & (steps <= BAND)
    stepsf = (steps * d).astype(F32)
    biases = []
    for hh in range(4):
        slope = 2.0 ** (-ALIBI_MAX_BIAS * (grp * 4 + hh + 1) / N_HEADS)
        biases.append(jnp.where(valid, -slope * stepsf, NEG))
    return biases, col


def _attn_tiles(grp, L):
    rr, rb = ATT_TILE[grp]
    rb = min(rb, L // BAND)
    return rr, rb, L // (rb * BAND)


def _kv_tile(cur_ref, prev_ref, rr, rb, cs):
    if rb == 0:
        return jnp.concatenate([prev_ref[rr, :, cs], cur_ref[rr, 0:BAND, cs]], axis=0)
    return cur_ref[rr, (rb - 1) * BAND:(rb + 1) * BAND, cs]


def _attn_fwd(qkv, grp):
    d = DILATIONS[grp]
    L = qkv.shape[2]
    RR, RB, nb = _attn_tiles(grp, L)

    def body(q_ref, kc_ref, kp_ref, vc_ref, vp_ref, o_ref, lse_ref):
        i = pl.program_id(0)
        biases, col = _band_bias(grp, d)
        first_keys_ok = (col >= BAND) | (i > 0)
        is_a = lax.broadcasted_iota(jnp.int32, (BAND, PAIR_W), 1) < HEAD_W
        heads = [(rr, rb, cp, h2) for rr in range(RR) for rb in range(RB) for cp in range(2) for h2 in range(2)]

        def tile(head):
            rr, rb, cp, _ = head
            return rr, rb, slice(rb * BAND, (rb + 1) * BAND), slice(cp * PAIR_W, (cp + 1) * PAIR_W)

        def scores(head):
            rr, rb, rows, cs = tile(head)
            q2 = q_ref[rr, rows, cs]
            b = biases[head[2] * 2 + head[3]]
            if rb == 0:
                b = jnp.where(first_keys_ok, b, NEG)
            sel = is_a if head[3] == 0 else jnp.logical_not(is_a)
            return _dot_nt(jnp.where(sel, q2, jnp.zeros_like(q2)), _kv_tile(kc_ref, kp_ref, rr, rb, cs)) + b

        s_next = scores(heads[0])
        outs, lses = {}, {}
        for idx, head in enumerate(heads):
            s = s_next
            if idx + 1 < len(heads):
                s_next = scores(heads[idx + 1])
            rr, rb, rows, cs = tile(head)
            m = jnp.max(s, axis=-1, keepdims=True)
            p = jnp.exp(s - m)
            l = jnp.sum(p, axis=-1, keepdims=True)
            outs[head[3]] = _dot(p.astype(BF16), _kv_tile(vc_ref, vp_ref, rr, rb, cs)) * (1.0 / l)
            lses[head[2] * 2 + head[3]] = m + jnp.log(l)
            if head[3] == 1:
                o_ref[rr, rows, cs] = jnp.where(is_a, outs[0], outs[1]).astype(BF16)
            if head[2] == 1 and head[3] == 1:
                lse_ref[rr, rows, :] = _per_head_lanes(lses)

    cur = lambda w: pl.BlockSpec((None, RR, RB * BAND, GROUP_W), lambda i, j: (w, j, i, 0))
    prev = lambda w: pl.BlockSpec((None, RR, BAND, GROUP_W), lambda i, j: (w, j, jnp.maximum(i * RB - 1, 0), 0))
    return pl.pallas_call(
        body, grid=(nb, d // RR), name=f"attn_fwd_g{grp}",
        in_specs=[cur(0), cur(1), prev(1), cur(2), prev(2)],
        out_specs=[pl.BlockSpec((RR, RB * BAND, GROUP_W), lambda i, j: (j, i, 0)),
                   pl.BlockSpec((RR, RB * BAND, STAT_W), lambda i, j: (j, i, 0))],
        out_shape=[SDS((d, L, GROUP_W), BF16), SDS((d, L, STAT_W), F32)],
        compiler_params=_cp("parallel", "parallel"),
    )(qkv, qkv, qkv, qkv, qkv)


def _pool_column_select(col, vals):
    return jnp.where(col < POOL_GROUP_W, vals[0],
                     jnp.where(col < 2 * POOL_GROUP_W, vals[1],
                               jnp.where(col < 3 * POOL_GROUP_W, vals[2], vals[3])))


def _pool_inv_count(i, rows):
    t = i * rows + lax.broadcasted_iota(jnp.int32, (rows, POOL_W), 0)
    col = lax.broadcasted_iota(jnp.int32, (rows, POOL_W), 1)
    win = _pool_column_select(col, POOL_WINDOWS)
    return 1.0 / jnp.minimum(t + 1, win).astype(F32), col


def _mixer_out(outs, lses, pz, gates, x, w_ao, w_po, wbd, scale, w_out, g_mlp, expand, shards, bufs):
    S = x.shape[0]
    n_tiles = S // TMB
    n = len(shards)

    def body(*refs):
        (o0_ref, l0_ref, o1_ref, l1_ref, o2_ref, l2_ref, pz_ref, halo_ref, gate_ref, x_ref,
         wao_ref, wpo_ref, wbd_ref, sc_ref, wout_ref, g_ref, expand_ref) = refs[0:17]
        shard_refs = refs[17:17 + n]
        (a_ref, lt0_ref, lt1_ref, lt2_ref, pooled_ref, mixed_ref, p_ref, merged_ref, h1_ref,
         m_ref) = refs[17 + 2 * n:27 + 2 * n]
        buf_refs = refs[27 + 2 * n:27 + 3 * n]
        so1, sl1, so2, sl2, slt, ext_ref = refs[27 + 3 * n:33 + 3 * n]
        i = pl.program_id(0)
        begin, end = _hosted_allgather(i, n_tiles, shard_refs, buf_refs, [sh.shape[0] for sh in shards],
                                       refs[33 + 3 * n:])
        begin()
        _interleave_load(o1_ref, (), so1, DILATIONS[1], TMB)
        _interleave_load(o2_ref, (), so2, DILATIONS[2], TMB)
        for ref, sref, d in ((l1_ref, sl1, DILATIONS[1]), (l2_ref, sl2, DILATIONS[2])):
            for r in range(d):
                sref[0, pl.ds(r, TMB // d, stride=d), :] = ref[r]
        l0, l1, l2 = l0_ref[0], sl1[0], sl2[0]
        mx = jnp.maximum(jnp.maximum(l0, l1), l2)
        e0, e1, e2 = jnp.exp(l0 - mx), jnp.exp(l1 - mx), jnp.exp(l2 - mx)
        den = e0 + e1 + e2
        inv = 1.0 / den
        slt[0] = mx + jnp.log(den)
        w0, w1, w2 = [_dot_split(e * inv, expand_ref[...]) for e in (e0, e1, e2)]
        for h in range(2):
            hs = slice(h * PAIR_W, (h + 1) * PAIR_W)
            a_ref[:, hs] = (w0[:, hs] * o0_ref[0, :, hs].astype(F32) + w1[:, hs] * so1[h]
                            + w2[:, hs] * so2[h]).astype(BF16)
        lt0_ref[0] = slt[0]
        for ref, d in ((lt1_ref, DILATIONS[1]), (lt2_ref, DILATIONS[2])):
            for r in range(d):
                ref[r] = slt[0, pl.ds(r, TMB // d, stride=d), :]

        pz_t = pz_ref[...].astype(F32)
        ext_ref[0:POOL_HALO, :] = jnp.where(i > 0, halo_ref[...].astype(F32), 0.0)
        ext_ref[POOL_HALO:, :] = pz_t
        sums = []
        acc = ext_ref[...]
        for k in (1, 2, 4, 8):
            acc = acc + pltpu.roll(acc, k, 0)
            sums.append(acc[POOL_HALO:, :])
        inv_cnt, col = _pool_inv_count(i, TMB)
        pooled = (_pool_column_select(col, sums) * inv_cnt - pz_t).astype(BF16)
        pooled_ref[...] = pooled
        mixed = _dot(pooled, wbd_ref[...])
        mixed_ref[...] = mixed.astype(BF16)
        p = (mixed * sc_ref[...]).astype(BF16)
        p_ref[...] = p

        a = a_ref[...]
        for j in range(N_CHIPS):
            js = slice(j * CHUNK, (j + 1) * CHUNK)
            ga = gate_ref[:, js]
            gp = gate_ref[:, D_MODEL + j * CHUNK:D_MODEL + (j + 1) * CHUNK]
            mj = _sigmoid(ga) * _dot(a, wao_ref[j]) + _sigmoid(gp) * _dot(p, wpo_ref[j])
            merged_ref[:, js] = mj.astype(BF16)
        h1 = x_ref[...] + _dot(merged_ref[...], wout_ref[...])
        h1_ref[...] = h1
        m_ref[...] = _rms_fwd(h1, g_ref[...])[0].astype(BF16)
        end()

    row = lambda w: pl.BlockSpec((TMB, w), lambda i: (i, 0))
    grp_spec = lambda d: pl.BlockSpec((d, TMB // d, GROUP_W), lambda i: (0, i, 0))
    stat_spec = lambda d: pl.BlockSpec((d, TMB // d, STAT_W), lambda i: (0, i, 0))
    halo = pl.BlockSpec((POOL_HALO, POOL_W), lambda i: (jnp.maximum(i * (TMB // POOL_HALO) - 1, 0), 0))
    d0, d1, d2 = DILATIONS
    pair_scratch = pltpu.VMEM((2, TMB, PAIR_W), F32)
    stat_scratch = pltpu.VMEM((1, TMB, STAT_W), F32)
    res = pl.pallas_call(
        body, grid=(n_tiles,), name="mixer_out",
        in_specs=[grp_spec(d0), stat_spec(d0), grp_spec(d1), stat_spec(d1), grp_spec(d2), stat_spec(d2),
                  row(POOL_W), halo, row(2 * D_MODEL), row(D_MODEL),
                  _resident(w_ao.shape), _resident(w_po.shape), _resident(wbd.shape), _resident(scale.shape),
                  _resident(w_out.shape), _resident(g_mlp.shape), _resident(expand.shape)] + [ANY] * (2 * n),
        out_specs=[row(GROUP_W), stat_spec(d0), stat_spec(d1), stat_spec(d2),
                   row(POOL_W), row(POOL_W), row(POOL_W), row(D_MODEL), row(D_MODEL), row(D_MODEL)] + [ANY] * n,
        out_shape=[SDS((S, GROUP_W), BF16)] + [SDS((d, S // d, STAT_W), F32) for d in DILATIONS]
        + [SDS((S, POOL_W), BF16), SDS((S, POOL_W), BF16), SDS((S, POOL_W), BF16),
           SDS((S, D_MODEL), BF16), SDS((S, D_MODEL), F32), SDS((S, D_MODEL), BF16)]
        + [SDS(b.shape, b.dtype) for b in bufs],
        scratch_shapes=[pair_scratch, stat_scratch, pair_scratch, stat_scratch, stat_scratch,
                        pltpu.VMEM((TMB + POOL_HALO, POOL_W), F32)] + [pltpu.SemaphoreType.DMA((3 * n,))] * 4,
        input_output_aliases={17 + n + w: 10 + w for w in range(n)},
        compiler_params=_cp("arbitrary"),
    )(outs[0], lses[0], outs[1], lses[1], outs[2], lses[2], pz, pz, gates, x,
      w_ao, w_po, wbd, scale, w_out, g_mlp, expand, *shards, *bufs)
    return res[:10], res[10:]


def _mlp_fwd_loss(m, h1, target, w_mi, w_mo, g_f):
    S = m.shape[0]

    def body(m_ref, h1_ref, t_ref, wmi_ref, wmo_ref, g_ref, hid_ref, dh2_ref, dh2b_ref, loss_ref, dg_ref):
        @pl.when(pl.program_id(0) == 0)
        def _():
            loss_ref[...] = jnp.zeros_like(loss_ref)
            dg_ref[...] = jnp.zeros_like(dg_ref)

        mt = m_ref[...]
        acc = h1_ref[...]
        for c in range(N_CHIPS):
            hid = jnp.square(jnp.maximum(_dot(mt, wmi_ref[c]), 0.0)).astype(BF16)
            hid_ref[:, c * D_MODEL:(c + 1) * D_MODEL] = hid
            acc = acc + _dot(hid, wmo_ref[c])
        g = g_ref[...]
        y, hh, r = _rms_fwd(acc, g)
        e = y - t_ref[...]
        loss_ref[...] += jnp.sum(e * e, axis=0, keepdims=True)
        dy = e * (1.0 / D_MODEL)
        dg_ref[...] += jnp.sum(dy * hh, axis=0, keepdims=True)
        dh2 = _rms_bwd(dy, hh, r, g)
        dh2_ref[...] = dh2
        dh2b_ref[...] = dh2.astype(BF16)

    row = lambda w: pl.BlockSpec((TM, w), lambda i: (i, 0))
    vec = pl.BlockSpec((1, D_MODEL), lambda i: (0, 0))
    return pl.pallas_call(
        body, grid=(S // TM,), name="mlp_fwd_loss",
        in_specs=[row(D_MODEL), row(D_MODEL), row(D_MODEL), _resident(w_mi.shape), _resident(w_mo.shape),
                  _resident(g_f.shape)],
        out_specs=[row(D_FF), row(D_MODEL), row(D_MODEL), vec, vec],
        out_shape=[SDS((S, D_FF), BF16), SDS((S, D_MODEL), F32), SDS((S, D_MODEL), BF16),
                   SDS((1, D_MODEL), F32), SDS((1, D_MODEL), F32)],
        compiler_params=_cp("arbitrary"),
    )(m, h1, target, w_mi, w_mo, g_f)


def _mlp_bwd(dh2, dh2b, hid, h1, w_mi, w_mo, g_mlp):
    S = dh2.shape[0]

    def body(dh2_ref, dh2b_ref, hid_ref, h1_ref, wmi_ref, wmo_ref, g_ref, dpre_ref, dh1_ref, dh1b_ref, dg_ref):
        @pl.when(pl.program_id(0) == 0)
        def _():
            dg_ref[...] = jnp.zeros_like(dg_ref)

        d2 = dh2b_ref[...]
        dm = jnp.zeros((TM, D_MODEL), F32)
        dhid_next = _dot_nt(d2, wmo_ref[0])
        for c in range(N_CHIPS):
            cs = slice(c * D_MODEL, (c + 1) * D_MODEL)
            dhid = dhid_next
            if c + 1 < N_CHIPS:
                dhid_next = _dot_nt(d2, wmo_ref[c + 1])
            dpre = (dhid * (2.0 * jnp.sqrt(hid_ref[:, cs].astype(F32)))).astype(BF16)
            dpre_ref[:, cs] = dpre
            dm = dm + _dot_nt(dpre, wmi_ref[c])
        g = g_ref[...]
        _, hh, r = _rms_fwd(h1_ref[...], g)
        dg_ref[...] += jnp.sum(dm * hh, axis=0, keepdims=True)
        dh1 = dh2_ref[...] + _rms_bwd(dm, hh, r, g)
        dh1_ref[...] = dh1
        dh1b_ref[...] = dh1.astype(BF16)

    row = lambda w: pl.BlockSpec((TM, w), lambda i: (i, 0))
    return pl.pallas_call(
        body, grid=(S // TM,), name="mlp_bwd",
        in_specs=[row(D_MODEL), row(D_MODEL), row(D_FF), row(D_MODEL), _resident(w_mi.shape),
                  _resident(w_mo.shape), _resident(g_mlp.shape)],
        out_specs=[row(D_FF), row(D_MODEL), row(D_MODEL), pl.BlockSpec((1, D_MODEL), lambda i: (0, 0))],
        out_shape=[SDS((S, D_FF), BF16), SDS((S, D_MODEL), F32), SDS((S, D_MODEL), BF16), SDS((1, D_MODEL), F32)],
        compiler_params=_cp("arbitrary"),
    )(dh2, dh2b, hid, h1, w_mi, w_mo, g_mlp)


def _mixer_bwd(dh1b, a, p, mixed, gates, w_out, w_ao, w_po, wbd, scale, head_ones, sums):
    S = a.shape[0]
    n_tiles = S // TMB
    n = len(sums)

    def body(*refs):
        (dh1b_ref, a_ref, p_ref, mixed_ref, gate_ref, wout_ref, wao_ref, wpo_ref, wbd_ref, sc_ref,
         ones_ref) = refs[0:11]
        sum_refs = refs[11:11 + n]
        (da1_ref, dp1_ref, dgate_ref, da0_ref, dag1_ref, dag2_ref, dd0_ref, dd1_ref, dd2_ref,
         dmixed_ref, dqp_ref, dscale_ref) = refs[11 + n:23 + n]
        land_refs = refs[23 + n:23 + 2 * n]
        s_da, s_dd, send_sem, recv_sem = refs[23 + 2 * n:]
        i = pl.program_id(0)

        @pl.when(i == 0)
        def _():
            dscale_ref[...] = jnp.zeros_like(dscale_ref)
            for cpy in _chip_sum_copies(sum_refs, land_refs, send_sem, recv_sem):
                cpy.start()

        dmerged = _dot_nt(dh1b_ref[...], wout_ref[...])
        a = a_ref[...]
        p = p_ref[...]
        da = jnp.zeros((TMB, GROUP_W), F32)
        dp = jnp.zeros((TMB, POOL_W), F32)
        for j in range(N_CHIPS):
            js = slice(j * CHUNK, (j + 1) * CHUNK)
            sa = _sigmoid(gate_ref[:, js])
            sp = _sigmoid(gate_ref[:, D_MODEL + j * CHUNK:D_MODEL + (j + 1) * CHUNK])
            dmj = dmerged[:, js]
            da1 = (dmj * sa).astype(BF16)
            dp1 = (dmj * sp).astype(BF16)
            da1_ref[:, js] = da1
            dp1_ref[:, js] = dp1
            dgate_ref[j] = (dmj * _dot(a, wao_ref[j]) * sa * (1.0 - sa)).astype(BF16)
            dgate_ref[N_CHIPS + j] = (dmj * _dot(p, wpo_ref[j]) * sp * (1.0 - sp)).astype(BF16)
            da = da + _dot_nt(da1, wao_ref[j])
            dp = dp + _dot_nt(dp1, wpo_ref[j])

        dd = _dot_split(da * a.astype(F32), ones_ref[...])
        da0_ref[0] = da.astype(BF16)
        dd0_ref[0] = dd
        for h in range(2):
            s_da[h] = da[:, h * PAIR_W:(h + 1) * PAIR_W]
        s_dd[0] = dd
        for refs, d in (((dag1_ref, dd1_ref), DILATIONS[1]), ((dag2_ref, dd2_ref), DILATIONS[2])):
            for r in range(d):
                for h in range(2):
                    hs = slice(h * PAIR_W, (h + 1) * PAIR_W)
                    refs[0][r, :, hs] = s_da[h, pl.ds(r, TMB // d, stride=d), :].astype(BF16)
                refs[1][r] = s_dd[0, pl.ds(r, TMB // d, stride=d), :]

        sc = sc_ref[...]
        dscale_ref[...] += jnp.sum(dp * mixed_ref[...].astype(F32), axis=0, keepdims=True)
        dmixed = (dp * sc).astype(BF16)
        dmixed_ref[...] = dmixed
        inv_cnt, _ = _pool_inv_count(i, TMB)
        dqp_ref[...] = (_dot_nt(dmixed, wbd_ref[...]) * inv_cnt).astype(BF16)

        @pl.when(i == n_tiles - 1)
        def _():
            for cpy in _chip_sum_copies(sum_refs, land_refs, send_sem, recv_sem):
                cpy.wait()

    row = lambda w: pl.BlockSpec((TMB, w), lambda i: (i, 0))
    grp_spec = lambda d: pl.BlockSpec((d, TMB // d, GROUP_W), lambda i: (0, i, 0))
    stat_spec = lambda d: pl.BlockSpec((d, TMB // d, STAT_W), lambda i: (0, i, 0))
    d0, d1, d2 = DILATIONS
    res = pl.pallas_call(
        body, grid=(n_tiles,), name="mixer_bwd",
        in_specs=[row(D_MODEL), row(GROUP_W), row(POOL_W), row(POOL_W), row(2 * D_MODEL),
                  _resident(w_out.shape), _resident(w_ao.shape), _resident(w_po.shape), _resident(wbd.shape),
                  _resident(scale.shape), _resident(head_ones.shape)] + [ANY] * n,
        out_specs=[row(D_MODEL), row(D_MODEL), pl.BlockSpec((2 * N_CHIPS, TMB, CHUNK), lambda i: (0, i, 0)),
                   grp_spec(d0), grp_spec(d1), grp_spec(d2), stat_spec(d0), stat_spec(d1), stat_spec(d2),
                   row(POOL_W), row(POOL_W), pl.BlockSpec((1, POOL_W), lambda i: (0, 0))] + [ANY] * n,
        out_shape=[SDS((S, D_MODEL), BF16), SDS((S, D_MODEL), BF16), SDS((2 * N_CHIPS, S, CHUNK), BF16)]
        + [SDS((d, S // d, GROUP_W), BF16) for d in DILATIONS]
        + [SDS((d, S // d, STAT_W), F32) for d in DILATIONS]
        + [SDS((S, POOL_W), BF16), SDS((S, POOL_W), BF16), SDS((1, POOL_W), F32)]
        + [SDS(t.shape, t.dtype) for t in sums],
        scratch_shapes=[pltpu.VMEM((2, TMB, PAIR_W), F32), pltpu.VMEM((1, TMB, STAT_W), F32),
                        pltpu.SemaphoreType.DMA((3 * n,)), pltpu.SemaphoreType.DMA((3 * n,))],
        compiler_params=_cp("arbitrary"),
    )(dh1b, a, p, mixed, gates, w_out, w_ao, w_po, wbd, scale, head_ones, *sums)
    return res[:12], res[12:]


def _attn_bwd(qkv, da, lt, dd, grp, packs=None):
    d = DILATIONS[grp]
    L = qkv.shape[2]
    RR, RB, nb = _attn_tiles(grp, L)
    n_j = d // RR
    hosted = packs is not None

    def body(*refs):
        q_ref, kc_ref, kp_ref, vc_ref, vp_ref, da_ref, lt_ref, dd_ref = refs[0:8]
        dq_ref, dk_ref, dv_ref = refs[8 + hosted:11 + hosted]
        dk_acc, dv_acc = refs[11 + 2 * hosted:13 + 2 * hosted]
        i = pl.program_id(1)
        if hosted:
            j = pl.program_id(0)
            start, relay, finish = _pack_allgather(refs[11 + hosted], *refs[13 + 2 * hosted:])
            pl.when((j == 0) & (i == 0))(start)
            pl.when((j == 0) & (i == nb // 2))(relay)

        @pl.when(i == 0)
        def _():
            dk_acc[...] = jnp.zeros_like(dk_acc)
            dv_acc[...] = jnp.zeros_like(dv_acc)

        def compute(cur, prv):
            dk_acc[cur] = jnp.zeros((RR, RB * BAND, GROUP_W), F32)
            dv_acc[cur] = jnp.zeros((RR, RB * BAND, GROUP_W), F32)
            biases, col = _band_bias(grp, d)
            first_keys_ok = (col >= BAND) | (i > 0)
            is_a = lax.broadcasted_iota(jnp.int32, (BAND, PAIR_W), 1) < HEAD_W
            for rr in range(RR):
                for rb in range(RB):
                    rows = slice(rb * BAND, (rb + 1) * BAND)
                    for cp in range(2):
                        cs = slice(cp * PAIR_W, (cp + 1) * PAIR_W)
                        q2 = q_ref[rr, rows, cs]
                        da2 = da_ref[rr, rows, cs]
                        lt2 = lt_ref[rr, rows, :]
                        dd2 = dd_ref[rr, rows, :]
                        kcat = _kv_tile(kc_ref, kp_ref, rr, rb, cs)
                        vcat = _kv_tile(vc_ref, vp_ref, rr, rb, cs)
                        q2t = q2.astype(F32).T.astype(BF16)
                        da2t = da2.astype(F32).T.astype(BF16)
                        dqs, dkts, dvts, scores, dpvs = [], [], [], [], []
                        for h2 in range(2):
                            sel = is_a if h2 == 0 else jnp.logical_not(is_a)
                            b = biases[cp * 2 + h2]
                            if rb == 0:
                                b = jnp.where(first_keys_ok, b, NEG)
                            scores.append(_dot_nt(jnp.where(sel, q2, jnp.zeros_like(q2)), kcat) + b)
                            dpvs.append(_dot_nt(jnp.where(sel, da2, jnp.zeros_like(da2)), vcat))
                        for h2 in range(2):
                            lane0 = h2 * HEAD_W
                            p = jnp.exp(scores[h2] - _head_col(lt2, cp * 2 + h2))
                            ds = (p * (dpvs[h2] - _head_col(dd2, cp * 2 + h2))).astype(BF16)
                            dqs.append(_dot(ds, kcat))
                            dkts.append(_dot(q2t[lane0:lane0 + HEAD_W, :], ds))
                            dvts.append(_dot(da2t[lane0:lane0 + HEAD_W, :], p.astype(BF16)))
                        dq_ref[rr, rows, cs] = (jnp.where(is_a, dqs[0], dqs[1]) * 0.125).astype(BF16)
                        dkc = jnp.concatenate(dkts, axis=0).T
                        dvc = jnp.concatenate(dvts, axis=0).T
                        if rb == 0:
                            last = slice((RB - 1) * BAND, RB * BAND)
                            dk_acc[prv, rr, last, cs] += dkc[0:BAND]
                            dv_acc[prv, rr, last, cs] += dvc[0:BAND]
                            dk_acc[cur, rr, 0:BAND, cs] += dkc[BAND:]
                            dv_acc[cur, rr, 0:BAND, cs] += dvc[BAND:]
                        else:
                            both = slice((rb - 1) * BAND, (rb + 1) * BAND)
                            dk_acc[cur, rr, both, cs] += dkc
                            dv_acc[cur, rr, both, cs] += dvc

        def flush(prv):
            dk_ref[...] = dk_acc[prv].astype(BF16)
            dv_ref[...] = dv_acc[prv].astype(BF16)

        for parity in (0, 1):
            on = (i % 2) == parity
            pl.when(on & (i < nb))(functools.partial(compute, parity, 1 - parity))
            pl.when(on & (i > 0))(functools.partial(flush, 1 - parity))
        if hosted:
            pl.when((j == n_j - 1) & (i == nb))(finish)

    qi = lambda i: jnp.minimum(i, nb - 1)
    cur_w = lambda w: pl.BlockSpec((None, RR, RB * BAND, GROUP_W), lambda j, i: (w, j, qi(i), 0))
    prev_w = lambda w: pl.BlockSpec((None, RR, BAND, GROUP_W),
                                    lambda j, i: (w, j, jnp.maximum(qi(i) * RB - 1, 0), 0))
    blk = pl.BlockSpec((RR, RB * BAND, GROUP_W), lambda j, i: (j, qi(i), 0))
    stat_blk = pl.BlockSpec((RR, RB * BAND, STAT_W), lambda j, i: (j, qi(i), 0))
    late = pl.BlockSpec((RR, RB * BAND, GROUP_W), lambda j, i: (j, jnp.maximum(i - 1, 0), 0))
    extra = [packs] if hosted else []
    return pl.pallas_call(
        body, grid=(n_j, nb + 1), name=f"attn_bwd_g{grp}",
        in_specs=[cur_w(0), cur_w(1), prev_w(1), cur_w(2), prev_w(2), blk, stat_blk, stat_blk] + [ANY] * hosted,
        out_specs=[blk, late, late] + [ANY] * hosted,
        out_shape=[SDS((d, L, GROUP_W), BF16)] * 3 + [SDS(t.shape, t.dtype) for t in extra],
        scratch_shapes=[pltpu.VMEM((2, RR, RB * BAND, GROUP_W), F32), pltpu.VMEM((2, RR, RB * BAND, GROUP_W), F32)]
        + [pltpu.SemaphoreType.DMA((N_DEV - 1,))] * (2 * hosted),
        input_output_aliases={8: 3} if hosted else {},
        compiler_params=_cp("arbitrary" if hosted else "parallel", "arbitrary"),
    )(qkv, qkv, qkv, qkv, qkv, da, lt, dd, *extra)


def _dz_assemble(dqkv, dqp):
    S = dqp.shape[0]
    n_tiles = S // TMB

    def body(*refs):
        dqkv_refs = refs[0:9]
        dqp_ref, halo_ref = refs[9:11]
        dz_ref, s_ref, ext_ref = refs[11:]
        i = pl.program_id(0)

        for grp in range(3):
            for which in range(3):
                n = which * 3 + grp
                ref = dqkv_refs[grp * 3 + which]
                if DILATIONS[grp] == 1:
                    dz_ref[n] = ref[0]
                else:
                    _interleave_load(ref, (), s_ref, DILATIONS[grp], TMB)
                    for h in range(2):
                        dz_ref[n, :, h * PAIR_W:(h + 1) * PAIR_W] = s_ref[h].astype(BF16)

        dqp = dqp_ref[...].astype(F32)
        ext_ref[0:TMB, :] = dqp
        ext_ref[TMB:, :] = jnp.where(i < n_tiles - 1, halo_ref[...].astype(F32), 0.0)
        sums = []
        acc = ext_ref[...]
        for k in (1, 2, 4, 8):
            acc = acc + pltpu.roll(acc, TMB + POOL_HALO - k, 0)
            sums.append(acc[0:TMB, :])
        inv_cnt, col = _pool_inv_count(i, TMB)
        dpz = _pool_column_select(col, sums) - dqp / inv_cnt
        for t in range(3):
            dz_ref[9 + t] = dpz[:, t * CHUNK:(t + 1) * CHUNK].astype(BF16)

    row = lambda w: pl.BlockSpec((TMB, w), lambda i: (i, 0))
    grp_spec = lambda d: pl.BlockSpec((d, TMB // d, GROUP_W), lambda i: (0, i, 0))
    halo = pl.BlockSpec((POOL_HALO, POOL_W),
                        lambda i: (jnp.minimum((i + 1) * (TMB // POOL_HALO), S // POOL_HALO - 1), 0))
    flat = [t for grp in range(3) for t in dqkv[grp]]
    return pl.pallas_call(
        body, grid=(n_tiles,), name="dz_assemble",
        in_specs=[grp_spec(DILATIONS[grp]) for grp in range(3) for _ in range(3)] + [row(POOL_W), halo],
        out_specs=pl.BlockSpec((N_DZ_CHUNKS, TMB, CHUNK), lambda i: (0, i, 0)),
        out_shape=SDS((N_DZ_CHUNKS, S, CHUNK), BF16),
        scratch_shapes=[pltpu.VMEM((2, TMB, PAIR_W), F32), pltpu.VMEM((TMB + POOL_HALO, POOL_W), F32)],
        compiler_params=_cp("parallel"),
    )(*flat, dqp, dqp)


def _inproj_dx(dz, dgates, dh1, x, g, w_in, sums):
    S = x.shape[0]
    n_tiles = S // TM
    n = len(sums)

    def body(*refs):
        dz_ref, dgate_ref, dh1_ref, x_ref, g_ref, w_ref = refs[0:6]
        sum_refs = refs[6:6 + n]
        dx_ref, dg_ref = refs[6 + n:8 + n]
        land_refs = refs[8 + n:8 + 2 * n]
        sems = refs[8 + 2 * n:]
        i = pl.program_id(0)

        def copies():
            return _chip_sum_copies(sum_refs, land_refs, *sems)

        @pl.when(i == 0)
        def _():
            dg_ref[...] = jnp.zeros_like(dg_ref)
            for cpy in copies():
                cpy.start()

        du = jnp.zeros((TM, D_MODEL), F32)
        for k in range(N_CHUNKS):
            dzk = dz_ref[k] if k < N_DZ_CHUNKS else dgate_ref[k - N_DZ_CHUNKS]
            du = du + _dot_nt(dzk, _w_in_chunk(w_ref, k))
        gv = g_ref[...]
        _, xh, r = _rms_fwd(x_ref[...], gv)
        dg_ref[...] += jnp.sum(du * xh, axis=0, keepdims=True)
        dx_ref[...] = dh1_ref[...] + _rms_bwd(du, xh, r, gv)

        @pl.when(i == n_tiles - 1)
        def _():
            for cpy in copies():
                cpy.wait()

    row = lambda w: pl.BlockSpec((TM, w), lambda i: (i, 0))
    res = pl.pallas_call(
        body, grid=(n_tiles,), name="inproj_dx",
        in_specs=[pl.BlockSpec((N_DZ_CHUNKS, TM, CHUNK), lambda i: (0, i, 0)),
                  pl.BlockSpec((N_CHUNKS - N_DZ_CHUNKS, TM, CHUNK), lambda i: (0, i, 0)),
                  row(D_MODEL), row(D_MODEL), _resident(g.shape), _resident(w_in.shape)] + [ANY] * n,
        out_specs=[row(D_MODEL), pl.BlockSpec((1, D_MODEL), lambda i: (0, 0))] + [ANY] * n,
        out_shape=[SDS((S, D_MODEL), F32), SDS((1, D_MODEL), F32)] + [SDS(t.shape, t.dtype) for t in sums],
        scratch_shapes=[pltpu.SemaphoreType.DMA((3 * n,)), pltpu.SemaphoreType.DMA((3 * n,))],
        compiler_params=_cp("arbitrary"),
    )(dz, dgates, dh1, x, g, w_in, *sums)
    return res[0], res[1], res[2:]


def _wgrad(a, b, name, *, out_shape, a_spec, b_spec, out_spec, grid, n_out_cols=None, fill=None, narrow=True):
    k_axis = len(grid) - 1
    n_k = grid[k_axis]
    n_out = 2 if narrow else 1

    def body(a_ref, b_ref, *rest):
        o_ref = rest[-n_out]

        @pl.when(pl.program_id(k_axis) == 0)
        def _():
            o_ref[...] = jnp.zeros_like(o_ref)

        at = a_ref[...]
        if n_out_cols is None:
            o_ref[...] += _dot_tn(at, b_ref[...])
        elif n_out_cols[0] == "lead_both":
            for t in range(b_ref.shape[0]):
                o_ref[t] += _dot_tn(at, b_ref[t])
        else:
            w = n_out_cols[1]
            for t in range(o_ref.shape[0]):
                o_ref[t] += _dot_tn(at, b_ref[:, t * w:(t + 1) * w])

        if narrow:
            @pl.when(pl.program_id(k_axis) == n_k - 1)
            def _():
                rest[-1][...] = o_ref[...].astype(BF16)

    sem = ("parallel",) * k_axis + ("arbitrary",)
    extra = [] if fill is None else list(fill) if narrow else [fill]
    shapes = [out_shape, SDS(out_shape.shape, BF16)] if narrow else out_shape
    return pl.pallas_call(body, grid=grid, name=name, in_specs=[a_spec, b_spec] + [ANY] * len(extra),
                          out_specs=[out_spec] * n_out if narrow else out_spec, out_shape=shapes,
                          input_output_aliases={2 + t: t for t in range(len(extra))},
                          compiler_params=_cp(*sem))(a, b, *extra)


def _wgrad_in(u, dz, dgates):
    bk = min(BK, u.shape[0])
    nk = u.shape[0] // bk
    g = WGRAD_IN_GROUP
    kw = dict(n_out_cols=("lead_both", CHUNK), a_spec=pl.BlockSpec((bk, D_MODEL), lambda j, k: (k, 0)),
              b_spec=pl.BlockSpec((g, bk, CHUNK), lambda j, k: (j, k, 0)),
              out_shape=SDS((N_CHUNKS, D_MODEL, CHUNK), F32))
    first = _wgrad(u, dz, "wgrad_in_qkvp", grid=(N_DZ_CHUNKS // g, nk),
                   out_spec=pl.BlockSpec((g, D_MODEL, CHUNK), lambda j, k: (j, 0, 0)), **kw)
    both = _wgrad(u, dgates, "wgrad_in_gates", grid=((N_CHUNKS - N_DZ_CHUNKS) // g, nk), fill=first,
                  out_spec=pl.BlockSpec((g, D_MODEL, CHUNK), lambda j, k: (N_DZ_CHUNKS // g + j, 0, 0)), **kw)
    return [t.reshape(N_CHIPS, CHUNKS_PER_SHARD * D_MODEL, CHUNK) for t in both]


def _wgrads_mixer(a, da1, p, dp1, merged, dh1b, pooled, dmixed):
    bk = min(BK, a.shape[0])
    nk = a.shape[0] // bk
    g_ao = _wgrad(
        a, da1, "wgrad_att_out", grid=(nk,), n_out_cols=("cols_b", CHUNK),
        a_spec=pl.BlockSpec((bk, GROUP_W), lambda k: (k, 0)),
        b_spec=pl.BlockSpec((bk, D_MODEL), lambda k: (k, 0)),
        out_spec=pl.BlockSpec((N_CHIPS, GROUP_W, CHUNK), lambda k: (0, 0, 0)),
        out_shape=SDS((N_CHIPS, GROUP_W, CHUNK), F32))
    g_po = _wgrad(
        p, dp1, "wgrad_pool_out", grid=(nk,), n_out_cols=("cols_b", CHUNK),
        a_spec=pl.BlockSpec((bk, POOL_W), lambda k: (k, 0)),
        b_spec=pl.BlockSpec((bk, D_MODEL), lambda k: (k, 0)),
        out_spec=pl.BlockSpec((N_CHIPS, POOL_W, CHUNK), lambda k: (0, 0, 0)),
        out_shape=SDS((N_CHIPS, POOL_W, CHUNK), F32))
    g_out = _wgrad(
        merged, dh1b, "wgrad_out", grid=(nk,),
        a_spec=pl.BlockSpec((bk, D_MODEL), lambda k: (k, 0)),
        b_spec=pl.BlockSpec((bk, D_MODEL), lambda k: (k, 0)),
        out_spec=pl.BlockSpec((D_MODEL, D_MODEL), lambda k: (0, 0)),
        out_shape=SDS((D_MODEL, D_MODEL), F32))
    g_bd = _wgrad(
        pooled, dmixed, "wgrad_pool_grp", grid=(nk,),
        a_spec=pl.BlockSpec((bk, POOL_W), lambda k: (k, 0)),
        b_spec=pl.BlockSpec((bk, POOL_W), lambda k: (k, 0)),
        out_spec=pl.BlockSpec((POOL_W, POOL_W), lambda k: (0, 0)),
        out_shape=SDS((POOL_W, POOL_W), F32), narrow=False)
    g_out = [t.reshape(N_CHIPS, D_MODEL // N_CHIPS, D_MODEL) for t in g_out]
    return [g_ao, g_po, g_out], g_bd


def _wgrads_mlp(m, dpre, hid, dh2b):
    bk = min(BK, m.shape[0])
    nk = m.shape[0] // bk
    g_mi = _wgrad(
        m, dpre, "wgrad_mlp_in", grid=(N_CHIPS, nk),
        a_spec=pl.BlockSpec((bk, D_MODEL), lambda c, k: (k, 0)),
        b_spec=pl.BlockSpec((bk, D_MODEL), lambda c, k: (k, c)),
        out_spec=pl.BlockSpec((None, D_MODEL, D_MODEL), lambda c, k: (c, 0, 0)),
        out_shape=SDS((N_CHIPS, D_MODEL, D_MODEL), F32))
    g_mo = _wgrad(
        hid, dh2b, "wgrad_mlp_out", grid=(N_CHIPS, nk),
        a_spec=pl.BlockSpec((bk, D_MODEL), lambda c, k: (k, c)),
        b_spec=pl.BlockSpec((bk, D_MODEL), lambda c, k: (k, 0)),
        out_spec=pl.BlockSpec((None, D_MODEL, D_MODEL), lambda c, k: (c, 0, 0)),
        out_shape=SDS((N_CHIPS, D_MODEL, D_MODEL), F32))
    return [g_mi, g_mo]


def _mesh_place():
    x, y, c = lax.axis_index("x"), lax.axis_index("y"), lax.axis_index("c")
    other_chips = [(x, 1 - y), (1 - x, y), (1 - x, 1 - y)]
    return x, y, c, other_chips


ANY = pl.BlockSpec(memory_space=pl.ANY)


def _weight_half_copies(shard_refs, buf_refs, rows, send_sem, recv_sem):
    x, y, c, chips = _mesh_place()
    me = 2 * x + y
    copies = []
    for w, r_full in enumerate(rows):
        rh = r_full // 2
        for r, (px, py) in enumerate(chips):
            k = w * 3 + r
            copies.append(pltpu.make_async_remote_copy(
                src_ref=shard_refs[w].at[pl.ds(c * rh, rh), :], dst_ref=buf_refs[w].at[me, pl.ds(c * rh, rh), :],
                send_sem=send_sem.at[k], recv_sem=recv_sem.at[k], device_id=(px, py, c), device_id_type=MESH))
    return copies


def _pair_forward_copies(buf_refs, rows, send_sem, recv_sem):
    x, y, c, chips = _mesh_place()
    out = []
    for w, r_full in enumerate(rows):
        rh = r_full // 2
        for r, (px, py) in enumerate(chips):
            k = w * 3 + r
            landed = buf_refs[w].at[2 * px + py, pl.ds(c * rh, rh), :]
            theirs = buf_refs[w].at[2 * px + py, pl.ds((1 - c) * rh, rh), :]
            mk = lambda ref: pltpu.make_async_remote_copy(
                src_ref=ref, dst_ref=ref, send_sem=send_sem.at[k], recv_sem=recv_sem.at[k],
                device_id=(x, y, 1 - c), device_id_type=MESH)
            out.append((mk(landed), mk(theirs)))
    return out


def _place_own(block, n_slots, slot):
    buf = lax.empty((n_slots,) + block.shape, block.dtype)
    return lax.dynamic_update_slice(buf, block[None], (slot,) + (0,) * block.ndim)


def _pair_forward(bufs, rows, name):
    n = len(bufs)

    def body(*refs):
        dst = refs[n:2 * n]
        send_sem, recv_sem = refs[2 * n:]
        fwds = _pair_forward_copies(dst, rows, send_sem, recv_sem)
        for fwd, _ in fwds:
            fwd.start()
        for fwd, landing in fwds:
            landing.wait_recv()
            fwd.wait_send()

    return pl.pallas_call(
        body, name=name,
        in_specs=[ANY] * n, out_specs=[ANY] * n,
        out_shape=[SDS(b.shape, b.dtype) for b in bufs],
        scratch_shapes=[pltpu.SemaphoreType.DMA((3 * n,))] * 2,
        input_output_aliases={w: w for w in range(n)},
    )(*bufs)


def _chip_sum_copies(src, dst, send_sem, recv_sem):
    x, y, c, chips = _mesh_place()
    copies = []
    for w in range(len(src)):
        for r, (px, py) in enumerate(chips):
            k = w * 3 + r
            copies.append(pltpu.make_async_remote_copy(
                src_ref=src[w].at[r + 1], dst_ref=dst[w].at[r + 1], send_sem=send_sem.at[k], recv_sem=recv_sem.at[k],
                device_id=(px, py, c), device_id_type=MESH))
    return copies


def _pair_exchange(grads):
    n = len(grads)

    def body(*refs):
        src, dst = refs[:n], refs[n:2 * n]
        send_sem, recv_sem = refs[2 * n:]
        x, y, c, _ = _mesh_place()
        copies = []
        for w in range(n):
            rh = grads[w].shape[1] // 2
            copies.append(pltpu.make_async_remote_copy(
                src_ref=src[w].at[:, pl.ds((1 - c) * rh, rh), :], dst_ref=dst[w],
                send_sem=send_sem.at[w], recv_sem=recv_sem.at[w],
                device_id=(x, y, 1 - c), device_id_type=MESH))
            copies[-1].start()
        for cpy in copies:
            cpy.wait()

    return pl.pallas_call(
        body, name="grad_pair_exchange",
        in_specs=[ANY] * n, out_specs=[ANY] * n,
        out_shape=[SDS((N_CHIPS, g.shape[1] // 2, g.shape[2]), g.dtype) for g in grads],
        scratch_shapes=[pltpu.SemaphoreType.DMA((n,)), pltpu.SemaphoreType.DMA((n,))],
    )(*grads)


def _pair_sum(place, grad, recv, name):
    _, R, C = grad.shape
    rh = R // 2
    br = _row_block(rh, max(256, ELEMENTWISE_BLOCK // C))
    nbh = rh // br

    def body(place_ref, g_ref, r_ref, own_ref, sums_ref):
        s = g_ref[...] + r_ref[...].astype(F32)

        @pl.when(pl.program_id(1) == 0)
        def _():
            own_ref[...] = s

        sums_ref[...] = s.astype(BF16)

    slot = lambda rel, pr: jnp.bitwise_xor(pr[0], rel)
    return pl.pallas_call(
        body, name=name,
        grid_spec=pltpu.PrefetchScalarGridSpec(
            num_scalar_prefetch=1, grid=(nbh, N_CHIPS),
            in_specs=[pl.BlockSpec((None, br, C), lambda i, rel, pr: (slot(rel, pr), pr[1] * nbh + i, 0)),
                      pl.BlockSpec((None, br, C), lambda i, rel, pr: (slot(rel, pr), i, 0))],
            out_specs=[pl.BlockSpec((br, C), lambda i, rel, pr: (i, 0)),
                       pl.BlockSpec((None, br, C), lambda i, rel, pr: (rel, i, 0))]),
        out_shape=[SDS((rh, C), F32), SDS((N_CHIPS, rh, C), BF16)],
        compiler_params=_cp("parallel", "arbitrary"),
    )(place, grad, recv)


def _chip_sum(place, own, recv, name):
    rh, C = own.shape
    br = _row_block(rh, max(256, ELEMENTWISE_BLOCK // C))
    nbh = rh // br

    def body(place_ref, own_ref, r_ref, o_ref):
        o_ref[...] = ((own_ref[...] + r_ref[1].astype(F32)) + r_ref[2].astype(F32)) + r_ref[3].astype(F32)

    return pl.pallas_call(
        body, name=name,
        grid_spec=pltpu.PrefetchScalarGridSpec(
            num_scalar_prefetch=1, grid=(nbh,),
            in_specs=[pl.BlockSpec((br, C), lambda i, pr: (i, 0)),
                      pl.BlockSpec((N_CHIPS, br, C), lambda i, pr: (0, i, 0))],
            out_specs=pl.BlockSpec((br, C), lambda i, pr: (pr[1] * nbh + i, 0))),
        out_shape=SDS((2 * rh, C), F32),
        compiler_params=_cp("parallel"),
    )(place, own, recv)


def _pack_allgather(all_ref, send_sem, recv_sem):
    x, y, c, chips = _mesh_place()
    sib = (x, y, 1 - c)

    def pack(dev, k, to):
        slot = 4 * dev[0] + 2 * dev[1] + dev[2]
        return pltpu.make_async_remote_copy(
            src_ref=all_ref.at[slot], dst_ref=all_ref.at[slot], send_sem=send_sem.at[k],
            recv_sem=recv_sem.at[k], device_id=to, device_id_type=MESH)

    first = [pack((x, y, c), 0, sib)] + [pack((x, y, c), 1 + r, (px, py, c)) for r, (px, py) in enumerate(chips)]
    relays = [pack((px, py, c), 4 + r, sib) for r, (px, py) in enumerate(chips)]

    def start():
        for cpy in first:
            cpy.start()

    def relay():
        for r, (px, py) in enumerate(chips):
            pack((px, py, c), 1 + r, (px, py, c)).wait_recv()
            relays[r].start()

    def finish():
        pack(sib, 0, sib).wait_recv()
        for r, (px, py) in enumerate(chips):
            pack((px, py, 1 - c), 4 + r, sib).wait_recv()
        for cpy in first + relays:
            cpy.wait_send()

    return start, relay, finish


def _finish_exchange(grads, late_all):
    n = len(grads)

    def body(*refs):
        dst, all_ref = refs[n + 1:2 * n + 1], refs[2 * n + 1]
        send_sem, recv_sem, ssend_sem, srecv_sem = refs[2 * n + 2:]
        x, y, c, _ = _mesh_place()
        start, relay, finish = _pack_allgather(all_ref, ssend_sem, srecv_sem)
        start()
        sends, landings = [], []
        for w in range(n):
            rh = grads[w].shape[0] // 2
            mk = lambda cc: pltpu.make_async_remote_copy(
                src_ref=dst[w].at[pl.ds(cc * rh, rh), :], dst_ref=dst[w].at[pl.ds(cc * rh, rh), :],
                send_sem=send_sem.at[w], recv_sem=recv_sem.at[w], device_id=(x, y, 1 - c), device_id_type=MESH)
            sends.append(mk(c))
            landings.append(mk(1 - c))
            sends[-1].start()
        relay()
        finish()
        for cpy in landings:
            cpy.wait_recv()
        for cpy in sends:
            cpy.wait_send()

    res = pl.pallas_call(
        body, name="grad_finish_exchange",
        in_specs=[ANY] * (n + 1), out_specs=[ANY] * (n + 1),
        out_shape=[SDS(g.shape, g.dtype) for g in grads] + [SDS(late_all.shape, late_all.dtype)],
        scratch_shapes=[pltpu.SemaphoreType.DMA((n,)), pltpu.SemaphoreType.DMA((n,)),
                        pltpu.SemaphoreType.DMA((N_DEV - 1,)), pltpu.SemaphoreType.DMA((N_DEV - 1,))],
        input_output_aliases={w: w for w in range(n + 1)},
    )(*grads, late_all)
    return res[:n], res[n]


def _adamw_math(w, g, m, v):
    m = ADAM_B1 * m + (1.0 - ADAM_B1) * g
    v = ADAM_B2 * v + (1.0 - ADAM_B2) * jnp.square(g)
    m_hat = m / (1.0 - ADAM_B1 ** ADAM_STEP)
    v_hat = v / (1.0 - ADAM_B2 ** ADAM_STEP)
    delta = -ADAM_LR * (m_hat / (jnp.sqrt(v_hat) + ADAM_EPS) + ADAM_WD * w)
    return delta, m, v


def _adamw(w, g, m, v, name):
    R, C = w.shape
    br = _row_block(R, 512)
    if g.ndim == 3:
        n_chunks, cw = g.shape[0], g.shape[2]
        g_spec = pl.BlockSpec((None, br, cw), lambda t, i: (t, i, 0))
    else:
        n_chunks, cw = 1, C
        g_spec = pl.BlockSpec((br, cw), lambda t, i: (i, t))

    def body(w_ref, g_ref, m_ref, v_ref, g_out_ref, d_ref, nm_ref, nv_ref):
        gv = g_ref[...]
        g_out_ref[...] = gv
        d_ref[...], nm_ref[...], nv_ref[...] = _adamw_math(w_ref[...], gv, m_ref[...], v_ref[...])

    spec = pl.BlockSpec((br, cw), lambda t, i: (i, t))
    return pl.pallas_call(
        body, grid=(n_chunks, R // br), name=name, in_specs=[spec, g_spec, spec, spec], out_specs=[spec] * 4,
        out_shape=[SDS((R, C), F32)] * 4, compiler_params=_cp("parallel", "parallel"),
    )(w, g, m, v)


def _small_sum_adamw(all_small, all_late, w, m, v):
    loss_row = PACK_ROWS - 8

    def body(all_ref, late_ref, w_ref, m_ref, v_ref, g_ref, d_ref, nm_ref, nv_ref, loss_ref):
        g = all_ref[0]
        late = late_ref[0]
        for k in range(1, N_DEV):
            g = g + all_ref[k]
            late = late + late_ref[k]
        g_ref[...] = g
        g_ref[PACK_LATE_ROW:PACK_LATE_ROW + 8, :] = late
        g = g_ref[...]
        d_ref[...], nm_ref[...], nv_ref[...] = _adamw_math(w_ref[...], g, m_ref[...], v_ref[...])
        total = jnp.sum(g[loss_row:loss_row + 1, :]) * (0.5 / D_MODEL)
        loss_ref[...] = jnp.full(loss_ref.shape, total, F32)

    full = lambda s: pl.BlockSpec(s, lambda i: (0,) * len(s))
    pack = (PACK_ROWS, D_MODEL)
    return pl.pallas_call(
        body, grid=(1,), name="small_sum_adamw",
        in_specs=[full((N_DEV,) + pack), full((N_DEV, 8, D_MODEL)), full(pack), full(pack), full(pack)],
        out_specs=[full(pack)] * 4 + [full((8, 128))],
        out_shape=[SDS(pack, F32)] * 4 + [SDS((8, 128), F32)],
        compiler_params=_cp("arbitrary"),
    )(all_small, all_late, w, m, v)


def _pack_small(grp, scale, g_mix, g_mlp, g_f, loss_lanes):
    def part(vec):
        vec = vec.reshape(1, -1)
        return jnp.pad(vec, ((0, 7), (0, D_MODEL - vec.shape[1])))
    return jnp.concatenate([grp.reshape(-1, D_MODEL), part(scale), part(g_mix), part(g_mlp), part(g_f),
                            part(loss_lanes)], axis=0)


def _unpack_small(pack):
    n_grp = len(POOL_WINDOWS) * POOL_GROUP_W * POOL_GROUP_W // D_MODEL
    grp = pack[:n_grp].reshape(1, len(POOL_WINDOWS), POOL_GROUP_W, POOL_GROUP_W)
    scale = pack[n_grp, :POOL_W].reshape(1, POOL_W)
    g_mix = pack[n_grp + 8].reshape(1, D_MODEL)
    g_mlp = pack[n_grp + 16].reshape(1, D_MODEL)
    g_f = pack[n_grp + 24].reshape(D_MODEL)
    return grp, scale, g_mix, g_mlp, g_f


def _block_diag(grp):
    out = jnp.zeros((POOL_W, POOL_W), grp.dtype)
    for k in range(len(POOL_WINDOWS)):
        out = lax.dynamic_update_slice(out, grp[k], (k * POOL_GROUP_W, k * POOL_GROUP_W))
    return out


def kernel(x, norm_mix_g, w_in, w_att_out, w_pool_grp, pool_scale, w_pool_out, w_out, norm_mlp_g, w_mlp_in, w_mlp_out, norm_final_g, loss_target, m_norm_mix_g, m_w_in, m_w_att_out, m_w_pool_grp, m_pool_scale, m_w_pool_out, m_w_out, m_norm_mlp_g, m_w_mlp_in, m_w_mlp_out, m_norm_final_g, v_norm_mix_g, v_w_in, v_w_att_out, v_w_pool_grp, v_pool_scale, v_w_pool_out, v_w_out, v_norm_mlp_g, v_w_mlp_in, v_w_mlp_out, v_norm_final_g):
    S = x.shape[1]
    xs, target = x[0], loss_target[0]
    big = [w_in[0], w_att_out[0], w_pool_out[0], w_out[0], w_mlp_in[0], w_mlp_out[0]]
    big_m = [m_w_in[0], m_w_att_out[0], m_w_pool_out[0], m_w_out[0], m_w_mlp_in[0], m_w_mlp_out[0]]
    big_v = [v_w_in[0], v_w_att_out[0], v_w_pool_out[0], v_w_out[0], v_w_mlp_in[0], v_w_mlp_out[0]]

    chip = 2 * lax.axis_index("x") + lax.axis_index("y")
    core = lax.axis_index("c")
    place = jnp.stack([chip, core]).astype(jnp.int32)
    names = ("w_in", "w_att_out", "w_pool_out", "w_out", "w_mlp_in", "w_mlp_out")

    shards = [w.astype(BF16) for w in big]
    bufs = [_place_own(sh, N_CHIPS, chip) for sh in shards]
    wbd = _block_diag(w_pool_grp[0]).astype(BF16)
    g_final = norm_final_g.reshape(1, D_MODEL)
    stat_expand, stat_reduce = _stat_matrices()

    u, z_own, landed_in = _norm_inproj_own(xs, norm_mix_g, shards[0], bufs[0])
    (wg_in,) = _pair_forward([landed_in], [shards[0].shape[0]], "w_in_pair_forward")
    (qkv0, qkv1, qkv2, pz, gates), (wg_ao, wg_po, wg_out) = _inproj_rest(u, z_own, wg_in, shards[1:4], bufs[1:4])
    wg_out = wg_out.reshape(D_MODEL, D_MODEL)
    qkv = (qkv0, qkv1, qkv2)
    att = [_attn_fwd(qkv[grp], grp) for grp in range(3)]
    (a, lt0, lt1, lt2, pooled, mixed, p, merged, h1, m), (wg_mi, wg_mo) = _mixer_out(
        [o for o, _ in att], [l for _, l in att], pz, gates, xs, wg_ao, wg_po, wbd, pool_scale, wg_out, norm_mlp_g,
        stat_expand, shards[4:], bufs[4:])
    hid, dh2, dh2b, loss_lanes, dg_final = _mlp_fwd_loss(m, h1, target, wg_mi, wg_mo, g_final)

    def pair_reduce(grads, grad_names):
        recv = _pair_exchange([narrow for _, narrow in grads])
        pair = [_pair_sum(place, g, r, f"pair_sum_{nm}") for (g, _), r, nm in zip(grads, recv, grad_names)]
        return [own for own, _ in pair], [s for _, s in pair]

    def chip_reduce(owns, landed_sums, grad_names):
        return [_chip_sum(place, own, r, f"chip_sum_{nm}") for own, r, nm in zip(owns, landed_sums, grad_names)]

    dpre, dh1, dh1b, dg_mlp = _mlp_bwd(dh2, dh2b, hid, h1, wg_mi, wg_mo, norm_mlp_g)
    own_mlp, sums_mlp = pair_reduce(_wgrads_mlp(m, dpre, hid, dh2b), names[4:])
    (da1, dp1, dgates, da0, dag1, dag2, dd0, dd1, dd2, dmixed, dqp, dscale), landed_mlp = _mixer_bwd(
        dh1b, a, p, mixed, gates, wg_out, wg_ao, wg_po, wbd, pool_scale, stat_reduce, sums_mlp)
    g_mi, g_mo = chip_reduce(own_mlp, landed_mlp, names[4:])
    grads_mixer, g_bd = _wgrads_mixer(a, da1, p, dp1, merged, dh1b, pooled, dmixed)

    zero = jnp.zeros((D_MODEL,), F32)
    g_grp = jnp.stack([g_bd[k * POOL_GROUP_W:(k + 1) * POOL_GROUP_W, k * POOL_GROUP_W:(k + 1) * POOL_GROUP_W]
                       for k in range(len(POOL_WINDOWS))])
    small = _pack_small(g_grp, dscale, zero, dg_mlp, dg_final, loss_lanes)
    *dqkv0, small_all = _attn_bwd(qkv[0], da0, lt0, dd0, 0, packs=_place_own(small, N_DEV, 2 * chip + core))
    dqkv = [dqkv0, _attn_bwd(qkv[1], dag1, lt1, dd1, 1), _attn_bwd(qkv[2], dag2, lt2, dd2, 2)]
    dz = _dz_assemble(dqkv, dqp)
    own_in, sums_in = pair_reduce([_wgrad_in(u, dz, dgates)] + grads_mixer, names[:4])
    dx, dg_mix, landed_in = _inproj_dx(dz, dgates, dh1, xs, norm_mix_g, wg_in, sums_in)
    g_in, g_ao, g_po, g_out = chip_reduce(own_in, landed_in, names[:4])
    late = jnp.pad(dg_mix, ((0, 7), (0, 0)))
    full, late_all = _finish_exchange([g_in, g_ao, g_po, g_out, g_mi, g_mo], _place_own(late, N_DEV, 2 * chip + core))

    small_w = _pack_small(w_pool_grp[0], pool_scale, norm_mix_g, norm_mlp_g, norm_final_g, zero)
    small_m = _pack_small(m_w_pool_grp[0], m_pool_scale, m_norm_mix_g, m_norm_mlp_g, m_norm_final_g, zero)
    small_v = _pack_small(v_w_pool_grp[0], v_pool_scale, v_norm_mix_g, v_norm_mlp_g, v_norm_final_g, zero)
    sg, sd, sm, sv, loss_tile = _small_sum_adamw(small_all, late_all, small_w, small_m, small_v)
    full = [full[0].reshape(CHUNKS_PER_SHARD, D_MODEL, CHUNK)] + list(full[1:])
    upd = [_adamw(w, g, mm, vv, f"adamw_{nm}") for w, g, mm, vv, nm in zip(big, full, big_m, big_v, names)]

    def ordered(small_pack, bigs):
        grp, scale, g_mix, g_mlp, g_f = _unpack_small(small_pack)
        b_in, b_ao, b_po, b_out, b_mi, b_mo = [b[None] for b in bigs]
        return (g_mix, b_in, b_ao, grp, scale, b_po, b_out, g_mlp, b_mi, b_mo, g_f)

    return (loss_tile[0, 0], dx[None],
            *ordered(sg, [t[0] for t in upd]),
            *ordered(sd, [t[1] for t in upd]),
            *ordered(sm, [t[2] for t in upd]),
            *ordered(sv, [t[3] for t in upd]))
```

```python
import functools

import jax
import jax.numpy as jnp
from jax import lax
from jax.experimental import pallas as pl
from jax.experimental.pallas import tpu as pltpu

F32 = jnp.float32
BF16 = jnp.bfloat16
SDS = jax.ShapeDtypeStruct
MESH = pl.DeviceIdType.MESH

D_MODEL = 1024
D_FF = 4096
N_CHIPS = 4
N_DEV = 8
DILATIONS = (1, 4, 16)
BAND = 128
GROUP_W = 256
PAIR_W = 128
HEAD_W = 64
STAT_W = 128
STAT_HEAD_W = 32
POOL_W = 768
POOL_GROUP_W = 192
POOL_WINDOWS = (2, 4, 8, 16)
POOL_HALO = 16
N_IN = 5120
CHUNK = 256
N_CHUNKS = N_IN // CHUNK
N_DZ_CHUNKS = 12
CHUNKS_PER_SHARD = 5
WGRAD_IN_GROUP = 4
NORM_EPS = 1e-6
ALIBI_MAX_BIAS = 8.0
N_HEADS = 12
NEG = -1e30

ADAM_LR, ADAM_B1, ADAM_B2, ADAM_EPS, ADAM_WD, ADAM_STEP = 0.001, 0.9, 0.999, 1e-08, 0.01, 10

TM = 512
TMB = 512
TMZ = 1024
ATT_TILE = ((1, 16), (4, 4), (16, 1))
BK = 4096
ELEMENTWISE_BLOCK = 1 << 20
VMEM_LIMIT = 56 * 1024 * 1024
PACK_ROWS = 184
PACK_LATE_ROW = 152

NT = (((1,), (1,)), ((), ()))
TN = (((0,), (0,)), ((), ()))


def _cp(*sem):
    return pltpu.CompilerParams(dimension_semantics=sem, vmem_limit_bytes=VMEM_LIMIT)


def _resident(shape):
    nd = len(shape)
    return pl.BlockSpec(shape, lambda *_: (0,) * nd, pipeline_mode=pl.Buffered(1))


def _row_block(rows, cap=256):
    return max(b for b in range(16, min(rows, cap) + 1, 16) if rows % b == 0)


def _dot(a, b):
    return jnp.dot(a, b, preferred_element_type=F32)


def _dot_nt(a, b):
    return lax.dot_general(a, b, NT, preferred_element_type=F32)


def _dot_tn(a, b):
    return lax.dot_general(a, b, TN, preferred_element_type=F32)


def _w_in_chunk(w_ref, n):
    return w_ref[n // CHUNKS_PER_SHARD, :, (n % CHUNKS_PER_SHARD) * CHUNK:(n % CHUNKS_PER_SHARD + 1) * CHUNK]


def _sigmoid(x):
    return 0.5 * jnp.tanh(0.5 * x.astype(F32)) + 0.5


def _rms_fwd(x, g):
    r = lax.rsqrt(jnp.mean(x * x, axis=-1, keepdims=True) + NORM_EPS)
    xh = x * r
    return xh * g, xh, r


def _rms_bwd(dy, xh, r, g):
    dxh = dy * g
    return r * (dxh - xh * jnp.mean(dxh * xh, axis=-1, keepdims=True))


def _per_head_lanes(cols):
    rows = cols[0].shape[0]
    lane = lax.broadcasted_iota(jnp.int32, (rows, STAT_W), 1)
    out = cols[3]
    for h in (2, 1, 0):
        out = jnp.where(lane < (h + 1) * STAT_HEAD_W, cols[h], out)
    return out


def _head_col(stat, h):
    return stat[:, h * STAT_HEAD_W:h * STAT_HEAD_W + 1]


def _stat_matrices():
    s = lax.broadcasted_iota(jnp.int32, (STAT_W, GROUP_W), 0)
    c = lax.broadcasted_iota(jnp.int32, (STAT_W, GROUP_W), 1)
    expand = (s == (c // HEAD_W) * STAT_HEAD_W).astype(BF16)
    reduce = (s // STAT_HEAD_W == c // HEAD_W).astype(BF16).T
    return expand, reduce


def _dot_split(x, m):
    hi = x.astype(BF16)
    lo = (x - hi.astype(F32)).astype(BF16)
    return _dot(hi, m) + _dot(lo, m)


def _deinterleave_store(val, s_ref, out_ref, lead, d, rows, dtype):
    if d == 1:
        out_ref[lead + (0,)] = val.astype(dtype)
        return
    for h in range(2):
        s_ref[h] = val[:, h * PAIR_W:(h + 1) * PAIR_W]
    for r in range(d):
        for h in range(2):
            out_ref[lead + (r, slice(None), slice(h * PAIR_W, (h + 1) * PAIR_W))] = (
                s_ref[h, pl.ds(r, rows // d, stride=d), :].astype(dtype))


def _interleave_load(in_ref, lead, s_ref, d, rows):
    for r in range(d):
        for h in range(2):
            s_ref[h, pl.ds(r, rows // d, stride=d), :] = (
                in_ref[lead + (r, slice(None), slice(h * PAIR_W, (h + 1) * PAIR_W))].astype(F32))


def _norm_inproj_own(x, g, w_own, buf):
    S = x.shape[0]
    tm = min(TMZ, S)
    n_tiles = S // tm

    def body(x_ref, g_ref, w_ref, shard_ref, buf_in, u_ref, z_ref, buf_ref, send_sem, recv_sem):
        i = pl.program_id(0)

        def copies():
            return _weight_half_copies([shard_ref], [buf_ref], [w_own.shape[0]], send_sem, recv_sem)

        @pl.when(i == 0)
        def _():
            for cpy in copies():
                cpy.start()

        u = _rms_fwd(x_ref[...], g_ref[...])[0].astype(BF16)
        u_ref[...] = u
        for t in range(CHUNKS_PER_SHARD):
            z_ref[t] = _dot(u, w_ref[:, t * CHUNK:(t + 1) * CHUNK]).astype(BF16)

        @pl.when(i == n_tiles - 1)
        def _():
            for cpy in copies():
                cpy.wait()

    row = lambda w: pl.BlockSpec((tm, w), lambda i: (i, 0))
    return pl.pallas_call(
        body, grid=(n_tiles,), name="norm_inproj_own",
        in_specs=[row(D_MODEL), _resident((1, D_MODEL)), _resident(w_own.shape), ANY, ANY],
        out_specs=[row(D_MODEL), pl.BlockSpec((CHUNKS_PER_SHARD, tm, CHUNK), lambda i: (0, i, 0)), ANY],
        out_shape=[SDS((S, D_MODEL), BF16), SDS((CHUNKS_PER_SHARD, S, CHUNK), BF16), SDS(buf.shape, buf.dtype)],
        scratch_shapes=[pltpu.SemaphoreType.DMA((3,)), pltpu.SemaphoreType.DMA((3,))],
        input_output_aliases={4: 2},
        compiler_params=_cp("arbitrary"),
    )(x, g, w_own, w_own, buf)


def _hosted_allgather(i, n_steps, shard_refs, buf_refs, rows, sems):
    send_sem, recv_sem, fsend_sem, frecv_sem = sems
    ici = lambda: _weight_half_copies(shard_refs, buf_refs, rows, send_sem, recv_sem)
    forward = lambda: _pair_forward_copies(buf_refs, rows, fsend_sem, frecv_sem)

    def begin():
        @pl.when(i == 0)
        def _():
            for cpy in ici():
                cpy.start()

        @pl.when(i == n_steps // 2)
        def _():
            for cpy, (fwd, _) in zip(ici(), forward()):
                cpy.wait_recv()
                fwd.start()

    def end():
        @pl.when(i == n_steps - 1)
        def _():
            for cpy, (fwd, landing) in zip(ici(), forward()):
                landing.wait_recv()
                fwd.wait_send()
                cpy.wait_send()

    return begin, end


def _inproj_rest(u, z_own, w_in, shards, bufs):
    S = u.shape[0]
    n_tiles = S // TM
    n = len(shards)

    def body(*refs):
        u_ref, zown_ref, w_ref = refs[0:3]
        shard_refs = refs[3:3 + n]
        q0_ref, q1_ref, q2_ref, pz_ref, gate_ref = refs[3 + 2 * n:8 + 2 * n]
        buf_refs = refs[8 + 2 * n:8 + 3 * n]
        s_ref = refs[8 + 3 * n]
        i = pl.program_id(0)
        chip = 2 * lax.axis_index("x") + lax.axis_index("y")
        begin, end = _hosted_allgather(i, n_tiles, shard_refs, buf_refs, [sh.shape[0] for sh in shards],
                                       refs[9 + 3 * n:])
        begin()

        u = u_ref[...]
        qkv_refs = (q0_ref, q1_ref, q2_ref)

        def emit(k, zc):
            if k < 9:
                which, grp = k // 3, k % 3
                if which == 0:
                    zc = zc * 0.125
                _deinterleave_store(zc, s_ref, qkv_refs[grp], (which,), DILATIONS[grp], TM, BF16)
            elif k < N_DZ_CHUNKS:
                pz_ref[:, (k - 9) * CHUNK:(k - 8) * CHUNK] = zc.astype(BF16)
            else:
                gate_ref[:, (k - N_DZ_CHUNKS) * CHUNK:(k - N_DZ_CHUNKS + 1) * CHUNK] = zc.astype(BF16)

        def all_chunks(own_shard):
            for k in range(N_CHUNKS):
                if k // CHUNKS_PER_SHARD == own_shard:
                    emit(k, zown_ref[k % CHUNKS_PER_SHARD].astype(F32))
                else:
                    emit(k, _dot(u, _w_in_chunk(w_ref, k)))

        for shard in range(N_CHIPS):
            pl.when(chip == shard)(functools.partial(all_chunks, shard))
        end()

    row = lambda w: pl.BlockSpec((TM, w), lambda i: (i, 0))
    res = pl.pallas_call(
        body, grid=(n_tiles,), name="inproj_rest",
        in_specs=[row(D_MODEL), pl.BlockSpec((CHUNKS_PER_SHARD, TM, CHUNK), lambda i: (0, i, 0)),
                  _resident(w_in.shape)] + [ANY] * (2 * n),
        out_specs=[pl.BlockSpec((3, d, TM // d, GROUP_W), lambda i: (0, 0, i, 0)) for d in DILATIONS]
        + [row(POOL_W), row(2 * D_MODEL)] + [ANY] * n,
        out_shape=[SDS((3, d, S // d, GROUP_W), BF16) for d in DILATIONS]
        + [SDS((S, POOL_W), BF16), SDS((S, 2 * D_MODEL), BF16)] + [SDS(b.shape, b.dtype) for b in bufs],
        scratch_shapes=[pltpu.VMEM((2, TM, PAIR_W), F32)] + [pltpu.SemaphoreType.DMA((3 * n,))] * 4,
        input_output_aliases={3 + n + w: 5 + w for w in range(n)},
        compiler_params=_cp("arbitrary"),
    )(u, z_own, w_in, *shards, *bufs)
    return res[:5], res[5:]


def _band_bias(grp, d):
    row = lax.broadcasted_iota(jnp.int32, (BAND, 2 * BAND), 0)
    col = lax.broadcasted_iota(jnp.int32, (BAND, 2 * BAND), 1)
    steps = BAND + row - col
    valid = (steps >= 0) & (steps <= BAND)
    stepsf = (steps * d).astype(F32)
    biases = []
    for hh in range(4):
        slope = 2.0 ** (-ALIBI_MAX_BIAS * (grp * 4 + hh + 1) / N_HEADS)
        biases.append(jnp.where(valid, -slope * stepsf, NEG))
    return biases, col


def _attn_tiles(grp, L):
    rr, rb = ATT_TILE[grp]
    rb = min(rb, L // BAND)
    return rr, rb, L // (rb * BAND)


def _kv_tile(cur_ref, prev_ref, rr, rb, cs):
    if rb == 0:
        return jnp.concatenate([prev_ref[rr, :, cs], cur_ref[rr, 0:BAND, cs]], axis=0)
    return cur_ref[rr, (rb - 1) * BAND:(rb + 1) * BAND, cs]


def _attn_fwd(qkv, grp):
    d = DILATIONS[grp]
    L = qkv.shape[2]
    RR, RB, nb = _attn_tiles(grp, L)

    def body(q_ref, kc_ref, kp_ref, vc_ref, vp_ref, o_ref, lse_ref):
        i = pl.program_id(0)
        biases, col = _band_bias(grp, d)
        first_keys_ok = (col >= BAND) | (i > 0)
        is_a = lax.broadcasted_iota(jnp.int32, (BAND, PAIR_W), 1) < HEAD_W
        heads = [(rr, rb, cp, h2) for rr in range(RR) for rb in range(RB) for cp in range(2) for h2 in range(2)]

        def tile(head):
            rr, rb, cp, _ = head
            return rr, rb, slice(rb * BAND, (rb + 1) * BAND), slice(cp * PAIR_W, (cp + 1) * PAIR_W)

        def scores(head):
            rr, rb, rows, cs = tile(head)
            q2 = q_ref[rr, rows, cs]
            b = biases[head[2] * 2 + head[3]]
            if rb == 0:
                b = jnp.where(first_keys_ok, b, NEG)
            sel = is_a if head[3] == 0 else jnp.logical_not(is_a)
            return _dot_nt(jnp.where(sel, q2, jnp.zeros_like(q2)), _kv_tile(kc_ref, kp_ref, rr, rb, cs)) + b

        s_next = scores(heads[0])
        outs, lses = {}, {}
        for idx, head in enumerate(heads):
            s = s_next
            if idx + 1 < len(heads):
                s_next = scores(heads[idx + 1])
            rr, rb, rows, cs = tile(head)
            m = jnp.max(s, axis=-1, keepdims=True)
            p = jnp.exp(s - m)
            l = jnp.sum(p, axis=-1, keepdims=True)
            outs[head[3]] = _dot(p.astype(BF16), _kv_tile(vc_ref, vp_ref, rr, rb, cs)) * (1.0 / l)
            lses[head[2] * 2 + head[3]] = m + jnp.log(l)
            if head[3] == 1:
                o_ref[rr, rows, cs] = jnp.where(is_a, outs[0], outs[1]).astype(BF16)
            if head[2] == 1 and head[3] == 1:
                lse_ref[rr, rows, :] = _per_head_lanes(lses)

    cur = lambda w: pl.BlockSpec((None, RR, RB * BAND, GROUP_W), lambda i, j: (w, j, i, 0))
    prev = lambda w: pl.BlockSpec((None, RR, BAND, GROUP_W), lambda i, j: (w, j, jnp.maximum(i * RB - 1, 0), 0))
    return pl.pallas_call(
        body, grid=(nb, d // RR), name=f"attn_fwd_g{grp}",
        in_specs=[cur(0), cur(1), prev(1), cur(2), prev(2)],
        out_specs=[pl.BlockSpec((RR, RB * BAND, GROUP_W), lambda i, j: (j, i, 0)),
                   pl.BlockSpec((RR, RB * BAND, STAT_W), lambda i, j: (j, i, 0))],
        out_shape=[SDS((d, L, GROUP_W), BF16), SDS((d, L, STAT_W), F32)],
        compiler_params=_cp("parallel", "parallel"),
    )(qkv, qkv, qkv, qkv, qkv)


def _pool_column_select(col, vals):
    return jnp.where(col < POOL_GROUP_W, vals[0],
                     jnp.where(col < 2 * POOL_GROUP_W, vals[1],
                               jnp.where(col < 3 * POOL_GROUP_W, vals[2], vals[3])))


def _pool_inv_count(i, rows):
    t = i * rows + lax.broadcasted_iota(jnp.int32, (rows, POOL_W), 0)
    col = lax.broadcasted_iota(jnp.int32, (rows, POOL_W), 1)
    win = _pool_column_select(col, POOL_WINDOWS)
    return 1.0 / jnp.minimum(t + 1, win).astype(F32), col


def _mixer_out(outs, lses, pz, gates, x, w_ao, w_po, wbd, scale, w_out, g_mlp, expand, shards, bufs):
    S = x.shape[0]
    n_tiles = S // TMB
    n = len(shards)

    def body(*refs):
        (o0_ref, l0_ref, o1_ref, l1_ref, o2_ref, l2_ref, pz_ref, halo_ref, gate_ref, x_ref,
         wao_ref, wpo_ref, wbd_ref, sc_ref, wout_ref, g_ref, expand_ref) = refs[0:17]
        shard_refs = refs[17:17 + n]
        (a_ref, lt0_ref, lt1_ref, lt2_ref, pooled_ref, mixed_ref, p_ref, merged_ref, h1_ref,
         m_ref) = refs[17 + 2 * n:27 + 2 * n]
        buf_refs = refs[27 + 2 * n:27 + 3 * n]
        so1, sl1, so2, sl2, slt, ext_ref = refs[27 + 3 * n:33 + 3 * n]
        i = pl.program_id(0)
        begin, end = _hosted_allgather(i, n_tiles, shard_refs, buf_refs, [sh.shape[0] for sh in shards],
                                       refs[33 + 3 * n:])
        begin()
        _interleave_load(o1_ref, (), so1, DILATIONS[1], TMB)
        _interleave_load(o2_ref, (), so2, DILATIONS[2], TMB)
        for ref, sref, d in ((l1_ref, sl1, DILATIONS[1]), (l2_ref, sl2, DILATIONS[2])):
            for r in range(d):
                sref[0, pl.ds(r, TMB // d, stride=d), :] = ref[r]
        l0, l1, l2 = l0_ref[0], sl1[0], sl2[0]
        mx = jnp.maximum(jnp.maximum(l0, l1), l2)
        e0, e1, e2 = jnp.exp(l0 - mx), jnp.exp(l1 - mx), jnp.exp(l2 - mx)
        den = e0 + e1 + e2
        inv = 1.0 / den
        slt[0] = mx + jnp.log(den)
        w0, w1, w2 = [_dot_split(e * inv, expand_ref[...]) for e in (e0, e1, e2)]
        for h in range(2):
            hs = slice(h * PAIR_W, (h + 1) * PAIR_W)
            a_ref[:, hs] = (w0[:, hs] * o0_ref[0, :, hs].astype(F32) + w1[:, hs] * so1[h]
                            + w2[:, hs] * so2[h]).astype(BF16)
        lt0_ref[0] = slt[0]
        for ref, d in ((lt1_ref, DILATIONS[1]), (lt2_ref, DILATIONS[2])):
            for r in range(d):
                ref[r] = slt[0, pl.ds(r, TMB // d, stride=d), :]

        pz_t = pz_ref[...].astype(F32)
        ext_ref[0:POOL_HALO, :] = jnp.where(i > 0, halo_ref[...].astype(F32), 0.0)
        ext_ref[POOL_HALO:, :] = pz_t
        sums = []
        acc = ext_ref[...]
        for k in (1, 2, 4, 8):
            acc = acc + pltpu.roll(acc, k, 0)
            sums.append(acc[POOL_HALO:, :])
        inv_cnt, col = _pool_inv_count(i, TMB)
        pooled = (_pool_column_select(col, sums) * inv_cnt - pz_t).astype(BF16)
        pooled_ref[...] = pooled
        mixed = _dot(pooled, wbd_ref[...])
        mixed_ref[...] = mixed.astype(BF16)
        p = (mixed * sc_ref[...]).astype(BF16)
        p_ref[...] = p

        a = a_ref[...]
        for j in range(N_CHIPS):
            js = slice(j * CHUNK, (j + 1) * CHUNK)
            ga = gate_ref[:, js]
            gp = gate_ref[:, D_MODEL + j * CHUNK:D_MODEL + (j + 1) * CHUNK]
            mj = _sigmoid(ga) * _dot(a, wao_ref[j]) + _sigmoid(gp) * _dot(p, wpo_ref[j])
            merged_ref[:, js] = mj.astype(BF16)
        h1 = x_ref[...] + _dot(merged_ref[...], wout_ref[...])
        h1_ref[...] = h1
        m_ref[...] = _rms_fwd(h1, g_ref[...])[0].astype(BF16)
        end()

    row = lambda w: pl.BlockSpec((TMB, w), lambda i: (i, 0))
    grp_spec = lambda d: pl.BlockSpec((d, TMB // d, GROUP_W), lambda i: (0, i, 0))
    stat_spec = lambda d: pl.BlockSpec((d, TMB // d, STAT_W), lambda i: (0, i, 0))
    halo = pl.BlockSpec((POOL_HALO, POOL_W), lambda i: (jnp.maximum(i * (TMB // POOL_HALO) - 1, 0), 0))
    d0, d1, d2 = DILATIONS
    pair_scratch = pltpu.VMEM((2, TMB, PAIR_W), F32)
    stat_scratch = pltpu.VMEM((1, TMB, STAT_W), F32)
    res = pl.pallas_call(
        body, grid=(n_tiles,), name="mixer_out",
        in_specs=[grp_spec(d0), stat_spec(d0), grp_spec(d1), stat_spec(d1), grp_spec(d2), stat_spec(d2),
                  row(POOL_W), halo, row(2 * D_MODEL), row(D_MODEL),
                  _resident(w_ao.shape), _resident(w_po.shape), _resident(wbd.shape), _resident(scale.shape),
                  _resident(w_out.shape), _resident(g_mlp.shape), _resident(expand.shape)] + [ANY] * (2 * n),
        out_specs=[row(GROUP_W), stat_spec(d0), stat_spec(d1), stat_spec(d2),
                   row(POOL_W), row(POOL_W), row(POOL_W), row(D_MODEL), row(D_MODEL), row(D_MODEL)] + [ANY] * n,
        out_shape=[SDS((S, GROUP_W), BF16)] + [SDS((d, S // d, STAT_W), F32) for d in DILATIONS]
        + [SDS((S, POOL_W), BF16), SDS((S, POOL_W), BF16), SDS((S, POOL_W), BF16),
           SDS((S, D_MODEL), BF16), SDS((S, D_MODEL), F32), SDS((S, D_MODEL), BF16)]
        + [SDS(b.shape, b.dtype) for b in bufs],
        scratch_shapes=[pair_scratch, stat_scratch, pair_scratch, stat_scratch, stat_scratch,
                        pltpu.VMEM((TMB + POOL_HALO, POOL_W), F32)] + [pltpu.SemaphoreType.DMA((3 * n,))] * 4,
        input_output_aliases={17 + n + w: 10 + w for w in range(n)},
        compiler_params=_cp("arbitrary"),
    )(outs[0], lses[0], outs[1], lses[1], outs[2], lses[2], pz, pz, gates, x,
      w_ao, w_po, wbd, scale, w_out, g_mlp, expand, *shards, *bufs)
    return res[:10], res[10:]


def _mlp_fwd_loss(m, h1, target, w_mi, w_mo, g_f):
    S = m.shape[0]

    def body(m_ref, h1_ref, t_ref, wmi_ref, wmo_ref, g_ref, hid_ref, dh2_ref, dh2b_ref, loss_ref, dg_ref):
        @pl.when(pl.program_id(0) == 0)
        def _():
            loss_ref[...] = jnp.zeros_like(loss_ref)
            dg_ref[...] = jnp.zeros_like(dg_ref)

        mt = m_ref[...]
        acc = h1_ref[...]
        for c in range(N_CHIPS):
            hid = jnp.square(jnp.maximum(_dot(mt, wmi_ref[c]), 0.0)).astype(BF16)
            hid_ref[:, c * D_MODEL:(c + 1) * D_MODEL] = hid
            acc = acc + _dot(hid, wmo_ref[c])
        g = g_ref[...]
        y, hh, r = _rms_fwd(acc, g)
        e = y - t_ref[...]
        loss_ref[...] += jnp.sum(e * e, axis=0, keepdims=True)
        dy = e * (1.0 / D_MODEL)
        dg_ref[...] += jnp.sum(dy * hh, axis=0, keepdims=True)
        dh2 = _rms_bwd(dy, hh, r, g)
        dh2_ref[...] = dh2
        dh2b_ref[...] = dh2.astype(BF16)

    row = lambda w: pl.BlockSpec((TM, w), lambda i: (i, 0))
    vec = pl.BlockSpec((1, D_MODEL), lambda i: (0, 0))
    return pl.pallas_call(
        body, grid=(S // TM,), name="mlp_fwd_loss",
        in_specs=[row(D_MODEL), row(D_MODEL), row(D_MODEL), _resident(w_mi.shape), _resident(w_mo.shape),
                  _resident(g_f.shape)],
        out_specs=[row(D_FF), row(D_MODEL), row(D_MODEL), vec, vec],
        out_shape=[SDS((S, D_FF), BF16), SDS((S, D_MODEL), F32), SDS((S, D_MODEL), BF16),
                   SDS((1, D_MODEL), F32), SDS((1, D_MODEL), F32)],
        compiler_params=_cp("arbitrary"),
    )(m, h1, target, w_mi, w_mo, g_f)


def _mlp_bwd(dh2, dh2b, hid, h1, w_mi, w_mo, g_mlp):
    S = dh2.shape[0]

    def body(dh2_ref, dh2b_ref, hid_ref, h1_ref, wmi_ref, wmo_ref, g_ref, dpre_ref, dh1_ref, dh1b_ref, dg_ref):
        @pl.when(pl.program_id(0) == 0)
        def _():
            dg_ref[...] = jnp.zeros_like(dg_ref)

        d2 = dh2b_ref[...]
        dm = jnp.zeros((TM, D_MODEL), F32)
        dhid_next = _dot_nt(d2, wmo_ref[0])
        for c in range(N_CHIPS):
            cs = slice(c * D_MODEL, (c + 1) * D_MODEL)
            dhid = dhid_next
            if c + 1 < N_CHIPS:
                dhid_next = _dot_nt(d2, wmo_ref[c + 1])
            dpre = (dhid * (2.0 * jnp.sqrt(hid_ref[:, cs].astype(F32)))).astype(BF16)
            dpre_ref[:, cs] = dpre
            dm = dm + _dot_nt(dpre, wmi_ref[c])
        g = g_ref[...]
        _, hh, r = _rms_fwd(h1_ref[...], g)
        dg_ref[...] += jnp.sum(dm * hh, axis=0, keepdims=True)
        dh1 = dh2_ref[...] + _rms_bwd(dm, hh, r, g)
        dh1_ref[...] = dh1
        dh1b_ref[...] = dh1.astype(BF16)

    row = lambda w: pl.BlockSpec((TM, w), lambda i: (i, 0))
    return pl.pallas_call(
        body, grid=(S // TM,), name="mlp_bwd",
        in_specs=[row(D_MODEL), row(D_MODEL), row(D_FF), row(D_MODEL), _resident(w_mi.shape),
                  _resident(w_mo.shape), _resident(g_mlp.shape)],
        out_specs=[row(D_FF), row(D_MODEL), row(D_MODEL), pl.BlockSpec((1, D_MODEL), lambda i: (0, 0))],
        out_shape=[SDS((S, D_FF), BF16), SDS((S, D_MODEL), F32), SDS((S, D_MODEL), BF16), SDS((1, D_MODEL), F32)],
        compiler_params=_cp("arbitrary"),
    )(dh2, dh2b, hid, h1, w_mi, w_mo, g_mlp)


def _mixer_bwd(dh1b, a, p, mixed, gates, w_out, w_ao, w_po, wbd, scale, stat_reduce, sums):
    S = a.shape[0]
    n_tiles = S // TMB
    n = len(sums)

    def body(*refs):
        (dh1b_ref, a_ref, p_ref, mixed_ref, gate_ref, wout_ref, wao_ref, wpo_ref, wbd_ref, sc_ref,
         ones_ref) = refs[0:11]
        sum_refs = refs[11:11 + n]
        (da1_ref, dp1_ref, dgate_ref, da0_ref, dag1_ref, dag2_ref, dd0_ref, dd1_ref, dd2_ref,
         dmixed_ref, dqp_ref, dscale_ref) = refs[11 + n:23 + n]
        land_refs = refs[23 + n:23 + 2 * n]
        s_da, s_dd, send_sem, recv_sem = refs[23 + 2 * n:]
        i = pl.program_id(0)

        @pl.when(i == 0)
        def _():
            dscale_ref[...] = jnp.zeros_like(dscale_ref)
            for cpy in _chip_sum_copies(sum_refs, land_refs, send_sem, recv_sem):
                cpy.start()

        dmerged = _dot_nt(dh1b_ref[...], wout_ref[...])
        a = a_ref[...]
        p = p_ref[...]
        da = jnp.zeros((TMB, GROUP_W), F32)
        dp = jnp.zeros((TMB, POOL_W), F32)
        for j in range(N_CHIPS):
            js = slice(j * CHUNK, (j + 1) * CHUNK)
            sa = _sigmoid(gate_ref[:, js])
            sp = _sigmoid(gate_ref[:, D_MODEL + j * CHUNK:D_MODEL + (j + 1) * CHUNK])
            dmj = dmerged[:, js]
            da1 = (dmj * sa).astype(BF16)
            dp1 = (dmj * sp).astype(BF16)
            da1_ref[:, js] = da1
            dp1_ref[:, js] = dp1
            dgate_ref[j] = (dmj * _dot(a, wao_ref[j]) * sa * (1.0 - sa)).astype(BF16)
            dgate_ref[N_CHIPS + j] = (dmj * _dot(p, wpo_ref[j]) * sp * (1.0 - sp)).astype(BF16)
            da = da + _dot_nt(da1, wao_ref[j])
            dp = dp + _dot_nt(dp1, wpo_ref[j])

        dd = _dot_split(da * a.astype(F32), ones_ref[...])
        da0_ref[0] = da.astype(BF16)
        dd0_ref[0] = dd
        for h in range(2):
            s_da[h] = da[:, h * PAIR_W:(h + 1) * PAIR_W]
        s_dd[0] = dd
        for refs, d in (((dag1_ref, dd1_ref), DILATIONS[1]), ((dag2_ref, dd2_ref), DILATIONS[2])):
            for r in range(d):
                for h in range(2):
                    hs = slice(h * PAIR_W, (h + 1) * PAIR_W)
                    refs[0][r, :, hs] = s_da[h, pl.ds(r, TMB // d, stride=d), :].astype(BF16)
                refs[1][r] = s_dd[0, pl.ds(r, TMB // d, stride=d), :]

        sc = sc_ref[...]
        dscale_ref[...] += jnp.sum(dp * mixed_ref[...].astype(F32), axis=0, keepdims=True)
        dmixed = (dp * sc).astype(BF16)
        dmixed_ref[...] = dmixed
        inv_cnt, _ = _pool_inv_count(i, TMB)
        dqp_ref[...] = (_dot_nt(dmixed, wbd_ref[...]) * inv_cnt).astype(BF16)

        @pl.when(i == n_tiles - 1)
        def _():
            for cpy in _chip_sum_copies(sum_refs, land_refs, send_sem, recv_sem):
                cpy.wait()

    row = lambda w: pl.BlockSpec((TMB, w), lambda i: (i, 0))
    grp_spec = lambda d: pl.BlockSpec((d, TMB // d, GROUP_W), lambda i: (0, i, 0))
    stat_spec = lambda d: pl.BlockSpec((d, TMB // d, STAT_W), lambda i: (0, i, 0))
    d0, d1, d2 = DILATIONS
    res = pl.pallas_call(
        body, grid=(n_tiles,), name="mixer_bwd",
        in_specs=[row(D_MODEL), row(GROUP_W), row(POOL_W), row(POOL_W), row(2 * D_MODEL),
                  _resident(w_out.shape), _resident(w_ao.shape), _resident(w_po.shape), _resident(wbd.shape),
                  _resident(scale.shape), _resident(stat_reduce.shape)] + [ANY] * n,
        out_specs=[row(D_MODEL), row(D_MODEL), pl.BlockSpec((2 * N_CHIPS, TMB, CHUNK), lambda i: (0, i, 0)),
                   grp_spec(d0), grp_spec(d1), grp_spec(d2), stat_spec(d0), stat_spec(d1), stat_spec(d2),
                   row(POOL_W), row(POOL_W), pl.BlockSpec((1, POOL_W), lambda i: (0, 0))] + [ANY] * n,
        out_shape=[SDS((S, D_MODEL), BF16), SDS((S, D_MODEL), BF16), SDS((2 * N_CHIPS, S, CHUNK), BF16)]
        + [SDS((d, S // d, GROUP_W), BF16) for d in DILATIONS]
        + [SDS((d, S // d, STAT_W), F32) for d in DILATIONS]
        + [SDS((S, POOL_W), BF16), SDS((S, POOL_W), BF16), SDS((1, POOL_W), F32)]
        + [SDS(t.shape, t.dtype) for t in sums],
        scratch_shapes=[pltpu.VMEM((2, TMB, PAIR_W), F32), pltpu.VMEM((1, TMB, STAT_W), F32),
                        pltpu.SemaphoreType.DMA((3 * n,)), pltpu.SemaphoreType.DMA((3 * n,))],
        compiler_params=_cp("arbitrary"),
    )(dh1b, a, p, mixed, gates, w_out, w_ao, w_po, wbd, scale, stat_reduce, *sums)
    return res[:12], res[12:]


def _attn_bwd(qkv, da, lt, dd, grp, packs=None):
    d = DILATIONS[grp]
    L = qkv.shape[2]
    RR, RB, nb = _attn_tiles(grp, L)
    n_j = d // RR
    hosted = packs is not None

    def body(*refs):
        q_ref, kc_ref, kp_ref, vc_ref, vp_ref, da_ref, lt_ref, dd_ref = refs[0:8]
        dq_ref, dk_ref, dv_ref = refs[8 + hosted:11 + hosted]
        dk_acc, dv_acc = refs[11 + 2 * hosted:13 + 2 * hosted]
        i = pl.program_id(1)
        if hosted:
            j = pl.program_id(0)
            start, relay, finish = _pack_allgather(refs[11 + hosted], *refs[13 + 2 * hosted:])
            pl.when((j == 0) & (i == 0))(start)
            pl.when((j == 0) & (i == nb // 2))(relay)

        @pl.when(i == 0)
        def _():
            dk_acc[...] = jnp.zeros_like(dk_acc)
            dv_acc[...] = jnp.zeros_like(dv_acc)

        def compute(cur, prv):
            dk_acc[cur] = jnp.zeros((RR, RB * BAND, GROUP_W), F32)
            dv_acc[cur] = jnp.zeros((RR, RB * BAND, GROUP_W), F32)
            biases, col = _band_bias(grp, d)
            first_keys_ok = (col >= BAND) | (i > 0)
            is_a = lax.broadcasted_iota(jnp.int32, (BAND, PAIR_W), 1) < HEAD_W
            for rr in range(RR):
                for rb in range(RB):
                    rows = slice(rb * BAND, (rb + 1) * BAND)
                    for cp in range(2):
                        cs = slice(cp * PAIR_W, (cp + 1) * PAIR_W)
                        q2 = q_ref[rr, rows, cs]
                        da2 = da_ref[rr, rows, cs]
                        lt2 = lt_ref[rr, rows, :]
                        dd2 = dd_ref[rr, rows, :]
                        kcat = _kv_tile(kc_ref, kp_ref, rr, rb, cs)
                        vcat = _kv_tile(vc_ref, vp_ref, rr, rb, cs)
                        q2t = q2.astype(F32).T.astype(BF16)
                        da2t = da2.astype(F32).T.astype(BF16)
                        dqs, dkts, dvts, scores, dpvs = [], [], [], [], []
                        for h2 in range(2):
                            sel = is_a if h2 == 0 else jnp.logical_not(is_a)
                            b = biases[cp * 2 + h2]
                            if rb == 0:
                                b = jnp.where(first_keys_ok, b, NEG)
                            scores.append(_dot_nt(jnp.where(sel, q2, jnp.zeros_like(q2)), kcat) + b)
                            dpvs.append(_dot_nt(jnp.where(sel, da2, jnp.zeros_like(da2)), vcat))
                        for h2 in range(2):
                            lane0 = h2 * HEAD_W
                            p = jnp.exp(scores[h2] - _head_col(lt2, cp * 2 + h2))
                            ds = (p * (dpvs[h2] - _head_col(dd2, cp * 2 + h2))).astype(BF16)
                            dqs.append(_dot(ds, kcat))
                            dkts.append(_dot(q2t[lane0:lane0 + HEAD_W, :], ds))
                            dvts.append(_dot(da2t[lane0:lane0 + HEAD_W, :], p.astype(BF16)))
                        dq_ref[rr, rows, cs] = (jnp.where(is_a, dqs[0], dqs[1]) * 0.125).astype(BF16)
                        dkc = jnp.concatenate(dkts, axis=0).T
                        dvc = jnp.concatenate(dvts, axis=0).T
                        if rb == 0:
                            last = slice((RB - 1) * BAND, RB * BAND)
                            dk_acc[prv, rr, last, cs] += dkc[0:BAND]
                            dv_acc[prv, rr, last, cs] += dvc[0:BAND]
                            dk_acc[cur, rr, 0:BAND, cs] += dkc[BAND:]
                            dv_acc[cur, rr, 0:BAND, cs] += dvc[BAND:]
                        else:
                            both = slice((rb - 1) * BAND, (rb + 1) * BAND)
                            dk_acc[cur, rr, both, cs] += dkc
                            dv_acc[cur, rr, both, cs] += dvc

        def flush(prv):
            dk_ref[...] = dk_acc[prv].astype(BF16)
            dv_ref[...] = dv_acc[prv].astype(BF16)

        for parity in (0, 1):
            on = (i % 2) == parity
            pl.when(on & (i < nb))(functools.partial(compute, parity, 1 - parity))
            pl.when(on & (i > 0))(functools.partial(flush, 1 - parity))
        if hosted:
            pl.when((j == n_j - 1) & (i == nb))(finish)

    qi = lambda i: jnp.minimum(i, nb - 1)
    cur_w = lambda w: pl.BlockSpec((None, RR, RB * BAND, GROUP_W), lambda j, i: (w, j, qi(i), 0))
    prev_w = lambda w: pl.BlockSpec((None, RR, BAND, GROUP_W),
                                    lambda j, i: (w, j, jnp.maximum(qi(i) * RB - 1, 0), 0))
    blk = pl.BlockSpec((RR, RB * BAND, GROUP_W), lambda j, i: (j, qi(i), 0))
    stat_blk = pl.BlockSpec((RR, RB * BAND, STAT_W), lambda j, i: (j, qi(i), 0))
    late = pl.BlockSpec((RR, RB * BAND, GROUP_W), lambda j, i: (j, jnp.maximum(i - 1, 0), 0))
    extra = [packs] if hosted else []
    return pl.pallas_call(
        body, grid=(n_j, nb + 1), name=f"attn_bwd_g{grp}",
        in_specs=[cur_w(0), cur_w(1), prev_w(1), cur_w(2), prev_w(2), blk, stat_blk, stat_blk] + [ANY] * hosted,
        out_specs=[blk, late, late] + [ANY] * hosted,
        out_shape=[SDS((d, L, GROUP_W), BF16)] * 3 + [SDS(t.shape, t.dtype) for t in extra],
        scratch_shapes=[pltpu.VMEM((2, RR, RB * BAND, GROUP_W), F32), pltpu.VMEM((2, RR, RB * BAND, GROUP_W), F32)]
        + [pltpu.SemaphoreType.DMA((N_DEV - 1,))] * (2 * hosted),
        input_output_aliases={8: 3} if hosted else {},
        compiler_params=_cp("arbitrary" if hosted else "parallel", "arbitrary"),
    )(qkv, qkv, qkv, qkv, qkv, da, lt, dd, *extra)


def _dz_assemble(dqkv, dqp):
    S = dqp.shape[0]
    tm = min(TMZ, S)
    n_tiles = S // tm

    def body(*refs):
        dqkv_refs = refs[0:9]
        dqp_ref, halo_ref = refs[9:11]
        dz_ref, s_ref, ext_ref = refs[11:]
        i = pl.program_id(0)

        for grp in range(3):
            for which in range(3):
                n = which * 3 + grp
                ref = dqkv_refs[grp * 3 + which]
                if DILATIONS[grp] == 1:
                    dz_ref[n] = ref[0]
                else:
                    _interleave_load(ref, (), s_ref, DILATIONS[grp], tm)
                    for h in range(2):
                        dz_ref[n, :, h * PAIR_W:(h + 1) * PAIR_W] = s_ref[h].astype(BF16)

        dqp = dqp_ref[...].astype(F32)
        ext_ref[0:tm, :] = dqp
        ext_ref[tm:, :] = jnp.where(i < n_tiles - 1, halo_ref[...].astype(F32), 0.0)
        sums = []
        acc = ext_ref[...]
        for k in (1, 2, 4, 8):
            acc = acc + pltpu.roll(acc, tm + POOL_HALO - k, 0)
            sums.append(acc[0:tm, :])
        inv_cnt, col = _pool_inv_count(i, tm)
        dpz = _pool_column_select(col, sums) - dqp / inv_cnt
        for t in range(3):
            dz_ref[9 + t] = dpz[:, t * CHUNK:(t + 1) * CHUNK].astype(BF16)

    row = lambda w: pl.BlockSpec((tm, w), lambda i: (i, 0))
    grp_spec = lambda d: pl.BlockSpec((d, tm // d, GROUP_W), lambda i: (0, i, 0))
    halo = pl.BlockSpec((POOL_HALO, POOL_W),
                        lambda i: (jnp.minimum((i + 1) * (tm // POOL_HALO), S // POOL_HALO - 1), 0))
    flat = [t for grp in range(3) for t in dqkv[grp]]
    return pl.pallas_call(
        body, grid=(n_tiles,), name="dz_assemble",
        in_specs=[grp_spec(DILATIONS[grp]) for grp in range(3) for _ in range(3)] + [row(POOL_W), halo],
        out_specs=pl.BlockSpec((N_DZ_CHUNKS, tm, CHUNK), lambda i: (0, i, 0)),
        out_shape=SDS((N_DZ_CHUNKS, S, CHUNK), BF16),
        scratch_shapes=[pltpu.VMEM((2, tm, PAIR_W), F32), pltpu.VMEM((tm + POOL_HALO, POOL_W), F32)],
        compiler_params=_cp("parallel"),
    )(*flat, dqp, dqp)


def _inproj_dx(dz, dgates, dh1, x, g, w_in, sums):
    S = x.shape[0]
    n_tiles = S // TM
    n = len(sums)

    def body(*refs):
        dz_ref, dgate_ref, dh1_ref, x_ref, g_ref, w_ref = refs[0:6]
        sum_refs = refs[6:6 + n]
        dx_ref, dg_ref = refs[6 + n:8 + n]
        land_refs = refs[8 + n:8 + 2 * n]
        sems = refs[8 + 2 * n:]
        i = pl.program_id(0)

        def copies():
            return _chip_sum_copies(sum_refs, land_refs, *sems)

        @pl.when(i == 0)
        def _():
            dg_ref[...] = jnp.zeros_like(dg_ref)
            for cpy in copies():
                cpy.start()

        du = jnp.zeros((TM, D_MODEL), F32)
        for k in range(N_CHUNKS):
            dzk = dz_ref[k] if k < N_DZ_CHUNKS else dgate_ref[k - N_DZ_CHUNKS]
            du = du + _dot_nt(dzk, _w_in_chunk(w_ref, k))
        gv = g_ref[...]
        _, xh, r = _rms_fwd(x_ref[...], gv)
        dg_ref[...] += jnp.sum(du * xh, axis=0, keepdims=True)
        dx_ref[...] = dh1_ref[...] + _rms_bwd(du, xh, r, gv)

        @pl.when(i == n_tiles - 1)
        def _():
            for cpy in copies():
                cpy.wait()

    row = lambda w: pl.BlockSpec((TM, w), lambda i: (i, 0))
    res = pl.pallas_call(
        body, grid=(n_tiles,), name="inproj_dx",
        in_specs=[pl.BlockSpec((N_DZ_CHUNKS, TM, CHUNK), lambda i: (0, i, 0)),
                  pl.BlockSpec((N_CHUNKS - N_DZ_CHUNKS, TM, CHUNK), lambda i: (0, i, 0)),
                  row(D_MODEL), row(D_MODEL), _resident(g.shape), _resident(w_in.shape)] + [ANY] * n,
        out_specs=[row(D_MODEL), pl.BlockSpec((1, D_MODEL), lambda i: (0, 0))] + [ANY] * n,
        out_shape=[SDS((S, D_MODEL), F32), SDS((1, D_MODEL), F32)] + [SDS(t.shape, t.dtype) for t in sums],
        scratch_shapes=[pltpu.SemaphoreType.DMA((3 * n,)), pltpu.SemaphoreType.DMA((3 * n,))],
        compiler_params=_cp("arbitrary"),
    )(dz, dgates, dh1, x, g, w_in, *sums)
    return res[0], res[1], res[2:]


def _wgrad(a, b, name, *, out_shape, a_spec, b_spec, out_spec, grid, n_out_cols=None, fill=None, narrow=True):
    k_axis = len(grid) - 1
    n_k = grid[k_axis]
    n_out = 2 if narrow else 1

    def body(a_ref, b_ref, *rest):
        o_ref = rest[-n_out]

        @pl.when(pl.program_id(k_axis) == 0)
        def _():
            o_ref[...] = jnp.zeros_like(o_ref)

        at = a_ref[...]
        if n_out_cols is None:
            o_ref[...] += _dot_tn(at, b_ref[...])
        elif n_out_cols[0] == "lead_both":
            for t in range(b_ref.shape[0]):
                o_ref[t] += _dot_tn(at, b_ref[t])
        else:
            w = n_out_cols[1]
            for t in range(o_ref.shape[0]):
                o_ref[t] += _dot_tn(at, b_ref[:, t * w:(t + 1) * w])

        if narrow:
            @pl.when(pl.program_id(k_axis) == n_k - 1)
            def _():
                rest[-1][...] = o_ref[...].astype(BF16)

    sem = ("parallel",) * k_axis + ("arbitrary",)
    extra = [] if fill is None else list(fill) if narrow else [fill]
    shapes = [out_shape, SDS(out_shape.shape, BF16)] if narrow else out_shape
    return pl.pallas_call(body, grid=grid, name=name, in_specs=[a_spec, b_spec] + [ANY] * len(extra),
                          out_specs=[out_spec] * n_out if narrow else out_spec, out_shape=shapes,
                          input_output_aliases={2 + t: t for t in range(len(extra))},
                          compiler_params=_cp(*sem))(a, b, *extra)


def _wgrad_in(u, dz, dgates):
    bk = min(BK, u.shape[0])
    nk = u.shape[0] // bk
    g = WGRAD_IN_GROUP
    kw = dict(n_out_cols=("lead_both", CHUNK), a_spec=pl.BlockSpec((bk, D_MODEL), lambda j, k: (k, 0)),
              b_spec=pl.BlockSpec((g, bk, CHUNK), lambda j, k: (j, k, 0)),
              out_shape=SDS((N_CHUNKS, D_MODEL, CHUNK), F32))
    first = _wgrad(u, dz, "wgrad_in_qkvp", grid=(N_DZ_CHUNKS // g, nk),
                   out_spec=pl.BlockSpec((g, D_MODEL, CHUNK), lambda j, k: (j, 0, 0)), **kw)
    both = _wgrad(u, dgates, "wgrad_in_gates", grid=((N_CHUNKS - N_DZ_CHUNKS) // g, nk), fill=first,
                  out_spec=pl.BlockSpec((g, D_MODEL, CHUNK), lambda j, k: (N_DZ_CHUNKS // g + j, 0, 0)), **kw)
    return [t.reshape(N_CHIPS, CHUNKS_PER_SHARD * D_MODEL, CHUNK) for t in both]


def _wgrads_mixer(a, da1, p, dp1, merged, dh1b, pooled, dmixed):
    bk = min(BK, a.shape[0])
    nk = a.shape[0] // bk
    g_ao = _wgrad(
        a, da1, "wgrad_att_out", grid=(nk,), n_out_cols=("cols_b", CHUNK),
        a_spec=pl.BlockSpec((bk, GROUP_W), lambda k: (k, 0)),
        b_spec=pl.BlockSpec((bk, D_MODEL), lambda k: (k, 0)),
        out_spec=pl.BlockSpec((N_CHIPS, GROUP_W, CHUNK), lambda k: (0, 0, 0)),
        out_shape=SDS((N_CHIPS, GROUP_W, CHUNK), F32))
    g_po = _wgrad(
        p, dp1, "wgrad_pool_out", grid=(nk,), n_out_cols=("cols_b", CHUNK),
        a_spec=pl.BlockSpec((bk, POOL_W), lambda k: (k, 0)),
        b_spec=pl.BlockSpec((bk, D_MODEL), lambda k: (k, 0)),
        out_spec=pl.BlockSpec((N_CHIPS, POOL_W, CHUNK), lambda k: (0, 0, 0)),
        out_shape=SDS((N_CHIPS, POOL_W, CHUNK), F32))
    g_out = _wgrad(
        merged, dh1b, "wgrad_out", grid=(nk,),
        a_spec=pl.BlockSpec((bk, D_MODEL), lambda k: (k, 0)),
        b_spec=pl.BlockSpec((bk, D_MODEL), lambda k: (k, 0)),
        out_spec=pl.BlockSpec((D_MODEL, D_MODEL), lambda k: (0, 0)),
        out_shape=SDS((D_MODEL, D_MODEL), F32))
    g_bd = _wgrad(
        pooled, dmixed, "wgrad_pool_grp", grid=(nk,),
        a_spec=pl.BlockSpec((bk, POOL_W), lambda k: (k, 0)),
        b_spec=pl.BlockSpec((bk, POOL_W), lambda k: (k, 0)),
        out_spec=pl.BlockSpec((POOL_W, POOL_W), lambda k: (0, 0)),
        out_shape=SDS((POOL_W, POOL_W), F32), narrow=False)
    g_out = [t.reshape(N_CHIPS, D_MODEL // N_CHIPS, D_MODEL) for t in g_out]
    return [g_ao, g_po, g_out], g_bd


def _wgrads_mlp(m, dpre, hid, dh2b):
    bk = min(BK, m.shape[0])
    nk = m.shape[0] // bk
    g_mi = _wgrad(
        m, dpre, "wgrad_mlp_in", grid=(N_CHIPS, nk),
        a_spec=pl.BlockSpec((bk, D_MODEL), lambda c, k: (k, 0)),
        b_spec=pl.BlockSpec((bk, D_MODEL), lambda c, k: (k, c)),
        out_spec=pl.BlockSpec((None, D_MODEL, D_MODEL), lambda c, k: (c, 0, 0)),
        out_shape=SDS((N_CHIPS, D_MODEL, D_MODEL), F32))
    g_mo = _wgrad(
        hid, dh2b, "wgrad_mlp_out", grid=(N_CHIPS, nk),
        a_spec=pl.BlockSpec((bk, D_MODEL), lambda c, k: (k, c)),
        b_spec=pl.BlockSpec((bk, D_MODEL), lambda c, k: (k, 0)),
        out_spec=pl.BlockSpec((None, D_MODEL, D_MODEL), lambda c, k: (c, 0, 0)),
        out_shape=SDS((N_CHIPS, D_MODEL, D_MODEL), F32))
    return [g_mi, g_mo]


def _mesh_place():
    x, y, c = lax.axis_index("x"), lax.axis_index("y"), lax.axis_index("c")
    other_chips = [(x, 1 - y), (1 - x, y), (1 - x, 1 - y)]
    return x, y, c, other_chips


ANY = pl.BlockSpec(memory_space=pl.ANY)


def _weight_half_copies(shard_refs, buf_refs, rows, send_sem, recv_sem):
    x, y, c, chips = _mesh_place()
    me = 2 * x + y
    copies = []
    for w, r_full in enumerate(rows):
        rh = r_full // 2
        for r, (px, py) in enumerate(chips):
            k = w * 3 + r
            copies.append(pltpu.make_async_remote_copy(
                src_ref=shard_refs[w].at[pl.ds(c * rh, rh), :], dst_ref=buf_refs[w].at[me, pl.ds(c * rh, rh), :],
                send_sem=send_sem.at[k], recv_sem=recv_sem.at[k], device_id=(px, py, c), device_id_type=MESH))
    return copies


def _pair_forward_copies(buf_refs, rows, send_sem, recv_sem):
    x, y, c, chips = _mesh_place()
    out = []
    for w, r_full in enumerate(rows):
        rh = r_full // 2
        for r, (px, py) in enumerate(chips):
            k = w * 3 + r
            landed = buf_refs[w].at[2 * px + py, pl.ds(c * rh, rh), :]
            theirs = buf_refs[w].at[2 * px + py, pl.ds((1 - c) * rh, rh), :]
            mk = lambda ref: pltpu.make_async_remote_copy(
                src_ref=ref, dst_ref=ref, send_sem=send_sem.at[k], recv_sem=recv_sem.at[k],
                device_id=(x, y, 1 - c), device_id_type=MESH)
            out.append((mk(landed), mk(theirs)))
    return out


def _place_own(block, n_slots, slot):
    buf = lax.empty((n_slots,) + block.shape, block.dtype)
    return lax.dynamic_update_slice(buf, block[None], (slot,) + (0,) * block.ndim)


def _pair_forward(bufs, rows, name):
    n = len(bufs)

    def body(*refs):
        dst = refs[n:2 * n]
        send_sem, recv_sem = refs[2 * n:]
        fwds = _pair_forward_copies(dst, rows, send_sem, recv_sem)
        for fwd, _ in fwds:
            fwd.start()
        for fwd, landing in fwds:
            landing.wait_recv()
            fwd.wait_send()

    return pl.pallas_call(
        body, name=name,
        in_specs=[ANY] * n, out_specs=[ANY] * n,
        out_shape=[SDS(b.shape, b.dtype) for b in bufs],
        scratch_shapes=[pltpu.SemaphoreType.DMA((3 * n,))] * 2,
        input_output_aliases={w: w for w in range(n)},
    )(*bufs)


def _chip_sum_copies(src, dst, send_sem, recv_sem):
    x, y, c, chips = _mesh_place()
    copies = []
    for w in range(len(src)):
        for r, (px, py) in enumerate(chips):
            k = w * 3 + r
            copies.append(pltpu.make_async_remote_copy(
                src_ref=src[w].at[r + 1], dst_ref=dst[w].at[r + 1], send_sem=send_sem.at[k], recv_sem=recv_sem.at[k],
                device_id=(px, py, c), device_id_type=MESH))
    return copies


def _pair_exchange(grads):
    n = len(grads)

    def body(*refs):
        src, dst = refs[:n], refs[n:2 * n]
        send_sem, recv_sem = refs[2 * n:]
        x, y, c, _ = _mesh_place()
        copies = []
        for w in range(n):
            rh = grads[w].shape[1] // 2
            copies.append(pltpu.make_async_remote_copy(
                src_ref=src[w].at[:, pl.ds((1 - c) * rh, rh), :], dst_ref=dst[w],
                send_sem=send_sem.at[w], recv_sem=recv_sem.at[w],
                device_id=(x, y, 1 - c), device_id_type=MESH))
            copies[-1].start()
        for cpy in copies:
            cpy.wait()

    return pl.pallas_call(
        body, name="grad_pair_exchange",
        in_specs=[ANY] * n, out_specs=[ANY] * n,
        out_shape=[SDS((N_CHIPS, g.shape[1] // 2, g.shape[2]), g.dtype) for g in grads],
        scratch_shapes=[pltpu.SemaphoreType.DMA((n,)), pltpu.SemaphoreType.DMA((n,))],
    )(*grads)


def _pair_sum(place, grad, recv, name):
    _, R, C = grad.shape
    rh = R // 2
    br = _row_block(rh, max(256, ELEMENTWISE_BLOCK // C))
    nbh = rh // br

    def body(place_ref, g_ref, r_ref, own_ref, sums_ref):
        s = g_ref[...] + r_ref[...].astype(F32)

        @pl.when(pl.program_id(1) == 0)
        def _():
            own_ref[...] = s

        sums_ref[...] = s.astype(BF16)

    slot = lambda rel, pr: jnp.bitwise_xor(pr[0], rel)
    return pl.pallas_call(
        body, name=name,
        grid_spec=pltpu.PrefetchScalarGridSpec(
            num_scalar_prefetch=1, grid=(nbh, N_CHIPS),
            in_specs=[pl.BlockSpec((None, br, C), lambda i, rel, pr: (slot(rel, pr), pr[1] * nbh + i, 0)),
                      pl.BlockSpec((None, br, C), lambda i, rel, pr: (slot(rel, pr), i, 0))],
            out_specs=[pl.BlockSpec((br, C), lambda i, rel, pr: (i, 0)),
                       pl.BlockSpec((None, br, C), lambda i, rel, pr: (rel, i, 0))]),
        out_shape=[SDS((rh, C), F32), SDS((N_CHIPS, rh, C), BF16)],
        compiler_params=_cp("parallel", "arbitrary"),
    )(place, grad, recv)


def _chip_sum(place, own, recv, name):
    rh, C = own.shape
    br = _row_block(rh, max(256, ELEMENTWISE_BLOCK // C))
    nbh = rh // br

    def body(place_ref, own_ref, r_ref, o_ref):
        o_ref[...] = ((own_ref[...] + r_ref[1].astype(F32)) + r_ref[2].astype(F32)) + r_ref[3].astype(F32)

    return pl.pallas_call(
        body, name=name,
        grid_spec=pltpu.PrefetchScalarGridSpec(
            num_scalar_prefetch=1, grid=(nbh,),
            in_specs=[pl.BlockSpec((br, C), lambda i, pr: (i, 0)),
                      pl.BlockSpec((N_CHIPS, br, C), lambda i, pr: (0, i, 0))],
            out_specs=pl.BlockSpec((br, C), lambda i, pr: (pr[1] * nbh + i, 0))),
        out_shape=SDS((2 * rh, C), F32),
        compiler_params=_cp("parallel"),
    )(place, own, recv)


def _pack_allgather(all_ref, send_sem, recv_sem):
    x, y, c, chips = _mesh_place()
    sib = (x, y, 1 - c)

    def pack(dev, k, to):
        slot = 4 * dev[0] + 2 * dev[1] + dev[2]
        return pltpu.make_async_remote_copy(
            src_ref=all_ref.at[slot], dst_ref=all_ref.at[slot], send_sem=send_sem.at[k],
            recv_sem=recv_sem.at[k], device_id=to, device_id_type=MESH)

    first = [pack((x, y, c), 0, sib)] + [pack((x, y, c), 1 + r, (px, py, c)) for r, (px, py) in enumerate(chips)]
    relays = [pack((px, py, c), 4 + r, sib) for r, (px, py) in enumerate(chips)]

    def start():
        for cpy in first:
            cpy.start()

    def relay():
        for r, (px, py) in enumerate(chips):
            pack((px, py, c), 1 + r, (px, py, c)).wait_recv()
            relays[r].start()

    def finish():
        pack(sib, 0, sib).wait_recv()
        for r, (px, py) in enumerate(chips):
            pack((px, py, 1 - c), 4 + r, sib).wait_recv()
        for cpy in first + relays:
            cpy.wait_send()

    return start, relay, finish


def _finish_exchange(grads, late_all):
    n = len(grads)

    def body(*refs):
        dst, all_ref = refs[n + 1:2 * n + 1], refs[2 * n + 1]
        send_sem, recv_sem, ssend_sem, srecv_sem = refs[2 * n + 2:]
        x, y, c, _ = _mesh_place()
        start, relay, finish = _pack_allgather(all_ref, ssend_sem, srecv_sem)
        start()
        sends, landings = [], []
        for w in range(n):
            rh = grads[w].shape[0] // 2
            mk = lambda cc: pltpu.make_async_remote_copy(
                src_ref=dst[w].at[pl.ds(cc * rh, rh), :], dst_ref=dst[w].at[pl.ds(cc * rh, rh), :],
                send_sem=send_sem.at[w], recv_sem=recv_sem.at[w], device_id=(x, y, 1 - c), device_id_type=MESH)
            sends.append(mk(c))
            landings.append(mk(1 - c))
            sends[-1].start()
        relay()
        finish()
        for cpy in landings:
            cpy.wait_recv()
        for cpy in sends:
            cpy.wait_send()

    res = pl.pallas_call(
        body, name="grad_finish_exchange",
        in_specs=[ANY] * (n + 1), out_specs=[ANY] * (n + 1),
        out_shape=[SDS(g.shape, g.dtype) for g in grads] + [SDS(late_all.shape, late_all.dtype)],
        scratch_shapes=[pltpu.SemaphoreType.DMA((n,)), pltpu.SemaphoreType.DMA((n,)),
                        pltpu.SemaphoreType.DMA((N_DEV - 1,)), pltpu.SemaphoreType.DMA((N_DEV - 1,))],
        input_output_aliases={w: w for w in range(n + 1)},
    )(*grads, late_all)
    return res[:n], res[n]


def _adamw_math(w, g, m, v):
    m = ADAM_B1 * m + (1.0 - ADAM_B1) * g
    v = ADAM_B2 * v + (1.0 - ADAM_B2) * jnp.square(g)
    m_hat = m / (1.0 - ADAM_B1 ** ADAM_STEP)
    v_hat = v / (1.0 - ADAM_B2 ** ADAM_STEP)
    delta = -ADAM_LR * (m_hat / (jnp.sqrt(v_hat) + ADAM_EPS) + ADAM_WD * w)
    return delta, m, v


def _adamw(w, g, m, v, name):
    R, C = w.shape
    br = _row_block(R, 512)
    if g.ndim == 3:
        n_chunks, cw = g.shape[0], g.shape[2]
        g_spec = pl.BlockSpec((None, br, cw), lambda t, i: (t, i, 0))
    else:
        n_chunks, cw = 1, C
        g_spec = pl.BlockSpec((br, cw), lambda t, i: (i, t))

    def body(w_ref, g_ref, m_ref, v_ref, g_out_ref, d_ref, nm_ref, nv_ref):
        gv = g_ref[...]
        g_out_ref[...] = gv
        d_ref[...], nm_ref[...], nv_ref[...] = _adamw_math(w_ref[...], gv, m_ref[...], v_ref[...])

    spec = pl.BlockSpec((br, cw), lambda t, i: (i, t))
    return pl.pallas_call(
        body, grid=(n_chunks, R // br), name=name, in_specs=[spec, g_spec, spec, spec], out_specs=[spec] * 4,
        out_shape=[SDS((R, C), F32)] * 4, compiler_params=_cp("parallel", "parallel"),
    )(w, g, m, v)


def _small_sum_adamw(all_small, all_late, w, m, v):
    loss_row = PACK_ROWS - 8

    def body(all_ref, late_ref, w_ref, m_ref, v_ref, g_ref, d_ref, nm_ref, nv_ref, loss_ref):
        g = all_ref[0]
        late = late_ref[0]
        for k in range(1, N_DEV):
            g = g + all_ref[k]
            late = late + late_ref[k]
        g_ref[...] = g
        g_ref[PACK_LATE_ROW:PACK_LATE_ROW + 8, :] = late
        g = g_ref[...]
        d_ref[...], nm_ref[...], nv_ref[...] = _adamw_math(w_ref[...], g, m_ref[...], v_ref[...])
        total = jnp.sum(g[loss_row:loss_row + 1, :]) * (0.5 / D_MODEL)
        loss_ref[...] = jnp.full(loss_ref.shape, total, F32)

    full = lambda s: pl.BlockSpec(s, lambda i: (0,) * len(s))
    pack = (PACK_ROWS, D_MODEL)
    return pl.pallas_call(
        body, grid=(1,), name="small_sum_adamw",
        in_specs=[full((N_DEV,) + pack), full((N_DEV, 8, D_MODEL)), full(pack), full(pack), full(pack)],
        out_specs=[full(pack)] * 4 + [full((8, 128))],
        out_shape=[SDS(pack, F32)] * 4 + [SDS((8, 128), F32)],
        compiler_params=_cp("arbitrary"),
    )(all_small, all_late, w, m, v)


def _pack_small(grp, scale, g_mix, g_mlp, g_f, loss_lanes):
    def part(vec):
        vec = vec.reshape(1, -1)
        return jnp.pad(vec, ((0, 7), (0, D_MODEL - vec.shape[1])))
    return jnp.concatenate([grp.reshape(-1, D_MODEL), part(scale), part(g_mix), part(g_mlp), part(g_f),
                            part(loss_lanes)], axis=0)


def _unpack_small(pack):
    n_grp = len(POOL_WINDOWS) * POOL_GROUP_W * POOL_GROUP_W // D_MODEL
    grp = pack[:n_grp].reshape(1, len(POOL_WINDOWS), POOL_GROUP_W, POOL_GROUP_W)
    scale = pack[n_grp, :POOL_W].reshape(1, POOL_W)
    g_mix = pack[n_grp + 8].reshape(1, D_MODEL)
    g_mlp = pack[n_grp + 16].reshape(1, D_MODEL)
    g_f = pack[n_grp + 24].reshape(D_MODEL)
    return grp, scale, g_mix, g_mlp, g_f


def _block_diag(grp):
    out = jnp.zeros((POOL_W, POOL_W), grp.dtype)
    for k in range(len(POOL_WINDOWS)):
        out = lax.dynamic_update_slice(out, grp[k], (k * POOL_GROUP_W, k * POOL_GROUP_W))
    return out


def kernel(x, norm_mix_g, w_in, w_att_out, w_pool_grp, pool_scale, w_pool_out, w_out, norm_mlp_g, w_mlp_in, w_mlp_out, norm_final_g, loss_target, m_norm_mix_g, m_w_in, m_w_att_out, m_w_pool_grp, m_pool_scale, m_w_pool_out, m_w_out, m_norm_mlp_g, m_w_mlp_in, m_w_mlp_out, m_norm_final_g, v_norm_mix_g, v_w_in, v_w_att_out, v_w_pool_grp, v_pool_scale, v_w_pool_out, v_w_out, v_norm_mlp_g, v_w_mlp_in, v_w_mlp_out, v_norm_final_g):
    S = x.shape[1]
    xs, target = x[0], loss_target[0]
    big = [w_in[0], w_att_out[0], w_pool_out[0], w_out[0], w_mlp_in[0], w_mlp_out[0]]
    big_m = [m_w_in[0], m_w_att_out[0], m_w_pool_out[0], m_w_out[0], m_w_mlp_in[0], m_w_mlp_out[0]]
    big_v = [v_w_in[0], v_w_att_out[0], v_w_pool_out[0], v_w_out[0], v_w_mlp_in[0], v_w_mlp_out[0]]

    chip = 2 * lax.axis_index("x") + lax.axis_index("y")
    core = lax.axis_index("c")
    place = jnp.stack([chip, core]).astype(jnp.int32)
    names = ("w_in", "w_att_out", "w_pool_out", "w_out", "w_mlp_in", "w_mlp_out")

    shards = [w.astype(BF16) for w in big]
    bufs = [_place_own(sh, N_CHIPS, chip) for sh in shards]
    wbd = _block_diag(w_pool_grp[0]).astype(BF16)
    g_final = norm_final_g.reshape(1, D_MODEL)
    stat_expand, stat_reduce = _stat_matrices()

    u, z_own, landed_in = _norm_inproj_own(xs, norm_mix_g, shards[0], bufs[0])
    (wg_in,) = _pair_forward([landed_in], [shards[0].shape[0]], "w_in_pair_forward")
    (qkv0, qkv1, qkv2, pz, gates), (wg_ao, wg_po, wg_out) = _inproj_rest(u, z_own, wg_in, shards[1:4], bufs[1:4])
    wg_out = wg_out.reshape(D_MODEL, D_MODEL)
    qkv = (qkv0, qkv1, qkv2)
    att = [_attn_fwd(qkv[grp], grp) for grp in range(3)]
    (a, lt0, lt1, lt2, pooled, mixed, p, merged, h1, m), (wg_mi, wg_mo) = _mixer_out(
        [o for o, _ in att], [l for _, l in att], pz, gates, xs, wg_ao, wg_po, wbd, pool_scale, wg_out, norm_mlp_g,
        stat_expand, shards[4:], bufs[4:])
    hid, dh2, dh2b, loss_lanes, dg_final = _mlp_fwd_loss(m, h1, target, wg_mi, wg_mo, g_final)

    def pair_reduce(grads, grad_names):
        recv = _pair_exchange([narrow for _, narrow in grads])
        pair = [_pair_sum(place, g, r, f"pair_sum_{nm}") for (g, _), r, nm in zip(grads, recv, grad_names)]
        return [own for own, _ in pair], [s for _, s in pair]

    def chip_reduce(owns, landed_sums, grad_names):
        return [_chip_sum(place, own, r, f"chip_sum_{nm}") for own, r, nm in zip(owns, landed_sums, grad_names)]

    dpre, dh1, dh1b, dg_mlp = _mlp_bwd(dh2, dh2b, hid, h1, wg_mi, wg_mo, norm_mlp_g)
    own_mlp, sums_mlp = pair_reduce(_wgrads_mlp(m, dpre, hid, dh2b), names[4:])
    (da1, dp1, dgates, da0, dag1, dag2, dd0, dd1, dd2, dmixed, dqp, dscale), landed_mlp = _mixer_bwd(
        dh1b, a, p, mixed, gates, wg_out, wg_ao, wg_po, wbd, pool_scale, stat_reduce, sums_mlp)
    g_mi, g_mo = chip_reduce(own_mlp, landed_mlp, names[4:])
    grads_mixer, g_bd = _wgrads_mixer(a, da1, p, dp1, merged, dh1b, pooled, dmixed)

    zero = jnp.zeros((D_MODEL,), F32)
    g_grp = jnp.stack([g_bd[k * POOL_GROUP_W:(k + 1) * POOL_GROUP_W, k * POOL_GROUP_W:(k + 1) * POOL_GROUP_W]
                       for k in range(len(POOL_WINDOWS))])
    small = _pack_small(g_grp, dscale, zero, dg_mlp, dg_final, loss_lanes)
    *dqkv0, small_all = _attn_bwd(qkv[0], da0, lt0, dd0, 0, packs=_place_own(small, N_DEV, 2 * chip + core))
    dqkv = [dqkv0, _attn_bwd(qkv[1], dag1, lt1, dd1, 1), _attn_bwd(qkv[2], dag2, lt2, dd2, 2)]
    dz = _dz_assemble(dqkv, dqp)
    own_in, sums_in = pair_reduce([_wgrad_in(u, dz, dgates)] + grads_mixer, names[:4])
    dx, dg_mix, landed_in = _inproj_dx(dz, dgates, dh1, xs, norm_mix_g, wg_in, sums_in)
    g_in, g_ao, g_po, g_out = chip_reduce(own_in, landed_in, names[:4])
    late = jnp.pad(dg_mix, ((0, 7), (0, 0)))
    full, late_all = _finish_exchange([g_in, g_ao, g_po, g_out, g_mi, g_mo], _place_own(late, N_DEV, 2 * chip + core))

    small_w = _pack_small(w_pool_grp[0], pool_scale, norm_mix_g, norm_mlp_g, norm_final_g, zero)
    small_m = _pack_small(m_w_pool_grp[0], m_pool_scale, m_norm_mix_g, m_norm_mlp_g, m_norm_final_g, zero)
    small_v = _pack_small(v_w_pool_grp[0], v_pool_scale, v_norm_mix_g, v_norm_mlp_g, v_norm_final_g, zero)
    sg, sd, sm, sv, loss_tile = _small_sum_adamw(small_all, late_all, small_w, small_m, small_v)
    full = [full[0].reshape(CHUNKS_PER_SHARD, D_MODEL, CHUNK)] + list(full[1:])
    upd = [_adamw(w, g, mm, vv, f"adamw_{nm}") for w, g, mm, vv, nm in zip(big, full, big_m, big_v, names)]

    def ordered(small_pack, bigs):
        grp, scale, g_mix, g_mlp, g_f = _unpack_small(small_pack)
        b_in, b_ao, b_po, b_out, b_mi, b_mo = [b[None] for b in bigs]
        return (g_mix, b_in, b_ao, grp, scale, b_po, b_out, g_mlp, b_mi, b_mo, g_f)

    return (loss_tile[0, 0], dx[None],
            *ordered(sg, [t[0] for t in upd]),
            *ordered(sd, [t[1] for t in upd]),
            *ordered(sm, [t[2] for t in upd]),
            *ordered(sv, [t[3] for t in upd]))
```

```python
import functools

import jax
import jax.numpy as jnp
from jax import lax
from jax.experimental import pallas as pl
from jax.experimental.pallas import tpu as pltpu

F32 = jnp.float32
BF16 = jnp.bfloat16
SDS = jax.ShapeDtypeStruct
MESH = pl.DeviceIdType.MESH

D_MODEL = 1024
D_FF = 4096
N_CHIPS = 4
N_DEV = 8
DILATIONS = (1, 4, 16)
BAND = 128
GROUP_W = 256
PAIR_W = 128
HEAD_W = 64
STAT_W = 128
STAT_HEAD_W = 32
POOL_W = 768
POOL_GROUP_W = 192
POOL_WINDOWS = (2, 4, 8, 16)
POOL_HALO = 16
N_IN = 5120
CHUNK = 256
N_CHUNKS = N_IN // CHUNK
N_DZ_CHUNKS = 12
CHUNKS_PER_SHARD = 5
WGRAD_IN_GROUP = 4
NORM_EPS = 1e-6
ALIBI_MAX_BIAS = 8.0
N_HEADS = 12
NEG = -1e30

ADAM_LR, ADAM_B1, ADAM_B2, ADAM_EPS, ADAM_WD, ADAM_STEP = 0.001, 0.9, 0.999, 1e-08, 0.01, 10

TM = 512
TMB = 512
TMZ = 1024
ATT_TILE = ((1, 16), (4, 4), (16, 1))
BK = 4096
ELEMENTWISE_BLOCK = 1 << 20
VMEM_LIMIT = 56 * 1024 * 1024
PACK_ROWS = 184
PACK_LATE_ROW = 152

NT = (((1,), (1,)), ((), ()))
TN = (((0,), (0,)), ((), ()))


def _cp(*sem):
    return pltpu.CompilerParams(dimension_semantics=sem, vmem_limit_bytes=VMEM_LIMIT)


def _resident(shape):
    nd = len(shape)
    return pl.BlockSpec(shape, lambda *_: (0,) * nd, pipeline_mode=pl.Buffered(1))


def _row_block(rows, cap=256):
    return max(b for b in range(16, min(rows, cap) + 1, 16) if rows % b == 0)


def _dot(a, b):
    return jnp.dot(a, b, preferred_element_type=F32)


def _dot_nt(a, b):
    return lax.dot_general(a, b, NT, preferred_element_type=F32)


def _dot_tn(a, b):
    return lax.dot_general(a, b, TN, preferred_element_type=F32)


def _w_in_chunk(w_ref, n):
    return w_ref[n // CHUNKS_PER_SHARD, :, (n % CHUNKS_PER_SHARD) * CHUNK:(n % CHUNKS_PER_SHARD + 1) * CHUNK]


def _sigmoid(x):
    return 0.5 * jnp.tanh(0.5 * x.astype(F32)) + 0.5


def _rms_fwd(x, g):
    r = lax.rsqrt(jnp.mean(x * x, axis=-1, keepdims=True) + NORM_EPS)
    xh = x * r
    return xh * g, xh, r


def _rms_bwd(dy, xh, r, g):
    dxh = dy * g
    return r * (dxh - xh * jnp.mean(dxh * xh, axis=-1, keepdims=True))


def _per_head_lanes(cols):
    rows = cols[0].shape[0]
    lane = lax.broadcasted_iota(jnp.int32, (rows, STAT_W), 1)
    out = cols[3]
    for h in (2, 1, 0):
        out = jnp.where(lane < (h + 1) * STAT_HEAD_W, cols[h], out)
    return out


def _head_col(stat, h):
    return stat[:, h * STAT_HEAD_W:h * STAT_HEAD_W + 1]


def _stat_matrices():
    s = lax.broadcasted_iota(jnp.int32, (STAT_W, GROUP_W), 0)
    c = lax.broadcasted_iota(jnp.int32, (STAT_W, GROUP_W), 1)
    expand = (s == (c // HEAD_W) * STAT_HEAD_W).astype(BF16)
    reduce = (s // STAT_HEAD_W == c // HEAD_W).astype(BF16).T
    return expand, reduce


def _dot_split(x, m):
    hi = x.astype(BF16)
    lo = (x - hi.astype(F32)).astype(BF16)
    return _dot(hi, m) + _dot(lo, m)


def _deinterleave_store(val, s_ref, out_ref, lead, d, rows, dtype):
    if d == 1:
        out_ref[lead + (0,)] = val.astype(dtype)
        return
    for h in range(2):
        s_ref[h] = val[:, h * PAIR_W:(h + 1) * PAIR_W]
    for r in range(d):
        for h in range(2):
            out_ref[lead + (r, slice(None), slice(h * PAIR_W, (h + 1) * PAIR_W))] = (
                s_ref[h, pl.ds(r, rows // d, stride=d), :].astype(dtype))


def _interleave_load(in_ref, lead, s_ref, d, rows):
    for r in range(d):
        for h in range(2):
            s_ref[h, pl.ds(r, rows // d, stride=d), :] = (
                in_ref[lead + (r, slice(None), slice(h * PAIR_W, (h + 1) * PAIR_W))].astype(F32))


def _norm_inproj_own(x, g, w_own, buf):
    S = x.shape[0]
    tm = min(TMZ, S)
    n_tiles = S // tm

    def body(x_ref, g_ref, w_ref, shard_ref, buf_in, u_ref, z_ref, buf_ref, send_sem, recv_sem):
        i = pl.program_id(0)

        def copies():
            return _weight_half_copies([shard_ref], [buf_ref], [w_own.shape[0]], send_sem, recv_sem)

        @pl.when(i == 0)
        def _():
            for cpy in copies():
                cpy.start()

        u = _rms_fwd(x_ref[...], g_ref[...])[0].astype(BF16)
        u_ref[...] = u
        for t in range(CHUNKS_PER_SHARD):
            z_ref[t] = _dot(u, w_ref[:, t * CHUNK:(t + 1) * CHUNK]).astype(BF16)

        @pl.when(i == n_tiles - 1)
        def _():
            for cpy in copies():
                cpy.wait()

    row = lambda w: pl.BlockSpec((tm, w), lambda i: (i, 0))
    return pl.pallas_call(
        body, grid=(n_tiles,), name="norm_inproj_own",
        in_specs=[row(D_MODEL), _resident((1, D_MODEL)), _resident(w_own.shape), ANY, ANY],
        out_specs=[row(D_MODEL), pl.BlockSpec((CHUNKS_PER_SHARD, tm, CHUNK), lambda i: (0, i, 0)), ANY],
        out_shape=[SDS((S, D_MODEL), BF16), SDS((CHUNKS_PER_SHARD, S, CHUNK), BF16), SDS(buf.shape, buf.dtype)],
        scratch_shapes=[pltpu.SemaphoreType.DMA((3,)), pltpu.SemaphoreType.DMA((3,))],
        input_output_aliases={4: 2},
        compiler_params=_cp("arbitrary"),
    )(x, g, w_own, w_own, buf)


def _hosted_allgather(i, n_steps, shard_refs, buf_refs, rows, sems):
    send_sem, recv_sem, fsend_sem, frecv_sem = sems
    ici = lambda: _weight_half_copies(shard_refs, buf_refs, rows, send_sem, recv_sem)
    forward = lambda: _pair_forward_copies(buf_refs, rows, fsend_sem, frecv_sem)

    def begin():
        @pl.when(i == 0)
        def _():
            for cpy in ici():
                cpy.start()

        @pl.when(i == n_steps // 2)
        def _():
            for cpy, (fwd, _) in zip(ici(), forward()):
                cpy.wait_recv()
                fwd.start()

    def end():
        @pl.when(i == n_steps - 1)
        def _():
            for cpy, (fwd, landing) in zip(ici(), forward()):
                landing.wait_recv()
                fwd.wait_send()
                cpy.wait_send()

    return begin, end


def _inproj_rest(u, z_own, w_in, shards, bufs):
    S = u.shape[0]
    tm = min(TMZ, S)
    n_tiles = S // tm
    n = len(shards)

    def body(*refs):
        u_ref, zown_ref, w_ref = refs[0:3]
        shard_refs = refs[3:3 + n]
        q0_ref, q1_ref, q2_ref, pz_ref, gate_ref = refs[3 + 2 * n:8 + 2 * n]
        buf_refs = refs[8 + 2 * n:8 + 3 * n]
        s_ref = refs[8 + 3 * n]
        i = pl.program_id(0)
        chip = 2 * lax.axis_index("x") + lax.axis_index("y")
        begin, end = _hosted_allgather(i, n_tiles, shard_refs, buf_refs, [sh.shape[0] for sh in shards],
                                       refs[9 + 3 * n:])
        begin()

        u = u_ref[...]
        qkv_refs = (q0_ref, q1_ref, q2_ref)

        def emit(k, zc):
            if k < 9:
                which, grp = k // 3, k % 3
                if which == 0:
                    zc = zc * 0.125
                _deinterleave_store(zc, s_ref, qkv_refs[grp], (which,), DILATIONS[grp], tm, BF16)
            elif k < N_DZ_CHUNKS:
                pz_ref[:, (k - 9) * CHUNK:(k - 8) * CHUNK] = zc.astype(BF16)
            else:
                gate_ref[:, (k - N_DZ_CHUNKS) * CHUNK:(k - N_DZ_CHUNKS + 1) * CHUNK] = zc.astype(BF16)

        def all_chunks(own_shard):
            for k in range(N_CHUNKS):
                if k // CHUNKS_PER_SHARD == own_shard:
                    emit(k, zown_ref[k % CHUNKS_PER_SHARD].astype(F32))
                else:
                    emit(k, _dot(u, _w_in_chunk(w_ref, k)))

        for shard in range(N_CHIPS):
            pl.when(chip == shard)(functools.partial(all_chunks, shard))
        end()

    row = lambda w: pl.BlockSpec((tm, w), lambda i: (i, 0))
    res = pl.pallas_call(
        body, grid=(n_tiles,), name="inproj_rest",
        in_specs=[row(D_MODEL), pl.BlockSpec((CHUNKS_PER_SHARD, tm, CHUNK), lambda i: (0, i, 0)),
                  _resident(w_in.shape)] + [ANY] * (2 * n),
        out_specs=[pl.BlockSpec((3, d, tm // d, GROUP_W), lambda i: (0, 0, i, 0)) for d in DILATIONS]
        + [row(POOL_W), row(2 * D_MODEL)] + [ANY] * n,
        out_shape=[SDS((3, d, S // d, GROUP_W), BF16) for d in DILATIONS]
        + [SDS((S, POOL_W), BF16), SDS((S, 2 * D_MODEL), BF16)] + [SDS(b.shape, b.dtype) for b in bufs],
        scratch_shapes=[pltpu.VMEM((2, tm, PAIR_W), F32)] + [pltpu.SemaphoreType.DMA((3 * n,))] * 4,
        input_output_aliases={3 + n + w: 5 + w for w in range(n)},
        compiler_params=_cp("arbitrary"),
    )(u, z_own, w_in, *shards, *bufs)
    return res[:5], res[5:]


def _band_bias(grp, d):
    row = lax.broadcasted_iota(jnp.int32, (BAND, 2 * BAND), 0)
    col = lax.broadcasted_iota(jnp.int32, (BAND, 2 * BAND), 1)
    steps = BAND + row - col
    valid = (steps >= 0) & (steps <= BAND)
    stepsf = (steps * d).astype(F32)
    biases = []
    for hh in range(4):
        slope = 2.0 ** (-ALIBI_MAX_BIAS * (grp * 4 + hh + 1) / N_HEADS)
        biases.append(jnp.where(valid, -slope * stepsf, NEG))
    return biases, col


def _attn_tiles(grp, L):
    rr, rb = ATT_TILE[grp]
    rb = min(rb, L // BAND)
    return rr, rb, L // (rb * BAND)


def _kv_tile(cur_ref, prev_ref, rr, rb, cs):
    if rb == 0:
        return jnp.concatenate([prev_ref[rr, :, cs], cur_ref[rr, 0:BAND, cs]], axis=0)
    return cur_ref[rr, (rb - 1) * BAND:(rb + 1) * BAND, cs]


def _attn_fwd(qkv, grp):
    d = DILATIONS[grp]
    L = qkv.shape[2]
    RR, RB, nb = _attn_tiles(grp, L)

    def body(q_ref, kc_ref, kp_ref, vc_ref, vp_ref, o_ref, lse_ref):
        i = pl.program_id(0)
        biases, col = _band_bias(grp, d)
        first_keys_ok = (col >= BAND) | (i > 0)
        is_a = lax.broadcasted_iota(jnp.int32, (BAND, PAIR_W), 1) < HEAD_W
        heads = [(rr, rb, cp, h2) for rr in range(RR) for rb in range(RB) for cp in range(2) for h2 in range(2)]

        def tile(head):
            rr, rb, cp, _ = head
            return rr, rb, slice(rb * BAND, (rb + 1) * BAND), slice(cp * PAIR_W, (cp + 1) * PAIR_W)

        def scores(head):
            rr, rb, rows, cs = tile(head)
            q2 = q_ref[rr, rows, cs]
            b = biases[head[2] * 2 + head[3]]
            if rb == 0:
                b = jnp.where(first_keys_ok, b, NEG)
            sel = is_a if head[3] == 0 else jnp.logical_not(is_a)
            return _dot_nt(jnp.where(sel, q2, jnp.zeros_like(q2)), _kv_tile(kc_ref, kp_ref, rr, rb, cs)) + b

        s_next = scores(heads[0])
        outs, lses = {}, {}
        for idx, head in enumerate(heads):
            s = s_next
            if idx + 1 < len(heads):
                s_next = scores(heads[idx + 1])
            rr, rb, rows, cs = tile(head)
            m = jnp.max(s, axis=-1, keepdims=True)
            p = jnp.exp(s - m)
            l = jnp.sum(p, axis=-1, keepdims=True)
            outs[head[3]] = _dot(p.astype(BF16), _kv_tile(vc_ref, vp_ref, rr, rb, cs)) * (1.0 / l)
            lses[head[2] * 2 + head[3]] = m + jnp.log(l)
            if head[3] == 1:
                o_ref[rr, rows, cs] = jnp.where(is_a, outs[0], outs[1]).astype(BF16)
            if head[2] == 1 and head[3] == 1:
                lse_ref[rr, rows, :] = _per_head_lanes(lses)

    cur = lambda w: pl.BlockSpec((None, RR, RB * BAND, GROUP_W), lambda i, j: (w, j, i, 0))
    prev = lambda w: pl.BlockSpec((None, RR, BAND, GROUP_W), lambda i, j: (w, j, jnp.maximum(i * RB - 1, 0), 0))
    return pl.pallas_call(
        body, grid=(nb, d // RR), name=f"attn_fwd_g{grp}",
        in_specs=[cur(0), cur(1), prev(1), cur(2), prev(2)],
        out_specs=[pl.BlockSpec((RR, RB * BAND, GROUP_W), lambda i, j: (j, i, 0)),
                   pl.BlockSpec((RR, RB * BAND, STAT_W), lambda i, j: (j, i, 0))],
        out_shape=[SDS((d, L, GROUP_W), BF16), SDS((d, L, STAT_W), F32)],
        compiler_params=_cp("parallel", "parallel"),
    )(qkv, qkv, qkv, qkv, qkv)


def _pool_column_select(col, vals):
    return jnp.where(col < POOL_GROUP_W, vals[0],
                     jnp.where(col < 2 * POOL_GROUP_W, vals[1],
                               jnp.where(col < 3 * POOL_GROUP_W, vals[2], vals[3])))


def _pool_inv_count(i, rows):
    t = i * rows + lax.broadcasted_iota(jnp.int32, (rows, POOL_W), 0)
    col = lax.broadcasted_iota(jnp.int32, (rows, POOL_W), 1)
    win = _pool_column_select(col, POOL_WINDOWS)
    return 1.0 / jnp.minimum(t + 1, win).astype(F32), col


def _mixer_out(outs, lses, pz, gates, x, w_ao, w_po, wbd, scale, w_out, g_mlp, expand, shards, bufs):
    S = x.shape[0]
    n_tiles = S // TMB
    n = len(shards)

    def body(*refs):
        (o0_ref, l0_ref, o1_ref, l1_ref, o2_ref, l2_ref, pz_ref, halo_ref, gate_ref, x_ref,
         wao_ref, wpo_ref, wbd_ref, sc_ref, wout_ref, g_ref, expand_ref) = refs[0:17]
        shard_refs = refs[17:17 + n]
        (a_ref, lt0_ref, lt1_ref, lt2_ref, pooled_ref, mixed_ref, p_ref, merged_ref, h1_ref,
         m_ref) = refs[17 + 2 * n:27 + 2 * n]
        buf_refs = refs[27 + 2 * n:27 + 3 * n]
        so1, sl1, so2, sl2, slt, ext_ref = refs[27 + 3 * n:33 + 3 * n]
        i = pl.program_id(0)
        begin, end = _hosted_allgather(i, n_tiles, shard_refs, buf_refs, [sh.shape[0] for sh in shards],
                                       refs[33 + 3 * n:])
        begin()
        _interleave_load(o1_ref, (), so1, DILATIONS[1], TMB)
        _interleave_load(o2_ref, (), so2, DILATIONS[2], TMB)
        for ref, sref, d in ((l1_ref, sl1, DILATIONS[1]), (l2_ref, sl2, DILATIONS[2])):
            for r in range(d):
                sref[0, pl.ds(r, TMB // d, stride=d), :] = ref[r]
        l0, l1, l2 = l0_ref[0], sl1[0], sl2[0]
        mx = jnp.maximum(jnp.maximum(l0, l1), l2)
        e0, e1, e2 = jnp.exp(l0 - mx), jnp.exp(l1 - mx), jnp.exp(l2 - mx)
        den = e0 + e1 + e2
        inv = 1.0 / den
        slt[0] = mx + jnp.log(den)
        w0, w1, w2 = [_dot_split(e * inv, expand_ref[...]) for e in (e0, e1, e2)]
        for h in range(2):
            hs = slice(h * PAIR_W, (h + 1) * PAIR_W)
            a_ref[:, hs] = (w0[:, hs] * o0_ref[0, :, hs].astype(F32) + w1[:, hs] * so1[h]
                            + w2[:, hs] * so2[h]).astype(BF16)
        lt0_ref[0] = slt[0]
        for ref, d in ((lt1_ref, DILATIONS[1]), (lt2_ref, DILATIONS[2])):
            for r in range(d):
                ref[r] = slt[0, pl.ds(r, TMB // d, stride=d), :]

        pz_t = pz_ref[...].astype(F32)
        ext_ref[0:POOL_HALO, :] = jnp.where(i > 0, halo_ref[...].astype(F32), 0.0)
        ext_ref[POOL_HALO:, :] = pz_t
        sums = []
        acc = ext_ref[...]
        for k in (1, 2, 4, 8):
            acc = acc + pltpu.roll(acc, k, 0)
            sums.append(acc[POOL_HALO:, :])
        inv_cnt, col = _pool_inv_count(i, TMB)
        pooled = (_pool_column_select(col, sums) * inv_cnt - pz_t).astype(BF16)
        pooled_ref[...] = pooled
        mixed = _dot(pooled, wbd_ref[...])
        mixed_ref[...] = mixed.astype(BF16)
        p = (mixed * sc_ref[...]).astype(BF16)
        p_ref[...] = p

        a = a_ref[...]
        for j in range(N_CHIPS):
            js = slice(j * CHUNK, (j + 1) * CHUNK)
            ga = gate_ref[:, js]
            gp = gate_ref[:, D_MODEL + j * CHUNK:D_MODEL + (j + 1) * CHUNK]
            mj = _sigmoid(ga) * _dot(a, wao_ref[j]) + _sigmoid(gp) * _dot(p, wpo_ref[j])
            merged_ref[:, js] = mj.astype(BF16)
        h1 = x_ref[...] + _dot(merged_ref[...], wout_ref[...])
        h1_ref[...] = h1
        m_ref[...] = _rms_fwd(h1, g_ref[...])[0].astype(BF16)
        end()

    row = lambda w: pl.BlockSpec((TMB, w), lambda i: (i, 0))
    grp_spec = lambda d: pl.BlockSpec((d, TMB // d, GROUP_W), lambda i: (0, i, 0))
    stat_spec = lambda d: pl.BlockSpec((d, TMB // d, STAT_W), lambda i: (0, i, 0))
    halo = pl.BlockSpec((POOL_HALO, POOL_W), lambda i: (jnp.maximum(i * (TMB // POOL_HALO) - 1, 0), 0))
    d0, d1, d2 = DILATIONS
    pair_scratch = pltpu.VMEM((2, TMB, PAIR_W), F32)
    stat_scratch = pltpu.VMEM((1, TMB, STAT_W), F32)
    res = pl.pallas_call(
        body, grid=(n_tiles,), name="mixer_out",
        in_specs=[grp_spec(d0), stat_spec(d0), grp_spec(d1), stat_spec(d1), grp_spec(d2), stat_spec(d2),
                  row(POOL_W), halo, row(2 * D_MODEL), row(D_MODEL),
                  _resident(w_ao.shape), _resident(w_po.shape), _resident(wbd.shape), _resident(scale.shape),
                  _resident(w_out.shape), _resident(g_mlp.shape), _resident(expand.shape)] + [ANY] * (2 * n),
        out_specs=[row(GROUP_W), stat_spec(d0), stat_spec(d1), stat_spec(d2),
                   row(POOL_W), row(POOL_W), row(POOL_W), row(D_MODEL), row(D_MODEL), row(D_MODEL)] + [ANY] * n,
        out_shape=[SDS((S, GROUP_W), BF16)] + [SDS((d, S // d, STAT_W), F32) for d in DILATIONS]
        + [SDS((S, POOL_W), BF16), SDS((S, POOL_W), BF16), SDS((S, POOL_W), BF16),
           SDS((S, D_MODEL), BF16), SDS((S, D_MODEL), F32), SDS((S, D_MODEL), BF16)]
        + [SDS(b.shape, b.dtype) for b in bufs],
        scratch_shapes=[pair_scratch, stat_scratch, pair_scratch, stat_scratch, stat_scratch,
                        pltpu.VMEM((TMB + POOL_HALO, POOL_W), F32)] + [pltpu.SemaphoreType.DMA((3 * n,))] * 4,
        input_output_aliases={17 + n + w: 10 + w for w in range(n)},
        compiler_params=_cp("arbitrary"),
    )(outs[0], lses[0], outs[1], lses[1], outs[2], lses[2], pz, pz, gates, x,
      w_ao, w_po, wbd, scale, w_out, g_mlp, expand, *shards, *bufs)
    return res[:10], res[10:]


def _mlp_fwd_loss(m, h1, target, w_mi, w_mo, g_f):
    S = m.shape[0]

    def body(m_ref, h1_ref, t_ref, wmi_ref, wmo_ref, g_ref, hid_ref, dh2_ref, dh2b_ref, loss_ref, dg_ref):
        @pl.when(pl.program_id(0) == 0)
        def _():
            loss_ref[...] = jnp.zeros_like(loss_ref)
            dg_ref[...] = jnp.zeros_like(dg_ref)

        mt = m_ref[...]
        acc = h1_ref[...]
        for c in range(N_CHIPS):
            hid = jnp.square(jnp.maximum(_dot(mt, wmi_ref[c]), 0.0)).astype(BF16)
            hid_ref[:, c * D_MODEL:(c + 1) * D_MODEL] = hid
            acc = acc + _dot(hid, wmo_ref[c])
        g = g_ref[...]
        y, hh, r = _rms_fwd(acc, g)
        e = y - t_ref[...]
        loss_ref[...] += jnp.sum(e * e, axis=0, keepdims=True)
        dy = e * (1.0 / D_MODEL)
        dg_ref[...] += jnp.sum(dy * hh, axis=0, keepdims=True)
        dh2 = _rms_bwd(dy, hh, r, g)
        dh2_ref[...] = dh2
        dh2b_ref[...] = dh2.astype(BF16)

    row = lambda w: pl.BlockSpec((TM, w), lambda i: (i, 0))
    vec = pl.BlockSpec((1, D_MODEL), lambda i: (0, 0))
    return pl.pallas_call(
        body, grid=(S // TM,), name="mlp_fwd_loss",
        in_specs=[row(D_MODEL), row(D_MODEL), row(D_MODEL), _resident(w_mi.shape), _resident(w_mo.shape),
                  _resident(g_f.shape)],
        out_specs=[row(D_FF), row(D_MODEL), row(D_MODEL), vec, vec],
        out_shape=[SDS((S, D_FF), BF16), SDS((S, D_MODEL), F32), SDS((S, D_MODEL), BF16),
                   SDS((1, D_MODEL), F32), SDS((1, D_MODEL), F32)],
        compiler_params=_cp("arbitrary"),
    )(m, h1, target, w_mi, w_mo, g_f)


def _mlp_bwd(dh2, dh2b, hid, h1, w_mi, w_mo, g_mlp):
    S = dh2.shape[0]

    def body(dh2_ref, dh2b_ref, hid_ref, h1_ref, wmi_ref, wmo_ref, g_ref, dpre_ref, dh1_ref, dh1b_ref, dg_ref):
        @pl.when(pl.program_id(0) == 0)
        def _():
            dg_ref[...] = jnp.zeros_like(dg_ref)

        d2 = dh2b_ref[...]
        dm = jnp.zeros((TM, D_MODEL), F32)
        dhid_next = _dot_nt(d2, wmo_ref[0])
        for c in range(N_CHIPS):
            cs = slice(c * D_MODEL, (c + 1) * D_MODEL)
            dhid = dhid_next
            if c + 1 < N_CHIPS:
                dhid_next = _dot_nt(d2, wmo_ref[c + 1])
            dpre = (dhid * (2.0 * jnp.sqrt(hid_ref[:, cs].astype(F32)))).astype(BF16)
            dpre_ref[:, cs] = dpre
            dm = dm + _dot_nt(dpre, wmi_ref[c])
        g = g_ref[...]
        _, hh, r = _rms_fwd(h1_ref[...], g)
        dg_ref[...] += jnp.sum(dm * hh, axis=0, keepdims=True)
        dh1 = dh2_ref[...] + _rms_bwd(dm, hh, r, g)
        dh1_ref[...] = dh1
        dh1b_ref[...] = dh1.astype(BF16)

    row = lambda w: pl.BlockSpec((TM, w), lambda i: (i, 0))
    return pl.pallas_call(
        body, grid=(S // TM,), name="mlp_bwd",
        in_specs=[row(D_MODEL), row(D_MODEL), row(D_FF), row(D_MODEL), _resident(w_mi.shape),
                  _resident(w_mo.shape), _resident(g_mlp.shape)],
        out_specs=[row(D_FF), row(D_MODEL), row(D_MODEL), pl.BlockSpec((1, D_MODEL), lambda i: (0, 0))],
        out_shape=[SDS((S, D_FF), BF16), SDS((S, D_MODEL), F32), SDS((S, D_MODEL), BF16), SDS((1, D_MODEL), F32)],
        compiler_params=_cp("arbitrary"),
    )(dh2, dh2b, hid, h1, w_mi, w_mo, g_mlp)


def _mixer_bwd(dh1b, a, p, mixed, gates, w_out, w_ao, w_po, wbd, scale, stat_reduce, sums):
    S = a.shape[0]
    n_tiles = S // TMB
    n = len(sums)

    def body(*refs):
        (dh1b_ref, a_ref, p_ref, mixed_ref, gate_ref, wout_ref, wao_ref, wpo_ref, wbd_ref, sc_ref,
         ones_ref) = refs[0:11]
        sum_refs = refs[11:11 + n]
        (da1_ref, dp1_ref, dgate_ref, da0_ref, dag1_ref, dag2_ref, dd0_ref, dd1_ref, dd2_ref,
         dmixed_ref, dqp_ref, dscale_ref) = refs[11 + n:23 + n]
        land_refs = refs[23 + n:23 + 2 * n]
        s_da, s_dd, send_sem, recv_sem = refs[23 + 2 * n:]
        i = pl.program_id(0)

        @pl.when(i == 0)
        def _():
            dscale_ref[...] = jnp.zeros_like(dscale_ref)
            for cpy in _chip_sum_copies(sum_refs, land_refs, send_sem, recv_sem):
                cpy.start()

        dmerged = _dot_nt(dh1b_ref[...], wout_ref[...])
        a = a_ref[...]
        p = p_ref[...]
        da = jnp.zeros((TMB, GROUP_W), F32)
        dp = jnp.zeros((TMB, POOL_W), F32)
        for j in range(N_CHIPS):
            js = slice(j * CHUNK, (j + 1) * CHUNK)
            sa = _sigmoid(gate_ref[:, js])
            sp = _sigmoid(gate_ref[:, D_MODEL + j * CHUNK:D_MODEL + (j + 1) * CHUNK])
            dmj = dmerged[:, js]
            da1 = (dmj * sa).astype(BF16)
            dp1 = (dmj * sp).astype(BF16)
            da1_ref[:, js] = da1
            dp1_ref[:, js] = dp1
            dgate_ref[j] = (dmj * _dot(a, wao_ref[j]) * sa * (1.0 - sa)).astype(BF16)
            dgate_ref[N_CHIPS + j] = (dmj * _dot(p, wpo_ref[j]) * sp * (1.0 - sp)).astype(BF16)
            da = da + _dot_nt(da1, wao_ref[j])
            dp = dp + _dot_nt(dp1, wpo_ref[j])

        dd = _dot_split(da * a.astype(F32), ones_ref[...])
        da0_ref[0] = da.astype(BF16)
        dd0_ref[0] = dd
        for h in range(2):
            s_da[h] = da[:, h * PAIR_W:(h + 1) * PAIR_W]
        s_dd[0] = dd
        for refs, d in (((dag1_ref, dd1_ref), DILATIONS[1]), ((dag2_ref, dd2_ref), DILATIONS[2])):
            for r in range(d):
                for h in range(2):
                    hs = slice(h * PAIR_W, (h + 1) * PAIR_W)
                    refs[0][r, :, hs] = s_da[h, pl.ds(r, TMB // d, stride=d), :].astype(BF16)
                refs[1][r] = s_dd[0, pl.ds(r, TMB // d, stride=d), :]

        sc = sc_ref[...]
        dscale_ref[...] += jnp.sum(dp * mixed_ref[...].astype(F32), axis=0, keepdims=True)
        dmixed = (dp * sc).astype(BF16)
        dmixed_ref[...] = dmixed
        inv_cnt, _ = _pool_inv_count(i, TMB)
        dqp_ref[...] = (_dot_nt(dmixed, wbd_ref[...]) * inv_cnt).astype(BF16)

        @pl.when(i == n_tiles - 1)
        def _():
            for cpy in _chip_sum_copies(sum_refs, land_refs, send_sem, recv_sem):
                cpy.wait()

    row = lambda w: pl.BlockSpec((TMB, w), lambda i: (i, 0))
    grp_spec = lambda d: pl.BlockSpec((d, TMB // d, GROUP_W), lambda i: (0, i, 0))
    stat_spec = lambda d: pl.BlockSpec((d, TMB // d, STAT_W), lambda i: (0, i, 0))
    d0, d1, d2 = DILATIONS
    res = pl.pallas_call(
        body, grid=(n_tiles,), name="mixer_bwd",
        in_specs=[row(D_MODEL), row(GROUP_W), row(POOL_W), row(POOL_W), row(2 * D_MODEL),
                  _resident(w_out.shape), _resident(w_ao.shape), _resident(w_po.shape), _resident(wbd.shape),
                  _resident(scale.shape), _resident(stat_reduce.shape)] + [ANY] * n,
        out_specs=[row(D_MODEL), row(D_MODEL), pl.BlockSpec((2 * N_CHIPS, TMB, CHUNK), lambda i: (0, i, 0)),
                   grp_spec(d0), grp_spec(d1), grp_spec(d2), stat_spec(d0), stat_spec(d1), stat_spec(d2),
                   row(POOL_W), row(POOL_W), pl.BlockSpec((1, POOL_W), lambda i: (0, 0))] + [ANY] * n,
        out_shape=[SDS((S, D_MODEL), BF16), SDS((S, D_MODEL), BF16), SDS((2 * N_CHIPS, S, CHUNK), BF16)]
        + [SDS((d, S // d, GROUP_W), BF16) for d in DILATIONS]
        + [SDS((d, S // d, STAT_W), F32) for d in DILATIONS]
        + [SDS((S, POOL_W), BF16), SDS((S, POOL_W), BF16), SDS((1, POOL_W), F32)]
        + [SDS(t.shape, t.dtype) for t in sums],
        scratch_shapes=[pltpu.VMEM((2, TMB, PAIR_W), F32), pltpu.VMEM((1, TMB, STAT_W), F32),
                        pltpu.SemaphoreType.DMA((3 * n,)), pltpu.SemaphoreType.DMA((3 * n,))],
        compiler_params=_cp("arbitrary"),
    )(dh1b, a, p, mixed, gates, w_out, w_ao, w_po, wbd, scale, stat_reduce, *sums)
    return res[:12], res[12:]


def _attn_bwd(qkv, da, lt, dd, grp, packs=None):
    d = DILATIONS[grp]
    L = qkv.shape[2]
    RR, RB, nb = _attn_tiles(grp, L)
    n_j = d // RR
    hosted = packs is not None

    def body(*refs):
        q_ref, kc_ref, kp_ref, vc_ref, vp_ref, da_ref, lt_ref, dd_ref = refs[0:8]
        dq_ref, dk_ref, dv_ref = refs[8 + hosted:11 + hosted]
        dk_acc, dv_acc = refs[11 + 2 * hosted:13 + 2 * hosted]
        i = pl.program_id(1)
        if hosted:
            j = pl.program_id(0)
            start, relay, finish = _pack_allgather(refs[11 + hosted], *refs[13 + 2 * hosted:])
            pl.when((j == 0) & (i == 0))(start)
            pl.when((j == 0) & (i == nb // 2))(relay)

        @pl.when(i == 0)
        def _():
            dk_acc[...] = jnp.zeros_like(dk_acc)
            dv_acc[...] = jnp.zeros_like(dv_acc)

        def compute(cur, prv):
            dk_acc[cur] = jnp.zeros((RR, RB * BAND, GROUP_W), F32)
            dv_acc[cur] = jnp.zeros((RR, RB * BAND, GROUP_W), F32)
            biases, col = _band_bias(grp, d)
            first_keys_ok = (col >= BAND) | (i > 0)
            is_a = lax.broadcasted_iota(jnp.int32, (BAND, PAIR_W), 1) < HEAD_W
            for rr in range(RR):
                for rb in range(RB):
                    rows = slice(rb * BAND, (rb + 1) * BAND)
                    for cp in range(2):
                        cs = slice(cp * PAIR_W, (cp + 1) * PAIR_W)
                        q2 = q_ref[rr, rows, cs]
                        da2 = da_ref[rr, rows, cs]
                        lt2 = lt_ref[rr, rows, :]
                        dd2 = dd_ref[rr, rows, :]
                        kcat = _kv_tile(kc_ref, kp_ref, rr, rb, cs)
                        vcat = _kv_tile(vc_ref, vp_ref, rr, rb, cs)
                        q2t = q2.astype(F32).T.astype(BF16)
                        da2t = da2.astype(F32).T.astype(BF16)
                        dqs, dkts, dvts, scores, dpvs = [], [], [], [], []
                        for h2 in range(2):
                            sel = is_a if h2 == 0 else jnp.logical_not(is_a)
                            b = biases[cp * 2 + h2]
                            if rb == 0:
                                b = jnp.where(first_keys_ok, b, NEG)
                            scores.append(_dot_nt(jnp.where(sel, q2, jnp.zeros_like(q2)), kcat) + b)
                            dpvs.append(_dot_nt(jnp.where(sel, da2, jnp.zeros_like(da2)), vcat))
                        for h2 in range(2):
                            lane0 = h2 * HEAD_W
                            p = jnp.exp(scores[h2] - _head_col(lt2, cp * 2 + h2))
                            ds = (p * (dpvs[h2] - _head_col(dd2, cp * 2 + h2))).astype(BF16)
                            dqs.append(_dot(ds, kcat))
                            dkts.append(_dot(q2t[lane0:lane0 + HEAD_W, :], ds))
                            dvts.append(_dot(da2t[lane0:lane0 + HEAD_W, :], p.astype(BF16)))
                        dq_ref[rr, rows, cs] = (jnp.where(is_a, dqs[0], dqs[1]) * 0.125).astype(BF16)
                        dkc = jnp.concatenate(dkts, axis=0).T
                        dvc = jnp.concatenate(dvts, axis=0).T
                        if rb == 0:
                            last = slice((RB - 1) * BAND, RB * BAND)
                            dk_acc[prv, rr, last, cs] += dkc[0:BAND]
                            dv_acc[prv, rr, last, cs] += dvc[0:BAND]
                            dk_acc[cur, rr, 0:BAND, cs] += dkc[BAND:]
                            dv_acc[cur, rr, 0:BAND, cs] += dvc[BAND:]
                        else:
                            both = slice((rb - 1) * BAND, (rb + 1) * BAND)
                            dk_acc[cur, rr, both, cs] += dkc
                            dv_acc[cur, rr, both, cs] += dvc

        def flush(prv):
            dk_ref[...] = dk_acc[prv].astype(BF16)
            dv_ref[...] = dv_acc[prv].astype(BF16)

        for parity in (0, 1):
            on = (i % 2) == parity
            pl.when(on & (i < nb))(functools.partial(compute, parity, 1 - parity))
            pl.when(on & (i > 0))(functools.partial(flush, 1 - parity))
        if hosted:
            pl.when((j == n_j - 1) & (i == nb))(finish)

    qi = lambda i: jnp.minimum(i, nb - 1)
    cur_w = lambda w: pl.BlockSpec((None, RR, RB * BAND, GROUP_W), lambda j, i: (w, j, qi(i), 0))
    prev_w = lambda w: pl.BlockSpec((None, RR, BAND, GROUP_W),
                                    lambda j, i: (w, j, jnp.maximum(qi(i) * RB - 1, 0), 0))
    blk = pl.BlockSpec((RR, RB * BAND, GROUP_W), lambda j, i: (j, qi(i), 0))
    stat_blk = pl.BlockSpec((RR, RB * BAND, STAT_W), lambda j, i: (j, qi(i), 0))
    late = pl.BlockSpec((RR, RB * BAND, GROUP_W), lambda j, i: (j, jnp.maximum(i - 1, 0), 0))
    extra = [packs] if hosted else []
    return pl.pallas_call(
        body, grid=(n_j, nb + 1), name=f"attn_bwd_g{grp}",
        in_specs=[cur_w(0), cur_w(1), prev_w(1), cur_w(2), prev_w(2), blk, stat_blk, stat_blk] + [ANY] * hosted,
        out_specs=[blk, late, late] + [ANY] * hosted,
        out_shape=[SDS((d, L, GROUP_W), BF16)] * 3 + [SDS(t.shape, t.dtype) for t in extra],
        scratch_shapes=[pltpu.VMEM((2, RR, RB * BAND, GROUP_W), F32), pltpu.VMEM((2, RR, RB * BAND, GROUP_W), F32)]
        + [pltpu.SemaphoreType.DMA((N_DEV - 1,))] * (2 * hosted),
        input_output_aliases={8: 3} if hosted else {},
        compiler_params=_cp("arbitrary" if hosted else "parallel", "arbitrary"),
    )(qkv, qkv, qkv, qkv, qkv, da, lt, dd, *extra)


def _dz_assemble(dqkv, dqp):
    S = dqp.shape[0]
    tm = min(TMZ, S)
    n_tiles = S // tm

    def body(*refs):
        dqkv_refs = refs[0:9]
        dqp_ref, halo_ref = refs[9:11]
        dz_ref, s_ref, ext_ref = refs[11:]
        i = pl.program_id(0)

        for grp in range(3):
            for which in range(3):
                n = which * 3 + grp
                ref = dqkv_refs[grp * 3 + which]
                if DILATIONS[grp] == 1:
                    dz_ref[n] = ref[0]
                else:
                    _interleave_load(ref, (), s_ref, DILATIONS[grp], tm)
                    for h in range(2):
                        dz_ref[n, :, h * PAIR_W:(h + 1) * PAIR_W] = s_ref[h].astype(BF16)

        dqp = dqp_ref[...].astype(F32)
        ext_ref[0:tm, :] = dqp
        ext_ref[tm:, :] = jnp.where(i < n_tiles - 1, halo_ref[...].astype(F32), 0.0)
        sums = []
        acc = ext_ref[...]
        for k in (1, 2, 4, 8):
            acc = acc + pltpu.roll(acc, tm + POOL_HALO - k, 0)
            sums.append(acc[0:tm, :])
        inv_cnt, col = _pool_inv_count(i, tm)
        dpz = _pool_column_select(col, sums) - dqp / inv_cnt
        for t in range(3):
            dz_ref[9 + t] = dpz[:, t * CHUNK:(t + 1) * CHUNK].astype(BF16)

    row = lambda w: pl.BlockSpec((tm, w), lambda i: (i, 0))
    grp_spec = lambda d: pl.BlockSpec((d, tm // d, GROUP_W), lambda i: (0, i, 0))
    halo = pl.BlockSpec((POOL_HALO, POOL_W),
                        lambda i: (jnp.minimum((i + 1) * (tm // POOL_HALO), S // POOL_HALO - 1), 0))
    flat = [t for grp in range(3) for t in dqkv[grp]]
    return pl.pallas_call(
        body, grid=(n_tiles,), name="dz_assemble",
        in_specs=[grp_spec(DILATIONS[grp]) for grp in range(3) for _ in range(3)] + [row(POOL_W), halo],
        out_specs=pl.BlockSpec((N_DZ_CHUNKS, tm, CHUNK), lambda i: (0, i, 0)),
        out_shape=SDS((N_DZ_CHUNKS, S, CHUNK), BF16),
        scratch_shapes=[pltpu.VMEM((2, tm, PAIR_W), F32), pltpu.VMEM((tm + POOL_HALO, POOL_W), F32)],
        compiler_params=_cp("parallel"),
    )(*flat, dqp, dqp)


def _inproj_dx(dz, dgates, dh1, x, g, w_in, sums):
    S = x.shape[0]
    n_tiles = S // TM
    n = len(sums)

    def body(*refs):
        dz_ref, dgate_ref, dh1_ref, x_ref, g_ref, w_ref = refs[0:6]
        sum_refs = refs[6:6 + n]
        dx_ref, dg_ref = refs[6 + n:8 + n]
        land_refs = refs[8 + n:8 + 2 * n]
        sems = refs[8 + 2 * n:]
        i = pl.program_id(0)

        def copies():
            return _chip_sum_copies(sum_refs, land_refs, *sems)

        @pl.when(i == 0)
        def _():
            dg_ref[...] = jnp.zeros_like(dg_ref)
            for cpy in copies():
                cpy.start()

        du = jnp.zeros((TM, D_MODEL), F32)
        for k in range(N_CHUNKS):
            dzk = dz_ref[k] if k < N_DZ_CHUNKS else dgate_ref[k - N_DZ_CHUNKS]
            du = du + _dot_nt(dzk, _w_in_chunk(w_ref, k))
        gv = g_ref[...]
        _, xh, r = _rms_fwd(x_ref[...], gv)
        dg_ref[...] += jnp.sum(du * xh, axis=0, keepdims=True)
        dx_ref[...] = dh1_ref[...] + _rms_bwd(du, xh, r, gv)

        @pl.when(i == n_tiles - 1)
        def _():
            for cpy in copies():
                cpy.wait()

    row = lambda w: pl.BlockSpec((TM, w), lambda i: (i, 0))
    res = pl.pallas_call(
        body, grid=(n_tiles,), name="inproj_dx",
        in_specs=[pl.BlockSpec((N_DZ_CHUNKS, TM, CHUNK), lambda i: (0, i, 0)),
                  pl.BlockSpec((N_CHUNKS - N_DZ_CHUNKS, TM, CHUNK), lambda i: (0, i, 0)),
                  row(D_MODEL), row(D_MODEL), _resident(g.shape), _resident(w_in.shape)] + [ANY] * n,
        out_specs=[row(D_MODEL), pl.BlockSpec((1, D_MODEL), lambda i: (0, 0))] + [ANY] * n,
        out_shape=[SDS((S, D_MODEL), F32), SDS((1, D_MODEL), F32)] + [SDS(t.shape, t.dtype) for t in sums],
        scratch_shapes=[pltpu.SemaphoreType.DMA((3 * n,)), pltpu.SemaphoreType.DMA((3 * n,))],
        compiler_params=_cp("arbitrary"),
    )(dz, dgates, dh1, x, g, w_in, *sums)
    return res[0], res[1], res[2:]


def _wgrad(a, b, name, *, out_shape, a_spec, b_spec, out_spec, grid, n_out_cols=None, fill=None, narrow=True):
    k_axis = len(grid) - 1
    n_k = grid[k_axis]
    n_out = 2 if narrow else 1

    def body(a_ref, b_ref, *rest):
        o_ref = rest[-n_out]

        @pl.when(pl.program_id(k_axis) == 0)
        def _():
            o_ref[...] = jnp.zeros_like(o_ref)

        at = a_ref[...]
        if n_out_cols is None:
            o_ref[...] += _dot_tn(at, b_ref[...])
        elif n_out_cols[0] == "lead_both":
            for t in range(b_ref.shape[0]):
                o_ref[t] += _dot_tn(at, b_ref[t])
        else:
            w = n_out_cols[1]
            for t in range(o_ref.shape[0]):
                o_ref[t] += _dot_tn(at, b_ref[:, t * w:(t + 1) * w])

        if narrow:
            @pl.when(pl.program_id(k_axis) == n_k - 1)
            def _():
                rest[-1][...] = o_ref[...].astype(BF16)

    sem = ("parallel",) * k_axis + ("arbitrary",)
    extra = [] if fill is None else list(fill) if narrow else [fill]
    shapes = [out_shape, SDS(out_shape.shape, BF16)] if narrow else out_shape
    return pl.pallas_call(body, grid=grid, name=name, in_specs=[a_spec, b_spec] + [ANY] * len(extra),
                          out_specs=[out_spec] * n_out if narrow else out_spec, out_shape=shapes,
                          input_output_aliases={2 + t: t for t in range(len(extra))},
                          compiler_params=_cp(*sem))(a, b, *extra)


def _wgrad_in(u, dz, dgates):
    bk = min(BK, u.shape[0])
    nk = u.shape[0] // bk
    g = WGRAD_IN_GROUP
    kw = dict(n_out_cols=("lead_both", CHUNK), a_spec=pl.BlockSpec((bk, D_MODEL), lambda j, k: (k, 0)),
              b_spec=pl.BlockSpec((g, bk, CHUNK), lambda j, k: (j, k, 0)),
              out_shape=SDS((N_CHUNKS, D_MODEL, CHUNK), F32))
    first = _wgrad(u, dz, "wgrad_in_qkvp", grid=(N_DZ_CHUNKS // g, nk),
                   out_spec=pl.BlockSpec((g, D_MODEL, CHUNK), lambda j, k: (j, 0, 0)), **kw)
    both = _wgrad(u, dgates, "wgrad_in_gates", grid=((N_CHUNKS - N_DZ_CHUNKS) // g, nk), fill=first,
                  out_spec=pl.BlockSpec((g, D_MODEL, CHUNK), lambda j, k: (N_DZ_CHUNKS // g + j, 0, 0)), **kw)
    return [t.reshape(N_CHIPS, CHUNKS_PER_SHARD * D_MODEL, CHUNK) for t in both]


def _wgrads_mixer(a, da1, p, dp1, merged, dh1b, pooled, dmixed):
    bk = min(BK, a.shape[0])
    nk = a.shape[0] // bk
    g_ao = _wgrad(
        a, da1, "wgrad_att_out", grid=(nk,), n_out_cols=("cols_b", CHUNK),
        a_spec=pl.BlockSpec((bk, GROUP_W), lambda k: (k, 0)),
        b_spec=pl.BlockSpec((bk, D_MODEL), lambda k: (k, 0)),
        out_spec=pl.BlockSpec((N_CHIPS, GROUP_W, CHUNK), lambda k: (0, 0, 0)),
        out_shape=SDS((N_CHIPS, GROUP_W, CHUNK), F32))
    g_po = _wgrad(
        p, dp1, "wgrad_pool_out", grid=(nk,), n_out_cols=("cols_b", CHUNK),
        a_spec=pl.BlockSpec((bk, POOL_W), lambda k: (k, 0)),
        b_spec=pl.BlockSpec((bk, D_MODEL), lambda k: (k, 0)),
        out_spec=pl.BlockSpec((N_CHIPS, POOL_W, CHUNK), lambda k: (0, 0, 0)),
        out_shape=SDS((N_CHIPS, POOL_W, CHUNK), F32))
    g_out = _wgrad(
        merged, dh1b, "wgrad_out", grid=(nk,),
        a_spec=pl.BlockSpec((bk, D_MODEL), lambda k: (k, 0)),
        b_spec=pl.BlockSpec((bk, D_MODEL), lambda k: (k, 0)),
        out_spec=pl.BlockSpec((D_MODEL, D_MODEL), lambda k: (0, 0)),
        out_shape=SDS((D_MODEL, D_MODEL), F32))
    g_bd = _wgrad(
        pooled, dmixed, "wgrad_pool_grp", grid=(nk,),
        a_spec=pl.BlockSpec((bk, POOL_W), lambda k: (k, 0)),
        b_spec=pl.BlockSpec((bk, POOL_W), lambda k: (k, 0)),
        out_spec=pl.BlockSpec((POOL_W, POOL_W), lambda k: (0, 0)),
        out_shape=SDS((POOL_W, POOL_W), F32), narrow=False)
    g_out = [t.reshape(N_CHIPS, D_MODEL // N_CHIPS, D_MODEL) for t in g_out]
    return [g_ao, g_po, g_out], g_bd


def _wgrads_mlp(m, dpre, hid, dh2b):
    bk = min(BK, m.shape[0])
    nk = m.shape[0] // bk
    g_mi = _wgrad(
        m, dpre, "wgrad_mlp_in", grid=(N_CHIPS, nk),
        a_spec=pl.BlockSpec((bk, D_MODEL), lambda c, k: (k, 0)),
        b_spec=pl.BlockSpec((bk, D_MODEL), lambda c, k: (k, c)),
        out_spec=pl.BlockSpec((None, D_MODEL, D_MODEL), lambda c, k: (c, 0, 0)),
        out_shape=SDS((N_CHIPS, D_MODEL, D_MODEL), F32))
    g_mo = _wgrad(
        hid, dh2b, "wgrad_mlp_out", grid=(N_CHIPS, nk),
        a_spec=pl.BlockSpec((bk, D_MODEL), lambda c, k: (k, c)),
        b_spec=pl.BlockSpec((bk, D_MODEL), lambda c, k: (k, 0)),
        out_spec=pl.BlockSpec((None, D_MODEL, D_MODEL), lambda c, k: (c, 0, 0)),
        out_shape=SDS((N_CHIPS, D_MODEL, D_MODEL), F32))
    return [g_mi, g_mo]


def _mesh_place():
    x, y, c = lax.axis_index("x"), lax.axis_index("y"), lax.axis_index("c")
    other_chips = [(x, 1 - y), (1 - x, y), (1 - x, 1 - y)]
    return x, y, c, other_chips


ANY = pl.BlockSpec(memory_space=pl.ANY)


def _weight_half_copies(shard_refs, buf_refs, rows, send_sem, recv_sem):
    x, y, c, chips = _mesh_place()
    me = 2 * x + y
    copies = []
    for w, r_full in enumerate(rows):
        rh = r_full // 2
        for r, (px, py) in enumerate(chips):
            k = w * 3 + r
            copies.append(pltpu.make_async_remote_copy(
                src_ref=shard_refs[w].at[pl.ds(c * rh, rh), :], dst_ref=buf_refs[w].at[me, pl.ds(c * rh, rh), :],
                send_sem=send_sem.at[k], recv_sem=recv_sem.at[k], device_id=(px, py, c), device_id_type=MESH))
    return copies


def _pair_forward_copies(buf_refs, rows, send_sem, recv_sem):
    x, y, c, chips = _mesh_place()
    out = []
    for w, r_full in enumerate(rows):
        rh = r_full // 2
        for r, (px, py) in enumerate(chips):
            k = w * 3 + r
            landed = buf_refs[w].at[2 * px + py, pl.ds(c * rh, rh), :]
            theirs = buf_refs[w].at[2 * px + py, pl.ds((1 - c) * rh, rh), :]
            mk = lambda ref: pltpu.make_async_remote_copy(
                src_ref=ref, dst_ref=ref, send_sem=send_sem.at[k], recv_sem=recv_sem.at[k],
                device_id=(x, y, 1 - c), device_id_type=MESH)
            out.append((mk(landed), mk(theirs)))
    return out


def _place_own(block, n_slots, slot):
    buf = lax.empty((n_slots,) + block.shape, block.dtype)
    return lax.dynamic_update_slice(buf, block[None], (slot,) + (0,) * block.ndim)


def _pair_forward(bufs, rows, name):
    n = len(bufs)

    def body(*refs):
        dst = refs[n:2 * n]
        send_sem, recv_sem = refs[2 * n:]
        fwds = _pair_forward_copies(dst, rows, send_sem, recv_sem)
        for fwd, _ in fwds:
            fwd.start()
        for fwd, landing in fwds:
            landing.wait_recv()
            fwd.wait_send()

    return pl.pallas_call(
        body, name=name,
        in_specs=[ANY] * n, out_specs=[ANY] * n,
        out_shape=[SDS(b.shape, b.dtype) for b in bufs],
        scratch_shapes=[pltpu.SemaphoreType.DMA((3 * n,))] * 2,
        input_output_aliases={w: w for w in range(n)},
    )(*bufs)


def _chip_sum_copies(src, dst, send_sem, recv_sem):
    x, y, c, chips = _mesh_place()
    copies = []
    for w in range(len(src)):
        for r, (px, py) in enumerate(chips):
            k = w * 3 + r
            copies.append(pltpu.make_async_remote_copy(
                src_ref=src[w].at[r + 1], dst_ref=dst[w].at[r + 1], send_sem=send_sem.at[k], recv_sem=recv_sem.at[k],
                device_id=(px, py, c), device_id_type=MESH))
    return copies


def _pair_exchange(grads):
    n = len(grads)

    def body(*refs):
        src, dst = refs[:n], refs[n:2 * n]
        send_sem, recv_sem = refs[2 * n:]
        x, y, c, _ = _mesh_place()
        copies = []
        for w in range(n):
            rh = grads[w].shape[1] // 2
            copies.append(pltpu.make_async_remote_copy(
                src_ref=src[w].at[:, pl.ds((1 - c) * rh, rh), :], dst_ref=dst[w],
                send_sem=send_sem.at[w], recv_sem=recv_sem.at[w],
                device_id=(x, y, 1 - c), device_id_type=MESH))
            copies[-1].start()
        for cpy in copies:
            cpy.wait()

    return pl.pallas_call(
        body, name="grad_pair_exchange",
        in_specs=[ANY] * n, out_specs=[ANY] * n,
        out_shape=[SDS((N_CHIPS, g.shape[1] // 2, g.shape[2]), g.dtype) for g in grads],
        scratch_shapes=[pltpu.SemaphoreType.DMA((n,)), pltpu.SemaphoreType.DMA((n,))],
    )(*grads)


def _pair_sum(place, grad, recv, name):
    _, R, C = grad.shape
    rh = R // 2
    br = _row_block(rh, max(256, ELEMENTWISE_BLOCK // C))
    nbh = rh // br

    def body(place_ref, g_ref, r_ref, own_ref, sums_ref):
        s = g_ref[...] + r_ref[...].astype(F32)

        @pl.when(pl.program_id(1) == 0)
        def _():
            own_ref[...] = s

        sums_ref[...] = s.astype(BF16)

    slot = lambda rel, pr: jnp.bitwise_xor(pr[0], rel)
    return pl.pallas_call(
        body, name=name,
        grid_spec=pltpu.PrefetchScalarGridSpec(
            num_scalar_prefetch=1, grid=(nbh, N_CHIPS),
            in_specs=[pl.BlockSpec((None, br, C), lambda i, rel, pr: (slot(rel, pr), pr[1] * nbh + i, 0)),
                      pl.BlockSpec((None, br, C), lambda i, rel, pr: (slot(rel, pr), i, 0))],
            out_specs=[pl.BlockSpec((br, C), lambda i, rel, pr: (i, 0)),
                       pl.BlockSpec((None, br, C), lambda i, rel, pr: (rel, i, 0))]),
        out_shape=[SDS((rh, C), F32), SDS((N_CHIPS, rh, C), BF16)],
        compiler_params=_cp("parallel", "arbitrary"),
    )(place, grad, recv)


def _chip_sum(place, own, recv, name):
    rh, C = own.shape
    br = _row_block(rh, max(256, ELEMENTWISE_BLOCK // C))
    nbh = rh // br

    def body(place_ref, own_ref, r_ref, o_ref):
        o_ref[...] = ((own_ref[...] + r_ref[1].astype(F32)) + r_ref[2].astype(F32)) + r_ref[3].astype(F32)

    return pl.pallas_call(
        body, name=name,
        grid_spec=pltpu.PrefetchScalarGridSpec(
            num_scalar_prefetch=1, grid=(nbh,),
            in_specs=[pl.BlockSpec((br, C), lambda i, pr: (i, 0)),
                      pl.BlockSpec((N_CHIPS, br, C), lambda i, pr: (0, i, 0))],
            out_specs=pl.BlockSpec((br, C), lambda i, pr: (pr[1] * nbh + i, 0))),
        out_shape=SDS((2 * rh, C), F32),
        compiler_params=_cp("parallel"),
    )(place, own, recv)


def _pack_allgather(all_ref, send_sem, recv_sem):
    x, y, c, chips = _mesh_place()
    sib = (x, y, 1 - c)

    def pack(dev, k, to):
        slot = 4 * dev[0] + 2 * dev[1] + dev[2]
        return pltpu.make_async_remote_copy(
            src_ref=all_ref.at[slot], dst_ref=all_ref.at[slot], send_sem=send_sem.at[k],
            recv_sem=recv_sem.at[k], device_id=to, device_id_type=MESH)

    first = [pack((x, y, c), 0, sib)] + [pack((x, y, c), 1 + r, (px, py, c)) for r, (px, py) in enumerate(chips)]
    relays = [pack((px, py, c), 4 + r, sib) for r, (px, py) in enumerate(chips)]

    def start():
        for cpy in first:
            cpy.start()

    def relay():
        for r, (px, py) in enumerate(chips):
            pack((px, py, c), 1 + r, (px, py, c)).wait_recv()
            relays[r].start()

    def finish():
        pack(sib, 0, sib).wait_recv()
        for r, (px, py) in enumerate(chips):
            pack((px, py, 1 - c), 4 + r, sib).wait_recv()
        for cpy in first + relays:
            cpy.wait_send()

    return start, relay, finish


def _finish_exchange(grads, late_all):
    n = len(grads)

    def body(*refs):
        dst, all_ref = refs[n + 1:2 * n + 1], refs[2 * n + 1]
        send_sem, recv_sem, ssend_sem, srecv_sem = refs[2 * n + 2:]
        x, y, c, _ = _mesh_place()
        start, relay, finish = _pack_allgather(all_ref, ssend_sem, srecv_sem)
        start()
        sends, landings = [], []
        for w in range(n):
            rh = grads[w].shape[0] // 2
            mk = lambda cc: pltpu.make_async_remote_copy(
                src_ref=dst[w].at[pl.ds(cc * rh, rh), :], dst_ref=dst[w].at[pl.ds(cc * rh, rh), :],
                send_sem=send_sem.at[w], recv_sem=recv_sem.at[w], device_id=(x, y, 1 - c), device_id_type=MESH)
            sends.append(mk(c))
            landings.append(mk(1 - c))
            sends[-1].start()
        relay()
        finish()
        for cpy in landings:
            cpy.wait_recv()
        for cpy in sends:
            cpy.wait_send()

    res = pl.pallas_call(
        body, name="grad_finish_exchange",
        in_specs=[ANY] * (n + 1), out_specs=[ANY] * (n + 1),
        out_shape=[SDS(g.shape, g.dtype) for g in grads] + [SDS(late_all.shape, late_all.dtype)],
        scratch_shapes=[pltpu.SemaphoreType.DMA((n,)), pltpu.SemaphoreType.DMA((n,)),
                        pltpu.SemaphoreType.DMA((N_DEV - 1,)), pltpu.SemaphoreType.DMA((N_DEV - 1,))],
        input_output_aliases={w: w for w in range(n + 1)},
    )(*grads, late_all)
    return res[:n], res[n]


def _adamw_math(w, g, m, v):
    m = ADAM_B1 * m + (1.0 - ADAM_B1) * g
    v = ADAM_B2 * v + (1.0 - ADAM_B2) * jnp.square(g)
    m_hat = m / (1.0 - ADAM_B1 ** ADAM_STEP)
    v_hat = v / (1.0 - ADAM_B2 ** ADAM_STEP)
    delta = -ADAM_LR * (m_hat / (jnp.sqrt(v_hat) + ADAM_EPS) + ADAM_WD * w)
    return delta, m, v


def _adamw(w, g, m, v, name):
    R, C = w.shape
    br = _row_block(R, 512)
    if g.ndim == 3:
        n_chunks, cw = g.shape[0], g.shape[2]
        g_spec = pl.BlockSpec((None, br, cw), lambda t, i: (t, i, 0))
    else:
        n_chunks, cw = 1, C
        g_spec = pl.BlockSpec((br, cw), lambda t, i: (i, t))

    def body(w_ref, g_ref, m_ref, v_ref, g_out_ref, d_ref, nm_ref, nv_ref):
        gv = g_ref[...]
        g_out_ref[...] = gv
        d_ref[...], nm_ref[...], nv_ref[...] = _adamw_math(w_ref[...], gv, m_ref[...], v_ref[...])

    spec = pl.BlockSpec((br, cw), lambda t, i: (i, t))
    return pl.pallas_call(
        body, grid=(n_chunks, R // br), name=name, in_specs=[spec, g_spec, spec, spec], out_specs=[spec] * 4,
        out_shape=[SDS((R, C), F32)] * 4, compiler_params=_cp("parallel", "parallel"),
    )(w, g, m, v)


def _small_sum_adamw(all_small, all_late, w, m, v):
    loss_row = PACK_ROWS - 8

    def body(all_ref, late_ref, w_ref, m_ref, v_ref, g_ref, d_ref, nm_ref, nv_ref, loss_ref):
        g = all_ref[0]
        late = late_ref[0]
        for k in range(1, N_DEV):
            g = g + all_ref[k]
            late = late + late_ref[k]
        g_ref[...] = g
        g_ref[PACK_LATE_ROW:PACK_LATE_ROW + 8, :] = late
        g = g_ref[...]
        d_ref[...], nm_ref[...], nv_ref[...] = _adamw_math(w_ref[...], g, m_ref[...], v_ref[...])
        total = jnp.sum(g[loss_row:loss_row + 1, :]) * (0.5 / D_MODEL)
        loss_ref[...] = jnp.full(loss_ref.shape, total, F32)

    full = lambda s: pl.BlockSpec(s, lambda i: (0,) * len(s))
    pack = (PACK_ROWS, D_MODEL)
    return pl.pallas_call(
        body, grid=(1,), name="small_sum_adamw",
        in_specs=[full((N_DEV,) + pack), full((N_DEV, 8, D_MODEL)), full(pack), full(pack), full(pack)],
        out_specs=[full(pack)] * 4 + [full((8, 128))],
        out_shape=[SDS(pack, F32)] * 4 + [SDS((8, 128), F32)],
        compiler_params=_cp("arbitrary"),
    )(all_small, all_late, w, m, v)


def _pack_small(grp, scale, g_mix, g_mlp, g_f, loss_lanes):
    def part(vec):
        vec = vec.reshape(1, -1)
        return jnp.pad(vec, ((0, 7), (0, D_MODEL - vec.shape[1])))
    return jnp.concatenate([grp.reshape(-1, D_MODEL), part(scale), part(g_mix), part(g_mlp), part(g_f),
                            part(loss_lanes)], axis=0)


def _unpack_small(pack):
    n_grp = len(POOL_WINDOWS) * POOL_GROUP_W * POOL_GROUP_W // D_MODEL
    grp = pack[:n_grp].reshape(1, len(POOL_WINDOWS), POOL_GROUP_W, POOL_GROUP_W)
    scale = pack[n_grp, :POOL_W].reshape(1, POOL_W)
    g_mix = pack[n_grp + 8].reshape(1, D_MODEL)
    g_mlp = pack[n_grp + 16].reshape(1, D_MODEL)
    g_f = pack[n_grp + 24].reshape(D_MODEL)
    return grp, scale, g_mix, g_mlp, g_f


def _block_diag(grp):
    out = jnp.zeros((POOL_W, POOL_W), grp.dtype)
    for k in range(len(POOL_WINDOWS)):
        out = lax.dynamic_update_slice(out, grp[k], (k * POOL_GROUP_W, k * POOL_GROUP_W))
    return out


def kernel(x, norm_mix_g, w_in, w_att_out, w_pool_grp, pool_scale, w_pool_out, w_out, norm_mlp_g, w_mlp_in, w_mlp_out, norm_final_g, loss_target, m_norm_mix_g, m_w_in, m_w_att_out, m_w_pool_grp, m_pool_scale, m_w_pool_out, m_w_out, m_norm_mlp_g, m_w_mlp_in, m_w_mlp_out, m_norm_final_g, v_norm_mix_g, v_w_in, v_w_att_out, v_w_pool_grp, v_pool_scale, v_w_pool_out, v_w_out, v_norm_mlp_g, v_w_mlp_in, v_w_mlp_out, v_norm_final_g):
    S = x.shape[1]
    xs, target = x[0], loss_target[0]
    big = [w_in[0], w_att_out[0], w_pool_out[0], w_out[0], w_mlp_in[0], w_mlp_out[0]]
    big_m = [m_w_in[0], m_w_att_out[0], m_w_pool_out[0], m_w_out[0], m_w_mlp_in[0], m_w_mlp_out[0]]
    big_v = [v_w_in[0], v_w_att_out[0], v_w_pool_out[0], v_w_out[0], v_w_mlp_in[0], v_w_mlp_out[0]]

    chip = 2 * lax.axis_index("x") + lax.axis_index("y")
    core = lax.axis_index("c")
    place = jnp.stack([chip, core]).astype(jnp.int32)
    names = ("w_in", "w_att_out", "w_pool_out", "w_out", "w_mlp_in", "w_mlp_out")

    shards = [w.astype(BF16) for w in big]
    bufs = [_place_own(sh, N_CHIPS, chip) for sh in shards]
    wbd = _block_diag(w_pool_grp[0]).astype(BF16)
    g_final = norm_final_g.reshape(1, D_MODEL)
    stat_expand, stat_reduce = _stat_matrices()

    u, z_own, landed_in = _norm_inproj_own(xs, norm_mix_g, shards[0], bufs[0])
    (wg_in,) = _pair_forward([landed_in], [shards[0].shape[0]], "w_in_pair_forward")
    (qkv0, qkv1, qkv2, pz, gates), (wg_ao, wg_po, wg_out) = _inproj_rest(u, z_own, wg_in, shards[1:4], bufs[1:4])
    wg_out = wg_out.reshape(D_MODEL, D_MODEL)
    qkv = (qkv0, qkv1, qkv2)
    att = [_attn_fwd(qkv[grp], grp) for grp in range(3)]
    (a, lt0, lt1, lt2, pooled, mixed, p, merged, h1, m), (wg_mi, wg_mo) = _mixer_out(
        [o for o, _ in att], [l for _, l in att], pz, gates, xs, wg_ao, wg_po, wbd, pool_scale, wg_out, norm_mlp_g,
        stat_expand, shards[4:], bufs[4:])
    hid, dh2, dh2b, loss_lanes, dg_final = _mlp_fwd_loss(m, h1, target, wg_mi, wg_mo, g_final)

    def pair_reduce(grads, grad_names):
        recv = _pair_exchange([narrow for _, narrow in grads])
        pair = [_pair_sum(place, g, r, f"pair_sum_{nm}") for (g, _), r, nm in zip(grads, recv, grad_names)]
        return [own for own, _ in pair], [s for _, s in pair]

    def chip_reduce(owns, landed_sums, grad_names):
        return [_chip_sum(place, own, r, f"chip_sum_{nm}") for own, r, nm in zip(owns, landed_sums, grad_names)]

    dpre, dh1, dh1b, dg_mlp = _mlp_bwd(dh2, dh2b, hid, h1, wg_mi, wg_mo, norm_mlp_g)
    own_mlp, sums_mlp = pair_reduce(_wgrads_mlp(m, dpre, hid, dh2b), names[4:])
    (da1, dp1, dgates, da0, dag1, dag2, dd0, dd1, dd2, dmixed, dqp, dscale), landed_mlp = _mixer_bwd(
        dh1b, a, p, mixed, gates, wg_out, wg_ao, wg_po, wbd, pool_scale, stat_reduce, sums_mlp)
    g_mi, g_mo = chip_reduce(own_mlp, landed_mlp, names[4:])
    grads_mixer, g_bd = _wgrads_mixer(a, da1, p, dp1, merged, dh1b, pooled, dmixed)

    zero = jnp.zeros((D_MODEL,), F32)
    g_grp = jnp.stack([g_bd[k * POOL_GROUP_W:(k + 1) * POOL_GROUP_W, k * POOL_GROUP_W:(k + 1) * POOL_GROUP_W]
                       for k in range(len(POOL_WINDOWS))])
    small = _pack_small(g_grp, dscale, zero, dg_mlp, dg_final, loss_lanes)
    *dqkv0, small_all = _attn_bwd(qkv[0], da0, lt0, dd0, 0, packs=_place_own(small, N_DEV, 2 * chip + core))
    dqkv = [dqkv0, _attn_bwd(qkv[1], dag1, lt1, dd1, 1), _attn_bwd(qkv[2], dag2, lt2, dd2, 2)]
    dz = _dz_assemble(dqkv, dqp)
    own_in, sums_in = pair_reduce([_wgrad_in(u, dz, dgates)] + grads_mixer, names[:4])
    dx, dg_mix, landed_in = _inproj_dx(dz, dgates, dh1, xs, norm_mix_g, wg_in, sums_in)
    g_in, g_ao, g_po, g_out = chip_reduce(own_in, landed_in, names[:4])
    late = jnp.pad(dg_mix, ((0, 7), (0, 0)))
    full, late_all = _finish_exchange([g_in, g_ao, g_po, g_out, g_mi, g_mo], _place_own(late, N_DEV, 2 * chip + core))

    small_w = _pack_small(w_pool_grp[0], pool_scale, norm_mix_g, norm_mlp_g, norm_final_g, zero)
    small_m = _pack_small(m_w_pool_grp[0], m_pool_scale, m_norm_mix_g, m_norm_mlp_g, m_norm_final_g, zero)
    small_v = _pack_small(v_w_pool_grp[0], v_pool_scale, v_norm_mix_g, v_norm_mlp_g, v_norm_final_g, zero)
    sg, sd, sm, sv, loss_tile = _small_sum_adamw(small_all, late_all, small_w, small_m, small_v)
    full = [full[0].reshape(CHUNKS_PER_SHARD, D_MODEL, CHUNK)] + list(full[1:])
    upd = [_adamw(w, g, mm, vv, f"adamw_{nm}") for w, g, mm, vv, nm in zip(big, full, big_m, big_v, names)]

    def ordered(small_pack, bigs):
        grp, scale, g_mix, g_mlp, g_f = _unpack_small(small_pack)
        b_in, b_ao, b_po, b_out, b_mi, b_mo = [b[None] for b in bigs]
        return (g_mix, b_in, b_ao, grp, scale, b_po, b_out, g_mlp, b_mi, b_mo, g_f)

    return (loss_tile[0, 0], dx[None],
            *ordered(sg, [t[0] for t in upd]),
            *ordered(sd, [t[1] for t in upd]),
            *ordered(sm, [t[2] for t in upd]),
            *ordered(sv, [t[3] for t in upd]))
```

```python
import functools

import jax
import jax.numpy as jnp
from jax import lax
from jax.experimental import pallas as pl
from jax.experimental.pallas import tpu as pltpu

F32 = jnp.float32
BF16 = jnp.bfloat16
SDS = jax.ShapeDtypeStruct
MESH = pl.DeviceIdType.MESH

D_MODEL = 1024
D_FF = 4096
N_CHIPS = 4
N_DEV = 8
DILATIONS = (1, 4, 16)
BAND = 128
GROUP_W = 256
PAIR_W = 128
HEAD_W = 64
STAT_W = 128
STAT_HEAD_W = 32
POOL_W = 768
POOL_GROUP_W = 192
POOL_WINDOWS = (2, 4, 8, 16)
POOL_HALO = 16
N_IN = 5120
CHUNK = 256
N_CHUNKS = N_IN // CHUNK
N_DZ_CHUNKS = 12
CHUNKS_PER_SHARD = 5
WGRAD_IN_GROUP = 4
NORM_EPS = 1e-6
ALIBI_MAX_BIAS = 8.0
N_HEADS = 12
NEG = -1e30

ADAM_LR, ADAM_B1, ADAM_B2, ADAM_EPS, ADAM_WD, ADAM_STEP = 0.001, 0.9, 0.999, 1e-08, 0.01, 10

TM = 512
TMB = 512
TMZ = 1024
ATT_TILE = ((1, 16), (4, 4), (16, 1))
BK = 4096
ELEMENTWISE_BLOCK = 1 << 20
VMEM_LIMIT = 56 * 1024 * 1024
PACK_ROWS = 184
PACK_LATE_ROW = 152

NT = (((1,), (1,)), ((), ()))
TN = (((0,), (0,)), ((), ()))


def _cp(*sem):
    return pltpu.CompilerParams(dimension_semantics=sem, vmem_limit_bytes=VMEM_LIMIT)


def _resident(shape):
    nd = len(shape)
    return pl.BlockSpec(shape, lambda *_: (0,) * nd, pipeline_mode=pl.Buffered(1))


def _row_block(rows, cap=256):
    return max(b for b in range(16, min(rows, cap) + 1, 16) if rows % b == 0)


def _dot(a, b):
    return jnp.dot(a, b, preferred_element_type=F32)


def _dot_nt(a, b):
    return lax.dot_general(a, b, NT, preferred_element_type=F32)


def _dot_tn(a, b):
    return lax.dot_general(a, b, TN, preferred_element_type=F32)


def _w_in_chunk(w_ref, n):
    return w_ref[n // CHUNKS_PER_SHARD, :, (n % CHUNKS_PER_SHARD) * CHUNK:(n % CHUNKS_PER_SHARD + 1) * CHUNK]


def _sigmoid(x):
    return 0.5 * jnp.tanh(0.5 * x.astype(F32)) + 0.5


def _rms_fwd(x, g):
    r = lax.rsqrt(jnp.mean(x * x, axis=-1, keepdims=True) + NORM_EPS)
    xh = x * r
    return xh * g, xh, r


def _rms_bwd(dy, xh, r, g):
    dxh = dy * g
    return r * (dxh - xh * jnp.mean(dxh * xh, axis=-1, keepdims=True))


def _per_head_lanes(cols):
    rows = cols[0].shape[0]
    lane = lax.broadcasted_iota(jnp.int32, (rows, STAT_W), 1)
    out = cols[3]
    for h in (2, 1, 0):
        out = jnp.where(lane < (h + 1) * STAT_HEAD_W, cols[h], out)
    return out


def _head_col(stat, h):
    return stat[:, h * STAT_HEAD_W:h * STAT_HEAD_W + 1]


def _stat_matrices():
    s = lax.broadcasted_iota(jnp.int32, (STAT_W, GROUP_W), 0)
    c = lax.broadcasted_iota(jnp.int32, (STAT_W, GROUP_W), 1)
    expand = (s == (c // HEAD_W) * STAT_HEAD_W).astype(BF16)
    reduce = (s // STAT_HEAD_W == c // HEAD_W).astype(BF16).T
    return expand, reduce


def _dot_split(x, m):
    hi = x.astype(BF16)
    lo = (x - hi.astype(F32)).astype(BF16)
    return _dot(hi, m) + _dot(lo, m)


def _deinterleave_store(val, s_ref, out_ref, lead, d, rows, dtype):
    if d == 1:
        out_ref[lead + (0,)] = val.astype(dtype)
        return
    for h in range(2):
        s_ref[h] = val[:, h * PAIR_W:(h + 1) * PAIR_W]
    for r in range(d):
        for h in range(2):
            out_ref[lead + (r, slice(None), slice(h * PAIR_W, (h + 1) * PAIR_W))] = (
                s_ref[h, pl.ds(r, rows // d, stride=d), :].astype(dtype))


def _interleave_load(in_ref, lead, s_ref, d, rows):
    for r in range(d):
        for h in range(2):
            s_ref[h, pl.ds(r, rows // d, stride=d), :] = (
                in_ref[lead + (r, slice(None), slice(h * PAIR_W, (h + 1) * PAIR_W))].astype(F32))


def _norm_inproj_own(x, g, w_own, buf):
    S = x.shape[0]
    tm = min(TMZ, S)
    n_tiles = S // tm

    def body(x_ref, g_ref, w_ref, shard_ref, buf_in, u_ref, z_ref, buf_ref, send_sem, recv_sem):
        i = pl.program_id(0)

        def copies():
            return _weight_half_copies([shard_ref], [buf_ref], [w_own.shape[0]], send_sem, recv_sem)

        @pl.when(i == 0)
        def _():
            for cpy in copies():
                cpy.start()

        u = _rms_fwd(x_ref[...], g_ref[...])[0].astype(BF16)
        u_ref[...] = u
        for t in range(CHUNKS_PER_SHARD):
            z_ref[t] = _dot(u, w_ref[:, t * CHUNK:(t + 1) * CHUNK]).astype(BF16)

        @pl.when(i == n_tiles - 1)
        def _():
            for cpy in copies():
                cpy.wait()

    row = lambda w: pl.BlockSpec((tm, w), lambda i: (i, 0))
    return pl.pallas_call(
        body, grid=(n_tiles,), name="norm_inproj_own",
        in_specs=[row(D_MODEL), _resident((1, D_MODEL)), _resident(w_own.shape), ANY, ANY],
        out_specs=[row(D_MODEL), pl.BlockSpec((CHUNKS_PER_SHARD, tm, CHUNK), lambda i: (0, i, 0)), ANY],
        out_shape=[SDS((S, D_MODEL), BF16), SDS((CHUNKS_PER_SHARD, S, CHUNK), BF16), SDS(buf.shape, buf.dtype)],
        scratch_shapes=[pltpu.SemaphoreType.DMA((3,)), pltpu.SemaphoreType.DMA((3,))],
        input_output_aliases={4: 2},
        compiler_params=_cp("arbitrary"),
    )(x, g, w_own, w_own, buf)


def _hosted_allgather(i, n_steps, shard_refs, buf_refs, rows, sems):
    send_sem, recv_sem, fsend_sem, frecv_sem = sems
    ici = lambda: _weight_half_copies(shard_refs, buf_refs, rows, send_sem, recv_sem)
    forward = lambda: _pair_forward_copies(buf_refs, rows, fsend_sem, frecv_sem)

    def begin():
        @pl.when(i == 0)
        def _():
            for cpy in ici():
                cpy.start()

        @pl.when(i == n_steps // 2)
        def _():
            for cpy, (fwd, _) in zip(ici(), forward()):
                cpy.wait_recv()
                fwd.start()

    def end():
        @pl.when(i == n_steps - 1)
        def _():
            for cpy, (fwd, landing) in zip(ici(), forward()):
                landing.wait_recv()
                fwd.wait_send()
                cpy.wait_send()

    return begin, end


def _inproj_rest(u, z_own, w_in, shards, bufs):
    S = u.shape[0]
    n_tiles = S // TM
    n = len(shards)

    def body(*refs):
        u_ref, zown_ref, w_ref = refs[0:3]
        shard_refs = refs[3:3 + n]
        q0_ref, q1_ref, q2_ref, pz_ref, gate_ref = refs[3 + 2 * n:8 + 2 * n]
        buf_refs = refs[8 + 2 * n:8 + 3 * n]
        s_ref = refs[8 + 3 * n]
        i = pl.program_id(0)
        chip = 2 * lax.axis_index("x") + lax.axis_index("y")
        begin, end = _hosted_allgather(i, n_tiles, shard_refs, buf_refs, [sh.shape[0] for sh in shards],
                                       refs[9 + 3 * n:])
        begin()

        u = u_ref[...]
        qkv_refs = (q0_ref, q1_ref, q2_ref)

        def emit(k, zc):
            if k < 9:
                which, grp = k // 3, k % 3
                if which == 0:
                    zc = zc * 0.125
                _deinterleave_store(zc, s_ref, qkv_refs[grp], (which,), DILATIONS[grp], TM, BF16)
            elif k < N_DZ_CHUNKS:
                pz_ref[:, (k - 9) * CHUNK:(k - 8) * CHUNK] = zc.astype(BF16)
            else:
                gate_ref[:, (k - N_DZ_CHUNKS) * CHUNK:(k - N_DZ_CHUNKS + 1) * CHUNK] = zc.astype(BF16)

        def all_chunks(own_shard):
            for k in range(N_CHUNKS):
                if k // CHUNKS_PER_SHARD == own_shard:
                    emit(k, zown_ref[k % CHUNKS_PER_SHARD].astype(F32))
                else:
                    emit(k, _dot(u, _w_in_chunk(w_ref, k)))

        for shard in range(N_CHIPS):
            pl.when(chip == shard)(functools.partial(all_chunks, shard))
        end()

    row = lambda w: pl.BlockSpec((TM, w), lambda i: (i, 0))
    res = pl.pallas_call(
        body, grid=(n_tiles,), name="inproj_rest",
        in_specs=[row(D_MODEL), pl.BlockSpec((CHUNKS_PER_SHARD, TM, CHUNK), lambda i: (0, i, 0)),
                  _resident(w_in.shape)] + [ANY] * (2 * n),
        out_specs=[pl.BlockSpec((3, d, TM // d, GROUP_W), lambda i: (0, 0, i, 0)) for d in DILATIONS]
        + [row(POOL_W), row(2 * D_MODEL)] + [ANY] * n,
        out_shape=[SDS((3, d, S // d, GROUP_W), BF16) for d in DILATIONS]
        + [SDS((S, POOL_W), BF16), SDS((S, 2 * D_MODEL), BF16)] + [SDS(b.shape, b.dtype) for b in bufs],
        scratch_shapes=[pltpu.VMEM((2, TM, PAIR_W), F32)] + [pltpu.SemaphoreType.DMA((3 * n,))] * 4,
        input_output_aliases={3 + n + w: 5 + w for w in range(n)},
        compiler_params=_cp("arbitrary"),
    )(u, z_own, w_in, *shards, *bufs)
    return res[:5], res[5:]


def _band_bias(grp, d):
    row = lax.broadcasted_iota(jnp.int32, (BAND, 2 * BAND), 0)
    col = lax.broadcasted_iota(jnp.int32, (BAND, 2 * BAND), 1)
    steps = BAND + row - col
    valid = (steps >= 0) & (steps <= BAND)
    stepsf = (steps * d).astype(F32)
    biases = []
    for hh in range(4):
        slope = 2.0 ** (-ALIBI_MAX_BIAS * (grp * 4 + hh + 1) / N_HEADS)
        biases.append(jnp.where(valid, -slope * stepsf, NEG))
    return biases, col


def _attn_tiles(grp, L):
    rr, rb = ATT_TILE[grp]
    rb = min(rb, L // BAND)
    return rr, rb, L // (rb * BAND)


def _kv_tile(cur_ref, prev_ref, rr, rb, cs):
    if rb == 0:
        return jnp.concatenate([prev_ref[rr, :, cs], cur_ref[rr, 0:BAND, cs]], axis=0)
    return cur_ref[rr, (rb - 1) * BAND:(rb + 1) * BAND, cs]


def _attn_fwd(qkv, grp):
    d = DILATIONS[grp]
    L = qkv.shape[2]
    RR, RB, nb = _attn_tiles(grp, L)

    def body(q_ref, kc_ref, kp_ref, vc_ref, vp_ref, o_ref, lse_ref):
        i = pl.program_id(0)
        biases, col = _band_bias(grp, d)
        first_keys_ok = (col >= BAND) | (i > 0)
        is_a = lax.broadcasted_iota(jnp.int32, (BAND, PAIR_W), 1) < HEAD_W
        heads = [(rr, rb, cp, h2) for rr in range(RR) for rb in range(RB) for cp in range(2) for h2 in range(2)]

        def tile(head):
            rr, rb, cp, _ = head
            return rr, rb, slice(rb * BAND, (rb + 1) * BAND), slice(cp * PAIR_W, (cp + 1) * PAIR_W)

        def scores(head):
            rr, rb, rows, cs = tile(head)
            q2 = q_ref[rr, rows, cs]
            b = biases[head[2] * 2 + head[3]]
            if rb == 0:
                b = jnp.where(first_keys_ok, b, NEG)
            sel = is_a if head[3] == 0 else jnp.logical_not(is_a)
            return _dot_nt(jnp.where(sel, q2, jnp.zeros_like(q2)), _kv_tile(kc_ref, kp_ref, rr, rb, cs)) + b

        s_next = scores(heads[0])
        outs, lses = {}, {}
        for idx, head in enumerate(heads):
            s = s_next
            if idx + 1 < len(heads):
                s_next = scores(heads[idx + 1])
            rr, rb, rows, cs = tile(head)
            m = jnp.max(s, axis=-1, keepdims=True)
            p = jnp.exp(s - m)
            l = jnp.sum(p, axis=-1, keepdims=True)
            outs[head[3]] = _dot(p.astype(BF16), _kv_tile(vc_ref, vp_ref, rr, rb, cs)) * (1.0 / l)
            lses[head[2] * 2 + head[3]] = m + jnp.log(l)
            if head[3] == 1:
                o_ref[rr, rows, cs] = jnp.where(is_a, outs[0], outs[1]).astype(BF16)
            if head[2] == 1 and head[3] == 1:
                lse_ref[rr, rows, :] = _per_head_lanes(lses)

    cur = lambda w: pl.BlockSpec((None, RR, RB * BAND, GROUP_W), lambda i, j: (w, j, i, 0))
    prev = lambda w: pl.BlockSpec((None, RR, BAND, GROUP_W), lambda i, j: (w, j, jnp.maximum(i * RB - 1, 0), 0))
    return pl.pallas_call(
        body, grid=(nb, d // RR), name=f"attn_fwd_g{grp}",
        in_specs=[cur(0), cur(1), prev(1), cur(2), prev(2)],
        out_specs=[pl.BlockSpec((RR, RB * BAND, GROUP_W), lambda i, j: (j, i, 0)),
                   pl.BlockSpec((RR, RB * BAND, STAT_W), lambda i, j: (j, i, 0))],
        out_shape=[SDS((d, L, GROUP_W), BF16), SDS((d, L, STAT_W), F32)],
        compiler_params=_cp("parallel", "parallel"),
    )(qkv, qkv, qkv, qkv, qkv)


def _pool_column_select(col, vals):
    return jnp.where(col < POOL_GROUP_W, vals[0],
                     jnp.where(col < 2 * POOL_GROUP_W, vals[1],
                               jnp.where(col < 3 * POOL_GROUP_W, vals[2], vals[3])))


def _pool_inv_count(i, rows):
    t = i * rows + lax.broadcasted_iota(jnp.int32, (rows, POOL_W), 0)
    col = lax.broadcasted_iota(jnp.int32, (rows, POOL_W), 1)
    win = _pool_column_select(col, POOL_WINDOWS)
    return 1.0 / jnp.minimum(t + 1, win).astype(F32), col


def _mixer_out(outs, lses, pz, gates, x, w_ao, w_po, wbd, scale, w_out, g_mlp, expand, shards, bufs):
    S = x.shape[0]
    n_tiles = S // TMB
    n = len(shards)

    def body(*refs):
        (o0_ref, l0_ref, o1_ref, l1_ref, o2_ref, l2_ref, pz_ref, halo_ref, gate_ref, x_ref,
         wao_ref, wpo_ref, wbd_ref, sc_ref, wout_ref, g_ref, expand_ref) = refs[0:17]
        shard_refs = refs[17:17 + n]
        (a_ref, lt0_ref, lt1_ref, lt2_ref, pooled_ref, mixed_ref, p_ref, merged_ref, h1_ref,
         m_ref) = refs[17 + 2 * n:27 + 2 * n]
        buf_refs = refs[27 + 2 * n:27 + 3 * n]
        so1, sl1, so2, sl2, slt, ext_ref = refs[27 + 3 * n:33 + 3 * n]
        i = pl.program_id(0)
        begin, end = _hosted_allgather(i, n_tiles, shard_refs, buf_refs, [sh.shape[0] for sh in shards],
                                       refs[33 + 3 * n:])
        begin()
        _interleave_load(o1_ref, (), so1, DILATIONS[1], TMB)
        _interleave_load(o2_ref, (), so2, DILATIONS[2], TMB)
        for ref, sref, d in ((l1_ref, sl1, DILATIONS[1]), (l2_ref, sl2, DILATIONS[2])):
            for r in range(d):
                sref[0, pl.ds(r, TMB // d, stride=d), :] = ref[r]
        l0, l1, l2 = l0_ref[0], sl1[0], sl2[0]
        mx = jnp.maximum(jnp.maximum(l0, l1), l2)
        e0, e1, e2 = jnp.exp(l0 - mx), jnp.exp(l1 - mx), jnp.exp(l2 - mx)
        den = e0 + e1 + e2
        inv = 1.0 / den
        slt[0] = mx + jnp.log(den)
        w0, w1, w2 = [_dot_split(e * inv, expand_ref[...]) for e in (e0, e1, e2)]
        for h in range(2):
            hs = slice(h * PAIR_W, (h + 1) * PAIR_W)
            a_ref[:, hs] = (w0[:, hs] * o0_ref[0, :, hs].astype(F32) + w1[:, hs] * so1[h]
                            + w2[:, hs] * so2[h]).astype(BF16)
        lt0_ref[0] = slt[0]
        for ref, d in ((lt1_ref, DILATIONS[1]), (lt2_ref, DILATIONS[2])):
            for r in range(d):
                ref[r] = slt[0, pl.ds(r, TMB // d, stride=d), :]

        pz_t = pz_ref[...].astype(F32)
        ext_ref[0:POOL_HALO, :] = jnp.where(i > 0, halo_ref[...].astype(F32), 0.0)
        ext_ref[POOL_HALO:, :] = pz_t
        sums = []
        acc = ext_ref[...]
        for k in (1, 2, 4, 8):
            acc = acc + pltpu.roll(acc, k, 0)
            sums.append(acc[POOL_HALO:, :])
        inv_cnt, col = _pool_inv_count(i, TMB)
        pooled = (_pool_column_select(col, sums) * inv_cnt - pz_t).astype(BF16)
        pooled_ref[...] = pooled
        mixed = _dot(pooled, wbd_ref[...])
        mixed_ref[...] = mixed.astype(BF16)
        p = (mixed * sc_ref[...]).astype(BF16)
        p_ref[...] = p

        a = a_ref[...]
        for j in range(N_CHIPS):
            js = slice(j * CHUNK, (j + 1) * CHUNK)
            ga = gate_ref[:, js]
            gp = gate_ref[:, D_MODEL + j * CHUNK:D_MODEL + (j + 1) * CHUNK]
            mj = _sigmoid(ga) * _dot(a, wao_ref[j]) + _sigmoid(gp) * _dot(p, wpo_ref[j])
            merged_ref[:, js] = mj.astype(BF16)
        h1 = x_ref[...] + _dot(merged_ref[...], wout_ref[...])
        h1_ref[...] = h1
        m_ref[...] = _rms_fwd(h1, g_ref[...])[0].astype(BF16)
        end()

    row = lambda w: pl.BlockSpec((TMB, w), lambda i: (i, 0))
    grp_spec = lambda d: pl.BlockSpec((d, TMB // d, GROUP_W), lambda i: (0, i, 0))
    stat_spec = lambda d: pl.BlockSpec((d, TMB // d, STAT_W), lambda i: (0, i, 0))
    halo = pl.BlockSpec((POOL_HALO, POOL_W), lambda i: (jnp.maximum(i * (TMB // POOL_HALO) - 1, 0), 0))
    d0, d1, d2 = DILATIONS
    pair_scratch = pltpu.VMEM((2, TMB, PAIR_W), F32)
    stat_scratch = pltpu.VMEM((1, TMB, STAT_W), F32)
    res = pl.pallas_call(
        body, grid=(n_tiles,), name="mixer_out",
        in_specs=[grp_spec(d0), stat_spec(d0), grp_spec(d1), stat_spec(d1), grp_spec(d2), stat_spec(d2),
                  row(POOL_W), halo, row(2 * D_MODEL), row(D_MODEL),
                  _resident(w_ao.shape), _resident(w_po.shape), _resident(wbd.shape), _resident(scale.shape),
                  _resident(w_out.shape), _resident(g_mlp.shape), _resident(expand.shape)] + [ANY] * (2 * n),
        out_specs=[row(GROUP_W), stat_spec(d0), stat_spec(d1), stat_spec(d2),
                   row(POOL_W), row(POOL_W), row(POOL_W), row(D_MODEL), row(D_MODEL), row(D_MODEL)] + [ANY] * n,
        out_shape=[SDS((S, GROUP_W), BF16)] + [SDS((d, S // d, STAT_W), F32) for d in DILATIONS]
        + [SDS((S, POOL_W), BF16), SDS((S, POOL_W), BF16), SDS((S, POOL_W), BF16),
           SDS((S, D_MODEL), BF16), SDS((S, D_MODEL), F32), SDS((S, D_MODEL), BF16)]
        + [SDS(b.shape, b.dtype) for b in bufs],
        scratch_shapes=[pair_scratch, stat_scratch, pair_scratch, stat_scratch, stat_scratch,
                        pltpu.VMEM((TMB + POOL_HALO, POOL_W), F32)] + [pltpu.SemaphoreType.DMA((3 * n,))] * 4,
        input_output_aliases={17 + n + w: 10 + w for w in range(n)},
        compiler_params=_cp("arbitrary"),
    )(outs[0], lses[0], outs[1], lses[1], outs[2], lses[2], pz, pz, gates, x,
      w_ao, w_po, wbd, scale, w_out, g_mlp, expand, *shards, *bufs)
    return res[:10], res[10:]


def _mlp_fwd_loss(m, h1, target, w_mi, w_mo, g_f):
    S = m.shape[0]

    def body(m_ref, h1_ref, t_ref, wmi_ref, wmo_ref, g_ref, hid_ref, dh2_ref, dh2b_ref, loss_ref, dg_ref):
        @pl.when(pl.program_id(0) == 0)
        def _():
            loss_ref[...] = jnp.zeros_like(loss_ref)
            dg_ref[...] = jnp.zeros_like(dg_ref)

        mt = m_ref[...]
        acc = h1_ref[...]
        for c in range(N_CHIPS):
            hid = jnp.square(jnp.maximum(_dot(mt, wmi_ref[c]), 0.0)).astype(BF16)
            hid_ref[:, c * D_MODEL:(c + 1) * D_MODEL] = hid
            acc = acc + _dot(hid, wmo_ref[c])
        g = g_ref[...]
        y, hh, r = _rms_fwd(acc, g)
        e = y - t_ref[...]
        loss_ref[...] += jnp.sum(e * e, axis=0, keepdims=True)
        dy = e * (1.0 / D_MODEL)
        dg_ref[...] += jnp.sum(dy * hh, axis=0, keepdims=True)
        dh2 = _rms_bwd(dy, hh, r, g)
        dh2_ref[...] = dh2
        dh2b_ref[...] = dh2.astype(BF16)

    row = lambda w: pl.BlockSpec((TM, w), lambda i: (i, 0))
    vec = pl.BlockSpec((1, D_MODEL), lambda i: (0, 0))
    return pl.pallas_call(
        body, grid=(S // TM,), name="mlp_fwd_loss",
        in_specs=[row(D_MODEL), row(D_MODEL), row(D_MODEL), _resident(w_mi.shape), _resident(w_mo.shape),
                  _resident(g_f.shape)],
        out_specs=[row(D_FF), row(D_MODEL), row(D_MODEL), vec, vec],
        out_shape=[SDS((S, D_FF), BF16), SDS((S, D_MODEL), F32), SDS((S, D_MODEL), BF16),
                   SDS((1, D_MODEL), F32), SDS((1, D_MODEL), F32)],
        compiler_params=_cp("arbitrary"),
    )(m, h1, target, w_mi, w_mo, g_f)


def _mlp_bwd(dh2, dh2b, hid, h1, w_mi, w_mo, g_mlp):
    S = dh2.shape[0]

    def body(dh2_ref, dh2b_ref, hid_ref, h1_ref, wmi_ref, wmo_ref, g_ref, dpre_ref, dh1_ref, dh1b_ref, dg_ref):
        @pl.when(pl.program_id(0) == 0)
        def _():
            dg_ref[...] = jnp.zeros_like(dg_ref)

        d2 = dh2b_ref[...]
        dm = jnp.zeros((TM, D_MODEL), F32)
        dhid_next = _dot_nt(d2, wmo_ref[0])
        for c in range(N_CHIPS):
            cs = slice(c * D_MODEL, (c + 1) * D_MODEL)
            dhid = dhid_next
            if c + 1 < N_CHIPS:
                dhid_next = _dot_nt(d2, wmo_ref[c + 1])
            dpre = (dhid * (2.0 * jnp.sqrt(hid_ref[:, cs].astype(F32)))).astype(BF16)
            dpre_ref[:, cs] = dpre
            dm = dm + _dot_nt(dpre, wmi_ref[c])
        g = g_ref[...]
        _, hh, r = _rms_fwd(h1_ref[...], g)
        dg_ref[...] += jnp.sum(dm * hh, axis=0, keepdims=True)
        dh1 = dh2_ref[...] + _rms_bwd(dm, hh, r, g)
        dh1_ref[...] = dh1
        dh1b_ref[...] = dh1.astype(BF16)

    row = lambda w: pl.BlockSpec((TM, w), lambda i: (i, 0))
    return pl.pallas_call(
        body, grid=(S // TM,), name="mlp_bwd",
        in_specs=[row(D_MODEL), row(D_MODEL), row(D_FF), row(D_MODEL), _resident(w_mi.shape),
                  _resident(w_mo.shape), _resident(g_mlp.shape)],
        out_specs=[row(D_FF), row(D_MODEL), row(D_MODEL), pl.BlockSpec((1, D_MODEL), lambda i: (0, 0))],
        out_shape=[SDS((S, D_FF), BF16), SDS((S, D_MODEL), F32), SDS((S, D_MODEL), BF16), SDS((1, D_MODEL), F32)],
        compiler_params=_cp("arbitrary"),
    )(dh2, dh2b, hid, h1, w_mi, w_mo, g_mlp)


def _mixer_bwd(dh1b, a, p, mixed, gates, w_out, w_ao, w_po, wbd, scale, stat_reduce, sums):
    S = a.shape[0]
    n_tiles = S // TMB
    n = len(sums)

    def body(*refs):
        (dh1b_ref, a_ref, p_ref, mixed_ref, gate_ref, wout_ref, wao_ref, wpo_ref, wbd_ref, sc_ref,
         ones_ref) = refs[0:11]
        sum_refs = refs[11:11 + n]
        (da1_ref, dp1_ref, dgate_ref, da0_ref, dag1_ref, dag2_ref, dd0_ref, dd1_ref, dd2_ref,
         dmixed_ref, dqp_ref, dscale_ref) = refs[11 + n:23 + n]
        land_refs = refs[23 + n:23 + 2 * n]
        s_da, s_dd, send_sem, recv_sem = refs[23 + 2 * n:]
        i = pl.program_id(0)

        @pl.when(i == 0)
        def _():
            dscale_ref[...] = jnp.zeros_like(dscale_ref)
            for cpy in _chip_sum_copies(sum_refs, land_refs, send_sem, recv_sem):
                cpy.start()

        dmerged = _dot_nt(dh1b_ref[...], wout_ref[...])
        a = a_ref[...]
        p = p_ref[...]
        da = jnp.zeros((TMB, GROUP_W), F32)
        dp = jnp.zeros((TMB, POOL_W), F32)
        for j in range(N_CHIPS):
            js = slice(j * CHUNK, (j + 1) * CHUNK)
            sa = _sigmoid(gate_ref[:, js])
            sp = _sigmoid(gate_ref[:, D_MODEL + j * CHUNK:D_MODEL + (j + 1) * CHUNK])
            dmj = dmerged[:, js]
            da1 = (dmj * sa).astype(BF16)
            dp1 = (dmj * sp).astype(BF16)
            da1_ref[:, js] = da1
            dp1_ref[:, js] = dp1
            dgate_ref[j] = (dmj * _dot(a, wao_ref[j]) * sa * (1.0 - sa)).astype(BF16)
            dgate_ref[N_CHIPS + j] = (dmj * _dot(p, wpo_ref[j]) * sp * (1.0 - sp)).astype(BF16)
            da = da + _dot_nt(da1, wao_ref[j])
            dp = dp + _dot_nt(dp1, wpo_ref[j])

        dd = _dot_split(da * a.astype(F32), ones_ref[...])
        da0_ref[0] = da.astype(BF16)
        dd0_ref[0] = dd
        for h in range(2):
            s_da[h] = da[:, h * PAIR_W:(h + 1) * PAIR_W]
        s_dd[0] = dd
        for refs, d in (((dag1_ref, dd1_ref), DILATIONS[1]), ((dag2_ref, dd2_ref), DILATIONS[2])):
            for r in range(d):
                for h in range(2):
                    hs = slice(h * PAIR_W, (h + 1) * PAIR_W)
                    refs[0][r, :, hs] = s_da[h, pl.ds(r, TMB // d, stride=d), :].astype(BF16)
                refs[1][r] = s_dd[0, pl.ds(r, TMB // d, stride=d), :]

        sc = sc_ref[...]
        dscale_ref[...] += jnp.sum(dp * mixed_ref[...].astype(F32), axis=0, keepdims=True)
        dmixed = (dp * sc).astype(BF16)
        dmixed_ref[...] = dmixed
        inv_cnt, _ = _pool_inv_count(i, TMB)
        dqp_ref[...] = (_dot_nt(dmixed, wbd_ref[...]) * inv_cnt).astype(BF16)

        @pl.when(i == n_tiles - 1)
        def _():
            for cpy in _chip_sum_copies(sum_refs, land_refs, send_sem, recv_sem):
                cpy.wait()

    row = lambda w: pl.BlockSpec((TMB, w), lambda i: (i, 0))
    grp_spec = lambda d: pl.BlockSpec((d, TMB // d, GROUP_W), lambda i: (0, i, 0))
    stat_spec = lambda d: pl.BlockSpec((d, TMB // d, STAT_W), lambda i: (0, i, 0))
    d0, d1, d2 = DILATIONS
    res = pl.pallas_call(
        body, grid=(n_tiles,), name="mixer_bwd",
        in_specs=[row(D_MODEL), row(GROUP_W), row(POOL_W), row(POOL_W), row(2 * D_MODEL),
                  _resident(w_out.shape), _resident(w_ao.shape), _resident(w_po.shape), _resident(wbd.shape),
                  _resident(scale.shape), _resident(stat_reduce.shape)] + [ANY] * n,
        out_specs=[row(D_MODEL), row(D_MODEL), pl.BlockSpec((2 * N_CHIPS, TMB, CHUNK), lambda i: (0, i, 0)),
                   grp_spec(d0), grp_spec(d1), grp_spec(d2), stat_spec(d0), stat_spec(d1), stat_spec(d2),
                   row(POOL_W), row(POOL_W), pl.BlockSpec((1, POOL_W), lambda i: (0, 0))] + [ANY] * n,
        out_shape=[SDS((S, D_MODEL), BF16), SDS((S, D_MODEL), BF16), SDS((2 * N_CHIPS, S, CHUNK), BF16)]
        + [SDS((d, S // d, GROUP_W), BF16) for d in DILATIONS]
        + [SDS((d, S // d, STAT_W), F32) for d in DILATIONS]
        + [SDS((S, POOL_W), BF16), SDS((S, POOL_W), BF16), SDS((1, POOL_W), F32)]
        + [SDS(t.shape, t.dtype) for t in sums],
        scratch_shapes=[pltpu.VMEM((2, TMB, PAIR_W), F32), pltpu.VMEM((1, TMB, STAT_W), F32),
                        pltpu.SemaphoreType.DMA((3 * n,)), pltpu.SemaphoreType.DMA((3 * n,))],
        compiler_params=_cp("arbitrary"),
    )(dh1b, a, p, mixed, gates, w_out, w_ao, w_po, wbd, scale, stat_reduce, *sums)
    return res[:12], res[12:]


def _attn_bwd(qkv, da, lt, dd, grp, packs=None):
    d = DILATIONS[grp]
    L = qkv.shape[2]
    RR, RB, nb = _attn_tiles(grp, L)
    n_j = d // RR
    hosted = packs is not None

    def body(*refs):
        q_ref, kc_ref, kp_ref, vc_ref, vp_ref, da_ref, lt_ref, dd_ref = refs[0:8]
        dq_ref, dk_ref, dv_ref = refs[8 + hosted:11 + hosted]
        dk_acc, dv_acc = refs[11 + 2 * hosted:13 + 2 * hosted]
        i = pl.program_id(1)
        if hosted:
            j = pl.program_id(0)
            start, relay, finish = _pack_allgather(refs[11 + hosted], *refs[13 + 2 * hosted:])
            pl.when((j == 0) & (i == 0))(start)
            pl.when((j == 0) & (i == nb // 2))(relay)

        @pl.when(i == 0)
        def _():
            dk_acc[...] = jnp.zeros_like(dk_acc)
            dv_acc[...] = jnp.zeros_like(dv_acc)

        def compute(cur, prv):
            dk_acc[cur] = jnp.zeros((RR, RB * BAND, GROUP_W), F32)
            dv_acc[cur] = jnp.zeros((RR, RB * BAND, GROUP_W), F32)
            biases, col = _band_bias(grp, d)
            first_keys_ok = (col >= BAND) | (i > 0)
            is_a = lax.broadcasted_iota(jnp.int32, (BAND, PAIR_W), 1) < HEAD_W
            for rr in range(RR):
                for rb in range(RB):
                    rows = slice(rb * BAND, (rb + 1) * BAND)
                    for cp in range(2):
                        cs = slice(cp * PAIR_W, (cp + 1) * PAIR_W)
                        q2 = q_ref[rr, rows, cs]
                        da2 = da_ref[rr, rows, cs]
                        lt2 = lt_ref[rr, rows, :]
                        dd2 = dd_ref[rr, rows, :]
                        kcat = _kv_tile(kc_ref, kp_ref, rr, rb, cs)
                        vcat = _kv_tile(vc_ref, vp_ref, rr, rb, cs)
                        q2t = q2.T
                        da2t = da2.T
                        dqs, dkts, dvts, scores, dpvs = [], [], [], [], []
                        for h2 in range(2):
                            sel = is_a if h2 == 0 else jnp.logical_not(is_a)
                            b = biases[cp * 2 + h2]
                            if rb == 0:
                                b = jnp.where(first_keys_ok, b, NEG)
                            scores.append(_dot_nt(jnp.where(sel, q2, jnp.zeros_like(q2)), kcat) + b)
                            dpvs.append(_dot_nt(jnp.where(sel, da2, jnp.zeros_like(da2)), vcat))
                        for h2 in range(2):
                            lane0 = h2 * HEAD_W
                            p = jnp.exp(scores[h2] - _head_col(lt2, cp * 2 + h2))
                            ds = (p * (dpvs[h2] - _head_col(dd2, cp * 2 + h2))).astype(BF16)
                            dqs.append(_dot(ds, kcat))
                            dkts.append(_dot(q2t[lane0:lane0 + HEAD_W, :], ds))
                            dvts.append(_dot(da2t[lane0:lane0 + HEAD_W, :], p.astype(BF16)))
                        dq_ref[rr, rows, cs] = (jnp.where(is_a, dqs[0], dqs[1]) * 0.125).astype(BF16)
                        dkc = jnp.concatenate(dkts, axis=0).T
                        dvc = jnp.concatenate(dvts, axis=0).T
                        if rb == 0:
                            last = slice((RB - 1) * BAND, RB * BAND)
                            dk_acc[prv, rr, last, cs] += dkc[0:BAND]
                            dv_acc[prv, rr, last, cs] += dvc[0:BAND]
                            dk_acc[cur, rr, 0:BAND, cs] += dkc[BAND:]
                            dv_acc[cur, rr, 0:BAND, cs] += dvc[BAND:]
                        else:
                            both = slice((rb - 1) * BAND, (rb + 1) * BAND)
                            dk_acc[cur, rr, both, cs] += dkc
                            dv_acc[cur, rr, both, cs] += dvc

        def flush(prv):
            dk_ref[...] = dk_acc[prv].astype(BF16)
            dv_ref[...] = dv_acc[prv].astype(BF16)

        for parity in (0, 1):
            on = (i % 2) == parity
            pl.when(on & (i < nb))(functools.partial(compute, parity, 1 - parity))
            pl.when(on & (i > 0))(functools.partial(flush, 1 - parity))
        if hosted:
            pl.when((j == n_j - 1) & (i == nb))(finish)

    qi = lambda i: jnp.minimum(i, nb - 1)
    cur_w = lambda w: pl.BlockSpec((None, RR, RB * BAND, GROUP_W), lambda j, i: (w, j, qi(i), 0))
    prev_w = lambda w: pl.BlockSpec((None, RR, BAND, GROUP_W),
                                    lambda j, i: (w, j, jnp.maximum(qi(i) * RB - 1, 0), 0))
    blk = pl.BlockSpec((RR, RB * BAND, GROUP_W), lambda j, i: (j, qi(i), 0))
    stat_blk = pl.BlockSpec((RR, RB * BAND, STAT_W), lambda j, i: (j, qi(i), 0))
    late = pl.BlockSpec((RR, RB * BAND, GROUP_W), lambda j, i: (j, jnp.maximum(i - 1, 0), 0))
    extra = [packs] if hosted else []
    return pl.pallas_call(
        body, grid=(n_j, nb + 1), name=f"attn_bwd_g{grp}",
        in_specs=[cur_w(0), cur_w(1), prev_w(1), cur_w(2), prev_w(2), blk, stat_blk, stat_blk] + [ANY] * hosted,
        out_specs=[blk, late, late] + [ANY] * hosted,
        out_shape=[SDS((d, L, GROUP_W), BF16)] * 3 + [SDS(t.shape, t.dtype) for t in extra],
        scratch_shapes=[pltpu.VMEM((2, RR, RB * BAND, GROUP_W), F32), pltpu.VMEM((2, RR, RB * BAND, GROUP_W), F32)]
        + [pltpu.SemaphoreType.DMA((N_DEV - 1,))] * (2 * hosted),
        input_output_aliases={8: 3} if hosted else {},
        compiler_params=_cp("arbitrary" if hosted else "parallel", "arbitrary"),
    )(qkv, qkv, qkv, qkv, qkv, da, lt, dd, *extra)


def _dz_assemble(dqkv, dqp):
    S = dqp.shape[0]
    tm = min(TMZ, S)
    n_tiles = S // tm

    def body(*refs):
        dqkv_refs = refs[0:9]
        dqp_ref, halo_ref = refs[9:11]
        dz_ref, s_ref, ext_ref = refs[11:]
        i = pl.program_id(0)

        for grp in range(3):
            for which in range(3):
                n = which * 3 + grp
                ref = dqkv_refs[grp * 3 + which]
                if DILATIONS[grp] == 1:
                    dz_ref[n] = ref[0]
                else:
                    _interleave_load(ref, (), s_ref, DILATIONS[grp], tm)
                    for h in range(2):
                        dz_ref[n, :, h * PAIR_W:(h + 1) * PAIR_W] = s_ref[h].astype(BF16)

        dqp = dqp_ref[...].astype(F32)
        ext_ref[0:tm, :] = dqp
        ext_ref[tm:, :] = jnp.where(i < n_tiles - 1, halo_ref[...].astype(F32), 0.0)
        sums = []
        acc = ext_ref[...]
        for k in (1, 2, 4, 8):
            acc = acc + pltpu.roll(acc, tm + POOL_HALO - k, 0)
            sums.append(acc[0:tm, :])
        inv_cnt, col = _pool_inv_count(i, tm)
        dpz = _pool_column_select(col, sums) - dqp / inv_cnt
        for t in range(3):
            dz_ref[9 + t] = dpz[:, t * CHUNK:(t + 1) * CHUNK].astype(BF16)

    row = lambda w: pl.BlockSpec((tm, w), lambda i: (i, 0))
    grp_spec = lambda d: pl.BlockSpec((d, tm // d, GROUP_W), lambda i: (0, i, 0))
    halo = pl.BlockSpec((POOL_HALO, POOL_W),
                        lambda i: (jnp.minimum((i + 1) * (tm // POOL_HALO), S // POOL_HALO - 1), 0))
    flat = [t for grp in range(3) for t in dqkv[grp]]
    return pl.pallas_call(
        body, grid=(n_tiles,), name="dz_assemble",
        in_specs=[grp_spec(DILATIONS[grp]) for grp in range(3) for _ in range(3)] + [row(POOL_W), halo],
        out_specs=pl.BlockSpec((N_DZ_CHUNKS, tm, CHUNK), lambda i: (0, i, 0)),
        out_shape=SDS((N_DZ_CHUNKS, S, CHUNK), BF16),
        scratch_shapes=[pltpu.VMEM((2, tm, PAIR_W), F32), pltpu.VMEM((tm + POOL_HALO, POOL_W), F32)],
        compiler_params=_cp("parallel"),
    )(*flat, dqp, dqp)


def _inproj_dx(dz, dgates, dh1, x, g, w_in, sums):
    S = x.shape[0]
    n_tiles = S // TM
    n = len(sums)

    def body(*refs):
        dz_ref, dgate_ref, dh1_ref, x_ref, g_ref, w_ref = refs[0:6]
        sum_refs = refs[6:6 + n]
        dx_ref, dg_ref = refs[6 + n:8 + n]
        land_refs = refs[8 + n:8 + 2 * n]
        sems = refs[8 + 2 * n:]
        i = pl.program_id(0)

        def copies():
            return _chip_sum_copies(sum_refs, land_refs, *sems)

        @pl.when(i == 0)
        def _():
            dg_ref[...] = jnp.zeros_like(dg_ref)
            for cpy in copies():
                cpy.start()

        du = jnp.zeros((TM, D_MODEL), F32)
        for k in range(N_CHUNKS):
            dzk = dz_ref[k] if k < N_DZ_CHUNKS else dgate_ref[k - N_DZ_CHUNKS]
            du = du + _dot_nt(dzk, _w_in_chunk(w_ref, k))
        gv = g_ref[...]
        _, xh, r = _rms_fwd(x_ref[...], gv)
        dg_ref[...] += jnp.sum(du * xh, axis=0, keepdims=True)
        dx_ref[...] = dh1_ref[...] + _rms_bwd(du, xh, r, gv)

        @pl.when(i == n_tiles - 1)
        def _():
            for cpy in copies():
                cpy.wait()

    row = lambda w: pl.BlockSpec((TM, w), lambda i: (i, 0))
    res = pl.pallas_call(
        body, grid=(n_tiles,), name="inproj_dx",
        in_specs=[pl.BlockSpec((N_DZ_CHUNKS, TM, CHUNK), lambda i: (0, i, 0)),
                  pl.BlockSpec((N_CHUNKS - N_DZ_CHUNKS, TM, CHUNK), lambda i: (0, i, 0)),
                  row(D_MODEL), row(D_MODEL), _resident(g.shape), _resident(w_in.shape)] + [ANY] * n,
        out_specs=[row(D_MODEL), pl.BlockSpec((1, D_MODEL), lambda i: (0, 0))] + [ANY] * n,
        out_shape=[SDS((S, D_MODEL), F32), SDS((1, D_MODEL), F32)] + [SDS(t.shape, t.dtype) for t in sums],
        scratch_shapes=[pltpu.SemaphoreType.DMA((3 * n,)), pltpu.SemaphoreType.DMA((3 * n,))],
        compiler_params=_cp("arbitrary"),
    )(dz, dgates, dh1, x, g, w_in, *sums)
    return res[0], res[1], res[2:]


def _wgrad(a, b, name, *, out_shape, a_spec, b_spec, out_spec, grid, n_out_cols=None, fill=None, narrow=True):
    k_axis = len(grid) - 1
    n_k = grid[k_axis]
    n_out = 2 if narrow else 1

    def body(a_ref, b_ref, *rest):
        o_ref = rest[-n_out]

        @pl.when(pl.program_id(k_axis) == 0)
        def _():
            o_ref[...] = jnp.zeros_like(o_ref)

        at = a_ref[...]
        if n_out_cols is None:
            o_ref[...] += _dot_tn(at, b_ref[...])
        elif n_out_cols[0] == "lead_both":
            for t in range(b_ref.shape[0]):
                o_ref[t] += _dot_tn(at, b_ref[t])
        else:
            w = n_out_cols[1]
            for t in range(o_ref.shape[0]):
                o_ref[t] += _dot_tn(at, b_ref[:, t * w:(t + 1) * w])

        if narrow:
            @pl.when(pl.program_id(k_axis) == n_k - 1)
            def _():
                rest[-1][...] = o_ref[...].astype(BF16)

    sem = ("parallel",) * k_axis + ("arbitrary",)
    extra = [] if fill is None else list(fill) if narrow else [fill]
    shapes = [out_shape, SDS(out_shape.shape, BF16)] if narrow else out_shape
    return pl.pallas_call(body, grid=grid, name=name, in_specs=[a_spec, b_spec] + [ANY] * len(extra),
                          out_specs=[out_spec] * n_out if narrow else out_spec, out_shape=shapes,
                          input_output_aliases={2 + t: t for t in range(len(extra))},
                          compiler_params=_cp(*sem))(a, b, *extra)


def _wgrad_in(u, dz, dgates):
    bk = min(BK, u.shape[0])
    nk = u.shape[0] // bk
    g = WGRAD_IN_GROUP
    kw = dict(n_out_cols=("lead_both", CHUNK), a_spec=pl.BlockSpec((bk, D_MODEL), lambda j, k: (k, 0)),
              b_spec=pl.BlockSpec((g, bk, CHUNK), lambda j, k: (j, k, 0)),
              out_shape=SDS((N_CHUNKS, D_MODEL, CHUNK), F32))
    first = _wgrad(u, dz, "wgrad_in_qkvp", grid=(N_DZ_CHUNKS // g, nk),
                   out_spec=pl.BlockSpec((g, D_MODEL, CHUNK), lambda j, k: (j, 0, 0)), **kw)
    both = _wgrad(u, dgates, "wgrad_in_gates", grid=((N_CHUNKS - N_DZ_CHUNKS) // g, nk), fill=first,
                  out_spec=pl.BlockSpec((g, D_MODEL, CHUNK), lambda j, k: (N_DZ_CHUNKS // g + j, 0, 0)), **kw)
    return [t.reshape(N_CHIPS, CHUNKS_PER_SHARD * D_MODEL, CHUNK) for t in both]


def _wgrads_mixer(a, da1, p, dp1, merged, dh1b, pooled, dmixed):
    bk = min(BK, a.shape[0])
    nk = a.shape[0] // bk
    g_ao = _wgrad(
        a, da1, "wgrad_att_out", grid=(nk,), n_out_cols=("cols_b", CHUNK),
        a_spec=pl.BlockSpec((bk, GROUP_W), lambda k: (k, 0)),
        b_spec=pl.BlockSpec((bk, D_MODEL), lambda k: (k, 0)),
        out_spec=pl.BlockSpec((N_CHIPS, GROUP_W, CHUNK), lambda k: (0, 0, 0)),
        out_shape=SDS((N_CHIPS, GROUP_W, CHUNK), F32))
    g_po = _wgrad(
        p, dp1, "wgrad_pool_out", grid=(nk,), n_out_cols=("cols_b", CHUNK),
        a_spec=pl.BlockSpec((bk, POOL_W), lambda k: (k, 0)),
        b_spec=pl.BlockSpec((bk, D_MODEL), lambda k: (k, 0)),
        out_spec=pl.BlockSpec((N_CHIPS, POOL_W, CHUNK), lambda k: (0, 0, 0)),
        out_shape=SDS((N_CHIPS, POOL_W, CHUNK), F32))
    g_out = _wgrad(
        merged, dh1b, "wgrad_out", grid=(nk,),
        a_spec=pl.BlockSpec((bk, D_MODEL), lambda k: (k, 0)),
        b_spec=pl.BlockSpec((bk, D_MODEL), lambda k: (k, 0)),
        out_spec=pl.BlockSpec((D_MODEL, D_MODEL), lambda k: (0, 0)),
        out_shape=SDS((D_MODEL, D_MODEL), F32))
    g_bd = _wgrad(
        pooled, dmixed, "wgrad_pool_grp", grid=(nk,),
        a_spec=pl.BlockSpec((bk, POOL_W), lambda k: (k, 0)),
        b_spec=pl.BlockSpec((bk, POOL_W), lambda k: (k, 0)),
        out_spec=pl.BlockSpec((POOL_W, POOL_W), lambda k: (0, 0)),
        out_shape=SDS((POOL_W, POOL_W), F32), narrow=False)
    g_out = [t.reshape(N_CHIPS, D_MODEL // N_CHIPS, D_MODEL) for t in g_out]
    return [g_ao, g_po, g_out], g_bd


def _wgrads_mlp(m, dpre, hid, dh2b):
    bk = min(BK, m.shape[0])
    nk = m.shape[0] // bk
    g_mi = _wgrad(
        m, dpre, "wgrad_mlp_in", grid=(N_CHIPS, nk),
        a_spec=pl.BlockSpec((bk, D_MODEL), lambda c, k: (k, 0)),
        b_spec=pl.BlockSpec((bk, D_MODEL), lambda c, k: (k, c)),
        out_spec=pl.BlockSpec((None, D_MODEL, D_MODEL), lambda c, k: (c, 0, 0)),
        out_shape=SDS((N_CHIPS, D_MODEL, D_MODEL), F32))
    g_mo = _wgrad(
        hid, dh2b, "wgrad_mlp_out", grid=(N_CHIPS, nk),
        a_spec=pl.BlockSpec((bk, D_MODEL), lambda c, k: (k, c)),
        b_spec=pl.BlockSpec((bk, D_MODEL), lambda c, k: (k, 0)),
        out_spec=pl.BlockSpec((None, D_MODEL, D_MODEL), lambda c, k: (c, 0, 0)),
        out_shape=SDS((N_CHIPS, D_MODEL, D_MODEL), F32))
    return [g_mi, g_mo]


def _mesh_place():
    x, y, c = lax.axis_index("x"), lax.axis_index("y"), lax.axis_index("c")
    other_chips = [(x, 1 - y), (1 - x, y), (1 - x, 1 - y)]
    return x, y, c, other_chips


ANY = pl.BlockSpec(memory_space=pl.ANY)


def _weight_half_copies(shard_refs, buf_refs, rows, send_sem, recv_sem):
    x, y, c, chips = _mesh_place()
    me = 2 * x + y
    copies = []
    for w, r_full in enumerate(rows):
        rh = r_full // 2
        for r, (px, py) in enumerate(chips):
            k = w * 3 + r
            copies.append(pltpu.make_async_remote_copy(
                src_ref=shard_refs[w].at[pl.ds(c * rh, rh), :], dst_ref=buf_refs[w].at[me, pl.ds(c * rh, rh), :],
                send_sem=send_sem.at[k], recv_sem=recv_sem.at[k], device_id=(px, py, c), device_id_type=MESH))
    return copies


def _pair_forward_copies(buf_refs, rows, send_sem, recv_sem):
    x, y, c, chips = _mesh_place()
    out = []
    for w, r_full in enumerate(rows):
        rh = r_full // 2
        for r, (px, py) in enumerate(chips):
            k = w * 3 + r
            landed = buf_refs[w].at[2 * px + py, pl.ds(c * rh, rh), :]
            theirs = buf_refs[w].at[2 * px + py, pl.ds((1 - c) * rh, rh), :]
            mk = lambda ref: pltpu.make_async_remote_copy(
                src_ref=ref, dst_ref=ref, send_sem=send_sem.at[k], recv_sem=recv_sem.at[k],
                device_id=(x, y, 1 - c), device_id_type=MESH)
            out.append((mk(landed), mk(theirs)))
    return out


def _place_own(block, n_slots, slot):
    buf = lax.empty((n_slots,) + block.shape, block.dtype)
    return lax.dynamic_update_slice(buf, block[None], (slot,) + (0,) * block.ndim)


def _pair_forward(bufs, rows, name):
    n = len(bufs)

    def body(*refs):
        dst = refs[n:2 * n]
        send_sem, recv_sem = refs[2 * n:]
        fwds = _pair_forward_copies(dst, rows, send_sem, recv_sem)
        for fwd, _ in fwds:
            fwd.start()
        for fwd, landing in fwds:
            landing.wait_recv()
            fwd.wait_send()

    return pl.pallas_call(
        body, name=name,
        in_specs=[ANY] * n, out_specs=[ANY] * n,
        out_shape=[SDS(b.shape, b.dtype) for b in bufs],
        scratch_shapes=[pltpu.SemaphoreType.DMA((3 * n,))] * 2,
        input_output_aliases={w: w for w in range(n)},
    )(*bufs)


def _chip_sum_copies(src, dst, send_sem, recv_sem):
    x, y, c, chips = _mesh_place()
    copies = []
    for w in range(len(src)):
        for r, (px, py) in enumerate(chips):
            k = w * 3 + r
            copies.append(pltpu.make_async_remote_copy(
                src_ref=src[w].at[r + 1], dst_ref=dst[w].at[r + 1], send_sem=send_sem.at[k], recv_sem=recv_sem.at[k],
                device_id=(px, py, c), device_id_type=MESH))
    return copies


def _pair_exchange(grads):
    n = len(grads)

    def body(*refs):
        src, dst = refs[:n], refs[n:2 * n]
        send_sem, recv_sem = refs[2 * n:]
        x, y, c, _ = _mesh_place()
        copies = []
        for w in range(n):
            rh = grads[w].shape[1] // 2
            copies.append(pltpu.make_async_remote_copy(
                src_ref=src[w].at[:, pl.ds((1 - c) * rh, rh), :], dst_ref=dst[w],
                send_sem=send_sem.at[w], recv_sem=recv_sem.at[w],
                device_id=(x, y, 1 - c), device_id_type=MESH))
            copies[-1].start()
        for cpy in copies:
            cpy.wait()

    return pl.pallas_call(
        body, name="grad_pair_exchange",
        in_specs=[ANY] * n, out_specs=[ANY] * n,
        out_shape=[SDS((N_CHIPS, g.shape[1] // 2, g.shape[2]), g.dtype) for g in grads],
        scratch_shapes=[pltpu.SemaphoreType.DMA((n,)), pltpu.SemaphoreType.DMA((n,))],
    )(*grads)


def _pair_sum(place, grad, recv, name):
    _, R, C = grad.shape
    rh = R // 2
    br = _row_block(rh, max(256, ELEMENTWISE_BLOCK // C))
    nbh = rh // br

    def body(place_ref, g_ref, r_ref, own_ref, sums_ref):
        s = g_ref[...] + r_ref[...].astype(F32)

        @pl.when(pl.program_id(1) == 0)
        def _():
            own_ref[...] = s

        sums_ref[...] = s.astype(BF16)

    slot = lambda rel, pr: jnp.bitwise_xor(pr[0], rel)
    return pl.pallas_call(
        body, name=name,
        grid_spec=pltpu.PrefetchScalarGridSpec(
            num_scalar_prefetch=1, grid=(nbh, N_CHIPS),
            in_specs=[pl.BlockSpec((None, br, C), lambda i, rel, pr: (slot(rel, pr), pr[1] * nbh + i, 0)),
                      pl.BlockSpec((None, br, C), lambda i, rel, pr: (slot(rel, pr), i, 0))],
            out_specs=[pl.BlockSpec((br, C), lambda i, rel, pr: (i, 0)),
                       pl.BlockSpec((None, br, C), lambda i, rel, pr: (rel, i, 0))]),
        out_shape=[SDS((rh, C), F32), SDS((N_CHIPS, rh, C), BF16)],
        compiler_params=_cp("parallel", "arbitrary"),
    )(place, grad, recv)


def _chip_sum(place, own, recv, name):
    rh, C = own.shape
    br = _row_block(rh, max(256, ELEMENTWISE_BLOCK // C))
    nbh = rh // br

    def body(place_ref, own_ref, r_ref, o_ref):
        o_ref[...] = ((own_ref[...] + r_ref[1].astype(F32)) + r_ref[2].astype(F32)) + r_ref[3].astype(F32)

    return pl.pallas_call(
        body, name=name,
        grid_spec=pltpu.PrefetchScalarGridSpec(
            num_scalar_prefetch=1, grid=(nbh,),
            in_specs=[pl.BlockSpec((br, C), lambda i, pr: (i, 0)),
                      pl.BlockSpec((N_CHIPS, br, C), lambda i, pr: (0, i, 0))],
            out_specs=pl.BlockSpec((br, C), lambda i, pr: (pr[1] * nbh + i, 0))),
        out_shape=SDS((2 * rh, C), F32),
        compiler_params=_cp("parallel"),
    )(place, own, recv)


def _pack_allgather(all_ref, send_sem, recv_sem):
    x, y, c, chips = _mesh_place()
    sib = (x, y, 1 - c)

    def pack(dev, k, to):
        slot = 4 * dev[0] + 2 * dev[1] + dev[2]
        return pltpu.make_async_remote_copy(
            src_ref=all_ref.at[slot], dst_ref=all_ref.at[slot], send_sem=send_sem.at[k],
            recv_sem=recv_sem.at[k], device_id=to, device_id_type=MESH)

    first = [pack((x, y, c), 0, sib)] + [pack((x, y, c), 1 + r, (px, py, c)) for r, (px, py) in enumerate(chips)]
    relays = [pack((px, py, c), 4 + r, sib) for r, (px, py) in enumerate(chips)]

    def start():
        for cpy in first:
            cpy.start()

    def relay():
        for r, (px, py) in enumerate(chips):
            pack((px, py, c), 1 + r, (px, py, c)).wait_recv()
            relays[r].start()

    def finish():
        pack(sib, 0, sib).wait_recv()
        for r, (px, py) in enumerate(chips):
            pack((px, py, 1 - c), 4 + r, sib).wait_recv()
        for cpy in first + relays:
            cpy.wait_send()

    return start, relay, finish


def _finish_exchange(grads, late_all):
    n = len(grads)

    def body(*refs):
        dst, all_ref = refs[n + 1:2 * n + 1], refs[2 * n + 1]
        send_sem, recv_sem, ssend_sem, srecv_sem = refs[2 * n + 2:]
        x, y, c, _ = _mesh_place()
        start, relay, finish = _pack_allgather(all_ref, ssend_sem, srecv_sem)
        start()
        sends, landings = [], []
        for w in range(n):
            rh = grads[w].shape[0] // 2
            mk = lambda cc: pltpu.make_async_remote_copy(
                src_ref=dst[w].at[pl.ds(cc * rh, rh), :], dst_ref=dst[w].at[pl.ds(cc * rh, rh), :],
                send_sem=send_sem.at[w], recv_sem=recv_sem.at[w], device_id=(x, y, 1 - c), device_id_type=MESH)
            sends.append(mk(c))
            landings.append(mk(1 - c))
            sends[-1].start()
        relay()
        finish()
        for cpy in landings:
            cpy.wait_recv()
        for cpy in sends:
            cpy.wait_send()

    res = pl.pallas_call(
        body, name="grad_finish_exchange",
        in_specs=[ANY] * (n + 1), out_specs=[ANY] * (n + 1),
        out_shape=[SDS(g.shape, g.dtype) for g in grads] + [SDS(late_all.shape, late_all.dtype)],
        scratch_shapes=[pltpu.SemaphoreType.DMA((n,)), pltpu.SemaphoreType.DMA((n,)),
                        pltpu.SemaphoreType.DMA((N_DEV - 1,)), pltpu.SemaphoreType.DMA((N_DEV - 1,))],
        input_output_aliases={w: w for w in range(n + 1)},
    )(*grads, late_all)
    return res[:n], res[n]


def _adamw_math(w, g, m, v):
    m = ADAM_B1 * m + (1.0 - ADAM_B1) * g
    v = ADAM_B2 * v + (1.0 - ADAM_B2) * jnp.square(g)
    m_hat = m / (1.0 - ADAM_B1 ** ADAM_STEP)
    v_hat = v / (1.0 - ADAM_B2 ** ADAM_STEP)
    delta = -ADAM_LR * (m_hat / (jnp.sqrt(v_hat) + ADAM_EPS) + ADAM_WD * w)
    return delta, m, v


def _adamw(w, g, m, v, name):
    R, C = w.shape
    br = _row_block(R, 512)
    if g.ndim == 3:
        n_chunks, cw = g.shape[0], g.shape[2]
        g_spec = pl.BlockSpec((None, br, cw), lambda t, i: (t, i, 0))
    else:
        n_chunks, cw = 1, C
        g_spec = pl.BlockSpec((br, cw), lambda t, i: (i, t))

    def body(w_ref, g_ref, m_ref, v_ref, g_out_ref, d_ref, nm_ref, nv_ref):
        gv = g_ref[...]
        g_out_ref[...] = gv
        d_ref[...], nm_ref[...], nv_ref[...] = _adamw_math(w_ref[...], gv, m_ref[...], v_ref[...])

    spec = pl.BlockSpec((br, cw), lambda t, i: (i, t))
    return pl.pallas_call(
        body, grid=(n_chunks, R // br), name=name, in_specs=[spec, g_spec, spec, spec], out_specs=[spec] * 4,
        out_shape=[SDS((R, C), F32)] * 4, compiler_params=_cp("parallel", "parallel"),
    )(w, g, m, v)


def _small_sum_adamw(all_small, all_late, w, m, v):
    loss_row = PACK_ROWS - 8

    def body(all_ref, late_ref, w_ref, m_ref, v_ref, g_ref, d_ref, nm_ref, nv_ref, loss_ref):
        g = all_ref[0]
        late = late_ref[0]
        for k in range(1, N_DEV):
            g = g + all_ref[k]
            late = late + late_ref[k]
        g_ref[...] = g
        g_ref[PACK_LATE_ROW:PACK_LATE_ROW + 8, :] = late
        g = g_ref[...]
        d_ref[...], nm_ref[...], nv_ref[...] = _adamw_math(w_ref[...], g, m_ref[...], v_ref[...])
        total = jnp.sum(g[loss_row:loss_row + 1, :]) * (0.5 / D_MODEL)
        loss_ref[...] = jnp.full(loss_ref.shape, total, F32)

    full = lambda s: pl.BlockSpec(s, lambda i: (0,) * len(s))
    pack = (PACK_ROWS, D_MODEL)
    return pl.pallas_call(
        body, grid=(1,), name="small_sum_adamw",
        in_specs=[full((N_DEV,) + pack), full((N_DEV, 8, D_MODEL)), full(pack), full(pack), full(pack)],
        out_specs=[full(pack)] * 4 + [full((8, 128))],
        out_shape=[SDS(pack, F32)] * 4 + [SDS((8, 128), F32)],
        compiler_params=_cp("arbitrary"),
    )(all_small, all_late, w, m, v)


def _pack_small(grp, scale, g_mix, g_mlp, g_f, loss_lanes):
    def part(vec):
        vec = vec.reshape(1, -1)
        return jnp.pad(vec, ((0, 7), (0, D_MODEL - vec.shape[1])))
    return jnp.concatenate([grp.reshape(-1, D_MODEL), part(scale), part(g_mix), part(g_mlp), part(g_f),
                            part(loss_lanes)], axis=0)


def _unpack_small(pack):
    n_grp = len(POOL_WINDOWS) * POOL_GROUP_W * POOL_GROUP_W // D_MODEL
    grp = pack[:n_grp].reshape(1, len(POOL_WINDOWS), POOL_GROUP_W, POOL_GROUP_W)
    scale = pack[n_grp, :POOL_W].reshape(1, POOL_W)
    g_mix = pack[n_grp + 8].reshape(1, D_MODEL)
    g_mlp = pack[n_grp + 16].reshape(1, D_MODEL)
    g_f = pack[n_grp + 24].reshape(D_MODEL)
    return grp, scale, g_mix, g_mlp, g_f


def _block_diag(grp):
    out = jnp.zeros((POOL_W, POOL_W), grp.dtype)
    for k in range(len(POOL_WINDOWS)):
        out = lax.dynamic_update_slice(out, grp[k], (k * POOL_GROUP_W, k * POOL_GROUP_W))
    return out


def kernel(x, norm_mix_g, w_in, w_att_out, w_pool_grp, pool_scale, w_pool_out, w_out, norm_mlp_g, w_mlp_in, w_mlp_out, norm_final_g, loss_target, m_norm_mix_g, m_w_in, m_w_att_out, m_w_pool_grp, m_pool_scale, m_w_pool_out, m_w_out, m_norm_mlp_g, m_w_mlp_in, m_w_mlp_out, m_norm_final_g, v_norm_mix_g, v_w_in, v_w_att_out, v_w_pool_grp, v_pool_scale, v_w_pool_out, v_w_out, v_norm_mlp_g, v_w_mlp_in, v_w_mlp_out, v_norm_final_g):
    S = x.shape[1]
    xs, target = x[0], loss_target[0]
    big = [w_in[0], w_att_out[0], w_pool_out[0], w_out[0], w_mlp_in[0], w_mlp_out[0]]
    big_m = [m_w_in[0], m_w_att_out[0], m_w_pool_out[0], m_w_out[0], m_w_mlp_in[0], m_w_mlp_out[0]]
    big_v = [v_w_in[0], v_w_att_out[0], v_w_pool_out[0], v_w_out[0], v_w_mlp_in[0], v_w_mlp_out[0]]

    chip = 2 * lax.axis_index("x") + lax.axis_index("y")
    core = lax.axis_index("c")
    place = jnp.stack([chip, core]).astype(jnp.int32)
    names = ("w_in", "w_att_out", "w_pool_out", "w_out", "w_mlp_in", "w_mlp_out")

    shards = [w.astype(BF16) for w in big]
    bufs = [_place_own(sh, N_CHIPS, chip) for sh in shards]
    wbd = _block_diag(w_pool_grp[0]).astype(BF16)
    g_final = norm_final_g.reshape(1, D_MODEL)
    stat_expand, stat_reduce = _stat_matrices()

    u, z_own, landed_in = _norm_inproj_own(xs, norm_mix_g, shards[0], bufs[0])
    (wg_in,) = _pair_forward([landed_in], [shards[0].shape[0]], "w_in_pair_forward")
    (qkv0, qkv1, qkv2, pz, gates), (wg_ao, wg_po, wg_out) = _inproj_rest(u, z_own, wg_in, shards[1:4], bufs[1:4])
    wg_out = wg_out.reshape(D_MODEL, D_MODEL)
    qkv = (qkv0, qkv1, qkv2)
    att = [_attn_fwd(qkv[grp], grp) for grp in range(3)]
    (a, lt0, lt1, lt2, pooled, mixed, p, merged, h1, m), (wg_mi, wg_mo) = _mixer_out(
        [o for o, _ in att], [l for _, l in att], pz, gates, xs, wg_ao, wg_po, wbd, pool_scale, wg_out, norm_mlp_g,
        stat_expand, shards[4:], bufs[4:])
    hid, dh2, dh2b, loss_lanes, dg_final = _mlp_fwd_loss(m, h1, target, wg_mi, wg_mo, g_final)

    def pair_reduce(grads, grad_names):
        recv = _pair_exchange([narrow for _, narrow in grads])
        pair = [_pair_sum(place, g, r, f"pair_sum_{nm}") for (g, _), r, nm in zip(grads, recv, grad_names)]
        return [own for own, _ in pair], [s for _, s in pair]

    def chip_reduce(owns, landed_sums, grad_names):
        return [_chip_sum(place, own, r, f"chip_sum_{nm}") for own, r, nm in zip(owns, landed_sums, grad_names)]

    dpre, dh1, dh1b, dg_mlp = _mlp_bwd(dh2, dh2b, hid, h1, wg_mi, wg_mo, norm_mlp_g)
    own_mlp, sums_mlp = pair_reduce(_wgrads_mlp(m, dpre, hid, dh2b), names[4:])
    (da1, dp1, dgates, da0, dag1, dag2, dd0, dd1, dd2, dmixed, dqp, dscale), landed_mlp = _mixer_bwd(
        dh1b, a, p, mixed, gates, wg_out, wg_ao, wg_po, wbd, pool_scale, stat_reduce, sums_mlp)
    g_mi, g_mo = chip_reduce(own_mlp, landed_mlp, names[4:])
    grads_mixer, g_bd = _wgrads_mixer(a, da1, p, dp1, merged, dh1b, pooled, dmixed)

    zero = jnp.zeros((D_MODEL,), F32)
    g_grp = jnp.stack([g_bd[k * POOL_GROUP_W:(k + 1) * POOL_GROUP_W, k * POOL_GROUP_W:(k + 1) * POOL_GROUP_W]
                       for k in range(len(POOL_WINDOWS))])
    small = _pack_small(g_grp, dscale, zero, dg_mlp, dg_final, loss_lanes)
    *dqkv0, small_all = _attn_bwd(qkv[0], da0, lt0, dd0, 0, packs=_place_own(small, N_DEV, 2 * chip + core))
    dqkv = [dqkv0, _attn_bwd(qkv[1], dag1, lt1, dd1, 1), _attn_bwd(qkv[2], dag2, lt2, dd2, 2)]
    dz = _dz_assemble(dqkv, dqp)
    own_in, sums_in = pair_reduce([_wgrad_in(u, dz, dgates)] + grads_mixer, names[:4])
    dx, dg_mix, landed_in = _inproj_dx(dz, dgates, dh1, xs, norm_mix_g, wg_in, sums_in)
    g_in, g_ao, g_po, g_out = chip_reduce(own_in, landed_in, names[:4])
    late = jnp.pad(dg_mix, ((0, 7), (0, 0)))
    full, late_all = _finish_exchange([g_in, g_ao, g_po, g_out, g_mi, g_mo], _place_own(late, N_DEV, 2 * chip + core))

    small_w = _pack_small(w_pool_grp[0], pool_scale, norm_mix_g, norm_mlp_g, norm_final_g, zero)
    small_m = _pack_small(m_w_pool_grp[0], m_pool_scale, m_norm_mix_g, m_norm_mlp_g, m_norm_final_g, zero)
    small_v = _pack_small(v_w_pool_grp[0], v_pool_scale, v_norm_mix_g, v_norm_mlp_g, v_norm_final_g, zero)
    sg, sd, sm, sv, loss_tile = _small_sum_adamw(small_all, late_all, small_w, small_m, small_v)
    full = [full[0].reshape(CHUNKS_PER_SHARD, D_MODEL, CHUNK)] + list(full[1:])
    upd = [_adamw(w, g, mm, vv, f"adamw_{nm}") for w, g, mm, vv, nm in zip(big, full, big_m, big_v, names)]

    def ordered(small_pack, bigs):
        grp, scale, g_mix, g_mlp, g_f = _unpack_small(small_pack)
        b_in, b_ao, b_po, b_out, b_mi, b_mo = [b[None] for b in bigs]
        return (g_mix, b_in, b_ao, grp, scale, b_po, b_out, g_mlp, b_mi, b_mo, g_f)

    return (loss_tile[0, 0], dx[None],
            *ordered(sg, [t[0] for t in upd]),
            *ordered(sd, [t[1] for t in upd]),
            *ordered(sm, [t[2] for t in upd]),
            *ordered(sv, [t[3] for t in upd]))
```

```python
import functools

import jax
import jax.numpy as jnp
from jax import lax
from jax.experimental import pallas as pl
from jax.experimental.pallas import tpu as pltpu

F32 = jnp.float32
BF16 = jnp.bfloat16
SDS = jax.ShapeDtypeStruct
MESH = pl.DeviceIdType.MESH

D_MODEL = 1024
D_FF = 4096
N_CHIPS = 4
N_DEV = 8
DILATIONS = (1, 4, 16)
BAND = 128
GROUP_W = 256
PAIR_W = 128
HEAD_W = 64
STAT_W = 128
STAT_HEAD_W = 32
POOL_W = 768
POOL_GROUP_W = 192
POOL_WINDOWS = (2, 4, 8, 16)
POOL_HALO = 16
N_IN = 5120
CHUNK = 256
N_CHUNKS = N_IN // CHUNK
N_DZ_CHUNKS = 12
CHUNKS_PER_SHARD = 5
WGRAD_IN_GROUP = 4
NORM_EPS = 1e-6
ALIBI_MAX_BIAS = 8.0
N_HEADS = 12
NEG = -1e30

ADAM_LR, ADAM_B1, ADAM_B2, ADAM_EPS, ADAM_WD, ADAM_STEP = 0.001, 0.9, 0.999, 1e-08, 0.01, 10

TM = 512
TMB = 512
TMZ = 1024
ATT_TILE = ((1, 16), (4, 4), (16, 1))
BK = 4096
ELEMENTWISE_BLOCK = 1 << 20
VMEM_LIMIT = 56 * 1024 * 1024
PACK_ROWS = 184
PACK_LATE_ROW = 152

NT = (((1,), (1,)), ((), ()))
TN = (((0,), (0,)), ((), ()))


def _cp(*sem):
    return pltpu.CompilerParams(dimension_semantics=sem, vmem_limit_bytes=VMEM_LIMIT)


def _resident(shape):
    nd = len(shape)
    return pl.BlockSpec(shape, lambda *_: (0,) * nd, pipeline_mode=pl.Buffered(1))


def _row_block(rows, cap=256):
    return max(b for b in range(16, min(rows, cap) + 1, 16) if rows % b == 0)


def _dot(a, b):
    return jnp.dot(a, b, preferred_element_type=F32)


def _dot_nt(a, b):
    return lax.dot_general(a, b, NT, preferred_element_type=F32)


def _dot_tn(a, b):
    return lax.dot_general(a, b, TN, preferred_element_type=F32)


def _w_in_chunk(w_ref, n):
    return w_ref[n // CHUNKS_PER_SHARD, :, (n % CHUNKS_PER_SHARD) * CHUNK:(n % CHUNKS_PER_SHARD + 1) * CHUNK]


def _sigmoid(x):
    return 0.5 * jnp.tanh(0.5 * x.astype(F32)) + 0.5


def _rms_fwd(x, g):
    r = lax.rsqrt(jnp.mean(x * x, axis=-1, keepdims=True) + NORM_EPS)
    xh = x * r
    return xh * g, xh, r


def _rms_bwd(dy, xh, r, g):
    dxh = dy * g
    return r * (dxh - xh * jnp.mean(dxh * xh, axis=-1, keepdims=True))


def _per_head_lanes(cols):
    rows = cols[0].shape[0]
    lane = lax.broadcasted_iota(jnp.int32, (rows, STAT_W), 1)
    out = cols[3]
    for h in (2, 1, 0):
        out = jnp.where(lane < (h + 1) * STAT_HEAD_W, cols[h], out)
    return out


def _head_col(stat, h):
    return stat[:, h * STAT_HEAD_W:h * STAT_HEAD_W + 1]


def _stat_matrices():
    s = lax.broadcasted_iota(jnp.int32, (STAT_W, GROUP_W), 0)
    c = lax.broadcasted_iota(jnp.int32, (STAT_W, GROUP_W), 1)
    expand = (s == (c // HEAD_W) * STAT_HEAD_W).astype(BF16)
    reduce = (s // STAT_HEAD_W == c // HEAD_W).astype(BF16).T
    return expand, reduce


def _dot_split(x, m):
    hi = x.astype(BF16)
    lo = (x - hi.astype(F32)).astype(BF16)
    return _dot(hi, m) + _dot(lo, m)


def _deinterleave_store(val, s_ref, out_ref, lead, d, rows, dtype):
    if d == 1:
        out_ref[lead + (0,)] = val.astype(dtype)
        return
    for h in range(2):
        s_ref[h] = val[:, h * PAIR_W:(h + 1) * PAIR_W]
    for r in range(d):
        for h in range(2):
            out_ref[lead + (r, slice(None), slice(h * PAIR_W, (h + 1) * PAIR_W))] = (
                s_ref[h, pl.ds(r, rows // d, stride=d), :].astype(dtype))


def _interleave_load(in_ref, lead, s_ref, d, rows):
    for r in range(d):
        for h in range(2):
            s_ref[h, pl.ds(r, rows // d, stride=d), :] = (
                in_ref[lead + (r, slice(None), slice(h * PAIR_W, (h + 1) * PAIR_W))].astype(F32))


def _norm_inproj_own(x, g, w_own, buf):
    S = x.shape[0]
    tm = min(TMZ, S)
    n_tiles = S // tm

    def body(x_ref, g_ref, w_ref, shard_ref, buf_in, u_ref, z_ref, buf_ref, send_sem, recv_sem):
        i = pl.program_id(0)

        def copies():
            return _weight_half_copies([shard_ref], [buf_ref], [w_own.shape[0]], send_sem, recv_sem)

        @pl.when(i == 0)
        def _():
            for cpy in copies():
                cpy.start()

        u = _rms_fwd(x_ref[...], g_ref[...])[0].astype(BF16)
        u_ref[...] = u
        for t in range(CHUNKS_PER_SHARD):
            z_ref[t] = _dot(u, w_ref[:, t * CHUNK:(t + 1) * CHUNK]).astype(BF16)

        @pl.when(i == n_tiles - 1)
        def _():
            for cpy in copies():
                cpy.wait()

    row = lambda w: pl.BlockSpec((tm, w), lambda i: (i, 0))
    return pl.pallas_call(
        body, grid=(n_tiles,), name="norm_inproj_own",
        in_specs=[row(D_MODEL), _resident((1, D_MODEL)), _resident(w_own.shape), ANY, ANY],
        out_specs=[row(D_MODEL), pl.BlockSpec((CHUNKS_PER_SHARD, tm, CHUNK), lambda i: (0, i, 0)), ANY],
        out_shape=[SDS((S, D_MODEL), BF16), SDS((CHUNKS_PER_SHARD, S, CHUNK), BF16), SDS(buf.shape, buf.dtype)],
        scratch_shapes=[pltpu.SemaphoreType.DMA((3,)), pltpu.SemaphoreType.DMA((3,))],
        input_output_aliases={4: 2},
        compiler_params=_cp("arbitrary"),
    )(x, g, w_own, w_own, buf)


def _hosted_allgather(i, n_steps, shard_refs, buf_refs, rows, sems):
    send_sem, recv_sem, fsend_sem, frecv_sem = sems
    ici = lambda: _weight_half_copies(shard_refs, buf_refs, rows, send_sem, recv_sem)
    forward = lambda: _pair_forward_copies(buf_refs, rows, fsend_sem, frecv_sem)

    def begin():
        @pl.when(i == 0)
        def _():
            for cpy in ici():
                cpy.start()

        @pl.when(i == (3 * n_steps) // 4)
        def _():
            for cpy, (fwd, _) in zip(ici(), forward()):
                cpy.wait_recv()
                fwd.start()

    def end():
        @pl.when(i == n_steps - 1)
        def _():
            for cpy, (fwd, landing) in zip(ici(), forward()):
                landing.wait_recv()
                fwd.wait_send()
                cpy.wait_send()

    return begin, end


def _inproj_rest(u, z_own, w_in, shards, bufs):
    S = u.shape[0]
    n_tiles = S // TM
    n = len(shards)

    def body(*refs):
        u_ref, zown_ref, w_ref = refs[0:3]
        shard_refs = refs[3:3 + n]
        q0_ref, q1_ref, q2_ref, pz_ref, gate_ref = refs[3 + 2 * n:8 + 2 * n]
        buf_refs = refs[8 + 2 * n:8 + 3 * n]
        s_ref = refs[8 + 3 * n]
        i = pl.program_id(0)
        chip = 2 * lax.axis_index("x") + lax.axis_index("y")
        begin, end = _hosted_allgather(i, n_tiles, shard_refs, buf_refs, [sh.shape[0] for sh in shards],
                                       refs[9 + 3 * n:])
        begin()

        u = u_ref[...]
        qkv_refs = (q0_ref, q1_ref, q2_ref)

        def emit(k, zc):
            if k < 9:
                which, grp = k // 3, k % 3
                if which == 0:
                    zc = zc * 0.125
                _deinterleave_store(zc, s_ref, qkv_refs[grp], (which,), DILATIONS[grp], TM, BF16)
            elif k < N_DZ_CHUNKS:
                pz_ref[:, (k - 9) * CHUNK:(k - 8) * CHUNK] = zc.astype(BF16)
            else:
                gate_ref[:, (k - N_DZ_CHUNKS) * CHUNK:(k - N_DZ_CHUNKS + 1) * CHUNK] = zc.astype(BF16)

        def all_chunks(own_shard):
            for k in range(N_CHUNKS):
                if k // CHUNKS_PER_SHARD == own_shard:
                    emit(k, zown_ref[k % CHUNKS_PER_SHARD].astype(F32))
                else:
                    emit(k, _dot(u, _w_in_chunk(w_ref, k)))

        for shard in range(N_CHIPS):
            pl.when(chip == shard)(functools.partial(all_chunks, shard))
        end()

    row = lambda w: pl.BlockSpec((TM, w), lambda i: (i, 0))
    res = pl.pallas_call(
        body, grid=(n_tiles,), name="inproj_rest",
        in_specs=[row(D_MODEL), pl.BlockSpec((CHUNKS_PER_SHARD, TM, CHUNK), lambda i: (0, i, 0)),
                  _resident(w_in.shape)] + [ANY] * (2 * n),
        out_specs=[pl.BlockSpec((3, d, TM // d, GROUP_W), lambda i: (0, 0, i, 0)) for d in DILATIONS]
        + [row(POOL_W), row(2 * D_MODEL)] + [ANY] * n,
        out_shape=[SDS((3, d, S // d, GROUP_W), BF16) for d in DILATIONS]
        + [SDS((S, POOL_W), BF16), SDS((S, 2 * D_MODEL), BF16)] + [SDS(b.shape, b.dtype) for b in bufs],
        scratch_shapes=[pltpu.VMEM((2, TM, PAIR_W), F32)] + [pltpu.SemaphoreType.DMA((3 * n,))] * 4,
        input_output_aliases={3 + n + w: 5 + w for w in range(n)},
        compiler_params=_cp("arbitrary"),
    )(u, z_own, w_in, *shards, *bufs)
    return res[:5], res[5:]


def _band_bias(grp, d):
    row = lax.broadcasted_iota(jnp.int32, (BAND, 2 * BAND), 0)
    col = lax.broadcasted_iota(jnp.int32, (BAND, 2 * BAND), 1)
    steps = BAND + row - col
    valid = (steps >= 0) & (steps <= BAND)
    stepsf = (steps * d).astype(F32)
    biases = []
    for hh in range(4):
        slope = 2.0 ** (-ALIBI_MAX_BIAS * (grp * 4 + hh + 1) / N_HEADS)
        biases.append(jnp.where(valid, -slope * stepsf, NEG))
    return biases, col


def _attn_tiles(grp, L):
    rr, rb = ATT_TILE[grp]
    rb = min(rb, L // BAND)
    return rr, rb, L // (rb * BAND)


def _kv_tile(cur_ref, prev_ref, rr, rb, cs):
    if rb == 0:
        return jnp.concatenate([prev_ref[rr, :, cs], cur_ref[rr, 0:BAND, cs]], axis=0)
    return cur_ref[rr, (rb - 1) * BAND:(rb + 1) * BAND, cs]


def _attn_fwd(qkv, grp):
    d = DILATIONS[grp]
    L = qkv.shape[2]
    RR, RB, nb = _attn_tiles(grp, L)

    def body(q_ref, kc_ref, kp_ref, vc_ref, vp_ref, o_ref, lse_ref):
        i = pl.program_id(0)
        biases, col = _band_bias(grp, d)
        first_keys_ok = (col >= BAND) | (i > 0)
        is_a = lax.broadcasted_iota(jnp.int32, (BAND, PAIR_W), 1) < HEAD_W
        heads = [(rr, rb, cp, h2) for rr in range(RR) for rb in range(RB) for cp in range(2) for h2 in range(2)]

        def tile(head):
            rr, rb, cp, _ = head
            return rr, rb, slice(rb * BAND, (rb + 1) * BAND), slice(cp * PAIR_W, (cp + 1) * PAIR_W)

        def scores(head):
            rr, rb, rows, cs = tile(head)
            q2 = q_ref[rr, rows, cs]
            b = biases[head[2] * 2 + head[3]]
            if rb == 0:
                b = jnp.where(first_keys_ok, b, NEG)
            sel = is_a if head[3] == 0 else jnp.logical_not(is_a)
            return _dot_nt(jnp.where(sel, q2, jnp.zeros_like(q2)), _kv_tile(kc_ref, kp_ref, rr, rb, cs)) + b

        s_next = scores(heads[0])
        outs, lses = {}, {}
        for idx, head in enumerate(heads):
            s = s_next
            if idx + 1 < len(heads):
                s_next = scores(heads[idx + 1])
            rr, rb, rows, cs = tile(head)
            m = jnp.max(s, axis=-1, keepdims=True)
            p = jnp.exp(s - m)
            l = jnp.sum(p, axis=-1, keepdims=True)
            outs[head[3]] = _dot(p.astype(BF16), _kv_tile(vc_ref, vp_ref, rr, rb, cs)) * (1.0 / l)
            lses[head[2] * 2 + head[3]] = m + jnp.log(l)
            if head[3] == 1:
                o_ref[rr, rows, cs] = jnp.where(is_a, outs[0], outs[1]).astype(BF16)
            if head[2] == 1 and head[3] == 1:
                lse_ref[rr, rows, :] = _per_head_lanes(lses)

    cur = lambda w: pl.BlockSpec((None, RR, RB * BAND, GROUP_W), lambda i, j: (w, j, i, 0))
    prev = lambda w: pl.BlockSpec((None, RR, BAND, GROUP_W), lambda i, j: (w, j, jnp.maximum(i * RB - 1, 0), 0))
    return pl.pallas_call(
        body, grid=(nb, d // RR), name=f"attn_fwd_g{grp}",
        in_specs=[cur(0), cur(1), prev(1), cur(2), prev(2)],
        out_specs=[pl.BlockSpec((RR, RB * BAND, GROUP_W), lambda i, j: (j, i, 0)),
                   pl.BlockSpec((RR, RB * BAND, STAT_W), lambda i, j: (j, i, 0))],
        out_shape=[SDS((d, L, GROUP_W), BF16), SDS((d, L, STAT_W), F32)],
        compiler_params=_cp("parallel", "parallel"),
    )(qkv, qkv, qkv, qkv, qkv)


def _pool_column_select(col, vals):
    return jnp.where(col < POOL_GROUP_W, vals[0],
                     jnp.where(col < 2 * POOL_GROUP_W, vals[1],
                               jnp.where(col < 3 * POOL_GROUP_W, vals[2], vals[3])))


def _pool_inv_count(i, rows):
    t = i * rows + lax.broadcasted_iota(jnp.int32, (rows, POOL_W), 0)
    col = lax.broadcasted_iota(jnp.int32, (rows, POOL_W), 1)
    win = _pool_column_select(col, POOL_WINDOWS)
    return 1.0 / jnp.minimum(t + 1, win).astype(F32), col


def _mixer_out(outs, lses, pz, gates, x, w_ao, w_po, wbd, scale, w_out, g_mlp, expand, shards, bufs):
    S = x.shape[0]
    n_tiles = S // TMB
    n = len(shards)

    def body(*refs):
        (o0_ref, l0_ref, o1_ref, l1_ref, o2_ref, l2_ref, pz_ref, halo_ref, gate_ref, x_ref,
         wao_ref, wpo_ref, wbd_ref, sc_ref, wout_ref, g_ref, expand_ref) = refs[0:17]
        shard_refs = refs[17:17 + n]
        (a_ref, lt0_ref, lt1_ref, lt2_ref, pooled_ref, mixed_ref, p_ref, merged_ref, h1_ref,
         m_ref) = refs[17 + 2 * n:27 + 2 * n]
        buf_refs = refs[27 + 2 * n:27 + 3 * n]
        so1, sl1, so2, sl2, slt, ext_ref = refs[27 + 3 * n:33 + 3 * n]
        i = pl.program_id(0)
        begin, end = _hosted_allgather(i, n_tiles, shard_refs, buf_refs, [sh.shape[0] for sh in shards],
                                       refs[33 + 3 * n:])
        begin()
        _interleave_load(o1_ref, (), so1, DILATIONS[1], TMB)
        _interleave_load(o2_ref, (), so2, DILATIONS[2], TMB)
        for ref, sref, d in ((l1_ref, sl1, DILATIONS[1]), (l2_ref, sl2, DILATIONS[2])):
            for r in range(d):
                sref[0, pl.ds(r, TMB // d, stride=d), :] = ref[r]
        l0, l1, l2 = l0_ref[0], sl1[0], sl2[0]
        mx = jnp.maximum(jnp.maximum(l0, l1), l2)
        e0, e1, e2 = jnp.exp(l0 - mx), jnp.exp(l1 - mx), jnp.exp(l2 - mx)
        den = e0 + e1 + e2
        inv = 1.0 / den
        slt[0] = mx + jnp.log(den)
        w0, w1, w2 = [_dot_split(e * inv, expand_ref[...]) for e in (e0, e1, e2)]
        for h in range(2):
            hs = slice(h * PAIR_W, (h + 1) * PAIR_W)
            a_ref[:, hs] = (w0[:, hs] * o0_ref[0, :, hs].astype(F32) + w1[:, hs] * so1[h]
                            + w2[:, hs] * so2[h]).astype(BF16)
        lt0_ref[0] = slt[0]
        for ref, d in ((lt1_ref, DILATIONS[1]), (lt2_ref, DILATIONS[2])):
            for r in range(d):
                ref[r] = slt[0, pl.ds(r, TMB // d, stride=d), :]

        pz_t = pz_ref[...].astype(F32)
        ext_ref[0:POOL_HALO, :] = jnp.where(i > 0, halo_ref[...].astype(F32), 0.0)
        ext_ref[POOL_HALO:, :] = pz_t
        sums = []
        acc = ext_ref[...]
        for k in (1, 2, 4, 8):
            acc = acc + pltpu.roll(acc, k, 0)
            sums.append(acc[POOL_HALO:, :])
        inv_cnt, col = _pool_inv_count(i, TMB)
        pooled = (_pool_column_select(col, sums) * inv_cnt - pz_t).astype(BF16)
        pooled_ref[...] = pooled
        mixed = _dot(pooled, wbd_ref[...])
        mixed_ref[...] = mixed.astype(BF16)
        p = (mixed * sc_ref[...]).astype(BF16)
        p_ref[...] = p

        a = a_ref[...]
        for j in range(N_CHIPS):
            js = slice(j * CHUNK, (j + 1) * CHUNK)
            ga = gate_ref[:, js]
            gp = gate_ref[:, D_MODEL + j * CHUNK:D_MODEL + (j + 1) * CHUNK]
            mj = _sigmoid(ga) * _dot(a, wao_ref[j]) + _sigmoid(gp) * _dot(p, wpo_ref[j])
            merged_ref[:, js] = mj.astype(BF16)
        h1 = x_ref[...] + _dot(merged_ref[...], wout_ref[...])
        h1_ref[...] = h1
        m_ref[...] = _rms_fwd(h1, g_ref[...])[0].astype(BF16)
        end()

    row = lambda w: pl.BlockSpec((TMB, w), lambda i: (i, 0))
    grp_spec = lambda d: pl.BlockSpec((d, TMB // d, GROUP_W), lambda i: (0, i, 0))
    stat_spec = lambda d: pl.BlockSpec((d, TMB // d, STAT_W), lambda i: (0, i, 0))
    halo = pl.BlockSpec((POOL_HALO, POOL_W), lambda i: (jnp.maximum(i * (TMB // POOL_HALO) - 1, 0), 0))
    d0, d1, d2 = DILATIONS
    pair_scratch = pltpu.VMEM((2, TMB, PAIR_W), F32)
    stat_scratch = pltpu.VMEM((1, TMB, STAT_W), F32)
    res = pl.pallas_call(
        body, grid=(n_tiles,), name="mixer_out",
        in_specs=[grp_spec(d0), stat_spec(d0), grp_spec(d1), stat_spec(d1), grp_spec(d2), stat_spec(d2),
                  row(POOL_W), halo, row(2 * D_MODEL), row(D_MODEL),
                  _resident(w_ao.shape), _resident(w_po.shape), _resident(wbd.shape), _resident(scale.shape),
                  _resident(w_out.shape), _resident(g_mlp.shape), _resident(expand.shape)] + [ANY] * (2 * n),
        out_specs=[row(GROUP_W), stat_spec(d0), stat_spec(d1), stat_spec(d2),
                   row(POOL_W), row(POOL_W), row(POOL_W), row(D_MODEL), row(D_MODEL), row(D_MODEL)] + [ANY] * n,
        out_shape=[SDS((S, GROUP_W), BF16)] + [SDS((d, S // d, STAT_W), F32) for d in DILATIONS]
        + [SDS((S, POOL_W), BF16), SDS((S, POOL_W), BF16), SDS((S, POOL_W), BF16),
           SDS((S, D_MODEL), BF16), SDS((S, D_MODEL), F32), SDS((S, D_MODEL), BF16)]
        + [SDS(b.shape, b.dtype) for b in bufs],
        scratch_shapes=[pair_scratch, stat_scratch, pair_scratch, stat_scratch, stat_scratch,
                        pltpu.VMEM((TMB + POOL_HALO, POOL_W), F32)] + [pltpu.SemaphoreType.DMA((3 * n,))] * 4,
        input_output_aliases={17 + n + w: 10 + w for w in range(n)},
        compiler_params=_cp("arbitrary"),
    )(outs[0], lses[0], outs[1], lses[1], outs[2], lses[2], pz, pz, gates, x,
      w_ao, w_po, wbd, scale, w_out, g_mlp, expand, *shards, *bufs)
    return res[:10], res[10:]


def _mlp_fwd_loss(m, h1, target, w_mi, w_mo, g_f):
    S = m.shape[0]

    def body(m_ref, h1_ref, t_ref, wmi_ref, wmo_ref, g_ref, hid_ref, dh2_ref, dh2b_ref, loss_ref, dg_ref):
        @pl.when(pl.program_id(0) == 0)
        def _():
            loss_ref[...] = jnp.zeros_like(loss_ref)
            dg_ref[...] = jnp.zeros_like(dg_ref)

        mt = m_ref[...]
        acc = h1_ref[...]
        for c in range(N_CHIPS):
            hid = jnp.square(jnp.maximum(_dot(mt, wmi_ref[c]), 0.0)).astype(BF16)
            hid_ref[:, c * D_MODEL:(c + 1) * D_MODEL] = hid
            acc = acc + _dot(hid, wmo_ref[c])
        g = g_ref[...]
        y, hh, r = _rms_fwd(acc, g)
        e = y - t_ref[...]
        loss_ref[...] += jnp.sum(e * e, axis=0, keepdims=True)
        dy = e * (1.0 / D_MODEL)
        dg_ref[...] += jnp.sum(dy * hh, axis=0, keepdims=True)
        dh2 = _rms_bwd(dy, hh, r, g)
        dh2_ref[...] = dh2
        dh2b_ref[...] = dh2.astype(BF16)

    row = lambda w: pl.BlockSpec((TM, w), lambda i: (i, 0))
    vec = pl.BlockSpec((1, D_MODEL), lambda i: (0, 0))
    return pl.pallas_call(
        body, grid=(S // TM,), name="mlp_fwd_loss",
        in_specs=[row(D_MODEL), row(D_MODEL), row(D_MODEL), _resident(w_mi.shape), _resident(w_mo.shape),
                  _resident(g_f.shape)],
        out_specs=[row(D_FF), row(D_MODEL), row(D_MODEL), vec, vec],
        out_shape=[SDS((S, D_FF), BF16), SDS((S, D_MODEL), F32), SDS((S, D_MODEL), BF16),
                   SDS((1, D_MODEL), F32), SDS((1, D_MODEL), F32)],
        compiler_params=_cp("arbitrary"),
    )(m, h1, target, w_mi, w_mo, g_f)


def _mlp_bwd(dh2, dh2b, hid, h1, w_mi, w_mo, g_mlp):
    S = dh2.shape[0]

    def body(dh2_ref, dh2b_ref, hid_ref, h1_ref, wmi_ref, wmo_ref, g_ref, dpre_ref, dh1_ref, dh1b_ref, dg_ref):
        @pl.when(pl.program_id(0) == 0)
        def _():
            dg_ref[...] = jnp.zeros_like(dg_ref)

        d2 = dh2b_ref[...]
        dm = jnp.zeros((TM, D_MODEL), F32)
        dhid_next = _dot_nt(d2, wmo_ref[0])
        for c in range(N_CHIPS):
            cs = slice(c * D_MODEL, (c + 1) * D_MODEL)
            dhid = dhid_next
            if c + 1 < N_CHIPS:
                dhid_next = _dot_nt(d2, wmo_ref[c + 1])
            dpre = (dhid * (2.0 * jnp.sqrt(hid_ref[:, cs].astype(F32)))).astype(BF16)
            dpre_ref[:, cs] = dpre
            dm = dm + _dot_nt(dpre, wmi_ref[c])
        g = g_ref[...]
        _, hh, r = _rms_fwd(h1_ref[...], g)
        dg_ref[...] += jnp.sum(dm * hh, axis=0, keepdims=True)
        dh1 = dh2_ref[...] + _rms_bwd(dm, hh, r, g)
        dh1_ref[...] = dh1
        dh1b_ref[...] = dh1.astype(BF16)

    row = lambda w: pl.BlockSpec((TM, w), lambda i: (i, 0))
    return pl.pallas_call(
        body, grid=(S // TM,), name="mlp_bwd",
        in_specs=[row(D_MODEL), row(D_MODEL), row(D_FF), row(D_MODEL), _resident(w_mi.shape),
                  _resident(w_mo.shape), _resident(g_mlp.shape)],
        out_specs=[row(D_FF), row(D_MODEL), row(D_MODEL), pl.BlockSpec((1, D_MODEL), lambda i: (0, 0))],
        out_shape=[SDS((S, D_FF), BF16), SDS((S, D_MODEL), F32), SDS((S, D_MODEL), BF16), SDS((1, D_MODEL), F32)],
        compiler_params=_cp("arbitrary"),
    )(dh2, dh2b, hid, h1, w_mi, w_mo, g_mlp)


def _mixer_bwd(dh1b, a, p, mixed, gates, w_out, w_ao, w_po, wbd, scale, stat_reduce, sums):
    S = a.shape[0]
    n_tiles = S // TMB
    n = len(sums)

    def body(*refs):
        (dh1b_ref, a_ref, p_ref, mixed_ref, gate_ref, wout_ref, wao_ref, wpo_ref, wbd_ref, sc_ref,
         ones_ref) = refs[0:11]
        sum_refs = refs[11:11 + n]
        (da1_ref, dp1_ref, dgate_ref, da0_ref, dag1_ref, dag2_ref, dd0_ref, dd1_ref, dd2_ref,
         dmixed_ref, dqp_ref, dscale_ref) = refs[11 + n:23 + n]
        land_refs = refs[23 + n:23 + 2 * n]
        s_da, s_dd, send_sem, recv_sem = refs[23 + 2 * n:]
        i = pl.program_id(0)

        @pl.when(i == 0)
        def _():
            dscale_ref[...] = jnp.zeros_like(dscale_ref)
            for cpy in _chip_sum_copies(sum_refs, land_refs, send_sem, recv_sem):
                cpy.start()

        dmerged = _dot_nt(dh1b_ref[...], wout_ref[...])
        a = a_ref[...]
        p = p_ref[...]
        da = jnp.zeros((TMB, GROUP_W), F32)
        dp = jnp.zeros((TMB, POOL_W), F32)
        for j in range(N_CHIPS):
            js = slice(j * CHUNK, (j + 1) * CHUNK)
            sa = _sigmoid(gate_ref[:, js])
            sp = _sigmoid(gate_ref[:, D_MODEL + j * CHUNK:D_MODEL + (j + 1) * CHUNK])
            dmj = dmerged[:, js]
            da1 = (dmj * sa).astype(BF16)
            dp1 = (dmj * sp).astype(BF16)
            da1_ref[:, js] = da1
            dp1_ref[:, js] = dp1
            dgate_ref[j] = (dmj * _dot(a, wao_ref[j]) * sa * (1.0 - sa)).astype(BF16)
            dgate_ref[N_CHIPS + j] = (dmj * _dot(p, wpo_ref[j]) * sp * (1.0 - sp)).astype(BF16)
            da = da + _dot_nt(da1, wao_ref[j])
            dp = dp + _dot_nt(dp1, wpo_ref[j])

        dd = _dot_split(da * a.astype(F32), ones_ref[...])
        da0_ref[0] = da.astype(BF16)
        dd0_ref[0] = dd
        for h in range(2):
            s_da[h] = da[:, h * PAIR_W:(h + 1) * PAIR_W]
        s_dd[0] = dd
        for refs, d in (((dag1_ref, dd1_ref), DILATIONS[1]), ((dag2_ref, dd2_ref), DILATIONS[2])):
            for r in range(d):
                for h in range(2):
                    hs = slice(h * PAIR_W, (h + 1) * PAIR_W)
                    refs[0][r, :, hs] = s_da[h, pl.ds(r, TMB // d, stride=d), :].astype(BF16)
                refs[1][r] = s_dd[0, pl.ds(r, TMB // d, stride=d), :]

        sc = sc_ref[...]
        dscale_ref[...] += jnp.sum(dp * mixed_ref[...].astype(F32), axis=0, keepdims=True)
        dmixed = (dp * sc).astype(BF16)
        dmixed_ref[...] = dmixed
        inv_cnt, _ = _pool_inv_count(i, TMB)
        dqp_ref[...] = (_dot_nt(dmixed, wbd_ref[...]) * inv_cnt).astype(BF16)

        @pl.when(i == n_tiles - 1)
        def _():
            for cpy in _chip_sum_copies(sum_refs, land_refs, send_sem, recv_sem):
                cpy.wait()

    row = lambda w: pl.BlockSpec((TMB, w), lambda i: (i, 0))
    grp_spec = lambda d: pl.BlockSpec((d, TMB // d, GROUP_W), lambda i: (0, i, 0))
    stat_spec = lambda d: pl.BlockSpec((d, TMB // d, STAT_W), lambda i: (0, i, 0))
    d0, d1, d2 = DILATIONS
    res = pl.pallas_call(
        body, grid=(n_tiles,), name="mixer_bwd",
        in_specs=[row(D_MODEL), row(GROUP_W), row(POOL_W), row(POOL_W), row(2 * D_MODEL),
                  _resident(w_out.shape), _resident(w_ao.shape), _resident(w_po.shape), _resident(wbd.shape),
                  _resident(scale.shape), _resident(stat_reduce.shape)] + [ANY] * n,
        out_specs=[row(D_MODEL), row(D_MODEL), pl.BlockSpec((2 * N_CHIPS, TMB, CHUNK), lambda i: (0, i, 0)),
                   grp_spec(d0), grp_spec(d1), grp_spec(d2), stat_spec(d0), stat_spec(d1), stat_spec(d2),
                   row(POOL_W), row(POOL_W), pl.BlockSpec((1, POOL_W), lambda i: (0, 0))] + [ANY] * n,
        out_shape=[SDS((S, D_MODEL), BF16), SDS((S, D_MODEL), BF16), SDS((2 * N_CHIPS, S, CHUNK), BF16)]
        + [SDS((d, S // d, GROUP_W), BF16) for d in DILATIONS]
        + [SDS((d, S // d, STAT_W), F32) for d in DILATIONS]
        + [SDS((S, POOL_W), BF16), SDS((S, POOL_W), BF16), SDS((1, POOL_W), F32)]
        + [SDS(t.shape, t.dtype) for t in sums],
        scratch_shapes=[pltpu.VMEM((2, TMB, PAIR_W), F32), pltpu.VMEM((1, TMB, STAT_W), F32),
                        pltpu.SemaphoreType.DMA((3 * n,)), pltpu.SemaphoreType.DMA((3 * n,))],
        compiler_params=_cp("arbitrary"),
    )(dh1b, a, p, mixed, gates, w_out, w_ao, w_po, wbd, scale, stat_reduce, *sums)
    return res[:12], res[12:]


def _attn_bwd(qkv, da, lt, dd, grp, packs=None):
    d = DILATIONS[grp]
    L = qkv.shape[2]
    RR, RB, nb = _attn_tiles(grp, L)
    n_j = d // RR
    hosted = packs is not None

    def body(*refs):
        q_ref, kc_ref, kp_ref, vc_ref, vp_ref, da_ref, lt_ref, dd_ref = refs[0:8]
        dq_ref, dk_ref, dv_ref = refs[8 + hosted:11 + hosted]
        dk_acc, dv_acc = refs[11 + 2 * hosted:13 + 2 * hosted]
        i = pl.program_id(1)
        if hosted:
            j = pl.program_id(0)
            start, relay, finish = _pack_allgather(refs[11 + hosted], *refs[13 + 2 * hosted:])
            pl.when((j == 0) & (i == 0))(start)
            pl.when((j == 0) & (i == nb // 2))(relay)

        @pl.when(i == 0)
        def _():
            dk_acc[...] = jnp.zeros_like(dk_acc)
            dv_acc[...] = jnp.zeros_like(dv_acc)

        def compute(cur, prv):
            dk_acc[cur] = jnp.zeros((RR, RB * BAND, GROUP_W), F32)
            dv_acc[cur] = jnp.zeros((RR, RB * BAND, GROUP_W), F32)
            biases, col = _band_bias(grp, d)
            first_keys_ok = (col >= BAND) | (i > 0)
            is_a = lax.broadcasted_iota(jnp.int32, (BAND, PAIR_W), 1) < HEAD_W
            for rr in range(RR):
                for rb in range(RB):
                    rows = slice(rb * BAND, (rb + 1) * BAND)
                    for cp in range(2):
                        cs = slice(cp * PAIR_W, (cp + 1) * PAIR_W)
                        q2 = q_ref[rr, rows, cs]
                        da2 = da_ref[rr, rows, cs]
                        lt2 = lt_ref[rr, rows, :]
                        dd2 = dd_ref[rr, rows, :]
                        kcat = _kv_tile(kc_ref, kp_ref, rr, rb, cs)
                        vcat = _kv_tile(vc_ref, vp_ref, rr, rb, cs)
                        q2t = q2.T
                        da2t = da2.T
                        dqs, dkts, dvts, scores, dpvs = [], [], [], [], []
                        for h2 in range(2):
                            sel = is_a if h2 == 0 else jnp.logical_not(is_a)
                            b = biases[cp * 2 + h2]
                            if rb == 0:
                                b = jnp.where(first_keys_ok, b, NEG)
                            scores.append(_dot_nt(jnp.where(sel, q2, jnp.zeros_like(q2)), kcat) + b)
                            dpvs.append(_dot_nt(jnp.where(sel, da2, jnp.zeros_like(da2)), vcat))
                        for h2 in range(2):
                            lane0 = h2 * HEAD_W
                            p = jnp.exp(scores[h2] - _head_col(lt2, cp * 2 + h2))
                            ds = (p * (dpvs[h2] - _head_col(dd2, cp * 2 + h2))).astype(BF16)
                            dqs.append(_dot(ds, kcat))
                            dkts.append(_dot(q2t[lane0:lane0 + HEAD_W, :], ds))
                            dvts.append(_dot(da2t[lane0:lane0 + HEAD_W, :], p.astype(BF16)))
                        dq_ref[rr, rows, cs] = (jnp.where(is_a, dqs[0], dqs[1]) * 0.125).astype(BF16)
                        dkc = jnp.concatenate(dkts, axis=0).T
                        dvc = jnp.concatenate(dvts, axis=0).T
                        if rb == 0:
                            last = slice((RB - 1) * BAND, RB * BAND)
                            dk_acc[prv, rr, last, cs] += dkc[0:BAND]
                            dv_acc[prv, rr, last, cs] += dvc[0:BAND]
                            dk_acc[cur, rr, 0:BAND, cs] += dkc[BAND:]
                            dv_acc[cur, rr, 0:BAND, cs] += dvc[BAND:]
                        else:
                            both = slice((rb - 1) * BAND, (rb + 1) * BAND)
                            dk_acc[cur, rr, both, cs] += dkc
                            dv_acc[cur, rr, both, cs] += dvc

        def flush(prv):
            dk_ref[...] = dk_acc[prv].astype(BF16)
            dv_ref[...] = dv_acc[prv].astype(BF16)

        for parity in (0, 1):
            on = (i % 2) == parity
            pl.when(on & (i < nb))(functools.partial(compute, parity, 1 - parity))
            pl.when(on & (i > 0))(functools.partial(flush, 1 - parity))
        if hosted:
            pl.when((j == n_j - 1) & (i == nb))(finish)

    qi = lambda i: jnp.minimum(i, nb - 1)
    cur_w = lambda w: pl.BlockSpec((None, RR, RB * BAND, GROUP_W), lambda j, i: (w, j, qi(i), 0))
    prev_w = lambda w: pl.BlockSpec((None, RR, BAND, GROUP_W),
                                    lambda j, i: (w, j, jnp.maximum(qi(i) * RB - 1, 0), 0))
    blk = pl.BlockSpec((RR, RB * BAND, GROUP_W), lambda j, i: (j, qi(i), 0))
    stat_blk = pl.BlockSpec((RR, RB * BAND, STAT_W), lambda j, i: (j, qi(i), 0))
    late = pl.BlockSpec((RR, RB * BAND, GROUP_W), lambda j, i: (j, jnp.maximum(i - 1, 0), 0))
    extra = [packs] if hosted else []
    return pl.pallas_call(
        body, grid=(n_j, nb + 1), name=f"attn_bwd_g{grp}",
        in_specs=[cur_w(0), cur_w(1), prev_w(1), cur_w(2), prev_w(2), blk, stat_blk, stat_blk] + [ANY] * hosted,
        out_specs=[blk, late, late] + [ANY] * hosted,
        out_shape=[SDS((d, L, GROUP_W), BF16)] * 3 + [SDS(t.shape, t.dtype) for t in extra],
        scratch_shapes=[pltpu.VMEM((2, RR, RB * BAND, GROUP_W), F32), pltpu.VMEM((2, RR, RB * BAND, GROUP_W), F32)]
        + [pltpu.SemaphoreType.DMA((N_DEV - 1,))] * (2 * hosted),
        input_output_aliases={8: 3} if hosted else {},
        compiler_params=_cp("arbitrary" if hosted else "parallel", "arbitrary"),
    )(qkv, qkv, qkv, qkv, qkv, da, lt, dd, *extra)


def _dz_assemble(dqkv, dqp):
    S = dqp.shape[0]
    tm = min(TMZ, S)
    n_tiles = S // tm

    def body(*refs):
        dqkv_refs = refs[0:9]
        dqp_ref, halo_ref = refs[9:11]
        dz_ref, s_ref, ext_ref = refs[11:]
        i = pl.program_id(0)

        for grp in range(3):
            for which in range(3):
                n = which * 3 + grp
                ref = dqkv_refs[grp * 3 + which]
                if DILATIONS[grp] == 1:
                    dz_ref[n] = ref[0]
                else:
                    _interleave_load(ref, (), s_ref, DILATIONS[grp], tm)
                    for h in range(2):
                        dz_ref[n, :, h * PAIR_W:(h + 1) * PAIR_W] = s_ref[h].astype(BF16)

        dqp = dqp_ref[...].astype(F32)
        ext_ref[0:tm, :] = dqp
        ext_ref[tm:, :] = jnp.where(i < n_tiles - 1, halo_ref[...].astype(F32), 0.0)
        sums = []
        acc = ext_ref[...]
        for k in (1, 2, 4, 8):
            acc = acc + pltpu.roll(acc, tm + POOL_HALO - k, 0)
            sums.append(acc[0:tm, :])
        inv_cnt, col = _pool_inv_count(i, tm)
        dpz = _pool_column_select(col, sums) - dqp / inv_cnt
        for t in range(3):
            dz_ref[9 + t] = dpz[:, t * CHUNK:(t + 1) * CHUNK].astype(BF16)

    row = lambda w: pl.BlockSpec((tm, w), lambda i: (i, 0))
    grp_spec = lambda d: pl.BlockSpec((d, tm // d, GROUP_W), lambda i: (0, i, 0))
    halo = pl.BlockSpec((POOL_HALO, POOL_W),
                        lambda i: (jnp.minimum((i + 1) * (tm // POOL_HALO), S // POOL_HALO - 1), 0))
    flat = [t for grp in range(3) for t in dqkv[grp]]
    return pl.pallas_call(
        body, grid=(n_tiles,), name="dz_assemble",
        in_specs=[grp_spec(DILATIONS[grp]) for grp in range(3) for _ in range(3)] + [row(POOL_W), halo],
        out_specs=pl.BlockSpec((N_DZ_CHUNKS, tm, CHUNK), lambda i: (0, i, 0)),
        out_shape=SDS((N_DZ_CHUNKS, S, CHUNK), BF16),
        scratch_shapes=[pltpu.VMEM((2, tm, PAIR_W), F32), pltpu.VMEM((tm + POOL_HALO, POOL_W), F32)],
        compiler_params=_cp("parallel"),
    )(*flat, dqp, dqp)


def _inproj_dx(dz, dgates, dh1, x, g, w_in, sums):
    S = x.shape[0]
    n_tiles = S // TM
    n = len(sums)

    def body(*refs):
        dz_ref, dgate_ref, dh1_ref, x_ref, g_ref, w_ref = refs[0:6]
        sum_refs = refs[6:6 + n]
        dx_ref, dg_ref = refs[6 + n:8 + n]
        land_refs = refs[8 + n:8 + 2 * n]
        sems = refs[8 + 2 * n:]
        i = pl.program_id(0)

        def copies():
            return _chip_sum_copies(sum_refs, land_refs, *sems)

        @pl.when(i == 0)
        def _():
            dg_ref[...] = jnp.zeros_like(dg_ref)
            for cpy in copies():
                cpy.start()

        du = jnp.zeros((TM, D_MODEL), F32)
        for k in range(N_CHUNKS):
            dzk = dz_ref[k] if k < N_DZ_CHUNKS else dgate_ref[k - N_DZ_CHUNKS]
            du = du + _dot_nt(dzk, _w_in_chunk(w_ref, k))
        gv = g_ref[...]
        _, xh, r = _rms_fwd(x_ref[...], gv)
        dg_ref[...] += jnp.sum(du * xh, axis=0, keepdims=True)
        dx_ref[...] = dh1_ref[...] + _rms_bwd(du, xh, r, gv)

        @pl.when(i == n_tiles - 1)
        def _():
            for cpy in copies():
                cpy.wait()

    row = lambda w: pl.BlockSpec((TM, w), lambda i: (i, 0))
    res = pl.pallas_call(
        body, grid=(n_tiles,), name="inproj_dx",
        in_specs=[pl.BlockSpec((N_DZ_CHUNKS, TM, CHUNK), lambda i: (0, i, 0)),
                  pl.BlockSpec((N_CHUNKS - N_DZ_CHUNKS, TM, CHUNK), lambda i: (0, i, 0)),
                  row(D_MODEL), row(D_MODEL), _resident(g.shape), _resident(w_in.shape)] + [ANY] * n,
        out_specs=[row(D_MODEL), pl.BlockSpec((1, D_MODEL), lambda i: (0, 0))] + [ANY] * n,
        out_shape=[SDS((S, D_MODEL), F32), SDS((1, D_MODEL), F32)] + [SDS(t.shape, t.dtype) for t in sums],
        scratch_shapes=[pltpu.SemaphoreType.DMA((3 * n,)), pltpu.SemaphoreType.DMA((3 * n,))],
        compiler_params=_cp("arbitrary"),
    )(dz, dgates, dh1, x, g, w_in, *sums)
    return res[0], res[1], res[2:]


def _wgrad(a, b, name, *, out_shape, a_spec, b_spec, out_spec, grid, n_out_cols=None, fill=None, narrow=True):
    k_axis = len(grid) - 1
    n_k = grid[k_axis]
    n_out = 2 if narrow else 1

    def body(a_ref, b_ref, *rest):
        o_ref = rest[-n_out]

        @pl.when(pl.program_id(k_axis) == 0)
        def _():
            o_ref[...] = jnp.zeros_like(o_ref)

        at = a_ref[...]
        if n_out_cols is None:
            o_ref[...] += _dot_tn(at, b_ref[...])
        elif n_out_cols[0] == "lead_both":
            for t in range(b_ref.shape[0]):
                o_ref[t] += _dot_tn(at, b_ref[t])
        else:
            w = n_out_cols[1]
            for t in range(o_ref.shape[0]):
                o_ref[t] += _dot_tn(at, b_ref[:, t * w:(t + 1) * w])

        if narrow:
            @pl.when(pl.program_id(k_axis) == n_k - 1)
            def _():
                rest[-1][...] = o_ref[...].astype(BF16)

    sem = ("parallel",) * k_axis + ("arbitrary",)
    extra = [] if fill is None else list(fill) if narrow else [fill]
    shapes = [out_shape, SDS(out_shape.shape, BF16)] if narrow else out_shape
    return pl.pallas_call(body, grid=grid, name=name, in_specs=[a_spec, b_spec] + [ANY] * len(extra),
                          out_specs=[out_spec] * n_out if narrow else out_spec, out_shape=shapes,
                          input_output_aliases={2 + t: t for t in range(len(extra))},
                          compiler_params=_cp(*sem))(a, b, *extra)


def _wgrad_in(u, dz, dgates):
    bk = min(BK, u.shape[0])
    nk = u.shape[0] // bk
    g = WGRAD_IN_GROUP
    kw = dict(n_out_cols=("lead_both", CHUNK), a_spec=pl.BlockSpec((bk, D_MODEL), lambda j, k: (k, 0)),
              b_spec=pl.BlockSpec((g, bk, CHUNK), lambda j, k: (j, k, 0)),
              out_shape=SDS((N_CHUNKS, D_MODEL, CHUNK), F32))
    first = _wgrad(u, dz, "wgrad_in_qkvp", grid=(N_DZ_CHUNKS // g, nk),
                   out_spec=pl.BlockSpec((g, D_MODEL, CHUNK), lambda j, k: (j, 0, 0)), **kw)
    both = _wgrad(u, dgates, "wgrad_in_gates", grid=((N_CHUNKS - N_DZ_CHUNKS) // g, nk), fill=first,
                  out_spec=pl.BlockSpec((g, D_MODEL, CHUNK), lambda j, k: (N_DZ_CHUNKS // g + j, 0, 0)), **kw)
    return [t.reshape(N_CHIPS, CHUNKS_PER_SHARD * D_MODEL, CHUNK) for t in both]


def _wgrads_mixer(a, da1, p, dp1, merged, dh1b, pooled, dmixed):
    bk = min(BK, a.shape[0])
    nk = a.shape[0] // bk
    g_ao = _wgrad(
        a, da1, "wgrad_att_out", grid=(nk,), n_out_cols=("cols_b", CHUNK),
        a_spec=pl.BlockSpec((bk, GROUP_W), lambda k: (k, 0)),
        b_spec=pl.BlockSpec((bk, D_MODEL), lambda k: (k, 0)),
        out_spec=pl.BlockSpec((N_CHIPS, GROUP_W, CHUNK), lambda k: (0, 0, 0)),
        out_shape=SDS((N_CHIPS, GROUP_W, CHUNK), F32))
    g_po = _wgrad(
        p, dp1, "wgrad_pool_out", grid=(nk,), n_out_cols=("cols_b", CHUNK),
        a_spec=pl.BlockSpec((bk, POOL_W), lambda k: (k, 0)),
        b_spec=pl.BlockSpec((bk, D_MODEL), lambda k: (k, 0)),
        out_spec=pl.BlockSpec((N_CHIPS, POOL_W, CHUNK), lambda k: (0, 0, 0)),
        out_shape=SDS((N_CHIPS, POOL_W, CHUNK), F32))
    g_out = _wgrad(
        merged, dh1b, "wgrad_out", grid=(nk,),
        a_spec=pl.BlockSpec((bk, D_MODEL), lambda k: (k, 0)),
        b_spec=pl.BlockSpec((bk, D_MODEL), lambda k: (k, 0)),
        out_spec=pl.BlockSpec((D_MODEL, D_MODEL), lambda k: (0, 0)),
        out_shape=SDS((D_MODEL, D_MODEL), F32))
    g_bd = _wgrad(
        pooled, dmixed, "wgrad_pool_grp", grid=(nk,),
        a_spec=pl.BlockSpec((bk, POOL_W), lambda k: (k, 0)),
        b_spec=pl.BlockSpec((bk, POOL_W), lambda k: (k, 0)),
        out_spec=pl.BlockSpec((POOL_W, POOL_W), lambda k: (0, 0)),
        out_shape=SDS((POOL_W, POOL_W), F32), narrow=False)
    g_out = [t.reshape(N_CHIPS, D_MODEL // N_CHIPS, D_MODEL) for t in g_out]
    return [g_ao, g_po, g_out], g_bd


def _wgrads_mlp(m, dpre, hid, dh2b):
    bk = min(BK, m.shape[0])
    nk = m.shape[0] // bk
    g_mi = _wgrad(
        m, dpre, "wgrad_mlp_in", grid=(N_CHIPS, nk),
        a_spec=pl.BlockSpec((bk, D_MODEL), lambda c, k: (k, 0)),
        b_spec=pl.BlockSpec((bk, D_MODEL), lambda c, k: (k, c)),
        out_spec=pl.BlockSpec((None, D_MODEL, D_MODEL), lambda c, k: (c, 0, 0)),
        out_shape=SDS((N_CHIPS, D_MODEL, D_MODEL), F32))
    g_mo = _wgrad(
        hid, dh2b, "wgrad_mlp_out", grid=(N_CHIPS, nk),
        a_spec=pl.BlockSpec((bk, D_MODEL), lambda c, k: (k, c)),
        b_spec=pl.BlockSpec((bk, D_MODEL), lambda c, k: (k, 0)),
        out_spec=pl.BlockSpec((None, D_MODEL, D_MODEL), lambda c, k: (c, 0, 0)),
        out_shape=SDS((N_CHIPS, D_MODEL, D_MODEL), F32))
    return [g_mi, g_mo]


def _mesh_place():
    x, y, c = lax.axis_index("x"), lax.axis_index("y"), lax.axis_index("c")
    other_chips = [(x, 1 - y), (1 - x, y), (1 - x, 1 - y)]
    return x, y, c, other_chips


ANY = pl.BlockSpec(memory_space=pl.ANY)


def _weight_half_copies(shard_refs, buf_refs, rows, send_sem, recv_sem):
    x, y, c, chips = _mesh_place()
    me = 2 * x + y
    copies = []
    for w, r_full in enumerate(rows):
        rh = r_full // 2
        for r, (px, py) in enumerate(chips):
            k = w * 3 + r
            copies.append(pltpu.make_async_remote_copy(
                src_ref=shard_refs[w].at[pl.ds(c * rh, rh), :], dst_ref=buf_refs[w].at[me, pl.ds(c * rh, rh), :],
                send_sem=send_sem.at[k], recv_sem=recv_sem.at[k], device_id=(px, py, c), device_id_type=MESH))
    return copies


def _pair_forward_copies(buf_refs, rows, send_sem, recv_sem):
    x, y, c, chips = _mesh_place()
    out = []
    for w, r_full in enumerate(rows):
        rh = r_full // 2
        for r, (px, py) in enumerate(chips):
            k = w * 3 + r
            landed = buf_refs[w].at[2 * px + py, pl.ds(c * rh, rh), :]
            theirs = buf_refs[w].at[2 * px + py, pl.ds((1 - c) * rh, rh), :]
            mk = lambda ref: pltpu.make_async_remote_copy(
                src_ref=ref, dst_ref=ref, send_sem=send_sem.at[k], recv_sem=recv_sem.at[k],
                device_id=(x, y, 1 - c), device_id_type=MESH)
            out.append((mk(landed), mk(theirs)))
    return out


def _place_own(block, n_slots, slot):
    buf = lax.empty((n_slots,) + block.shape, block.dtype)
    return lax.dynamic_update_slice(buf, block[None], (slot,) + (0,) * block.ndim)


def _pair_forward(bufs, rows, name):
    n = len(bufs)

    def body(*refs):
        dst = refs[n:2 * n]
        send_sem, recv_sem = refs[2 * n:]
        fwds = _pair_forward_copies(dst, rows, send_sem, recv_sem)
        for fwd, _ in fwds:
            fwd.start()
        for fwd, landing in fwds:
            landing.wait_recv()
            fwd.wait_send()

    return pl.pallas_call(
        body, name=name,
        in_specs=[ANY] * n, out_specs=[ANY] * n,
        out_shape=[SDS(b.shape, b.dtype) for b in bufs],
        scratch_shapes=[pltpu.SemaphoreType.DMA((3 * n,))] * 2,
        input_output_aliases={w: w for w in range(n)},
    )(*bufs)


def _chip_sum_copies(src, dst, send_sem, recv_sem):
    x, y, c, chips = _mesh_place()
    copies = []
    for w in range(len(src)):
        for r, (px, py) in enumerate(chips):
            k = w * 3 + r
            copies.append(pltpu.make_async_remote_copy(
                src_ref=src[w].at[r + 1], dst_ref=dst[w].at[r + 1], send_sem=send_sem.at[k], recv_sem=recv_sem.at[k],
                device_id=(px, py, c), device_id_type=MESH))
    return copies


def _pair_exchange(grads):
    n = len(grads)

    def body(*refs):
        src, dst = refs[:n], refs[n:2 * n]
        send_sem, recv_sem = refs[2 * n:]
        x, y, c, _ = _mesh_place()
        copies = []
        for w in range(n):
            rh = grads[w].shape[1] // 2
            copies.append(pltpu.make_async_remote_copy(
                src_ref=src[w].at[:, pl.ds((1 - c) * rh, rh), :], dst_ref=dst[w],
                send_sem=send_sem.at[w], recv_sem=recv_sem.at[w],
                device_id=(x, y, 1 - c), device_id_type=MESH))
            copies[-1].start()
        for cpy in copies:
            cpy.wait()

    return pl.pallas_call(
        body, name="grad_pair_exchange",
        in_specs=[ANY] * n, out_specs=[ANY] * n,
        out_shape=[SDS((N_CHIPS, g.shape[1] // 2, g.shape[2]), g.dtype) for g in grads],
        scratch_shapes=[pltpu.SemaphoreType.DMA((n,)), pltpu.SemaphoreType.DMA((n,))],
    )(*grads)


def _pair_sum(place, grad, recv, name):
    _, R, C = grad.shape
    rh = R // 2
    br = _row_block(rh, max(256, ELEMENTWISE_BLOCK // C))
    nbh = rh // br

    def body(place_ref, g_ref, r_ref, own_ref, sums_ref):
        s = g_ref[...] + r_ref[...].astype(F32)

        @pl.when(pl.program_id(1) == 0)
        def _():
            own_ref[...] = s

        sums_ref[...] = s.astype(BF16)

    slot = lambda rel, pr: jnp.bitwise_xor(pr[0], rel)
    return pl.pallas_call(
        body, name=name,
        grid_spec=pltpu.PrefetchScalarGridSpec(
            num_scalar_prefetch=1, grid=(nbh, N_CHIPS),
            in_specs=[pl.BlockSpec((None, br, C), lambda i, rel, pr: (slot(rel, pr), pr[1] * nbh + i, 0)),
                      pl.BlockSpec((None, br, C), lambda i, rel, pr: (slot(rel, pr), i, 0))],
            out_specs=[pl.BlockSpec((br, C), lambda i, rel, pr: (i, 0)),
                       pl.BlockSpec((None, br, C), lambda i, rel, pr: (rel, i, 0))]),
        out_shape=[SDS((rh, C), F32), SDS((N_CHIPS, rh, C), BF16)],
        compiler_params=_cp("parallel", "arbitrary"),
    )(place, grad, recv)


def _chip_sum(place, own, recv, name):
    rh, C = own.shape
    br = _row_block(rh, max(256, ELEMENTWISE_BLOCK // C))
    nbh = rh // br

    def body(place_ref, own_ref, r_ref, o_ref):
        o_ref[...] = ((own_ref[...] + r_ref[1].astype(F32)) + r_ref[2].astype(F32)) + r_ref[3].astype(F32)

    return pl.pallas_call(
        body, name=name,
        grid_spec=pltpu.PrefetchScalarGridSpec(
            num_scalar_prefetch=1, grid=(nbh,),
            in_specs=[pl.BlockSpec((br, C), lambda i, pr: (i, 0)),
                      pl.BlockSpec((N_CHIPS, br, C), lambda i, pr: (0, i, 0))],
            out_specs=pl.BlockSpec((br, C), lambda i, pr: (pr[1] * nbh + i, 0))),
        out_shape=SDS((2 * rh, C), F32),
        compiler_params=_cp("parallel"),
    )(place, own, recv)


def _pack_allgather(all_ref, send_sem, recv_sem):
    x, y, c, chips = _mesh_place()
    sib = (x, y, 1 - c)

    def pack(dev, k, to):
        slot = 4 * dev[0] + 2 * dev[1] + dev[2]
        return pltpu.make_async_remote_copy(
            src_ref=all_ref.at[slot], dst_ref=all_ref.at[slot], send_sem=send_sem.at[k],
            recv_sem=recv_sem.at[k], device_id=to, device_id_type=MESH)

    first = [pack((x, y, c), 0, sib)] + [pack((x, y, c), 1 + r, (px, py, c)) for r, (px, py) in enumerate(chips)]
    relays = [pack((px, py, c), 4 + r, sib) for r, (px, py) in enumerate(chips)]

    def start():
        for cpy in first:
            cpy.start()

    def relay():
        for r, (px, py) in enumerate(chips):
            pack((px, py, c), 1 + r, (px, py, c)).wait_recv()
            relays[r].start()

    def finish():
        pack(sib, 0, sib).wait_recv()
        for r, (px, py) in enumerate(chips):
            pack((px, py, 1 - c), 4 + r, sib).wait_recv()
        for cpy in first + relays:
            cpy.wait_send()

    return start, relay, finish


def _finish_exchange(grads, late_all):
    n = len(grads)

    def body(*refs):
        dst, all_ref = refs[n + 1:2 * n + 1], refs[2 * n + 1]
        send_sem, recv_sem, ssend_sem, srecv_sem = refs[2 * n + 2:]
        x, y, c, _ = _mesh_place()
        start, relay, finish = _pack_allgather(all_ref, ssend_sem, srecv_sem)
        start()
        sends, landings = [], []
        for w in range(n):
            rh = grads[w].shape[0] // 2
            mk = lambda cc: pltpu.make_async_remote_copy(
                src_ref=dst[w].at[pl.ds(cc * rh, rh), :], dst_ref=dst[w].at[pl.ds(cc * rh, rh), :],
                send_sem=send_sem.at[w], recv_sem=recv_sem.at[w], device_id=(x, y, 1 - c), device_id_type=MESH)
            sends.append(mk(c))
            landings.append(mk(1 - c))
            sends[-1].start()
        relay()
        finish()
        for cpy in landings:
            cpy.wait_recv()
        for cpy in sends:
            cpy.wait_send()

    res = pl.pallas_call(
        body, name="grad_finish_exchange",
        in_specs=[ANY] * (n + 1), out_specs=[ANY] * (n + 1),
        out_shape=[SDS(g.shape, g.dtype) for g in grads] + [SDS(late_all.shape, late_all.dtype)],
        scratch_shapes=[pltpu.SemaphoreType.DMA((n,)), pltpu.SemaphoreType.DMA((n,)),
                        pltpu.SemaphoreType.DMA((N_DEV - 1,)), pltpu.SemaphoreType.DMA((N_DEV - 1,))],
        input_output_aliases={w: w for w in range(n + 1)},
    )(*grads, late_all)
    return res[:n], res[n]


def _adamw_math(w, g, m, v):
    m = ADAM_B1 * m + (1.0 - ADAM_B1) * g
    v = ADAM_B2 * v + (1.0 - ADAM_B2) * jnp.square(g)
    m_hat = m / (1.0 - ADAM_B1 ** ADAM_STEP)
    v_hat = v / (1.0 - ADAM_B2 ** ADAM_STEP)
    delta = -ADAM_LR * (m_hat / (jnp.sqrt(v_hat) + ADAM_EPS) + ADAM_WD * w)
    return delta, m, v


def _adamw(w, g, m, v, name):
    R, C = w.shape
    br = _row_block(R, 512)
    if g.ndim == 3:
        n_chunks, cw = g.shape[0], g.shape[2]
        g_spec = pl.BlockSpec((None, br, cw), lambda t, i: (t, i, 0))
    else:
        n_chunks, cw = 1, C
        g_spec = pl.BlockSpec((br, cw), lambda t, i: (i, t))

    def body(w_ref, g_ref, m_ref, v_ref, g_out_ref, d_ref, nm_ref, nv_ref):
        gv = g_ref[...]
        g_out_ref[...] = gv
        d_ref[...], nm_ref[...], nv_ref[...] = _adamw_math(w_ref[...], gv, m_ref[...], v_ref[...])

    spec = pl.BlockSpec((br, cw), lambda t, i: (i, t))
    return pl.pallas_call(
        body, grid=(n_chunks, R // br), name=name, in_specs=[spec, g_spec, spec, spec], out_specs=[spec] * 4,
        out_shape=[SDS((R, C), F32)] * 4, compiler_params=_cp("parallel", "parallel"),
    )(w, g, m, v)


def _small_sum_adamw(all_small, all_late, w, m, v):
    loss_row = PACK_ROWS - 8

    def body(all_ref, late_ref, w_ref, m_ref, v_ref, g_ref, d_ref, nm_ref, nv_ref, loss_ref):
        g = all_ref[0]
        late = late_ref[0]
        for k in range(1, N_DEV):
            g = g + all_ref[k]
            late = late + late_ref[k]
        g_ref[...] = g
        g_ref[PACK_LATE_ROW:PACK_LATE_ROW + 8, :] = late
        g = g_ref[...]
        d_ref[...], nm_ref[...], nv_ref[...] = _adamw_math(w_ref[...], g, m_ref[...], v_ref[...])
        total = jnp.sum(g[loss_row:loss_row + 1, :]) * (0.5 / D_MODEL)
        loss_ref[...] = jnp.full(loss_ref.shape, total, F32)

    full = lambda s: pl.BlockSpec(s, lambda i: (0,) * len(s))
    pack = (PACK_ROWS, D_MODEL)
    return pl.pallas_call(
        body, grid=(1,), name="small_sum_adamw",
        in_specs=[full((N_DEV,) + pack), full((N_DEV, 8, D_MODEL)), full(pack), full(pack), full(pack)],
        out_specs=[full(pack)] * 4 + [full((8, 128))],
        out_shape=[SDS(pack, F32)] * 4 + [SDS((8, 128), F32)],
        compiler_params=_cp("arbitrary"),
    )(all_small, all_late, w, m, v)


def _pack_small(grp, scale, g_mix, g_mlp, g_f, loss_lanes):
    def part(vec):
        vec = vec.reshape(1, -1)
        return jnp.pad(vec, ((0, 7), (0, D_MODEL - vec.shape[1])))
    return jnp.concatenate([grp.reshape(-1, D_MODEL), part(scale), part(g_mix), part(g_mlp), part(g_f),
                            part(loss_lanes)], axis=0)


def _unpack_small(pack):
    n_grp = len(POOL_WINDOWS) * POOL_GROUP_W * POOL_GROUP_W // D_MODEL
    grp = pack[:n_grp].reshape(1, len(POOL_WINDOWS), POOL_GROUP_W, POOL_GROUP_W)
    scale = pack[n_grp, :POOL_W].reshape(1, POOL_W)
    g_mix = pack[n_grp + 8].reshape(1, D_MODEL)
    g_mlp = pack[n_grp + 16].reshape(1, D_MODEL)
    g_f = pack[n_grp + 24].reshape(D_MODEL)
    return grp, scale, g_mix, g_mlp, g_f


def _block_diag(grp):
    out = jnp.zeros((POOL_W, POOL_W), grp.dtype)
    for k in range(len(POOL_WINDOWS)):
        out = lax.dynamic_update_slice(out, grp[k], (k * POOL_GROUP_W, k * POOL_GROUP_W))
    return out


def kernel(x, norm_mix_g, w_in, w_att_out, w_pool_grp, pool_scale, w_pool_out, w_out, norm_mlp_g, w_mlp_in, w_mlp_out, norm_final_g, loss_target, m_norm_mix_g, m_w_in, m_w_att_out, m_w_pool_grp, m_pool_scale, m_w_pool_out, m_w_out, m_norm_mlp_g, m_w_mlp_in, m_w_mlp_out, m_norm_final_g, v_norm_mix_g, v_w_in, v_w_att_out, v_w_pool_grp, v_pool_scale, v_w_pool_out, v_w_out, v_norm_mlp_g, v_w_mlp_in, v_w_mlp_out, v_norm_final_g):
    S = x.shape[1]
    xs, target = x[0], loss_target[0]
    big = [w_in[0], w_att_out[0], w_pool_out[0], w_out[0], w_mlp_in[0], w_mlp_out[0]]
    big_m = [m_w_in[0], m_w_att_out[0], m_w_pool_out[0], m_w_out[0], m_w_mlp_in[0], m_w_mlp_out[0]]
    big_v = [v_w_in[0], v_w_att_out[0], v_w_pool_out[0], v_w_out[0], v_w_mlp_in[0], v_w_mlp_out[0]]

    chip = 2 * lax.axis_index("x") + lax.axis_index("y")
    core = lax.axis_index("c")
    place = jnp.stack([chip, core]).astype(jnp.int32)
    names = ("w_in", "w_att_out", "w_pool_out", "w_out", "w_mlp_in", "w_mlp_out")

    shards = [w.astype(BF16) for w in big]
    bufs = [_place_own(sh, N_CHIPS, chip) for sh in shards]
    wbd = _block_diag(w_pool_grp[0]).astype(BF16)
    g_final = norm_final_g.reshape(1, D_MODEL)
    stat_expand, stat_reduce = _stat_matrices()

    u, z_own, landed_in = _norm_inproj_own(xs, norm_mix_g, shards[0], bufs[0])
    (wg_in,) = _pair_forward([landed_in], [shards[0].shape[0]], "w_in_pair_forward")
    (qkv0, qkv1, qkv2, pz, gates), (wg_ao, wg_po, wg_out) = _inproj_rest(u, z_own, wg_in, shards[1:4], bufs[1:4])
    wg_out = wg_out.reshape(D_MODEL, D_MODEL)
    qkv = (qkv0, qkv1, qkv2)
    att = [_attn_fwd(qkv[grp], grp) for grp in range(3)]
    (a, lt0, lt1, lt2, pooled, mixed, p, merged, h1, m), (wg_mi, wg_mo) = _mixer_out(
        [o for o, _ in att], [l for _, l in att], pz, gates, xs, wg_ao, wg_po, wbd, pool_scale, wg_out, norm_mlp_g,
        stat_expand, shards[4:], bufs[4:])
    hid, dh2, dh2b, loss_lanes, dg_final = _mlp_fwd_loss(m, h1, target, wg_mi, wg_mo, g_final)

    def pair_reduce(grads, grad_names):
        recv = _pair_exchange([narrow for _, narrow in grads])
        pair = [_pair_sum(place, g, r, f"pair_sum_{nm}") for (g, _), r, nm in zip(grads, recv, grad_names)]
        return [own for own, _ in pair], [s for _, s in pair]

    def chip_reduce(owns, landed_sums, grad_names):
        return [_chip_sum(place, own, r, f"chip_sum_{nm}") for own, r, nm in zip(owns, landed_sums, grad_names)]

    dpre, dh1, dh1b, dg_mlp = _mlp_bwd(dh2, dh2b, hid, h1, wg_mi, wg_mo, norm_mlp_g)
    own_mlp, sums_mlp = pair_reduce(_wgrads_mlp(m, dpre, hid, dh2b), names[4:])
    (da1, dp1, dgates, da0, dag1, dag2, dd0, dd1, dd2, dmixed, dqp, dscale), landed_mlp = _mixer_bwd(
        dh1b, a, p, mixed, gates, wg_out, wg_ao, wg_po, wbd, pool_scale, stat_reduce, sums_mlp)
    g_mi, g_mo = chip_reduce(own_mlp, landed_mlp, names[4:])
    grads_mixer, g_bd = _wgrads_mixer(a, da1, p, dp1, merged, dh1b, pooled, dmixed)

    zero = jnp.zeros((D_MODEL,), F32)
    g_grp = jnp.stack([g_bd[k * POOL_GROUP_W:(k + 1) * POOL_GROUP_W, k * POOL_GROUP_W:(k + 1) * POOL_GROUP_W]
                       for k in range(len(POOL_WINDOWS))])
    small = _pack_small(g_grp, dscale, zero, dg_mlp, dg_final, loss_lanes)
    *dqkv0, small_all = _attn_bwd(qkv[0], da0, lt0, dd0, 0, packs=_place_own(small, N_DEV, 2 * chip + core))
    dqkv = [dqkv0, _attn_bwd(qkv[1], dag1, lt1, dd1, 1), _attn_bwd(qkv[2], dag2, lt2, dd2, 2)]
    dz = _dz_assemble(dqkv, dqp)
    own_in, sums_in = pair_reduce([_wgrad_in(u, dz, dgates)] + grads_mixer, names[:4])
    dx, dg_mix, landed_in = _inproj_dx(dz, dgates, dh1, xs, norm_mix_g, wg_in, sums_in)
    g_in, g_ao, g_po, g_out = chip_reduce(own_in, landed_in, names[:4])
    late = jnp.pad(dg_mix, ((0, 7), (0, 0)))
    full, late_all = _finish_exchange([g_in, g_ao, g_po, g_out, g_mi, g_mo], _place_own(late, N_DEV, 2 * chip + core))

    small_w = _pack_small(w_pool_grp[0], pool_scale, norm_mix_g, norm_mlp_g, norm_final_g, zero)
    small_m = _pack_small(m_w_pool_grp[0], m_pool_scale, m_norm_mix_g, m_norm_mlp_g, m_norm_final_g, zero)
    small_v = _pack_small(v_w_pool_grp[0], v_pool_scale, v_norm_mix_g, v_norm_mlp_g, v_norm_final_g, zero)
    sg, sd, sm, sv, loss_tile = _small_sum_adamw(small_all, late_all, small_w, small_m, small_v)
    full = [full[0].reshape(CHUNKS_PER_SHARD, D_MODEL, CHUNK)] + list(full[1:])
    upd = [_adamw(w, g, mm, vv, f"adamw_{nm}") for w, g, mm, vv, nm in zip(big, full, big_m, big_v, names)]

    def ordered(small_pack, bigs):
        grp, scale, g_mix, g_mlp, g_f = _unpack_small(small_pack)
        b_in, b_ao, b_po, b_out, b_mi, b_mo = [b[None] for b in bigs]
        return (g_mix, b_in, b_ao, grp, scale, b_po, b_out, g_mlp, b_mi, b_mo, g_f)

    return (loss_tile[0, 0], dx[None],
            *ordered(sg, [t[0] for t in upd]),
            *ordered(sd, [t[1] for t in upd]),
            *ordered(sm, [t[2] for t in upd]),
            *ordered(sv, [t[3] for t in upd]))
```

```python
import functools

import jax
import jax.numpy as jnp
from jax import lax
from jax.experimental import pallas as pl
from jax.experimental.pallas import tpu as pltpu

F32 = jnp.float32
BF16 = jnp.bfloat16
SDS = jax.ShapeDtypeStruct
MESH = pl.DeviceIdType.MESH

D_MODEL = 1024
D_FF = 4096
N_CHIPS = 4
N_DEV = 8
DILATIONS = (1, 4, 16)
BAND = 128
GROUP_W = 256
PAIR_W = 128
HEAD_W = 64
STAT_W = 128
STAT_HEAD_W = 32
POOL_W = 768
POOL_GROUP_W = 192
POOL_WINDOWS = (2, 4, 8, 16)
POOL_HALO = 16
N_IN = 5120
CHUNK = 256
N_CHUNKS = N_IN // CHUNK
N_DZ_CHUNKS = 12
CHUNKS_PER_SHARD = 5
WGRAD_IN_GROUP = 4
NORM_EPS = 1e-6
ALIBI_MAX_BIAS = 8.0
N_HEADS = 12
NEG = -1e30

ADAM_LR, ADAM_B1, ADAM_B2, ADAM_EPS, ADAM_WD, ADAM_STEP = 0.001, 0.9, 0.999, 1e-08, 0.01, 10

TM = 512
TMB = 512
TMZ = 1024
ATT_TILE = ((1, 16), (4, 4), (16, 1))
BK = 4096
ELEMENTWISE_BLOCK = 1 << 20
VMEM_LIMIT = 56 * 1024 * 1024
PACK_ROWS = 184
PACK_LATE_ROW = 152

NT = (((1,), (1,)), ((), ()))
TN = (((0,), (0,)), ((), ()))


def _cp(*sem):
    return pltpu.CompilerParams(dimension_semantics=sem, vmem_limit_bytes=VMEM_LIMIT)


def _resident(shape):
    nd = len(shape)
    return pl.BlockSpec(shape, lambda *_: (0,) * nd, pipeline_mode=pl.Buffered(1))


def _row_block(rows, cap=256):
    return max(b for b in range(16, min(rows, cap) + 1, 16) if rows % b == 0)


def _dot(a, b):
    return jnp.dot(a, b, preferred_element_type=F32)


def _dot_nt(a, b):
    return lax.dot_general(a, b, NT, preferred_element_type=F32)


def _dot_tn(a, b):
    return lax.dot_general(a, b, TN, preferred_element_type=F32)


def _w_in_chunk(w_ref, n):
    return w_ref[n // CHUNKS_PER_SHARD, :, (n % CHUNKS_PER_SHARD) * CHUNK:(n % CHUNKS_PER_SHARD + 1) * CHUNK]


def _sigmoid(x):
    return 0.5 * jnp.tanh(0.5 * x.astype(F32)) + 0.5


def _rms_fwd(x, g):
    r = lax.rsqrt(jnp.mean(x * x, axis=-1, keepdims=True) + NORM_EPS)
    xh = x * r
    return xh * g, xh, r


def _rms_bwd(dy, xh, r, g):
    dxh = dy * g
    return r * (dxh - xh * jnp.mean(dxh * xh, axis=-1, keepdims=True))


def _per_head_lanes(cols):
    rows = cols[0].shape[0]
    lane = lax.broadcasted_iota(jnp.int32, (rows, STAT_W), 1)
    out = cols[3]
    for h in (2, 1, 0):
        out = jnp.where(lane < (h + 1) * STAT_HEAD_W, cols[h], out)
    return out


def _head_col(stat, h):
    return stat[:, h * STAT_HEAD_W:h * STAT_HEAD_W + 1]


def _stat_matrices():
    s = lax.broadcasted_iota(jnp.int32, (STAT_W, GROUP_W), 0)
    c = lax.broadcasted_iota(jnp.int32, (STAT_W, GROUP_W), 1)
    expand = (s == (c // HEAD_W) * STAT_HEAD_W).astype(BF16)
    reduce = (s // STAT_HEAD_W == c // HEAD_W).astype(BF16).T
    return expand, reduce


def _dot_split(x, m):
    hi = x.astype(BF16)
    lo = (x - hi.astype(F32)).astype(BF16)
    return _dot(hi, m) + _dot(lo, m)


def _deinterleave_store(val, s_ref, out_ref, lead, d, rows, dtype):
    if d == 1:
        out_ref[lead + (0,)] = val.astype(dtype)
        return
    for h in range(2):
        s_ref[h] = val[:, h * PAIR_W:(h + 1) * PAIR_W]
    for r in range(d):
        for h in range(2):
            out_ref[lead + (r, slice(None), slice(h * PAIR_W, (h + 1) * PAIR_W))] = (
                s_ref[h, pl.ds(r, rows // d, stride=d), :].astype(dtype))


def _interleave_load(in_ref, lead, s_ref, d, rows):
    for r in range(d):
        for h in range(2):
            s_ref[h, pl.ds(r, rows // d, stride=d), :] = (
                in_ref[lead + (r, slice(None), slice(h * PAIR_W, (h + 1) * PAIR_W))].astype(F32))


def _norm_inproj_own(x, g, w_own, buf):
    S = x.shape[0]
    tm = min(TMZ, S)
    n_tiles = S // tm

    def body(x_ref, g_ref, w_ref, shard_ref, buf_in, u_ref, z_ref, buf_ref, send_sem, recv_sem):
        i = pl.program_id(0)

        def copies():
            return _weight_half_copies([shard_ref], [buf_ref], [w_own.shape[0]], send_sem, recv_sem)

        @pl.when(i == 0)
        def _():
            for cpy in copies():
                cpy.start()

        u = _rms_fwd(x_ref[...], g_ref[...])[0].astype(BF16)
        u_ref[...] = u
        for t in range(CHUNKS_PER_SHARD):
            z_ref[t] = _dot(u, w_ref[:, t * CHUNK:(t + 1) * CHUNK]).astype(BF16)

        @pl.when(i == n_tiles - 1)
        def _():
            for cpy in copies():
                cpy.wait()

    row = lambda w: pl.BlockSpec((tm, w), lambda i: (i, 0))
    return pl.pallas_call(
        body, grid=(n_tiles,), name="norm_inproj_own",
        in_specs=[row(D_MODEL), _resident((1, D_MODEL)), _resident(w_own.shape), ANY, ANY],
        out_specs=[row(D_MODEL), pl.BlockSpec((CHUNKS_PER_SHARD, tm, CHUNK), lambda i: (0, i, 0)), ANY],
        out_shape=[SDS((S, D_MODEL), BF16), SDS((CHUNKS_PER_SHARD, S, CHUNK), BF16), SDS(buf.shape, buf.dtype)],
        scratch_shapes=[pltpu.SemaphoreType.DMA((3,)), pltpu.SemaphoreType.DMA((3,))],
        input_output_aliases={4: 2},
        compiler_params=_cp("arbitrary"),
    )(x, g, w_own, w_own, buf)


def _hosted_allgather(i, n_steps, shard_refs, buf_refs, rows, sems):
    send_sem, recv_sem, fsend_sem, frecv_sem = sems
    ici = lambda: _weight_half_copies(shard_refs, buf_refs, rows, send_sem, recv_sem)
    forward = lambda: _pair_forward_copies(buf_refs, rows, fsend_sem, frecv_sem)

    def begin():
        @pl.when(i == 0)
        def _():
            for cpy in ici():
                cpy.start()

        @pl.when(i == (7 * n_steps) // 8)
        def _():
            for cpy, (fwd, _) in zip(ici(), forward()):
                cpy.wait_recv()
                fwd.start()

    def end():
        @pl.when(i == n_steps - 1)
        def _():
            for cpy, (fwd, landing) in zip(ici(), forward()):
                landing.wait_recv()
                fwd.wait_send()
                cpy.wait_send()

    return begin, end


def _inproj_rest(u, z_own, w_in, shards, bufs):
    S = u.shape[0]
    n_tiles = S // TM
    n = len(shards)

    def body(*refs):
        u_ref, zown_ref, w_ref = refs[0:3]
        shard_refs = refs[3:3 + n]
        q0_ref, q1_ref, q2_ref, pz_ref, gate_ref = refs[3 + 2 * n:8 + 2 * n]
        buf_refs = refs[8 + 2 * n:8 + 3 * n]
        s_ref = refs[8 + 3 * n]
        i = pl.program_id(0)
        chip = 2 * lax.axis_index("x") + lax.axis_index("y")
        begin, end = _hosted_allgather(i, n_tiles, shard_refs, buf_refs, [sh.shape[0] for sh in shards],
                                       refs[9 + 3 * n:])
        begin()

        u = u_ref[...]
        qkv_refs = (q0_ref, q1_ref, q2_ref)

        def emit(k, zc):
            if k < 9:
                which, grp = k // 3, k % 3
                if which == 0:
                    zc = zc * 0.125
                _deinterleave_store(zc, s_ref, qkv_refs[grp], (which,), DILATIONS[grp], TM, BF16)
            elif k < N_DZ_CHUNKS:
                pz_ref[:, (k - 9) * CHUNK:(k - 8) * CHUNK] = zc.astype(BF16)
            else:
                gate_ref[:, (k - N_DZ_CHUNKS) * CHUNK:(k - N_DZ_CHUNKS + 1) * CHUNK] = zc.astype(BF16)

        def all_chunks(own_shard):
            for k in range(N_CHUNKS):
                if k // CHUNKS_PER_SHARD == own_shard:
                    emit(k, zown_ref[k % CHUNKS_PER_SHARD].astype(F32))
                else:
                    emit(k, _dot(u, _w_in_chunk(w_ref, k)))

        for shard in range(N_CHIPS):
            pl.when(chip == shard)(functools.partial(all_chunks, shard))
        end()

    row = lambda w: pl.BlockSpec((TM, w), lambda i: (i, 0))
    res = pl.pallas_call(
        body, grid=(n_tiles,), name="inproj_rest",
        in_specs=[row(D_MODEL), pl.BlockSpec((CHUNKS_PER_SHARD, TM, CHUNK), lambda i: (0, i, 0)),
                  _resident(w_in.shape)] + [ANY] * (2 * n),
        out_specs=[pl.BlockSpec((3, d, TM // d, GROUP_W), lambda i: (0, 0, i, 0)) for d in DILATIONS]
        + [row(POOL_W), row(2 * D_MODEL)] + [ANY] * n,
        out_shape=[SDS((3, d, S // d, GROUP_W), BF16) for d in DILATIONS]
        + [SDS((S, POOL_W), BF16), SDS((S, 2 * D_MODEL), BF16)] + [SDS(b.shape, b.dtype) for b in bufs],
        scratch_shapes=[pltpu.VMEM((2, TM, PAIR_W), F32)] + [pltpu.SemaphoreType.DMA((3 * n,))] * 4,
        input_output_aliases={3 + n + w: 5 + w for w in range(n)},
        compiler_params=_cp("arbitrary"),
    )(u, z_own, w_in, *shards, *bufs)
    return res[:5], res[5:]


def _band_bias(grp, d):
    row = lax.broadcasted_iota(jnp.int32, (BAND, 2 * BAND), 0)
    col = lax.broadcasted_iota(jnp.int32, (BAND, 2 * BAND), 1)
    steps = BAND + row - col
    valid = (steps >= 0) & (steps <= BAND)
    stepsf = (steps * d).astype(F32)
    biases = []
    for hh in range(4):
        slope = 2.0 ** (-ALIBI_MAX_BIAS * (grp * 4 + hh + 1) / N_HEADS)
        biases.append(jnp.where(valid, -slope * stepsf, NEG))
    return biases, col


def _attn_tiles(grp, L):
    rr, rb = ATT_TILE[grp]
    rb = min(rb, L // BAND)
    return rr, rb, L // (rb * BAND)


def _kv_tile(cur_ref, prev_ref, rr, rb, cs):
    if rb == 0:
        return jnp.concatenate([prev_ref[rr, :, cs], cur_ref[rr, 0:BAND, cs]], axis=0)
    return cur_ref[rr, (rb - 1) * BAND:(rb + 1) * BAND, cs]


def _attn_fwd(qkv, grp):
    d = DILATIONS[grp]
    L = qkv.shape[2]
    RR, RB, nb = _attn_tiles(grp, L)

    def body(q_ref, kc_ref, kp_ref, vc_ref, vp_ref, o_ref, lse_ref):
        i = pl.program_id(0)
        biases, col = _band_bias(grp, d)
        first_keys_ok = (col >= BAND) | (i > 0)
        is_a = lax.broadcasted_iota(jnp.int32, (BAND, PAIR_W), 1) < HEAD_W
        heads = [(rr, rb, cp, h2) for rr in range(RR) for rb in range(RB) for cp in range(2) for h2 in range(2)]

        def tile(head):
            rr, rb, cp, _ = head
            return rr, rb, slice(rb * BAND, (rb + 1) * BAND), slice(cp * PAIR_W, (cp + 1) * PAIR_W)

        def scores(head):
            rr, rb, rows, cs = tile(head)
            q2 = q_ref[rr, rows, cs]
            b = biases[head[2] * 2 + head[3]]
            if rb == 0:
                b = jnp.where(first_keys_ok, b, NEG)
            sel = is_a if head[3] == 0 else jnp.logical_not(is_a)
            return _dot_nt(jnp.where(sel, q2, jnp.zeros_like(q2)), _kv_tile(kc_ref, kp_ref, rr, rb, cs)) + b

        s_next = scores(heads[0])
        outs, lses = {}, {}
        for idx, head in enumerate(heads):
            s = s_next
            if idx + 1 < len(heads):
                s_next = scores(heads[idx + 1])
            rr, rb, rows, cs = tile(head)
            m = jnp.max(s, axis=-1, keepdims=True)
            p = jnp.exp(s - m)
            l = jnp.sum(p, axis=-1, keepdims=True)
            outs[head[3]] = _dot(p.astype(BF16), _kv_tile(vc_ref, vp_ref, rr, rb, cs)) * (1.0 / l)
            lses[head[2] * 2 + head[3]] = m + jnp.log(l)
            if head[3] == 1:
                o_ref[rr, rows, cs] = jnp.where(is_a, outs[0], outs[1]).astype(BF16)
            if head[2] == 1 and head[3] == 1:
                lse_ref[rr, rows, :] = _per_head_lanes(lses)

    cur = lambda w: pl.BlockSpec((None, RR, RB * BAND, GROUP_W), lambda i, j: (w, j, i, 0))
    prev = lambda w: pl.BlockSpec((None, RR, BAND, GROUP_W), lambda i, j: (w, j, jnp.maximum(i * RB - 1, 0), 0))
    return pl.pallas_call(
        body, grid=(nb, d // RR), name=f"attn_fwd_g{grp}",
        in_specs=[cur(0), cur(1), prev(1), cur(2), prev(2)],
        out_specs=[pl.BlockSpec((RR, RB * BAND, GROUP_W), lambda i, j: (j, i, 0)),
                   pl.BlockSpec((RR, RB * BAND, STAT_W), lambda i, j: (j, i, 0))],
        out_shape=[SDS((d, L, GROUP_W), BF16), SDS((d, L, STAT_W), F32)],
        compiler_params=_cp("parallel", "parallel"),
    )(qkv, qkv, qkv, qkv, qkv)


def _pool_column_select(col, vals):
    return jnp.where(col < POOL_GROUP_W, vals[0],
                     jnp.where(col < 2 * POOL_GROUP_W, vals[1],
                               jnp.where(col < 3 * POOL_GROUP_W, vals[2], vals[3])))


def _pool_inv_count(i, rows):
    t = i * rows + lax.broadcasted_iota(jnp.int32, (rows, POOL_W), 0)
    col = lax.broadcasted_iota(jnp.int32, (rows, POOL_W), 1)
    win = _pool_column_select(col, POOL_WINDOWS)
    return 1.0 / jnp.minimum(t + 1, win).astype(F32), col


def _mixer_out(outs, lses, pz, gates, x, w_ao, w_po, wbd, scale, w_out, g_mlp, expand, shards, bufs):
    S = x.shape[0]
    n_tiles = S // TMB
    n = len(shards)

    def body(*refs):
        (o0_ref, l0_ref, o1_ref, l1_ref, o2_ref, l2_ref, pz_ref, halo_ref, gate_ref, x_ref,
         wao_ref, wpo_ref, wbd_ref, sc_ref, wout_ref, g_ref, expand_ref) = refs[0:17]
        shard_refs = refs[17:17 + n]
        (a_ref, lt0_ref, lt1_ref, lt2_ref, pooled_ref, mixed_ref, p_ref, merged_ref, h1_ref,
         m_ref) = refs[17 + 2 * n:27 + 2 * n]
        buf_refs = refs[27 + 2 * n:27 + 3 * n]
        so1, sl1, so2, sl2, slt, ext_ref = refs[27 + 3 * n:33 + 3 * n]
        i = pl.program_id(0)
        begin, end = _hosted_allgather(i, n_tiles, shard_refs, buf_refs, [sh.shape[0] for sh in shards],
                                       refs[33 + 3 * n:])
        begin()
        _interleave_load(o1_ref, (), so1, DILATIONS[1], TMB)
        _interleave_load(o2_ref, (), so2, DILATIONS[2], TMB)
        for ref, sref, d in ((l1_ref, sl1, DILATIONS[1]), (l2_ref, sl2, DILATIONS[2])):
            for r in range(d):
                sref[0, pl.ds(r, TMB // d, stride=d), :] = ref[r]
        l0, l1, l2 = l0_ref[0], sl1[0], sl2[0]
        mx = jnp.maximum(jnp.maximum(l0, l1), l2)
        e0, e1, e2 = jnp.exp(l0 - mx), jnp.exp(l1 - mx), jnp.exp(l2 - mx)
        den = e0 + e1 + e2
        inv = 1.0 / den
        slt[0] = mx + jnp.log(den)
        w0, w1, w2 = [_dot_split(e * inv, expand_ref[...]) for e in (e0, e1, e2)]
        for h in range(2):
            hs = slice(h * PAIR_W, (h + 1) * PAIR_W)
            a_ref[:, hs] = (w0[:, hs] * o0_ref[0, :, hs].astype(F32) + w1[:, hs] * so1[h]
                            + w2[:, hs] * so2[h]).astype(BF16)
        lt0_ref[0] = slt[0]
        for ref, d in ((lt1_ref, DILATIONS[1]), (lt2_ref, DILATIONS[2])):
            for r in range(d):
                ref[r] = slt[0, pl.ds(r, TMB // d, stride=d), :]

        pz_t = pz_ref[...].astype(F32)
        ext_ref[0:POOL_HALO, :] = jnp.where(i > 0, halo_ref[...].astype(F32), 0.0)
        ext_ref[POOL_HALO:, :] = pz_t
        sums = []
        acc = ext_ref[...]
        for k in (1, 2, 4, 8):
            acc = acc + pltpu.roll(acc, k, 0)
            sums.append(acc[POOL_HALO:, :])
        inv_cnt, col = _pool_inv_count(i, TMB)
        pooled = (_pool_column_select(col, sums) * inv_cnt - pz_t).astype(BF16)
        pooled_ref[...] = pooled
        mixed = _dot(pooled, wbd_ref[...])
        mixed_ref[...] = mixed.astype(BF16)
        p = (mixed * sc_ref[...]).astype(BF16)
        p_ref[...] = p

        a = a_ref[...]
        for j in range(N_CHIPS):
            js = slice(j * CHUNK, (j + 1) * CHUNK)
            ga = gate_ref[:, js]
            gp = gate_ref[:, D_MODEL + j * CHUNK:D_MODEL + (j + 1) * CHUNK]
            mj = _sigmoid(ga) * _dot(a, wao_ref[j]) + _sigmoid(gp) * _dot(p, wpo_ref[j])
            merged_ref[:, js] = mj.astype(BF16)
        h1 = x_ref[...] + _dot(merged_ref[...], wout_ref[...])
        h1_ref[...] = h1
        m_ref[...] = _rms_fwd(h1, g_ref[...])[0].astype(BF16)
        end()

    row = lambda w: pl.BlockSpec((TMB, w), lambda i: (i, 0))
    grp_spec = lambda d: pl.BlockSpec((d, TMB // d, GROUP_W), lambda i: (0, i, 0))
    stat_spec = lambda d: pl.BlockSpec((d, TMB // d, STAT_W), lambda i: (0, i, 0))
    halo = pl.BlockSpec((POOL_HALO, POOL_W), lambda i: (jnp.maximum(i * (TMB // POOL_HALO) - 1, 0), 0))
    d0, d1, d2 = DILATIONS
    pair_scratch = pltpu.VMEM((2, TMB, PAIR_W), F32)
    stat_scratch = pltpu.VMEM((1, TMB, STAT_W), F32)
    res = pl.pallas_call(
        body, grid=(n_tiles,), name="mixer_out",
        in_specs=[grp_spec(d0), stat_spec(d0), grp_spec(d1), stat_spec(d1), grp_spec(d2), stat_spec(d2),
                  row(POOL_W), halo, row(2 * D_MODEL), row(D_MODEL),
                  _resident(w_ao.shape), _resident(w_po.shape), _resident(wbd.shape), _resident(scale.shape),
                  _resident(w_out.shape), _resident(g_mlp.shape), _resident(expand.shape)] + [ANY] * (2 * n),
        out_specs=[row(GROUP_W), stat_spec(d0), stat_spec(d1), stat_spec(d2),
                   row(POOL_W), row(POOL_W), row(POOL_W), row(D_MODEL), row(D_MODEL), row(D_MODEL)] + [ANY] * n,
        out_shape=[SDS((S, GROUP_W), BF16)] + [SDS((d, S // d, STAT_W), F32) for d in DILATIONS]
        + [SDS((S, POOL_W), BF16), SDS((S, POOL_W), BF16), SDS((S, POOL_W), BF16),
           SDS((S, D_MODEL), BF16), SDS((S, D_MODEL), F32), SDS((S, D_MODEL), BF16)]
        + [SDS(b.shape, b.dtype) for b in bufs],
        scratch_shapes=[pair_scratch, stat_scratch, pair_scratch, stat_scratch, stat_scratch,
                        pltpu.VMEM((TMB + POOL_HALO, POOL_W), F32)] + [pltpu.SemaphoreType.DMA((3 * n,))] * 4,
        input_output_aliases={17 + n + w: 10 + w for w in range(n)},
        compiler_params=_cp("arbitrary"),
    )(outs[0], lses[0], outs[1], lses[1], outs[2], lses[2], pz, pz, gates, x,
      w_ao, w_po, wbd, scale, w_out, g_mlp, expand, *shards, *bufs)
    return res[:10], res[10:]


def _mlp_fwd_loss(m, h1, target, w_mi, w_mo, g_f):
    S = m.shape[0]

    def body(m_ref, h1_ref, t_ref, wmi_ref, wmo_ref, g_ref, hid_ref, dh2_ref, dh2b_ref, loss_ref, dg_ref):
        @pl.when(pl.program_id(0) == 0)
        def _():
            loss_ref[...] = jnp.zeros_like(loss_ref)
            dg_ref[...] = jnp.zeros_like(dg_ref)

        mt = m_ref[...]
        acc = h1_ref[...]
        for c in range(N_CHIPS):
            hid = jnp.square(jnp.maximum(_dot(mt, wmi_ref[c]), 0.0)).astype(BF16)
            hid_ref[:, c * D_MODEL:(c + 1) * D_MODEL] = hid
            acc = acc + _dot(hid, wmo_ref[c])
        g = g_ref[...]
        y, hh, r = _rms_fwd(acc, g)
        e = y - t_ref[...]
        loss_ref[...] += jnp.sum(e * e, axis=0, keepdims=True)
        dy = e * (1.0 / D_MODEL)
        dg_ref[...] += jnp.sum(dy * hh, axis=0, keepdims=True)
        dh2 = _rms_bwd(dy, hh, r, g)
        dh2_ref[...] = dh2
        dh2b_ref[...] = dh2.astype(BF16)

    row = lambda w: pl.BlockSpec((TM, w), lambda i: (i, 0))
    vec = pl.BlockSpec((1, D_MODEL), lambda i: (0, 0))
    return pl.pallas_call(
        body, grid=(S // TM,), name="mlp_fwd_loss",
        in_specs=[row(D_MODEL), row(D_MODEL), row(D_MODEL), _resident(w_mi.shape), _resident(w_mo.shape),
                  _resident(g_f.shape)],
        out_specs=[row(D_FF), row(D_MODEL), row(D_MODEL), vec, vec],
        out_shape=[SDS((S, D_FF), BF16), SDS((S, D_MODEL), F32), SDS((S, D_MODEL), BF16),
                   SDS((1, D_MODEL), F32), SDS((1, D_MODEL), F32)],
        compiler_params=_cp("arbitrary"),
    )(m, h1, target, w_mi, w_mo, g_f)


def _mlp_bwd(dh2, dh2b, hid, h1, w_mi, w_mo, g_mlp):
    S = dh2.shape[0]

    def body(dh2_ref, dh2b_ref, hid_ref, h1_ref, wmi_ref, wmo_ref, g_ref, dpre_ref, dh1_ref, dh1b_ref, dg_ref):
        @pl.when(pl.program_id(0) == 0)
        def _():
            dg_ref[...] = jnp.zeros_like(dg_ref)

        d2 = dh2b_ref[...]
        dm = jnp.zeros((TM, D_MODEL), F32)
        dhid_next = _dot_nt(d2, wmo_ref[0])
        for c in range(N_CHIPS):
            cs = slice(c * D_MODEL, (c + 1) * D_MODEL)
            dhid = dhid_next
            if c + 1 < N_CHIPS:
                dhid_next = _dot_nt(d2, wmo_ref[c + 1])
            dpre = (dhid * (2.0 * jnp.sqrt(hid_ref[:, cs].astype(F32)))).astype(BF16)
            dpre_ref[:, cs] = dpre
            dm = dm + _dot_nt(dpre, wmi_ref[c])
        g = g_ref[...]
        _, hh, r = _rms_fwd(h1_ref[...], g)
        dg_ref[...] += jnp.sum(dm * hh, axis=0, keepdims=True)
        dh1 = dh2_ref[...] + _rms_bwd(dm, hh, r, g)
        dh1_ref[...] = dh1
        dh1b_ref[...] = dh1.astype(BF16)

    row = lambda w: pl.BlockSpec((TM, w), lambda i: (i, 0))
    return pl.pallas_call(
        body, grid=(S // TM,), name="mlp_bwd",
        in_specs=[row(D_MODEL), row(D_MODEL), row(D_FF), row(D_MODEL), _resident(w_mi.shape),
                  _resident(w_mo.shape), _resident(g_mlp.shape)],
        out_specs=[row(D_FF), row(D_MODEL), row(D_MODEL), pl.BlockSpec((1, D_MODEL), lambda i: (0, 0))],
        out_shape=[SDS((S, D_FF), BF16), SDS((S, D_MODEL), F32), SDS((S, D_MODEL), BF16), SDS((1, D_MODEL), F32)],
        compiler_params=_cp("arbitrary"),
    )(dh2, dh2b, hid, h1, w_mi, w_mo, g_mlp)


def _mixer_bwd(dh1b, a, p, mixed, gates, w_out, w_ao, w_po, wbd, scale, stat_reduce, sums):
    S = a.shape[0]
    n_tiles = S // TMB
    n = len(sums)

    def body(*refs):
        (dh1b_ref, a_ref, p_ref, mixed_ref, gate_ref, wout_ref, wao_ref, wpo_ref, wbd_ref, sc_ref,
         ones_ref) = refs[0:11]
        sum_refs = refs[11:11 + n]
        (da1_ref, dp1_ref, dgate_ref, da0_ref, dag1_ref, dag2_ref, dd0_ref, dd1_ref, dd2_ref,
         dmixed_ref, dqp_ref, dscale_ref) = refs[11 + n:23 + n]
        land_refs = refs[23 + n:23 + 2 * n]
        s_da, s_dd, send_sem, recv_sem = refs[23 + 2 * n:]
        i = pl.program_id(0)

        @pl.when(i == 0)
        def _():
            dscale_ref[...] = jnp.zeros_like(dscale_ref)
            for cpy in _chip_sum_copies(sum_refs, land_refs, send_sem, recv_sem):
                cpy.start()

        dmerged = _dot_nt(dh1b_ref[...], wout_ref[...])
        a = a_ref[...]
        p = p_ref[...]
        da = jnp.zeros((TMB, GROUP_W), F32)
        dp = jnp.zeros((TMB, POOL_W), F32)
        for j in range(N_CHIPS):
            js = slice(j * CHUNK, (j + 1) * CHUNK)
            sa = _sigmoid(gate_ref[:, js])
            sp = _sigmoid(gate_ref[:, D_MODEL + j * CHUNK:D_MODEL + (j + 1) * CHUNK])
            dmj = dmerged[:, js]
            da1 = (dmj * sa).astype(BF16)
            dp1 = (dmj * sp).astype(BF16)
            da1_ref[:, js] = da1
            dp1_ref[:, js] = dp1
            dgate_ref[j] = (dmj * _dot(a, wao_ref[j]) * sa * (1.0 - sa)).astype(BF16)
            dgate_ref[N_CHIPS + j] = (dmj * _dot(p, wpo_ref[j]) * sp * (1.0 - sp)).astype(BF16)
            da = da + _dot_nt(da1, wao_ref[j])
            dp = dp + _dot_nt(dp1, wpo_ref[j])

        dd = _dot_split(da * a.astype(F32), ones_ref[...])
        da0_ref[0] = da.astype(BF16)
        dd0_ref[0] = dd
        for h in range(2):
            s_da[h] = da[:, h * PAIR_W:(h + 1) * PAIR_W]
        s_dd[0] = dd
        for refs, d in (((dag1_ref, dd1_ref), DILATIONS[1]), ((dag2_ref, dd2_ref), DILATIONS[2])):
            for r in range(d):
                for h in range(2):
                    hs = slice(h * PAIR_W, (h + 1) * PAIR_W)
                    refs[0][r, :, hs] = s_da[h, pl.ds(r, TMB // d, stride=d), :].astype(BF16)
                refs[1][r] = s_dd[0, pl.ds(r, TMB // d, stride=d), :]

        sc = sc_ref[...]
        dscale_ref[...] += jnp.sum(dp * mixed_ref[...].astype(F32), axis=0, keepdims=True)
        dmixed = (dp * sc).astype(BF16)
        dmixed_ref[...] = dmixed
        inv_cnt, _ = _pool_inv_count(i, TMB)
        dqp_ref[...] = (_dot_nt(dmixed, wbd_ref[...]) * inv_cnt).astype(BF16)

        @pl.when(i == n_tiles - 1)
        def _():
            for cpy in _chip_sum_copies(sum_refs, land_refs, send_sem, recv_sem):
                cpy.wait()

    row = lambda w: pl.BlockSpec((TMB, w), lambda i: (i, 0))
    grp_spec = lambda d: pl.BlockSpec((d, TMB // d, GROUP_W), lambda i: (0, i, 0))
    stat_spec = lambda d: pl.BlockSpec((d, TMB // d, STAT_W), lambda i: (0, i, 0))
    d0, d1, d2 = DILATIONS
    res = pl.pallas_call(
        body, grid=(n_tiles,), name="mixer_bwd",
        in_specs=[row(D_MODEL), row(GROUP_W), row(POOL_W), row(POOL_W), row(2 * D_MODEL),
                  _resident(w_out.shape), _resident(w_ao.shape), _resident(w_po.shape), _resident(wbd.shape),
                  _resident(scale.shape), _resident(stat_reduce.shape)] + [ANY] * n,
        out_specs=[row(D_MODEL), row(D_MODEL), pl.BlockSpec((2 * N_CHIPS, TMB, CHUNK), lambda i: (0, i, 0)),
                   grp_spec(d0), grp_spec(d1), grp_spec(d2), stat_spec(d0), stat_spec(d1), stat_spec(d2),
                   row(POOL_W), row(POOL_W), pl.BlockSpec((1, POOL_W), lambda i: (0, 0))] + [ANY] * n,
        out_shape=[SDS((S, D_MODEL), BF16), SDS((S, D_MODEL), BF16), SDS((2 * N_CHIPS, S, CHUNK), BF16)]
        + [SDS((d, S // d, GROUP_W), BF16) for d in DILATIONS]
        + [SDS((d, S // d, STAT_W), F32) for d in DILATIONS]
        + [SDS((S, POOL_W), BF16), SDS((S, POOL_W), BF16), SDS((1, POOL_W), F32)]
        + [SDS(t.shape, t.dtype) for t in sums],
        scratch_shapes=[pltpu.VMEM((2, TMB, PAIR_W), F32), pltpu.VMEM((1, TMB, STAT_W), F32),
                        pltpu.SemaphoreType.DMA((3 * n,)), pltpu.SemaphoreType.DMA((3 * n,))],
        compiler_params=_cp("arbitrary"),
    )(dh1b, a, p, mixed, gates, w_out, w_ao, w_po, wbd, scale, stat_reduce, *sums)
    return res[:12], res[12:]


def _attn_bwd(qkv, da, lt, dd, grp, packs=None):
    d = DILATIONS[grp]
    L = qkv.shape[2]
    RR, RB, nb = _attn_tiles(grp, L)
    n_j = d // RR
    hosted = packs is not None

    def body(*refs):
        q_ref, kc_ref, kp_ref, vc_ref, vp_ref, da_ref, lt_ref, dd_ref = refs[0:8]
        dq_ref, dk_ref, dv_ref = refs[8 + hosted:11 + hosted]
        dk_acc, dv_acc = refs[11 + 2 * hosted:13 + 2 * hosted]
        i = pl.program_id(1)
        if hosted:
            j = pl.program_id(0)
            start, relay, finish = _pack_allgather(refs[11 + hosted], *refs[13 + 2 * hosted:])
            pl.when((j == 0) & (i == 0))(start)
            pl.when((j == 0) & (i == nb // 2))(relay)

        @pl.when(i == 0)
        def _():
            dk_acc[...] = jnp.zeros_like(dk_acc)
            dv_acc[...] = jnp.zeros_like(dv_acc)

        def compute(cur, prv):
            dk_acc[cur] = jnp.zeros((RR, RB * BAND, GROUP_W), F32)
            dv_acc[cur] = jnp.zeros((RR, RB * BAND, GROUP_W), F32)
            biases, col = _band_bias(grp, d)
            first_keys_ok = (col >= BAND) | (i > 0)
            is_a = lax.broadcasted_iota(jnp.int32, (BAND, PAIR_W), 1) < HEAD_W
            for rr in range(RR):
                for rb in range(RB):
                    rows = slice(rb * BAND, (rb + 1) * BAND)
                    for cp in range(2):
                        cs = slice(cp * PAIR_W, (cp + 1) * PAIR_W)
                        q2 = q_ref[rr, rows, cs]
                        da2 = da_ref[rr, rows, cs]
                        lt2 = lt_ref[rr, rows, :]
                        dd2 = dd_ref[rr, rows, :]
                        kcat = _kv_tile(kc_ref, kp_ref, rr, rb, cs)
                        vcat = _kv_tile(vc_ref, vp_ref, rr, rb, cs)
                        q2t = q2.T
                        da2t = da2.T
                        dqs, dkts, dvts, scores, dpvs = [], [], [], [], []
                        for h2 in range(2):
                            sel = is_a if h2 == 0 else jnp.logical_not(is_a)
                            b = biases[cp * 2 + h2]
                            if rb == 0:
                                b = jnp.where(first_keys_ok, b, NEG)
                            scores.append(_dot_nt(jnp.where(sel, q2, jnp.zeros_like(q2)), kcat) + b)
                            dpvs.append(_dot_nt(jnp.where(sel, da2, jnp.zeros_like(da2)), vcat))
                        for h2 in range(2):
                            lane0 = h2 * HEAD_W
                            p = jnp.exp(scores[h2] - _head_col(lt2, cp * 2 + h2))
                            ds = (p * (dpvs[h2] - _head_col(dd2, cp * 2 + h2))).astype(BF16)
                            dqs.append(_dot(ds, kcat))
                            dkts.append(_dot(q2t[lane0:lane0 + HEAD_W, :], ds))
                            dvts.append(_dot(da2t[lane0:lane0 + HEAD_W, :], p.astype(BF16)))
                        dq_ref[rr, rows, cs] = (jnp.where(is_a, dqs[0], dqs[1]) * 0.125).astype(BF16)
                        dkc = jnp.concatenate(dkts, axis=0).T
                        dvc = jnp.concatenate(dvts, axis=0).T
                        if rb == 0:
                            last = slice((RB - 1) * BAND, RB * BAND)
                            dk_acc[prv, rr, last, cs] += dkc[0:BAND]
                            dv_acc[prv, rr, last, cs] += dvc[0:BAND]
                            dk_acc[cur, rr, 0:BAND, cs] += dkc[BAND:]
                            dv_acc[cur, rr, 0:BAND, cs] += dvc[BAND:]
                        else:
                            both = slice((rb - 1) * BAND, (rb + 1) * BAND)
                            dk_acc[cur, rr, both, cs] += dkc
                            dv_acc[cur, rr, both, cs] += dvc

        def flush(prv):
            dk_ref[...] = dk_acc[prv].astype(BF16)
            dv_ref[...] = dv_acc[prv].astype(BF16)

        for parity in (0, 1):
            on = (i % 2) == parity
            pl.when(on & (i < nb))(functools.partial(compute, parity, 1 - parity))
            pl.when(on & (i > 0))(functools.partial(flush, 1 - parity))
        if hosted:
            pl.when((j == n_j - 1) & (i == nb))(finish)

    qi = lambda i: jnp.minimum(i, nb - 1)
    cur_w = lambda w: pl.BlockSpec((None, RR, RB * BAND, GROUP_W), lambda j, i: (w, j, qi(i), 0))
    prev_w = lambda w: pl.BlockSpec((None, RR, BAND, GROUP_W),
                                    lambda j, i: (w, j, jnp.maximum(qi(i) * RB - 1, 0), 0))
    blk = pl.BlockSpec((RR, RB * BAND, GROUP_W), lambda j, i: (j, qi(i), 0))
    stat_blk = pl.BlockSpec((RR, RB * BAND, STAT_W), lambda j, i: (j, qi(i), 0))
    late = pl.BlockSpec((RR, RB * BAND, GROUP_W), lambda j, i: (j, jnp.maximum(i - 1, 0), 0))
    extra = [packs] if hosted else []
    return pl.pallas_call(
        body, grid=(n_j, nb + 1), name=f"attn_bwd_g{grp}",
        in_specs=[cur_w(0), cur_w(1), prev_w(1), cur_w(2), prev_w(2), blk, stat_blk, stat_blk] + [ANY] * hosted,
        out_specs=[blk, late, late] + [ANY] * hosted,
        out_shape=[SDS((d, L, GROUP_W), BF16)] * 3 + [SDS(t.shape, t.dtype) for t in extra],
        scratch_shapes=[pltpu.VMEM((2, RR, RB * BAND, GROUP_W), F32), pltpu.VMEM((2, RR, RB * BAND, GROUP_W), F32)]
        + [pltpu.SemaphoreType.DMA((N_DEV - 1,))] * (2 * hosted),
        input_output_aliases={8: 3} if hosted else {},
        compiler_params=_cp("arbitrary" if hosted else "parallel", "arbitrary"),
    )(qkv, qkv, qkv, qkv, qkv, da, lt, dd, *extra)


def _dz_assemble(dqkv, dqp):
    S = dqp.shape[0]
    tm = min(TMZ, S)
    n_tiles = S // tm

    def body(*refs):
        dqkv_refs = refs[0:9]
        dqp_ref, halo_ref = refs[9:11]
        dz_ref, s_ref, ext_ref = refs[11:]
        i = pl.program_id(0)

        for grp in range(3):
            for which in range(3):
                n = which * 3 + grp
                ref = dqkv_refs[grp * 3 + which]
                if DILATIONS[grp] == 1:
                    dz_ref[n] = ref[0]
                else:
                    _interleave_load(ref, (), s_ref, DILATIONS[grp], tm)
                    for h in range(2):
                        dz_ref[n, :, h * PAIR_W:(h + 1) * PAIR_W] = s_ref[h].astype(BF16)

        dqp = dqp_ref[...].astype(F32)
        ext_ref[0:tm, :] = dqp
        ext_ref[tm:, :] = jnp.where(i < n_tiles - 1, halo_ref[...].astype(F32), 0.0)
        sums = []
        acc = ext_ref[...]
        for k in (1, 2, 4, 8):
            acc = acc + pltpu.roll(acc, tm + POOL_HALO - k, 0)
            sums.append(acc[0:tm, :])
        inv_cnt, col = _pool_inv_count(i, tm)
        dpz = _pool_column_select(col, sums) - dqp / inv_cnt
        for t in range(3):
            dz_ref[9 + t] = dpz[:, t * CHUNK:(t + 1) * CHUNK].astype(BF16)

    row = lambda w: pl.BlockSpec((tm, w), lambda i: (i, 0))
    grp_spec = lambda d: pl.BlockSpec((d, tm // d, GROUP_W), lambda i: (0, i, 0))
    halo = pl.BlockSpec((POOL_HALO, POOL_W),
                        lambda i: (jnp.minimum((i + 1) * (tm // POOL_HALO), S // POOL_HALO - 1), 0))
    flat = [t for grp in range(3) for t in dqkv[grp]]
    return pl.pallas_call(
        body, grid=(n_tiles,), name="dz_assemble",
        in_specs=[grp_spec(DILATIONS[grp]) for grp in range(3) for _ in range(3)] + [row(POOL_W), halo],
        out_specs=pl.BlockSpec((N_DZ_CHUNKS, tm, CHUNK), lambda i: (0, i, 0)),
        out_shape=SDS((N_DZ_CHUNKS, S, CHUNK), BF16),
        scratch_shapes=[pltpu.VMEM((2, tm, PAIR_W), F32), pltpu.VMEM((tm + POOL_HALO, POOL_W), F32)],
        compiler_params=_cp("parallel"),
    )(*flat, dqp, dqp)


def _inproj_dx(dz, dgates, dh1, x, g, w_in, sums):
    S = x.shape[0]
    n_tiles = S // TM
    n = len(sums)

    def body(*refs):
        dz_ref, dgate_ref, dh1_ref, x_ref, g_ref, w_ref = refs[0:6]
        sum_refs = refs[6:6 + n]
        dx_ref, dg_ref = refs[6 + n:8 + n]
        land_refs = refs[8 + n:8 + 2 * n]
        sems = refs[8 + 2 * n:]
        i = pl.program_id(0)

        def copies():
            return _chip_sum_copies(sum_refs, land_refs, *sems)

        @pl.when(i == 0)
        def _():
            dg_ref[...] = jnp.zeros_like(dg_ref)
            for cpy in copies():
                cpy.start()

        du = jnp.zeros((TM, D_MODEL), F32)
        for k in range(N_CHUNKS):
            dzk = dz_ref[k] if k < N_DZ_CHUNKS else dgate_ref[k - N_DZ_CHUNKS]
            du = du + _dot_nt(dzk, _w_in_chunk(w_ref, k))
        gv = g_ref[...]
        _, xh, r = _rms_fwd(x_ref[...], gv)
        dg_ref[...] += jnp.sum(du * xh, axis=0, keepdims=True)
        dx_ref[...] = dh1_ref[...] + _rms_bwd(du, xh, r, gv)

        @pl.when(i == n_tiles - 1)
        def _():
            for cpy in copies():
                cpy.wait()

    row = lambda w: pl.BlockSpec((TM, w), lambda i: (i, 0))
    res = pl.pallas_call(
        body, grid=(n_tiles,), name="inproj_dx",
        in_specs=[pl.BlockSpec((N_DZ_CHUNKS, TM, CHUNK), lambda i: (0, i, 0)),
                  pl.BlockSpec((N_CHUNKS - N_DZ_CHUNKS, TM, CHUNK), lambda i: (0, i, 0)),
                  row(D_MODEL), row(D_MODEL), _resident(g.shape), _resident(w_in.shape)] + [ANY] * n,
        out_specs=[row(D_MODEL), pl.BlockSpec((1, D_MODEL), lambda i: (0, 0))] + [ANY] * n,
        out_shape=[SDS((S, D_MODEL), F32), SDS((1, D_MODEL), F32)] + [SDS(t.shape, t.dtype) for t in sums],
        scratch_shapes=[pltpu.SemaphoreType.DMA((3 * n,)), pltpu.SemaphoreType.DMA((3 * n,))],
        compiler_params=_cp("arbitrary"),
    )(dz, dgates, dh1, x, g, w_in, *sums)
    return res[0], res[1], res[2:]


def _wgrad(a, b, name, *, out_shape, a_spec, b_spec, out_spec, grid, n_out_cols=None, fill=None, narrow=True):
    k_axis = len(grid) - 1
    n_k = grid[k_axis]
    n_out = 2 if narrow else 1

    def body(a_ref, b_ref, *rest):
        o_ref = rest[-n_out]

        @pl.when(pl.program_id(k_axis) == 0)
        def _():
            o_ref[...] = jnp.zeros_like(o_ref)

        at = a_ref[...]
        if n_out_cols is None:
            o_ref[...] += _dot_tn(at, b_ref[...])
        elif n_out_cols[0] == "lead_both":
            for t in range(b_ref.shape[0]):
                o_ref[t] += _dot_tn(at, b_ref[t])
        else:
            w = n_out_cols[1]
            for t in range(o_ref.shape[0]):
                o_ref[t] += _dot_tn(at, b_ref[:, t * w:(t + 1) * w])

        if narrow:
            @pl.when(pl.program_id(k_axis) == n_k - 1)
            def _():
                rest[-1][...] = o_ref[...].astype(BF16)

    sem = ("parallel",) * k_axis + ("arbitrary",)
    extra = [] if fill is None else list(fill) if narrow else [fill]
    shapes = [out_shape, SDS(out_shape.shape, BF16)] if narrow else out_shape
    return pl.pallas_call(body, grid=grid, name=name, in_specs=[a_spec, b_spec] + [ANY] * len(extra),
                          out_specs=[out_spec] * n_out if narrow else out_spec, out_shape=shapes,
                          input_output_aliases={2 + t: t for t in range(len(extra))},
                          compiler_params=_cp(*sem))(a, b, *extra)


def _wgrad_in(u, dz, dgates):
    bk = min(BK, u.shape[0])
    nk = u.shape[0] // bk
    g = WGRAD_IN_GROUP
    kw = dict(n_out_cols=("lead_both", CHUNK), a_spec=pl.BlockSpec((bk, D_MODEL), lambda j, k: (k, 0)),
              b_spec=pl.BlockSpec((g, bk, CHUNK), lambda j, k: (j, k, 0)),
              out_shape=SDS((N_CHUNKS, D_MODEL, CHUNK), F32))
    first = _wgrad(u, dz, "wgrad_in_qkvp", grid=(N_DZ_CHUNKS // g, nk),
                   out_spec=pl.BlockSpec((g, D_MODEL, CHUNK), lambda j, k: (j, 0, 0)), **kw)
    both = _wgrad(u, dgates, "wgrad_in_gates", grid=((N_CHUNKS - N_DZ_CHUNKS) // g, nk), fill=first,
                  out_spec=pl.BlockSpec((g, D_MODEL, CHUNK), lambda j, k: (N_DZ_CHUNKS // g + j, 0, 0)), **kw)
    return [t.reshape(N_CHIPS, CHUNKS_PER_SHARD * D_MODEL, CHUNK) for t in both]


def _wgrads_mixer(a, da1, p, dp1, merged, dh1b, pooled, dmixed):
    bk = min(BK, a.shape[0])
    nk = a.shape[0] // bk
    g_ao = _wgrad(
        a, da1, "wgrad_att_out", grid=(nk,), n_out_cols=("cols_b", CHUNK),
        a_spec=pl.BlockSpec((bk, GROUP_W), lambda k: (k, 0)),
        b_spec=pl.BlockSpec((bk, D_MODEL), lambda k: (k, 0)),
        out_spec=pl.BlockSpec((N_CHIPS, GROUP_W, CHUNK), lambda k: (0, 0, 0)),
        out_shape=SDS((N_CHIPS, GROUP_W, CHUNK), F32))
    g_po = _wgrad(
        p, dp1, "wgrad_pool_out", grid=(nk,), n_out_cols=("cols_b", CHUNK),
        a_spec=pl.BlockSpec((bk, POOL_W), lambda k: (k, 0)),
        b_spec=pl.BlockSpec((bk, D_MODEL), lambda k: (k, 0)),
        out_spec=pl.BlockSpec((N_CHIPS, POOL_W, CHUNK), lambda k: (0, 0, 0)),
        out_shape=SDS((N_CHIPS, POOL_W, CHUNK), F32))
    g_out = _wgrad(
        merged, dh1b, "wgrad_out", grid=(nk,),
        a_spec=pl.BlockSpec((bk, D_MODEL), lambda k: (k, 0)),
        b_spec=pl.BlockSpec((bk, D_MODEL), lambda k: (k, 0)),
        out_spec=pl.BlockSpec((D_MODEL, D_MODEL), lambda k: (0, 0)),
        out_shape=SDS((D_MODEL, D_MODEL), F32))
    g_bd = _wgrad(
        pooled, dmixed, "wgrad_pool_grp", grid=(nk,),
        a_spec=pl.BlockSpec((bk, POOL_W), lambda k: (k, 0)),
        b_spec=pl.BlockSpec((bk, POOL_W), lambda k: (k, 0)),
        out_spec=pl.BlockSpec((POOL_W, POOL_W), lambda k: (0, 0)),
        out_shape=SDS((POOL_W, POOL_W), F32), narrow=False)
    g_out = [t.reshape(N_CHIPS, D_MODEL // N_CHIPS, D_MODEL) for t in g_out]
    return [g_ao, g_po, g_out], g_bd


def _wgrads_mlp(m, dpre, hid, dh2b):
    bk = min(BK, m.shape[0])
    nk = m.shape[0] // bk
    g_mi = _wgrad(
        m, dpre, "wgrad_mlp_in", grid=(N_CHIPS, nk),
        a_spec=pl.BlockSpec((bk, D_MODEL), lambda c, k: (k, 0)),
        b_spec=pl.BlockSpec((bk, D_MODEL), lambda c, k: (k, c)),
        out_spec=pl.BlockSpec((None, D_MODEL, D_MODEL), lambda c, k: (c, 0, 0)),
        out_shape=SDS((N_CHIPS, D_MODEL, D_MODEL), F32))
    g_mo = _wgrad(
        hid, dh2b, "wgrad_mlp_out", grid=(N_CHIPS, nk),
        a_spec=pl.BlockSpec((bk, D_MODEL), lambda c, k: (k, c)),
        b_spec=pl.BlockSpec((bk, D_MODEL), lambda c, k: (k, 0)),
        out_spec=pl.BlockSpec((None, D_MODEL, D_MODEL), lambda c, k: (c, 0, 0)),
        out_shape=SDS((N_CHIPS, D_MODEL, D_MODEL), F32))
    return [g_mi, g_mo]


def _mesh_place():
    x, y, c = lax.axis_index("x"), lax.axis_index("y"), lax.axis_index("c")
    other_chips = [(x, 1 - y), (1 - x, y), (1 - x, 1 - y)]
    return x, y, c, other_chips


ANY = pl.BlockSpec(memory_space=pl.ANY)


def _weight_half_copies(shard_refs, buf_refs, rows, send_sem, recv_sem):
    x, y, c, chips = _mesh_place()
    me = 2 * x + y
    copies = []
    for w, r_full in enumerate(rows):
        rh = r_full // 2
        for r, (px, py) in enumerate(chips):
            k = w * 3 + r
            copies.append(pltpu.make_async_remote_copy(
                src_ref=shard_refs[w].at[pl.ds(c * rh, rh), :], dst_ref=buf_refs[w].at[me, pl.ds(c * rh, rh), :],
                send_sem=send_sem.at[k], recv_sem=recv_sem.at[k], device_id=(px, py, c), device_id_type=MESH))
    return copies


def _pair_forward_copies(buf_refs, rows, send_sem, recv_sem):
    x, y, c, chips = _mesh_place()
    out = []
    for w, r_full in enumerate(rows):
        rh = r_full // 2
        for r, (px, py) in enumerate(chips):
            k = w * 3 + r
            landed = buf_refs[w].at[2 * px + py, pl.ds(c * rh, rh), :]
            theirs = buf_refs[w].at[2 * px + py, pl.ds((1 - c) * rh, rh), :]
            mk = lambda ref: pltpu.make_async_remote_copy(
                src_ref=ref, dst_ref=ref, send_sem=send_sem.at[k], recv_sem=recv_sem.at[k],
                device_id=(x, y, 1 - c), device_id_type=MESH)
            out.append((mk(landed), mk(theirs)))
    return out


def _place_own(block, n_slots, slot):
    buf = lax.empty((n_slots,) + block.shape, block.dtype)
    return lax.dynamic_update_slice(buf, block[None], (slot,) + (0,) * block.ndim)


def _pair_forward(bufs, rows, name):
    n = len(bufs)

    def body(*refs):
        dst = refs[n:2 * n]
        send_sem, recv_sem = refs[2 * n:]
        fwds = _pair_forward_copies(dst, rows, send_sem, recv_sem)
        for fwd, _ in fwds:
            fwd.start()
        for fwd, landing in fwds:
            landing.wait_recv()
            fwd.wait_send()

    return pl.pallas_call(
        body, name=name,
        in_specs=[ANY] * n, out_specs=[ANY] * n,
        out_shape=[SDS(b.shape, b.dtype) for b in bufs],
        scratch_shapes=[pltpu.SemaphoreType.DMA((3 * n,))] * 2,
        input_output_aliases={w: w for w in range(n)},
    )(*bufs)


def _chip_sum_copies(src, dst, send_sem, recv_sem):
    x, y, c, chips = _mesh_place()
    copies = []
    for w in range(len(src)):
        for r, (px, py) in enumerate(chips):
            k = w * 3 + r
            copies.append(pltpu.make_async_remote_copy(
                src_ref=src[w].at[r + 1], dst_ref=dst[w].at[r + 1], send_sem=send_sem.at[k], recv_sem=recv_sem.at[k],
                device_id=(px, py, c), device_id_type=MESH))
    return copies


def _pair_exchange(grads):
    n = len(grads)

    def body(*refs):
        src, dst = refs[:n], refs[n:2 * n]
        send_sem, recv_sem = refs[2 * n:]
        x, y, c, _ = _mesh_place()
        copies = []
        for w in range(n):
            rh = grads[w].shape[1] // 2
            copies.append(pltpu.make_async_remote_copy(
                src_ref=src[w].at[:, pl.ds((1 - c) * rh, rh), :], dst_ref=dst[w],
                send_sem=send_sem.at[w], recv_sem=recv_sem.at[w],
                device_id=(x, y, 1 - c), device_id_type=MESH))
            copies[-1].start()
        for cpy in copies:
            cpy.wait()

    return pl.pallas_call(
        body, name="grad_pair_exchange",
        in_specs=[ANY] * n, out_specs=[ANY] * n,
        out_shape=[SDS((N_CHIPS, g.shape[1] // 2, g.shape[2]), g.dtype) for g in grads],
        scratch_shapes=[pltpu.SemaphoreType.DMA((n,)), pltpu.SemaphoreType.DMA((n,))],
    )(*grads)


def _pair_sum(place, grad, recv, name):
    _, R, C = grad.shape
    rh = R // 2
    br = _row_block(rh, max(256, ELEMENTWISE_BLOCK // C))
    nbh = rh // br

    def body(place_ref, g_ref, r_ref, own_ref, sums_ref):
        s = g_ref[...] + r_ref[...].astype(F32)

        @pl.when(pl.program_id(1) == 0)
        def _():
            own_ref[...] = s

        sums_ref[...] = s.astype(BF16)

    slot = lambda rel, pr: jnp.bitwise_xor(pr[0], rel)
    return pl.pallas_call(
        body, name=name,
        grid_spec=pltpu.PrefetchScalarGridSpec(
            num_scalar_prefetch=1, grid=(nbh, N_CHIPS),
            in_specs=[pl.BlockSpec((None, br, C), lambda i, rel, pr: (slot(rel, pr), pr[1] * nbh + i, 0)),
                      pl.BlockSpec((None, br, C), lambda i, rel, pr: (slot(rel, pr), i, 0))],
            out_specs=[pl.BlockSpec((br, C), lambda i, rel, pr: (i, 0)),
                       pl.BlockSpec((None, br, C), lambda i, rel, pr: (rel, i, 0))]),
        out_shape=[SDS((rh, C), F32), SDS((N_CHIPS, rh, C), BF16)],
        compiler_params=_cp("parallel", "arbitrary"),
    )(place, grad, recv)


def _chip_sum(place, own, recv, name):
    rh, C = own.shape
    br = _row_block(rh, max(256, ELEMENTWISE_BLOCK // C))
    nbh = rh // br

    def body(place_ref, own_ref, r_ref, o_ref):
        o_ref[...] = ((own_ref[...] + r_ref[1].astype(F32)) + r_ref[2].astype(F32)) + r_ref[3].astype(F32)

    return pl.pallas_call(
        body, name=name,
        grid_spec=pltpu.PrefetchScalarGridSpec(
            num_scalar_prefetch=1, grid=(nbh,),
            in_specs=[pl.BlockSpec((br, C), lambda i, pr: (i, 0)),
                      pl.BlockSpec((N_CHIPS, br, C), lambda i, pr: (0, i, 0))],
            out_specs=pl.BlockSpec((br, C), lambda i, pr: (pr[1] * nbh + i, 0))),
        out_shape=SDS((2 * rh, C), F32),
        compiler_params=_cp("parallel"),
    )(place, own, recv)


def _pack_allgather(all_ref, send_sem, recv_sem):
    x, y, c, chips = _mesh_place()
    sib = (x, y, 1 - c)

    def pack(dev, k, to):
        slot = 4 * dev[0] + 2 * dev[1] + dev[2]
        return pltpu.make_async_remote_copy(
            src_ref=all_ref.at[slot], dst_ref=all_ref.at[slot], send_sem=send_sem.at[k],
            recv_sem=recv_sem.at[k], device_id=to, device_id_type=MESH)

    first = [pack((x, y, c), 0, sib)] + [pack((x, y, c), 1 + r, (px, py, c)) for r, (px, py) in enumerate(chips)]
    relays = [pack((px, py, c), 4 + r, sib) for r, (px, py) in enumerate(chips)]

    def start():
        for cpy in first:
            cpy.start()

    def relay():
        for r, (px, py) in enumerate(chips):
            pack((px, py, c), 1 + r, (px, py, c)).wait_recv()
            relays[r].start()

    def finish():
        pack(sib, 0, sib).wait_recv()
        for r, (px, py) in enumerate(chips):
            pack((px, py, 1 - c), 4 + r, sib).wait_recv()
        for cpy in first + relays:
            cpy.wait_send()

    return start, relay, finish


def _finish_exchange(grads, late_all):
    n = len(grads)

    def body(*refs):
        dst, all_ref = refs[n + 1:2 * n + 1], refs[2 * n + 1]
        send_sem, recv_sem, ssend_sem, srecv_sem = refs[2 * n + 2:]
        x, y, c, _ = _mesh_place()
        start, relay, finish = _pack_allgather(all_ref, ssend_sem, srecv_sem)
        start()
        sends, landings = [], []
        for w in range(n):
            rh = grads[w].shape[0] // 2
            mk = lambda cc: pltpu.make_async_remote_copy(
                src_ref=dst[w].at[pl.ds(cc * rh, rh), :], dst_ref=dst[w].at[pl.ds(cc * rh, rh), :],
                send_sem=send_sem.at[w], recv_sem=recv_sem.at[w], device_id=(x, y, 1 - c), device_id_type=MESH)
            sends.append(mk(c))
            landings.append(mk(1 - c))
            sends[-1].start()
        relay()
        finish()
        for cpy in landings:
            cpy.wait_recv()
        for cpy in sends:
            cpy.wait_send()

    res = pl.pallas_call(
        body, name="grad_finish_exchange",
        in_specs=[ANY] * (n + 1), out_specs=[ANY] * (n + 1),
        out_shape=[SDS(g.shape, g.dtype) for g in grads] + [SDS(late_all.shape, late_all.dtype)],
        scratch_shapes=[pltpu.SemaphoreType.DMA((n,)), pltpu.SemaphoreType.DMA((n,)),
                        pltpu.SemaphoreType.DMA((N_DEV - 1,)), pltpu.SemaphoreType.DMA((N_DEV - 1,))],
        input_output_aliases={w: w for w in range(n + 1)},
    )(*grads, late_all)
    return res[:n], res[n]


def _adamw_math(w, g, m, v):
    m = ADAM_B1 * m + (1.0 - ADAM_B1) * g
    v = ADAM_B2 * v + (1.0 - ADAM_B2) * jnp.square(g)
    m_hat = m / (1.0 - ADAM_B1 ** ADAM_STEP)
    v_hat = v / (1.0 - ADAM_B2 ** ADAM_STEP)
    delta = -ADAM_LR * (m_hat / (jnp.sqrt(v_hat) + ADAM_EPS) + ADAM_WD * w)
    return delta, m, v


def _adamw(w, g, m, v, name):
    R, C = w.shape
    br = _row_block(R, 512)
    if g.ndim == 3:
        n_chunks, cw = g.shape[0], g.shape[2]
        g_spec = pl.BlockSpec((None, br, cw), lambda t, i: (t, i, 0))
    else:
        n_chunks, cw = 1, C
        g_spec = pl.BlockSpec((br, cw), lambda t, i: (i, t))

    def body(w_ref, g_ref, m_ref, v_ref, g_out_ref, d_ref, nm_ref, nv_ref):
        gv = g_ref[...]
        g_out_ref[...] = gv
        d_ref[...], nm_ref[...], nv_ref[...] = _adamw_math(w_ref[...], gv, m_ref[...], v_ref[...])

    spec = pl.BlockSpec((br, cw), lambda t, i: (i, t))
    return pl.pallas_call(
        body, grid=(n_chunks, R // br), name=name, in_specs=[spec, g_spec, spec, spec], out_specs=[spec] * 4,
        out_shape=[SDS((R, C), F32)] * 4, compiler_params=_cp("parallel", "parallel"),
    )(w, g, m, v)


def _small_sum_adamw(all_small, all_late, w, m, v):
    loss_row = PACK_ROWS - 8

    def body(all_ref, late_ref, w_ref, m_ref, v_ref, g_ref, d_ref, nm_ref, nv_ref, loss_ref):
        g = all_ref[0]
        late = late_ref[0]
        for k in range(1, N_DEV):
            g = g + all_ref[k]
            late = late + late_ref[k]
        g_ref[...] = g
        g_ref[PACK_LATE_ROW:PACK_LATE_ROW + 8, :] = late
        g = g_ref[...]
        d_ref[...], nm_ref[...], nv_ref[...] = _adamw_math(w_ref[...], g, m_ref[...], v_ref[...])
        total = jnp.sum(g[loss_row:loss_row + 1, :]) * (0.5 / D_MODEL)
        loss_ref[...] = jnp.full(loss_ref.shape, total, F32)

    full = lambda s: pl.BlockSpec(s, lambda i: (0,) * len(s))
    pack = (PACK_ROWS, D_MODEL)
    return pl.pallas_call(
        body, grid=(1,), name="small_sum_adamw",
        in_specs=[full((N_DEV,) + pack), full((N_DEV, 8, D_MODEL)), full(pack), full(pack), full(pack)],
        out_specs=[full(pack)] * 4 + [full((8, 128))],
        out_shape=[SDS(pack, F32)] * 4 + [SDS((8, 128), F32)],
        compiler_params=_cp("arbitrary"),
    )(all_small, all_late, w, m, v)


def _pack_small(grp, scale, g_mix, g_mlp, g_f, loss_lanes):
    def part(vec):
        vec = vec.reshape(1, -1)
        return jnp.pad(vec, ((0, 7), (0, D_MODEL - vec.shape[1])))
    return jnp.concatenate([grp.reshape(-1, D_MODEL), part(scale), part(g_mix), part(g_mlp), part(g_f),
                            part(loss_lanes)], axis=0)


def _unpack_small(pack):
    n_grp = len(POOL_WINDOWS) * POOL_GROUP_W * POOL_GROUP_W // D_MODEL
    grp = pack[:n_grp].reshape(1, len(POOL_WINDOWS), POOL_GROUP_W, POOL_GROUP_W)
    scale = pack[n_grp, :POOL_W].reshape(1, POOL_W)
    g_mix = pack[n_grp + 8].reshape(1, D_MODEL)
    g_mlp = pack[n_grp + 16].reshape(1, D_MODEL)
    g_f = pack[n_grp + 24].reshape(D_MODEL)
    return grp, scale, g_mix, g_mlp, g_f


def _block_diag(grp):
    out = jnp.zeros((POOL_W, POOL_W), grp.dtype)
    for k in range(len(POOL_WINDOWS)):
        out = lax.dynamic_update_slice(out, grp[k], (k * POOL_GROUP_W, k * POOL_GROUP_W))
    return out


def kernel(x, norm_mix_g, w_in, w_att_out, w_pool_grp, pool_scale, w_pool_out, w_out, norm_mlp_g, w_mlp_in, w_mlp_out, norm_final_g, loss_target, m_norm_mix_g, m_w_in, m_w_att_out, m_w_pool_grp, m_pool_scale, m_w_pool_out, m_w_out, m_norm_mlp_g, m_w_mlp_in, m_w_mlp_out, m_norm_final_g, v_norm_mix_g, v_w_in, v_w_att_out, v_w_pool_grp, v_pool_scale, v_w_pool_out, v_w_out, v_norm_mlp_g, v_w_mlp_in, v_w_mlp_out, v_norm_final_g):
    S = x.shape[1]
    xs, target = x[0], loss_target[0]
    big = [w_in[0], w_att_out[0], w_pool_out[0], w_out[0], w_mlp_in[0], w_mlp_out[0]]
    big_m = [m_w_in[0], m_w_att_out[0], m_w_pool_out[0], m_w_out[0], m_w_mlp_in[0], m_w_mlp_out[0]]
    big_v = [v_w_in[0], v_w_att_out[0], v_w_pool_out[0], v_w_out[0], v_w_mlp_in[0], v_w_mlp_out[0]]

    chip = 2 * lax.axis_index("x") + lax.axis_index("y")
    core = lax.axis_index("c")
    place = jnp.stack([chip, core]).astype(jnp.int32)
    names = ("w_in", "w_att_out", "w_pool_out", "w_out", "w_mlp_in", "w_mlp_out")

    shards = [w.astype(BF16) for w in big]
    bufs = [_place_own(sh, N_CHIPS, chip) for sh in shards]
    wbd = _block_diag(w_pool_grp[0]).astype(BF16)
    g_final = norm_final_g.reshape(1, D_MODEL)
    stat_expand, stat_reduce = _stat_matrices()

    u, z_own, landed_in = _norm_inproj_own(xs, norm_mix_g, shards[0], bufs[0])
    (wg_in,) = _pair_forward([landed_in], [shards[0].shape[0]], "w_in_pair_forward")
    (qkv0, qkv1, qkv2, pz, gates), (wg_ao, wg_po, wg_out) = _inproj_rest(u, z_own, wg_in, shards[1:4], bufs[1:4])
    wg_out = wg_out.reshape(D_MODEL, D_MODEL)
    qkv = (qkv0, qkv1, qkv2)
    att = [_attn_fwd(qkv[grp], grp) for grp in range(3)]
    (a, lt0, lt1, lt2, pooled, mixed, p, merged, h1, m), (wg_mi, wg_mo) = _mixer_out(
        [o for o, _ in att], [l for _, l in att], pz, gates, xs, wg_ao, wg_po, wbd, pool_scale, wg_out, norm_mlp_g,
        stat_expand, shards[4:], bufs[4:])
    hid, dh2, dh2b, loss_lanes, dg_final = _mlp_fwd_loss(m, h1, target, wg_mi, wg_mo, g_final)

    def pair_reduce(grads, grad_names):
        recv = _pair_exchange([narrow for _, narrow in grads])
        pair = [_pair_sum(place, g, r, f"pair_sum_{nm}") for (g, _), r, nm in zip(grads, recv, grad_names)]
        return [own for own, _ in pair], [s for _, s in pair]

    def chip_reduce(owns, landed_sums, grad_names):
        return [_chip_sum(place, own, r, f"chip_sum_{nm}") for own, r, nm in zip(owns, landed_sums, grad_names)]

    dpre, dh1, dh1b, dg_mlp = _mlp_bwd(dh2, dh2b, hid, h1, wg_mi, wg_mo, norm_mlp_g)
    own_mlp, sums_mlp = pair_reduce(_wgrads_mlp(m, dpre, hid, dh2b), names[4:])
    (da1, dp1, dgates, da0, dag1, dag2, dd0, dd1, dd2, dmixed, dqp, dscale), landed_mlp = _mixer_bwd(
        dh1b, a, p, mixed, gates, wg_out, wg_ao, wg_po, wbd, pool_scale, stat_reduce, sums_mlp)
    g_mi, g_mo = chip_reduce(own_mlp, landed_mlp, names[4:])
    grads_mixer, g_bd = _wgrads_mixer(a, da1, p, dp1, merged, dh1b, pooled, dmixed)

    zero = jnp.zeros((D_MODEL,), F32)
    g_grp = jnp.stack([g_bd[k * POOL_GROUP_W:(k + 1) * POOL_GROUP_W, k * POOL_GROUP_W:(k + 1) * POOL_GROUP_W]
                       for k in range(len(POOL_WINDOWS))])
    small = _pack_small(g_grp, dscale, zero, dg_mlp, dg_final, loss_lanes)
    *dqkv0, small_all = _attn_bwd(qkv[0], da0, lt0, dd0, 0, packs=_place_own(small, N_DEV, 2 * chip + core))
    dqkv = [dqkv0, _attn_bwd(qkv[1], dag1, lt1, dd1, 1), _attn_bwd(qkv[2], dag2, lt2, dd2, 2)]
    dz = _dz_assemble(dqkv, dqp)
    own_in, sums_in = pair_reduce([_wgrad_in(u, dz, dgates)] + grads_mixer, names[:4])
    dx, dg_mix, landed_in = _inproj_dx(dz, dgates, dh1, xs, norm_mix_g, wg_in, sums_in)
    g_in, g_ao, g_po, g_out = chip_reduce(own_in, landed_in, names[:4])
    late = jnp.pad(dg_mix, ((0, 7), (0, 0)))
    full, late_all = _finish_exchange([g_in, g_ao, g_po, g_out, g_mi, g_mo], _place_own(late, N_DEV, 2 * chip + core))

    small_w = _pack_small(w_pool_grp[0], pool_scale, norm_mix_g, norm_mlp_g, norm_final_g, zero)
    small_m = _pack_small(m_w_pool_grp[0], m_pool_scale, m_norm_mix_g, m_norm_mlp_g, m_norm_final_g, zero)
    small_v = _pack_small(v_w_pool_grp[0], v_pool_scale, v_norm_mix_g, v_norm_mlp_g, v_norm_final_g, zero)
    sg, sd, sm, sv, loss_tile = _small_sum_adamw(small_all, late_all, small_w, small_m, small_v)
    full = [full[0].reshape(CHUNKS_PER_SHARD, D_MODEL, CHUNK)] + list(full[1:])
    upd = [_adamw(w, g, mm, vv, f"adamw_{nm}") for w, g, mm, vv, nm in zip(big, full, big_m, big_v, names)]

    def ordered(small_pack, bigs):
        grp, scale, g_mix, g_mlp, g_f = _unpack_small(small_pack)
        b_in, b_ao, b_po, b_out, b_mi, b_mo = [b[None] for b in bigs]
        return (g_mix, b_in, b_ao, grp, scale, b_po, b_out, g_mlp, b_mi, b_mo, g_f)

    return (loss_tile[0, 0], dx[None],
            *ordered(sg, [t[0] for t in upd]),
            *ordered(sd, [t[1] for t in upd]),
            *ordered(sm, [t[2] for t in upd]),
            *ordered(sv, [t[3] for t in upd]))
```

```python
import functools

import jax
import jax.numpy as jnp
from jax import lax
from jax.experimental import pallas as pl
from jax.experimental.pallas import tpu as pltpu

F32 = jnp.float32
BF16 = jnp.bfloat16
SDS = jax.ShapeDtypeStruct
MESH = pl.DeviceIdType.MESH

D_MODEL = 1024
D_FF = 4096
N_CHIPS = 4
N_DEV = 8
DILATIONS = (1, 4, 16)
BAND = 128
GROUP_W = 256
PAIR_W = 128
HEAD_W = 64
STAT_W = 128
STAT_HEAD_W = 32
POOL_W = 768
POOL_GROUP_W = 192
POOL_WINDOWS = (2, 4, 8, 16)
POOL_HALO = 16
N_IN = 5120
CHUNK = 256
N_CHUNKS = N_IN // CHUNK
N_DZ_CHUNKS = 12
CHUNKS_PER_SHARD = 5
WGRAD_IN_GROUP = 4
NORM_EPS = 1e-6
ALIBI_MAX_BIAS = 8.0
N_HEADS = 12
NEG = -1e30

ADAM_LR, ADAM_B1, ADAM_B2, ADAM_EPS, ADAM_WD, ADAM_STEP = 0.001, 0.9, 0.999, 1e-08, 0.01, 10

TM = 512
TMB = 512
TMZ = 1024
ATT_TILE = ((1, 16), (4, 4), (16, 1))
BK = 4096
ELEMENTWISE_BLOCK = 1 << 20
VMEM_LIMIT = 56 * 1024 * 1024
PACK_ROWS = 184
PACK_LATE_ROW = 152

NT = (((1,), (1,)), ((), ()))
TN = (((0,), (0,)), ((), ()))


def _cp(*sem):
    return pltpu.CompilerParams(dimension_semantics=sem, vmem_limit_bytes=VMEM_LIMIT)


def _resident(shape):
    nd = len(shape)
    return pl.BlockSpec(shape, lambda *_: (0,) * nd, pipeline_mode=pl.Buffered(1))


def _row_block(rows, cap=256):
    return max(b for b in range(16, min(rows, cap) + 1, 16) if rows % b == 0)


def _dot(a, b):
    return jnp.dot(a, b, preferred_element_type=F32)


def _dot_nt(a, b):
    return lax.dot_general(a, b, NT, preferred_element_type=F32)


def _dot_tn(a, b):
    return lax.dot_general(a, b, TN, preferred_element_type=F32)


def _w_in_chunk(w_ref, n):
    return w_ref[n // CHUNKS_PER_SHARD, :, (n % CHUNKS_PER_SHARD) * CHUNK:(n % CHUNKS_PER_SHARD + 1) * CHUNK]


def _sigmoid(x):
    return 0.5 * jnp.tanh(0.5 * x.astype(F32)) + 0.5


def _rms_fwd(x, g):
    r = lax.rsqrt(jnp.mean(x * x, axis=-1, keepdims=True) + NORM_EPS)
    xh = x * r
    return xh * g, xh, r


def _rms_bwd(dy, xh, r, g):
    dxh = dy * g
    return r * (dxh - xh * jnp.mean(dxh * xh, axis=-1, keepdims=True))


def _per_head_lanes(cols):
    rows = cols[0].shape[0]
    lane = lax.broadcasted_iota(jnp.int32, (rows, STAT_W), 1)
    out = cols[3]
    for h in (2, 1, 0):
        out = jnp.where(lane < (h + 1) * STAT_HEAD_W, cols[h], out)
    return out


def _head_col(stat, h):
    return stat[:, h * STAT_HEAD_W:h * STAT_HEAD_W + 1]


def _stat_matrices():
    s = lax.broadcasted_iota(jnp.int32, (STAT_W, GROUP_W), 0)
    c = lax.broadcasted_iota(jnp.int32, (STAT_W, GROUP_W), 1)
    expand = (s == (c // HEAD_W) * STAT_HEAD_W).astype(BF16)
    reduce = (s // STAT_HEAD_W == c // HEAD_W).astype(BF16).T
    return expand, reduce


def _dot_split(x, m):
    hi = x.astype(BF16)
    lo = (x - hi.astype(F32)).astype(BF16)
    return _dot(hi, m) + _dot(lo, m)


def _deinterleave_store(val, s_ref, out_ref, lead, d, rows, dtype):
    if d == 1:
        out_ref[lead + (0,)] = val.astype(dtype)
        return
    for h in range(2):
        s_ref[h] = val[:, h * PAIR_W:(h + 1) * PAIR_W]
    for r in range(d):
        for h in range(2):
            out_ref[lead + (r, slice(None), slice(h * PAIR_W, (h + 1) * PAIR_W))] = (
                s_ref[h, pl.ds(r, rows // d, stride=d), :].astype(dtype))


def _interleave_load(in_ref, lead, s_ref, d, rows):
    for r in range(d):
        for h in range(2):
            s_ref[h, pl.ds(r, rows // d, stride=d), :] = (
                in_ref[lead + (r, slice(None), slice(h * PAIR_W, (h + 1) * PAIR_W))].astype(F32))


def _norm_inproj_own(x, g, w_own, buf):
    S = x.shape[0]
    tm = min(TMZ, S)
    n_tiles = S // tm

    def body(x_ref, g_ref, w_ref, shard_ref, buf_in, u_ref, z_ref, buf_ref, send_sem, recv_sem):
        i = pl.program_id(0)

        def copies():
            return _weight_half_copies([shard_ref], [buf_ref], [w_own.shape[0]], send_sem, recv_sem)

        @pl.when(i == 0)
        def _():
            for cpy in copies():
                cpy.start()

        u = _rms_fwd(x_ref[...], g_ref[...])[0].astype(BF16)
        u_ref[...] = u
        for t in range(CHUNKS_PER_SHARD):
            z_ref[t] = _dot(u, w_ref[:, t * CHUNK:(t + 1) * CHUNK]).astype(BF16)

        @pl.when(i == n_tiles - 1)
        def _():
            for cpy in copies():
                cpy.wait()

    row = lambda w: pl.BlockSpec((tm, w), lambda i: (i, 0))
    return pl.pallas_call(
        body, grid=(n_tiles,), name="norm_inproj_own",
        in_specs=[row(D_MODEL), _resident((1, D_MODEL)), _resident(w_own.shape), ANY, ANY],
        out_specs=[row(D_MODEL), pl.BlockSpec((CHUNKS_PER_SHARD, tm, CHUNK), lambda i: (0, i, 0)), ANY],
        out_shape=[SDS((S, D_MODEL), BF16), SDS((CHUNKS_PER_SHARD, S, CHUNK), BF16), SDS(buf.shape, buf.dtype)],
        scratch_shapes=[pltpu.SemaphoreType.DMA((3,)), pltpu.SemaphoreType.DMA((3,))],
        input_output_aliases={4: 2},
        compiler_params=_cp("arbitrary"),
    )(x, g, w_own, w_own, buf)


def _hosted_allgather(i, n_steps, shard_refs, buf_refs, rows, sems):
    send_sem, recv_sem, fsend_sem, frecv_sem = sems
    ici = lambda: _weight_half_copies(shard_refs, buf_refs, rows, send_sem, recv_sem)
    forward = lambda: _pair_forward_copies(buf_refs, rows, fsend_sem, frecv_sem)

    def begin():
        @pl.when(i == 0)
        def _():
            for cpy in ici():
                cpy.start()

        @pl.when(i == (3 * n_steps) // 4)
        def _():
            for cpy, (fwd, _) in zip(ici(), forward()):
                cpy.wait_recv()
                fwd.start()

    def end():
        @pl.when(i == n_steps - 1)
        def _():
            for cpy, (fwd, landing) in zip(ici(), forward()):
                landing.wait_recv()
                fwd.wait_send()
                cpy.wait_send()

    return begin, end


def _inproj_rest(u, z_own, w_in, shards, bufs):
    S = u.shape[0]
    n_tiles = S // TM
    n = len(shards)

    def body(*refs):
        u_ref, zown_ref, w_ref = refs[0:3]
        shard_refs = refs[3:3 + n]
        q0_ref, q1_ref, q2_ref, pz_ref, gate_ref = refs[3 + 2 * n:8 + 2 * n]
        buf_refs = refs[8 + 2 * n:8 + 3 * n]
        s_ref = refs[8 + 3 * n]
        i = pl.program_id(0)
        chip = 2 * lax.axis_index("x") + lax.axis_index("y")
        begin, end = _hosted_allgather(i, n_tiles, shard_refs, buf_refs, [sh.shape[0] for sh in shards],
                                       refs[9 + 3 * n:])
        begin()

        u = u_ref[...]
        qkv_refs = (q0_ref, q1_ref, q2_ref)

        def emit(k, zc):
            if k < 9:
                which, grp = k // 3, k % 3
                if which == 0:
                    zc = zc * 0.125
                _deinterleave_store(zc, s_ref, qkv_refs[grp], (which,), DILATIONS[grp], TM, BF16)
            elif k < N_DZ_CHUNKS:
                pz_ref[:, (k - 9) * CHUNK:(k - 8) * CHUNK] = zc.astype(BF16)
            else:
                gate_ref[:, (k - N_DZ_CHUNKS) * CHUNK:(k - N_DZ_CHUNKS + 1) * CHUNK] = zc.astype(BF16)

        def all_chunks(own_shard):
            for k in range(N_CHUNKS):
                if k // CHUNKS_PER_SHARD == own_shard:
                    emit(k, zown_ref[k % CHUNKS_PER_SHARD].astype(F32))
                else:
                    emit(k, _dot(u, _w_in_chunk(w_ref, k)))

        for shard in range(N_CHIPS):
            pl.when(chip == shard)(functools.partial(all_chunks, shard))
        end()

    row = lambda w: pl.BlockSpec((TM, w), lambda i: (i, 0))
    res = pl.pallas_call(
        body, grid=(n_tiles,), name="inproj_rest",
        in_specs=[row(D_MODEL), pl.BlockSpec((CHUNKS_PER_SHARD, TM, CHUNK), lambda i: (0, i, 0)),
                  _resident(w_in.shape)] + [ANY] * (2 * n),
        out_specs=[pl.BlockSpec((3, d, TM // d, GROUP_W), lambda i: (0, 0, i, 0)) for d in DILATIONS]
        + [row(POOL_W), row(2 * D_MODEL)] + [ANY] * n,
        out_shape=[SDS((3, d, S // d, GROUP_W), BF16) for d in DILATIONS]
        + [SDS((S, POOL_W), BF16), SDS((S, 2 * D_MODEL), BF16)] + [SDS(b.shape, b.dtype) for b in bufs],
        scratch_shapes=[pltpu.VMEM((2, TM, PAIR_W), F32)] + [pltpu.SemaphoreType.DMA((3 * n,))] * 4,
        input_output_aliases={3 + n + w: 5 + w for w in range(n)},
        compiler_params=_cp("arbitrary"),
    )(u, z_own, w_in, *shards, *bufs)
    return res[:5], res[5:]


def _band_bias(grp, d):
    row = lax.broadcasted_iota(jnp.int32, (BAND, 2 * BAND), 0)
    col = lax.broadcasted_iota(jnp.int32, (BAND, 2 * BAND), 1)
    steps = BAND + row - col
    valid = (steps >= 0) & (steps <= BAND)
    stepsf = (steps * d).astype(F32)
    biases = []
    for hh in range(4):
        slope = 2.0 ** (-ALIBI_MAX_BIAS * (grp * 4 + hh + 1) / N_HEADS)
        biases.append(jnp.where(valid, -slope * stepsf, NEG))
    return biases, col


def _attn_tiles(grp, L):
    rr, rb = ATT_TILE[grp]
    rb = min(rb, L // BAND)
    return rr, rb, L // (rb * BAND)


def _kv_tile(cur_ref, prev_ref, rr, rb, cs):
    if rb == 0:
        return jnp.concatenate([prev_ref[rr, :, cs], cur_ref[rr, 0:BAND, cs]], axis=0)
    return cur_ref[rr, (rb - 1) * BAND:(rb + 1) * BAND, cs]


def _attn_fwd(qkv, grp):
    d = DILATIONS[grp]
    L = qkv.shape[2]
    RR, RB, nb = _attn_tiles(grp, L)

    def body(q_ref, kc_ref, kp_ref, vc_ref, vp_ref, o_ref, lse_ref):
        i = pl.program_id(0)
        biases, col = _band_bias(grp, d)
        first_keys_ok = (col >= BAND) | (i > 0)
        is_a = lax.broadcasted_iota(jnp.int32, (BAND, PAIR_W), 1) < HEAD_W
        heads = [(rr, rb, cp, h2) for rr in range(RR) for rb in range(RB) for cp in range(2) for h2 in range(2)]

        def tile(head):
            rr, rb, cp, _ = head
            return rr, rb, slice(rb * BAND, (rb + 1) * BAND), slice(cp * PAIR_W, (cp + 1) * PAIR_W)

        def scores(head):
            rr, rb, rows, cs = tile(head)
            q2 = q_ref[rr, rows, cs]
            b = biases[head[2] * 2 + head[3]]
            if rb == 0:
                b = jnp.where(first_keys_ok, b, NEG)
            sel = is_a if head[3] == 0 else jnp.logical_not(is_a)
            return _dot_nt(jnp.where(sel, q2, jnp.zeros_like(q2)), _kv_tile(kc_ref, kp_ref, rr, rb, cs)) + b

        s_next = scores(heads[0])
        outs, lses = {}, {}
        for idx, head in enumerate(heads):
            s = s_next
            if idx + 1 < len(heads):
                s_next = scores(heads[idx + 1])
            rr, rb, rows, cs = tile(head)
            m = jnp.max(s, axis=-1, keepdims=True)
            p = jnp.exp(s - m)
            l = jnp.sum(p, axis=-1, keepdims=True)
            outs[head[3]] = _dot(p.astype(BF16), _kv_tile(vc_ref, vp_ref, rr, rb, cs)) * (1.0 / l)
            lses[head[2] * 2 + head[3]] = m + jnp.log(l)
            if head[3] == 1:
                o_ref[rr, rows, cs] = jnp.where(is_a, outs[0], outs[1]).astype(BF16)
            if head[2] == 1 and head[3] == 1:
                lse_ref[rr, rows, :] = _per_head_lanes(lses)

    cur = lambda w: pl.BlockSpec((None, RR, RB * BAND, GROUP_W), lambda i, j: (w, j, i, 0))
    prev = lambda w: pl.BlockSpec((None, RR, BAND, GROUP_W), lambda i, j: (w, j, jnp.maximum(i * RB - 1, 0), 0))
    return pl.pallas_call(
        body, grid=(nb, d // RR), name=f"attn_fwd_g{grp}",
        in_specs=[cur(0), cur(1), prev(1), cur(2), prev(2)],
        out_specs=[pl.BlockSpec((RR, RB * BAND, GROUP_W), lambda i, j: (j, i, 0)),
                   pl.BlockSpec((RR, RB * BAND, STAT_W), lambda i, j: (j, i, 0))],
        out_shape=[SDS((d, L, GROUP_W), BF16), SDS((d, L, STAT_W), F32)],
        compiler_params=_cp("parallel", "parallel"),
    )(qkv, qkv, qkv, qkv, qkv)


def _pool_column_select(col, vals):
    return jnp.where(col < POOL_GROUP_W, vals[0],
                     jnp.where(col < 2 * POOL_GROUP_W, vals[1],
                               jnp.where(col < 3 * POOL_GROUP_W, vals[2], vals[3])))


def _pool_inv_count(i, rows):
    t = i * rows + lax.broadcasted_iota(jnp.int32, (rows, POOL_W), 0)
    col = lax.broadcasted_iota(jnp.int32, (rows, POOL_W), 1)
    win = _pool_column_select(col, POOL_WINDOWS)
    return 1.0 / jnp.minimum(t + 1, win).astype(F32), col


def _mixer_out(outs, lses, pz, gates, x, w_ao, w_po, wbd, scale, w_out, g_mlp, expand, shards, bufs):
    S = x.shape[0]
    n_tiles = S // TMB
    n = len(shards)

    def body(*refs):
        (o0_ref, l0_ref, o1_ref, l1_ref, o2_ref, l2_ref, pz_ref, halo_ref, gate_ref, x_ref,
         wao_ref, wpo_ref, wbd_ref, sc_ref, wout_ref, g_ref, expand_ref) = refs[0:17]
        shard_refs = refs[17:17 + n]
        (a_ref, lt0_ref, lt1_ref, lt2_ref, pooled_ref, mixed_ref, p_ref, merged_ref, h1_ref,
         m_ref) = refs[17 + 2 * n:27 + 2 * n]
        buf_refs = refs[27 + 2 * n:27 + 3 * n]
        so1, sl1, so2, sl2, slt, ext_ref = refs[27 + 3 * n:33 + 3 * n]
        i = pl.program_id(0)
        begin, end = _hosted_allgather(i, n_tiles, shard_refs, buf_refs, [sh.shape[0] for sh in shards],
                                       refs[33 + 3 * n:])
        begin()
        _interleave_load(o1_ref, (), so1, DILATIONS[1], TMB)
        _interleave_load(o2_ref, (), so2, DILATIONS[2], TMB)
        for ref, sref, d in ((l1_ref, sl1, DILATIONS[1]), (l2_ref, sl2, DILATIONS[2])):
            for r in range(d):
                sref[0, pl.ds(r, TMB // d, stride=d), :] = ref[r]
        l0, l1, l2 = l0_ref[0], sl1[0], sl2[0]
        mx = jnp.maximum(jnp.maximum(l0, l1), l2)
        e0, e1, e2 = jnp.exp(l0 - mx), jnp.exp(l1 - mx), jnp.exp(l2 - mx)
        den = e0 + e1 + e2
        inv = 1.0 / den
        slt[0] = mx + jnp.log(den)
        w0, w1, w2 = [_dot_split(e * inv, expand_ref[...]) for e in (e0, e1, e2)]
        for h in range(2):
            hs = slice(h * PAIR_W, (h + 1) * PAIR_W)
            a_ref[:, hs] = (w0[:, hs] * o0_ref[0, :, hs].astype(F32) + w1[:, hs] * so1[h]
                            + w2[:, hs] * so2[h]).astype(BF16)
        lt0_ref[0] = slt[0]
        for ref, d in ((lt1_ref, DILATIONS[1]), (lt2_ref, DILATIONS[2])):
            for r in range(d):
                ref[r] = slt[0, pl.ds(r, TMB // d, stride=d), :]

        pz_t = pz_ref[...].astype(F32)
        ext_ref[0:POOL_HALO, :] = jnp.where(i > 0, halo_ref[...].astype(F32), 0.0)
        ext_ref[POOL_HALO:, :] = pz_t
        sums = []
        acc = ext_ref[...]
        for k in (1, 2, 4, 8):
            acc = acc + pltpu.roll(acc, k, 0)
            sums.append(acc[POOL_HALO:, :])
        inv_cnt, col = _pool_inv_count(i, TMB)
        pooled = (_pool_column_select(col, sums) * inv_cnt - pz_t).astype(BF16)
        pooled_ref[...] = pooled
        mixed = _dot(pooled, wbd_ref[...])
        mixed_ref[...] = mixed.astype(BF16)
        p = (mixed * sc_ref[...]).astype(BF16)
        p_ref[...] = p

        a = a_ref[...]
        for j in range(N_CHIPS):
            js = slice(j * CHUNK, (j + 1) * CHUNK)
            ga = gate_ref[:, js]
            gp = gate_ref[:, D_MODEL + j * CHUNK:D_MODEL + (j + 1) * CHUNK]
            mj = _sigmoid(ga) * _dot(a, wao_ref[j]) + _sigmoid(gp) * _dot(p, wpo_ref[j])
            merged_ref[:, js] = mj.astype(BF16)
        h1 = x_ref[...] + _dot(merged_ref[...], wout_ref[...])
        h1_ref[...] = h1
        m_ref[...] = _rms_fwd(h1, g_ref[...])[0].astype(BF16)
        end()

    row = lambda w: pl.BlockSpec((TMB, w), lambda i: (i, 0))
    grp_spec = lambda d: pl.BlockSpec((d, TMB // d, GROUP_W), lambda i: (0, i, 0))
    stat_spec = lambda d: pl.BlockSpec((d, TMB // d, STAT_W), lambda i: (0, i, 0))
    halo = pl.BlockSpec((POOL_HALO, POOL_W), lambda i: (jnp.maximum(i * (TMB // POOL_HALO) - 1, 0), 0))
    d0, d1, d2 = DILATIONS
    pair_scratch = pltpu.VMEM((2, TMB, PAIR_W), F32)
    stat_scratch = pltpu.VMEM((1, TMB, STAT_W), F32)
    res = pl.pallas_call(
        body, grid=(n_tiles,), name="mixer_out",
        in_specs=[grp_spec(d0), stat_spec(d0), grp_spec(d1), stat_spec(d1), grp_spec(d2), stat_spec(d2),
                  row(POOL_W), halo, row(2 * D_MODEL), row(D_MODEL),
                  _resident(w_ao.shape), _resident(w_po.shape), _resident(wbd.shape), _resident(scale.shape),
                  _resident(w_out.shape), _resident(g_mlp.shape), _resident(expand.shape)] + [ANY] * (2 * n),
        out_specs=[row(GROUP_W), stat_spec(d0), stat_spec(d1), stat_spec(d2),
                   row(POOL_W), row(POOL_W), row(POOL_W), row(D_MODEL), row(D_MODEL), row(D_MODEL)] + [ANY] * n,
        out_shape=[SDS((S, GROUP_W), BF16)] + [SDS((d, S // d, STAT_W), F32) for d in DILATIONS]
        + [SDS((S, POOL_W), BF16), SDS((S, POOL_W), BF16), SDS((S, POOL_W), BF16),
           SDS((S, D_MODEL), BF16), SDS((S, D_MODEL), F32), SDS((S, D_MODEL), BF16)]
        + [SDS(b.shape, b.dtype) for b in bufs],
        scratch_shapes=[pair_scratch, stat_scratch, pair_scratch, stat_scratch, stat_scratch,
                        pltpu.VMEM((TMB + POOL_HALO, POOL_W), F32)] + [pltpu.SemaphoreType.DMA((3 * n,))] * 4,
        input_output_aliases={17 + n + w: 10 + w for w in range(n)},
        compiler_params=_cp("arbitrary"),
    )(outs[0], lses[0], outs[1], lses[1], outs[2], lses[2], pz, pz, gates, x,
      w_ao, w_po, wbd, scale, w_out, g_mlp, expand, *shards, *bufs)
    return res[:10], res[10:]


def _mlp_fwd_loss(m, h1, target, w_mi, w_mo, g_f):
    S = m.shape[0]

    def body(m_ref, h1_ref, t_ref, wmi_ref, wmo_ref, g_ref, hid_ref, dh2_ref, dh2b_ref, loss_ref, dg_ref):
        @pl.when(pl.program_id(0) == 0)
        def _():
            loss_ref[...] = jnp.zeros_like(loss_ref)
            dg_ref[...] = jnp.zeros_like(dg_ref)

        acc = None
        for c in range(N_CHIPS):
            hid = jnp.square(jnp.maximum(_dot(m_ref[...], wmi_ref[c]), 0.0)).astype(BF16)
            hid_ref[:, c * D_MODEL:(c + 1) * D_MODEL] = hid
            part = _dot(hid, wmo_ref[c])
            acc = part if acc is None else acc + part
        dh2_ref[...] = acc + h1_ref[...]
        g = g_ref[...]
        loss_sum = jnp.zeros((1, D_MODEL), F32)
        dg_sum = jnp.zeros((1, D_MODEL), F32)
        for rows in (slice(k * BAND, (k + 1) * BAND) for k in range(TM // BAND)):
            y, hh, r = _rms_fwd(dh2_ref[rows, :], g)
            e = y - t_ref[rows, :]
            loss_sum = loss_sum + jnp.sum(e * e, axis=0, keepdims=True)
            dy = e * (1.0 / D_MODEL)
            dg_sum = dg_sum + jnp.sum(dy * hh, axis=0, keepdims=True)
            dh2 = _rms_bwd(dy, hh, r, g)
            dh2_ref[rows, :] = dh2
            dh2b_ref[rows, :] = dh2.astype(BF16)
        loss_ref[...] += loss_sum
        dg_ref[...] += dg_sum

    row = lambda w: pl.BlockSpec((TM, w), lambda i: (i, 0))
    vec = pl.BlockSpec((1, D_MODEL), lambda i: (0, 0))
    return pl.pallas_call(
        body, grid=(S // TM,), name="mlp_fwd_loss",
        in_specs=[row(D_MODEL), row(D_MODEL), row(D_MODEL), _resident(w_mi.shape), _resident(w_mo.shape),
                  _resident(g_f.shape)],
        out_specs=[row(D_FF), row(D_MODEL), row(D_MODEL), vec, vec],
        out_shape=[SDS((S, D_FF), BF16), SDS((S, D_MODEL), F32), SDS((S, D_MODEL), BF16),
                   SDS((1, D_MODEL), F32), SDS((1, D_MODEL), F32)],
        compiler_params=_cp("arbitrary"),
    )(m, h1, target, w_mi, w_mo, g_f)


def _mlp_bwd(dh2, dh2b, hid, h1, w_mi, w_mo, g_mlp):
    S = dh2.shape[0]

    def body(dh2_ref, dh2b_ref, hid_ref, h1_ref, wmi_ref, wmo_ref, g_ref, dpre_ref, dh1_ref, dh1b_ref, dg_ref):
        @pl.when(pl.program_id(0) == 0)
        def _():
            dg_ref[...] = jnp.zeros_like(dg_ref)

        d2 = dh2b_ref[...]
        dm = jnp.zeros((TM, D_MODEL), F32)
        dhid_next = _dot_nt(d2, wmo_ref[0])
        for c in range(N_CHIPS):
            cs = slice(c * D_MODEL, (c + 1) * D_MODEL)
            dhid = dhid_next
            if c + 1 < N_CHIPS:
                dhid_next = _dot_nt(d2, wmo_ref[c + 1])
            dpre = (dhid * (2.0 * jnp.sqrt(hid_ref[:, cs].astype(F32)))).astype(BF16)
            dpre_ref[:, cs] = dpre
            dm = dm + _dot_nt(dpre, wmi_ref[c])
        g = g_ref[...]
        _, hh, r = _rms_fwd(h1_ref[...], g)
        dg_ref[...] += jnp.sum(dm * hh, axis=0, keepdims=True)
        dh1 = dh2_ref[...] + _rms_bwd(dm, hh, r, g)
        dh1_ref[...] = dh1
        dh1b_ref[...] = dh1.astype(BF16)

    row = lambda w: pl.BlockSpec((TM, w), lambda i: (i, 0))
    return pl.pallas_call(
        body, grid=(S // TM,), name="mlp_bwd",
        in_specs=[row(D_MODEL), row(D_MODEL), row(D_FF), row(D_MODEL), _resident(w_mi.shape),
                  _resident(w_mo.shape), _resident(g_mlp.shape)],
        out_specs=[row(D_FF), row(D_MODEL), row(D_MODEL), pl.BlockSpec((1, D_MODEL), lambda i: (0, 0))],
        out_shape=[SDS((S, D_FF), BF16), SDS((S, D_MODEL), F32), SDS((S, D_MODEL), BF16), SDS((1, D_MODEL), F32)],
        compiler_params=_cp("arbitrary"),
    )(dh2, dh2b, hid, h1, w_mi, w_mo, g_mlp)


def _mixer_bwd(dh1b, a, p, mixed, gates, w_out, w_ao, w_po, wbd, scale, stat_reduce, sums):
    S = a.shape[0]
    n_tiles = S // TMB
    n = len(sums)

    def body(*refs):
        (dh1b_ref, a_ref, p_ref, mixed_ref, gate_ref, wout_ref, wao_ref, wpo_ref, wbd_ref, sc_ref,
         ones_ref) = refs[0:11]
        sum_refs = refs[11:11 + n]
        (da1_ref, dp1_ref, dgate_ref, da0_ref, dag1_ref, dag2_ref, dd0_ref, dd1_ref, dd2_ref,
         dmixed_ref, dqp_ref, dscale_ref) = refs[11 + n:23 + n]
        land_refs = refs[23 + n:23 + 2 * n]
        s_da, s_dd, send_sem, recv_sem = refs[23 + 2 * n:]
        i = pl.program_id(0)

        @pl.when(i == 0)
        def _():
            dscale_ref[...] = jnp.zeros_like(dscale_ref)
            for cpy in _chip_sum_copies(sum_refs, land_refs, send_sem, recv_sem):
                cpy.start()

        dmerged = _dot_nt(dh1b_ref[...], wout_ref[...])
        a = a_ref[...]
        p = p_ref[...]
        da = jnp.zeros((TMB, GROUP_W), F32)
        dp = jnp.zeros((TMB, POOL_W), F32)
        for j in range(N_CHIPS):
            js = slice(j * CHUNK, (j + 1) * CHUNK)
            sa = _sigmoid(gate_ref[:, js])
            sp = _sigmoid(gate_ref[:, D_MODEL + j * CHUNK:D_MODEL + (j + 1) * CHUNK])
            dmj = dmerged[:, js]
            da1 = (dmj * sa).astype(BF16)
            dp1 = (dmj * sp).astype(BF16)
            da1_ref[:, js] = da1
            dp1_ref[:, js] = dp1
            dgate_ref[j] = (dmj * _dot(a, wao_ref[j]) * sa * (1.0 - sa)).astype(BF16)
            dgate_ref[N_CHIPS + j] = (dmj * _dot(p, wpo_ref[j]) * sp * (1.0 - sp)).astype(BF16)
            da = da + _dot_nt(da1, wao_ref[j])
            dp = dp + _dot_nt(dp1, wpo_ref[j])

        dd = _dot_split(da * a.astype(F32), ones_ref[...])
        da0_ref[0] = da.astype(BF16)
        dd0_ref[0] = dd
        for h in range(2):
            s_da[h] = da[:, h * PAIR_W:(h + 1) * PAIR_W]
        s_dd[0] = dd
        for refs, d in (((dag1_ref, dd1_ref), DILATIONS[1]), ((dag2_ref, dd2_ref), DILATIONS[2])):
            for r in range(d):
                for h in range(2):
                    hs = slice(h * PAIR_W, (h + 1) * PAIR_W)
                    refs[0][r, :, hs] = s_da[h, pl.ds(r, TMB // d, stride=d), :].astype(BF16)
                refs[1][r] = s_dd[0, pl.ds(r, TMB // d, stride=d), :]

        sc = sc_ref[...]
        dscale_ref[...] += jnp.sum(dp * mixed_ref[...].astype(F32), axis=0, keepdims=True)
        dmixed = (dp * sc).astype(BF16)
        dmixed_ref[...] = dmixed
        inv_cnt, _ = _pool_inv_count(i, TMB)
        dqp_ref[...] = (_dot_nt(dmixed, wbd_ref[...]) * inv_cnt).astype(BF16)

        @pl.when(i == n_tiles - 1)
        def _():
            for cpy in _chip_sum_copies(sum_refs, land_refs, send_sem, recv_sem):
                cpy.wait()

    row = lambda w: pl.BlockSpec((TMB, w), lambda i: (i, 0))
    grp_spec = lambda d: pl.BlockSpec((d, TMB // d, GROUP_W), lambda i: (0, i, 0))
    stat_spec = lambda d: pl.BlockSpec((d, TMB // d, STAT_W), lambda i: (0, i, 0))
    d0, d1, d2 = DILATIONS
    res = pl.pallas_call(
        body, grid=(n_tiles,), name="mixer_bwd",
        in_specs=[row(D_MODEL), row(GROUP_W), row(POOL_W), row(POOL_W), row(2 * D_MODEL),
                  _resident(w_out.shape), _resident(w_ao.shape), _resident(w_po.shape), _resident(wbd.shape),
                  _resident(scale.shape), _resident(stat_reduce.shape)] + [ANY] * n,
        out_specs=[row(D_MODEL), row(D_MODEL), pl.BlockSpec((2 * N_CHIPS, TMB, CHUNK), lambda i: (0, i, 0)),
                   grp_spec(d0), grp_spec(d1), grp_spec(d2), stat_spec(d0), stat_spec(d1), stat_spec(d2),
                   row(POOL_W), row(POOL_W), pl.BlockSpec((1, POOL_W), lambda i: (0, 0))] + [ANY] * n,
        out_shape=[SDS((S, D_MODEL), BF16), SDS((S, D_MODEL), BF16), SDS((2 * N_CHIPS, S, CHUNK), BF16)]
        + [SDS((d, S // d, GROUP_W), BF16) for d in DILATIONS]
        + [SDS((d, S // d, STAT_W), F32) for d in DILATIONS]
        + [SDS((S, POOL_W), BF16), SDS((S, POOL_W), BF16), SDS((1, POOL_W), F32)]
        + [SDS(t.shape, t.dtype) for t in sums],
        scratch_shapes=[pltpu.VMEM((2, TMB, PAIR_W), F32), pltpu.VMEM((1, TMB, STAT_W), F32),
                        pltpu.SemaphoreType.DMA((3 * n,)), pltpu.SemaphoreType.DMA((3 * n,))],
        compiler_params=_cp("arbitrary"),
    )(dh1b, a, p, mixed, gates, w_out, w_ao, w_po, wbd, scale, stat_reduce, *sums)
    return res[:12], res[12:]


def _attn_bwd(qkv, da, lt, dd, grp, packs=None):
    d = DILATIONS[grp]
    L = qkv.shape[2]
    RR, RB, nb = _attn_tiles(grp, L)
    n_j = d // RR
    hosted = packs is not None

    def body(*refs):
        q_ref, kc_ref, kp_ref, vc_ref, vp_ref, da_ref, lt_ref, dd_ref = refs[0:8]
        dq_ref, dk_ref, dv_ref = refs[8 + hosted:11 + hosted]
        dk_acc, dv_acc = refs[11 + 2 * hosted:13 + 2 * hosted]
        i = pl.program_id(1)
        if hosted:
            j = pl.program_id(0)
            start, relay, finish = _pack_allgather(refs[11 + hosted], *refs[13 + 2 * hosted:])
            pl.when((j == 0) & (i == 0))(start)
            pl.when((j == 0) & (i == nb // 2))(relay)

        @pl.when(i == 0)
        def _():
            dk_acc[...] = jnp.zeros_like(dk_acc)
            dv_acc[...] = jnp.zeros_like(dv_acc)

        def compute(cur, prv):
            dk_acc[cur] = jnp.zeros((RR, RB * BAND, GROUP_W), F32)
            dv_acc[cur] = jnp.zeros((RR, RB * BAND, GROUP_W), F32)
            biases, col = _band_bias(grp, d)
            first_keys_ok = (col >= BAND) | (i > 0)
            is_a = lax.broadcasted_iota(jnp.int32, (BAND, PAIR_W), 1) < HEAD_W
            for rr in range(RR):
                for rb in range(RB):
                    rows = slice(rb * BAND, (rb + 1) * BAND)
                    for cp in range(2):
                        cs = slice(cp * PAIR_W, (cp + 1) * PAIR_W)
                        q2 = q_ref[rr, rows, cs]
                        da2 = da_ref[rr, rows, cs]
                        lt2 = lt_ref[rr, rows, :]
                        dd2 = dd_ref[rr, rows, :]
                        kcat = _kv_tile(kc_ref, kp_ref, rr, rb, cs)
                        vcat = _kv_tile(vc_ref, vp_ref, rr, rb, cs)
                        q2t = q2.T
                        da2t = da2.T
                        dqs, dkts, dvts, scores, dpvs = [], [], [], [], []
                        for h2 in range(2):
                            sel = is_a if h2 == 0 else jnp.logical_not(is_a)
                            b = biases[cp * 2 + h2]
                            if rb == 0:
                                b = jnp.where(first_keys_ok, b, NEG)
                            scores.append(_dot_nt(jnp.where(sel, q2, jnp.zeros_like(q2)), kcat) + b)
                            dpvs.append(_dot_nt(jnp.where(sel, da2, jnp.zeros_like(da2)), vcat))
                        for h2 in range(2):
                            lane0 = h2 * HEAD_W
                            p = jnp.exp(scores[h2] - _head_col(lt2, cp * 2 + h2))
                            ds = (p * (dpvs[h2] - _head_col(dd2, cp * 2 + h2))).astype(BF16)
                            dqs.append(_dot(ds, kcat))
                            dkts.append(_dot(q2t[lane0:lane0 + HEAD_W, :], ds))
                            dvts.append(_dot(da2t[lane0:lane0 + HEAD_W, :], p.astype(BF16)))
                        dq_ref[rr, rows, cs] = (jnp.where(is_a, dqs[0], dqs[1]) * 0.125).astype(BF16)
                        dkc = jnp.concatenate(dkts, axis=0).T
                        dvc = jnp.concatenate(dvts, axis=0).T
                        if rb == 0:
                            last = slice((RB - 1) * BAND, RB * BAND)
                            dk_acc[prv, rr, last, cs] += dkc[0:BAND]
                            dv_acc[prv, rr, last, cs] += dvc[0:BAND]
                            dk_acc[cur, rr, 0:BAND, cs] += dkc[BAND:]
                            dv_acc[cur, rr, 0:BAND, cs] += dvc[BAND:]
                        else:
                            both = slice((rb - 1) * BAND, (rb + 1) * BAND)
                            dk_acc[cur, rr, both, cs] += dkc
                            dv_acc[cur, rr, both, cs] += dvc

        def flush(prv):
            dk_ref[...] = dk_acc[prv].astype(BF16)
            dv_ref[...] = dv_acc[prv].astype(BF16)

        for parity in (0, 1):
            on = (i % 2) == parity
            pl.when(on & (i < nb))(functools.partial(compute, parity, 1 - parity))
            pl.when(on & (i > 0))(functools.partial(flush, 1 - parity))
        if hosted:
            pl.when((j == n_j - 1) & (i == nb))(finish)

    qi = lambda i: jnp.minimum(i, nb - 1)
    cur_w = lambda w: pl.BlockSpec((None, RR, RB * BAND, GROUP_W), lambda j, i: (w, j, qi(i), 0))
    prev_w = lambda w: pl.BlockSpec((None, RR, BAND, GROUP_W),
                                    lambda j, i: (w, j, jnp.maximum(qi(i) * RB - 1, 0), 0))
    blk = pl.BlockSpec((RR, RB * BAND, GROUP_W), lambda j, i: (j, qi(i), 0))
    stat_blk = pl.BlockSpec((RR, RB * BAND, STAT_W), lambda j, i: (j, qi(i), 0))
    late = pl.BlockSpec((RR, RB * BAND, GROUP_W), lambda j, i: (j, jnp.maximum(i - 1, 0), 0))
    extra = [packs] if hosted else []
    return pl.pallas_call(
        body, grid=(n_j, nb + 1), name=f"attn_bwd_g{grp}",
        in_specs=[cur_w(0), cur_w(1), prev_w(1), cur_w(2), prev_w(2), blk, stat_blk, stat_blk] + [ANY] * hosted,
        out_specs=[blk, late, late] + [ANY] * hosted,
        out_shape=[SDS((d, L, GROUP_W), BF16)] * 3 + [SDS(t.shape, t.dtype) for t in extra],
        scratch_shapes=[pltpu.VMEM((2, RR, RB * BAND, GROUP_W), F32), pltpu.VMEM((2, RR, RB * BAND, GROUP_W), F32)]
        + [pltpu.SemaphoreType.DMA((N_DEV - 1,))] * (2 * hosted),
        input_output_aliases={8: 3} if hosted else {},
        compiler_params=_cp("arbitrary" if hosted else "parallel", "arbitrary"),
    )(qkv, qkv, qkv, qkv, qkv, da, lt, dd, *extra)


def _dz_assemble(dqkv, dqp):
    S = dqp.shape[0]
    tm = min(TMZ, S)
    n_tiles = S // tm

    def body(*refs):
        dqkv_refs = refs[0:9]
        dqp_ref, halo_ref = refs[9:11]
        dz_ref, s_ref, ext_ref = refs[11:]
        i = pl.program_id(0)

        for grp in range(3):
            for which in range(3):
                n = which * 3 + grp
                ref = dqkv_refs[grp * 3 + which]
                if DILATIONS[grp] == 1:
                    dz_ref[n] = ref[0]
                else:
                    _interleave_load(ref, (), s_ref, DILATIONS[grp], tm)
                    for h in range(2):
                        dz_ref[n, :, h * PAIR_W:(h + 1) * PAIR_W] = s_ref[h].astype(BF16)

        dqp = dqp_ref[...].astype(F32)
        ext_ref[0:tm, :] = dqp
        ext_ref[tm:, :] = jnp.where(i < n_tiles - 1, halo_ref[...].astype(F32), 0.0)
        sums = []
        acc = ext_ref[...]
        for k in (1, 2, 4, 8):
            acc = acc + pltpu.roll(acc, tm + POOL_HALO - k, 0)
            sums.append(acc[0:tm, :])
        inv_cnt, col = _pool_inv_count(i, tm)
        dpz = _pool_column_select(col, sums) - dqp / inv_cnt
        for t in range(3):
            dz_ref[9 + t] = dpz[:, t * CHUNK:(t + 1) * CHUNK].astype(BF16)

    row = lambda w: pl.BlockSpec((tm, w), lambda i: (i, 0))
    grp_spec = lambda d: pl.BlockSpec((d, tm // d, GROUP_W), lambda i: (0, i, 0))
    halo = pl.BlockSpec((POOL_HALO, POOL_W),
                        lambda i: (jnp.minimum((i + 1) * (tm // POOL_HALO), S // POOL_HALO - 1), 0))
    flat = [t for grp in range(3) for t in dqkv[grp]]
    return pl.pallas_call(
        body, grid=(n_tiles,), name="dz_assemble",
        in_specs=[grp_spec(DILATIONS[grp]) for grp in range(3) for _ in range(3)] + [row(POOL_W), halo],
        out_specs=pl.BlockSpec((N_DZ_CHUNKS, tm, CHUNK), lambda i: (0, i, 0)),
        out_shape=SDS((N_DZ_CHUNKS, S, CHUNK), BF16),
        scratch_shapes=[pltpu.VMEM((2, tm, PAIR_W), F32), pltpu.VMEM((tm + POOL_HALO, POOL_W), F32)],
        compiler_params=_cp("parallel"),
    )(*flat, dqp, dqp)


def _inproj_dx(dz, dgates, dh1, x, g, w_in, sums):
    S = x.shape[0]
    n_tiles = S // TM
    n = len(sums)

    def body(*refs):
        dz_ref, dgate_ref, dh1_ref, x_ref, g_ref, w_ref = refs[0:6]
        sum_refs = refs[6:6 + n]
        dx_ref, dg_ref = refs[6 + n:8 + n]
        land_refs = refs[8 + n:8 + 2 * n]
        sems = refs[8 + 2 * n:]
        i = pl.program_id(0)

        def copies():
            return _chip_sum_copies(sum_refs, land_refs, *sems)

        @pl.when(i == 0)
        def _():
            dg_ref[...] = jnp.zeros_like(dg_ref)
            for cpy in copies():
                cpy.start()

        du = jnp.zeros((TM, D_MODEL), F32)
        for k in range(N_CHUNKS):
            dzk = dz_ref[k] if k < N_DZ_CHUNKS else dgate_ref[k - N_DZ_CHUNKS]
            du = du + _dot_nt(dzk, _w_in_chunk(w_ref, k))
        gv = g_ref[...]
        _, xh, r = _rms_fwd(x_ref[...], gv)
        dg_ref[...] += jnp.sum(du * xh, axis=0, keepdims=True)
        dx_ref[...] = dh1_ref[...] + _rms_bwd(du, xh, r, gv)

        @pl.when(i == n_tiles - 1)
        def _():
            for cpy in copies():
                cpy.wait()

    row = lambda w: pl.BlockSpec((TM, w), lambda i: (i, 0))
    res = pl.pallas_call(
        body, grid=(n_tiles,), name="inproj_dx",
        in_specs=[pl.BlockSpec((N_DZ_CHUNKS, TM, CHUNK), lambda i: (0, i, 0)),
                  pl.BlockSpec((N_CHUNKS - N_DZ_CHUNKS, TM, CHUNK), lambda i: (0, i, 0)),
                  row(D_MODEL), row(D_MODEL), _resident(g.shape), _resident(w_in.shape)] + [ANY] * n,
        out_specs=[row(D_MODEL), pl.BlockSpec((1, D_MODEL), lambda i: (0, 0))] + [ANY] * n,
        out_shape=[SDS((S, D_MODEL), F32), SDS((1, D_MODEL), F32)] + [SDS(t.shape, t.dtype) for t in sums],
        scratch_shapes=[pltpu.SemaphoreType.DMA((3 * n,)), pltpu.SemaphoreType.DMA((3 * n,))],
        compiler_params=_cp("arbitrary"),
    )(dz, dgates, dh1, x, g, w_in, *sums)
    return res[0], res[1], res[2:]


def _wgrad(a, b, name, *, out_shape, a_spec, b_spec, out_spec, grid, n_out_cols=None, fill=None, narrow=True):
    k_axis = len(grid) - 1
    n_k = grid[k_axis]
    n_out = 2 if narrow else 1

    def body(a_ref, b_ref, *rest):
        o_ref = rest[-n_out]

        @pl.when(pl.program_id(k_axis) == 0)
        def _():
            o_ref[...] = jnp.zeros_like(o_ref)

        at = a_ref[...]
        if n_out_cols is None:
            o_ref[...] += _dot_tn(at, b_ref[...])
        elif n_out_cols[0] == "lead_both":
            for t in range(b_ref.shape[0]):
                o_ref[t] += _dot_tn(at, b_ref[t])
        else:
            w = n_out_cols[1]
            for t in range(o_ref.shape[0]):
                o_ref[t] += _dot_tn(at, b_ref[:, t * w:(t + 1) * w])

        if narrow:
            @pl.when(pl.program_id(k_axis) == n_k - 1)
            def _():
                rest[-1][...] = o_ref[...].astype(BF16)

    sem = ("parallel",) * k_axis + ("arbitrary",)
    extra = [] if fill is None else list(fill) if narrow else [fill]
    shapes = [out_shape, SDS(out_shape.shape, BF16)] if narrow else out_shape
    return pl.pallas_call(body, grid=grid, name=name, in_specs=[a_spec, b_spec] + [ANY] * len(extra),
                          out_specs=[out_spec] * n_out if narrow else out_spec, out_shape=shapes,
                          input_output_aliases={2 + t: t for t in range(len(extra))},
                          compiler_params=_cp(*sem))(a, b, *extra)


def _wgrad_in(u, dz, dgates):
    bk = min(BK, u.shape[0])
    nk = u.shape[0] // bk
    g = WGRAD_IN_GROUP
    kw = dict(n_out_cols=("lead_both", CHUNK), a_spec=pl.BlockSpec((bk, D_MODEL), lambda j, k: (k, 0)),
              b_spec=pl.BlockSpec((g, bk, CHUNK), lambda j, k: (j, k, 0)),
              out_shape=SDS((N_CHUNKS, D_MODEL, CHUNK), F32))
    first = _wgrad(u, dz, "wgrad_in_qkvp", grid=(N_DZ_CHUNKS // g, nk),
                   out_spec=pl.BlockSpec((g, D_MODEL, CHUNK), lambda j, k: (j, 0, 0)), **kw)
    both = _wgrad(u, dgates, "wgrad_in_gates", grid=((N_CHUNKS - N_DZ_CHUNKS) // g, nk), fill=first,
                  out_spec=pl.BlockSpec((g, D_MODEL, CHUNK), lambda j, k: (N_DZ_CHUNKS // g + j, 0, 0)), **kw)
    return [t.reshape(N_CHIPS, CHUNKS_PER_SHARD * D_MODEL, CHUNK) for t in both]


def _wgrads_mixer(a, da1, p, dp1, merged, dh1b, pooled, dmixed):
    bk = min(BK, a.shape[0])
    nk = a.shape[0] // bk
    g_ao = _wgrad(
        a, da1, "wgrad_att_out", grid=(nk,), n_out_cols=("cols_b", CHUNK),
        a_spec=pl.BlockSpec((bk, GROUP_W), lambda k: (k, 0)),
        b_spec=pl.BlockSpec((bk, D_MODEL), lambda k: (k, 0)),
        out_spec=pl.BlockSpec((N_CHIPS, GROUP_W, CHUNK), lambda k: (0, 0, 0)),
        out_shape=SDS((N_CHIPS, GROUP_W, CHUNK), F32))
    g_po = _wgrad(
        p, dp1, "wgrad_pool_out", grid=(nk,), n_out_cols=("cols_b", CHUNK),
        a_spec=pl.BlockSpec((bk, POOL_W), lambda k: (k, 0)),
        b_spec=pl.BlockSpec((bk, D_MODEL), lambda k: (k, 0)),
        out_spec=pl.BlockSpec((N_CHIPS, POOL_W, CHUNK), lambda k: (0, 0, 0)),
        out_shape=SDS((N_CHIPS, POOL_W, CHUNK), F32))
    g_out = _wgrad(
        merged, dh1b, "wgrad_out", grid=(nk,),
        a_spec=pl.BlockSpec((bk, D_MODEL), lambda k: (k, 0)),
        b_spec=pl.BlockSpec((bk, D_MODEL), lambda k: (k, 0)),
        out_spec=pl.BlockSpec((D_MODEL, D_MODEL), lambda k: (0, 0)),
        out_shape=SDS((D_MODEL, D_MODEL), F32))
    g_bd = _wgrad(
        pooled, dmixed, "wgrad_pool_grp", grid=(nk,),
        a_spec=pl.BlockSpec((bk, POOL_W), lambda k: (k, 0)),
        b_spec=pl.BlockSpec((bk, POOL_W), lambda k: (k, 0)),
        out_spec=pl.BlockSpec((POOL_W, POOL_W), lambda k: (0, 0)),
        out_shape=SDS((POOL_W, POOL_W), F32), narrow=False)
    g_out = [t.reshape(N_CHIPS, D_MODEL // N_CHIPS, D_MODEL) for t in g_out]
    return [g_ao, g_po, g_out], g_bd


def _wgrads_mlp(m, dpre, hid, dh2b):
    bk = min(BK, m.shape[0])
    nk = m.shape[0] // bk
    g_mi = _wgrad(
        m, dpre, "wgrad_mlp_in", grid=(N_CHIPS, nk),
        a_spec=pl.BlockSpec((bk, D_MODEL), lambda c, k: (k, 0)),
        b_spec=pl.BlockSpec((bk, D_MODEL), lambda c, k: (k, c)),
        out_spec=pl.BlockSpec((None, D_MODEL, D_MODEL), lambda c, k: (c, 0, 0)),
        out_shape=SDS((N_CHIPS, D_MODEL, D_MODEL), F32))
    g_mo = _wgrad(
        hid, dh2b, "wgrad_mlp_out", grid=(N_CHIPS, nk),
        a_spec=pl.BlockSpec((bk, D_MODEL), lambda c, k: (k, c)),
        b_spec=pl.BlockSpec((bk, D_MODEL), lambda c, k: (k, 0)),
        out_spec=pl.BlockSpec((None, D_MODEL, D_MODEL), lambda c, k: (c, 0, 0)),
        out_shape=SDS((N_CHIPS, D_MODEL, D_MODEL), F32))
    return [g_mi, g_mo]


def _mesh_place():
    x, y, c = lax.axis_index("x"), lax.axis_index("y"), lax.axis_index("c")
    other_chips = [(x, 1 - y), (1 - x, y), (1 - x, 1 - y)]
    return x, y, c, other_chips


ANY = pl.BlockSpec(memory_space=pl.ANY)


def _weight_half_copies(shard_refs, buf_refs, rows, send_sem, recv_sem):
    x, y, c, chips = _mesh_place()
    me = 2 * x + y
    copies = []
    for w, r_full in enumerate(rows):
        rh = r_full // 2
        for r, (px, py) in enumerate(chips):
            k = w * 3 + r
            copies.append(pltpu.make_async_remote_copy(
                src_ref=shard_refs[w].at[pl.ds(c * rh, rh), :], dst_ref=buf_refs[w].at[me, pl.ds(c * rh, rh), :],
                send_sem=send_sem.at[k], recv_sem=recv_sem.at[k], device_id=(px, py, c), device_id_type=MESH))
    return copies


def _pair_forward_copies(buf_refs, rows, send_sem, recv_sem):
    x, y, c, chips = _mesh_place()
    out = []
    for w, r_full in enumerate(rows):
        rh = r_full // 2
        for r, (px, py) in enumerate(chips):
            k = w * 3 + r
            landed = buf_refs[w].at[2 * px + py, pl.ds(c * rh, rh), :]
            theirs = buf_refs[w].at[2 * px + py, pl.ds((1 - c) * rh, rh), :]
            mk = lambda ref: pltpu.make_async_remote_copy(
                src_ref=ref, dst_ref=ref, send_sem=send_sem.at[k], recv_sem=recv_sem.at[k],
                device_id=(x, y, 1 - c), device_id_type=MESH)
            out.append((mk(landed), mk(theirs)))
    return out


def _place_own(block, n_slots, slot):
    buf = lax.empty((n_slots,) + block.shape, block.dtype)
    return lax.dynamic_update_slice(buf, block[None], (slot,) + (0,) * block.ndim)


def _pair_forward(bufs, rows, name):
    n = len(bufs)

    def body(*refs):
        dst = refs[n:2 * n]
        send_sem, recv_sem = refs[2 * n:]
        fwds = _pair_forward_copies(dst, rows, send_sem, recv_sem)
        for fwd, _ in fwds:
            fwd.start()
        for fwd, landing in fwds:
            landing.wait_recv()
            fwd.wait_send()

    return pl.pallas_call(
        body, name=name,
        in_specs=[ANY] * n, out_specs=[ANY] * n,
        out_shape=[SDS(b.shape, b.dtype) for b in bufs],
        scratch_shapes=[pltpu.SemaphoreType.DMA((3 * n,))] * 2,
        input_output_aliases={w: w for w in range(n)},
    )(*bufs)


def _chip_sum_copies(src, dst, send_sem, recv_sem):
    x, y, c, chips = _mesh_place()
    copies = []
    for w in range(len(src)):
        for r, (px, py) in enumerate(chips):
            k = w * 3 + r
            copies.append(pltpu.make_async_remote_copy(
                src_ref=src[w].at[r + 1], dst_ref=dst[w].at[r + 1], send_sem=send_sem.at[k], recv_sem=recv_sem.at[k],
                device_id=(px, py, c), device_id_type=MESH))
    return copies


def _pair_exchange(grads):
    n = len(grads)

    def body(*refs):
        src, dst = refs[:n], refs[n:2 * n]
        send_sem, recv_sem = refs[2 * n:]
        x, y, c, _ = _mesh_place()
        copies = []
        for w in range(n):
            rh = grads[w].shape[1] // 2
            copies.append(pltpu.make_async_remote_copy(
                src_ref=src[w].at[:, pl.ds((1 - c) * rh, rh), :], dst_ref=dst[w],
                send_sem=send_sem.at[w], recv_sem=recv_sem.at[w],
                device_id=(x, y, 1 - c), device_id_type=MESH))
            copies[-1].start()
        for cpy in copies:
            cpy.wait()

    return pl.pallas_call(
        body, name="grad_pair_exchange",
        in_specs=[ANY] * n, out_specs=[ANY] * n,
        out_shape=[SDS((N_CHIPS, g.shape[1] // 2, g.shape[2]), g.dtype) for g in grads],
        scratch_shapes=[pltpu.SemaphoreType.DMA((n,)), pltpu.SemaphoreType.DMA((n,))],
    )(*grads)


def _pair_sum(place, grad, recv, name):
    _, R, C = grad.shape
    rh = R // 2
    br = _row_block(rh, max(256, ELEMENTWISE_BLOCK // C))
    nbh = rh // br

    def body(place_ref, g_ref, r_ref, own_ref, sums_ref):
        s = g_ref[...] + r_ref[...].astype(F32)

        @pl.when(pl.program_id(1) == 0)
        def _():
            own_ref[...] = s

        sums_ref[...] = s.astype(BF16)

    slot = lambda rel, pr: jnp.bitwise_xor(pr[0], rel)
    return pl.pallas_call(
        body, name=name,
        grid_spec=pltpu.PrefetchScalarGridSpec(
            num_scalar_prefetch=1, grid=(nbh, N_CHIPS),
            in_specs=[pl.BlockSpec((None, br, C), lambda i, rel, pr: (slot(rel, pr), pr[1] * nbh + i, 0)),
                      pl.BlockSpec((None, br, C), lambda i, rel, pr: (slot(rel, pr), i, 0))],
            out_specs=[pl.BlockSpec((br, C), lambda i, rel, pr: (i, 0)),
                       pl.BlockSpec((None, br, C), lambda i, rel, pr: (rel, i, 0))]),
        out_shape=[SDS((rh, C), F32), SDS((N_CHIPS, rh, C), BF16)],
        compiler_params=_cp("parallel", "arbitrary"),
    )(place, grad, recv)


def _chip_sum(place, own, recv, name):
    rh, C = own.shape
    br = _row_block(rh, max(256, ELEMENTWISE_BLOCK // C))
    nbh = rh // br

    def body(place_ref, own_ref, r_ref, o_ref):
        o_ref[...] = ((own_ref[...] + r_ref[1].astype(F32)) + r_ref[2].astype(F32)) + r_ref[3].astype(F32)

    return pl.pallas_call(
        body, name=name,
        grid_spec=pltpu.PrefetchScalarGridSpec(
            num_scalar_prefetch=1, grid=(nbh,),
            in_specs=[pl.BlockSpec((br, C), lambda i, pr: (i, 0)),
                      pl.BlockSpec((N_CHIPS, br, C), lambda i, pr: (0, i, 0))],
            out_specs=pl.BlockSpec((br, C), lambda i, pr: (pr[1] * nbh + i, 0))),
        out_shape=SDS((2 * rh, C), F32),
        compiler_params=_cp("parallel"),
    )(place, own, recv)


def _pack_allgather(all_ref, send_sem, recv_sem):
    x, y, c, chips = _mesh_place()
    sib = (x, y, 1 - c)

    def pack(dev, k, to):
        slot = 4 * dev[0] + 2 * dev[1] + dev[2]
        return pltpu.make_async_remote_copy(
            src_ref=all_ref.at[slot], dst_ref=all_ref.at[slot], send_sem=send_sem.at[k],
            recv_sem=recv_sem.at[k], device_id=to, device_id_type=MESH)

    first = [pack((x, y, c), 0, sib)] + [pack((x, y, c), 1 + r, (px, py, c)) for r, (px, py) in enumerate(chips)]
    relays = [pack((px, py, c), 4 + r, sib) for r, (px, py) in enumerate(chips)]

    def start():
        for cpy in first:
            cpy.start()

    def relay():
        for r, (px, py) in enumerate(chips):
            pack((px, py, c), 1 + r, (px, py, c)).wait_recv()
            relays[r].start()

    def finish():
        pack(sib, 0, sib).wait_recv()
        for r, (px, py) in enumerate(chips):
            pack((px, py, 1 - c), 4 + r, sib).wait_recv()
        for cpy in first + relays:
            cpy.wait_send()

    return start, relay, finish


def _finish_exchange(grads, late_all):
    n = len(grads)

    def body(*refs):
        dst, all_ref = refs[n + 1:2 * n + 1], refs[2 * n + 1]
        send_sem, recv_sem, ssend_sem, srecv_sem = refs[2 * n + 2:]
        x, y, c, _ = _mesh_place()
        start, relay, finish = _pack_allgather(all_ref, ssend_sem, srecv_sem)
        start()
        sends, landings = [], []
        for w in range(n):
            rh = grads[w].shape[0] // 2
            mk = lambda cc: pltpu.make_async_remote_copy(
                src_ref=dst[w].at[pl.ds(cc * rh, rh), :], dst_ref=dst[w].at[pl.ds(cc * rh, rh), :],
                send_sem=send_sem.at[w], recv_sem=recv_sem.at[w], device_id=(x, y, 1 - c), device_id_type=MESH)
            sends.append(mk(c))
            landings.append(mk(1 - c))
            sends[-1].start()
        relay()
        finish()
        for cpy in landings:
            cpy.wait_recv()
        for cpy in sends:
            cpy.wait_send()

    res = pl.pallas_call(
        body, name="grad_finish_exchange",
        in_specs=[ANY] * (n + 1), out_specs=[ANY] * (n + 1),
        out_shape=[SDS(g.shape, g.dtype) for g in grads] + [SDS(late_all.shape, late_all.dtype)],
        scratch_shapes=[pltpu.SemaphoreType.DMA((n,)), pltpu.SemaphoreType.DMA((n,)),
                        pltpu.SemaphoreType.DMA((N_DEV - 1,)), pltpu.SemaphoreType.DMA((N_DEV - 1,))],
        input_output_aliases={w: w for w in range(n + 1)},
    )(*grads, late_all)
    return res[:n], res[n]


def _adamw_math(w, g, m, v):
    m = ADAM_B1 * m + (1.0 - ADAM_B1) * g
    v = ADAM_B2 * v + (1.0 - ADAM_B2) * jnp.square(g)
    m_hat = m / (1.0 - ADAM_B1 ** ADAM_STEP)
    v_hat = v / (1.0 - ADAM_B2 ** ADAM_STEP)
    delta = -ADAM_LR * (m_hat / (jnp.sqrt(v_hat) + ADAM_EPS) + ADAM_WD * w)
    return delta, m, v


def _adamw(w, g, m, v, name):
    R, C = w.shape
    br = _row_block(R, 512)
    if g.ndim == 3:
        n_chunks, cw = g.shape[0], g.shape[2]
        g_spec = pl.BlockSpec((None, br, cw), lambda t, i: (t, i, 0))
    else:
        n_chunks, cw = 1, C
        g_spec = pl.BlockSpec((br, cw), lambda t, i: (i, t))

    def body(w_ref, g_ref, m_ref, v_ref, g_out_ref, d_ref, nm_ref, nv_ref):
        gv = g_ref[...]
        g_out_ref[...] = gv
        d_ref[...], nm_ref[...], nv_ref[...] = _adamw_math(w_ref[...], gv, m_ref[...], v_ref[...])

    spec = pl.BlockSpec((br, cw), lambda t, i: (i, t))
    return pl.pallas_call(
        body, grid=(n_chunks, R // br), name=name, in_specs=[spec, g_spec, spec, spec], out_specs=[spec] * 4,
        out_shape=[SDS((R, C), F32)] * 4, compiler_params=_cp("parallel", "parallel"),
    )(w, g, m, v)


def _small_sum_adamw(all_small, all_late, w, m, v):
    loss_row = PACK_ROWS - 8

    def body(all_ref, late_ref, w_ref, m_ref, v_ref, g_ref, d_ref, nm_ref, nv_ref, loss_ref):
        g = all_ref[0]
        late = late_ref[0]
        for k in range(1, N_DEV):
            g = g + all_ref[k]
            late = late + late_ref[k]
        g_ref[...] = g
        g_ref[PACK_LATE_ROW:PACK_LATE_ROW + 8, :] = late
        g = g_ref[...]
        d_ref[...], nm_ref[...], nv_ref[...] = _adamw_math(w_ref[...], g, m_ref[...], v_ref[...])
        total = jnp.sum(g[loss_row:loss_row + 1, :]) * (0.5 / D_MODEL)
        loss_ref[...] = jnp.full(loss_ref.shape, total, F32)

    full = lambda s: pl.BlockSpec(s, lambda i: (0,) * len(s))
    pack = (PACK_ROWS, D_MODEL)
    return pl.pallas_call(
        body, grid=(1,), name="small_sum_adamw",
        in_specs=[full((N_DEV,) + pack), full((N_DEV, 8, D_MODEL)), full(pack), full(pack), full(pack)],
        out_specs=[full(pack)] * 4 + [full((8, 128))],
        out_shape=[SDS(pack, F32)] * 4 + [SDS((8, 128), F32)],
        compiler_params=_cp("arbitrary"),
    )(all_small, all_late, w, m, v)


def _pack_small(grp, scale, g_mix, g_mlp, g_f, loss_lanes):
    def part(vec):
        vec = vec.reshape(1, -1)
        return jnp.pad(vec, ((0, 7), (0, D_MODEL - vec.shape[1])))
    return jnp.concatenate([grp.reshape(-1, D_MODEL), part(scale), part(g_mix), part(g_mlp), part(g_f),
                            part(loss_lanes)], axis=0)


def _unpack_small(pack):
    n_grp = len(POOL_WINDOWS) * POOL_GROUP_W * POOL_GROUP_W // D_MODEL
    grp = pack[:n_grp].reshape(1, len(POOL_WINDOWS), POOL_GROUP_W, POOL_GROUP_W)
    scale = pack[n_grp, :POOL_W].reshape(1, POOL_W)
    g_mix = pack[n_grp + 8].reshape(1, D_MODEL)
    g_mlp = pack[n_grp + 16].reshape(1, D_MODEL)
    g_f = pack[n_grp + 24].reshape(D_MODEL)
    return grp, scale, g_mix, g_mlp, g_f


def _block_diag(grp):
    out = jnp.zeros((POOL_W, POOL_W), grp.dtype)
    for k in range(len(POOL_WINDOWS)):
        out = lax.dynamic_update_slice(out, grp[k], (k * POOL_GROUP_W, k * POOL_GROUP_W))
    return out


def kernel(x, norm_mix_g, w_in, w_att_out, w_pool_grp, pool_scale, w_pool_out, w_out, norm_mlp_g, w_mlp_in, w_mlp_out, norm_final_g, loss_target, m_norm_mix_g, m_w_in, m_w_att_out, m_w_pool_grp, m_pool_scale, m_w_pool_out, m_w_out, m_norm_mlp_g, m_w_mlp_in, m_w_mlp_out, m_norm_final_g, v_norm_mix_g, v_w_in, v_w_att_out, v_w_pool_grp, v_pool_scale, v_w_pool_out, v_w_out, v_norm_mlp_g, v_w_mlp_in, v_w_mlp_out, v_norm_final_g):
    S = x.shape[1]
    xs, target = x[0], loss_target[0]
    big = [w_in[0], w_att_out[0], w_pool_out[0], w_out[0], w_mlp_in[0], w_mlp_out[0]]
    big_m = [m_w_in[0], m_w_att_out[0], m_w_pool_out[0], m_w_out[0], m_w_mlp_in[0], m_w_mlp_out[0]]
    big_v = [v_w_in[0], v_w_att_out[0], v_w_pool_out[0], v_w_out[0], v_w_mlp_in[0], v_w_mlp_out[0]]

    chip = 2 * lax.axis_index("x") + lax.axis_index("y")
    core = lax.axis_index("c")
    place = jnp.stack([chip, core]).astype(jnp.int32)
    names = ("w_in", "w_att_out", "w_pool_out", "w_out", "w_mlp_in", "w_mlp_out")

    shards = [w.astype(BF16) for w in big]
    bufs = [_place_own(sh, N_CHIPS, chip) for sh in shards]
    wbd = _block_diag(w_pool_grp[0]).astype(BF16)
    g_final = norm_final_g.reshape(1, D_MODEL)
    stat_expand, stat_reduce = _stat_matrices()

    u, z_own, landed_in = _norm_inproj_own(xs, norm_mix_g, shards[0], bufs[0])
    (wg_in,) = _pair_forward([landed_in], [shards[0].shape[0]], "w_in_pair_forward")
    (qkv0, qkv1, qkv2, pz, gates), (wg_ao, wg_po, wg_out) = _inproj_rest(u, z_own, wg_in, shards[1:4], bufs[1:4])
    wg_out = wg_out.reshape(D_MODEL, D_MODEL)
    qkv = (qkv0, qkv1, qkv2)
    att = [_attn_fwd(qkv[grp], grp) for grp in range(3)]
    (a, lt0, lt1, lt2, pooled, mixed, p, merged, h1, m), (wg_mi, wg_mo) = _mixer_out(
        [o for o, _ in att], [l for _, l in att], pz, gates, xs, wg_ao, wg_po, wbd, pool_scale, wg_out, norm_mlp_g,
        stat_expand, shards[4:], bufs[4:])
    hid, dh2, dh2b, loss_lanes, dg_final = _mlp_fwd_loss(m, h1, target, wg_mi, wg_mo, g_final)

    def pair_reduce(grads, grad_names):
        recv = _pair_exchange([narrow for _, narrow in grads])
        pair = [_pair_sum(place, g, r, f"pair_sum_{nm}") for (g, _), r, nm in zip(grads, recv, grad_names)]
        return [own for own, _ in pair], [s for _, s in pair]

    def chip_reduce(owns, landed_sums, grad_names):
        return [_chip_sum(place, own, r, f"chip_sum_{nm}") for own, r, nm in zip(owns, landed_sums, grad_names)]

    dpre, dh1, dh1b, dg_mlp = _mlp_bwd(dh2, dh2b, hid, h1, wg_mi, wg_mo, norm_mlp_g)
    own_mlp, sums_mlp = pair_reduce(_wgrads_mlp(m, dpre, hid, dh2b), names[4:])
    (da1, dp1, dgates, da0, dag1, dag2, dd0, dd1, dd2, dmixed, dqp, dscale), landed_mlp = _mixer_bwd(
        dh1b, a, p, mixed, gates, wg_out, wg_ao, wg_po, wbd, pool_scale, stat_reduce, sums_mlp)
    g_mi, g_mo = chip_reduce(own_mlp, landed_mlp, names[4:])
    grads_mixer, g_bd = _wgrads_mixer(a, da1, p, dp1, merged, dh1b, pooled, dmixed)

    zero = jnp.zeros((D_MODEL,), F32)
    g_grp = jnp.stack([g_bd[k * POOL_GROUP_W:(k + 1) * POOL_GROUP_W, k * POOL_GROUP_W:(k + 1) * POOL_GROUP_W]
                       for k in range(len(POOL_WINDOWS))])
    small = _pack_small(g_grp, dscale, zero, dg_mlp, dg_final, loss_lanes)
    *dqkv0, small_all = _attn_bwd(qkv[0], da0, lt0, dd0, 0, packs=_place_own(small, N_DEV, 2 * chip + core))
    dqkv = [dqkv0, _attn_bwd(qkv[1], dag1, lt1, dd1, 1), _attn_bwd(qkv[2], dag2, lt2, dd2, 2)]
    dz = _dz_assemble(dqkv, dqp)
    own_in, sums_in = pair_reduce([_wgrad_in(u, dz, dgates)] + grads_mixer, names[:4])
    dx, dg_mix, landed_in = _inproj_dx(dz, dgates, dh1, xs, norm_mix_g, wg_in, sums_in)
    g_in, g_ao, g_po, g_out = chip_reduce(own_in, landed_in, names[:4])
    late = jnp.pad(dg_mix, ((0, 7), (0, 0)))
    full, late_all = _finish_exchange([g_in, g_ao, g_po, g_out, g_mi, g_mo], _place_own(late, N_DEV, 2 * chip + core))

    small_w = _pack_small(w_pool_grp[0], pool_scale, norm_mix_g, norm_mlp_g, norm_final_g, zero)
    small_m = _pack_small(m_w_pool_grp[0], m_pool_scale, m_norm_mix_g, m_norm_mlp_g, m_norm_final_g, zero)
    small_v = _pack_small(v_w_pool_grp[0], v_pool_scale, v_norm_mix_g, v_norm_mlp_g, v_norm_final_g, zero)
    sg, sd, sm, sv, loss_tile = _small_sum_adamw(small_all, late_all, small_w, small_m, small_v)
    full = [full[0].reshape(CHUNKS_PER_SHARD, D_MODEL, CHUNK)] + list(full[1:])
    upd = [_adamw(w, g, mm, vv, f"adamw_{nm}") for w, g, mm, vv, nm in zip(big, full, big_m, big_v, names)]

    def ordered(small_pack, bigs):
        grp, scale, g_mix, g_mlp, g_f = _unpack_small(small_pack)
        b_in, b_ao, b_po, b_out, b_mi, b_mo = [b[None] for b in bigs]
        return (g_mix, b_in, b_ao, grp, scale, b_po, b_out, g_mlp, b_mi, b_mo, g_f)

    return (loss_tile[0, 0], dx[None],
            *ordered(sg, [t[0] for t in upd]),
            *ordered(sd, [t[1] for t in upd]),
            *ordered(sm, [t[2] for t in upd]),
            *ordered(sv, [t[3] for t in upd]))
```
